```python
import jax, jax.numpy as jnp
from jax import lax
import numpy as np

D_MODEL = 1024
BATCH = 16
SEQ = 2048
DEPTH = 1

D_MIX = D_MODEL
D_RWKV = D_MIX // 2
D_CONV = D_MIX - D_RWKV
HEAD_SIZE = 64
N_RWKV_HEADS = D_RWKV // HEAD_SIZE
CONV_GROUPS = 8
CONV_WIDTH = 3
D_DECAY_LORA = 64
D_AAA_LORA = 64
D_GATE_LORA = 160
D_FF = -(-(8 * D_MODEL) // (3 * 256)) * 256
D_SHIFTED = 3 * D_RWKV + D_DECAY_LORA + D_AAA_LORA + D_GATE_LORA
D_IN = D_SHIFTED + 3 * D_CONV
LOG_DECAY_SCALE = 0.606531
RMS_EPS = 1e-6
GN_EPS = 64e-5
NORM_EPS = 1e-12

kernel_name = "hybrid_rwkv7_shortconv_encoder_block"


def _rmsnorm(x, w):
    x32 = x.astype(jnp.float32)
    y = x32 * lax.rsqrt(jnp.mean(x32 * x32, axis=-1, keepdims=True) + RMS_EPS)
    return (y * w.astype(jnp.float32)).astype(x.dtype)


def _token_shift(p, mu):
    zeros = jnp.zeros_like(p[:, :1])
    prev = jnp.concatenate([zeros, p[:, :-1]], axis=1)
    nxt = jnp.concatenate([p[:, 1:], zeros], axis=1)
    return p + mu * (0.5 * (prev + nxt) - p)


def _to_heads(t):
    b, s, _ = t.shape
    return t.reshape(b, s, N_RWKV_HEADS, HEAD_SIZE)


def _wkv7_scan(r, w, k, v, kk, a, reverse):
    b = r.shape[0]
    xs = tuple(jnp.swapaxes(t, 0, 1) for t in (r, w, k, v, kk, a))

    def step(state, inp):
        r_t, w_t, k_t, v_t, kk_t, a_t = inp
        sa = jnp.einsum('bhvk,bhk->bhv', state, kk_t)
        state = (state * w_t[:, :, None, :]
                 - sa[..., None] * (kk_t * a_t)[:, :, None, :]
                 + v_t[..., None] * k_t[:, :, None, :])
        y_t = jnp.einsum('bhvk,bhk->bhv', state, r_t)
        return state, y_t

    s0 = jnp.zeros((b, N_RWKV_HEADS, HEAD_SIZE, HEAD_SIZE), jnp.float32)
    _, ys = lax.scan(step, s0, xs, reverse=reverse)
    return jnp.swapaxes(ys, 0, 1)


def _rwkv7_direction(r, k, v, kk, xw, xa, w_up, w0, a_up, a0, k_a, r_k, reverse):
    logit_w = w0 + jnp.tanh(xw) @ w_up.astype(jnp.float32)
    w = jnp.exp(-LOG_DECAY_SCALE * jax.nn.sigmoid(logit_w))
    a = jax.nn.sigmoid(a0 + xa @ a_up.astype(jnp.float32))
    kd = k * (1.0 + (a - 1.0) * k_a)
    rh, kh, vh = _to_heads(r), _to_heads(kd), _to_heads(v)
    y = _wkv7_scan(rh, _to_heads(w), kh, vh, kk, _to_heads(a), reverse)
    bonus = jnp.sum(rh * kh * r_k, axis=-1, keepdims=True) * vh
    return y, bonus


def _rwkv7_mixer(r, k, v, xw, xa, xg, w_up_f, w0_f, w_up_b, w0_b, a_up_f, a0_f,
                 a_up_b, a0_b, g_up, k_k, k_a_f, k_a_b, r_k_f, r_k_b, gn_w, gn_b):
    out_dtype = r.dtype
    r, k, v, xw, xa, xg = (t.astype(jnp.float32) for t in (r, k, v, xw, xa, xg))
    kk = _to_heads(k * k_k)
    kk = kk / jnp.maximum(jnp.linalg.norm(kk, axis=-1, keepdims=True), NORM_EPS)
    y_f, bonus_f = _rwkv7_direction(r, k, v, kk, xw, xa, w_up_f, w0_f, a_up_f, a0_f,
                                    k_a_f, r_k_f, reverse=False)
    y_b, bonus_b = _rwkv7_direction(r, k, v, kk, xw, xa, w_up_b, w0_b, a_up_b, a0_b,
                                    k_a_b, r_k_b, reverse=True)
    y = y_f + y_b
    mean = jnp.mean(y, axis=-1, keepdims=True)
    var = jnp.mean(jnp.square(y - mean), axis=-1, keepdims=True)
    y = (y - mean) * lax.rsqrt(var + GN_EPS)
    y = y * gn_w.reshape(N_RWKV_HEADS, HEAD_SIZE) + gn_b.reshape(N_RWKV_HEADS, HEAD_SIZE)
    y = y + bonus_f + bonus_b
    g = jax.nn.sigmoid(xg) @ g_up.astype(jnp.float32)
    b, s = r.shape[:2]
    return (y.reshape(b, s, D_RWKV) * g).astype(out_dtype)


def _short_conv(pc, conv_w):
    gate_b, gate_c, h = jnp.split(pc, [D_CONV, 2 * D_CONV], axis=-1)
    u = gate_c * h
    u = lax.conv_general_dilated(
        u, conv_w.reshape(CONV_WIDTH, 1, D_CONV).astype(u.dtype),
        window_strides=(1,), padding=[((CONV_WIDTH - 1) // 2, (CONV_WIDTH - 1) // 2)],
        dimension_numbers=('NWC', 'WIO', 'NWC'), feature_group_count=D_CONV)
    return gate_b * u


def _swiglu(h, w_gate, w_up, w_down):
    return (jax.nn.silu(h @ w_gate) * (h @ w_up)) @ w_down


def _fwd_setup_inputs(seed: int = 0) -> dict:
    key = jax.random.key(seed)
    ks = jax.random.split(key, 32)

    def nrm(k, shape, scale):
        return jax.random.normal(k, shape, jnp.float32) * scale

    L = DEPTH
    return {
        "x": nrm(ks[0], (BATCH, SEQ, D_MODEL), 1.0),
        "norm1_w": 1.0 + nrm(ks[1], (L, D_MODEL), 0.02),
        "w_in": nrm(ks[2], (L, D_MODEL, D_IN), D_MODEL ** -0.5),
        "mu_shift": jax.random.uniform(ks[3], (L, D_SHIFTED), jnp.float32),
        "w_up_f": nrm(ks[4], (L, D_DECAY_LORA, D_RWKV), D_DECAY_LORA ** -0.5),
        "w0_f": nrm(ks[5], (L, D_RWKV), 1.0),
        "w_up_b": nrm(ks[6], (L, D_DECAY_LORA, D_RWKV), D_DECAY_LORA ** -0.5),
        "w0_b": nrm(ks[7], (L, D_RWKV), 1.0),
        "a_up_f": nrm(ks[8], (L, D_AAA_LORA, D_RWKV), D_AAA_LORA ** -0.5),
        "a0_f": nrm(ks[9], (L, D_RWKV), 0.5),
        "a_up_b": nrm(ks[10], (L, D_AAA_LORA, D_RWKV), D_AAA_LORA ** -0.5),
        "a0_b": nrm(ks[11], (L, D_RWKV), 0.5),
        "g_up": nrm(ks[12], (L, D_GATE_LORA, D_RWKV), D_GATE_LORA ** -0.5),
        "k_k": 0.85 + nrm(ks[13], (L, D_RWKV), 0.05),
        "k_a_f": 1.0 + nrm(ks[14], (L, D_RWKV), 0.05),
        "k_a_b": 1.0 + nrm(ks[15], (L, D_RWKV), 0.05),
        "r_k_f": nrm(ks[16], (L, N_RWKV_HEADS, HEAD_SIZE), 0.1),
        "r_k_b": nrm(ks[17], (L, N_RWKV_HEADS, HEAD_SIZE), 0.1),
        "gn_w": 1.0 + nrm(ks[18], (L, D_RWKV), 0.02),
        "gn_b": nrm(ks[19], (L, D_RWKV), 0.02),
        "conv_w": nrm(ks[20], (L, CONV_WIDTH, D_CONV), CONV_WIDTH ** -0.5),
        "w_out": nrm(ks[21], (L, D_MIX, D_MODEL), D_MIX ** -0.5),
        "norm2_w": 1.0 + nrm(ks[22], (L, D_MODEL), 0.02),
        "w_gate": nrm(ks[23], (L, D_MODEL, D_FF), D_MODEL ** -0.5),
        "w_up": nrm(ks[24], (L, D_MODEL, D_FF), D_MODEL ** -0.5),
        "w_down": nrm(ks[25], (L, D_FF, D_MODEL), D_FF ** -0.5),
        "norm_f_w": 1.0 + nrm(ks[26], (D_MODEL,), 0.02),
    }


def _fwd_reference(x, norm1_w, w_in, mu_shift, w_up_f, w0_f, w_up_b, w0_b, a_up_f, a0_f,
              a_up_b, a0_b, g_up, k_k, k_a_f, k_a_b, r_k_f, r_k_b, gn_w, gn_b, conv_w,
              w_out, norm2_w, w_gate, w_up, w_down, norm_f_w):
    split_pts = [D_RWKV, 2 * D_RWKV, 3 * D_RWKV, 3 * D_RWKV + D_DECAY_LORA,
                 3 * D_RWKV + D_DECAY_LORA + D_AAA_LORA]
    for l in range(DEPTH):
        h = _rmsnorm(x, norm1_w[l])
        p = h @ w_in[l]
        ps, pc = p[..., :D_SHIFTED], p[..., D_SHIFTED:]
        ps = _token_shift(ps, mu_shift[l])
        r, k, v, xw, xa, xg = jnp.split(ps, split_pts, axis=-1)
        o_rwkv = _rwkv7_mixer(r, k, v, xw, xa, xg, w_up_f[l], w0_f[l], w_up_b[l], w0_b[l],
                              a_up_f[l], a0_f[l], a_up_b[l], a0_b[l], g_up[l], k_k[l],
                              k_a_f[l], k_a_b[l], r_k_f[l], r_k_b[l], gn_w[l], gn_b[l])
        o_conv = _short_conv(pc, conv_w[l])
        x = x + jnp.concatenate([o_rwkv, o_conv], axis=-1) @ w_out[l]
        x = x + _swiglu(_rmsnorm(x, norm2_w[l]), w_gate[l], w_up[l], w_down[l])
    return _rmsnorm(x, norm_f_w)


import jax as _jax
import jax.numpy as _jnp

TWIN_FORMAT = 'train_step'
FWD_PARAMS = ['x', 'norm1_w', 'w_in', 'mu_shift', 'w_up_f', 'w0_f', 'w_up_b', 'w0_b', 'a_up_f', 'a0_f', 'a_up_b', 'a0_b', 'g_up', 'k_k', 'k_a_f', 'k_a_b', 'r_k_f', 'r_k_b', 'gn_w', 'gn_b', 'conv_w', 'w_out', 'norm2_w', 'w_gate', 'w_up', 'w_down', 'norm_f_w']
TWIN_WEIGHTS = ['norm1_w', 'w_in', 'mu_shift', 'w_up_f', 'w0_f', 'w_up_b', 'w0_b', 'a_up_f', 'a0_f', 'a_up_b', 'a0_b', 'g_up', 'k_k', 'k_a_f', 'k_a_b', 'r_k_f', 'r_k_b', 'gn_w', 'gn_b', 'conv_w', 'w_out', 'norm2_w', 'w_gate', 'w_up', 'w_down', 'norm_f_w']
TWIN_DIFF_INPUT = 'x'
TWIN_INPUTS = ['x', 'norm1_w', 'w_in', 'mu_shift', 'w_up_f', 'w0_f', 'w_up_b', 'w0_b', 'a_up_f', 'a0_f', 'a_up_b', 'a0_b', 'g_up', 'k_k', 'k_a_f', 'k_a_b', 'r_k_f', 'r_k_b', 'gn_w', 'gn_b', 'conv_w', 'w_out', 'norm2_w', 'w_gate', 'w_up', 'w_down', 'norm_f_w', 'loss_target', 'm_norm1_w', 'm_w_in', 'm_mu_shift', 'm_w_up_f', 'm_w0_f', 'm_w_up_b', 'm_w0_b', 'm_a_up_f', 'm_a0_f', 'm_a_up_b', 'm_a0_b', 'm_g_up', 'm_k_k', 'm_k_a_f', 'm_k_a_b', 'm_r_k_f', 'm_r_k_b', 'm_gn_w', 'm_gn_b', 'm_conv_w', 'm_w_out', 'm_norm2_w', 'm_w_gate', 'm_w_up', 'm_w_down', 'm_norm_f_w', 'v_norm1_w', 'v_w_in', 'v_mu_shift', 'v_w_up_f', 'v_w0_f', 'v_w_up_b', 'v_w0_b', 'v_a_up_f', 'v_a0_f', 'v_a_up_b', 'v_a0_b', 'v_g_up', 'v_k_k', 'v_k_a_f', 'v_k_a_b', 'v_r_k_f', 'v_r_k_b', 'v_gn_w', 'v_gn_b', 'v_conv_w', 'v_w_out', 'v_norm2_w', 'v_w_gate', 'v_w_up', 'v_w_down', 'v_norm_f_w']
TWIN_OUTPUTS = ['loss', 'grad_x', 'grad_norm1_w', 'grad_w_in', 'grad_mu_shift', 'grad_w_up_f', 'grad_w0_f', 'grad_w_up_b', 'grad_w0_b', 'grad_a_up_f', 'grad_a0_f', 'grad_a_up_b', 'grad_a0_b', 'grad_g_up', 'grad_k_k', 'grad_k_a_f', 'grad_k_a_b', 'grad_r_k_f', 'grad_r_k_b', 'grad_gn_w', 'grad_gn_b', 'grad_conv_w', 'grad_w_out', 'grad_norm2_w', 'grad_w_gate', 'grad_w_up', 'grad_w_down', 'grad_norm_f_w', 'delta_norm1_w', 'delta_w_in', 'delta_mu_shift', 'delta_w_up_f', 'delta_w0_f', 'delta_w_up_b', 'delta_w0_b', 'delta_a_up_f', 'delta_a0_f', 'delta_a_up_b', 'delta_a0_b', 'delta_g_up', 'delta_k_k', 'delta_k_a_f', 'delta_k_a_b', 'delta_r_k_f', 'delta_r_k_b', 'delta_gn_w', 'delta_gn_b', 'delta_conv_w', 'delta_w_out', 'delta_norm2_w', 'delta_w_gate', 'delta_w_up', 'delta_w_down', 'delta_norm_f_w', 'new_m_norm1_w', 'new_m_w_in', 'new_m_mu_shift', 'new_m_w_up_f', 'new_m_w0_f', 'new_m_w_up_b', 'new_m_w0_b', 'new_m_a_up_f', 'new_m_a0_f', 'new_m_a_up_b', 'new_m_a0_b', 'new_m_g_up', 'new_m_k_k', 'new_m_k_a_f', 'new_m_k_a_b', 'new_m_r_k_f', 'new_m_r_k_b', 'new_m_gn_w', 'new_m_gn_b', 'new_m_conv_w', 'new_m_w_out', 'new_m_norm2_w', 'new_m_w_gate', 'new_m_w_up', 'new_m_w_down', 'new_m_norm_f_w', 'new_v_norm1_w', 'new_v_w_in', 'new_v_mu_shift', 'new_v_w_up_f', 'new_v_w0_f', 'new_v_w_up_b', 'new_v_w0_b', 'new_v_a_up_f', 'new_v_a0_f', 'new_v_a_up_b', 'new_v_a0_b', 'new_v_g_up', 'new_v_k_k', 'new_v_k_a_f', 'new_v_k_a_b', 'new_v_r_k_f', 'new_v_r_k_b', 'new_v_gn_w', 'new_v_gn_b', 'new_v_conv_w', 'new_v_w_out', 'new_v_norm2_w', 'new_v_w_gate', 'new_v_w_up', 'new_v_w_down', 'new_v_norm_f_w']
TWIN_LEAF_KINDS = {'loss': 'loss', 'grad_x': 'grad_x', 'grad_norm1_w': 'grad_w', 'grad_w_in': 'grad_w', 'grad_mu_shift': 'grad_w', 'grad_w_up_f': 'grad_w', 'grad_w0_f': 'grad_w', 'grad_w_up_b': 'grad_w', 'grad_w0_b': 'grad_w', 'grad_a_up_f': 'grad_w', 'grad_a0_f': 'grad_w', 'grad_a_up_b': 'grad_w', 'grad_a0_b': 'grad_w', 'grad_g_up': 'grad_w', 'grad_k_k': 'grad_w', 'grad_k_a_f': 'grad_w', 'grad_k_a_b': 'grad_w', 'grad_r_k_f': 'grad_w', 'grad_r_k_b': 'grad_w', 'grad_gn_w': 'grad_w', 'grad_gn_b': 'grad_w', 'grad_conv_w': 'grad_w', 'grad_w_out': 'grad_w', 'grad_norm2_w': 'grad_w', 'grad_w_gate': 'grad_w', 'grad_w_up': 'grad_w', 'grad_w_down': 'grad_w', 'grad_norm_f_w': 'grad_w', 'delta_norm1_w': 'delta_w', 'delta_w_in': 'delta_w', 'delta_mu_shift': 'delta_w', 'delta_w_up_f': 'delta_w', 'delta_w0_f': 'delta_w', 'delta_w_up_b': 'delta_w', 'delta_w0_b': 'delta_w', 'delta_a_up_f': 'delta_w', 'delta_a0_f': 'delta_w', 'delta_a_up_b': 'delta_w', 'delta_a0_b': 'delta_w', 'delta_g_up': 'delta_w', 'delta_k_k': 'delta_w', 'delta_k_a_f': 'delta_w', 'delta_k_a_b': 'delta_w', 'delta_r_k_f': 'delta_w', 'delta_r_k_b': 'delta_w', 'delta_gn_w': 'delta_w', 'delta_gn_b': 'delta_w', 'delta_conv_w': 'delta_w', 'delta_w_out': 'delta_w', 'delta_norm2_w': 'delta_w', 'delta_w_gate': 'delta_w', 'delta_w_up': 'delta_w', 'delta_w_down': 'delta_w', 'delta_norm_f_w': 'delta_w', 'new_m_norm1_w': 'new_m', 'new_m_w_in': 'new_m', 'new_m_mu_shift': 'new_m', 'new_m_w_up_f': 'new_m', 'new_m_w0_f': 'new_m', 'new_m_w_up_b': 'new_m', 'new_m_w0_b': 'new_m', 'new_m_a_up_f': 'new_m', 'new_m_a0_f': 'new_m', 'new_m_a_up_b': 'new_m', 'new_m_a0_b': 'new_m', 'new_m_g_up': 'new_m', 'new_m_k_k': 'new_m', 'new_m_k_a_f': 'new_m', 'new_m_k_a_b': 'new_m', 'new_m_r_k_f': 'new_m', 'new_m_r_k_b': 'new_m', 'new_m_gn_w': 'new_m', 'new_m_gn_b': 'new_m', 'new_m_conv_w': 'new_m', 'new_m_w_out': 'new_m', 'new_m_norm2_w': 'new_m', 'new_m_w_gate': 'new_m', 'new_m_w_up': 'new_m', 'new_m_w_down': 'new_m', 'new_m_norm_f_w': 'new_m', 'new_v_norm1_w': 'new_v', 'new_v_w_in': 'new_v', 'new_v_mu_shift': 'new_v', 'new_v_w_up_f': 'new_v', 'new_v_w0_f': 'new_v', 'new_v_w_up_b': 'new_v', 'new_v_w0_b': 'new_v', 'new_v_a_up_f': 'new_v', 'new_v_a0_f': 'new_v', 'new_v_a_up_b': 'new_v', 'new_v_a0_b': 'new_v', 'new_v_g_up': 'new_v', 'new_v_k_k': 'new_v', 'new_v_k_a_f': 'new_v', 'new_v_k_a_b': 'new_v', 'new_v_r_k_f': 'new_v', 'new_v_r_k_b': 'new_v', 'new_v_gn_w': 'new_v', 'new_v_gn_b': 'new_v', 'new_v_conv_w': 'new_v', 'new_v_w_out': 'new_v', 'new_v_norm2_w': 'new_v', 'new_v_w_gate': 'new_v', 'new_v_w_up': 'new_v', 'new_v_w_down': 'new_v', 'new_v_norm_f_w': 'new_v'}


def _forward(args):
    return _fwd_reference(*[args[k] for k in FWD_PARAMS])


def _output_shape():
    out = _jax.eval_shape(lambda: _forward(_fwd_setup_inputs(0)))
    return out.shape, out.dtype

N_MICROBATCH = 1
ADAM_LR = 0.001
ADAM_B1 = 0.9
ADAM_B2 = 0.999
ADAM_EPS = 1e-08
ADAM_WD = 0.01
ADAM_STEP = 10
PER_EXAMPLE_BATCH_AXIS = {'x': 0, 'loss_target': 0}
SHARED_INPUTS = []
_WEIGHT_DTYPES = {'norm1_w': _jnp.float32, 'w_in': _jnp.float32, 'mu_shift': _jnp.float32, 'w_up_f': _jnp.float32, 'w0_f': _jnp.float32, 'w_up_b': _jnp.float32, 'w0_b': _jnp.float32, 'a_up_f': _jnp.float32, 'a0_f': _jnp.float32, 'a_up_b': _jnp.float32, 'a0_b': _jnp.float32, 'g_up': _jnp.float32, 'k_k': _jnp.float32, 'k_a_f': _jnp.float32, 'k_a_b': _jnp.float32, 'r_k_f': _jnp.float32, 'r_k_b': _jnp.float32, 'gn_w': _jnp.float32, 'gn_b': _jnp.float32, 'conv_w': _jnp.float32, 'w_out': _jnp.float32, 'norm2_w': _jnp.float32, 'w_gate': _jnp.float32, 'w_up': _jnp.float32, 'w_down': _jnp.float32, 'norm_f_w': _jnp.float32}
MOMENT_SCALE = {'norm1_w': 2.172700e-01, 'w_in': 1.189979e-01, 'mu_shift': 1.472170e-01, 'w_up_f': 6.079919e-03, 'w0_f': 2.507475e-02, 'w_up_b': 6.440523e-03, 'w0_b': 2.391323e-02, 'a_up_f': 1.664463e-02, 'a0_f': 2.453522e-02, 'a_up_b': 1.679868e-02, 'a0_b': 2.455795e-02, 'g_up': 8.433233e-02, 'k_k': 2.713494e-02, 'k_a_f': 6.817366e-02, 'k_a_b': 6.855244e-02, 'r_k_f': 1.249533e-01, 'r_k_b': 1.268901e-01, 'gn_w': 9.452427e-02, 'gn_b': 8.466934e-02, 'conv_w': 1.645493e-01, 'w_out': 1.193200e-01, 'norm2_w': 1.075734e-01, 'w_gate': 4.681836e-02, 'w_up': 4.533389e-02, 'w_down': 7.528998e-02, 'norm_f_w': 3.192850e+01}


def _to_microbatches(a, axis):
    t = _jnp.moveaxis(a, axis, 0)
    t = t.reshape((N_MICROBATCH, t.shape[0] // N_MICROBATCH) + t.shape[1:])
    return _jnp.moveaxis(t, 1, axis + 1)


def setup_inputs(seed: int = 0) -> dict:
    inp = _fwd_setup_inputs(seed)
    key = _jax.random.fold_in(_jax.random.key(seed), 7919)
    shape, _ = _output_shape()
    out = dict(inp)
    out["loss_target"] = _jax.random.normal(_jax.random.fold_in(key, 0), shape, _jnp.float32)
    for i, name in enumerate(TWIN_WEIGHTS):
        w = inp[name].astype(_jnp.float32)
        if MOMENT_SCALE is None:
            s = _jnp.sqrt(_jnp.mean(_jnp.square(w)) + 1e-30)
        else:
            s = MOMENT_SCALE[name]
        km, kv = _jax.random.split(_jax.random.fold_in(key, i + 1))
        out[name] = w
        out["m_" + name] = s * _jax.random.normal(km, w.shape, _jnp.float32)
        out["v_" + name] = (s * s) * _jax.random.uniform(kv, w.shape, _jnp.float32, 0.5, 1.5)
    if N_MICROBATCH > 1:
        for name, axis in PER_EXAMPLE_BATCH_AXIS.items():
            out[name] = _to_microbatches(out[name], axis)
    return {'x': out['x'], 'norm1_w': out['norm1_w'], 'w_in': out['w_in'], 'mu_shift': out['mu_shift'], 'w_up_f': out['w_up_f'], 'w0_f': out['w0_f'], 'w_up_b': out['w_up_b'], 'w0_b': out['w0_b'], 'a_up_f': out['a_up_f'], 'a0_f': out['a0_f'], 'a_up_b': out['a_up_b'], 'a0_b': out['a0_b'], 'g_up': out['g_up'], 'k_k': out['k_k'], 'k_a_f': out['k_a_f'], 'k_a_b': out['k_a_b'], 'r_k_f': out['r_k_f'], 'r_k_b': out['r_k_b'], 'gn_w': out['gn_w'], 'gn_b': out['gn_b'], 'conv_w': out['conv_w'], 'w_out': out['w_out'], 'norm2_w': out['norm2_w'], 'w_gate': out['w_gate'], 'w_up': out['w_up'], 'w_down': out['w_down'], 'norm_f_w': out['norm_f_w'], 'loss_target': out['loss_target'], 'm_norm1_w': out['m_norm1_w'], 'm_w_in': out['m_w_in'], 'm_mu_shift': out['m_mu_shift'], 'm_w_up_f': out['m_w_up_f'], 'm_w0_f': out['m_w0_f'], 'm_w_up_b': out['m_w_up_b'], 'm_w0_b': out['m_w0_b'], 'm_a_up_f': out['m_a_up_f'], 'm_a0_f': out['m_a0_f'], 'm_a_up_b': out['m_a_up_b'], 'm_a0_b': out['m_a0_b'], 'm_g_up': out['m_g_up'], 'm_k_k': out['m_k_k'], 'm_k_a_f': out['m_k_a_f'], 'm_k_a_b': out['m_k_a_b'], 'm_r_k_f': out['m_r_k_f'], 'm_r_k_b': out['m_r_k_b'], 'm_gn_w': out['m_gn_w'], 'm_gn_b': out['m_gn_b'], 'm_conv_w': out['m_conv_w'], 'm_w_out': out['m_w_out'], 'm_norm2_w': out['m_norm2_w'], 'm_w_gate': out['m_w_gate'], 'm_w_up': out['m_w_up'], 'm_w_down': out['m_w_down'], 'm_norm_f_w': out['m_norm_f_w'], 'v_norm1_w': out['v_norm1_w'], 'v_w_in': out['v_w_in'], 'v_mu_shift': out['v_mu_shift'], 'v_w_up_f': out['v_w_up_f'], 'v_w0_f': out['v_w0_f'], 'v_w_up_b': out['v_w_up_b'], 'v_w0_b': out['v_w0_b'], 'v_a_up_f': out['v_a_up_f'], 'v_a0_f': out['v_a0_f'], 'v_a_up_b': out['v_a_up_b'], 'v_a0_b': out['v_a0_b'], 'v_g_up': out['v_g_up'], 'v_k_k': out['v_k_k'], 'v_k_a_f': out['v_k_a_f'], 'v_k_a_b': out['v_k_a_b'], 'v_r_k_f': out['v_r_k_f'], 'v_r_k_b': out['v_r_k_b'], 'v_gn_w': out['v_gn_w'], 'v_gn_b': out['v_gn_b'], 'v_conv_w': out['v_conv_w'], 'v_w_out': out['v_w_out'], 'v_norm2_w': out['v_norm2_w'], 'v_w_gate': out['v_w_gate'], 'v_w_up': out['v_w_up'], 'v_w_down': out['v_w_down'], 'v_norm_f_w': out['v_norm_f_w']}


def _loss(weights, diff, rest, loss_target):
    with _jax.named_scope("forward"):
        args = {**rest, TWIN_DIFF_INPUT: diff, **{k: w.astype(_WEIGHT_DTYPES[k]) for k, w in weights.items()}}
        y = _forward(args)
    with _jax.named_scope("loss_head"):
        err = _jnp.square(y.astype(_jnp.float32) - loss_target)
        return 0.5 * _jnp.sum(_jnp.mean(err, axis=-1)) if err.ndim else 0.5 * err


def _adamw(w, g, m, v):
    m = ADAM_B1 * m + (1.0 - ADAM_B1) * g
    v = ADAM_B2 * v + (1.0 - ADAM_B2) * _jnp.square(g)
    m_hat = m / (1.0 - ADAM_B1 ** ADAM_STEP)
    v_hat = v / (1.0 - ADAM_B2 ** ADAM_STEP)
    delta = -ADAM_LR * (m_hat / (_jnp.sqrt(v_hat) + ADAM_EPS) + ADAM_WD * w)
    return delta, m, v


def reference(x, norm1_w, w_in, mu_shift, w_up_f, w0_f, w_up_b, w0_b, a_up_f, a0_f, a_up_b, a0_b, g_up, k_k, k_a_f, k_a_b, r_k_f, r_k_b, gn_w, gn_b, conv_w, w_out, norm2_w, w_gate, w_up, w_down, norm_f_w, loss_target, m_norm1_w, m_w_in, m_mu_shift, m_w_up_f, m_w0_f, m_w_up_b, m_w0_b, m_a_up_f, m_a0_f, m_a_up_b, m_a0_b, m_g_up, m_k_k, m_k_a_f, m_k_a_b, m_r_k_f, m_r_k_b, m_gn_w, m_gn_b, m_conv_w, m_w_out, m_norm2_w, m_w_gate, m_w_up, m_w_down, m_norm_f_w, v_norm1_w, v_w_in, v_mu_shift, v_w_up_f, v_w0_f, v_w_up_b, v_w0_b, v_a_up_f, v_a0_f, v_a_up_b, v_a0_b, v_g_up, v_k_k, v_k_a_f, v_k_a_b, v_r_k_f, v_r_k_b, v_gn_w, v_gn_b, v_conv_w, v_w_out, v_norm2_w, v_w_gate, v_w_up, v_w_down, v_norm_f_w):
    given = dict(x=x, norm1_w=norm1_w, w_in=w_in, mu_shift=mu_shift, w_up_f=w_up_f, w0_f=w0_f, w_up_b=w_up_b, w0_b=w0_b, a_up_f=a_up_f, a0_f=a0_f, a_up_b=a_up_b, a0_b=a0_b, g_up=g_up, k_k=k_k, k_a_f=k_a_f, k_a_b=k_a_b, r_k_f=r_k_f, r_k_b=r_k_b, gn_w=gn_w, gn_b=gn_b, conv_w=conv_w, w_out=w_out, norm2_w=norm2_w, w_gate=w_gate, w_up=w_up, w_down=w_down, norm_f_w=norm_f_w, loss_target=loss_target, m_norm1_w=m_norm1_w, m_w_in=m_w_in, m_mu_shift=m_mu_shift, m_w_up_f=m_w_up_f, m_w0_f=m_w0_f, m_w_up_b=m_w_up_b, m_w0_b=m_w0_b, m_a_up_f=m_a_up_f, m_a0_f=m_a0_f, m_a_up_b=m_a_up_b, m_a0_b=m_a0_b, m_g_up=m_g_up, m_k_k=m_k_k, m_k_a_f=m_k_a_f, m_k_a_b=m_k_a_b, m_r_k_f=m_r_k_f, m_r_k_b=m_r_k_b, m_gn_w=m_gn_w, m_gn_b=m_gn_b, m_conv_w=m_conv_w, m_w_out=m_w_out, m_norm2_w=m_norm2_w, m_w_gate=m_w_gate, m_w_up=m_w_up, m_w_down=m_w_down, m_norm_f_w=m_norm_f_w, v_norm1_w=v_norm1_w, v_w_in=v_w_in, v_mu_shift=v_mu_shift, v_w_up_f=v_w_up_f, v_w0_f=v_w0_f, v_w_up_b=v_w_up_b, v_w0_b=v_w0_b, v_a_up_f=v_a_up_f, v_a0_f=v_a0_f, v_a_up_b=v_a_up_b, v_a0_b=v_a0_b, v_g_up=v_g_up, v_k_k=v_k_k, v_k_a_f=v_k_a_f, v_k_a_b=v_k_a_b, v_r_k_f=v_r_k_f, v_r_k_b=v_r_k_b, v_gn_w=v_gn_w, v_gn_b=v_gn_b, v_conv_w=v_conv_w, v_w_out=v_w_out, v_norm2_w=v_norm2_w, v_w_gate=v_w_gate, v_w_up=v_w_up, v_w_down=v_w_down, v_norm_f_w=v_norm_f_w)
    weights = {n: given[n] for n in TWIN_WEIGHTS}
    shared = {n: given[n] for n in SHARED_INPUTS}
    per_example = {n: given[n] for n in ['x']}
    grad_fn = _jax.value_and_grad(_loss, argnums=(0, 1))

    def one_microbatch(ex, loss_target):
        ex = dict(ex)
        diff = ex.pop(TWIN_DIFF_INPUT)
        return grad_fn(weights, diff, {**shared, **ex}, loss_target)

    if N_MICROBATCH == 1:
        loss, (grad_w, grad_x) = one_microbatch(per_example, given["loss_target"])
    else:
        def body(carry, xs):
            loss_sum, grad_sum = carry
            l_k, (gw_k, gx_k) = one_microbatch(xs[0], xs[1])
            with _jax.named_scope("update"):
                return (loss_sum + l_k, _jax.tree.map(_jnp.add, grad_sum, gw_k)), gx_k

        init = (_jnp.zeros((), _jnp.float32), _jax.tree.map(_jnp.zeros_like, weights))
        (loss, grad_w), grad_x = _jax.lax.scan(body, init, (per_example, given["loss_target"]))
    with _jax.named_scope("update"):
        delta_w, new_m, new_v = {}, {}, {}
        for n in TWIN_WEIGHTS:
            delta_w[n], new_m[n], new_v[n] = _adamw(weights[n], grad_w[n], given["m_" + n], given["v_" + n])
    return (loss, grad_x, *[grad_w[n] for n in TWIN_WEIGHTS], *[delta_w[n] for n in TWIN_WEIGHTS],
            *[new_m[n] for n in TWIN_WEIGHTS], *[new_v[n] for n in TWIN_WEIGHTS])
```

```python
import functools

import jax
import jax.numpy as jnp
from jax import lax
from jax.experimental import pallas as pl
from jax.experimental.pallas import tpu as pltpu

F32 = jnp.float32
BF16 = jnp.bfloat16
HIGHEST = lax.Precision.HIGHEST

N_DEV = 8
D_MODEL = 1024
D_RWKV = 512
D_CONV = 512
HEAD = 64
D_LORA = 64
D_GATE = 160
D_FF = 2816
D_SHIFTED = 3 * D_RWKV + 2 * D_LORA + D_GATE
D_IN = D_SHIFTED + 3 * D_CONV
XW0, XA0, XG0 = 1536, 1664, 1792
D_SP = 2048
D_INP = D_SP + 3 * D_CONV
LOG_DECAY_SCALE = 0.606531
RMS_EPS = 1e-6
GN_EPS = 64e-5
NORM_EPS = 1e-12
ADAM_LR, ADAM_B1, ADAM_B2, ADAM_EPS, ADAM_WD, ADAM_STEP = 0.001, 0.9, 0.999, 1e-08, 0.01, 10

LANES = 128
SUBLANES = 8
VMEM_LIMIT = 48 * 1024 * 1024
SCAN_CHUNK = 16
ROW_TILE = 128
ADAM_TILE = 512

SHARDED = ("w_in", "w_out", "w_gate", "w_up", "w_down",
           "w_up_f", "w_up_b", "a_up_f", "a_up_b", "g_up", "conv_w")
SHARD_AXIS = {"w_in": 2, "w_out": 1, "w_gate": 2, "w_up": 2, "w_down": 1, "w_up_f": 2, "w_up_b": 2,
              "a_up_f": 2, "a_up_b": 2, "g_up": 2, "conv_w": 2}
BIG = ("w_in", "w_out", "w_gate", "w_up", "w_down")
SMALL = ("w_up_f", "w_up_b", "a_up_f", "a_up_b", "g_up", "conv_w")
REPLICATED = ("norm1_w", "mu_shift", "w0_f", "w0_b", "a0_f", "a0_b", "k_k", "k_a_f", "k_a_b",
              "r_k_f", "r_k_b", "gn_w", "gn_b", "norm2_w", "norm_f_w")
WEIGHTS = ("norm1_w", "w_in", "mu_shift", "w_up_f", "w0_f", "w_up_b", "w0_b", "a_up_f", "a0_f", "a_up_b",
           "a0_b", "g_up", "k_k", "k_a_f", "k_a_b", "r_k_f", "r_k_b", "gn_w", "gn_b", "conv_w", "w_out",
           "norm2_w", "w_gate", "w_up", "w_down", "norm_f_w")
PACK_ORDER = SHARDED + REPLICATED


def _params(sem, limit=VMEM_LIMIT):
    return pltpu.CompilerParams(dimension_semantics=sem, vmem_limit_bytes=limit)


def _tile(n, cands):
    for c in cands:
        if n % c == 0:
            return c
    raise ValueError(f"no tile for {n}")


def _mm(a, b, *, ta=False, tb=False, add=None, name):
    (k_dim, m) = a.shape if ta else a.shape[::-1]
    (k2, n) = b.shape[::-1] if tb else b.shape
    assert k_dim == k2, (a.shape, b.shape, ta, tb)
    tm = _tile(m, (512, 256, 128))
    tn = _tile(n, (512, 256, 128))
    tk = _tile(k_dim, (512, 256, 128))
    nk = k_dim // tk
    dims = (((0 if ta else 1,), (1 if tb else 0,)), ((), ()))

    def kern(*refs):
        if add is None:
            a_ref, b_ref, o_ref, acc_ref = refs
        else:
            a_ref, b_ref, add_ref, o_ref, acc_ref = refs
        k = pl.program_id(2)

        @pl.when(k == 0)
        def _():
            acc_ref[...] = jnp.zeros_like(acc_ref)

        acc_ref[...] += lax.dot_general(a_ref[...].astype(BF16), b_ref[...].astype(BF16), dims,
                                        preferred_element_type=F32)

        @pl.when(k == nk - 1)
        def _():
            if add is None:
                o_ref[...] = acc_ref[...]
            else:
                o_ref[...] = acc_ref[...] + add_ref[...]

    a_spec = (pl.BlockSpec((tk, tm), lambda i, j, k: (k, i)) if ta
              else pl.BlockSpec((tm, tk), lambda i, j, k: (i, k)))
    b_spec = (pl.BlockSpec((tn, tk), lambda i, j, k: (j, k)) if tb
              else pl.BlockSpec((tk, tn), lambda i, j, k: (k, j)))
    o_spec = pl.BlockSpec((tm, tn), lambda i, j, k: (i, j))
    in_specs = [a_spec, b_spec] + ([o_spec] if add is not None else [])
    args = (a, b) + ((add,) if add is not None else ())
    return pl.pallas_call(
        kern, out_shape=jax.ShapeDtypeStruct((m, n), F32), grid=(m // tm, n // tn, nk),
        in_specs=in_specs, out_specs=o_spec, scratch_shapes=[pltpu.VMEM((tm, tn), F32)],
        compiler_params=_params(("parallel", "parallel", "arbitrary")), name=name)(*args)


def _rowwise(fn, rows, consts, out_rows, out_accs, *, name, tb=ROW_TILE):
    t = (rows[0][0] if isinstance(rows[0], tuple) else rows[0]).shape[0]
    n_r, n_c, n_o, n_a = len(rows), len(consts), len(out_rows), len(out_accs)
    pieces = [w if isinstance(w, (list, tuple)) else [w] for w in out_rows]

    def kern(*refs):
        r_refs = refs[:n_r]
        c_refs = refs[n_r:n_r + n_c]
        o_refs = refs[n_r + n_c:n_r + n_c + n_o]
        a_refs = refs[n_r + n_c + n_o:]
        vals = fn(*[r[...] for r in r_refs], *[c[...] for c in c_refs])
        vals = list(vals) if isinstance(vals, (tuple, list)) else [vals]
        pos = 0
        for o_ref, ws in zip(o_refs, pieces):
            off = 0
            for w in ws:
                o_ref[:, off:off + w] = vals[pos]
                off += w
                pos += 1
        if n_a:
            @pl.when(pl.program_id(0) == 0)
            def _():
                for a_ref in a_refs:
                    a_ref[...] = jnp.zeros_like(a_ref)
            for a_ref, v in zip(a_refs, vals[pos:]):
                a_ref[...] += v

    in_specs, args = [], []
    for r in rows:
        if isinstance(r, tuple):
            arr, blk, w = r
            in_specs.append(pl.BlockSpec((tb, w), functools.partial(lambda i, blk: (i, blk), blk=blk)))
        else:
            arr = r
            in_specs.append(pl.BlockSpec((tb, arr.shape[1]), lambda i: (i, 0)))
        args.append(arr)
    for c in consts:
        in_specs.append(pl.BlockSpec(c.shape, lambda i: (0, 0)))
        args.append(c)
    out_shape = [jax.ShapeDtypeStruct((t, sum(ws)), F32) for ws in pieces]
    out_specs = [pl.BlockSpec((tb, sum(ws)), lambda i: (i, 0)) for ws in pieces]
    for shp in out_accs:
        out_shape.append(jax.ShapeDtypeStruct(shp, F32))
        out_specs.append(pl.BlockSpec(shp, lambda i: (0, 0)))
    res = pl.pallas_call(
        kern, out_shape=out_shape, grid=(t // tb,), in_specs=in_specs, out_specs=out_specs,
        compiler_params=_params(("arbitrary",) if n_a else ("parallel",)), name=name)(*args)
    return res


def _rms(x, w):
    return x * lax.rsqrt(jnp.mean(x * x, axis=-1, keepdims=True) + RMS_EPS) * w


def _seg(x, bd):
    return jnp.dot(x, bd, precision=HIGHEST, preferred_element_type=F32)


def _colsum(x):
    return jnp.sum(x, axis=0, keepdims=True)


def _prescan_math(r, k, xw, xa, xg, k_k, w0f, w0b, a0f, a0b, kaf, kab, wupf, wupb, aupf, aupb, gup, bd):
    del r
    kkr = k * k_k
    norm = jnp.sqrt(_seg(kkr * kkr, bd))
    kk = kkr / jnp.maximum(norm, NORM_EPS)
    th = jnp.tanh(xw)

    def direction(w0, wup, a0, aup, ka):
        logit = w0 + jnp.dot(th, wup, preferred_element_type=F32)
        w = jnp.exp(-LOG_DECAY_SCALE * jax.nn.sigmoid(logit))
        a = jax.nn.sigmoid(a0 + jnp.dot(xa, aup, preferred_element_type=F32))
        kd = k * (1.0 + (a - 1.0) * ka)
        return w, kd, kk * a

    wf, kdf, bf = direction(w0f, wupf, a0f, aupf, kaf)
    wb, kdb, bb = direction(w0b, wupb, a0b, aupb, kab)
    g = jnp.dot(jax.nn.sigmoid(xg), gup, preferred_element_type=F32)
    return kk, wf, wb, kdf, kdb, bf, bb, g


def _postscan_math(yf, yb, r, v, kdf, kdb, g, gn_w, gn_b, rkf, rkb, bd):
    y = yf + yb
    mean = _seg(y, bd) * (1.0 / HEAD)
    yc = y - mean
    var = _seg(yc * yc, bd) * (1.0 / HEAD)
    yg = yc * lax.rsqrt(var + GN_EPS) * gn_w + gn_b
    bonus = (_seg(r * kdf * rkf, bd) + _seg(r * kdb * rkb, bd)) * v
    return (yg + bonus) * g


def _halo_specs(width, col_blk, tb, t, seq):
    nb = t // SUBLANES
    step = tb // SUBLANES
    main = pl.BlockSpec((tb, width), lambda i: (i, col_blk))
    prev = pl.BlockSpec((SUBLANES, width), lambda i: (jnp.maximum(i * step - 1, 0), col_blk))
    nxt = pl.BlockSpec((SUBLANES, width), lambda i: (jnp.minimum((i + 1) * step, nb - 1), col_blk))
    del seq
    return [main, prev, nxt]


def _neighbours(z, prev8, next8, first, last):
    tb = z.shape[0]
    row = lax.broadcasted_iota(jnp.int32, z.shape, 0)
    prow = jnp.where(first, 0.0, prev8[SUBLANES - 1:SUBLANES, :])
    nrow = jnp.where(last, 0.0, next8[0:1, :])
    down = jnp.where(row == 0, prow, pltpu.roll(z, 1, 0))
    up = jnp.where(row == tb - 1, nrow, pltpu.roll(z, tb - 1, 0))
    return down, up


def _shift_conv_fwd(p, mu, conv_w, seq, *, name, tb=ROW_TILE):
    t = p.shape[0]
    per_seq = seq // tb

    def kern(p_ref, pp_ref, pn_ref, mu_ref, cw_ref, pss_ref, oc_ref):
        i = pl.program_id(0)
        first = (i % per_seq) == 0
        last = (i % per_seq) == per_seq - 1
        ps = p_ref[:, :D_SP]
        down, up = _neighbours(ps, pp_ref[:, :D_SP], pn_ref[:, :D_SP], first, last)
        pss_ref[...] = ps + mu_ref[...] * (0.5 * (down + up) - ps)
        gb = p_ref[:, D_SP:D_SP + D_CONV]
        u = p_ref[:, D_SP + D_CONV:D_SP + 2 * D_CONV] * p_ref[:, D_SP + 2 * D_CONV:]
        u_p = pp_ref[:, D_SP + D_CONV:D_SP + 2 * D_CONV] * pp_ref[:, D_SP + 2 * D_CONV:]
        u_n = pn_ref[:, D_SP + D_CONV:D_SP + 2 * D_CONV] * pn_ref[:, D_SP + 2 * D_CONV:]
        udown, uup = _neighbours(u, u_p, u_n, first, last)
        oc_ref[...] = gb * (cw_ref[0:1, :] * udown + cw_ref[1:2, :] * u + cw_ref[2:3, :] * uup)

    return pl.pallas_call(
        kern,
        out_shape=[jax.ShapeDtypeStruct((t, D_SP), F32), jax.ShapeDtypeStruct((t, D_CONV), F32)],
        grid=(t // tb,),
        in_specs=_halo_specs(D_INP, 0, tb, t, seq) + [pl.BlockSpec((1, D_SP), lambda i: (0, 0)),
                                                      pl.BlockSpec((SUBLANES, D_CONV), lambda i: (0, 0))],
        out_specs=[pl.BlockSpec((tb, D_SP), lambda i: (i, 0)), pl.BlockSpec((tb, D_CONV), lambda i: (i, 0))],
        compiler_params=_params(("parallel",)), name=name)(p, p, p, mu, conv_w)


def _shift_conv_bwd(p, d_pss, d_o, mu, conv_w, seq, *, name, tb=ROW_TILE):
    t = p.shape[0]
    per_seq = seq // tb

    def kern(p_ref, pp_ref, pn_ref, d_ref, dp_ref, dn_ref, do_ref, dop_ref, don_ref, mu_ref, cw_ref,
             out_ref, dmu_ref, dcw_ref):
        i = pl.program_id(0)
        first = (i % per_seq) == 0
        last = (i % per_seq) == per_seq - 1

        @pl.when(i == 0)
        def _():
            dmu_ref[...] = jnp.zeros_like(dmu_ref)
            dcw_ref[...] = jnp.zeros_like(dcw_ref)

        mu_v = mu_ref[...]
        ps = p_ref[:, :D_SP]
        down, up = _neighbours(ps, pp_ref[:, :D_SP], pn_ref[:, :D_SP], first, last)
        d = d_ref[...]
        ddown, dup = _neighbours(d, dp_ref[...], dn_ref[...], first, last)
        out_ref[:, :D_SP] = d - mu_v * d + 0.5 * (mu_v * ddown + mu_v * dup)
        dmu_ref[...] += _colsum(d * (0.5 * (down + up) - ps))

        def parts(ref):
            return (ref[:, D_SP:D_SP + D_CONV], ref[:, D_SP + D_CONV:D_SP + 2 * D_CONV],
                    ref[:, D_SP + 2 * D_CONV:])

        gb, gc, hh = parts(p_ref)
        gb_p, gc_p, hh_p = parts(pp_ref)
        gb_n, gc_n, hh_n = parts(pn_ref)
        u = gc * hh
        udown, uup = _neighbours(u, gc_p * hh_p, gc_n * hh_n, first, last)
        cw0, cw1, cw2 = cw_ref[0:1, :], cw_ref[1:2, :], cw_ref[2:3, :]
        do = do_ref[...]
        duc = do * gb
        ducdown, ducup = _neighbours(duc, dop_ref[...] * gb_p, don_ref[...] * gb_n, first, last)
        du = cw0 * ducup + cw1 * duc + cw2 * ducdown
        out_ref[:, D_SP:D_SP + D_CONV] = do * (cw0 * udown + cw1 * u + cw2 * uup)
        out_ref[:, D_SP + D_CONV:D_SP + 2 * D_CONV] = du * hh
        out_ref[:, D_SP + 2 * D_CONV:] = du * gc
        dcw_ref[0:1, :] += _colsum(duc * udown)
        dcw_ref[1:2, :] += _colsum(duc * u)
        dcw_ref[2:3, :] += _colsum(duc * uup)

    return pl.pallas_call(
        kern,
        out_shape=[jax.ShapeDtypeStruct((t, D_INP), F32), jax.ShapeDtypeStruct((1, D_SP), F32),
                   jax.ShapeDtypeStruct((SUBLANES, D_CONV), F32)],
        grid=(t // tb,),
        in_specs=(_halo_specs(D_INP, 0, tb, t, seq) + _halo_specs(D_SP, 0, tb, t, seq)
                  + _halo_specs(D_CONV, 1, tb, t, seq)
                  + [pl.BlockSpec((1, D_SP), lambda i: (0, 0)),
                     pl.BlockSpec((SUBLANES, D_CONV), lambda i: (0, 0))]),
        out_specs=[pl.BlockSpec((tb, D_INP), lambda i: (i, 0)), pl.BlockSpec((1, D_SP), lambda i: (0, 0)),
                   pl.BlockSpec((SUBLANES, D_CONV), lambda i: (0, 0))],
        compiler_params=_params(("arbitrary",)), name=name)(p, p, p, d_pss, d_pss, d_pss, d_o, d_o, d_o, mu, conv_w)


def _masks():
    lane = lax.broadcasted_iota(jnp.int32, (HEAD, LANES), 1)
    row = lax.broadcasted_iota(jnp.int32, (HEAD, LANES), 0)
    lo = lane < HEAD
    dlo = lane == row
    dhi = lane == row + HEAD
    return lo, dlo, dhi


def _segsum(x, lo):
    s_lo = jnp.sum(jnp.where(lo, x, 0.0), axis=1, keepdims=True)
    s_hi = jnp.sum(jnp.where(lo, 0.0, x), axis=1, keepdims=True)
    return jnp.where(lo, s_lo, s_hi)


def _to_col(row_vec, lo, dlo, dhi):
    rb = jnp.broadcast_to(row_vec, (HEAD, LANES))
    c_lo = jnp.sum(jnp.where(dlo, rb, 0.0), axis=1, keepdims=True)
    c_hi = jnp.sum(jnp.where(dhi, rb, 0.0), axis=1, keepdims=True)
    return jnp.where(lo, c_lo, c_hi)


def _to_row(col_tile, dlo, dhi):
    return jnp.sum(jnp.where(dlo | dhi, col_tile, 0.0), axis=0, keepdims=True)


def _scan_specs(arrs, seq, cidx):
    specs, args = [], []
    for a in arrs:
        arr, blk = a if isinstance(a, tuple) else (a, 0)
        specs.append(pl.BlockSpec((2, SCAN_CHUNK, D_RWKV),
                                  functools.partial(lambda c, blk: (0, cidx(c), blk), blk=blk)))
        args.append(arr)
    return specs, args


def _scan_fwd(r, w, k, v, kk, b, *, reverse, name):
    seq = (w[0] if isinstance(w, tuple) else w).shape[1]
    nc = seq // SCAN_CHUNK
    groups = SCAN_CHUNK // SUBLANES
    cidx = (lambda c: nc - 1 - c) if reverse else (lambda c: c)
    order = tuple(range(SUBLANES - 1, -1, -1)) if reverse else tuple(range(SUBLANES))

    def kern(r_ref, w_ref, k_ref, v_ref, kk_ref, b_ref, y_ref, hist_ref, fin_ref, st_ref):
        c = pl.program_id(0)

        @pl.when(c == 0)
        def _():
            st_ref[...] = jnp.zeros_like(st_ref)

        lo, dlo, dhi = _masks()
        row8 = lax.broadcasted_iota(jnp.int32, (SUBLANES, LANES), 0)

        def group(gi, carry):
            g = (groups - 1 - gi) if reverse else gi
            base = pl.multiple_of(g * SUBLANES, SUBLANES)
            for bi in range(2):
                for j in range(4):
                    lanes = slice(LANES * j, LANES * (j + 1))
                    rt, wt, kt, vt, kkt, bt = (ref[bi, pl.ds(base, SUBLANES), lanes]
                                               for ref in (r_ref, w_ref, k_ref, v_ref, kk_ref, b_ref))
                    state = st_ref[bi, j]
                    ytile = jnp.zeros((SUBLANES, LANES), F32)
                    for s in order:
                        hist_ref[base + s, bi, j] = state
                        sa = _segsum(state * kkt[s:s + 1, :], lo)
                        vcol = _to_col(vt[s:s + 1, :], lo, dlo, dhi)
                        state = state * wt[s:s + 1, :] - sa * bt[s:s + 1, :] + vcol * kt[s:s + 1, :]
                        yrow = _to_row(_segsum(state * rt[s:s + 1, :], lo), dlo, dhi)
                        ytile = jnp.where(row8 == s, yrow, ytile)
                    st_ref[bi, j] = state
                    y_ref[bi, pl.ds(base, SUBLANES), lanes] = ytile
            return carry

        lax.fori_loop(0, groups, group, 0)

        @pl.when(c == nc - 1)
        def _():
            fin_ref[...] = st_ref[...]

    in_specs, args = _scan_specs((r, w, k, v, kk, b), seq, cidx)
    tile_shape = (2, 4, HEAD, LANES)
    return pl.pallas_call(
        kern,
        out_shape=[jax.ShapeDtypeStruct((2, seq, D_RWKV), F32),
                   jax.ShapeDtypeStruct((seq,) + tile_shape, F32),
                   jax.ShapeDtypeStruct(tile_shape, F32)],
        grid=(nc,), in_specs=in_specs,
        out_specs=[pl.BlockSpec((2, SCAN_CHUNK, D_RWKV), lambda c: (0, cidx(c), 0)),
                   pl.BlockSpec((SCAN_CHUNK,) + tile_shape, lambda c: (cidx(c), 0, 0, 0, 0)),
                   pl.BlockSpec(tile_shape, lambda c: (0, 0, 0, 0))],
        scratch_shapes=[pltpu.VMEM(tile_shape, F32)],
        compiler_params=_params(("arbitrary",)), name=name)(*args)


def _scan_bwd(r, w, k, v, kk, b, dy, hist, fin, *, reverse, name):
    seq = dy.shape[1]
    nc = seq // SCAN_CHUNK
    groups = SCAN_CHUNK // SUBLANES
    desc = not reverse
    cidx = (lambda c: nc - 1 - c) if desc else (lambda c: c)
    order = tuple(range(SUBLANES - 1, -1, -1)) if desc else tuple(range(SUBLANES))

    def kern(r_ref, w_ref, k_ref, v_ref, kk_ref, b_ref, dy_ref, hist_ref, fin_ref,
             dr_ref, dw_ref, dk_ref, dv_ref, dkk_ref, db_ref, ds_ref, sc_ref):
        c = pl.program_id(0)

        @pl.when(c == 0)
        def _():
            ds_ref[...] = jnp.zeros_like(ds_ref)
            sc_ref[...] = fin_ref[...]

        lo, dlo, dhi = _masks()
        row8 = lax.broadcasted_iota(jnp.int32, (SUBLANES, LANES), 0)

        def group(gi, carry):
            g = (groups - 1 - gi) if desc else gi
            base = pl.multiple_of(g * SUBLANES, SUBLANES)
            for bi in range(2):
                for j in range(4):
                    lanes = slice(LANES * j, LANES * (j + 1))
                    rt, wt, kt, vt, kkt, bt, dyt = (
                        ref[bi, pl.ds(base, SUBLANES), lanes]
                        for ref in (r_ref, w_ref, k_ref, v_ref, kk_ref, b_ref, dy_ref))
                    ds = ds_ref[bi, j]
                    after = sc_ref[bi, j]
                    tiles = [jnp.zeros((SUBLANES, LANES), F32) for _ in range(6)]
                    for s in order:
                        before = hist_ref[base + s, bi, j]
                        dycol = _to_col(dyt[s:s + 1, :], lo, dlo, dhi)
                        vcol = _to_col(vt[s:s + 1, :], lo, dlo, dhi)
                        kkr, br = kkt[s:s + 1, :], bt[s:s + 1, :]
                        gs = ds + dycol * rt[s:s + 1, :]
                        sa = _segsum(before * kkr, lo)
                        dsa = -_segsum(gs * br, lo)
                        rows = (_colsum(after * dycol),
                                _colsum(gs * before),
                                _colsum(gs * vcol),
                                _to_row(_segsum(gs * kt[s:s + 1, :], lo), dlo, dhi),
                                _colsum(before * dsa),
                                -_colsum(gs * sa))
                        tiles = [jnp.where(row8 == s, rw, tl) for rw, tl in zip(rows, tiles)]
                        ds = gs * wt[s:s + 1, :] + dsa * kkr
                        after = before
                    ds_ref[bi, j] = ds
                    sc_ref[bi, j] = after
                    for ref, tl in zip((dr_ref, dw_ref, dk_ref, dv_ref, dkk_ref, db_ref), tiles):
                        ref[bi, pl.ds(base, SUBLANES), lanes] = tl
            return carry

        lax.fori_loop(0, groups, group, 0)

    in_specs, args = _scan_specs((r, w, k, v, kk, b, dy), seq, cidx)
    tile_shape = (2, 4, HEAD, LANES)
    in_specs += [pl.BlockSpec((SCAN_CHUNK,) + tile_shape, lambda c: (cidx(c), 0, 0, 0, 0)),
                 pl.BlockSpec(tile_shape, lambda c: (0, 0, 0, 0))]
    out_spec = pl.BlockSpec((2, SCAN_CHUNK, D_RWKV), lambda c: (0, cidx(c), 0))
    return pl.pallas_call(
        kern, out_shape=[jax.ShapeDtypeStruct((2, seq, D_RWKV), F32)] * 6,
        grid=(nc,), in_specs=in_specs, out_specs=[out_spec] * 6,
        scratch_shapes=[pltpu.VMEM(tile_shape, F32), pltpu.VMEM(tile_shape, F32)],
        compiler_params=_params(("arbitrary",)), name=name)(*args, hist, fin)


def _pad_cols(a, segs):
    out, off = [], 0
    for w, wp in segs:
        out.append(a[..., off:off + w])
        if wp > w:
            out.append(jnp.zeros(a.shape[:-1] + (wp - w,), a.dtype))
        off += w
    return jnp.concatenate(out, axis=-1)


def _unpad_cols(a, segs):
    out, off = [], 0
    for w, wp in segs:
        out.append(a[..., off:off + w])
        off += wp
    return jnp.concatenate(out, axis=-1)


P_SEGS = ((3 * D_RWKV, 3 * D_RWKV), (D_LORA, 128), (D_LORA, 128), (D_GATE, 256), (3 * D_CONV, 3 * D_CONV))
S_SEGS = P_SEGS[:4]


def _pad_rows(a, rows):
    return jnp.concatenate([a, jnp.zeros((rows - a.shape[0], a.shape[1]), a.dtype)], axis=0)


def _local_step(x, target, w):
    bsz, seq, _ = x.shape
    t = bsz * seq
    x2d = x.reshape(t, D_MODEL)
    tg2d = target.reshape(t, D_MODEL)
    row = lambda a: a.reshape(1, -1).astype(F32)

    w_in = _pad_cols(w["w_in"][0], P_SEGS)
    mu = _pad_cols(row(w["mu_shift"]), S_SEGS)
    wupf, wupb, aupf, aupb = (_pad_rows(w[n][0].astype(F32), 128) for n in ("w_up_f", "w_up_b", "a_up_f", "a_up_b"))
    gup = _pad_rows(w["g_up"][0].astype(F32), 256)
    conv_w = _pad_rows(w["conv_w"][0].astype(F32), SUBLANES)
    w_out, w_gate, w_up, w_down = w["w_out"][0], w["w_gate"][0], w["w_up"][0], w["w_down"][0]
    norm1, norm2, normf = row(w["norm1_w"]), row(w["norm2_w"]), row(w["norm_f_w"])
    vec = {n: row(w[n]) for n in ("k_k", "w0_f", "w0_b", "a0_f", "a0_b", "k_a_f", "k_a_b", "gn_w", "gn_b",
                                  "r_k_f", "r_k_b")}
    head_of = jnp.arange(D_RWKV) // HEAD
    bd = (head_of[:, None] == head_of[None, :]).astype(F32)
    pre_consts = [vec["k_k"], vec["w0_f"], vec["w0_b"], vec["a0_f"], vec["a0_b"], vec["k_a_f"], vec["k_a_b"],
                  wupf, wupb, aupf, aupb, gup, bd]
    post_consts = [vec["gn_w"], vec["gn_b"], vec["r_k_f"], vec["r_k_b"], bd]

    h1, = _rowwise(_rms, [x2d], [norm1], [D_MODEL], [], name="rms1_fwd")
    p = _mm(h1, w_in, name="mm_in")
    pss, oconv = _shift_conv_fwd(p, mu, conv_w, seq, name="shift_conv_fwd")
    pre_rows = [(pss, 0, 512), (pss, 1, 512), (pss, XW0 // 128, 128), (pss, XA0 // 128, 128), (pss, XG0 // 256, 256)]
    kk, wf, wb, kdf, kdb, bf, bb, g = _rowwise(_prescan_math, pre_rows, pre_consts, [D_RWKV] * 8, [],
                                               name="prescan_fwd")
    pss3 = pss.reshape(bsz, seq, D_SP)
    to3 = lambda a: a.reshape(bsz, seq, D_RWKV)
    r3, v3 = (pss3, 0), (pss3, 2)
    yf, hist_f, fin_f = _scan_fwd(r3, to3(wf), to3(kdf), v3, to3(kk), to3(bf), reverse=False, name="scan_fwd_f")
    yb, hist_b, fin_b = _scan_fwd(r3, to3(wb), to3(kdb), v3, to3(kk), to3(bb), reverse=True, name="scan_fwd_b")
    yf2, yb2 = yf.reshape(t, D_RWKV), yb.reshape(t, D_RWKV)
    post_rows = [yf2, yb2, (pss, 0, 512), (pss, 2, 512), kdf, kdb, g]

    def post_fwd(yf_, yb_, r_, v_, kdf_, kdb_, g_, oc_, *consts):
        return _postscan_math(yf_, yb_, r_, v_, kdf_, kdb_, g_, *consts), oc_

    o, = _rowwise(post_fwd, post_rows + [oconv], post_consts, [[D_RWKV, D_CONV]], [], name="postscan_fwd")
    x1 = _mm(o, w_out, add=x2d, name="mm_out")
    h2, = _rowwise(_rms, [x1], [norm2], [D_MODEL], [], name="rms2_fwd")
    gg = _mm(h2, w_gate, name="mm_gate")
    uu = _mm(h2, w_up, name="mm_up")
    ff, = _rowwise(lambda a, c: jax.nn.silu(a) * c, [gg, uu], [], [D_FF], [], name="swiglu_fwd")
    x2 = _mm(ff, w_down, add=x1, name="mm_down")

    def final(x_, tg_, wn_):
        yo, vjp = jax.vjp(_rms, x_, wn_)
        err = yo - tg_
        dx_, dwn_ = vjp(err * (1.0 / D_MODEL))
        part = jnp.sum(jnp.sum(err * err, axis=1, keepdims=True), axis=0, keepdims=True) * (0.5 / D_MODEL)
        return dx_, part + jnp.zeros((1, LANES), F32), dwn_

    dx2, loss_acc, d_normf = _rowwise(final, [x2, tg2d], [normf], [D_MODEL], [(1, LANES), (1, D_MODEL)],
                                      name="loss_head")
    dff = _mm(dx2, w_down, tb=True, name="mm_down_dx")
    g_w_down = _mm(ff, dx2, ta=True, name="mm_down_dw")

    def swiglu_bwd(a, c, d):
        _, vjp = jax.vjp(lambda a_, c_: jax.nn.silu(a_) * c_, a, c)
        return vjp(d)

    dgg, duu = _rowwise(swiglu_bwd, [gg, uu, dff], [], [D_FF, D_FF], [], name="swiglu_bwd")
    dh2 = _mm(dgg, w_gate, tb=True, name="mm_gate_dx")
    dh2 = _mm(duu, w_up, tb=True, add=dh2, name="mm_up_dx")
    g_w_gate = _mm(h2, dgg, ta=True, name="mm_gate_dw")
    g_w_up = _mm(h2, duu, ta=True, name="mm_up_dw")

    def rms_bwd(x_, dh_, dres_, wn_):
        _, vjp = jax.vjp(_rms, x_, wn_)
        dx_, dwn_ = vjp(dh_)
        return dx_ + dres_, dwn_

    dx1, d_norm2 = _rowwise(rms_bwd, [x1, dh2, dx2], [norm2], [D_MODEL], [(1, D_MODEL)], name="rms2_bwd")
    do = _mm(dx1, w_out, tb=True, name="mm_out_dx")
    g_w_out = _mm(o, dx1, ta=True, name="mm_out_dw")

    def post_bwd(yf_, yb_, r_, v_, kdf_, kdb_, g_, do_, *consts):
        _, vjp = jax.vjp(lambda *a: _postscan_math(*a, consts[4]), yf_, yb_, r_, v_, kdf_, kdb_, g_, *consts[:4])
        dyf_, _, dr_, dv_, dkdf_, dkdb_, dg_, dgnw, dgnb, drkf, drkb = vjp(do_)
        return dyf_, dr_, dv_, dkdf_, dkdb_, dg_, dgnw, dgnb, drkf, drkb

    (dy, dr_c, dv_c, dkdf_c, dkdb_c, dg, d_gn_w, d_gn_b, d_rkf, d_rkb) = _rowwise(
        post_bwd, post_rows + [(do, 0, 512)], post_consts, [D_RWKV] * 6, [(1, D_RWKV)] * 4, name="postscan_bwd")
    dy3 = to3(dy)
    sf = _scan_bwd(r3, to3(wf), to3(kdf), v3, to3(kk), to3(bf), dy3, hist_f, fin_f, reverse=False, name="scan_bwd_f")
    sb = _scan_bwd(r3, to3(wb), to3(kdb), v3, to3(kk), to3(bb), dy3, hist_b, fin_b, reverse=True, name="scan_bwd_b")
    dr_f, dw_f, dkd_f, dv_f, dkk_f, db_f = (a.reshape(t, D_RWKV) for a in sf)
    dr_b, dw_b, dkd_b, dv_b, dkk_b, db_b = (a.reshape(t, D_RWKV) for a in sb)

    def pre_bwd(r_, k_, xw_, xa_, xg_, dr1, dr2, dr3, dv1, dv2, dv3, dkk1, dkk2, dwf_, dwb_, dkdf1, dkdf2,
                dkdb1, dkdb2, dbf_, dbb_, dg_, *consts):
        _, vjp = jax.vjp(lambda *a: _prescan_math(*a, consts[-1]), r_, k_, xw_, xa_, xg_, *consts[:-1])
        cts = (dkk1 + dkk2, dwf_, dwb_, dkdf1 + dkdf2, dkdb1 + dkdb2, dbf_, dbb_, dg_)
        grads = vjp(cts)
        _, dk_, dxw_, dxa_, dxg_ = grads[:5]
        return (dr1 + dr2 + dr3, dk_, dv1 + dv2 + dv3, dxw_, dxa_, dxg_) + tuple(grads[5:])

    pre_b_rows = pre_rows + [dr_c, dr_f, dr_b, dv_c, dv_f, dv_b, dkk_f, dkk_b, dw_f, dw_b, dkdf_c, dkd_f,
                             dkdb_c, dkd_b, db_f, db_b, dg]
    pre_b = _rowwise(pre_bwd, pre_b_rows, pre_consts, [[512, 512, 512, 128, 128, 256]],
                     [(1, D_RWKV)] * 7 + [(128, D_RWKV)] * 4 + [(256, D_RWKV)], name="prescan_bwd")
    d_pss = pre_b[0]
    d_kk_, d_w0f, d_w0b, d_a0f, d_a0b, d_kaf, d_kab, d_wupf, d_wupb, d_aupf, d_aupb, d_gup = pre_b[1:]
    dp, d_mu, d_conv = _shift_conv_bwd(p, d_pss, do, mu, conv_w, seq, name="shift_conv_bwd")
    dh1 = _mm(dp, w_in, tb=True, name="mm_in_dx")
    g_w_in = _mm(h1, dp, ta=True, name="mm_in_dw")
    dx, d_norm1 = _rowwise(rms_bwd, [x2d, dh1, dx1], [norm1], [D_MODEL], [(1, D_MODEL)], name="rms1_bwd")

    grads = {
        "norm1_w": d_norm1, "w_in": _unpad_cols(g_w_in, P_SEGS)[None], "mu_shift": _unpad_cols(d_mu, S_SEGS),
        "w_up_f": d_wupf[None, :D_LORA], "w0_f": d_w0f, "w_up_b": d_wupb[None, :D_LORA], "w0_b": d_w0b,
        "a_up_f": d_aupf[None, :D_LORA], "a0_f": d_a0f, "a_up_b": d_aupb[None, :D_LORA], "a0_b": d_a0b,
        "g_up": d_gup[None, :D_GATE], "k_k": d_kk_, "k_a_f": d_kaf, "k_a_b": d_kab,
        "r_k_f": d_rkf.reshape(1, D_RWKV // HEAD, HEAD), "r_k_b": d_rkb.reshape(1, D_RWKV // HEAD, HEAD),
        "gn_w": d_gn_w, "gn_b": d_gn_b, "conv_w": d_conv[None, :3], "w_out": g_w_out[None],
        "norm2_w": d_norm2, "w_gate": g_w_gate[None], "w_up": g_w_up[None], "w_down": g_w_down[None],
        "norm_f_w": d_normf.reshape(D_MODEL),
    }
    return loss_acc[0, 0], dx.reshape(bsz, seq, D_MODEL), grads


def _all_gather(xs, *, name):
    rws, cols = xs.shape

    def body(x_ref, out_ref, send_sems, recv_sems, local_sem):
        x, y, c = lax.axis_index("x"), lax.axis_index("y"), lax.axis_index("c")
        me, sibling = (x, y, c), (x, y, 1 - c)
        chips = [(1 - x, y), (x, 1 - y), (1 - x, 1 - y)]

        def slot(px, py, pc):
            return out_ref.at[4 * px + 2 * py + pc]

        def copy(k, block, to, src=None):
            return pltpu.make_async_remote_copy(
                src_ref=slot(*block) if src is None else src, dst_ref=slot(*block),
                send_sem=send_sems.at[k], recv_sem=recv_sems.at[k],
                device_id=to, device_id_type=pl.DeviceIdType.MESH)

        mine = pltpu.make_async_copy(x_ref, slot(*me), local_sem)
        mine.start()
        first = [copy(0, me, sibling, src=x_ref)]
        first += [copy(1 + j, me, (*chip, c), src=x_ref) for j, chip in enumerate(chips)]
        for cp in first:
            cp.start()
        passed = [copy(4 + j, (*chip, c), sibling) for j, chip in enumerate(chips)]
        for j, chip in enumerate(chips):
            copy(1 + j, (*chip, c), me).wait_recv()
            passed[j].start()
        copy(0, sibling, me).wait_recv()
        for j, chip in enumerate(chips):
            copy(4 + j, (*chip, 1 - c), me).wait_recv()
        for cp in first + passed:
            cp.wait_send()
        mine.wait()

    return pl.pallas_call(
        body, out_shape=jax.ShapeDtypeStruct((N_DEV, rws, cols), xs.dtype),
        in_specs=[pl.BlockSpec(memory_space=pl.ANY)], out_specs=pl.BlockSpec(memory_space=pl.ANY),
        scratch_shapes=[pltpu.SemaphoreType.DMA((7,)), pltpu.SemaphoreType.DMA((7,)), pltpu.SemaphoreType.DMA],
        name=name)(xs)


def _exchange(send, *, name):
    _, rws, cols = send.shape

    def body(send_ref, recv_ref, send_sems, recv_sems, local_sem):
        x, y, c = lax.axis_index("x"), lax.axis_index("y"), lax.axis_index("c")
        me = 4 * x + 2 * y + c
        local = pltpu.make_async_copy(send_ref.at[me], recv_ref.at[me], local_sem)
        local.start()
        copies = []
        for k in range(1, N_DEV):
            px = 1 - x if k & 4 else x
            py = 1 - y if k & 2 else y
            pc = 1 - c if k & 1 else c
            copies.append(pltpu.make_async_remote_copy(
                src_ref=send_ref.at[4 * px + 2 * py + pc], dst_ref=recv_ref.at[me],
                send_sem=send_sems.at[k - 1], recv_sem=recv_sems.at[k - 1],
                device_id=(px, py, pc), device_id_type=pl.DeviceIdType.MESH))
        for cp in copies:
            cp.start()
        for cp in copies:
            cp.wait()
        local.wait()

    return pl.pallas_call(
        body, out_shape=jax.ShapeDtypeStruct((N_DEV, rws, cols), send.dtype),
        in_specs=[pl.BlockSpec(memory_space=pl.ANY)], out_specs=pl.BlockSpec(memory_space=pl.ANY),
        scratch_shapes=[pltpu.SemaphoreType.DMA((7,)), pltpu.SemaphoreType.DMA((7,)), pltpu.SemaphoreType.DMA],
        name=name)(send)


def _adamw(parts, w, m, v, *, name):
    rws = w.shape[0]
    c1 = 1.0 - ADAM_B1 ** ADAM_STEP
    c2 = 1.0 - ADAM_B2 ** ADAM_STEP

    def kern(p_ref, w_ref, m_ref, v_ref, g_ref, d_ref, nm_ref, nv_ref):
        g = p_ref[0]
        for s in range(1, N_DEV):
            g = g + p_ref[s]
        nm = ADAM_B1 * m_ref[...] + (1.0 - ADAM_B1) * g
        nv = ADAM_B2 * v_ref[...] + (1.0 - ADAM_B2) * (g * g)
        g_ref[...] = g
        nm_ref[...] = nm
        nv_ref[...] = nv
        d_ref[...] = -ADAM_LR * ((nm / c1) / (jnp.sqrt(nv / c2) + ADAM_EPS) + ADAM_WD * w_ref[...])

    spec = pl.BlockSpec((ADAM_TILE, LANES), lambda i: (i, 0))
    return pl.pallas_call(
        kern, out_shape=[jax.ShapeDtypeStruct((rws, LANES), F32)] * 4, grid=(rws // ADAM_TILE,),
        in_specs=[pl.BlockSpec((N_DEV, ADAM_TILE, LANES), lambda i: (0, i, 0)), spec, spec, spec],
        out_specs=[spec] * 4, compiler_params=_params(("parallel",)), name=name)(parts, w, m, v)


def _pack(arrs, dtype, multiple):
    flat = jnp.concatenate([a.reshape(-1).astype(dtype) for a in arrs])
    pad = (-flat.shape[0]) % (multiple * LANES)
    flat = jnp.concatenate([flat, jnp.zeros((pad,), dtype)])
    return flat.reshape(-1, LANES)


def _unpack(packed, shapes):
    flat = packed.reshape(-1)
    out, off = [], 0
    for shp in shapes:
        n = 1
        for d in shp:
            n *= d
        out.append(flat[off:off + n].reshape(shp))
        off += n
    return out


def _gather_full(shards, names, dtype, *, name):
    packed = _pack([shards[n] for n in names], dtype, 16)
    got = _all_gather(packed, name=name)
    shapes = [shards[n].shape for n in names]
    per_dev = [_unpack(got[s], shapes) for s in range(N_DEV)]
    return {n: jnp.concatenate([per_dev[s][i] for s in range(N_DEV)], axis=SHARD_AXIS[n])
            for i, n in enumerate(names)}


def kernel(x, norm1_w, w_in, mu_shift, w_up_f, w0_f, w_up_b, w0_b, a_up_f, a0_f, a_up_b, a0_b, g_up, k_k, k_a_f, k_a_b, r_k_f, r_k_b, gn_w, gn_b, conv_w, w_out, norm2_w, w_gate, w_up, w_down, norm_f_w, loss_target, m_norm1_w, m_w_in, m_mu_shift, m_w_up_f, m_w0_f, m_w_up_b, m_w0_b, m_a_up_f, m_a0_f, m_a_up_b, m_a0_b, m_g_up, m_k_k, m_k_a_f, m_k_a_b, m_r_k_f, m_r_k_b, m_gn_w, m_gn_b, m_conv_w, m_w_out, m_norm2_w, m_w_gate, m_w_up, m_w_down, m_norm_f_w, v_norm1_w, v_w_in, v_mu_shift, v_w_up_f, v_w0_f, v_w_up_b, v_w0_b, v_a_up_f, v_a0_f, v_a_up_b, v_a0_b, v_g_up, v_k_k, v_k_a_f, v_k_a_b, v_r_k_f, v_r_k_b, v_gn_w, v_gn_b, v_conv_w, v_w_out, v_norm2_w, v_w_gate, v_w_up, v_w_down, v_norm_f_w):
    local = dict(norm1_w=norm1_w, w_in=w_in, mu_shift=mu_shift, w_up_f=w_up_f, w0_f=w0_f, w_up_b=w_up_b,
                 w0_b=w0_b, a_up_f=a_up_f, a0_f=a0_f, a_up_b=a_up_b, a0_b=a0_b, g_up=g_up, k_k=k_k, k_a_f=k_a_f,
                 k_a_b=k_a_b, r_k_f=r_k_f, r_k_b=r_k_b, gn_w=gn_w, gn_b=gn_b, conv_w=conv_w, w_out=w_out,
                 norm2_w=norm2_w, w_gate=w_gate, w_up=w_up, w_down=w_down, norm_f_w=norm_f_w)
    mom_m = dict(norm1_w=m_norm1_w, w_in=m_w_in, mu_shift=m_mu_shift, w_up_f=m_w_up_f, w0_f=m_w0_f,
                 w_up_b=m_w_up_b, w0_b=m_w0_b, a_up_f=m_a_up_f, a0_f=m_a0_f, a_up_b=m_a_up_b, a0_b=m_a0_b,
                 g_up=m_g_up, k_k=m_k_k, k_a_f=m_k_a_f, k_a_b=m_k_a_b, r_k_f=m_r_k_f, r_k_b=m_r_k_b,
                 gn_w=m_gn_w, gn_b=m_gn_b, conv_w=m_conv_w, w_out=m_w_out, norm2_w=m_norm2_w, w_gate=m_w_gate,
                 w_up=m_w_up, w_down=m_w_down, norm_f_w=m_norm_f_w)
    mom_v = dict(norm1_w=v_norm1_w, w_in=v_w_in, mu_shift=v_mu_shift, w_up_f=v_w_up_f, w0_f=v_w0_f,
                 w_up_b=v_w_up_b, w0_b=v_w0_b, a_up_f=v_a_up_f, a0_f=v_a0_f, a_up_b=v_a_up_b, a0_b=v_a0_b,
                 g_up=v_g_up, k_k=v_k_k, k_a_f=v_k_a_f, k_a_b=v_k_a_b, r_k_f=v_r_k_f, r_k_b=v_r_k_b,
                 gn_w=v_gn_w, gn_b=v_gn_b, conv_w=v_conv_w, w_out=v_w_out, norm2_w=v_norm2_w, w_gate=v_w_gate,
                 w_up=v_w_up, w_down=v_w_down, norm_f_w=v_norm_f_w)

    full = dict(local)
    full.update(_gather_full(local, BIG, BF16, name="gather_big"))
    full.update(_gather_full(local, SMALL, F32, name="gather_small"))

    loss_part, grad_x, grads = _local_step(x, loss_target, full)
    loss = lax.psum(loss_part, ("x", "y", "c"))

    rep = [grads[n].reshape(local[n].shape) for n in REPLICATED]
    slices = []
    for s in range(N_DEV):
        cut = [jnp.split(grads[n], N_DEV, axis=SHARD_AXIS[n])[s] for n in SHARDED]
        slices.append(_pack(cut + rep, F32, ADAM_TILE))
    got = _exchange(jnp.stack(slices), name="grad_exchange")
    shapes = [local[n].shape for n in PACK_ORDER]
    g_p, d_p, m_p, v_p = _adamw(got, _pack([local[n] for n in PACK_ORDER], F32, ADAM_TILE),
                                _pack([mom_m[n] for n in PACK_ORDER], F32, ADAM_TILE),
                                _pack([mom_v[n] for n in PACK_ORDER], F32, ADAM_TILE), name="adamw")
    outs = []
    for packed in (g_p, d_p, m_p, v_p):
        by_name = dict(zip(PACK_ORDER, _unpack(packed, shapes)))
        outs += [by_name[n] for n in WEIGHTS]
    return (loss, grad_x, *outs)
```

```python
import functools

import jax
import jax.numpy as jnp
from jax import lax
from jax.experimental import pallas as pl
from jax.experimental.pallas import tpu as pltpu

F32 = jnp.float32
BF16 = jnp.bfloat16
HIGHEST = lax.Precision.HIGHEST

N_DEV = 8
D_MODEL = 1024
D_RWKV = 512
D_CONV = 512
HEAD = 64
N_HEAD = D_RWKV // HEAD
D_LORA = 64
D_GATE = 160
D_FF = 2816
D_SHIFTED = 3 * D_RWKV + 2 * D_LORA + D_GATE
D_IN = D_SHIFTED + 3 * D_CONV
XW0, XA0, XG0 = 1536, 1664, 1792
D_SP = 2048
D_INP = D_SP + 3 * D_CONV
LOG_DECAY_SCALE = 0.606531
RMS_EPS = 1e-6
GN_EPS = 64e-5
NORM_EPS = 1e-12
ADAM_LR, ADAM_B1, ADAM_B2, ADAM_EPS, ADAM_WD, ADAM_STEP = 0.001, 0.9, 0.999, 1e-08, 0.01, 10

LANES = 128
SUBLANES = 8
VMEM_LIMIT = 48 * 1024 * 1024
SCAN_CHUNK = 16
ROW_TILE = 128

BIG = ("w_in", "w_out", "w_gate", "w_up", "w_down")
LORA = ("w_up_f", "w_up_b", "a_up_f", "a_up_b", "g_up", "conv_w")
SHARD_AXIS = {"w_in": 2, "w_out": 1, "w_gate": 2, "w_up": 2, "w_down": 1, "w_up_f": 2, "w_up_b": 2,
              "a_up_f": 2, "a_up_b": 2, "g_up": 2, "conv_w": 2}
VEC = ("w0_f", "w0_b", "a0_f", "a0_b", "k_k", "k_a_f", "k_a_b", "r_k_f", "r_k_b", "gn_w", "gn_b")
WIDE = ("mu_shift", "norm1_w", "norm2_w", "norm_f_w")
WIDE_ROW = 2048
WEIGHTS = ("norm1_w", "w_in", "mu_shift", "w_up_f", "w0_f", "w_up_b", "w0_b", "a_up_f", "a0_f", "a_up_b",
           "a0_b", "g_up", "k_k", "k_a_f", "k_a_b", "r_k_f", "r_k_b", "gn_w", "gn_b", "conv_w", "w_out",
           "norm2_w", "w_gate", "w_up", "w_down", "norm_f_w")


def _params(sem, limit=VMEM_LIMIT):
    return pltpu.CompilerParams(dimension_semantics=sem, vmem_limit_bytes=limit)


def _tile(n, cands):
    for c in cands:
        if n % c == 0:
            return c
    raise ValueError(f"no tile for {n}")


def _mm(a, b, *, ta=False, tb=False, add=None, name):
    (k_dim, m) = a.shape if ta else a.shape[::-1]
    (k2, n) = b.shape[::-1] if tb else b.shape
    assert k_dim == k2, (a.shape, b.shape, ta, tb)
    tm = _tile(m, (1024, 512, 256, 128))
    tn = _tile(n, (1408, 1024, 896, 512, 256, 128))
    tk = k_dim if k_dim <= 1024 else _tile(k_dim, (1408, 896, 512, 256, 128))
    nk = k_dim // tk
    dims = (((0 if ta else 1,), (1 if tb else 0,)), ((), ()))

    def kern(*refs):
        if add is None:
            a_ref, b_ref, o_ref, acc_ref = refs
        else:
            a_ref, b_ref, add_ref, o_ref, acc_ref = refs
        k = pl.program_id(2)

        @pl.when(k == 0)
        def _():
            acc_ref[...] = jnp.zeros_like(acc_ref)

        acc_ref[...] += lax.dot_general(a_ref[...].astype(BF16), b_ref[...].astype(BF16), dims,
                                        preferred_element_type=F32)

        @pl.when(k == nk - 1)
        def _():
            if add is None:
                o_ref[...] = acc_ref[...]
            else:
                o_ref[...] = acc_ref[...] + add_ref[...]

    a_spec = (pl.BlockSpec((tk, tm), lambda i, j, k: (k, i)) if ta
              else pl.BlockSpec((tm, tk), lambda i, j, k: (i, k)))
    b_spec = (pl.BlockSpec((tn, tk), lambda i, j, k: (j, k)) if tb
              else pl.BlockSpec((tk, tn), lambda i, j, k: (k, j)))
    o_spec = pl.BlockSpec((tm, tn), lambda i, j, k: (i, j))
    in_specs = [a_spec, b_spec] + ([o_spec] if add is not None else [])
    args = (a, b) + ((add,) if add is not None else ())
    return pl.pallas_call(
        kern, out_shape=jax.ShapeDtypeStruct((m, n), F32), grid=(m // tm, n // tn, nk),
        in_specs=in_specs, out_specs=o_spec, scratch_shapes=[pltpu.VMEM((tm, tn), F32)],
        compiler_params=_params(("parallel", "parallel", "arbitrary")), name=name)(*args)


def _rowwise(fn, rows, consts, out_rows, out_accs, *, name, tb=ROW_TILE):
    t = (rows[0][0] if isinstance(rows[0], tuple) else rows[0]).shape[0]
    n_r, n_c, n_o, n_a = len(rows), len(consts), len(out_rows), len(out_accs)
    pieces = [w if isinstance(w, (list, tuple)) else [w] for w in out_rows]

    def kern(*refs):
        r_refs = refs[:n_r]
        c_refs = refs[n_r:n_r + n_c]
        o_refs = refs[n_r + n_c:n_r + n_c + n_o]
        a_refs = refs[n_r + n_c + n_o:]
        vals = fn(*[r[...] for r in r_refs], *[c[...] for c in c_refs])
        vals = list(vals) if isinstance(vals, (tuple, list)) else [vals]
        pos = 0
        for o_ref, ws in zip(o_refs, pieces):
            off = 0
            for w in ws:
                o_ref[:, off:off + w] = vals[pos]
                off += w
                pos += 1
        if n_a:
            @pl.when(pl.program_id(0) == 0)
            def _():
                for a_ref in a_refs:
                    a_ref[...] = jnp.zeros_like(a_ref)
            for a_ref, v in zip(a_refs, vals[pos:]):
                a_ref[...] += v

    in_specs, args = [], []
    for r in rows:
        if isinstance(r, tuple):
            arr, blk, w = r
            in_specs.append(pl.BlockSpec((tb, w), functools.partial(lambda i, blk: (i, blk), blk=blk)))
        else:
            arr = r
            in_specs.append(pl.BlockSpec((tb, arr.shape[1]), lambda i: (i, 0)))
        args.append(arr)
    for c in consts:
        in_specs.append(pl.BlockSpec(c.shape, lambda i: (0, 0)))
        args.append(c)
    out_shape = [jax.ShapeDtypeStruct((t, sum(ws)), F32) for ws in pieces]
    out_specs = [pl.BlockSpec((tb, sum(ws)), lambda i: (i, 0)) for ws in pieces]
    for shp in out_accs:
        out_shape.append(jax.ShapeDtypeStruct(shp, F32))
        out_specs.append(pl.BlockSpec(shp, lambda i: (0, 0)))
    res = pl.pallas_call(
        kern, out_shape=out_shape, grid=(t // tb,), in_specs=in_specs, out_specs=out_specs,
        compiler_params=_params(("arbitrary",) if n_a else ("parallel",)), name=name)(*args)
    return res


def _rms(x, w):
    return x * lax.rsqrt(jnp.mean(x * x, axis=-1, keepdims=True) + RMS_EPS) * w


def _seg(x, bd):
    return jnp.dot(x, bd, precision=HIGHEST, preferred_element_type=F32)


def _colsum(x):
    return jnp.sum(x, axis=0, keepdims=True)


def _prescan_math(r, k, xw, xa, xg, k_k, w0f, w0b, a0f, a0b, kaf, kab, wupf, wupb, aupf, aupb, gup, bd):
    kkr = k * k_k
    norm = jnp.sqrt(_seg(kkr * kkr, bd))
    kk = kkr / jnp.maximum(norm, NORM_EPS)
    th = jnp.tanh(xw)

    def direction(w0, wup, a0, aup, ka):
        logit = w0 + jnp.dot(th, wup, preferred_element_type=F32)
        w = jnp.exp(-LOG_DECAY_SCALE * jax.nn.sigmoid(logit))
        a = jax.nn.sigmoid(a0 + jnp.dot(xa, aup, preferred_element_type=F32))
        kd = k * (1.0 + (a - 1.0) * ka)
        return w, kd, kk * a

    wf, kdf, bf = direction(w0f, wupf, a0f, aupf, kaf)
    wb, kdb, bb = direction(w0b, wupb, a0b, aupb, kab)
    g = jnp.dot(jax.nn.sigmoid(xg), gup, preferred_element_type=F32)
    return kk, r, wf, wb, bf, bb, kdf, kdb, g


def _postscan_math(y, r, v, kdf, kdb, g, gn_w, gn_b, rkf, rkb, bd):
    mean = _seg(y, bd) * (1.0 / HEAD)
    yc = y - mean
    var = _seg(yc * yc, bd) * (1.0 / HEAD)
    yg = yc * lax.rsqrt(var + GN_EPS) * gn_w + gn_b
    bonus = (_seg(r * kdf * rkf, bd) + _seg(r * kdb * rkb, bd)) * v
    return (yg + bonus) * g


def _halo_specs(width, col_blk, tb, t):
    nb = t // SUBLANES
    step = tb // SUBLANES
    main = pl.BlockSpec((tb, width), lambda i: (i, col_blk))
    prev = pl.BlockSpec((SUBLANES, width), lambda i: (jnp.maximum(i * step - 1, 0), col_blk))
    nxt = pl.BlockSpec((SUBLANES, width), lambda i: (jnp.minimum((i + 1) * step, nb - 1), col_blk))
    return [main, prev, nxt]


def _neighbours(z, prev8, next8, first, last):
    tb = z.shape[0]
    row = lax.broadcasted_iota(jnp.int32, z.shape, 0)
    prow = jnp.where(first, 0.0, prev8[SUBLANES - 1:SUBLANES, :])
    nrow = jnp.where(last, 0.0, next8[0:1, :])
    down = jnp.where(row == 0, prow, pltpu.roll(z, 1, 0))
    up = jnp.where(row == tb - 1, nrow, pltpu.roll(z, tb - 1, 0))
    return down, up


def _shift_conv_fwd(p, mu, conv_w, seq, *, name, tb=ROW_TILE):
    t = p.shape[0]
    per_seq = seq // tb

    def kern(p_ref, pp_ref, pn_ref, mu_ref, cw_ref, pss_ref, oc_ref):
        i = pl.program_id(0)
        first = (i % per_seq) == 0
        last = (i % per_seq) == per_seq - 1
        ps = p_ref[:, :D_SP]
        down, up = _neighbours(ps, pp_ref[:, :D_SP], pn_ref[:, :D_SP], first, last)
        pss_ref[...] = ps + mu_ref[...] * (0.5 * (down + up) - ps)
        gb = p_ref[:, D_SP:D_SP + D_CONV]
        u = p_ref[:, D_SP + D_CONV:D_SP + 2 * D_CONV] * p_ref[:, D_SP + 2 * D_CONV:]
        u_p = pp_ref[:, D_SP + D_CONV:D_SP + 2 * D_CONV] * pp_ref[:, D_SP + 2 * D_CONV:]
        u_n = pn_ref[:, D_SP + D_CONV:D_SP + 2 * D_CONV] * pn_ref[:, D_SP + 2 * D_CONV:]
        udown, uup = _neighbours(u, u_p, u_n, first, last)
        oc_ref[...] = gb * (cw_ref[0:1, :] * udown + cw_ref[1:2, :] * u + cw_ref[2:3, :] * uup)

    return pl.pallas_call(
        kern,
        out_shape=[jax.ShapeDtypeStruct((t, D_SP), F32), jax.ShapeDtypeStruct((t, D_CONV), F32)],
        grid=(t // tb,),
        in_specs=_halo_specs(D_INP, 0, tb, t) + [pl.BlockSpec((1, D_SP), lambda i: (0, 0)),
                                                 pl.BlockSpec((SUBLANES, D_CONV), lambda i: (0, 0))],
        out_specs=[pl.BlockSpec((tb, D_SP), lambda i: (i, 0)), pl.BlockSpec((tb, D_CONV), lambda i: (i, 0))],
        compiler_params=_params(("parallel",)), name=name)(p, p, p, mu, conv_w)


def _shift_conv_bwd(p, d_pss, d_o, mu, conv_w, seq, *, name, tb=ROW_TILE):
    t = p.shape[0]
    per_seq = seq // tb

    def kern(p_ref, pp_ref, pn_ref, d_ref, dp_ref, dn_ref, do_ref, dop_ref, don_ref, mu_ref, cw_ref,
             out_ref, dmu_ref, dcw_ref):
        i = pl.program_id(0)
        first = (i % per_seq) == 0
        last = (i % per_seq) == per_seq - 1

        @pl.when(i == 0)
        def _():
            dmu_ref[...] = jnp.zeros_like(dmu_ref)
            dcw_ref[...] = jnp.zeros_like(dcw_ref)

        mu_v = mu_ref[...]
        ps = p_ref[:, :D_SP]
        down, up = _neighbours(ps, pp_ref[:, :D_SP], pn_ref[:, :D_SP], first, last)
        d = d_ref[...]
        ddown, dup = _neighbours(d, dp_ref[...], dn_ref[...], first, last)
        out_ref[:, :D_SP] = d - mu_v * d + 0.5 * (mu_v * ddown + mu_v * dup)
        dmu_ref[...] += _colsum(d * (0.5 * (down + up) - ps))

        def parts(ref):
            return (ref[:, D_SP:D_SP + D_CONV], ref[:, D_SP + D_CONV:D_SP + 2 * D_CONV],
                    ref[:, D_SP + 2 * D_CONV:])

        gb, gc, hh = parts(p_ref)
        gb_p, gc_p, hh_p = parts(pp_ref)
        gb_n, gc_n, hh_n = parts(pn_ref)
        u = gc * hh
        udown, uup = _neighbours(u, gc_p * hh_p, gc_n * hh_n, first, last)
        cw0, cw1, cw2 = cw_ref[0:1, :], cw_ref[1:2, :], cw_ref[2:3, :]
        do = do_ref[...]
        duc = do * gb
        ducdown, ducup = _neighbours(duc, dop_ref[...] * gb_p, don_ref[...] * gb_n, first, last)
        du = cw0 * ducup + cw1 * duc + cw2 * ducdown
        out_ref[:, D_SP:D_SP + D_CONV] = do * (cw0 * udown + cw1 * u + cw2 * uup)
        out_ref[:, D_SP + D_CONV:D_SP + 2 * D_CONV] = du * hh
        out_ref[:, D_SP + 2 * D_CONV:] = du * gc
        dcw_ref[0:1, :] += _colsum(duc * udown)
        dcw_ref[1:2, :] += _colsum(duc * u)
        dcw_ref[2:3, :] += _colsum(duc * uup)

    return pl.pallas_call(
        kern,
        out_shape=[jax.ShapeDtypeStruct((t, D_INP), F32), jax.ShapeDtypeStruct((1, D_SP), F32),
                   jax.ShapeDtypeStruct((SUBLANES, D_CONV), F32)],
        grid=(t // tb,),
        in_specs=(_halo_specs(D_INP, 0, tb, t) + _halo_specs(D_SP, 0, tb, t) + _halo_specs(D_CONV, 1, tb, t)
                  + [pl.BlockSpec((1, D_SP), lambda i: (0, 0)),
                     pl.BlockSpec((SUBLANES, D_CONV), lambda i: (0, 0))]),
        out_specs=[pl.BlockSpec((tb, D_INP), lambda i: (i, 0)), pl.BlockSpec((1, D_SP), lambda i: (0, 0)),
                   pl.BlockSpec((SUBLANES, D_CONV), lambda i: (0, 0))],
        compiler_params=_params(("arbitrary",)), name=name)(p, p, p, d_pss, d_pss, d_pss, d_o, d_o, d_o, mu, conv_w)


N_CHAIN = 16
V_LO = LANES // N_CHAIN
V_HI = HEAD // V_LO
N_GROUP = LANES // N_CHAIN
G_KK, G_R, G_W, G_B, G_KD = 0, 1, (2, 3), (4, 5), (6, 7)


def _group(x, j, lane):
    g = pltpu.roll(x, (LANES - N_CHAIN * j) % LANES, 1) if j else x
    g = jnp.where(lane < N_CHAIN, g, pltpu.roll(g, N_CHAIN, 1))
    g = jnp.where(lane < 2 * N_CHAIN, g, pltpu.roll(g, 2 * N_CHAIN, 1))
    return jnp.where(lane < 4 * N_CHAIN, g, pltpu.roll(g, 4 * N_CHAIN, 1))


def _scan_inputs(x, d, lane):
    return [_group(x, j, lane) for j in (G_KK, G_R, G_W[d], G_B[d], G_KD[d])]


def _lane_scan_fwd(xall, v_l, *, name):
    steps = xall.shape[0]
    nc = steps // SCAN_CHUNK
    mirror = lambda c: nc - 1 - c

    def kern(xf_ref, xb_ref, vf_ref, vb_ref, yf_ref, yb_ref, hist_ref, fin_ref, st_ref):
        c = pl.program_id(0)

        @pl.when(c == 0)
        def _():
            st_ref[...] = jnp.zeros_like(st_ref)

        row = lax.broadcasted_iota(jnp.int32, (V_HI, LANES), 0)
        lane = lax.broadcasted_iota(jnp.int32, (HEAD, LANES), 1)

        def step(i, carry):
            j = SCAN_CHUNK - 1 - i
            for d, (x_t, v_t, y_ref, at) in enumerate(((xf_ref[i], vf_ref[i], yf_ref, i),
                                                       (xb_ref[j], vb_ref[j], yb_ref, j))):
                kk_t, r_t, w_t, b_t, kd_t = _scan_inputs(x_t, d, lane)
                y_t = jnp.zeros((V_HI, LANES), F32)
                for vh in range(V_HI):
                    tile = d * V_HI + vh
                    state = st_ref[tile]
                    hist_ref[i, tile] = state
                    sa = _colsum(state * kk_t)
                    state = state * w_t - sa * b_t + v_t[vh:vh + 1, :] * kd_t
                    st_ref[tile] = state
                    y_t = jnp.where(row == vh, _colsum(state * r_t), y_t)
                y_ref[at] = y_t
            return carry

        lax.fori_loop(0, SCAN_CHUNK, step, 0)

        @pl.when(c == nc - 1)
        def _():
            fin_ref[...] = st_ref[...]

    def k_spec(fn):
        return pl.BlockSpec((SCAN_CHUNK, HEAD, LANES), lambda c: (fn(c), 0, 0))

    def v_spec(fn):
        return pl.BlockSpec((SCAN_CHUNK, V_HI, LANES), lambda c: (fn(c), 0, 0))

    same = lambda c: c
    st_shape = (2 * V_HI, HEAD, LANES)
    return pl.pallas_call(
        kern,
        out_shape=[jax.ShapeDtypeStruct((steps, V_HI, LANES), F32)] * 2
        + [jax.ShapeDtypeStruct((steps,) + st_shape, F32), jax.ShapeDtypeStruct(st_shape, F32)],
        grid=(nc,), in_specs=[k_spec(same), k_spec(mirror), v_spec(same), v_spec(mirror)],
        out_specs=[v_spec(same), v_spec(mirror),
                   pl.BlockSpec((SCAN_CHUNK,) + st_shape, lambda c: (c, 0, 0, 0)),
                   pl.BlockSpec(st_shape, lambda c: (0, 0, 0))],
        scratch_shapes=[pltpu.VMEM(st_shape, F32)],
        compiler_params=_params(("arbitrary",)), name=name)(xall, xall, v_l, v_l)


def _lane_scan_bwd(xall, v_l, dy_l, hist, fin, *, name):
    steps = xall.shape[0]
    nc = steps // SCAN_CHUNK
    back = lambda c: nc - 1 - c
    same = lambda c: c

    def kern(xf_ref, xb_ref, vf_ref, vb_ref, dyf_ref, dyb_ref, hist_ref, fin_ref,
             gf_ref, gb_ref, dvf_ref, dvb_ref, ds_ref, after_ref):
        c = pl.program_id(0)

        @pl.when(c == 0)
        def _():
            ds_ref[...] = jnp.zeros_like(ds_ref)
            after_ref[...] = fin_ref[...]

        row = lax.broadcasted_iota(jnp.int32, (V_HI, LANES), 0)
        lane = lax.broadcasted_iota(jnp.int32, (HEAD, LANES), 1)
        grp = lax.shift_right_logical(lane, jnp.full_like(lane, 4))

        def group_sum(x):
            x = x + pltpu.roll(x, 4 * N_CHAIN, 1)
            x = x + pltpu.roll(x, 2 * N_CHAIN, 1)
            return x + pltpu.roll(x, N_CHAIN, 1)

        def step(ii, carry):
            i = SCAN_CHUNK - 1 - ii
            j = ii
            for d, (x_t, v_t, dy_t, g_ref, dv_ref, at) in enumerate((
                    (xf_ref[i], vf_ref[i], dyf_ref[i], gf_ref, dvf_ref, i),
                    (xb_ref[j], vb_ref[j], dyb_ref[j], gb_ref, dvb_ref, j))):
                kk_t, r_t, w_t, b_t, kd_t = _scan_inputs(x_t, d, lane)
                dv_t = jnp.zeros((V_HI, LANES), F32)
                zero = jnp.zeros((HEAD, LANES), F32)
                dkk, dr, dw, db, dkd = zero, zero, zero, zero, zero
                for vh in range(V_HI):
                    tile = d * V_HI + vh
                    before = hist_ref[i, tile]
                    dy_r, v_r = dy_t[vh:vh + 1, :], v_t[vh:vh + 1, :]
                    g = ds_ref[tile] + dy_r * r_t
                    sa = _colsum(before * kk_t)
                    dsa = -_colsum(g * b_t)
                    dv_t = jnp.where(row == vh, _colsum(g * kd_t), dv_t)
                    dr = dr + after_ref[tile] * dy_r
                    dw = dw + g * before
                    dkd = dkd + g * v_r
                    db = db - g * sa
                    dkk = dkk + before * dsa
                    ds_ref[tile] = g * w_t + dsa * kk_t
                    after_ref[tile] = before
                out = jnp.where(grp == G_KK, group_sum(dkk), 0.0)
                out = jnp.where(grp == G_R, group_sum(dr), out)
                out = jnp.where(grp == G_W[d], group_sum(dw), out)
                out = jnp.where(grp == G_B[d], group_sum(db), out)
                out = jnp.where(grp == G_KD[d], group_sum(dkd), out)
                g_ref[at] = out
                dv_ref[at] = dv_t
            return carry

        lax.fori_loop(0, SCAN_CHUNK, step, 0)

    def k_spec(fn):
        return pl.BlockSpec((SCAN_CHUNK, HEAD, LANES), lambda c: (fn(c), 0, 0))

    def v_spec(fn):
        return pl.BlockSpec((SCAN_CHUNK, V_HI, LANES), lambda c: (fn(c), 0, 0))

    st_shape = (2 * V_HI, HEAD, LANES)
    return pl.pallas_call(
        kern,
        out_shape=[jax.ShapeDtypeStruct((steps, HEAD, LANES), F32)] * 2
        + [jax.ShapeDtypeStruct((steps, V_HI, LANES), F32)] * 2,
        grid=(nc,),
        in_specs=[k_spec(back), k_spec(same), v_spec(back), v_spec(same), v_spec(back), v_spec(same),
                  pl.BlockSpec((SCAN_CHUNK,) + st_shape, lambda c: (back(c), 0, 0, 0)),
                  pl.BlockSpec(st_shape, lambda c: (0, 0, 0))],
        out_specs=[k_spec(back), k_spec(same), v_spec(back), v_spec(same)],
        scratch_shapes=[pltpu.VMEM(st_shape, F32), pltpu.VMEM(st_shape, F32)],
        compiler_params=_params(("arbitrary",)), name=name)(xall, xall, v_l, v_l, dy_l, dy_l, hist, fin)


def _to_key_lanes(wide, bsz, seq):
    z = wide.reshape(bsz, seq, N_GROUP, N_HEAD, HEAD).transpose(1, 4, 2, 0, 3)
    return z.reshape(seq, HEAD, LANES)


def _from_key_lanes(g, bsz, seq):
    z = g.reshape(seq, HEAD, N_GROUP, bsz, N_HEAD).transpose(3, 0, 2, 4, 1)
    return z.reshape(bsz * seq, N_GROUP * D_RWKV)


def _to_value_lanes(a, bsz, seq):
    z = a.reshape(bsz, seq, N_HEAD, V_HI, V_LO).transpose(1, 3, 4, 0, 2)
    return z.reshape(seq, V_HI, LANES)


def _from_value_lanes(y, bsz, seq):
    z = y.reshape(seq, V_HI, V_LO, bsz, N_HEAD).transpose(3, 0, 4, 1, 2)
    return z.reshape(bsz * seq, D_RWKV)


def _pad_cols(a, segs):
    out, off = [], 0
    for w, wp in segs:
        out.append(a[..., off:off + w])
        if wp > w:
            out.append(jnp.zeros(a.shape[:-1] + (wp - w,), a.dtype))
        off += w
    return jnp.concatenate(out, axis=-1)


def _unpad_cols(a, segs):
    out, off = [], 0
    for w, wp in segs:
        out.append(a[..., off:off + w])
        off += wp
    return jnp.concatenate(out, axis=-1)


P_SEGS = ((3 * D_RWKV, 3 * D_RWKV), (D_LORA, 128), (D_LORA, 128), (D_GATE, 256), (3 * D_CONV, 3 * D_CONV))
S_SEGS = P_SEGS[:4]


def _pad_rows(a, rows):
    return jnp.concatenate([a, jnp.zeros((rows - a.shape[0], a.shape[1]), a.dtype)], axis=0)


def _local_step(x, target, w):
    bsz, seq, _ = x.shape
    t = bsz * seq
    x2d = x.reshape(t, D_MODEL)
    tg2d = target.reshape(t, D_MODEL)
    row = lambda a: a.reshape(1, -1).astype(F32)

    w_in = _pad_cols(w["w_in"][0], P_SEGS)
    mu = _pad_cols(row(w["mu_shift"]), S_SEGS)
    wupf, wupb, aupf, aupb = (_pad_rows(w[n][0].astype(F32), 128) for n in ("w_up_f", "w_up_b", "a_up_f", "a_up_b"))
    gup = _pad_rows(w["g_up"][0].astype(F32), 256)
    conv_w = _pad_rows(w["conv_w"][0].astype(F32), SUBLANES)
    w_out, w_gate, w_up, w_down = w["w_out"][0], w["w_gate"][0], w["w_up"][0], w["w_down"][0]
    norm1, norm2, normf = row(w["norm1_w"]), row(w["norm2_w"]), row(w["norm_f_w"])
    vec = {n: row(w[n]) for n in VEC}
    head_of = jnp.arange(D_RWKV) // HEAD
    bd = (head_of[:, None] == head_of[None, :]).astype(F32)
    pre_consts = [vec["k_k"], vec["w0_f"], vec["w0_b"], vec["a0_f"], vec["a0_b"], vec["k_a_f"], vec["k_a_b"],
                  wupf, wupb, aupf, aupb, gup, bd]
    post_consts = [vec["gn_w"], vec["gn_b"], vec["r_k_f"], vec["r_k_b"], bd]

    h1, = _rowwise(_rms, [x2d], [norm1], [D_MODEL], [], name="rms1_fwd")
    p = _mm(h1, w_in, name="mm_in")
    pss, oconv = _shift_conv_fwd(p, mu, conv_w, seq, name="shift_conv_fwd")
    pre_rows = [(pss, 0, 512), (pss, 1, 512), (pss, XW0 // 128, 128), (pss, XA0 // 128, 128), (pss, XG0 // 256, 256)]
    sc, g = _rowwise(_prescan_math, pre_rows, pre_consts, [[D_RWKV] * N_GROUP, D_RWKV], [], name="prescan_fwd")
    xall = _to_key_lanes(sc, bsz, seq)
    v_l = _to_value_lanes(pss[:, 2 * D_RWKV:3 * D_RWKV], bsz, seq)
    y_f, y_b, hist, fin = _lane_scan_fwd(xall, v_l, name="scan_fwd")
    y = _from_value_lanes(y_f + y_b, bsz, seq)
    post_rows = [y, (pss, 0, 512), (pss, 2, 512), (sc, G_KD[0], 512), (sc, G_KD[1], 512), g]

    def post_fwd(y_, r_, v_, kdf_, kdb_, g_, oc_, *consts):
        return _postscan_math(y_, r_, v_, kdf_, kdb_, g_, *consts), oc_

    o, = _rowwise(post_fwd, post_rows + [oconv], post_consts, [[D_RWKV, D_CONV]], [], name="postscan_fwd")
    x1 = _mm(o, w_out, add=x2d, name="mm_out")
    h2, = _rowwise(_rms, [x1], [norm2], [D_MODEL], [], name="rms2_fwd")
    gg = _mm(h2, w_gate, name="mm_gate")
    uu = _mm(h2, w_up, name="mm_up")
    ff, = _rowwise(lambda a, c: jax.nn.silu(a) * c, [gg, uu], [], [D_FF], [], name="swiglu_fwd")
    x2 = _mm(ff, w_down, add=x1, name="mm_down")

    def final(x_, tg_, wn_):
        yo, vjp = jax.vjp(_rms, x_, wn_)
        err = yo - tg_
        dx_, dwn_ = vjp(err * (1.0 / D_MODEL))
        part = jnp.sum(jnp.sum(err * err, axis=1, keepdims=True), axis=0, keepdims=True) * (0.5 / D_MODEL)
        return dx_, part + jnp.zeros((1, LANES), F32), dwn_

    dx2, loss_acc, d_normf = _rowwise(final, [x2, tg2d], [normf], [D_MODEL], [(1, LANES), (1, D_MODEL)],
                                      name="loss_head")
    dff = _mm(dx2, w_down, tb=True, name="mm_down_dx")
    g_w_down = _mm(ff, dx2, ta=True, name="mm_down_dw")

    def swiglu_bwd(a, c, d):
        _, vjp = jax.vjp(lambda a_, c_: jax.nn.silu(a_) * c_, a, c)
        return vjp(d)

    dgg, duu = _rowwise(swiglu_bwd, [gg, uu, dff], [], [D_FF, D_FF], [], name="swiglu_bwd")
    dh2 = _mm(dgg, w_gate, tb=True, name="mm_gate_dx")
    dh2 = _mm(duu, w_up, tb=True, add=dh2, name="mm_up_dx")
    g_w_gate = _mm(h2, dgg, ta=True, name="mm_gate_dw")
    g_w_up = _mm(h2, duu, ta=True, name="mm_up_dw")

    def rms_bwd(x_, dh_, dres_, wn_):
        _, vjp = jax.vjp(_rms, x_, wn_)
        dx_, dwn_ = vjp(dh_)
        return dx_ + dres_, dwn_

    dx1, d_norm2 = _rowwise(rms_bwd, [x1, dh2, dx2], [norm2], [D_MODEL], [(1, D_MODEL)], name="rms2_bwd")
    do = _mm(dx1, w_out, tb=True, name="mm_out_dx")
    g_w_out = _mm(o, dx1, ta=True, name="mm_out_dw")

    def post_bwd(y_, r_, v_, kdf_, kdb_, g_, do_, *consts):
        _, vjp = jax.vjp(lambda *a: _postscan_math(*a, consts[4]), y_, r_, v_, kdf_, kdb_, g_, *consts[:4])
        return vjp(do_)

    (dy, dr_c, dv_c, dkdf_c, dkdb_c, dg, d_gn_w, d_gn_b, d_rkf, d_rkb) = _rowwise(
        post_bwd, post_rows + [(do, 0, 512)], post_consts, [D_RWKV] * 6, [(1, D_RWKV)] * 4, name="postscan_bwd")
    dy_l = _to_value_lanes(dy, bsz, seq)
    g_f, g_b, dv_f, dv_b = _lane_scan_bwd(xall, v_l, dy_l, hist, fin, name="scan_bwd")
    dsc = _from_key_lanes(g_f + g_b, bsz, seq)
    dv_s = _from_value_lanes(dv_f + dv_b, bsz, seq)

    def pre_bwd(r_, k_, xw_, xa_, xg_, dkk_, dr_s, dwf_, dwb_, dbf_, dbb_, dkdf_s, dkdb_s,
                dr_c_, dv_c_, dv_s_, dkdf_c_, dkdb_c_, dg_, *consts):
        _, vjp = jax.vjp(lambda *a: _prescan_math(*a, consts[-1]), r_, k_, xw_, xa_, xg_, *consts[:-1])
        grads = vjp((dkk_, dr_s + dr_c_, dwf_, dwb_, dbf_, dbb_, dkdf_s + dkdf_c_, dkdb_s + dkdb_c_, dg_))
        dr_, dk_, dxw_, dxa_, dxg_ = grads[:5]
        return (dr_, dk_, dv_c_ + dv_s_, dxw_, dxa_, dxg_) + tuple(grads[5:])

    pre_b_rows = (pre_rows + [(dsc, j, 512) for j in range(N_GROUP)]
                  + [dr_c, dv_c, dv_s, dkdf_c, dkdb_c, dg])
    pre_b = _rowwise(pre_bwd, pre_b_rows, pre_consts, [[512, 512, 512, 128, 128, 256]],
                     [(1, D_RWKV)] * 7 + [(128, D_RWKV)] * 4 + [(256, D_RWKV)], name="prescan_bwd")
    d_pss = pre_b[0]
    d_kk_, d_w0f, d_w0b, d_a0f, d_a0b, d_kaf, d_kab, d_wupf, d_wupb, d_aupf, d_aupb, d_gup = pre_b[1:]
    dp, d_mu, d_conv = _shift_conv_bwd(p, d_pss, do, mu, conv_w, seq, name="shift_conv_bwd")
    dh1 = _mm(dp, w_in, tb=True, name="mm_in_dx")
    g_w_in = _mm(h1, dp, ta=True, name="mm_in_dw")
    dx, d_norm1 = _rowwise(rms_bwd, [x2d, dh1, dx1], [norm1], [D_MODEL], [(1, D_MODEL)], name="rms1_bwd")

    grads = {
        "norm1_w": d_norm1, "w_in": _unpad_cols(g_w_in, P_SEGS)[None], "mu_shift": _unpad_cols(d_mu, S_SEGS),
        "w_up_f": d_wupf[None, :D_LORA], "w0_f": d_w0f, "w_up_b": d_wupb[None, :D_LORA], "w0_b": d_w0b,
        "a_up_f": d_aupf[None, :D_LORA], "a0_f": d_a0f, "a_up_b": d_aupb[None, :D_LORA], "a0_b": d_a0b,
        "g_up": d_gup[None, :D_GATE], "k_k": d_kk_, "k_a_f": d_kaf, "k_a_b": d_kab,
        "r_k_f": d_rkf, "r_k_b": d_rkb, "gn_w": d_gn_w, "gn_b": d_gn_b, "conv_w": d_conv[None, :3],
        "w_out": g_w_out[None], "norm2_w": d_norm2, "w_gate": g_w_gate[None], "w_up": g_w_up[None],
        "w_down": g_w_down[None], "norm_f_w": d_normf,
    }
    return loss_acc[0, 0], dx.reshape(bsz, seq, D_MODEL), grads


def _hbm_specs(n):
    return [pl.BlockSpec(memory_space=pl.ANY)] * n


def _all_gather(arrs, *, name):
    n = len(arrs)

    def body(*refs):
        x_refs, out_refs = refs[:n], refs[n:2 * n]
        send_sems, recv_sems, local_sems = refs[2 * n:]
        x, y, c = lax.axis_index("x"), lax.axis_index("y"), lax.axis_index("c")
        me, sibling = (x, y, c), (x, y, 1 - c)
        chips = [(1 - x, y), (x, 1 - y), (1 - x, 1 - y)]

        def slot(a, px, py, pc):
            return out_refs[a].at[4 * px + 2 * py + pc]

        def copy(a, k, block, to, src=None):
            return pltpu.make_async_remote_copy(
                src_ref=slot(a, *block) if src is None else src, dst_ref=slot(a, *block),
                send_sem=send_sems.at[k, a], recv_sem=recv_sems.at[k, a],
                device_id=to, device_id_type=pl.DeviceIdType.MESH)

        mine = [pltpu.make_async_copy(x_refs[a], slot(a, *me), local_sems.at[a]) for a in range(n)]
        for cp in mine:
            cp.start()
        first = []
        for a in range(n):
            first.append(copy(a, 0, me, sibling, src=x_refs[a]))
            first += [copy(a, 1 + j, me, (*chip, c), src=x_refs[a]) for j, chip in enumerate(chips)]
        for cp in first:
            cp.start()
        passed = []
        for j, chip in enumerate(chips):
            for a in range(n):
                copy(a, 1 + j, (*chip, c), me).wait_recv()
                cp = copy(a, 4 + j, (*chip, c), sibling)
                cp.start()
                passed.append(cp)
        for a in range(n):
            copy(a, 0, sibling, me).wait_recv()
            for j, chip in enumerate(chips):
                copy(a, 4 + j, (*chip, 1 - c), me).wait_recv()
        for cp in first + passed:
            cp.wait_send()
        for cp in mine:
            cp.wait()

    return pl.pallas_call(
        body, out_shape=[jax.ShapeDtypeStruct((N_DEV,) + a.shape, a.dtype) for a in arrs],
        in_specs=_hbm_specs(n), out_specs=_hbm_specs(n),
        scratch_shapes=[pltpu.SemaphoreType.DMA((7, n)), pltpu.SemaphoreType.DMA((7, n)),
                        pltpu.SemaphoreType.DMA((n,))],
        name=name)(*arrs)


def _exchange(sliced, whole, *, name):
    arrs = list(sliced) + list(whole)
    n, n_sliced = len(arrs), len(sliced)

    def body(*refs):
        in_refs, out_refs = refs[:n], refs[n:2 * n]
        send_sems, recv_sems, local_sems = refs[2 * n:]
        x, y, c = lax.axis_index("x"), lax.axis_index("y"), lax.axis_index("c")
        me = 4 * x + 2 * y + c

        def src(a, dev):
            return in_refs[a].at[dev] if a < n_sliced else in_refs[a]

        local = [pltpu.make_async_copy(src(a, me), out_refs[a].at[me], local_sems.at[a]) for a in range(n)]
        for cp in local:
            cp.start()
        copies = []
        for k in range(1, N_DEV):
            px = 1 - x if k & 4 else x
            py = 1 - y if k & 2 else y
            pc = 1 - c if k & 1 else c
            for a in range(n):
                copies.append(pltpu.make_async_remote_copy(
                    src_ref=src(a, 4 * px + 2 * py + pc), dst_ref=out_refs[a].at[me],
                    send_sem=send_sems.at[k - 1, a], recv_sem=recv_sems.at[k - 1, a],
                    device_id=(px, py, pc), device_id_type=pl.DeviceIdType.MESH))
        for cp in copies:
            cp.start()
        for cp in copies:
            cp.wait()
        for cp in local:
            cp.wait()

    out_shape = [jax.ShapeDtypeStruct(a.shape if i < n_sliced else (N_DEV,) + a.shape, a.dtype)
                 for i, a in enumerate(arrs)]
    return pl.pallas_call(
        body, out_shape=out_shape, in_specs=_hbm_specs(n), out_specs=_hbm_specs(n),
        scratch_shapes=[pltpu.SemaphoreType.DMA((7, n)), pltpu.SemaphoreType.DMA((7, n)),
                        pltpu.SemaphoreType.DMA((n,))],
        name=name)(*arrs)


def _adam_math(g, w, m, v):
    nm = ADAM_B1 * m + (1.0 - ADAM_B1) * g
    nv = ADAM_B2 * v + (1.0 - ADAM_B2) * (g * g)
    m_hat = nm / (1.0 - ADAM_B1 ** ADAM_STEP)
    v_hat = nv / (1.0 - ADAM_B2 ** ADAM_STEP)
    return -ADAM_LR * (m_hat / (jnp.sqrt(v_hat) + ADAM_EPS) + ADAM_WD * w), nm, nv


def _slot_sum(ref):
    g = ref[0]
    for s in range(1, N_DEV):
        g = g + ref[s]
    return g


def _adamw_big(parts, w, m, v, *, name):
    _, rws, cols = w.shape
    tr = _tile(rws, (256, 176, 128))

    def kern(p_ref, w_ref, m_ref, v_ref, g_ref, d_ref, nm_ref, nv_ref):
        g = _slot_sum(p_ref)
        g_ref[...] = g
        d_ref[...], nm_ref[...], nv_ref[...] = _adam_math(g, w_ref[...], m_ref[...], v_ref[...])

    spec = pl.BlockSpec((1, tr, cols), lambda i: (0, i, 0))
    return pl.pallas_call(
        kern, out_shape=[jax.ShapeDtypeStruct(w.shape, F32)] * 4, grid=(rws // tr,),
        in_specs=[pl.BlockSpec((N_DEV, 1, tr, cols), lambda i: (0, 0, i, 0)), spec, spec, spec],
        out_specs=[spec] * 4, compiler_params=_params(("parallel",)), name=name)(parts, w, m, v)


def _adamw_small(lora_parts, vec_parts, wide_parts, wmv, *, name):
    names = LORA + VEC + WIDE
    n_l, n = len(LORA), len(names)
    flat = [a for trip in wmv for a in trip]

    def kern(*refs):
        l_refs, vec_ref, wide_ref = refs[:n_l], refs[n_l], refs[n_l + 1]
        in_refs = refs[n_l + 2:n_l + 2 + 3 * n]
        out_refs = refs[n_l + 2 + 3 * n:]
        vec_sum, wide_sum = _slot_sum(vec_ref), _slot_sum(wide_ref)
        for i, nm in enumerate(names):
            w_ref, m_ref, v_ref = in_refs[3 * i:3 * i + 3]
            if i < n_l:
                g = _slot_sum(l_refs[i])
            elif nm in VEC:
                g = vec_sum[i - n_l:i - n_l + 1, :]
            else:
                g = wide_sum[WIDE.index(nm):WIDE.index(nm) + 1, :w_ref.shape[-1]]
            o = out_refs[4 * i:4 * i + 4]
            o[0][...] = g
            o[1][...], o[2][...], o[3][...] = _adam_math(g, w_ref[...], m_ref[...], v_ref[...])

    out_shape = [jax.ShapeDtypeStruct(trip[0].shape, F32) for trip in wmv for _ in range(4)]
    outs = pl.pallas_call(kern, out_shape=out_shape, name=name,
                          compiler_params=pltpu.CompilerParams(vmem_limit_bytes=VMEM_LIMIT))(
        *lora_parts, vec_parts, wide_parts, *flat)
    return [tuple(outs[4 * i:4 * i + 4]) for i in range(n)]


def _to_slots(g, axis):
    _, rws, cols = g.shape
    if axis == 1:
        return g.reshape(N_DEV, 1, rws // N_DEV, cols)
    return g.reshape(1, rws, N_DEV, cols // N_DEV).transpose(2, 0, 1, 3)


def _from_slots(got, axis):
    _, _, rws, cols = got.shape
    if axis == 1:
        return got.reshape(1, N_DEV * rws, cols)
    return got.transpose(1, 2, 0, 3).reshape(1, rws, N_DEV * cols)


def _pad_lanes(a, width):
    return jnp.concatenate([a, jnp.zeros(a.shape[:-1] + (width - a.shape[-1],), a.dtype)], axis=-1)


def kernel(x, norm1_w, w_in, mu_shift, w_up_f, w0_f, w_up_b, w0_b, a_up_f, a0_f, a_up_b, a0_b, g_up, k_k, k_a_f, k_a_b, r_k_f, r_k_b, gn_w, gn_b, conv_w, w_out, norm2_w, w_gate, w_up, w_down, norm_f_w, loss_target, m_norm1_w, m_w_in, m_mu_shift, m_w_up_f, m_w0_f, m_w_up_b, m_w0_b, m_a_up_f, m_a0_f, m_a_up_b, m_a0_b, m_g_up, m_k_k, m_k_a_f, m_k_a_b, m_r_k_f, m_r_k_b, m_gn_w, m_gn_b, m_conv_w, m_w_out, m_norm2_w, m_w_gate, m_w_up, m_w_down, m_norm_f_w, v_norm1_w, v_w_in, v_mu_shift, v_w_up_f, v_w0_f, v_w_up_b, v_w0_b, v_a_up_f, v_a0_f, v_a_up_b, v_a0_b, v_g_up, v_k_k, v_k_a_f, v_k_a_b, v_r_k_f, v_r_k_b, v_gn_w, v_gn_b, v_conv_w, v_w_out, v_norm2_w, v_w_gate, v_w_up, v_w_down, v_norm_f_w):
    local = dict(norm1_w=norm1_w, w_in=w_in, mu_shift=mu_shift, w_up_f=w_up_f, w0_f=w0_f, w_up_b=w_up_b,
                 w0_b=w0_b, a_up_f=a_up_f, a0_f=a0_f, a_up_b=a_up_b, a0_b=a0_b, g_up=g_up, k_k=k_k, k_a_f=k_a_f,
                 k_a_b=k_a_b, r_k_f=r_k_f, r_k_b=r_k_b, gn_w=gn_w, gn_b=gn_b, conv_w=conv_w, w_out=w_out,
                 norm2_w=norm2_w, w_gate=w_gate, w_up=w_up, w_down=w_down, norm_f_w=norm_f_w)
    mom_m = dict(norm1_w=m_norm1_w, w_in=m_w_in, mu_shift=m_mu_shift, w_up_f=m_w_up_f, w0_f=m_w0_f,
                 w_up_b=m_w_up_b, w0_b=m_w0_b, a_up_f=m_a_up_f, a0_f=m_a0_f, a_up_b=m_a_up_b, a0_b=m_a0_b,
                 g_up=m_g_up, k_k=m_k_k, k_a_f=m_k_a_f, k_a_b=m_k_a_b, r_k_f=m_r_k_f, r_k_b=m_r_k_b,
                 gn_w=m_gn_w, gn_b=m_gn_b, conv_w=m_conv_w, w_out=m_w_out, norm2_w=m_norm2_w, w_gate=m_w_gate,
                 w_up=m_w_up, w_down=m_w_down, norm_f_w=m_norm_f_w)
    mom_v = dict(norm1_w=v_norm1_w, w_in=v_w_in, mu_shift=v_mu_shift, w_up_f=v_w_up_f, w0_f=v_w0_f,
                 w_up_b=v_w_up_b, w0_b=v_w0_b, a_up_f=v_a_up_f, a0_f=v_a0_f, a_up_b=v_a_up_b, a0_b=v_a0_b,
                 g_up=v_g_up, k_k=v_k_k, k_a_f=v_k_a_f, k_a_b=v_k_a_b, r_k_f=v_r_k_f, r_k_b=v_r_k_b,
                 gn_w=v_gn_w, gn_b=v_gn_b, conv_w=v_conv_w, w_out=v_w_out, norm2_w=v_norm2_w, w_gate=v_w_gate,
                 w_up=v_w_up, w_down=v_w_down, norm_f_w=v_norm_f_w)

    sharded = BIG + LORA
    got = _all_gather([local[n].astype(BF16) for n in BIG] + [local[n] for n in LORA], name="gather")
    full = dict(local)
    full.update({n: _from_slots(a, SHARD_AXIS[n]) for n, a in zip(sharded, got)})

    loss_part, grad_x, grads = _local_step(x, loss_target, full)
    loss = lax.psum(loss_part, ("x", "y", "c"))

    vec_rows = jnp.concatenate([grads[n] for n in VEC] + [jnp.zeros((16 - len(VEC), D_RWKV), F32)], axis=0)
    wide_rows = jnp.concatenate([_pad_lanes(grads[n], WIDE_ROW) for n in WIDE]
                                + [jnp.zeros((SUBLANES - len(WIDE), WIDE_ROW), F32)], axis=0)
    recv = _exchange([_to_slots(grads[n], SHARD_AXIS[n]) for n in sharded], [vec_rows, wide_rows],
                     name="grad_exchange")
    out = {}
    for n, parts in zip(BIG, recv):
        out[n] = _adamw_big(parts, local[n], mom_m[n], mom_v[n], name="adamw_" + n)

    def small_form(n, a):
        if n in LORA:
            return a
        a = a.reshape(1, -1)
        return _pad_lanes(a, WIDE_ROW) if n == "mu_shift" else a

    small = LORA + VEC + WIDE
    res = _adamw_small(recv[len(BIG):len(sharded)], recv[len(sharded)], recv[len(sharded) + 1],
                       [tuple(small_form(n, d[n]) for d in (local, mom_m, mom_v)) for n in small],
                       name="adamw_small")
    for n, quad in zip(small, res):
        out[n] = tuple(a[..., :local[n].size].reshape(local[n].shape) if n not in LORA else a for a in quad)
    return (loss, grad_x, *[out[n][i] for i in range(4) for n in WEIGHTS])
```

```python
import functools

import jax
import jax.numpy as jnp
from jax import lax
from jax.experimental import pallas as pl
from jax.experimental.pallas import tpu as pltpu

F32 = jnp.float32
BF16 = jnp.bfloat16
HIGHEST = lax.Precision.HIGHEST

N_DEV = 8
D_MODEL = 1024
D_RWKV = 512
D_CONV = 512
HEAD = 64
N_HEAD = D_RWKV // HEAD
D_LORA = 64
D_GATE = 160
D_FF = 2816
D_SHIFTED = 3 * D_RWKV + 2 * D_LORA + D_GATE
D_IN = D_SHIFTED + 3 * D_CONV
XW0, XA0, XG0 = 1536, 1664, 1792
D_SP = 2048
D_INP = D_SP + 3 * D_CONV
LOG_DECAY_SCALE = 0.606531
RMS_EPS = 1e-6
GN_EPS = 64e-5
NORM_EPS = 1e-12
ADAM_LR, ADAM_B1, ADAM_B2, ADAM_EPS, ADAM_WD, ADAM_STEP = 0.001, 0.9, 0.999, 1e-08, 0.01, 10

LANES = 128
SUBLANES = 8
VMEM_LIMIT = 48 * 1024 * 1024
SCAN_CHUNK = 16
ROW_TILE = 128

BIG = ("w_in", "w_out", "w_gate", "w_up", "w_down")
LORA = ("w_up_f", "w_up_b", "a_up_f", "a_up_b", "g_up", "conv_w")
SHARD_AXIS = {"w_in": 2, "w_out": 1, "w_gate": 2, "w_up": 2, "w_down": 1, "w_up_f": 2, "w_up_b": 2,
              "a_up_f": 2, "a_up_b": 2, "g_up": 2, "conv_w": 2}
VEC = ("w0_f", "w0_b", "a0_f", "a0_b", "k_k", "k_a_f", "k_a_b", "r_k_f", "r_k_b", "gn_w", "gn_b")
WIDE = ("mu_shift", "norm1_w", "norm2_w", "norm_f_w")
WIDE_ROW = 2048
WEIGHTS = ("norm1_w", "w_in", "mu_shift", "w_up_f", "w0_f", "w_up_b", "w0_b", "a_up_f", "a0_f", "a_up_b",
           "a0_b", "g_up", "k_k", "k_a_f", "k_a_b", "r_k_f", "r_k_b", "gn_w", "gn_b", "conv_w", "w_out",
           "norm2_w", "w_gate", "w_up", "w_down", "norm_f_w")


def _params(sem, limit=VMEM_LIMIT):
    return pltpu.CompilerParams(dimension_semantics=sem, vmem_limit_bytes=limit)


def _tile(n, cands):
    for c in cands:
        if n % c == 0:
            return c
    raise ValueError(f"no tile for {n}")


def _mm(a, b, *, ta=False, tb=False, add=None, name):
    (k_dim, m) = a.shape if ta else a.shape[::-1]
    (k2, n) = b.shape[::-1] if tb else b.shape
    assert k_dim == k2, (a.shape, b.shape, ta, tb)
    tm = _tile(m, (1024, 512, 256, 128))
    tn = _tile(n, (1408, 1024, 896, 512, 256, 128))
    tk = k_dim if k_dim <= 1024 else _tile(k_dim, (1408, 896, 512, 256, 128))
    nk = k_dim // tk
    dims = (((0 if ta else 1,), (1 if tb else 0,)), ((), ()))

    def kern(*refs):
        if add is None:
            a_ref, b_ref, o_ref, acc_ref = refs
        else:
            a_ref, b_ref, add_ref, o_ref, acc_ref = refs
        k = pl.program_id(2)

        @pl.when(k == 0)
        def _():
            acc_ref[...] = jnp.zeros_like(acc_ref)

        acc_ref[...] += lax.dot_general(a_ref[...].astype(BF16), b_ref[...].astype(BF16), dims,
                                        preferred_element_type=F32)

        @pl.when(k == nk - 1)
        def _():
            if add is None:
                o_ref[...] = acc_ref[...]
            else:
                o_ref[...] = acc_ref[...] + add_ref[...]

    a_spec = (pl.BlockSpec((tk, tm), lambda i, j, k: (k, i)) if ta
              else pl.BlockSpec((tm, tk), lambda i, j, k: (i, k)))
    b_spec = (pl.BlockSpec((tn, tk), lambda i, j, k: (j, k)) if tb
              else pl.BlockSpec((tk, tn), lambda i, j, k: (k, j)))
    o_spec = pl.BlockSpec((tm, tn), lambda i, j, k: (i, j))
    in_specs = [a_spec, b_spec] + ([o_spec] if add is not None else [])
    args = (a, b) + ((add,) if add is not None else ())
    return pl.pallas_call(
        kern, out_shape=jax.ShapeDtypeStruct((m, n), F32), grid=(m // tm, n // tn, nk),
        in_specs=in_specs, out_specs=o_spec, scratch_shapes=[pltpu.VMEM((tm, tn), F32)],
        compiler_params=_params(("parallel", "parallel", "arbitrary")), name=name)(*args)


def _rowwise(fn, rows, consts, out_rows, out_accs, *, name, tb=ROW_TILE):
    t = (rows[0][0] if isinstance(rows[0], tuple) else rows[0]).shape[0]
    n_r, n_c, n_o, n_a = len(rows), len(consts), len(out_rows), len(out_accs)
    pieces = [w if isinstance(w, (list, tuple)) else [w] for w in out_rows]

    def kern(*refs):
        r_refs = refs[:n_r]
        c_refs = refs[n_r:n_r + n_c]
        o_refs = refs[n_r + n_c:n_r + n_c + n_o]
        a_refs = refs[n_r + n_c + n_o:]
        vals = fn(*[r[...] for r in r_refs], *[c[...] for c in c_refs])
        vals = list(vals) if isinstance(vals, (tuple, list)) else [vals]
        pos = 0
        for o_ref, ws in zip(o_refs, pieces):
            off = 0
            for w in ws:
                o_ref[:, off:off + w] = vals[pos]
                off += w
                pos += 1
        if n_a:
            @pl.when(pl.program_id(0) == 0)
            def _():
                for a_ref in a_refs:
                    a_ref[...] = jnp.zeros_like(a_ref)
            for a_ref, v in zip(a_refs, vals[pos:]):
                a_ref[...] += v

    in_specs, args = [], []
    for r in rows:
        if isinstance(r, tuple):
            arr, blk, w = r
            in_specs.append(pl.BlockSpec((tb, w), functools.partial(lambda i, blk: (i, blk), blk=blk)))
        else:
            arr = r
            in_specs.append(pl.BlockSpec((tb, arr.shape[1]), lambda i: (i, 0)))
        args.append(arr)
    for c in consts:
        in_specs.append(pl.BlockSpec(c.shape, lambda i: (0, 0)))
        args.append(c)
    out_shape = [jax.ShapeDtypeStruct((t, sum(ws)), F32) for ws in pieces]
    out_specs = [pl.BlockSpec((tb, sum(ws)), lambda i: (i, 0)) for ws in pieces]
    for shp in out_accs:
        out_shape.append(jax.ShapeDtypeStruct(shp, F32))
        out_specs.append(pl.BlockSpec(shp, lambda i: (0, 0)))
    res = pl.pallas_call(
        kern, out_shape=out_shape, grid=(t // tb,), in_specs=in_specs, out_specs=out_specs,
        compiler_params=_params(("arbitrary",) if n_a else ("parallel",)), name=name)(*args)
    return res


def _rms(x, w):
    return x * lax.rsqrt(jnp.mean(x * x, axis=-1, keepdims=True) + RMS_EPS) * w


def _seg(x, bd):
    return jnp.dot(x, bd, precision=HIGHEST, preferred_element_type=F32)


def _colsum(x):
    return jnp.sum(x, axis=0, keepdims=True)


def _prescan_math(r, k, xw, xa, xg, k_k, w0f, w0b, a0f, a0b, kaf, kab, wupf, wupb, aupf, aupb, gup, bd):
    kkr = k * k_k
    norm = jnp.sqrt(_seg(kkr * kkr, bd))
    kk = kkr / jnp.maximum(norm, NORM_EPS)
    th = jnp.tanh(xw)

    def direction(w0, wup, a0, aup, ka):
        logit = w0 + jnp.dot(th, wup, preferred_element_type=F32)
        w = jnp.exp(-LOG_DECAY_SCALE * jax.nn.sigmoid(logit))
        a = jax.nn.sigmoid(a0 + jnp.dot(xa, aup, preferred_element_type=F32))
        kd = k * (1.0 + (a - 1.0) * ka)
        return w, kd, kk * a

    wf, kdf, bf = direction(w0f, wupf, a0f, aupf, kaf)
    wb, kdb, bb = direction(w0b, wupb, a0b, aupb, kab)
    g = jnp.dot(jax.nn.sigmoid(xg), gup, preferred_element_type=F32)
    return kk, r, wf, wb, bf, bb, kdf, kdb, g


def _postscan_math(y, r, v, kdf, kdb, g, gn_w, gn_b, rkf, rkb, bd):
    mean = _seg(y, bd) * (1.0 / HEAD)
    yc = y - mean
    var = _seg(yc * yc, bd) * (1.0 / HEAD)
    yg = yc * lax.rsqrt(var + GN_EPS) * gn_w + gn_b
    bonus = (_seg(r * kdf * rkf, bd) + _seg(r * kdb * rkb, bd)) * v
    return (yg + bonus) * g


def _halo_specs(width, col_blk, tb, t):
    nb = t // SUBLANES
    step = tb // SUBLANES
    main = pl.BlockSpec((tb, width), lambda i: (i, col_blk))
    prev = pl.BlockSpec((SUBLANES, width), lambda i: (jnp.maximum(i * step - 1, 0), col_blk))
    nxt = pl.BlockSpec((SUBLANES, width), lambda i: (jnp.minimum((i + 1) * step, nb - 1), col_blk))
    return [main, prev, nxt]


def _neighbours(z, prev8, next8, first, last):
    tb = z.shape[0]
    row = lax.broadcasted_iota(jnp.int32, z.shape, 0)
    prow = jnp.where(first, 0.0, prev8[SUBLANES - 1:SUBLANES, :])
    nrow = jnp.where(last, 0.0, next8[0:1, :])
    down = jnp.where(row == 0, prow, pltpu.roll(z, 1, 0))
    up = jnp.where(row == tb - 1, nrow, pltpu.roll(z, tb - 1, 0))
    return down, up


def _shift_conv_fwd(p, mu, conv_w, seq, *, name, tb=ROW_TILE):
    t = p.shape[0]
    per_seq = seq // tb

    def kern(p_ref, pp_ref, pn_ref, mu_ref, cw_ref, pss_ref, oc_ref):
        i = pl.program_id(0)
        first = (i % per_seq) == 0
        last = (i % per_seq) == per_seq - 1
        ps = p_ref[:, :D_SP]
        down, up = _neighbours(ps, pp_ref[:, :D_SP], pn_ref[:, :D_SP], first, last)
        pss_ref[...] = ps + mu_ref[...] * (0.5 * (down + up) - ps)
        gb = p_ref[:, D_SP:D_SP + D_CONV]
        u = p_ref[:, D_SP + D_CONV:D_SP + 2 * D_CONV] * p_ref[:, D_SP + 2 * D_CONV:]
        u_p = pp_ref[:, D_SP + D_CONV:D_SP + 2 * D_CONV] * pp_ref[:, D_SP + 2 * D_CONV:]
        u_n = pn_ref[:, D_SP + D_CONV:D_SP + 2 * D_CONV] * pn_ref[:, D_SP + 2 * D_CONV:]
        udown, uup = _neighbours(u, u_p, u_n, first, last)
        oc_ref[...] = gb * (cw_ref[0:1, :] * udown + cw_ref[1:2, :] * u + cw_ref[2:3, :] * uup)

    return pl.pallas_call(
        kern,
        out_shape=[jax.ShapeDtypeStruct((t, D_SP), F32), jax.ShapeDtypeStruct((t, D_CONV), F32)],
        grid=(t // tb,),
        in_specs=_halo_specs(D_INP, 0, tb, t) + [pl.BlockSpec((1, D_SP), lambda i: (0, 0)),
                                                 pl.BlockSpec((SUBLANES, D_CONV), lambda i: (0, 0))],
        out_specs=[pl.BlockSpec((tb, D_SP), lambda i: (i, 0)), pl.BlockSpec((tb, D_CONV), lambda i: (i, 0))],
        compiler_params=_params(("parallel",)), name=name)(p, p, p, mu, conv_w)


def _shift_conv_bwd(p, d_pss, d_o, mu, conv_w, seq, *, name, tb=ROW_TILE):
    t = p.shape[0]
    per_seq = seq // tb

    def kern(p_ref, pp_ref, pn_ref, d_ref, dp_ref, dn_ref, do_ref, dop_ref, don_ref, mu_ref, cw_ref,
             out_ref, dmu_ref, dcw_ref):
        i = pl.program_id(0)
        first = (i % per_seq) == 0
        last = (i % per_seq) == per_seq - 1

        @pl.when(i == 0)
        def _():
            dmu_ref[...] = jnp.zeros_like(dmu_ref)
            dcw_ref[...] = jnp.zeros_like(dcw_ref)

        mu_v = mu_ref[...]
        ps = p_ref[:, :D_SP]
        down, up = _neighbours(ps, pp_ref[:, :D_SP], pn_ref[:, :D_SP], first, last)
        d = d_ref[...]
        ddown, dup = _neighbours(d, dp_ref[...], dn_ref[...], first, last)
        out_ref[:, :D_SP] = d - mu_v * d + 0.5 * (mu_v * ddown + mu_v * dup)
        dmu_ref[...] += _colsum(d * (0.5 * (down + up) - ps))

        def parts(ref):
            return (ref[:, D_SP:D_SP + D_CONV], ref[:, D_SP + D_CONV:D_SP + 2 * D_CONV],
                    ref[:, D_SP + 2 * D_CONV:])

        gb, gc, hh = parts(p_ref)
        gb_p, gc_p, hh_p = parts(pp_ref)
        gb_n, gc_n, hh_n = parts(pn_ref)
        u = gc * hh
        udown, uup = _neighbours(u, gc_p * hh_p, gc_n * hh_n, first, last)
        cw0, cw1, cw2 = cw_ref[0:1, :], cw_ref[1:2, :], cw_ref[2:3, :]
        do = do_ref[...]
        duc = do * gb
        ducdown, ducup = _neighbours(duc, dop_ref[...] * gb_p, don_ref[...] * gb_n, first, last)
        du = cw0 * ducup + cw1 * duc + cw2 * ducdown
        out_ref[:, D_SP:D_SP + D_CONV] = do * (cw0 * udown + cw1 * u + cw2 * uup)
        out_ref[:, D_SP + D_CONV:D_SP + 2 * D_CONV] = du * hh
        out_ref[:, D_SP + 2 * D_CONV:] = du * gc
        dcw_ref[0:1, :] += _colsum(duc * udown)
        dcw_ref[1:2, :] += _colsum(duc * u)
        dcw_ref[2:3, :] += _colsum(duc * uup)

    return pl.pallas_call(
        kern,
        out_shape=[jax.ShapeDtypeStruct((t, D_INP), F32), jax.ShapeDtypeStruct((1, D_SP), F32),
                   jax.ShapeDtypeStruct((SUBLANES, D_CONV), F32)],
        grid=(t // tb,),
        in_specs=(_halo_specs(D_INP, 0, tb, t) + _halo_specs(D_SP, 0, tb, t) + _halo_specs(D_CONV, 1, tb, t)
                  + [pl.BlockSpec((1, D_SP), lambda i: (0, 0)),
                     pl.BlockSpec((SUBLANES, D_CONV), lambda i: (0, 0))]),
        out_specs=[pl.BlockSpec((tb, D_INP), lambda i: (i, 0)), pl.BlockSpec((1, D_SP), lambda i: (0, 0)),
                   pl.BlockSpec((SUBLANES, D_CONV), lambda i: (0, 0))],
        compiler_params=_params(("arbitrary",)), name=name)(p, p, p, d_pss, d_pss, d_pss, d_o, d_o, d_o, mu, conv_w)


N_CHAIN = 16
V_LO = LANES // N_CHAIN
V_HI = HEAD // V_LO
N_GROUP = LANES // N_CHAIN
G_KK, G_R, G_W, G_B, G_KD = 0, 1, (2, 3), (4, 5), (6, 7)


def _group(x, j, lane):
    g = pltpu.roll(x, (LANES - N_CHAIN * j) % LANES, 1) if j else x
    g = jnp.where(lane < N_CHAIN, g, pltpu.roll(g, N_CHAIN, 1))
    g = jnp.where(lane < 2 * N_CHAIN, g, pltpu.roll(g, 2 * N_CHAIN, 1))
    return jnp.where(lane < 4 * N_CHAIN, g, pltpu.roll(g, 4 * N_CHAIN, 1))


def _scan_inputs(x, d, lane):
    return [_group(x, j, lane) for j in (G_KK, G_R, G_W[d], G_B[d], G_KD[d])]


def _lane_scan_fwd(xall, v_l, *, name):
    steps = xall.shape[0]
    nc = steps // SCAN_CHUNK
    mirror = lambda c: nc - 1 - c

    def kern(xf_ref, xb_ref, vf_ref, vb_ref, yf_ref, yb_ref, hist_ref, fin_ref, st_ref):
        c = pl.program_id(0)

        @pl.when(c == 0)
        def _():
            st_ref[...] = jnp.zeros_like(st_ref)

        row = lax.broadcasted_iota(jnp.int32, (V_HI, LANES), 0)
        lane = lax.broadcasted_iota(jnp.int32, (HEAD, LANES), 1)

        def step(i, carry):
            j = SCAN_CHUNK - 1 - i
            for d, (x_t, v_t, y_ref, at) in enumerate(((xf_ref[i], vf_ref[i], yf_ref, i),
                                                       (xb_ref[j], vb_ref[j], yb_ref, j))):
                kk_t, r_t, w_t, b_t, kd_t = _scan_inputs(x_t, d, lane)
                y_t = jnp.zeros((V_HI, LANES), F32)
                for vh in range(V_HI):
                    tile = d * V_HI + vh
                    state = st_ref[tile]
                    hist_ref[i, tile] = state
                    sa = _colsum(state * kk_t)
                    state = state * w_t - sa * b_t + v_t[vh:vh + 1, :] * kd_t
                    st_ref[tile] = state
                    y_t = jnp.where(row == vh, _colsum(state * r_t), y_t)
                y_ref[at] = y_t
            return carry

        lax.fori_loop(0, SCAN_CHUNK, step, 0)

        @pl.when(c == nc - 1)
        def _():
            fin_ref[...] = st_ref[...]

    def k_spec(fn):
        return pl.BlockSpec((SCAN_CHUNK, HEAD, LANES), lambda c: (fn(c), 0, 0))

    def v_spec(fn):
        return pl.BlockSpec((SCAN_CHUNK, V_HI, LANES), lambda c: (fn(c), 0, 0))

    same = lambda c: c
    st_shape = (2 * V_HI, HEAD, LANES)
    return pl.pallas_call(
        kern,
        out_shape=[jax.ShapeDtypeStruct((steps, V_HI, LANES), F32)] * 2
        + [jax.ShapeDtypeStruct((steps,) + st_shape, F32), jax.ShapeDtypeStruct(st_shape, F32)],
        grid=(nc,), in_specs=[k_spec(same), k_spec(mirror), v_spec(same), v_spec(mirror)],
        out_specs=[v_spec(same), v_spec(mirror),
                   pl.BlockSpec((SCAN_CHUNK,) + st_shape, lambda c: (c, 0, 0, 0)),
                   pl.BlockSpec(st_shape, lambda c: (0, 0, 0))],
        scratch_shapes=[pltpu.VMEM(st_shape, F32)],
        compiler_params=_params(("arbitrary",)), name=name)(xall, xall, v_l, v_l)


def _lane_scan_bwd(xall, v_l, dy_l, hist, fin, *, name):
    steps = xall.shape[0]
    nc = steps // SCAN_CHUNK
    back = lambda c: nc - 1 - c
    same = lambda c: c

    def kern(xf_ref, xb_ref, vf_ref, vb_ref, dyf_ref, dyb_ref, hist_ref, fin_ref,
             gf_ref, gb_ref, dvf_ref, dvb_ref, ds_ref, after_ref):
        c = pl.program_id(0)

        @pl.when(c == 0)
        def _():
            ds_ref[...] = jnp.zeros_like(ds_ref)
            after_ref[...] = fin_ref[...]

        row = lax.broadcasted_iota(jnp.int32, (V_HI, LANES), 0)
        lane = lax.broadcasted_iota(jnp.int32, (HEAD, LANES), 1)
        grp = lax.shift_right_logical(lane, jnp.full_like(lane, 4))

        def group_sum(x):
            x = x + pltpu.roll(x, 4 * N_CHAIN, 1)
            x = x + pltpu.roll(x, 2 * N_CHAIN, 1)
            return x + pltpu.roll(x, N_CHAIN, 1)

        def step(ii, carry):
            i = SCAN_CHUNK - 1 - ii
            j = ii
            for d, (x_t, v_t, dy_t, g_ref, dv_ref, at) in enumerate((
                    (xf_ref[i], vf_ref[i], dyf_ref[i], gf_ref, dvf_ref, i),
                    (xb_ref[j], vb_ref[j], dyb_ref[j], gb_ref, dvb_ref, j))):
                kk_t, r_t, w_t, b_t, kd_t = _scan_inputs(x_t, d, lane)
                dv_t = jnp.zeros((V_HI, LANES), F32)
                zero = jnp.zeros((HEAD, LANES), F32)
                dkk, dr, dw, db, dkd = zero, zero, zero, zero, zero
                for vh in range(V_HI):
                    tile = d * V_HI + vh
                    before = hist_ref[i, tile]
                    dy_r, v_r = dy_t[vh:vh + 1, :], v_t[vh:vh + 1, :]
                    g = ds_ref[tile] + dy_r * r_t
                    sa = _colsum(before * kk_t)
                    dsa = -_colsum(g * b_t)
                    dv_t = jnp.where(row == vh, _colsum(g * kd_t), dv_t)
                    dr = dr + after_ref[tile] * dy_r
                    dw = dw + g * before
                    dkd = dkd + g * v_r
                    db = db - g * sa
                    dkk = dkk + before * dsa
                    ds_ref[tile] = g * w_t + dsa * kk_t
                    after_ref[tile] = before
                out = jnp.where(grp == G_KK, group_sum(dkk), 0.0)
                out = jnp.where(grp == G_R, group_sum(dr), out)
                out = jnp.where(grp == G_W[d], group_sum(dw), out)
                out = jnp.where(grp == G_B[d], group_sum(db), out)
                out = jnp.where(grp == G_KD[d], group_sum(dkd), out)
                g_ref[at] = out
                dv_ref[at] = dv_t
            return carry

        lax.fori_loop(0, SCAN_CHUNK, step, 0)

    def k_spec(fn):
        return pl.BlockSpec((SCAN_CHUNK, HEAD, LANES), lambda c: (fn(c), 0, 0))

    def v_spec(fn):
        return pl.BlockSpec((SCAN_CHUNK, V_HI, LANES), lambda c: (fn(c), 0, 0))

    st_shape = (2 * V_HI, HEAD, LANES)
    return pl.pallas_call(
        kern,
        out_shape=[jax.ShapeDtypeStruct((steps, HEAD, LANES), F32)] * 2
        + [jax.ShapeDtypeStruct((steps, V_HI, LANES), F32)] * 2,
        grid=(nc,),
        in_specs=[k_spec(back), k_spec(same), v_spec(back), v_spec(same), v_spec(back), v_spec(same),
                  pl.BlockSpec((SCAN_CHUNK,) + st_shape, lambda c: (back(c), 0, 0, 0)),
                  pl.BlockSpec(st_shape, lambda c: (0, 0, 0))],
        out_specs=[k_spec(back), k_spec(same), v_spec(back), v_spec(same)],
        scratch_shapes=[pltpu.VMEM(st_shape, F32), pltpu.VMEM(st_shape, F32)],
        compiler_params=_params(("arbitrary",)), name=name)(xall, xall, v_l, v_l, dy_l, dy_l, hist, fin)


def _to_key_lanes(wide, bsz, seq):
    z = wide.reshape(bsz, seq, N_GROUP, N_HEAD, HEAD).transpose(1, 4, 2, 0, 3)
    return z.reshape(seq, HEAD, LANES)


def _from_key_lanes(g, bsz, seq):
    z = g.reshape(seq, HEAD, N_GROUP, bsz, N_HEAD).transpose(3, 0, 2, 4, 1)
    return z.reshape(bsz * seq, N_GROUP * D_RWKV)


def _to_value_lanes(a, bsz, seq):
    z = a.reshape(bsz, seq, N_HEAD, V_HI, V_LO).transpose(1, 3, 4, 0, 2)
    return z.reshape(seq, V_HI, LANES)


def _from_value_lanes(y, bsz, seq):
    z = y.reshape(seq, V_HI, V_LO, bsz, N_HEAD).transpose(3, 0, 4, 1, 2)
    return z.reshape(bsz * seq, D_RWKV)


K_HI = HEAD // SUBLANES


def _lane_group_sum(x):
    x = x + pltpu.roll(x, 4 * N_CHAIN, 1)
    x = x + pltpu.roll(x, 2 * N_CHAIN, 1)
    return x + pltpu.roll(x, N_CHAIN, 1)


def _key_rows(x_t, d):
    out = []
    for grp in (G_KK, G_R, G_W[d], G_B[d], G_KD[d]):
        blk = x_t[SUBLANES * grp:SUBLANES * (grp + 1), :]
        out.append([jnp.broadcast_to(blk[kh:kh + 1, :], (SUBLANES, LANES)) for kh in range(K_HI)])
    return out


def _tree_sum(terms):
    terms = list(terms)
    while len(terms) > 1:
        terms = [a + b for a, b in zip(terms[::2], terms[1::2])]
    return terms[0]


def _kscan_specs(nc):
    same = lambda c: c
    mirror = lambda c: nc - 1 - c

    def k_spec(fn):
        return pl.BlockSpec((SCAN_CHUNK, HEAD, LANES), lambda c: (fn(c), 0, 0))

    def v_spec(fn):
        return pl.BlockSpec((SCAN_CHUNK, SUBLANES, LANES), lambda c: (fn(c), 0, 0))

    return same, mirror, k_spec, v_spec


ST_SHAPE = (2, K_HI, V_HI, SUBLANES, LANES)


def _lane_group_index():
    lane = lax.broadcasted_iota(jnp.int32, (SUBLANES, LANES), 1)
    return lax.shift_right_logical(lane, jnp.full_like(lane, 4))


def _spread_groups(x, grp):
    rolled = [x] + [pltpu.roll(x, s * N_CHAIN, 1) for s in range(1, N_GROUP)]
    out = []
    for j in range(N_GROUP):
        t = rolled[(0 - j) % N_GROUP]
        for g in range(1, N_GROUP):
            t = jnp.where(grp == g, rolled[(g - j) % N_GROUP], t)
        out.append(t)
    return out


def _gather_groups(tiles, grp):
    total = None
    for s in range(N_GROUP):
        b = tiles[s % N_GROUP]
        for g in range(1, N_GROUP):
            b = jnp.where(grp == g, tiles[(g + s) % N_GROUP], b)
        b = pltpu.roll(b, s * N_CHAIN, 1) if s else b
        total = b if total is None else total + b
    return total


def _lane_group_sum_short(x):
    return _tree_sum([x] + [pltpu.roll(x, k * N_CHAIN, 1) for k in range(1, N_GROUP)])


def _kscan_fwd(xall, v_c, *, name):
    steps = xall.shape[0]
    nc = steps // SCAN_CHUNK
    same, mirror, k_spec, v_spec = _kscan_specs(nc)

    def kern(xf_ref, xb_ref, vf_ref, vb_ref, yf_ref, yb_ref, hist_ref, fin_ref, st_ref):
        c = pl.program_id(0)

        @pl.when(c == 0)
        def _():
            st_ref[...] = jnp.zeros_like(st_ref)

        grp = _lane_group_index()

        def step(i, carry):
            j = SCAN_CHUNK - 1 - i
            for d, (x_t, v_t, y_ref, at) in enumerate(((xf_ref[i], vf_ref[i], yf_ref, i),
                                                       (xb_ref[j], vb_ref[j], yb_ref, j))):
                kk_r, r_r, w_r, b_r, kd_r = _key_rows(x_t, d)
                v_b = _spread_groups(v_t, grp)
                y_p = []
                for vh in range(V_HI):
                    st = [st_ref[d, kh, vh] for kh in range(K_HI)]
                    for kh in range(K_HI):
                        hist_ref[i, d, kh, vh] = st[kh]
                    sa = _lane_group_sum_short(_tree_sum(st[kh] * kk_r[kh] for kh in range(K_HI)))
                    new = [st[kh] * w_r[kh] - sa * b_r[kh] + v_b[vh] * kd_r[kh] for kh in range(K_HI)]
                    for kh in range(K_HI):
                        st_ref[d, kh, vh] = new[kh]
                    y_p.append(_tree_sum(new[kh] * r_r[kh] for kh in range(K_HI)))
                y_ref[at] = _gather_groups(y_p, grp)
            return carry

        lax.fori_loop(0, SCAN_CHUNK, step, 0)

        @pl.when(c == nc - 1)
        def _():
            fin_ref[...] = st_ref[...]

    return pl.pallas_call(
        kern,
        out_shape=[jax.ShapeDtypeStruct((steps, SUBLANES, LANES), F32)] * 2
        + [jax.ShapeDtypeStruct((steps,) + ST_SHAPE, F32), jax.ShapeDtypeStruct(ST_SHAPE, F32)],
        grid=(nc,), in_specs=[k_spec(same), k_spec(mirror), v_spec(same), v_spec(mirror)],
        out_specs=[v_spec(same), v_spec(mirror),
                   pl.BlockSpec((SCAN_CHUNK,) + ST_SHAPE, lambda c: (c, 0, 0, 0, 0, 0)),
                   pl.BlockSpec(ST_SHAPE, lambda c: (0, 0, 0, 0, 0))],
        scratch_shapes=[pltpu.VMEM(ST_SHAPE, F32)],
        compiler_params=_params(("arbitrary",)), name=name)(xall, xall, v_c, v_c)


def _kscan_bwd(xall, v_c, dy_c, hist, fin, *, name):
    steps = xall.shape[0]
    nc = steps // SCAN_CHUNK
    same, back, k_spec, v_spec = _kscan_specs(nc)

    def kern(xf_ref, xb_ref, vf_ref, vb_ref, dyf_ref, dyb_ref, hist_ref, fin_ref,
             gf_ref, gb_ref, dvf_ref, dvb_ref, ds_ref, after_ref):
        c = pl.program_id(0)

        @pl.when(c == 0)
        def _():
            ds_ref[...] = jnp.zeros_like(ds_ref)
            after_ref[...] = fin_ref[...]

        grp = _lane_group_index()
        row = lax.broadcasted_iota(jnp.int32, (SUBLANES, LANES), 0)

        def step(ii, carry):
            i = SCAN_CHUNK - 1 - ii
            j = ii
            for d, (x_t, v_t, dy_t, g_ref, dv_ref, at) in enumerate((
                    (xf_ref[i], vf_ref[i], dyf_ref[i], gf_ref, dvf_ref, i),
                    (xb_ref[j], vb_ref[j], dyb_ref[j], gb_ref, dvb_ref, j))):
                kk_r, r_r, w_r, b_r, kd_r = _key_rows(x_t, d)
                v_s, dy_s = _spread_groups(v_t, grp), _spread_groups(dy_t, grp)
                ks = range(K_HI)
                zero = jnp.zeros((SUBLANES, LANES), F32)
                dkk, dr, dw, db, dkd = ([zero] * K_HI for _ in range(5))
                dv_p = []
                for vh in range(V_HI):
                    v_b, dy_b = v_s[vh], dy_s[vh]
                    before = [hist_ref[i, d, kh, vh] for kh in ks]
                    g = [ds_ref[d, kh, vh] + dy_b * r_r[kh] for kh in ks]
                    dsa = -_lane_group_sum_short(_tree_sum(g[kh] * b_r[kh] for kh in ks))
                    sa = _lane_group_sum(_tree_sum(before[kh] * kk_r[kh] for kh in ks))
                    dv_p.append(_tree_sum(g[kh] * kd_r[kh] for kh in ks))
                    dr = [dr[kh] + after_ref[d, kh, vh] * dy_b for kh in ks]
                    dw = [dw[kh] + g[kh] * before[kh] for kh in ks]
                    dkd = [dkd[kh] + g[kh] * v_b for kh in ks]
                    db = [db[kh] - g[kh] * sa for kh in ks]
                    dkk = [dkk[kh] + before[kh] * dsa for kh in ks]
                    for kh in ks:
                        ds_ref[d, kh, vh] = g[kh] * w_r[kh] + dsa * kk_r[kh]
                        after_ref[d, kh, vh] = before[kh]
                dv_ref[at] = _gather_groups(dv_p, grp)
                blocks = {G_KK: dkk, G_R: dr, G_W[d]: dw, G_B[d]: db, G_KD[d]: dkd}
                for gi in range(N_GROUP):
                    blk = zero
                    if gi in blocks:
                        for kh in ks:
                            blk = jnp.where(row == kh, _colsum(blocks[gi][kh]), blk)
                    g_ref[at, SUBLANES * gi:SUBLANES * (gi + 1), :] = blk
            return carry

        lax.fori_loop(0, SCAN_CHUNK, step, 0)

    return pl.pallas_call(
        kern,
        out_shape=[jax.ShapeDtypeStruct((steps, HEAD, LANES), F32)] * 2
        + [jax.ShapeDtypeStruct((steps, SUBLANES, LANES), F32)] * 2,
        grid=(nc,),
        in_specs=[k_spec(back), k_spec(same), v_spec(back), v_spec(same), v_spec(back), v_spec(same),
                  pl.BlockSpec((SCAN_CHUNK,) + ST_SHAPE, lambda c: (back(c), 0, 0, 0, 0, 0)),
                  pl.BlockSpec(ST_SHAPE, lambda c: (0, 0, 0, 0, 0))],
        out_specs=[k_spec(back), k_spec(same), v_spec(back), v_spec(same)],
        scratch_shapes=[pltpu.VMEM(ST_SHAPE, F32), pltpu.VMEM(ST_SHAPE, F32)],
        compiler_params=_params(("arbitrary",)), name=name)(xall, xall, v_c, v_c, dy_c, dy_c, hist, fin)


def _to_key_rows(wide, bsz, seq):
    z = wide.reshape(bsz, seq, N_GROUP, N_HEAD, K_HI, SUBLANES).transpose(1, 2, 4, 5, 0, 3)
    return z.reshape(seq, HEAD, LANES)


def _from_key_rows(g, bsz, seq):
    z = g.reshape(seq, N_GROUP, K_HI, SUBLANES, bsz, N_HEAD).transpose(4, 0, 1, 5, 2, 3)
    return z.reshape(bsz * seq, N_GROUP * D_RWKV)


def _to_value_rows(a, bsz, seq):
    z = a.reshape(bsz, seq, N_HEAD, V_HI, SUBLANES).transpose(1, 4, 3, 0, 2)
    return z.reshape(seq, SUBLANES, LANES)


def _from_value_rows(y, bsz, seq):
    z = y.reshape(seq, SUBLANES, V_HI, bsz, N_HEAD).transpose(3, 0, 4, 2, 1)
    return z.reshape(bsz * seq, D_RWKV)


def _pad_cols(a, segs):
    out, off = [], 0
    for w, wp in segs:
        out.append(a[..., off:off + w])
        if wp > w:
            out.append(jnp.zeros(a.shape[:-1] + (wp - w,), a.dtype))
        off += w
    return jnp.concatenate(out, axis=-1)


def _unpad_cols(a, segs):
    out, off = [], 0
    for w, wp in segs:
        out.append(a[..., off:off + w])
        off += wp
    return jnp.concatenate(out, axis=-1)


P_SEGS = ((3 * D_RWKV, 3 * D_RWKV), (D_LORA, 128), (D_LORA, 128), (D_GATE, 256), (3 * D_CONV, 3 * D_CONV))
S_SEGS = P_SEGS[:4]


def _pad_rows(a, rows):
    return jnp.concatenate([a, jnp.zeros((rows - a.shape[0], a.shape[1]), a.dtype)], axis=0)


def _local_step(x, target, w):
    bsz, seq, _ = x.shape
    t = bsz * seq
    x2d = x.reshape(t, D_MODEL)
    tg2d = target.reshape(t, D_MODEL)
    row = lambda a: a.reshape(1, -1).astype(F32)

    w_in = _pad_cols(w["w_in"][0], P_SEGS)
    mu = _pad_cols(row(w["mu_shift"]), S_SEGS)
    wupf, wupb, aupf, aupb = (_pad_rows(w[n][0].astype(F32), 128) for n in ("w_up_f", "w_up_b", "a_up_f", "a_up_b"))
    gup = _pad_rows(w["g_up"][0].astype(F32), 256)
    conv_w = _pad_rows(w["conv_w"][0].astype(F32), SUBLANES)
    w_out, w_gate, w_up, w_down = w["w_out"][0], w["w_gate"][0], w["w_up"][0], w["w_down"][0]
    norm1, norm2, normf = row(w["norm1_w"]), row(w["norm2_w"]), row(w["norm_f_w"])
    vec = {n: row(w[n]) for n in VEC}
    head_of = jnp.arange(D_RWKV) // HEAD
    bd = (head_of[:, None] == head_of[None, :]).astype(F32)
    pre_consts = [vec["k_k"], vec["w0_f"], vec["w0_b"], vec["a0_f"], vec["a0_b"], vec["k_a_f"], vec["k_a_b"],
                  wupf, wupb, aupf, aupb, gup, bd]
    post_consts = [vec["gn_w"], vec["gn_b"], vec["r_k_f"], vec["r_k_b"], bd]

    h1, = _rowwise(_rms, [x2d], [norm1], [D_MODEL], [], name="rms1_fwd")
    p = _mm(h1, w_in, name="mm_in")
    pss, oconv = _shift_conv_fwd(p, mu, conv_w, seq, name="shift_conv_fwd")
    pre_rows = [(pss, 0, 512), (pss, 1, 512), (pss, XW0 // 128, 128), (pss, XA0 // 128, 128), (pss, XG0 // 256, 256)]
    sc, g = _rowwise(_prescan_math, pre_rows, pre_consts, [[D_RWKV] * N_GROUP, D_RWKV], [], name="prescan_fwd")
    xall = _to_key_rows(sc, bsz, seq)
    v_l = _to_value_rows(pss[:, 2 * D_RWKV:3 * D_RWKV], bsz, seq)
    y_f, y_b, hist, fin = _kscan_fwd(xall, v_l, name="scan_fwd")
    y = _from_value_rows(y_f + y_b, bsz, seq)
    post_rows = [y, (pss, 0, 512), (pss, 2, 512), (sc, G_KD[0], 512), (sc, G_KD[1], 512), g]

    def post_fwd(y_, r_, v_, kdf_, kdb_, g_, oc_, *consts):
        return _postscan_math(y_, r_, v_, kdf_, kdb_, g_, *consts), oc_

    o, = _rowwise(post_fwd, post_rows + [oconv], post_consts, [[D_RWKV, D_CONV]], [], name="postscan_fwd")
    x1 = _mm(o, w_out, add=x2d, name="mm_out")
    h2, = _rowwise(_rms, [x1], [norm2], [D_MODEL], [], name="rms2_fwd")
    gg = _mm(h2, w_gate, name="mm_gate")
    uu = _mm(h2, w_up, name="mm_up")
    ff, = _rowwise(lambda a, c: jax.nn.silu(a) * c, [gg, uu], [], [D_FF], [], name="swiglu_fwd")
    x2 = _mm(ff, w_down, add=x1, name="mm_down")

    def final(x_, tg_, wn_):
        yo, vjp = jax.vjp(_rms, x_, wn_)
        err = yo - tg_
        dx_, dwn_ = vjp(err * (1.0 / D_MODEL))
        part = jnp.sum(jnp.sum(err * err, axis=1, keepdims=True), axis=0, keepdims=True) * (0.5 / D_MODEL)
        return dx_, part + jnp.zeros((1, LANES), F32), dwn_

    dx2, loss_acc, d_normf = _rowwise(final, [x2, tg2d], [normf], [D_MODEL], [(1, LANES), (1, D_MODEL)],
                                      name="loss_head")
    dff = _mm(dx2, w_down, tb=True, name="mm_down_dx")
    g_w_down = _mm(ff, dx2, ta=True, name="mm_down_dw")

    def swiglu_bwd(a, c, d):
        _, vjp = jax.vjp(lambda a_, c_: jax.nn.silu(a_) * c_, a, c)
        return vjp(d)

    dgg, duu = _rowwise(swiglu_bwd, [gg, uu, dff], [], [D_FF, D_FF], [], name="swiglu_bwd")
    dh2 = _mm(dgg, w_gate, tb=True, name="mm_gate_dx")
    dh2 = _mm(duu, w_up, tb=True, add=dh2, name="mm_up_dx")
    g_w_gate = _mm(h2, dgg, ta=True, name="mm_gate_dw")
    g_w_up = _mm(h2, duu, ta=True, name="mm_up_dw")

    def rms_bwd(x_, dh_, dres_, wn_):
        _, vjp = jax.vjp(_rms, x_, wn_)
        dx_, dwn_ = vjp(dh_)
        return dx_ + dres_, dwn_

    dx1, d_norm2 = _rowwise(rms_bwd, [x1, dh2, dx2], [norm2], [D_MODEL], [(1, D_MODEL)], name="rms2_bwd")
    do = _mm(dx1, w_out, tb=True, name="mm_out_dx")
    g_w_out = _mm(o, dx1, ta=True, name="mm_out_dw")

    def post_bwd(y_, r_, v_, kdf_, kdb_, g_, do_, *consts):
        _, vjp = jax.vjp(lambda *a: _postscan_math(*a, consts[4]), y_, r_, v_, kdf_, kdb_, g_, *consts[:4])
        return vjp(do_)

    (dy, dr_c, dv_c, dkdf_c, dkdb_c, dg, d_gn_w, d_gn_b, d_rkf, d_rkb) = _rowwise(
        post_bwd, post_rows + [(do, 0, 512)], post_consts, [D_RWKV] * 6, [(1, D_RWKV)] * 4, name="postscan_bwd")
    dy_l = _to_value_rows(dy, bsz, seq)
    g_f, g_b, dv_f, dv_b = _kscan_bwd(xall, v_l, dy_l, hist, fin, name="scan_bwd")
    dsc = _from_key_rows(g_f + g_b, bsz, seq)
    dv_s = _from_value_rows(dv_f + dv_b, bsz, seq)

    def pre_bwd(r_, k_, xw_, xa_, xg_, dkk_, dr_s, dwf_, dwb_, dbf_, dbb_, dkdf_s, dkdb_s,
                dr_c_, dv_c_, dv_s_, dkdf_c_, dkdb_c_, dg_, *consts):
        _, vjp = jax.vjp(lambda *a: _prescan_math(*a, consts[-1]), r_, k_, xw_, xa_, xg_, *consts[:-1])
        grads = vjp((dkk_, dr_s + dr_c_, dwf_, dwb_, dbf_, dbb_, dkdf_s + dkdf_c_, dkdb_s + dkdb_c_, dg_))
        dr_, dk_, dxw_, dxa_, dxg_ = grads[:5]
        return (dr_, dk_, dv_c_ + dv_s_, dxw_, dxa_, dxg_) + tuple(grads[5:])

    pre_b_rows = (pre_rows + [(dsc, j, 512) for j in range(N_GROUP)]
                  + [dr_c, dv_c, dv_s, dkdf_c, dkdb_c, dg])
    pre_b = _rowwise(pre_bwd, pre_b_rows, pre_consts, [[512, 512, 512, 128, 128, 256]],
                     [(1, D_RWKV)] * 7 + [(128, D_RWKV)] * 4 + [(256, D_RWKV)], name="prescan_bwd")
    d_pss = pre_b[0]
    d_kk_, d_w0f, d_w0b, d_a0f, d_a0b, d_kaf, d_kab, d_wupf, d_wupb, d_aupf, d_aupb, d_gup = pre_b[1:]
    dp, d_mu, d_conv = _shift_conv_bwd(p, d_pss, do, mu, conv_w, seq, name="shift_conv_bwd")
    dh1 = _mm(dp, w_in, tb=True, name="mm_in_dx")
    g_w_in = _mm(h1, dp, ta=True, name="mm_in_dw")
    dx, d_norm1 = _rowwise(rms_bwd, [x2d, dh1, dx1], [norm1], [D_MODEL], [(1, D_MODEL)], name="rms1_bwd")

    grads = {
        "norm1_w": d_norm1, "w_in": _unpad_cols(g_w_in, P_SEGS)[None], "mu_shift": _unpad_cols(d_mu, S_SEGS),
        "w_up_f": d_wupf[None, :D_LORA], "w0_f": d_w0f, "w_up_b": d_wupb[None, :D_LORA], "w0_b": d_w0b,
        "a_up_f": d_aupf[None, :D_LORA], "a0_f": d_a0f, "a_up_b": d_aupb[None, :D_LORA], "a0_b": d_a0b,
        "g_up": d_gup[None, :D_GATE], "k_k": d_kk_, "k_a_f": d_kaf, "k_a_b": d_kab,
        "r_k_f": d_rkf, "r_k_b": d_rkb, "gn_w": d_gn_w, "gn_b": d_gn_b, "conv_w": d_conv[None, :3],
        "w_out": g_w_out[None], "norm2_w": d_norm2, "w_gate": g_w_gate[None], "w_up": g_w_up[None],
        "w_down": g_w_down[None], "norm_f_w": d_normf,
    }
    return loss_acc[0, 0], dx.reshape(bsz, seq, D_MODEL), grads


def _hbm_specs(n):
    return [pl.BlockSpec(memory_space=pl.ANY)] * n


def _all_gather(arrs, *, name):
    n = len(arrs)

    def body(*refs):
        x_refs, out_refs = refs[:n], refs[n:2 * n]
        send_sems, recv_sems, local_sems = refs[2 * n:]
        x, y, c = lax.axis_index("x"), lax.axis_index("y"), lax.axis_index("c")
        me, sibling = (x, y, c), (x, y, 1 - c)
        chips = [(1 - x, y), (x, 1 - y), (1 - x, 1 - y)]

        def slot(a, px, py, pc):
            return out_refs[a].at[4 * px + 2 * py + pc]

        def copy(a, k, block, to, src=None):
            return pltpu.make_async_remote_copy(
                src_ref=slot(a, *block) if src is None else src, dst_ref=slot(a, *block),
                send_sem=send_sems.at[k, a], recv_sem=recv_sems.at[k, a],
                device_id=to, device_id_type=pl.DeviceIdType.MESH)

        mine = [pltpu.make_async_copy(x_refs[a], slot(a, *me), local_sems.at[a]) for a in range(n)]
        for cp in mine:
            cp.start()
        first = []
        for a in range(n):
            first.append(copy(a, 0, me, sibling, src=x_refs[a]))
            first += [copy(a, 1 + j, me, (*chip, c), src=x_refs[a]) for j, chip in enumerate(chips)]
        for cp in first:
            cp.start()
        passed = []
        for j, chip in enumerate(chips):
            for a in range(n):
                copy(a, 1 + j, (*chip, c), me).wait_recv()
                cp = copy(a, 4 + j, (*chip, c), sibling)
                cp.start()
                passed.append(cp)
        for a in range(n):
            copy(a, 0, sibling, me).wait_recv()
            for j, chip in enumerate(chips):
                copy(a, 4 + j, (*chip, 1 - c), me).wait_recv()
        for cp in first + passed:
            cp.wait_send()
        for cp in mine:
            cp.wait()

    return pl.pallas_call(
        body, out_shape=[jax.ShapeDtypeStruct((N_DEV,) + a.shape, a.dtype) for a in arrs],
        in_specs=_hbm_specs(n), out_specs=_hbm_specs(n),
        scratch_shapes=[pltpu.SemaphoreType.DMA((7, n)), pltpu.SemaphoreType.DMA((7, n)),
                        pltpu.SemaphoreType.DMA((n,))],
        name=name)(*arrs)


def _exchange(sliced, whole, *, name):
    arrs = list(sliced) + list(whole)
    n, n_sliced = len(arrs), len(sliced)

    def body(*refs):
        in_refs, out_refs = refs[:n], refs[n:2 * n]
        send_sems, recv_sems, local_sems = refs[2 * n:]
        x, y, c = lax.axis_index("x"), lax.axis_index("y"), lax.axis_index("c")
        me = 4 * x + 2 * y + c

        def src(a, dev):
            return in_refs[a].at[dev] if a < n_sliced else in_refs[a]

        local = [pltpu.make_async_copy(src(a, me), out_refs[a].at[me], local_sems.at[a]) for a in range(n)]
        for cp in local:
            cp.start()
        copies = []
        for k in range(1, N_DEV):
            px = 1 - x if k & 4 else x
            py = 1 - y if k & 2 else y
            pc = 1 - c if k & 1 else c
            for a in range(n):
                copies.append(pltpu.make_async_remote_copy(
                    src_ref=src(a, 4 * px + 2 * py + pc), dst_ref=out_refs[a].at[me],
                    send_sem=send_sems.at[k - 1, a], recv_sem=recv_sems.at[k - 1, a],
                    device_id=(px, py, pc), device_id_type=pl.DeviceIdType.MESH))
        for cp in copies:
            cp.start()
        for cp in copies:
            cp.wait()
        for cp in local:
            cp.wait()

    out_shape = [jax.ShapeDtypeStruct(a.shape if i < n_sliced else (N_DEV,) + a.shape, a.dtype)
                 for i, a in enumerate(arrs)]
    return pl.pallas_call(
        body, out_shape=out_shape, in_specs=_hbm_specs(n), out_specs=_hbm_specs(n),
        scratch_shapes=[pltpu.SemaphoreType.DMA((7, n)), pltpu.SemaphoreType.DMA((7, n)),
                        pltpu.SemaphoreType.DMA((n,))],
        name=name)(*arrs)


def _adam_math(g, w, m, v):
    nm = ADAM_B1 * m + (1.0 - ADAM_B1) * g
    nv = ADAM_B2 * v + (1.0 - ADAM_B2) * (g * g)
    m_hat = nm / (1.0 - ADAM_B1 ** ADAM_STEP)
    v_hat = nv / (1.0 - ADAM_B2 ** ADAM_STEP)
    return -ADAM_LR * (m_hat / (jnp.sqrt(v_hat) + ADAM_EPS) + ADAM_WD * w), nm, nv


def _slot_sum(ref):
    g = ref[0]
    for s in range(1, N_DEV):
        g = g + ref[s]
    return g


def _adamw_big(parts, w, m, v, *, name):
    _, rws, cols = w.shape
    tr = _tile(rws, (256, 176, 128))

    def kern(p_ref, w_ref, m_ref, v_ref, g_ref, d_ref, nm_ref, nv_ref):
        g = _slot_sum(p_ref)
        g_ref[...] = g
        d_ref[...], nm_ref[...], nv_ref[...] = _adam_math(g, w_ref[...], m_ref[...], v_ref[...])

    spec = pl.BlockSpec((1, tr, cols), lambda i: (0, i, 0))
    return pl.pallas_call(
        kern, out_shape=[jax.ShapeDtypeStruct(w.shape, F32)] * 4, grid=(rws // tr,),
        in_specs=[pl.BlockSpec((N_DEV, 1, tr, cols), lambda i: (0, 0, i, 0)), spec, spec, spec],
        out_specs=[spec] * 4, compiler_params=_params(("parallel",)), name=name)(parts, w, m, v)


def _adamw_small(lora_parts, vec_parts, wide_parts, wmv, *, name):
    names = LORA + VEC + WIDE
    n_l, n = len(LORA), len(names)
    flat = [a for trip in wmv for a in trip]

    def kern(*refs):
        l_refs, vec_ref, wide_ref = refs[:n_l], refs[n_l], refs[n_l + 1]
        in_refs = refs[n_l + 2:n_l + 2 + 3 * n]
        out_refs = refs[n_l + 2 + 3 * n:]
        vec_sum, wide_sum = _slot_sum(vec_ref), _slot_sum(wide_ref)
        for i, nm in enumerate(names):
            w_ref, m_ref, v_ref = in_refs[3 * i:3 * i + 3]
            if i < n_l:
                g = _slot_sum(l_refs[i])
            elif nm in VEC:
                g = vec_sum[i - n_l:i - n_l + 1, :]
            else:
                g = wide_sum[WIDE.index(nm):WIDE.index(nm) + 1, :w_ref.shape[-1]]
            o = out_refs[4 * i:4 * i + 4]
            o[0][...] = g
            o[1][...], o[2][...], o[3][...] = _adam_math(g, w_ref[...], m_ref[...], v_ref[...])

    out_shape = [jax.ShapeDtypeStruct(trip[0].shape, F32) for trip in wmv for _ in range(4)]
    outs = pl.pallas_call(kern, out_shape=out_shape, name=name,
                          compiler_params=pltpu.CompilerParams(vmem_limit_bytes=VMEM_LIMIT))(
        *lora_parts, vec_parts, wide_parts, *flat)
    return [tuple(outs[4 * i:4 * i + 4]) for i in range(n)]


def _to_slots(g, axis):
    _, rws, cols = g.shape
    if axis == 1:
        return g.reshape(N_DEV, 1, rws // N_DEV, cols)
    return g.reshape(1, rws, N_DEV, cols // N_DEV).transpose(2, 0, 1, 3)


def _from_slots(got, axis):
    _, _, rws, cols = got.shape
    if axis == 1:
        return got.reshape(1, N_DEV * rws, cols)
    return got.transpose(1, 2, 0, 3).reshape(1, rws, N_DEV * cols)


def _pad_lanes(a, width):
    return jnp.concatenate([a, jnp.zeros(a.shape[:-1] + (width - a.shape[-1],), a.dtype)], axis=-1)


def kernel(x, norm1_w, w_in, mu_shift, w_up_f, w0_f, w_up_b, w0_b, a_up_f, a0_f, a_up_b, a0_b, g_up, k_k, k_a_f, k_a_b, r_k_f, r_k_b, gn_w, gn_b, conv_w, w_out, norm2_w, w_gate, w_up, w_down, norm_f_w, loss_target, m_norm1_w, m_w_in, m_mu_shift, m_w_up_f, m_w0_f, m_w_up_b, m_w0_b, m_a_up_f, m_a0_f, m_a_up_b, m_a0_b, m_g_up, m_k_k, m_k_a_f, m_k_a_b, m_r_k_f, m_r_k_b, m_gn_w, m_gn_b, m_conv_w, m_w_out, m_norm2_w, m_w_gate, m_w_up, m_w_down, m_norm_f_w, v_norm1_w, v_w_in, v_mu_shift, v_w_up_f, v_w0_f, v_w_up_b, v_w0_b, v_a_up_f, v_a0_f, v_a_up_b, v_a0_b, v_g_up, v_k_k, v_k_a_f, v_k_a_b, v_r_k_f, v_r_k_b, v_gn_w, v_gn_b, v_conv_w, v_w_out, v_norm2_w, v_w_gate, v_w_up, v_w_down, v_norm_f_w):
    local = dict(norm1_w=norm1_w, w_in=w_in, mu_shift=mu_shift, w_up_f=w_up_f, w0_f=w0_f, w_up_b=w_up_b,
                 w0_b=w0_b, a_up_f=a_up_f, a0_f=a0_f, a_up_b=a_up_b, a0_b=a0_b, g_up=g_up, k_k=k_k, k_a_f=k_a_f,
                 k_a_b=k_a_b, r_k_f=r_k_f, r_k_b=r_k_b, gn_w=gn_w, gn_b=gn_b, conv_w=conv_w, w_out=w_out,
                 norm2_w=norm2_w, w_gate=w_gate, w_up=w_up, w_down=w_down, norm_f_w=norm_f_w)
    mom_m = dict(norm1_w=m_norm1_w, w_in=m_w_in, mu_shift=m_mu_shift, w_up_f=m_w_up_f, w0_f=m_w0_f,
                 w_up_b=m_w_up_b, w0_b=m_w0_b, a_up_f=m_a_up_f, a0_f=m_a0_f, a_up_b=m_a_up_b, a0_b=m_a0_b,
                 g_up=m_g_up, k_k=m_k_k, k_a_f=m_k_a_f, k_a_b=m_k_a_b, r_k_f=m_r_k_f, r_k_b=m_r_k_b,
                 gn_w=m_gn_w, gn_b=m_gn_b, conv_w=m_conv_w, w_out=m_w_out, norm2_w=m_norm2_w, w_gate=m_w_gate,
                 w_up=m_w_up, w_down=m_w_down, norm_f_w=m_norm_f_w)
    mom_v = dict(norm1_w=v_norm1_w, w_in=v_w_in, mu_shift=v_mu_shift, w_up_f=v_w_up_f, w0_f=v_w0_f,
                 w_up_b=v_w_up_b, w0_b=v_w0_b, a_up_f=v_a_up_f, a0_f=v_a0_f, a_up_b=v_a_up_b, a0_b=v_a0_b,
                 g_up=v_g_up, k_k=v_k_k, k_a_f=v_k_a_f, k_a_b=v_k_a_b, r_k_f=v_r_k_f, r_k_b=v_r_k_b,
                 gn_w=v_gn_w, gn_b=v_gn_b, conv_w=v_conv_w, w_out=v_w_out, norm2_w=v_norm2_w, w_gate=v_w_gate,
                 w_up=v_w_up, w_down=v_w_down, norm_f_w=v_norm_f_w)

    sharded = BIG + LORA
    got = _all_gather([local[n].astype(BF16) for n in BIG] + [local[n] for n in LORA], name="gather")
    full = dict(local)
    full.update({n: _from_slots(a, SHARD_AXIS[n]) for n, a in zip(sharded, got)})

    loss_part, grad_x, grads = _local_step(x, loss_target, full)
    loss = lax.psum(loss_part, ("x", "y", "c"))

    vec_rows = jnp.concatenate([grads[n] for n in VEC] + [jnp.zeros((16 - len(VEC), D_RWKV), F32)], axis=0)
    wide_rows = jnp.concatenate([_pad_lanes(grads[n], WIDE_ROW) for n in WIDE]
                                + [jnp.zeros((SUBLANES - len(WIDE), WIDE_ROW), F32)], axis=0)
    recv = _exchange([_to_slots(grads[n], SHARD_AXIS[n]) for n in sharded], [vec_rows, wide_rows],
                     name="grad_exchange")
    out = {}
    for n, parts in zip(BIG, recv):
        out[n] = _adamw_big(parts, local[n], mom_m[n], mom_v[n], name="adamw_" + n)

    def small_form(n, a):
        if n in LORA:
            return a
        a = a.reshape(1, -1)
        return _pad_lanes(a, WIDE_ROW) if n == "mu_shift" else a

    small = LORA + VEC + WIDE
    res = _adamw_small(recv[len(BIG):len(sharded)], recv[len(sharded)], recv[len(sharded) + 1],
                       [tuple(small_form(n, d[n]) for d in (local, mom_m, mom_v)) for n in small],
                       name="adamw_small")
    for n, quad in zip(small, res):
        out[n] = tuple(a[..., :local[n].size].reshape(local[n].shape) if n not in LORA else a for a in quad)
    return (loss, grad_x, *[out[n][i] for i in range(4) for n in WEIGHTS])
```

```python
import functools

import jax
import jax.numpy as jnp
from jax import lax
from jax.experimental import pallas as pl
from jax.experimental.pallas import tpu as pltpu

F32 = jnp.float32
BF16 = jnp.bfloat16
HIGHEST = lax.Precision.HIGHEST

N_DEV = 8
D_MODEL = 1024
D_RWKV = 512
D_CONV = 512
HEAD = 64
N_HEAD = D_RWKV // HEAD
D_LORA = 64
D_GATE = 160
D_FF = 2816
D_SHIFTED = 3 * D_RWKV + 2 * D_LORA + D_GATE
D_IN = D_SHIFTED + 3 * D_CONV
XW0, XA0, XG0 = 1536, 1664, 1792
D_SP = 2048
D_INP = D_SP + 3 * D_CONV
LOG_DECAY_SCALE = 0.606531
RMS_EPS = 1e-6
GN_EPS = 64e-5
NORM_EPS = 1e-12
ADAM_LR, ADAM_B1, ADAM_B2, ADAM_EPS, ADAM_WD, ADAM_STEP = 0.001, 0.9, 0.999, 1e-08, 0.01, 10

LANES = 128
SUBLANES = 8
VMEM_LIMIT = 48 * 1024 * 1024
SCAN_CHUNK = 16
ROW_TILE = 128

BIG = ("w_in", "w_out", "w_gate", "w_up", "w_down")
LORA = ("w_up_f", "w_up_b", "a_up_f", "a_up_b", "g_up", "conv_w")
SHARD_AXIS = {"w_in": 2, "w_out": 1, "w_gate": 2, "w_up": 2, "w_down": 1, "w_up_f": 2, "w_up_b": 2,
              "a_up_f": 2, "a_up_b": 2, "g_up": 2, "conv_w": 2}
VEC = ("w0_f", "w0_b", "a0_f", "a0_b", "k_k", "k_a_f", "k_a_b", "r_k_f", "r_k_b", "gn_w", "gn_b")
WIDE = ("mu_shift", "norm1_w", "norm2_w", "norm_f_w")
WIDE_ROW = 2048
WEIGHTS = ("norm1_w", "w_in", "mu_shift", "w_up_f", "w0_f", "w_up_b", "w0_b", "a_up_f", "a0_f", "a_up_b",
           "a0_b", "g_up", "k_k", "k_a_f", "k_a_b", "r_k_f", "r_k_b", "gn_w", "gn_b", "conv_w", "w_out",
           "norm2_w", "w_gate", "w_up", "w_down", "norm_f_w")


def _params(sem, limit=VMEM_LIMIT):
    return pltpu.CompilerParams(dimension_semantics=sem, vmem_limit_bytes=limit)


def _tile(n, cands):
    for c in cands:
        if n % c == 0:
            return c
    raise ValueError(f"no tile for {n}")


def _mm(a, b, *, ta=False, tb=False, add=None, name):
    (k_dim, m) = a.shape if ta else a.shape[::-1]
    (k2, n) = b.shape[::-1] if tb else b.shape
    assert k_dim == k2, (a.shape, b.shape, ta, tb)
    tm = _tile(m, (1408, 1024, 512, 256, 128))
    tn = _tile(n, (1408, 1024, 896, 512, 256, 128))
    tk = k_dim if k_dim <= 1024 else _tile(k_dim, (1408, 896, 512, 256, 128))
    nk = k_dim // tk
    dims = (((0 if ta else 1,), (1 if tb else 0,)), ((), ()))

    def kern(*refs):
        if add is None:
            a_ref, b_ref, o_ref, acc_ref = refs
        else:
            a_ref, b_ref, add_ref, o_ref, acc_ref = refs
        k = pl.program_id(2)

        @pl.when(k == 0)
        def _():
            acc_ref[...] = jnp.zeros_like(acc_ref)

        acc_ref[...] += lax.dot_general(a_ref[...].astype(BF16), b_ref[...].astype(BF16), dims,
                                        preferred_element_type=F32)

        @pl.when(k == nk - 1)
        def _():
            if add is None:
                o_ref[...] = acc_ref[...]
            else:
                o_ref[...] = acc_ref[...] + add_ref[...]

    a_spec = (pl.BlockSpec((tk, tm), lambda i, j, k: (k, i)) if ta
              else pl.BlockSpec((tm, tk), lambda i, j, k: (i, k)))
    b_spec = (pl.BlockSpec((tn, tk), lambda i, j, k: (j, k)) if tb
              else pl.BlockSpec((tk, tn), lambda i, j, k: (k, j)))
    o_spec = pl.BlockSpec((tm, tn), lambda i, j, k: (i, j))
    in_specs = [a_spec, b_spec] + ([o_spec] if add is not None else [])
    args = (a, b) + ((add,) if add is not None else ())
    return pl.pallas_call(
        kern, out_shape=jax.ShapeDtypeStruct((m, n), F32), grid=(m // tm, n // tn, nk),
        in_specs=in_specs, out_specs=o_spec, scratch_shapes=[pltpu.VMEM((tm, tn), F32)],
        compiler_params=_params(("parallel", "parallel", "arbitrary")), name=name)(*args)


def _rowwise(fn, rows, consts, out_rows, out_accs, *, name, tb=ROW_TILE):
    t = (rows[0][0] if isinstance(rows[0], tuple) else rows[0]).shape[0]
    n_r, n_c, n_o, n_a = len(rows), len(consts), len(out_rows), len(out_accs)
    pieces = [w if isinstance(w, (list, tuple)) else [w] for w in out_rows]

    def kern(*refs):
        r_refs = refs[:n_r]
        c_refs = refs[n_r:n_r + n_c]
        o_refs = refs[n_r + n_c:n_r + n_c + n_o]
        a_refs = refs[n_r + n_c + n_o:]
        vals = fn(*[r[...] for r in r_refs], *[c[...] for c in c_refs])
        vals = list(vals) if isinstance(vals, (tuple, list)) else [vals]
        pos = 0
        for o_ref, ws in zip(o_refs, pieces):
            off = 0
            for w in ws:
                o_ref[:, off:off + w] = vals[pos]
                off += w
                pos += 1
        if n_a:
            @pl.when(pl.program_id(0) == 0)
            def _():
                for a_ref in a_refs:
                    a_ref[...] = jnp.zeros_like(a_ref)
            for a_ref, v in zip(a_refs, vals[pos:]):
                a_ref[...] += v

    in_specs, args = [], []
    for r in rows:
        if isinstance(r, tuple):
            arr, blk, w = r
            in_specs.append(pl.BlockSpec((tb, w), functools.partial(lambda i, blk: (i, blk), blk=blk)))
        else:
            arr = r
            in_specs.append(pl.BlockSpec((tb, arr.shape[1]), lambda i: (i, 0)))
        args.append(arr)
    for c in consts:
        in_specs.append(pl.BlockSpec(c.shape, lambda i: (0, 0)))
        args.append(c)
    out_shape = [jax.ShapeDtypeStruct((t, sum(ws)), F32) for ws in pieces]
    out_specs = [pl.BlockSpec((tb, sum(ws)), lambda i: (i, 0)) for ws in pieces]
    for shp in out_accs:
        out_shape.append(jax.ShapeDtypeStruct(shp, F32))
        out_specs.append(pl.BlockSpec(shp, lambda i: (0, 0)))
    res = pl.pallas_call(
        kern, out_shape=out_shape, grid=(t // tb,), in_specs=in_specs, out_specs=out_specs,
        compiler_params=_params(("arbitrary",) if n_a else ("parallel",)), name=name)(*args)
    return res


def _rms(x, w):
    return x * lax.rsqrt(jnp.mean(x * x, axis=-1, keepdims=True) + RMS_EPS) * w


def _seg_sum(x, bd):
    return jnp.concatenate(
        [jnp.dot(x[:, LANES * j:LANES * (j + 1)], bd, precision=HIGHEST, preferred_element_type=F32)
         for j in range(x.shape[1] // LANES)], axis=1)


@jax.custom_vjp
def _seg(x, bd):
    return _seg_sum(x, bd)


_seg.defvjp(lambda x, bd: (_seg_sum(x, bd), bd), lambda bd, ct: (_seg_sum(ct, bd), jnp.zeros_like(bd)))


def _colsum(x):
    return jnp.sum(x, axis=0, keepdims=True)


def _prescan_math(r, k, xw, xa, xg, k_k, w0f, w0b, a0f, a0b, kaf, kab, wupf, wupb, aupf, aupb, gup, bd):
    kkr = k * k_k
    norm = jnp.sqrt(_seg(kkr * kkr, bd))
    kk = kkr / jnp.maximum(norm, NORM_EPS)
    th = jnp.tanh(xw)

    def direction(w0, wup, a0, aup, ka):
        logit = w0 + jnp.dot(th, wup, preferred_element_type=F32)
        w = jnp.exp(-LOG_DECAY_SCALE * jax.nn.sigmoid(logit))
        a = jax.nn.sigmoid(a0 + jnp.dot(xa, aup, preferred_element_type=F32))
        kd = k * (1.0 + (a - 1.0) * ka)
        return w, kd, kk * a

    wf, kdf, bf = direction(w0f, wupf, a0f, aupf, kaf)
    wb, kdb, bb = direction(w0b, wupb, a0b, aupb, kab)
    g = jnp.dot(jax.nn.sigmoid(xg), gup, preferred_element_type=F32)
    return kk, r, wf, wb, bf, bb, kdf, kdb, g


def _postscan_math(y, r, v, kdf, kdb, g, gn_w, gn_b, rkf, rkb, bd):
    mean = _seg(y, bd) * (1.0 / HEAD)
    yc = y - mean
    var = _seg(yc * yc, bd) * (1.0 / HEAD)
    yg = yc * lax.rsqrt(var + GN_EPS) * gn_w + gn_b
    bonus = (_seg(r * kdf * rkf, bd) + _seg(r * kdb * rkb, bd)) * v
    return (yg + bonus) * g


def _halo_specs(width, col_blk, tb, t):
    nb = t // SUBLANES
    step = tb // SUBLANES
    main = pl.BlockSpec((tb, width), lambda i: (i, col_blk))
    prev = pl.BlockSpec((SUBLANES, width), lambda i: (jnp.maximum(i * step - 1, 0), col_blk))
    nxt = pl.BlockSpec((SUBLANES, width), lambda i: (jnp.minimum((i + 1) * step, nb - 1), col_blk))
    return [main, prev, nxt]


def _neighbours(z, prev8, next8, first, last):
    tb = z.shape[0]
    row = lax.broadcasted_iota(jnp.int32, z.shape, 0)
    prow = jnp.where(first, 0.0, prev8[SUBLANES - 1:SUBLANES, :])
    nrow = jnp.where(last, 0.0, next8[0:1, :])
    down = jnp.where(row == 0, prow, pltpu.roll(z, 1, 0))
    up = jnp.where(row == tb - 1, nrow, pltpu.roll(z, tb - 1, 0))
    return down, up


def _shift_conv_fwd(p, mu, conv_w, seq, *, name, tb=ROW_TILE):
    t = p.shape[0]
    per_seq = seq // tb

    def kern(p_ref, pp_ref, pn_ref, mu_ref, cw_ref, pss_ref, oc_ref):
        i = pl.program_id(0)
        first = (i % per_seq) == 0
        last = (i % per_seq) == per_seq - 1
        ps = p_ref[:, :D_SP]
        down, up = _neighbours(ps, pp_ref[:, :D_SP], pn_ref[:, :D_SP], first, last)
        pss_ref[...] = ps + mu_ref[...] * (0.5 * (down + up) - ps)
        gb = p_ref[:, D_SP:D_SP + D_CONV]
        u = p_ref[:, D_SP + D_CONV:D_SP + 2 * D_CONV] * p_ref[:, D_SP + 2 * D_CONV:]
        u_p = pp_ref[:, D_SP + D_CONV:D_SP + 2 * D_CONV] * pp_ref[:, D_SP + 2 * D_CONV:]
        u_n = pn_ref[:, D_SP + D_CONV:D_SP + 2 * D_CONV] * pn_ref[:, D_SP + 2 * D_CONV:]
        udown, uup = _neighbours(u, u_p, u_n, first, last)
        oc_ref[...] = gb * (cw_ref[0:1, :] * udown + cw_ref[1:2, :] * u + cw_ref[2:3, :] * uup)

    return pl.pallas_call(
        kern,
        out_shape=[jax.ShapeDtypeStruct((t, D_SP), F32), jax.ShapeDtypeStruct((t, D_CONV), F32)],
        grid=(t // tb,),
        in_specs=_halo_specs(D_INP, 0, tb, t) + [pl.BlockSpec((1, D_SP), lambda i: (0, 0)),
                                                 pl.BlockSpec((SUBLANES, D_CONV), lambda i: (0, 0))],
        out_specs=[pl.BlockSpec((tb, D_SP), lambda i: (i, 0)), pl.BlockSpec((tb, D_CONV), lambda i: (i, 0))],
        compiler_params=_params(("parallel",)), name=name)(p, p, p, mu, conv_w)


def _shift_conv_bwd(p, d_pss, d_o, mu, conv_w, seq, *, name, tb=ROW_TILE):
    t = p.shape[0]
    per_seq = seq // tb

    def kern(p_ref, pp_ref, pn_ref, d_ref, dp_ref, dn_ref, do_ref, dop_ref, don_ref, mu_ref, cw_ref,
             out_ref, dmu_ref, dcw_ref):
        i = pl.program_id(0)
        first = (i % per_seq) == 0
        last = (i % per_seq) == per_seq - 1

        @pl.when(i == 0)
        def _():
            dmu_ref[...] = jnp.zeros_like(dmu_ref)
            dcw_ref[...] = jnp.zeros_like(dcw_ref)

        mu_v = mu_ref[...]
        ps = p_ref[:, :D_SP]
        down, up = _neighbours(ps, pp_ref[:, :D_SP], pn_ref[:, :D_SP], first, last)
        d = d_ref[...]
        ddown, dup = _neighbours(d, dp_ref[...], dn_ref[...], first, last)
        out_ref[:, :D_SP] = d - mu_v * d + 0.5 * (mu_v * ddown + mu_v * dup)
        dmu_ref[...] += _colsum(d * (0.5 * (down + up) - ps))

        def parts(ref):
            return (ref[:, D_SP:D_SP + D_CONV], ref[:, D_SP + D_CONV:D_SP + 2 * D_CONV],
                    ref[:, D_SP + 2 * D_CONV:])

        gb, gc, hh = parts(p_ref)
        gb_p, gc_p, hh_p = parts(pp_ref)
        gb_n, gc_n, hh_n = parts(pn_ref)
        u = gc * hh
        udown, uup = _neighbours(u, gc_p * hh_p, gc_n * hh_n, first, last)
        cw0, cw1, cw2 = cw_ref[0:1, :], cw_ref[1:2, :], cw_ref[2:3, :]
        do = do_ref[...]
        duc = do * gb
        ducdown, ducup = _neighbours(duc, dop_ref[...] * gb_p, don_ref[...] * gb_n, first, last)
        du = cw0 * ducup + cw1 * duc + cw2 * ducdown
        out_ref[:, D_SP:D_SP + D_CONV] = do * (cw0 * udown + cw1 * u + cw2 * uup)
        out_ref[:, D_SP + D_CONV:D_SP + 2 * D_CONV] = du * hh
        out_ref[:, D_SP + 2 * D_CONV:] = du * gc
        dcw_ref[0:1, :] += _colsum(duc * udown)
        dcw_ref[1:2, :] += _colsum(duc * u)
        dcw_ref[2:3, :] += _colsum(duc * uup)

    return pl.pallas_call(
        kern,
        out_shape=[jax.ShapeDtypeStruct((t, D_INP), F32), jax.ShapeDtypeStruct((1, D_SP), F32),
                   jax.ShapeDtypeStruct((SUBLANES, D_CONV), F32)],
        grid=(t // tb,),
        in_specs=(_halo_specs(D_INP, 0, tb, t) + _halo_specs(D_SP, 0, tb, t) + _halo_specs(D_CONV, 1, tb, t)
                  + [pl.BlockSpec((1, D_SP), lambda i: (0, 0)),
                     pl.BlockSpec((SUBLANES, D_CONV), lambda i: (0, 0))]),
        out_specs=[pl.BlockSpec((tb, D_INP), lambda i: (i, 0)), pl.BlockSpec((1, D_SP), lambda i: (0, 0)),
                   pl.BlockSpec((SUBLANES, D_CONV), lambda i: (0, 0))],
        compiler_params=_params(("arbitrary",)), name=name)(p, p, p, d_pss, d_pss, d_pss, d_o, d_o, d_o, mu, conv_w)


N_CHAIN = 16
V_LO = LANES // N_CHAIN
V_HI = HEAD // V_LO
N_GROUP = LANES // N_CHAIN
G_KK, G_R, G_W, G_B, G_KD = 0, 1, (2, 3), (4, 5), (6, 7)


def _group(x, j, lane):
    g = pltpu.roll(x, (LANES - N_CHAIN * j) % LANES, 1) if j else x
    g = jnp.where(lane < N_CHAIN, g, pltpu.roll(g, N_CHAIN, 1))
    g = jnp.where(lane < 2 * N_CHAIN, g, pltpu.roll(g, 2 * N_CHAIN, 1))
    return jnp.where(lane < 4 * N_CHAIN, g, pltpu.roll(g, 4 * N_CHAIN, 1))


def _scan_inputs(x, d, lane):
    return [_group(x, j, lane) for j in (G_KK, G_R, G_W[d], G_B[d], G_KD[d])]


def _lane_scan_fwd(xall, v_l, *, name):
    steps = xall.shape[0]
    nc = steps // SCAN_CHUNK
    mirror = lambda c: nc - 1 - c

    def kern(xf_ref, xb_ref, vf_ref, vb_ref, yf_ref, yb_ref, hist_ref, fin_ref, st_ref):
        c = pl.program_id(0)

        @pl.when(c == 0)
        def _():
            st_ref[...] = jnp.zeros_like(st_ref)

        row = lax.broadcasted_iota(jnp.int32, (V_HI, LANES), 0)
        lane = lax.broadcasted_iota(jnp.int32, (HEAD, LANES), 1)

        def step(i, carry):
            j = SCAN_CHUNK - 1 - i
            for d, (x_t, v_t, y_ref, at) in enumerate(((xf_ref[i], vf_ref[i], yf_ref, i),
                                                       (xb_ref[j], vb_ref[j], yb_ref, j))):
                kk_t, r_t, w_t, b_t, kd_t = _scan_inputs(x_t, d, lane)
                y_t = jnp.zeros((V_HI, LANES), F32)
                for vh in range(V_HI):
                    tile = d * V_HI + vh
                    state = st_ref[tile]
                    hist_ref[i, tile] = state
                    sa = _colsum(state * kk_t)
                    state = state * w_t - sa * b_t + v_t[vh:vh + 1, :] * kd_t
                    st_ref[tile] = state
                    y_t = jnp.where(row == vh, _colsum(state * r_t), y_t)
                y_ref[at] = y_t
            return carry

        lax.fori_loop(0, SCAN_CHUNK, step, 0)

        @pl.when(c == nc - 1)
        def _():
            fin_ref[...] = st_ref[...]

    def k_spec(fn):
        return pl.BlockSpec((SCAN_CHUNK, HEAD, LANES), lambda c: (fn(c), 0, 0))

    def v_spec(fn):
        return pl.BlockSpec((SCAN_CHUNK, V_HI, LANES), lambda c: (fn(c), 0, 0))

    same = lambda c: c
    st_shape = (2 * V_HI, HEAD, LANES)
    return pl.pallas_call(
        kern,
        out_shape=[jax.ShapeDtypeStruct((steps, V_HI, LANES), F32)] * 2
        + [jax.ShapeDtypeStruct((steps,) + st_shape, F32), jax.ShapeDtypeStruct(st_shape, F32)],
        grid=(nc,), in_specs=[k_spec(same), k_spec(mirror), v_spec(same), v_spec(mirror)],
        out_specs=[v_spec(same), v_spec(mirror),
                   pl.BlockSpec((SCAN_CHUNK,) + st_shape, lambda c: (c, 0, 0, 0)),
                   pl.BlockSpec(st_shape, lambda c: (0, 0, 0))],
        scratch_shapes=[pltpu.VMEM(st_shape, F32)],
        compiler_params=_params(("arbitrary",)), name=name)(xall, xall, v_l, v_l)


def _lane_scan_bwd(xall, v_l, dy_l, hist, fin, *, name):
    steps = xall.shape[0]
    nc = steps // SCAN_CHUNK
    back = lambda c: nc - 1 - c
    same = lambda c: c

    def kern(xf_ref, xb_ref, vf_ref, vb_ref, dyf_ref, dyb_ref, hist_ref, fin_ref,
             gf_ref, gb_ref, dvf_ref, dvb_ref, ds_ref, after_ref):
        c = pl.program_id(0)

        @pl.when(c == 0)
        def _():
            ds_ref[...] = jnp.zeros_like(ds_ref)
            after_ref[...] = fin_ref[...]

        row = lax.broadcasted_iota(jnp.int32, (V_HI, LANES), 0)
        lane = lax.broadcasted_iota(jnp.int32, (HEAD, LANES), 1)
        grp = lax.shift_right_logical(lane, jnp.full_like(lane, 4))

        def group_sum(x):
            x = x + pltpu.roll(x, 4 * N_CHAIN, 1)
            x = x + pltpu.roll(x, 2 * N_CHAIN, 1)
            return x + pltpu.roll(x, N_CHAIN, 1)

        def step(ii, carry):
            i = SCAN_CHUNK - 1 - ii
            j = ii
            for d, (x_t, v_t, dy_t, g_ref, dv_ref, at) in enumerate((
                    (xf_ref[i], vf_ref[i], dyf_ref[i], gf_ref, dvf_ref, i),
                    (xb_ref[j], vb_ref[j], dyb_ref[j], gb_ref, dvb_ref, j))):
                kk_t, r_t, w_t, b_t, kd_t = _scan_inputs(x_t, d, lane)
                dv_t = jnp.zeros((V_HI, LANES), F32)
                zero = jnp.zeros((HEAD, LANES), F32)
                dkk, dr, dw, db, dkd = zero, zero, zero, zero, zero
                for vh in range(V_HI):
                    tile = d * V_HI + vh
                    before = hist_ref[i, tile]
                    dy_r, v_r = dy_t[vh:vh + 1, :], v_t[vh:vh + 1, :]
                    g = ds_ref[tile] + dy_r * r_t
                    sa = _colsum(before * kk_t)
                    dsa = -_colsum(g * b_t)
                    dv_t = jnp.where(row == vh, _colsum(g * kd_t), dv_t)
                    dr = dr + after_ref[tile] * dy_r
                    dw = dw + g * before
                    dkd = dkd + g * v_r
                    db = db - g * sa
                    dkk = dkk + before * dsa
                    ds_ref[tile] = g * w_t + dsa * kk_t
                    after_ref[tile] = before
                out = jnp.where(grp == G_KK, group_sum(dkk), 0.0)
                out = jnp.where(grp == G_R, group_sum(dr), out)
                out = jnp.where(grp == G_W[d], group_sum(dw), out)
                out = jnp.where(grp == G_B[d], group_sum(db), out)
                out = jnp.where(grp == G_KD[d], group_sum(dkd), out)
                g_ref[at] = out
                dv_ref[at] = dv_t
            return carry

        lax.fori_loop(0, SCAN_CHUNK, step, 0)

    def k_spec(fn):
        return pl.BlockSpec((SCAN_CHUNK, HEAD, LANES), lambda c: (fn(c), 0, 0))

    def v_spec(fn):
        return pl.BlockSpec((SCAN_CHUNK, V_HI, LANES), lambda c: (fn(c), 0, 0))

    st_shape = (2 * V_HI, HEAD, LANES)
    return pl.pallas_call(
        kern,
        out_shape=[jax.ShapeDtypeStruct((steps, HEAD, LANES), F32)] * 2
        + [jax.ShapeDtypeStruct((steps, V_HI, LANES), F32)] * 2,
        grid=(nc,),
        in_specs=[k_spec(back), k_spec(same), v_spec(back), v_spec(same), v_spec(back), v_spec(same),
                  pl.BlockSpec((SCAN_CHUNK,) + st_shape, lambda c: (back(c), 0, 0, 0)),
                  pl.BlockSpec(st_shape, lambda c: (0, 0, 0))],
        out_specs=[k_spec(back), k_spec(same), v_spec(back), v_spec(same)],
        scratch_shapes=[pltpu.VMEM(st_shape, F32), pltpu.VMEM(st_shape, F32)],
        compiler_params=_params(("arbitrary",)), name=name)(xall, xall, v_l, v_l, dy_l, dy_l, hist, fin)


def _to_key_lanes(wide, bsz, seq):
    z = wide.reshape(bsz, seq, N_GROUP, N_HEAD, HEAD).transpose(1, 4, 2, 0, 3)
    return z.reshape(seq, HEAD, LANES)


def _from_key_lanes(g, bsz, seq):
    z = g.reshape(seq, HEAD, N_GROUP, bsz, N_HEAD).transpose(3, 0, 2, 4, 1)
    return z.reshape(bsz * seq, N_GROUP * D_RWKV)


def _to_value_lanes(a, bsz, seq):
    z = a.reshape(bsz, seq, N_HEAD, V_HI, V_LO).transpose(1, 3, 4, 0, 2)
    return z.reshape(seq, V_HI, LANES)


def _from_value_lanes(y, bsz, seq):
    z = y.reshape(seq, V_HI, V_LO, bsz, N_HEAD).transpose(3, 0, 4, 1, 2)
    return z.reshape(bsz * seq, D_RWKV)


K_HI = HEAD // SUBLANES


def _lane_group_sum(x):
    x = x + pltpu.roll(x, 4 * N_CHAIN, 1)
    x = x + pltpu.roll(x, 2 * N_CHAIN, 1)
    return x + pltpu.roll(x, N_CHAIN, 1)


def _key_rows(x_t, d):
    out = []
    for grp in (G_KK, G_R, G_W[d], G_B[d], G_KD[d]):
        blk = x_t[SUBLANES * grp:SUBLANES * (grp + 1), :]
        out.append([jnp.broadcast_to(blk[kh:kh + 1, :], (SUBLANES, LANES)) for kh in range(K_HI)])
    return out


def _tree_sum(terms):
    terms = list(terms)
    while len(terms) > 1:
        terms = [a + b for a, b in zip(terms[::2], terms[1::2])]
    return terms[0]


def _kscan_specs(nc):
    same = lambda c: c
    mirror = lambda c: nc - 1 - c

    def k_spec(fn):
        return pl.BlockSpec((SCAN_CHUNK, HEAD, LANES), lambda c: (fn(c), 0, 0))

    def v_spec(fn):
        return pl.BlockSpec((SCAN_CHUNK, SUBLANES, LANES), lambda c: (fn(c), 0, 0))

    return same, mirror, k_spec, v_spec


ST_SHAPE = (2, K_HI, V_HI, SUBLANES, LANES)


def _lane_group_index():
    lane = lax.broadcasted_iota(jnp.int32, (SUBLANES, LANES), 1)
    return lax.shift_right_logical(lane, jnp.full_like(lane, 4))


def _spread_groups(x, grp):
    rolled = [x] + [pltpu.roll(x, s * N_CHAIN, 1) for s in range(1, N_GROUP)]
    out = []
    for j in range(N_GROUP):
        t = rolled[(0 - j) % N_GROUP]
        for g in range(1, N_GROUP):
            t = jnp.where(grp == g, rolled[(g - j) % N_GROUP], t)
        out.append(t)
    return out


def _gather_groups(tiles, grp):
    total = None
    for s in range(N_GROUP):
        b = tiles[s % N_GROUP]
        for g in range(1, N_GROUP):
            b = jnp.where(grp == g, tiles[(g + s) % N_GROUP], b)
        b = pltpu.roll(b, s * N_CHAIN, 1) if s else b
        total = b if total is None else total + b
    return total


def _lane_group_sum_short(x):
    return _tree_sum([x] + [pltpu.roll(x, k * N_CHAIN, 1) for k in range(1, N_GROUP)])


def _kscan_fwd(xall, v_c, *, name):
    steps = xall.shape[0]
    nc = steps // SCAN_CHUNK
    same, mirror, k_spec, v_spec = _kscan_specs(nc)

    def kern(xf_ref, xb_ref, vf_ref, vb_ref, yf_ref, yb_ref, hist_ref, fin_ref, st_ref):
        c = pl.program_id(0)

        @pl.when(c == 0)
        def _():
            st_ref[...] = jnp.zeros_like(st_ref)

        grp = _lane_group_index()

        def step(i, carry):
            j = SCAN_CHUNK - 1 - i
            for d, (x_t, v_t, y_ref, at) in enumerate(((xf_ref[i], vf_ref[i], yf_ref, i),
                                                       (xb_ref[j], vb_ref[j], yb_ref, j))):
                kk_r, r_r, w_r, b_r, kd_r = _key_rows(x_t, d)
                v_b = _spread_groups(v_t, grp)
                y_p = []
                for vh in range(V_HI):
                    st = [st_ref[d, kh, vh] for kh in range(K_HI)]
                    for kh in range(K_HI):
                        hist_ref[i, d, kh, vh] = st[kh]
                    sa = _lane_group_sum_short(_tree_sum(st[kh] * kk_r[kh] for kh in range(K_HI)))
                    new = [st[kh] * w_r[kh] - sa * b_r[kh] + v_b[vh] * kd_r[kh] for kh in range(K_HI)]
                    for kh in range(K_HI):
                        st_ref[d, kh, vh] = new[kh]
                    y_p.append(_tree_sum(new[kh] * r_r[kh] for kh in range(K_HI)))
                y_ref[at] = _gather_groups(y_p, grp)
            return carry

        lax.fori_loop(0, SCAN_CHUNK, step, 0)

        @pl.when(c == nc - 1)
        def _():
            fin_ref[...] = st_ref[...]

    return pl.pallas_call(
        kern,
        out_shape=[jax.ShapeDtypeStruct((steps, SUBLANES, LANES), F32)] * 2
        + [jax.ShapeDtypeStruct((steps,) + ST_SHAPE, F32), jax.ShapeDtypeStruct(ST_SHAPE, F32)],
        grid=(nc,), in_specs=[k_spec(same), k_spec(mirror), v_spec(same), v_spec(mirror)],
        out_specs=[v_spec(same), v_spec(mirror),
                   pl.BlockSpec((SCAN_CHUNK,) + ST_SHAPE, lambda c: (c, 0, 0, 0, 0, 0)),
                   pl.BlockSpec(ST_SHAPE, lambda c: (0, 0, 0, 0, 0))],
        scratch_shapes=[pltpu.VMEM(ST_SHAPE, F32)],
        compiler_params=_params(("arbitrary",)), name=name)(xall, xall, v_c, v_c)


def _kscan_bwd(xall, v_c, dy_c, hist, fin, *, name):
    steps = xall.shape[0]
    nc = steps // SCAN_CHUNK
    same, back, k_spec, v_spec = _kscan_specs(nc)

    def kern(xf_ref, xb_ref, vf_ref, vb_ref, dyf_ref, dyb_ref, hist_ref, fin_ref,
             gf_ref, gb_ref, dvf_ref, dvb_ref, ds_ref, after_ref):
        c = pl.program_id(0)

        @pl.when(c == 0)
        def _():
            ds_ref[...] = jnp.zeros_like(ds_ref)
            after_ref[...] = fin_ref[...]

        grp = _lane_group_index()
        row = lax.broadcasted_iota(jnp.int32, (SUBLANES, LANES), 0)

        def step(ii, carry):
            i = SCAN_CHUNK - 1 - ii
            j = ii
            for d, (x_t, v_t, dy_t, g_ref, dv_ref, at) in enumerate((
                    (xf_ref[i], vf_ref[i], dyf_ref[i], gf_ref, dvf_ref, i),
                    (xb_ref[j], vb_ref[j], dyb_ref[j], gb_ref, dvb_ref, j))):
                kk_r, r_r, w_r, b_r, kd_r = _key_rows(x_t, d)
                v_s, dy_s = _spread_groups(v_t, grp), _spread_groups(dy_t, grp)
                ks = range(K_HI)
                zero = jnp.zeros((SUBLANES, LANES), F32)
                dkk, dr, dw, db, dkd = ([zero] * K_HI for _ in range(5))
                dv_p = []
                for vh in range(V_HI):
                    v_b, dy_b = v_s[vh], dy_s[vh]
                    before = [hist_ref[i, d, kh, vh] for kh in ks]
                    g = [ds_ref[d, kh, vh] + dy_b * r_r[kh] for kh in ks]
                    dsa = -_lane_group_sum_short(_tree_sum(g[kh] * b_r[kh] for kh in ks))
                    sa = _lane_group_sum(_tree_sum(before[kh] * kk_r[kh] for kh in ks))
                    dv_p.append(_tree_sum(g[kh] * kd_r[kh] for kh in ks))
                    dr = [dr[kh] + after_ref[d, kh, vh] * dy_b for kh in ks]
                    dw = [dw[kh] + g[kh] * before[kh] for kh in ks]
                    dkd = [dkd[kh] + g[kh] * v_b for kh in ks]
                    db = [db[kh] - g[kh] * sa for kh in ks]
                    dkk = [dkk[kh] + before[kh] * dsa for kh in ks]
                    for kh in ks:
                        ds_ref[d, kh, vh] = g[kh] * w_r[kh] + dsa * kk_r[kh]
                        after_ref[d, kh, vh] = before[kh]
                dv_ref[at] = _gather_groups(dv_p, grp)
                blocks = {G_KK: dkk, G_R: dr, G_W[d]: dw, G_B[d]: db, G_KD[d]: dkd}
                for gi in range(N_GROUP):
                    blk = zero
                    if gi in blocks:
                        for kh in ks:
                            blk = jnp.where(row == kh, _colsum(blocks[gi][kh]), blk)
                    g_ref[at, SUBLANES * gi:SUBLANES * (gi + 1), :] = blk
            return carry

        lax.fori_loop(0, SCAN_CHUNK, step, 0)

    return pl.pallas_call(
        kern,
        out_shape=[jax.ShapeDtypeStruct((steps, HEAD, LANES), F32)] * 2
        + [jax.ShapeDtypeStruct((steps, SUBLANES, LANES), F32)] * 2,
        grid=(nc,),
        in_specs=[k_spec(back), k_spec(same), v_spec(back), v_spec(same), v_spec(back), v_spec(same),
                  pl.BlockSpec((SCAN_CHUNK,) + ST_SHAPE, lambda c: (back(c), 0, 0, 0, 0, 0)),
                  pl.BlockSpec(ST_SHAPE, lambda c: (0, 0, 0, 0, 0))],
        out_specs=[k_spec(back), k_spec(same), v_spec(back), v_spec(same)],
        scratch_shapes=[pltpu.VMEM(ST_SHAPE, F32), pltpu.VMEM(ST_SHAPE, F32)],
        compiler_params=_params(("arbitrary",)), name=name)(xall, xall, v_c, v_c, dy_c, dy_c, hist, fin)


def _to_key_rows(wide, bsz, seq):
    z = wide.reshape(bsz, seq, N_GROUP, N_HEAD, K_HI, SUBLANES).transpose(1, 2, 4, 5, 0, 3)
    return z.reshape(seq, HEAD, LANES)


def _from_key_rows(g, bsz, seq):
    z = g.reshape(seq, N_GROUP, K_HI, SUBLANES, bsz, N_HEAD).transpose(4, 0, 1, 5, 2, 3)
    return z.reshape(bsz * seq, N_GROUP * D_RWKV)


def _to_value_rows(a, bsz, seq):
    z = a.reshape(bsz, seq, N_HEAD, V_HI, SUBLANES).transpose(1, 4, 3, 0, 2)
    return z.reshape(seq, SUBLANES, LANES)


def _from_value_rows(y, bsz, seq):
    z = y.reshape(seq, SUBLANES, V_HI, bsz, N_HEAD).transpose(3, 0, 4, 2, 1)
    return z.reshape(bsz * seq, D_RWKV)


def _pad_cols(a, segs):
    out, off = [], 0
    for w, wp in segs:
        out.append(a[..., off:off + w])
        if wp > w:
            out.append(jnp.zeros(a.shape[:-1] + (wp - w,), a.dtype))
        off += w
    return jnp.concatenate(out, axis=-1)


def _unpad_cols(a, segs):
    out, off = [], 0
    for w, wp in segs:
        out.append(a[..., off:off + w])
        off += wp
    return jnp.concatenate(out, axis=-1)


P_SEGS = ((3 * D_RWKV, 3 * D_RWKV), (D_LORA, 128), (D_LORA, 128), (D_GATE, 256), (3 * D_CONV, 3 * D_CONV))
S_SEGS = P_SEGS[:4]


def _pad_rows(a, rows):
    return jnp.concatenate([a, jnp.zeros((rows - a.shape[0], a.shape[1]), a.dtype)], axis=0)


def _local_step(x, target, w):
    bsz, seq, _ = x.shape
    t = bsz * seq
    x2d = x.reshape(t, D_MODEL)
    tg2d = target.reshape(t, D_MODEL)
    row = lambda a: a.reshape(1, -1).astype(F32)

    w_in = _pad_cols(w["w_in"][0], P_SEGS)
    mu = _pad_cols(row(w["mu_shift"]), S_SEGS)
    wupf, wupb, aupf, aupb = (_pad_rows(w[n][0].astype(F32), 128) for n in ("w_up_f", "w_up_b", "a_up_f", "a_up_b"))
    gup = _pad_rows(w["g_up"][0].astype(F32), 256)
    conv_w = _pad_rows(w["conv_w"][0].astype(F32), SUBLANES)
    w_out, w_gate, w_up, w_down = w["w_out"][0], w["w_gate"][0], w["w_up"][0], w["w_down"][0]
    norm1, norm2, normf = row(w["norm1_w"]), row(w["norm2_w"]), row(w["norm_f_w"])
    vec = {n: row(w[n]) for n in VEC}
    head_of = jnp.arange(LANES) // HEAD
    bd = (head_of[:, None] == head_of[None, :]).astype(F32)
    pre_consts = [vec["k_k"], vec["w0_f"], vec["w0_b"], vec["a0_f"], vec["a0_b"], vec["k_a_f"], vec["k_a_b"],
                  wupf, wupb, aupf, aupb, gup, bd]
    post_consts = [vec["gn_w"], vec["gn_b"], vec["r_k_f"], vec["r_k_b"], bd]

    h1, = _rowwise(_rms, [x2d], [norm1], [D_MODEL], [], name="rms1_fwd")
    p = _mm(h1, w_in, name="mm_in")
    pss, oconv = _shift_conv_fwd(p, mu, conv_w, seq, name="shift_conv_fwd")
    pre_rows = [(pss, 0, 512), (pss, 1, 512), (pss, XW0 // 128, 128), (pss, XA0 // 128, 128), (pss, XG0 // 256, 256)]
    sc, g = _rowwise(_prescan_math, pre_rows, pre_consts, [[D_RWKV] * N_GROUP, D_RWKV], [], name="prescan_fwd")
    xall = _to_key_rows(sc, bsz, seq)
    v_l = _to_value_rows(pss[:, 2 * D_RWKV:3 * D_RWKV], bsz, seq)
    y_f, y_b, hist, fin = _kscan_fwd(xall, v_l, name="scan_fwd")
    y = _from_value_rows(y_f + y_b, bsz, seq)
    post_rows = [y, (pss, 0, 512), (pss, 2, 512), (sc, G_KD[0], 512), (sc, G_KD[1], 512), g]

    def post_fwd(y_, r_, v_, kdf_, kdb_, g_, oc_, *consts):
        return _postscan_math(y_, r_, v_, kdf_, kdb_, g_, *consts), oc_

    o, = _rowwise(post_fwd, post_rows + [oconv], post_consts, [[D_RWKV, D_CONV]], [], name="postscan_fwd")
    x1 = _mm(o, w_out, add=x2d, name="mm_out")
    h2, = _rowwise(_rms, [x1], [norm2], [D_MODEL], [], name="rms2_fwd")
    gg = _mm(h2, w_gate, name="mm_gate")
    uu = _mm(h2, w_up, name="mm_up")
    ff, = _rowwise(lambda a, c: jax.nn.silu(a) * c, [gg, uu], [], [D_FF], [], name="swiglu_fwd")
    x2 = _mm(ff, w_down, add=x1, name="mm_down")

    def final(x_, tg_, wn_):
        yo, vjp = jax.vjp(_rms, x_, wn_)
        err = yo - tg_
        dx_, dwn_ = vjp(err * (1.0 / D_MODEL))
        part = jnp.sum(jnp.sum(err * err, axis=1, keepdims=True), axis=0, keepdims=True) * (0.5 / D_MODEL)
        return dx_, part + jnp.zeros((1, LANES), F32), dwn_

    dx2, loss_acc, d_normf = _rowwise(final, [x2, tg2d], [normf], [D_MODEL], [(1, LANES), (1, D_MODEL)],
                                      name="loss_head")
    dff = _mm(dx2, w_down, tb=True, name="mm_down_dx")
    g_w_down = _mm(ff, dx2, ta=True, name="mm_down_dw")

    def swiglu_bwd(a, c, d):
        _, vjp = jax.vjp(lambda a_, c_: jax.nn.silu(a_) * c_, a, c)
        return vjp(d)

    dgg, duu = _rowwise(swiglu_bwd, [gg, uu, dff], [], [D_FF, D_FF], [], name="swiglu_bwd")
    dh2 = _mm(dgg, w_gate, tb=True, name="mm_gate_dx")
    dh2 = _mm(duu, w_up, tb=True, add=dh2, name="mm_up_dx")
    g_w_gate = _mm(h2, dgg, ta=True, name="mm_gate_dw")
    g_w_up = _mm(h2, duu, ta=True, name="mm_up_dw")

    def rms_bwd(x_, dh_, dres_, wn_):
        _, vjp = jax.vjp(_rms, x_, wn_)
        dx_, dwn_ = vjp(dh_)
        return dx_ + dres_, dwn_

    dx1, d_norm2 = _rowwise(rms_bwd, [x1, dh2, dx2], [norm2], [D_MODEL], [(1, D_MODEL)], name="rms2_bwd")
    do = _mm(dx1, w_out, tb=True, name="mm_out_dx")
    g_w_out = _mm(o, dx1, ta=True, name="mm_out_dw")

    def post_bwd(y_, r_, v_, kdf_, kdb_, g_, do_, *consts):
        _, vjp = jax.vjp(lambda *a: _postscan_math(*a, consts[4]), y_, r_, v_, kdf_, kdb_, g_, *consts[:4])
        return vjp(do_)

    (dy, dr_c, dv_c, dkdf_c, dkdb_c, dg, d_gn_w, d_gn_b, d_rkf, d_rkb) = _rowwise(
        post_bwd, post_rows + [(do, 0, 512)], post_consts, [D_RWKV] * 6, [(1, D_RWKV)] * 4, name="postscan_bwd")
    dy_l = _to_value_rows(dy, bsz, seq)
    g_f, g_b, dv_f, dv_b = _kscan_bwd(xall, v_l, dy_l, hist, fin, name="scan_bwd")
    dsc = _from_key_rows(g_f + g_b, bsz, seq)
    dv_s = _from_value_rows(dv_f + dv_b, bsz, seq)

    def pre_bwd(r_, k_, xw_, xa_, xg_, dkk_, dr_s, dwf_, dwb_, dbf_, dbb_, dkdf_s, dkdb_s,
                dr_c_, dv_c_, dv_s_, dkdf_c_, dkdb_c_, dg_, *consts):
        _, vjp = jax.vjp(lambda *a: _prescan_math(*a, consts[-1]), r_, k_, xw_, xa_, xg_, *consts[:-1])
        grads = vjp((dkk_, dr_s + dr_c_, dwf_, dwb_, dbf_, dbb_, dkdf_s + dkdf_c_, dkdb_s + dkdb_c_, dg_))
        dr_, dk_, dxw_, dxa_, dxg_ = grads[:5]
        return (dr_, dk_, dv_c_ + dv_s_, dxw_, dxa_, dxg_) + tuple(grads[5:])

    pre_b_rows = (pre_rows + [(dsc, j, 512) for j in range(N_GROUP)]
                  + [dr_c, dv_c, dv_s, dkdf_c, dkdb_c, dg])
    pre_b = _rowwise(pre_bwd, pre_b_rows, pre_consts, [[512, 512, 512, 128, 128, 256]],
                     [(1, D_RWKV)] * 7 + [(128, D_RWKV)] * 4 + [(256, D_RWKV)], name="prescan_bwd")
    d_pss = pre_b[0]
    d_kk_, d_w0f, d_w0b, d_a0f, d_a0b, d_kaf, d_kab, d_wupf, d_wupb, d_aupf, d_aupb, d_gup = pre_b[1:]
    dp, d_mu, d_conv = _shift_conv_bwd(p, d_pss, do, mu, conv_w, seq, name="shift_conv_bwd")
    dh1 = _mm(dp, w_in, tb=True, name="mm_in_dx")
    g_w_in = _mm(h1, dp, ta=True, name="mm_in_dw")
    dx, d_norm1 = _rowwise(rms_bwd, [x2d, dh1, dx1], [norm1], [D_MODEL], [(1, D_MODEL)], name="rms1_bwd")

    grads = {
        "norm1_w": d_norm1, "w_in": _unpad_cols(g_w_in, P_SEGS)[None], "mu_shift": _unpad_cols(d_mu, S_SEGS),
        "w_up_f": d_wupf[None, :D_LORA], "w0_f": d_w0f, "w_up_b": d_wupb[None, :D_LORA], "w0_b": d_w0b,
        "a_up_f": d_aupf[None, :D_LORA], "a0_f": d_a0f, "a_up_b": d_aupb[None, :D_LORA], "a0_b": d_a0b,
        "g_up": d_gup[None, :D_GATE], "k_k": d_kk_, "k_a_f": d_kaf, "k_a_b": d_kab,
        "r_k_f": d_rkf, "r_k_b": d_rkb, "gn_w": d_gn_w, "gn_b": d_gn_b, "conv_w": d_conv[None, :3],
        "w_out": g_w_out[None], "norm2_w": d_norm2, "w_gate": g_w_gate[None], "w_up": g_w_up[None],
        "w_down": g_w_down[None], "norm_f_w": d_normf,
    }
    return loss_acc[0, 0], dx.reshape(bsz, seq, D_MODEL), grads


def _hbm_specs(n):
    return [pl.BlockSpec(memory_space=pl.ANY)] * n


def _all_gather(arrs, *, name):
    n = len(arrs)

    def body(*refs):
        x_refs, out_refs = refs[:n], refs[n:2 * n]
        send_sems, recv_sems, local_sems = refs[2 * n:]
        x, y, c = lax.axis_index("x"), lax.axis_index("y"), lax.axis_index("c")
        me, sibling = (x, y, c), (x, y, 1 - c)
        chips = [(1 - x, y), (x, 1 - y), (1 - x, 1 - y)]

        def slot(a, px, py, pc):
            return out_refs[a].at[4 * px + 2 * py + pc]

        def copy(a, k, block, to, src=None):
            return pltpu.make_async_remote_copy(
                src_ref=slot(a, *block) if src is None else src, dst_ref=slot(a, *block),
                send_sem=send_sems.at[k, a], recv_sem=recv_sems.at[k, a],
                device_id=to, device_id_type=pl.DeviceIdType.MESH)

        mine = [pltpu.make_async_copy(x_refs[a], slot(a, *me), local_sems.at[a]) for a in range(n)]
        for cp in mine:
            cp.start()
        first = []
        for a in range(n):
            first.append(copy(a, 0, me, sibling, src=x_refs[a]))
            first += [copy(a, 1 + j, me, (*chip, c), src=x_refs[a]) for j, chip in enumerate(chips)]
        for cp in first:
            cp.start()
        passed = []
        for j, chip in enumerate(chips):
            for a in range(n):
                copy(a, 1 + j, (*chip, c), me).wait_recv()
                cp = copy(a, 4 + j, (*chip, c), sibling)
                cp.start()
                passed.append(cp)
        for a in range(n):
            copy(a, 0, sibling, me).wait_recv()
            for j, chip in enumerate(chips):
                copy(a, 4 + j, (*chip, 1 - c), me).wait_recv()
        for cp in first + passed:
            cp.wait_send()
        for cp in mine:
            cp.wait()

    return pl.pallas_call(
        body, out_shape=[jax.ShapeDtypeStruct((N_DEV,) + a.shape, a.dtype) for a in arrs],
        in_specs=_hbm_specs(n), out_specs=_hbm_specs(n),
        scratch_shapes=[pltpu.SemaphoreType.DMA((7, n)), pltpu.SemaphoreType.DMA((7, n)),
                        pltpu.SemaphoreType.DMA((n,))],
        name=name)(*arrs)


def _exchange(sliced, whole, *, name):
    arrs = list(sliced) + list(whole)
    n, n_sliced = len(arrs), len(sliced)

    def body(*refs):
        in_refs, out_refs = refs[:n], refs[n:2 * n]
        send_sems, recv_sems, local_sems = refs[2 * n:]
        x, y, c = lax.axis_index("x"), lax.axis_index("y"), lax.axis_index("c")
        me = 4 * x + 2 * y + c

        def src(a, dev):
            return in_refs[a].at[dev] if a < n_sliced else in_refs[a]

        local = [pltpu.make_async_copy(src(a, me), out_refs[a].at[me], local_sems.at[a]) for a in range(n)]
        for cp in local:
            cp.start()
        copies = []
        for k in range(1, N_DEV):
            px = 1 - x if k & 4 else x
            py = 1 - y if k & 2 else y
            pc = 1 - c if k & 1 else c
            for a in range(n):
                copies.append(pltpu.make_async_remote_copy(
                    src_ref=src(a, 4 * px + 2 * py + pc), dst_ref=out_refs[a].at[me],
                    send_sem=send_sems.at[k - 1, a], recv_sem=recv_sems.at[k - 1, a],
                    device_id=(px, py, pc), device_id_type=pl.DeviceIdType.MESH))
        for cp in copies:
            cp.start()
        for cp in copies:
            cp.wait()
        for cp in local:
            cp.wait()

    out_shape = [jax.ShapeDtypeStruct(a.shape if i < n_sliced else (N_DEV,) + a.shape, a.dtype)
                 for i, a in enumerate(arrs)]
    return pl.pallas_call(
        body, out_shape=out_shape, in_specs=_hbm_specs(n), out_specs=_hbm_specs(n),
        scratch_shapes=[pltpu.SemaphoreType.DMA((7, n)), pltpu.SemaphoreType.DMA((7, n)),
                        pltpu.SemaphoreType.DMA((n,))],
        name=name)(*arrs)


def _adam_math(g, w, m, v):
    nm = ADAM_B1 * m + (1.0 - ADAM_B1) * g
    nv = ADAM_B2 * v + (1.0 - ADAM_B2) * (g * g)
    m_hat = nm / (1.0 - ADAM_B1 ** ADAM_STEP)
    v_hat = nv / (1.0 - ADAM_B2 ** ADAM_STEP)
    return -ADAM_LR * (m_hat / (jnp.sqrt(v_hat) + ADAM_EPS) + ADAM_WD * w), nm, nv


def _slot_sum(ref):
    g = ref[0].astype(F32)
    for s in range(1, N_DEV):
        g = g + ref[s].astype(F32)
    return g


def _adamw_big(parts, w, m, v, *, name):
    _, rws, cols = w.shape
    tr = _tile(rws, (256, 176, 128))

    def kern(p_ref, w_ref, m_ref, v_ref, g_ref, d_ref, nm_ref, nv_ref):
        g = _slot_sum(p_ref)
        g_ref[...] = g
        d_ref[...], nm_ref[...], nv_ref[...] = _adam_math(g, w_ref[...], m_ref[...], v_ref[...])

    spec = pl.BlockSpec((1, tr, cols), lambda i: (0, i, 0))
    return pl.pallas_call(
        kern, out_shape=[jax.ShapeDtypeStruct(w.shape, F32)] * 4, grid=(rws // tr,),
        in_specs=[pl.BlockSpec((N_DEV, 1, tr, cols), lambda i: (0, 0, i, 0)), spec, spec, spec],
        out_specs=[spec] * 4, compiler_params=_params(("parallel",)), name=name)(parts, w, m, v)


def _adamw_small(lora_parts, vec_parts, wide_parts, wmv, *, name):
    names = LORA + VEC + WIDE
    n_l, n = len(LORA), len(names)
    flat = [a for trip in wmv for a in trip]

    def kern(*refs):
        l_refs, vec_ref, wide_ref = refs[:n_l], refs[n_l], refs[n_l + 1]
        in_refs = refs[n_l + 2:n_l + 2 + 3 * n]
        out_refs = refs[n_l + 2 + 3 * n:]
        vec_sum, wide_sum = _slot_sum(vec_ref), _slot_sum(wide_ref)
        for i, nm in enumerate(names):
            w_ref, m_ref, v_ref = in_refs[3 * i:3 * i + 3]
            if i < n_l:
                g = _slot_sum(l_refs[i])
            elif nm in VEC:
                g = vec_sum[i - n_l:i - n_l + 1, :]
            else:
                g = wide_sum[WIDE.index(nm):WIDE.index(nm) + 1, :w_ref.shape[-1]]
            o = out_refs[4 * i:4 * i + 4]
            o[0][...] = g
            o[1][...], o[2][...], o[3][...] = _adam_math(g, w_ref[...], m_ref[...], v_ref[...])

    out_shape = [jax.ShapeDtypeStruct(trip[0].shape, F32) for trip in wmv for _ in range(4)]
    outs = pl.pallas_call(kern, out_shape=out_shape, name=name,
                          compiler_params=pltpu.CompilerParams(vmem_limit_bytes=VMEM_LIMIT))(
        *lora_parts, vec_parts, wide_parts, *flat)
    return [tuple(outs[4 * i:4 * i + 4]) for i in range(n)]


def _to_slots(g, axis):
    _, rws, cols = g.shape
    if axis == 1:
        return g.reshape(N_DEV, 1, rws // N_DEV, cols)
    return g.reshape(1, rws, N_DEV, cols // N_DEV).transpose(2, 0, 1, 3)


def _from_slots(got, axis):
    _, _, rws, cols = got.shape
    if axis == 1:
        return got.reshape(1, N_DEV * rws, cols)
    return got.transpose(1, 2, 0, 3).reshape(1, rws, N_DEV * cols)


def _pad_lanes(a, width):
    return jnp.concatenate([a, jnp.zeros(a.shape[:-1] + (width - a.shape[-1],), a.dtype)], axis=-1)


def kernel(x, norm1_w, w_in, mu_shift, w_up_f, w0_f, w_up_b, w0_b, a_up_f, a0_f, a_up_b, a0_b, g_up, k_k, k_a_f, k_a_b, r_k_f, r_k_b, gn_w, gn_b, conv_w, w_out, norm2_w, w_gate, w_up, w_down, norm_f_w, loss_target, m_norm1_w, m_w_in, m_mu_shift, m_w_up_f, m_w0_f, m_w_up_b, m_w0_b, m_a_up_f, m_a0_f, m_a_up_b, m_a0_b, m_g_up, m_k_k, m_k_a_f, m_k_a_b, m_r_k_f, m_r_k_b, m_gn_w, m_gn_b, m_conv_w, m_w_out, m_norm2_w, m_w_gate, m_w_up, m_w_down, m_norm_f_w, v_norm1_w, v_w_in, v_mu_shift, v_w_up_f, v_w0_f, v_w_up_b, v_w0_b, v_a_up_f, v_a0_f, v_a_up_b, v_a0_b, v_g_up, v_k_k, v_k_a_f, v_k_a_b, v_r_k_f, v_r_k_b, v_gn_w, v_gn_b, v_conv_w, v_w_out, v_norm2_w, v_w_gate, v_w_up, v_w_down, v_norm_f_w):
    local = dict(norm1_w=norm1_w, w_in=w_in, mu_shift=mu_shift, w_up_f=w_up_f, w0_f=w0_f, w_up_b=w_up_b,
                 w0_b=w0_b, a_up_f=a_up_f, a0_f=a0_f, a_up_b=a_up_b, a0_b=a0_b, g_up=g_up, k_k=k_k, k_a_f=k_a_f,
                 k_a_b=k_a_b, r_k_f=r_k_f, r_k_b=r_k_b, gn_w=gn_w, gn_b=gn_b, conv_w=conv_w, w_out=w_out,
                 norm2_w=norm2_w, w_gate=w_gate, w_up=w_up, w_down=w_down, norm_f_w=norm_f_w)
    mom_m = dict(norm1_w=m_norm1_w, w_in=m_w_in, mu_shift=m_mu_shift, w_up_f=m_w_up_f, w0_f=m_w0_f,
                 w_up_b=m_w_up_b, w0_b=m_w0_b, a_up_f=m_a_up_f, a0_f=m_a0_f, a_up_b=m_a_up_b, a0_b=m_a0_b,
                 g_up=m_g_up, k_k=m_k_k, k_a_f=m_k_a_f, k_a_b=m_k_a_b, r_k_f=m_r_k_f, r_k_b=m_r_k_b,
                 gn_w=m_gn_w, gn_b=m_gn_b, conv_w=m_conv_w, w_out=m_w_out, norm2_w=m_norm2_w, w_gate=m_w_gate,
                 w_up=m_w_up, w_down=m_w_down, norm_f_w=m_norm_f_w)
    mom_v = dict(norm1_w=v_norm1_w, w_in=v_w_in, mu_shift=v_mu_shift, w_up_f=v_w_up_f, w0_f=v_w0_f,
                 w_up_b=v_w_up_b, w0_b=v_w0_b, a_up_f=v_a_up_f, a0_f=v_a0_f, a_up_b=v_a_up_b, a0_b=v_a0_b,
                 g_up=v_g_up, k_k=v_k_k, k_a_f=v_k_a_f, k_a_b=v_k_a_b, r_k_f=v_r_k_f, r_k_b=v_r_k_b,
                 gn_w=v_gn_w, gn_b=v_gn_b, conv_w=v_conv_w, w_out=v_w_out, norm2_w=v_norm2_w, w_gate=v_w_gate,
                 w_up=v_w_up, w_down=v_w_down, norm_f_w=v_norm_f_w)

    sharded = BIG + LORA
    got = _all_gather([local[n].astype(BF16) for n in BIG] + [local[n] for n in LORA], name="gather")
    full = dict(local)
    full.update({n: _from_slots(a, SHARD_AXIS[n]) for n, a in zip(sharded, got)})

    loss_part, grad_x, grads = _local_step(x, loss_target, full)
    loss = lax.psum(loss_part, ("x", "y", "c"))

    vec_rows = jnp.concatenate([grads[n] for n in VEC] + [jnp.zeros((16 - len(VEC), D_RWKV), F32)], axis=0)
    wide_rows = jnp.concatenate([_pad_lanes(grads[n], WIDE_ROW) for n in WIDE]
                                + [jnp.zeros((SUBLANES - len(WIDE), WIDE_ROW), F32)], axis=0)
    slots = [_to_slots(grads[n], SHARD_AXIS[n]).astype(BF16 if n in BIG else F32) for n in sharded]
    recv = _exchange(slots, [vec_rows, wide_rows], name="grad_exchange")
    out = {}
    for n, parts in zip(BIG, recv):
        out[n] = _adamw_big(parts, local[n], mom_m[n], mom_v[n], name="adamw_" + n)

    def small_form(n, a):
        if n in LORA:
            return a
        a = a.reshape(1, -1)
        return _pad_lanes(a, WIDE_ROW) if n == "mu_shift" else a

    small = LORA + VEC + WIDE
    res = _adamw_small(recv[len(BIG):len(sharded)], recv[len(sharded)], recv[len(sharded) + 1],
                       [tuple(small_form(n, d[n]) for d in (local, mom_m, mom_v)) for n in small],
                       name="adamw_small")
    for n, quad in zip(small, res):
        out[n] = tuple(a[..., :local[n].size].reshape(local[n].shape) if n not in LORA else a for a in quad)
    return (loss, grad_x, *[out[n][i] for i in range(4) for n in WEIGHTS])
```

```python
import functools

import jax
import jax.numpy as jnp
from jax import lax
from jax.experimental import pallas as pl
from jax.experimental.pallas import tpu as pltpu

F32 = jnp.float32
BF16 = jnp.bfloat16
HIGHEST = lax.Precision.HIGHEST

N_DEV = 8
D_MODEL = 1024
D_RWKV = 512
D_CONV = 512
HEAD = 64
N_HEAD = D_RWKV // HEAD
D_LORA = 64
D_GATE = 160
D_FF = 2816
D_SHIFTED = 3 * D_RWKV + 2 * D_LORA + D_GATE
D_IN = D_SHIFTED + 3 * D_CONV
XW0, XA0, XG0 = 1536, 1664, 1792
D_SP = 2048
D_INP = D_SP + 3 * D_CONV
LOG_DECAY_SCALE = 0.606531
RMS_EPS = 1e-6
GN_EPS = 64e-5
NORM_EPS = 1e-12
ADAM_LR, ADAM_B1, ADAM_B2, ADAM_EPS, ADAM_WD, ADAM_STEP = 0.001, 0.9, 0.999, 1e-08, 0.01, 10

LANES = 128
SUBLANES = 8
VMEM_LIMIT = 48 * 1024 * 1024
SCAN_CHUNK = 16
ROW_TILE = 128

BIG = ("w_in", "w_out", "w_gate", "w_up", "w_down")
LORA = ("w_up_f", "w_up_b", "a_up_f", "a_up_b", "g_up", "conv_w")
SHARD_AXIS = {"w_in": 2, "w_out": 1, "w_gate": 2, "w_up": 2, "w_down": 1, "w_up_f": 2, "w_up_b": 2,
              "a_up_f": 2, "a_up_b": 2, "g_up": 2, "conv_w": 2}
VEC = ("w0_f", "w0_b", "a0_f", "a0_b", "k_k", "k_a_f", "k_a_b", "r_k_f", "r_k_b", "gn_w", "gn_b")
WIDE = ("mu_shift", "norm1_w", "norm2_w", "norm_f_w")
WIDE_ROW = 2048
WEIGHTS = ("norm1_w", "w_in", "mu_shift", "w_up_f", "w0_f", "w_up_b", "w0_b", "a_up_f", "a0_f", "a_up_b",
           "a0_b", "g_up", "k_k", "k_a_f", "k_a_b", "r_k_f", "r_k_b", "gn_w", "gn_b", "conv_w", "w_out",
           "norm2_w", "w_gate", "w_up", "w_down", "norm_f_w")


def _params(sem, limit=VMEM_LIMIT):
    return pltpu.CompilerParams(dimension_semantics=sem, vmem_limit_bytes=limit)


def _tile(n, cands):
    for c in cands:
        if n % c == 0:
            return c
    raise ValueError(f"no tile for {n}")


def _mm(a, b, *, ta=False, tb=False, add=None, name):
    (k_dim, m) = a.shape if ta else a.shape[::-1]
    (k2, n) = b.shape[::-1] if tb else b.shape
    assert k_dim == k2, (a.shape, b.shape, ta, tb)
    tm = _tile(m, (1408, 1024, 512, 256, 128))
    tn = _tile(n, (1408, 1024, 896, 512, 256, 128))
    tk = k_dim if k_dim <= 1024 else _tile(k_dim, (1408, 896, 512, 256, 128))
    nk = k_dim // tk
    dims = (((0 if ta else 1,), (1 if tb else 0,)), ((), ()))

    def kern(*refs):
        if add is None:
            a_ref, b_ref, o_ref, acc_ref = refs
        else:
            a_ref, b_ref, add_ref, o_ref, acc_ref = refs
        k = pl.program_id(2)

        @pl.when(k == 0)
        def _():
            acc_ref[...] = jnp.zeros_like(acc_ref)

        acc_ref[...] += lax.dot_general(a_ref[...].astype(BF16), b_ref[...].astype(BF16), dims,
                                        preferred_element_type=F32)

        @pl.when(k == nk - 1)
        def _():
            if add is None:
                o_ref[...] = acc_ref[...]
            else:
                o_ref[...] = acc_ref[...] + add_ref[...]

    a_spec = (pl.BlockSpec((tk, tm), lambda i, j, k: (k, i)) if ta
              else pl.BlockSpec((tm, tk), lambda i, j, k: (i, k)))
    b_spec = (pl.BlockSpec((tn, tk), lambda i, j, k: (j, k)) if tb
              else pl.BlockSpec((tk, tn), lambda i, j, k: (k, j)))
    o_spec = pl.BlockSpec((tm, tn), lambda i, j, k: (i, j))
    in_specs = [a_spec, b_spec] + ([o_spec] if add is not None else [])
    args = (a, b) + ((add,) if add is not None else ())
    return pl.pallas_call(
        kern, out_shape=jax.ShapeDtypeStruct((m, n), F32), grid=(m // tm, n // tn, nk),
        in_specs=in_specs, out_specs=o_spec, scratch_shapes=[pltpu.VMEM((tm, tn), F32)],
        compiler_params=_params(("parallel", "parallel", "arbitrary")), name=name)(*args)


def _rowwise(fn, rows, consts, out_rows, out_accs, *, name, tb=ROW_TILE):
    t = (rows[0][0] if isinstance(rows[0], tuple) else rows[0]).shape[0]
    n_r, n_c, n_o, n_a = len(rows), len(consts), len(out_rows), len(out_accs)
    pieces = [w if isinstance(w, (list, tuple)) else [w] for w in out_rows]

    def kern(*refs):
        r_refs = refs[:n_r]
        c_refs = refs[n_r:n_r + n_c]
        o_refs = refs[n_r + n_c:n_r + n_c + n_o]
        a_refs = refs[n_r + n_c + n_o:]
        vals = fn(*[r[...] for r in r_refs], *[c[...] for c in c_refs])
        vals = list(vals) if isinstance(vals, (tuple, list)) else [vals]
        pos = 0
        for o_ref, ws in zip(o_refs, pieces):
            off = 0
            for w in ws:
                o_ref[:, off:off + w] = vals[pos]
                off += w
                pos += 1
        if n_a:
            @pl.when(pl.program_id(0) == 0)
            def _():
                for a_ref in a_refs:
                    a_ref[...] = jnp.zeros_like(a_ref)
            for a_ref, v in zip(a_refs, vals[pos:]):
                a_ref[...] += v

    in_specs, args = [], []
    for r in rows:
        if isinstance(r, tuple):
            arr, blk, w = r
            in_specs.append(pl.BlockSpec((tb, w), functools.partial(lambda i, blk: (i, blk), blk=blk)))
        else:
            arr = r
            in_specs.append(pl.BlockSpec((tb, arr.shape[1]), lambda i: (i, 0)))
        args.append(arr)
    for c in consts:
        in_specs.append(pl.BlockSpec(c.shape, lambda i: (0, 0)))
        args.append(c)
    out_shape = [jax.ShapeDtypeStruct((t, sum(ws)), F32) for ws in pieces]
    out_specs = [pl.BlockSpec((tb, sum(ws)), lambda i: (i, 0)) for ws in pieces]
    for shp in out_accs:
        out_shape.append(jax.ShapeDtypeStruct(shp, F32))
        out_specs.append(pl.BlockSpec(shp, lambda i: (0, 0)))
    res = pl.pallas_call(
        kern, out_shape=out_shape, grid=(t // tb,), in_specs=in_specs, out_specs=out_specs,
        compiler_params=_params(("arbitrary",) if n_a else ("parallel",)), name=name)(*args)
    return res


def _rms(x, w):
    return x * lax.rsqrt(jnp.mean(x * x, axis=-1, keepdims=True) + RMS_EPS) * w


def _seg_sum(x, bd):
    return jnp.concatenate(
        [jnp.dot(x[:, LANES * j:LANES * (j + 1)], bd, precision=HIGHEST, preferred_element_type=F32)
         for j in range(x.shape[1] // LANES)], axis=1)


@jax.custom_vjp
def _seg(x, bd):
    return _seg_sum(x, bd)


_seg.defvjp(lambda x, bd: (_seg_sum(x, bd), bd), lambda bd, ct: (_seg_sum(ct, bd), jnp.zeros_like(bd)))


def _colsum(x):
    return jnp.sum(x, axis=0, keepdims=True)


def _prescan_math(r, k, xw, xa, xg, k_k, w0f, w0b, a0f, a0b, kaf, kab, wupf, wupb, aupf, aupb, gup, bd):
    kkr = k * k_k
    norm = jnp.sqrt(_seg(kkr * kkr, bd))
    kk = kkr / jnp.maximum(norm, NORM_EPS)
    th = jnp.tanh(xw)

    def direction(w0, wup, a0, aup, ka):
        logit = w0 + jnp.dot(th, wup, preferred_element_type=F32)
        w = jnp.exp(-LOG_DECAY_SCALE * jax.nn.sigmoid(logit))
        a = jax.nn.sigmoid(a0 + jnp.dot(xa, aup, preferred_element_type=F32))
        kd = k * (1.0 + (a - 1.0) * ka)
        return w, kd, kk * a

    wf, kdf, bf = direction(w0f, wupf, a0f, aupf, kaf)
    wb, kdb, bb = direction(w0b, wupb, a0b, aupb, kab)
    g = jnp.dot(jax.nn.sigmoid(xg), gup, preferred_element_type=F32)
    return kk, r, wf, wb, bf, bb, kdf, kdb, g


def _postscan_math(y, r, v, kdf, kdb, g, gn_w, gn_b, rkf, rkb, bd):
    mean = _seg(y, bd) * (1.0 / HEAD)
    yc = y - mean
    var = _seg(yc * yc, bd) * (1.0 / HEAD)
    yg = yc * lax.rsqrt(var + GN_EPS) * gn_w + gn_b
    bonus = (_seg(r * kdf * rkf, bd) + _seg(r * kdb * rkb, bd)) * v
    return (yg + bonus) * g


def _halo_specs(width, col_blk, tb, t):
    nb = t // SUBLANES
    step = tb // SUBLANES
    main = pl.BlockSpec((tb, width), lambda i: (i, col_blk))
    prev = pl.BlockSpec((SUBLANES, width), lambda i: (jnp.maximum(i * step - 1, 0), col_blk))
    nxt = pl.BlockSpec((SUBLANES, width), lambda i: (jnp.minimum((i + 1) * step, nb - 1), col_blk))
    return [main, prev, nxt]


def _neighbours(z, prev8, next8, first, last):
    tb = z.shape[0]
    row = lax.broadcasted_iota(jnp.int32, z.shape, 0)
    prow = jnp.where(first, 0.0, prev8[SUBLANES - 1:SUBLANES, :])
    nrow = jnp.where(last, 0.0, next8[0:1, :])
    down = jnp.where(row == 0, prow, pltpu.roll(z, 1, 0))
    up = jnp.where(row == tb - 1, nrow, pltpu.roll(z, tb - 1, 0))
    return down, up


def _shift_conv_fwd(p, mu, conv_w, seq, *, name, tb=ROW_TILE):
    t = p.shape[0]
    per_seq = seq // tb

    def kern(p_ref, pp_ref, pn_ref, mu_ref, cw_ref, pss_ref, oc_ref):
        i = pl.program_id(0)
        first = (i % per_seq) == 0
        last = (i % per_seq) == per_seq - 1
        ps = p_ref[:, :D_SP]
        down, up = _neighbours(ps, pp_ref[:, :D_SP], pn_ref[:, :D_SP], first, last)
        pss_ref[...] = ps + mu_ref[...] * (0.5 * (down + up) - ps)
        gb = p_ref[:, D_SP:D_SP + D_CONV]
        u = p_ref[:, D_SP + D_CONV:D_SP + 2 * D_CONV] * p_ref[:, D_SP + 2 * D_CONV:]
        u_p = pp_ref[:, D_SP + D_CONV:D_SP + 2 * D_CONV] * pp_ref[:, D_SP + 2 * D_CONV:]
        u_n = pn_ref[:, D_SP + D_CONV:D_SP + 2 * D_CONV] * pn_ref[:, D_SP + 2 * D_CONV:]
        udown, uup = _neighbours(u, u_p, u_n, first, last)
        oc_ref[...] = gb * (cw_ref[0:1, :] * udown + cw_ref[1:2, :] * u + cw_ref[2:3, :] * uup)

    return pl.pallas_call(
        kern,
        out_shape=[jax.ShapeDtypeStruct((t, D_SP), F32), jax.ShapeDtypeStruct((t, D_CONV), F32)],
        grid=(t // tb,),
        in_specs=_halo_specs(D_INP, 0, tb, t) + [pl.BlockSpec((1, D_SP), lambda i: (0, 0)),
                                                 pl.BlockSpec((SUBLANES, D_CONV), lambda i: (0, 0))],
        out_specs=[pl.BlockSpec((tb, D_SP), lambda i: (i, 0)), pl.BlockSpec((tb, D_CONV), lambda i: (i, 0))],
        compiler_params=_params(("parallel",)), name=name)(p, p, p, mu, conv_w)


def _shift_conv_bwd(p, d_pss, d_o, mu, conv_w, seq, *, name, tb=ROW_TILE):
    t = p.shape[0]
    per_seq = seq // tb

    def kern(p_ref, pp_ref, pn_ref, d_ref, dp_ref, dn_ref, do_ref, dop_ref, don_ref, mu_ref, cw_ref,
             out_ref, dmu_ref, dcw_ref):
        i = pl.program_id(0)
        first = (i % per_seq) == 0
        last = (i % per_seq) == per_seq - 1

        @pl.when(i == 0)
        def _():
            dmu_ref[...] = jnp.zeros_like(dmu_ref)
            dcw_ref[...] = jnp.zeros_like(dcw_ref)

        mu_v = mu_ref[...]
        ps = p_ref[:, :D_SP]
        down, up = _neighbours(ps, pp_ref[:, :D_SP], pn_ref[:, :D_SP], first, last)
        d = d_ref[...]
        ddown, dup = _neighbours(d, dp_ref[...], dn_ref[...], first, last)
        out_ref[:, :D_SP] = d - mu_v * d + 0.5 * (mu_v * ddown + mu_v * dup)
        dmu_ref[...] += _colsum(d * (0.5 * (down + up) - ps))

        def parts(ref):
            return (ref[:, D_SP:D_SP + D_CONV], ref[:, D_SP + D_CONV:D_SP + 2 * D_CONV],
                    ref[:, D_SP + 2 * D_CONV:])

        gb, gc, hh = parts(p_ref)
        gb_p, gc_p, hh_p = parts(pp_ref)
        gb_n, gc_n, hh_n = parts(pn_ref)
        u = gc * hh
        udown, uup = _neighbours(u, gc_p * hh_p, gc_n * hh_n, first, last)
        cw0, cw1, cw2 = cw_ref[0:1, :], cw_ref[1:2, :], cw_ref[2:3, :]
        do = do_ref[...]
        duc = do * gb
        ducdown, ducup = _neighbours(duc, dop_ref[...] * gb_p, don_ref[...] * gb_n, first, last)
        du = cw0 * ducup + cw1 * duc + cw2 * ducdown
        out_ref[:, D_SP:D_SP + D_CONV] = do * (cw0 * udown + cw1 * u + cw2 * uup)
        out_ref[:, D_SP + D_CONV:D_SP + 2 * D_CONV] = du * hh
        out_ref[:, D_SP + 2 * D_CONV:] = du * gc
        dcw_ref[0:1, :] += _colsum(duc * udown)
        dcw_ref[1:2, :] += _colsum(duc * u)
        dcw_ref[2:3, :] += _colsum(duc * uup)

    return pl.pallas_call(
        kern,
        out_shape=[jax.ShapeDtypeStruct((t, D_INP), F32), jax.ShapeDtypeStruct((1, D_SP), F32),
                   jax.ShapeDtypeStruct((SUBLANES, D_CONV), F32)],
        grid=(t // tb,),
        in_specs=(_halo_specs(D_INP, 0, tb, t) + _halo_specs(D_SP, 0, tb, t) + _halo_specs(D_CONV, 1, tb, t)
                  + [pl.BlockSpec((1, D_SP), lambda i: (0, 0)),
                     pl.BlockSpec((SUBLANES, D_CONV), lambda i: (0, 0))]),
        out_specs=[pl.BlockSpec((tb, D_INP), lambda i: (i, 0)), pl.BlockSpec((1, D_SP), lambda i: (0, 0)),
                   pl.BlockSpec((SUBLANES, D_CONV), lambda i: (0, 0))],
        compiler_params=_params(("arbitrary",)), name=name)(p, p, p, d_pss, d_pss, d_pss, d_o, d_o, d_o, mu, conv_w)


N_CHAIN = 16
V_LO = LANES // N_CHAIN
V_HI = HEAD // V_LO
N_GROUP = LANES // N_CHAIN
G_KK, G_R, G_W, G_B, G_KD = 0, 1, (2, 3), (4, 5), (6, 7)


def _group(x, j, lane):
    g = pltpu.roll(x, (LANES - N_CHAIN * j) % LANES, 1) if j else x
    g = jnp.where(lane < N_CHAIN, g, pltpu.roll(g, N_CHAIN, 1))
    g = jnp.where(lane < 2 * N_CHAIN, g, pltpu.roll(g, 2 * N_CHAIN, 1))
    return jnp.where(lane < 4 * N_CHAIN, g, pltpu.roll(g, 4 * N_CHAIN, 1))


def _scan_inputs(x, d, lane):
    return [_group(x, j, lane) for j in (G_KK, G_R, G_W[d], G_B[d], G_KD[d])]


def _lane_scan_fwd(xall, v_l, *, name):
    steps = xall.shape[0]
    nc = steps // SCAN_CHUNK
    mirror = lambda c: nc - 1 - c

    def kern(xf_ref, xb_ref, vf_ref, vb_ref, yf_ref, yb_ref, hist_ref, fin_ref, st_ref):
        c = pl.program_id(0)

        @pl.when(c == 0)
        def _():
            st_ref[...] = jnp.zeros_like(st_ref)

        row = lax.broadcasted_iota(jnp.int32, (V_HI, LANES), 0)
        lane = lax.broadcasted_iota(jnp.int32, (HEAD, LANES), 1)

        def step(i, carry):
            j = SCAN_CHUNK - 1 - i
            for d, (x_t, v_t, y_ref, at) in enumerate(((xf_ref[i], vf_ref[i], yf_ref, i),
                                                       (xb_ref[j], vb_ref[j], yb_ref, j))):
                kk_t, r_t, w_t, b_t, kd_t = _scan_inputs(x_t, d, lane)
                y_t = jnp.zeros((V_HI, LANES), F32)
                for vh in range(V_HI):
                    tile = d * V_HI + vh
                    state = st_ref[tile]
                    hist_ref[i, tile] = state
                    sa = _colsum(state * kk_t)
                    state = state * w_t - sa * b_t + v_t[vh:vh + 1, :] * kd_t
                    st_ref[tile] = state
                    y_t = jnp.where(row == vh, _colsum(state * r_t), y_t)
                y_ref[at] = y_t
            return carry

        lax.fori_loop(0, SCAN_CHUNK, step, 0)

        @pl.when(c == nc - 1)
        def _():
            fin_ref[...] = st_ref[...]

    def k_spec(fn):
        return pl.BlockSpec((SCAN_CHUNK, HEAD, LANES), lambda c: (fn(c), 0, 0))

    def v_spec(fn):
        return pl.BlockSpec((SCAN_CHUNK, V_HI, LANES), lambda c: (fn(c), 0, 0))

    same = lambda c: c
    st_shape = (2 * V_HI, HEAD, LANES)
    return pl.pallas_call(
        kern,
        out_shape=[jax.ShapeDtypeStruct((steps, V_HI, LANES), F32)] * 2
        + [jax.ShapeDtypeStruct((steps,) + st_shape, F32), jax.ShapeDtypeStruct(st_shape, F32)],
        grid=(nc,), in_specs=[k_spec(same), k_spec(mirror), v_spec(same), v_spec(mirror)],
        out_specs=[v_spec(same), v_spec(mirror),
                   pl.BlockSpec((SCAN_CHUNK,) + st_shape, lambda c: (c, 0, 0, 0)),
                   pl.BlockSpec(st_shape, lambda c: (0, 0, 0))],
        scratch_shapes=[pltpu.VMEM(st_shape, F32)],
        compiler_params=_params(("arbitrary",)), name=name)(xall, xall, v_l, v_l)


def _lane_scan_bwd(xall, v_l, dy_l, hist, fin, *, name):
    steps = xall.shape[0]
    nc = steps // SCAN_CHUNK
    back = lambda c: nc - 1 - c
    same = lambda c: c

    def kern(xf_ref, xb_ref, vf_ref, vb_ref, dyf_ref, dyb_ref, hist_ref, fin_ref,
             gf_ref, gb_ref, dvf_ref, dvb_ref, ds_ref, after_ref):
        c = pl.program_id(0)

        @pl.when(c == 0)
        def _():
            ds_ref[...] = jnp.zeros_like(ds_ref)
            after_ref[...] = fin_ref[...]

        row = lax.broadcasted_iota(jnp.int32, (V_HI, LANES), 0)
        lane = lax.broadcasted_iota(jnp.int32, (HEAD, LANES), 1)
        grp = lax.shift_right_logical(lane, jnp.full_like(lane, 4))

        def group_sum(x):
            x = x + pltpu.roll(x, 4 * N_CHAIN, 1)
            x = x + pltpu.roll(x, 2 * N_CHAIN, 1)
            return x + pltpu.roll(x, N_CHAIN, 1)

        def step(ii, carry):
            i = SCAN_CHUNK - 1 - ii
            j = ii
            for d, (x_t, v_t, dy_t, g_ref, dv_ref, at) in enumerate((
                    (xf_ref[i], vf_ref[i], dyf_ref[i], gf_ref, dvf_ref, i),
                    (xb_ref[j], vb_ref[j], dyb_ref[j], gb_ref, dvb_ref, j))):
                kk_t, r_t, w_t, b_t, kd_t = _scan_inputs(x_t, d, lane)
                dv_t = jnp.zeros((V_HI, LANES), F32)
                zero = jnp.zeros((HEAD, LANES), F32)
                dkk, dr, dw, db, dkd = zero, zero, zero, zero, zero
                for vh in range(V_HI):
                    tile = d * V_HI + vh
                    before = hist_ref[i, tile]
                    dy_r, v_r = dy_t[vh:vh + 1, :], v_t[vh:vh + 1, :]
                    g = ds_ref[tile] + dy_r * r_t
                    sa = _colsum(before * kk_t)
                    dsa = -_colsum(g * b_t)
                    dv_t = jnp.where(row == vh, _colsum(g * kd_t), dv_t)
                    dr = dr + after_ref[tile] * dy_r
                    dw = dw + g * before
                    dkd = dkd + g * v_r
                    db = db - g * sa
                    dkk = dkk + before * dsa
                    ds_ref[tile] = g * w_t + dsa * kk_t
                    after_ref[tile] = before
                out = jnp.where(grp == G_KK, group_sum(dkk), 0.0)
                out = jnp.where(grp == G_R, group_sum(dr), out)
                out = jnp.where(grp == G_W[d], group_sum(dw), out)
                out = jnp.where(grp == G_B[d], group_sum(db), out)
                out = jnp.where(grp == G_KD[d], group_sum(dkd), out)
                g_ref[at] = out
                dv_ref[at] = dv_t
            return carry

        lax.fori_loop(0, SCAN_CHUNK, step, 0)

    def k_spec(fn):
        return pl.BlockSpec((SCAN_CHUNK, HEAD, LANES), lambda c: (fn(c), 0, 0))

    def v_spec(fn):
        return pl.BlockSpec((SCAN_CHUNK, V_HI, LANES), lambda c: (fn(c), 0, 0))

    st_shape = (2 * V_HI, HEAD, LANES)
    return pl.pallas_call(
        kern,
        out_shape=[jax.ShapeDtypeStruct((steps, HEAD, LANES), F32)] * 2
        + [jax.ShapeDtypeStruct((steps, V_HI, LANES), F32)] * 2,
        grid=(nc,),
        in_specs=[k_spec(back), k_spec(same), v_spec(back), v_spec(same), v_spec(back), v_spec(same),
                  pl.BlockSpec((SCAN_CHUNK,) + st_shape, lambda c: (back(c), 0, 0, 0)),
                  pl.BlockSpec(st_shape, lambda c: (0, 0, 0))],
        out_specs=[k_spec(back), k_spec(same), v_spec(back), v_spec(same)],
        scratch_shapes=[pltpu.VMEM(st_shape, F32), pltpu.VMEM(st_shape, F32)],
        compiler_params=_params(("arbitrary",)), name=name)(xall, xall, v_l, v_l, dy_l, dy_l, hist, fin)


def _to_key_lanes(wide, bsz, seq):
    z = wide.reshape(bsz, seq, N_GROUP, N_HEAD, HEAD).transpose(1, 4, 2, 0, 3)
    return z.reshape(seq, HEAD, LANES)


def _from_key_lanes(g, bsz, seq):
    z = g.reshape(seq, HEAD, N_GROUP, bsz, N_HEAD).transpose(3, 0, 2, 4, 1)
    return z.reshape(bsz * seq, N_GROUP * D_RWKV)


def _to_value_lanes(a, bsz, seq):
    z = a.reshape(bsz, seq, N_HEAD, V_HI, V_LO).transpose(1, 3, 4, 0, 2)
    return z.reshape(seq, V_HI, LANES)


def _from_value_lanes(y, bsz, seq):
    z = y.reshape(seq, V_HI, V_LO, bsz, N_HEAD).transpose(3, 0, 4, 1, 2)
    return z.reshape(bsz * seq, D_RWKV)


K_HI = HEAD // SUBLANES


def _lane_group_sum(x):
    x = x + pltpu.roll(x, 4 * N_CHAIN, 1)
    x = x + pltpu.roll(x, 2 * N_CHAIN, 1)
    return x + pltpu.roll(x, N_CHAIN, 1)


def _key_rows(x_t, d):
    out = []
    for grp in (G_KK, G_R, G_W[d], G_B[d], G_KD[d]):
        blk = x_t[SUBLANES * grp:SUBLANES * (grp + 1), :]
        out.append([jnp.broadcast_to(blk[kh:kh + 1, :], (SUBLANES, LANES)) for kh in range(K_HI)])
    return out


def _tree_sum(terms):
    terms = list(terms)
    while len(terms) > 1:
        terms = [a + b for a, b in zip(terms[::2], terms[1::2])]
    return terms[0]


def _kscan_specs(nc):
    same = lambda c: c
    mirror = lambda c: nc - 1 - c

    def k_spec(fn):
        return pl.BlockSpec((SCAN_CHUNK, HEAD, LANES), lambda c: (fn(c), 0, 0))

    def v_spec(fn):
        return pl.BlockSpec((SCAN_CHUNK, SUBLANES, LANES), lambda c: (fn(c), 0, 0))

    return same, mirror, k_spec, v_spec


ST_SHAPE = (2, K_HI, V_HI, SUBLANES, LANES)


def _lane_group_index():
    lane = lax.broadcasted_iota(jnp.int32, (SUBLANES, LANES), 1)
    return lax.shift_right_logical(lane, jnp.full_like(lane, 4))


def _spread_groups(x, grp):
    rolled = [x] + [pltpu.roll(x, s * N_CHAIN, 1) for s in range(1, N_GROUP)]
    out = []
    for j in range(N_GROUP):
        t = rolled[(0 - j) % N_GROUP]
        for g in range(1, N_GROUP):
            t = jnp.where(grp == g, rolled[(g - j) % N_GROUP], t)
        out.append(t)
    return out


def _gather_groups(tiles, grp):
    total = None
    for s in range(N_GROUP):
        b = tiles[s % N_GROUP]
        for g in range(1, N_GROUP):
            b = jnp.where(grp == g, tiles[(g + s) % N_GROUP], b)
        b = pltpu.roll(b, s * N_CHAIN, 1) if s else b
        total = b if total is None else total + b
    return total


def _lane_group_sum_short(x):
    return _tree_sum([x] + [pltpu.roll(x, k * N_CHAIN, 1) for k in range(1, N_GROUP)])


def _kscan_fwd(xall, v_c, *, name):
    steps = xall.shape[0]
    nc = steps // SCAN_CHUNK
    same, mirror, k_spec, v_spec = _kscan_specs(nc)

    def kern(xf_ref, xb_ref, vf_ref, vb_ref, yf_ref, yb_ref, hist_ref, fin_ref, st_ref):
        c = pl.program_id(0)

        @pl.when(c == 0)
        def _():
            st_ref[...] = jnp.zeros_like(st_ref)

        grp = _lane_group_index()

        def step(i, carry):
            j = SCAN_CHUNK - 1 - i
            for d, (x_t, v_t, y_ref, at) in enumerate(((xf_ref[i], vf_ref[i], yf_ref, i),
                                                       (xb_ref[j], vb_ref[j], yb_ref, j))):
                kk_r, r_r, w_r, b_r, kd_r = _key_rows(x_t, d)
                v_b = _spread_groups(v_t, grp)
                y_p = []
                for vh in range(V_HI):
                    st = [st_ref[d, kh, vh] for kh in range(K_HI)]
                    for kh in range(K_HI):
                        hist_ref[i, d, kh, vh] = st[kh]
                    sa = _lane_group_sum_short(_tree_sum(st[kh] * kk_r[kh] for kh in range(K_HI)))
                    new = [st[kh] * w_r[kh] - sa * b_r[kh] + v_b[vh] * kd_r[kh] for kh in range(K_HI)]
                    for kh in range(K_HI):
                        st_ref[d, kh, vh] = new[kh]
                    y_p.append(_tree_sum(new[kh] * r_r[kh] for kh in range(K_HI)))
                y_ref[at] = _gather_groups(y_p, grp)
            return carry

        lax.fori_loop(0, SCAN_CHUNK, step, 0)

        @pl.when(c == nc - 1)
        def _():
            fin_ref[...] = st_ref[...]

    return pl.pallas_call(
        kern,
        out_shape=[jax.ShapeDtypeStruct((steps, SUBLANES, LANES), F32)] * 2
        + [jax.ShapeDtypeStruct((steps,) + ST_SHAPE, F32), jax.ShapeDtypeStruct(ST_SHAPE, F32)],
        grid=(nc,), in_specs=[k_spec(same), k_spec(mirror), v_spec(same), v_spec(mirror)],
        out_specs=[v_spec(same), v_spec(mirror),
                   pl.BlockSpec((SCAN_CHUNK,) + ST_SHAPE, lambda c: (c, 0, 0, 0, 0, 0)),
                   pl.BlockSpec(ST_SHAPE, lambda c: (0, 0, 0, 0, 0))],
        scratch_shapes=[pltpu.VMEM(ST_SHAPE, F32)],
        compiler_params=_params(("arbitrary",)), name=name)(xall, xall, v_c, v_c)


def _kscan_bwd(xall, v_c, dy_c, hist, fin, *, name):
    steps = xall.shape[0]
    nc = steps // SCAN_CHUNK
    same, back, k_spec, v_spec = _kscan_specs(nc)

    def kern(xf_ref, xb_ref, vf_ref, vb_ref, dyf_ref, dyb_ref, hist_ref, fin_ref,
             gf_ref, gb_ref, dvf_ref, dvb_ref, ds_ref, after_ref):
        c = pl.program_id(0)

        @pl.when(c == 0)
        def _():
            ds_ref[...] = jnp.zeros_like(ds_ref)
            after_ref[...] = fin_ref[...]

        grp = _lane_group_index()
        row = lax.broadcasted_iota(jnp.int32, (SUBLANES, LANES), 0)

        def step(ii, carry):
            i = SCAN_CHUNK - 1 - ii
            j = ii
            for d, (x_t, v_t, dy_t, g_ref, dv_ref, at) in enumerate((
                    (xf_ref[i], vf_ref[i], dyf_ref[i], gf_ref, dvf_ref, i),
                    (xb_ref[j], vb_ref[j], dyb_ref[j], gb_ref, dvb_ref, j))):
                kk_r, r_r, w_r, b_r, kd_r = _key_rows(x_t, d)
                v_s, dy_s = _spread_groups(v_t, grp), _spread_groups(dy_t, grp)
                ks = range(K_HI)
                zero = jnp.zeros((SUBLANES, LANES), F32)
                dkk, dr, dw, db, dkd = ([zero] * K_HI for _ in range(5))
                dv_p = []
                for vh in range(V_HI):
                    v_b, dy_b = v_s[vh], dy_s[vh]
                    before = [hist_ref[i, d, kh, vh] for kh in ks]
                    g = [ds_ref[d, kh, vh] + dy_b * r_r[kh] for kh in ks]
                    dsa = -_lane_group_sum_short(_tree_sum(g[kh] * b_r[kh] for kh in ks))
                    sa = _lane_group_sum(_tree_sum(before[kh] * kk_r[kh] for kh in ks))
                    dv_p.append(_tree_sum(g[kh] * kd_r[kh] for kh in ks))
                    dr = [dr[kh] + after_ref[d, kh, vh] * dy_b for kh in ks]
                    dw = [dw[kh] + g[kh] * before[kh] for kh in ks]
                    dkd = [dkd[kh] + g[kh] * v_b for kh in ks]
                    db = [db[kh] - g[kh] * sa for kh in ks]
                    dkk = [dkk[kh] + before[kh] * dsa for kh in ks]
                    for kh in ks:
                        ds_ref[d, kh, vh] = g[kh] * w_r[kh] + dsa * kk_r[kh]
                        after_ref[d, kh, vh] = before[kh]
                dv_ref[at] = _gather_groups(dv_p, grp)
                blocks = {G_KK: dkk, G_R: dr, G_W[d]: dw, G_B[d]: db, G_KD[d]: dkd}
                for gi in range(N_GROUP):
                    blk = zero
                    if gi in blocks:
                        for kh in ks:
                            blk = jnp.where(row == kh, _colsum(blocks[gi][kh]), blk)
                    g_ref[at, SUBLANES * gi:SUBLANES * (gi + 1), :] = blk
            return carry

        lax.fori_loop(0, SCAN_CHUNK, step, 0)

    return pl.pallas_call(
        kern,
        out_shape=[jax.ShapeDtypeStruct((steps, HEAD, LANES), F32)] * 2
        + [jax.ShapeDtypeStruct((steps, SUBLANES, LANES), F32)] * 2,
        grid=(nc,),
        in_specs=[k_spec(back), k_spec(same), v_spec(back), v_spec(same), v_spec(back), v_spec(same),
                  pl.BlockSpec((SCAN_CHUNK,) + ST_SHAPE, lambda c: (back(c), 0, 0, 0, 0, 0)),
                  pl.BlockSpec(ST_SHAPE, lambda c: (0, 0, 0, 0, 0))],
        out_specs=[k_spec(back), k_spec(same), v_spec(back), v_spec(same)],
        scratch_shapes=[pltpu.VMEM(ST_SHAPE, F32), pltpu.VMEM(ST_SHAPE, F32)],
        compiler_params=_params(("arbitrary",)), name=name)(xall, xall, v_c, v_c, dy_c, dy_c, hist, fin)


def _key_row(x_t, grp, kh):
    r = SUBLANES * grp + kh
    return jnp.broadcast_to(x_t[r:r + 1, :], (SUBLANES, LANES))


def _acc(total, term):
    return term if total is None else total + term


def _scan_fwd(xall, v_c, *, gather=(), name):
    steps = xall.shape[0]
    nc = steps // SCAN_CHUNK
    same, mirror, k_spec, v_spec = _kscan_specs(nc)
    last = SCAN_CHUNK - 1
    n_x = len(gather)

    def kern(*refs):
        xf_ref, xb_ref, vf_ref, vb_ref = refs[:4]
        yf_ref, yb_ref, hist_ref, fin_ref = refs[4 + n_x:8 + n_x]
        st_ref = refs[8 + 2 * n_x]
        c = pl.program_id(0)

        def riders():
            return _exchange_copies(refs[4:4 + n_x], refs[8 + n_x:8 + 2 * n_x], 0, *refs[9 + 2 * n_x:])

        @pl.when(c == 0)
        def _():
            st_ref[...] = jnp.zeros_like(st_ref)
            if n_x:
                for cp in riders():
                    cp.start()

        hist_ref[0] = st_ref[...]
        grp = _lane_group_index()

        def body(i, put):
            j = last - i
            for d, (x_t, v_t, y_ref, at) in enumerate(((xf_ref[i], vf_ref[i], yf_ref, i),
                                                       (xb_ref[j], vb_ref[j], yb_ref, j))):
                v_b = _spread_groups(v_t, grp)
                part = [None] * V_HI
                for kh in range(K_HI):
                    kk_r = _key_row(x_t, G_KK, kh)
                    for vh in range(V_HI):
                        part[vh] = _acc(part[vh], hist_ref[i, d, kh, vh] * kk_r)
                sa = [_lane_group_sum_short(p) for p in part]
                y_p = [None] * V_HI
                for kh in range(K_HI):
                    r_r, w_r = _key_row(x_t, G_R, kh), _key_row(x_t, G_W[d], kh)
                    b_r, kd_r = _key_row(x_t, G_B[d], kh), _key_row(x_t, G_KD[d], kh)
                    for vh in range(V_HI):
                        new = hist_ref[i, d, kh, vh] * w_r - sa[vh] * b_r + v_b[vh] * kd_r
                        put(d, kh, vh, new)
                        y_p[vh] = _acc(y_p[vh], new * r_r)
                y_ref[at] = _gather_groups(y_p, grp)

        def step(i, carry):
            def put(d, kh, vh, val):
                hist_ref[i + 1, d, kh, vh] = val
            body(i, put)
            return carry

        lax.fori_loop(0, last, step, 0)

        def put_carry(d, kh, vh, val):
            st_ref[d, kh, vh] = val

        body(last, put_carry)

        @pl.when(c == nc - 1)
        def _():
            fin_ref[...] = st_ref[...]
            if n_x:
                for cp in riders():
                    cp.wait()

    return pl.pallas_call(
        kern,
        out_shape=[jax.ShapeDtypeStruct((steps, SUBLANES, LANES), F32)] * 2
        + [jax.ShapeDtypeStruct((steps,) + ST_SHAPE, F32), jax.ShapeDtypeStruct(ST_SHAPE, F32)]
        + _exchange_out_shapes(gather, 0),
        grid=(nc,), in_specs=[k_spec(same), k_spec(mirror), v_spec(same), v_spec(mirror)] + _hbm_specs(n_x),
        out_specs=[v_spec(same), v_spec(mirror),
                   pl.BlockSpec((SCAN_CHUNK,) + ST_SHAPE, lambda c: (c, 0, 0, 0, 0, 0)),
                   pl.BlockSpec(ST_SHAPE, lambda c: (0, 0, 0, 0, 0))] + _hbm_specs(n_x),
        scratch_shapes=[pltpu.VMEM(ST_SHAPE, F32)] + (_exchange_sems(n_x) if n_x else []),
        compiler_params=_params(("arbitrary",)), name=name)(xall, xall, v_c, v_c, *gather)


def _scan_bwd(xall, v_c, dy_c, hist, fin, *, exchange=(), name):
    steps = xall.shape[0]
    nc = steps // SCAN_CHUNK
    same, back, k_spec, v_spec = _kscan_specs(nc)
    last = SCAN_CHUNK - 1
    n_x = len(exchange)

    def kern(*refs):
        xf_ref, xb_ref, vf_ref, vb_ref, dyf_ref, dyb_ref, hist_ref, fin_ref = refs[:8]
        gf_ref, gb_ref, dvf_ref, dvb_ref = refs[8 + n_x:12 + n_x]
        ds_ref, after_ref = refs[12 + 2 * n_x:14 + 2 * n_x]
        c = pl.program_id(0)

        def riders():
            return _exchange_copies(refs[8:8 + n_x], refs[12 + n_x:12 + 2 * n_x], n_x, *refs[14 + 2 * n_x:])

        @pl.when(c == 0)
        def _():
            ds_ref[...] = jnp.zeros_like(ds_ref)
            after_ref[...] = fin_ref[...]
            if n_x:
                for cp in riders():
                    cp.start()

        grp = _lane_group_index()
        row = lax.broadcasted_iota(jnp.int32, (SUBLANES, LANES), 0)
        zero = jnp.zeros((SUBLANES, LANES), F32)

        def body(i, after):
            j = last - i
            for d, (x_t, v_t, dy_t, g_ref, dv_ref, at) in enumerate((
                    (xf_ref[i], vf_ref[i], dyf_ref[i], gf_ref, dvf_ref, i),
                    (xb_ref[j], vb_ref[j], dyb_ref[j], gb_ref, dvb_ref, j))):
                v_s, dy_s = _spread_groups(v_t, grp), _spread_groups(dy_t, grp)
                dsa_p, sa_p, dv_p = [None] * V_HI, [None] * V_HI, [None] * V_HI
                for kh in range(K_HI):
                    r_r, kk_r = _key_row(x_t, G_R, kh), _key_row(x_t, G_KK, kh)
                    b_r, kd_r = _key_row(x_t, G_B[d], kh), _key_row(x_t, G_KD[d], kh)
                    for vh in range(V_HI):
                        g = ds_ref[d, kh, vh] + dy_s[vh] * r_r
                        ds_ref[d, kh, vh] = g
                        dsa_p[vh] = _acc(dsa_p[vh], g * b_r)
                        sa_p[vh] = _acc(sa_p[vh], hist_ref[i, d, kh, vh] * kk_r)
                        dv_p[vh] = _acc(dv_p[vh], g * kd_r)
                dsa = [-_lane_group_sum_short(p) for p in dsa_p]
                sa = [_lane_group_sum(p) for p in sa_p]
                dv_ref[at] = _gather_groups(dv_p, grp)
                blocks = {G_KK: zero, G_R: zero, G_W[d]: zero, G_B[d]: zero, G_KD[d]: zero}
                for kh in range(K_HI):
                    w_r, kk_r = _key_row(x_t, G_W[d], kh), _key_row(x_t, G_KK, kh)
                    dkk = dr = dw = db = dkd = None
                    for vh in range(V_HI):
                        g, before = ds_ref[d, kh, vh], hist_ref[i, d, kh, vh]
                        dr = _acc(dr, after(d, kh, vh) * dy_s[vh])
                        dw = _acc(dw, g * before)
                        dkd = _acc(dkd, g * v_s[vh])
                        db = _acc(db, g * sa[vh])
                        dkk = _acc(dkk, before * dsa[vh])
                        ds_ref[d, kh, vh] = g * w_r + dsa[vh] * kk_r
                    for gi, a in ((G_KK, dkk), (G_R, dr), (G_W[d], dw), (G_B[d], -db), (G_KD[d], dkd)):
                        blocks[gi] = jnp.where(row == kh, _colsum(a), blocks[gi])
                for gi in range(N_GROUP):
                    g_ref[at, SUBLANES * gi:SUBLANES * (gi + 1), :] = blocks.get(gi, zero)

        body(last, lambda d, kh, vh: after_ref[d, kh, vh])

        def step(ii, carry):
            i = last - ii
            body(i, lambda d, kh, vh: hist_ref[i + 1, d, kh, vh])
            return carry

        lax.fori_loop(1, SCAN_CHUNK, step, 0)
        after_ref[...] = hist_ref[0]

        if n_x:
            @pl.when(c == nc - 1)
            def _():
                for cp in riders():
                    cp.wait()

    return pl.pallas_call(
        kern,
        out_shape=[jax.ShapeDtypeStruct((steps, HEAD, LANES), F32)] * 2
        + [jax.ShapeDtypeStruct((steps, SUBLANES, LANES), F32)] * 2 + _exchange_out_shapes(exchange, n_x),
        grid=(nc,),
        in_specs=[k_spec(back), k_spec(same), v_spec(back), v_spec(same), v_spec(back), v_spec(same),
                  pl.BlockSpec((SCAN_CHUNK,) + ST_SHAPE, lambda c: (back(c), 0, 0, 0, 0, 0)),
                  pl.BlockSpec(ST_SHAPE, lambda c: (0, 0, 0, 0, 0))] + _hbm_specs(n_x),
        out_specs=[k_spec(back), k_spec(same), v_spec(back), v_spec(same)] + _hbm_specs(n_x),
        scratch_shapes=[pltpu.VMEM(ST_SHAPE, F32), pltpu.VMEM(ST_SHAPE, F32)]
        + (_exchange_sems(n_x) if n_x else []),
        compiler_params=_params(("arbitrary",)), name=name)(xall, xall, v_c, v_c, dy_c, dy_c, hist, fin,
                                                            *exchange)


def _to_key_rows(wide, bsz, seq):
    z = wide.reshape(bsz, seq, N_GROUP, N_HEAD, K_HI, SUBLANES).transpose(1, 2, 4, 5, 0, 3)
    return z.reshape(seq, HEAD, LANES)


def _from_key_rows(g, bsz, seq):
    z = g.reshape(seq, N_GROUP, K_HI, SUBLANES, bsz, N_HEAD).transpose(4, 0, 1, 5, 2, 3)
    return z.reshape(bsz * seq, N_GROUP * D_RWKV)


def _to_value_rows(a, bsz, seq):
    z = a.reshape(bsz, seq, N_HEAD, V_HI, SUBLANES).transpose(1, 4, 3, 0, 2)
    return z.reshape(seq, SUBLANES, LANES)


def _from_value_rows(y, bsz, seq):
    z = y.reshape(seq, SUBLANES, V_HI, bsz, N_HEAD).transpose(3, 0, 4, 2, 1)
    return z.reshape(bsz * seq, D_RWKV)


def _pad_cols(a, segs):
    out, off = [], 0
    for w, wp in segs:
        out.append(a[..., off:off + w])
        if wp > w:
            out.append(jnp.zeros(a.shape[:-1] + (wp - w,), a.dtype))
        off += w
    return jnp.concatenate(out, axis=-1)


def _unpad_cols(a, segs):
    out, off = [], 0
    for w, wp in segs:
        out.append(a[..., off:off + w])
        off += wp
    return jnp.concatenate(out, axis=-1)


P_SEGS = ((3 * D_RWKV, 3 * D_RWKV), (D_LORA, 128), (D_LORA, 128), (D_GATE, 256), (3 * D_CONV, 3 * D_CONV))
S_SEGS = P_SEGS[:4]


def _pad_rows(a, rows):
    return jnp.concatenate([a, jnp.zeros((rows - a.shape[0], a.shape[1]), a.dtype)], axis=0)


LATE = ("w_out", "w_gate", "w_up", "w_down")


def _local_step(x, target, w, late=None):
    bsz, seq, _ = x.shape
    t = bsz * seq
    x2d = x.reshape(t, D_MODEL)
    tg2d = target.reshape(t, D_MODEL)
    row = lambda a: a.reshape(1, -1).astype(F32)

    w_in = _pad_cols(w["w_in"][0], P_SEGS)
    mu = _pad_cols(row(w["mu_shift"]), S_SEGS)
    wupf, wupb, aupf, aupb = (_pad_rows(w[n][0].astype(F32), 128) for n in ("w_up_f", "w_up_b", "a_up_f", "a_up_b"))
    gup = _pad_rows(w["g_up"][0].astype(F32), 256)
    conv_w = _pad_rows(w["conv_w"][0].astype(F32), SUBLANES)
    norm1, norm2, normf = row(w["norm1_w"]), row(w["norm2_w"]), row(w["norm_f_w"])
    vec = {n: row(w[n]) for n in VEC}
    head_of = jnp.arange(LANES) // HEAD
    bd = (head_of[:, None] == head_of[None, :]).astype(F32)
    pre_consts = [vec["k_k"], vec["w0_f"], vec["w0_b"], vec["a0_f"], vec["a0_b"], vec["k_a_f"], vec["k_a_b"],
                  wupf, wupb, aupf, aupb, gup, bd]
    post_consts = [vec["gn_w"], vec["gn_b"], vec["r_k_f"], vec["r_k_b"], bd]

    h1, = _rowwise(_rms, [x2d], [norm1], [D_MODEL], [], name="rms1_fwd")
    p = _mm(h1, w_in, name="mm_in")
    pss, oconv = _shift_conv_fwd(p, mu, conv_w, seq, name="shift_conv_fwd")
    pre_rows = [(pss, 0, 512), (pss, 1, 512), (pss, XW0 // 128, 128), (pss, XA0 // 128, 128), (pss, XG0 // 256, 256)]
    sc, g = _rowwise(_prescan_math, pre_rows, pre_consts, [[D_RWKV] * N_GROUP, D_RWKV], [], name="prescan_fwd")
    xall = _to_key_rows(sc, bsz, seq)
    v_l = _to_value_rows(pss[:, 2 * D_RWKV:3 * D_RWKV], bsz, seq)
    y_f, y_b, hist, fin, *gathered = _scan_fwd(xall, v_l, gather=[late[n] for n in LATE] if late else (),
                                               name="scan_fwd")
    w_out, w_gate, w_up, w_down = (
        (_from_slots(a, SHARD_AXIS[n]) if late else w[n])[0] for n, a in zip(LATE, gathered or LATE))
    y = _from_value_rows(y_f + y_b, bsz, seq)
    post_rows = [y, (pss, 0, 512), (pss, 2, 512), (sc, G_KD[0], 512), (sc, G_KD[1], 512), g]

    def post_fwd(y_, r_, v_, kdf_, kdb_, g_, oc_, *consts):
        return _postscan_math(y_, r_, v_, kdf_, kdb_, g_, *consts), oc_

    o, = _rowwise(post_fwd, post_rows + [oconv], post_consts, [[D_RWKV, D_CONV]], [], name="postscan_fwd")
    x1 = _mm(o, w_out, add=x2d, name="mm_out")
    h2, = _rowwise(_rms, [x1], [norm2], [D_MODEL], [], name="rms2_fwd")
    gg = _mm(h2, w_gate, name="mm_gate")
    uu = _mm(h2, w_up, name="mm_up")
    ff, = _rowwise(lambda a, c: jax.nn.silu(a) * c, [gg, uu], [], [D_FF], [], name="swiglu_fwd")
    x2 = _mm(ff, w_down, add=x1, name="mm_down")

    def final(x_, tg_, wn_):
        yo, vjp = jax.vjp(_rms, x_, wn_)
        err = yo - tg_
        dx_, dwn_ = vjp(err * (1.0 / D_MODEL))
        part = jnp.sum(jnp.sum(err * err, axis=1, keepdims=True), axis=0, keepdims=True) * (0.5 / D_MODEL)
        return dx_, part + jnp.zeros((1, LANES), F32), dwn_

    dx2, loss_acc, d_normf = _rowwise(final, [x2, tg2d], [normf], [D_MODEL], [(1, LANES), (1, D_MODEL)],
                                      name="loss_head")
    dff = _mm(dx2, w_down, tb=True, name="mm_down_dx")
    g_w_down = _mm(ff, dx2, ta=True, name="mm_down_dw")

    def swiglu_bwd(a, c, d):
        _, vjp = jax.vjp(lambda a_, c_: jax.nn.silu(a_) * c_, a, c)
        return vjp(d)

    dgg, duu = _rowwise(swiglu_bwd, [gg, uu, dff], [], [D_FF, D_FF], [], name="swiglu_bwd")
    dh2 = _mm(dgg, w_gate, tb=True, name="mm_gate_dx")
    dh2 = _mm(duu, w_up, tb=True, add=dh2, name="mm_up_dx")
    g_w_gate = _mm(h2, dgg, ta=True, name="mm_gate_dw")
    g_w_up = _mm(h2, duu, ta=True, name="mm_up_dw")

    def rms_bwd(x_, dh_, dres_, wn_):
        _, vjp = jax.vjp(_rms, x_, wn_)
        dx_, dwn_ = vjp(dh_)
        return dx_ + dres_, dwn_

    dx1, d_norm2 = _rowwise(rms_bwd, [x1, dh2, dx2], [norm2], [D_MODEL], [(1, D_MODEL)], name="rms2_bwd")
    do = _mm(dx1, w_out, tb=True, name="mm_out_dx")
    g_w_out = _mm(o, dx1, ta=True, name="mm_out_dw")

    def post_bwd(y_, r_, v_, kdf_, kdb_, g_, do_, *consts):
        _, vjp = jax.vjp(lambda *a: _postscan_math(*a, consts[4]), y_, r_, v_, kdf_, kdb_, g_, *consts[:4])
        return vjp(do_)

    (dy, dr_c, dv_c, dkdf_c, dkdb_c, dg, d_gn_w, d_gn_b, d_rkf, d_rkb) = _rowwise(
        post_bwd, post_rows + [(do, 0, 512)], post_consts, [D_RWKV] * 6, [(1, D_RWKV)] * 4, name="postscan_bwd")
    dy_l = _to_value_rows(dy, bsz, seq)
    late_grads = {"w_out": g_w_out[None], "w_gate": g_w_gate[None], "w_up": g_w_up[None], "w_down": g_w_down[None]}
    g_f, g_b, dv_f, dv_b, *late_parts = _scan_bwd(
        xall, v_l, dy_l, hist, fin, name="scan_bwd",
        exchange=[_to_slots(late_grads[n], SHARD_AXIS[n]).astype(BF16) for n in LATE] if late else ())
    dsc = _from_key_rows(g_f + g_b, bsz, seq)
    dv_s = _from_value_rows(dv_f + dv_b, bsz, seq)

    def pre_bwd(r_, k_, xw_, xa_, xg_, dkk_, dr_s, dwf_, dwb_, dbf_, dbb_, dkdf_s, dkdb_s,
                dr_c_, dv_c_, dv_s_, dkdf_c_, dkdb_c_, dg_, *consts):
        _, vjp = jax.vjp(lambda *a: _prescan_math(*a, consts[-1]), r_, k_, xw_, xa_, xg_, *consts[:-1])
        grads = vjp((dkk_, dr_s + dr_c_, dwf_, dwb_, dbf_, dbb_, dkdf_s + dkdf_c_, dkdb_s + dkdb_c_, dg_))
        dr_, dk_, dxw_, dxa_, dxg_ = grads[:5]
        return (dr_, dk_, dv_c_ + dv_s_, dxw_, dxa_, dxg_) + tuple(grads[5:])

    pre_b_rows = (pre_rows + [(dsc, j, 512) for j in range(N_GROUP)]
                  + [dr_c, dv_c, dv_s, dkdf_c, dkdb_c, dg])
    pre_b = _rowwise(pre_bwd, pre_b_rows, pre_consts, [[512, 512, 512, 128, 128, 256]],
                     [(1, D_RWKV)] * 7 + [(128, D_RWKV)] * 4 + [(256, D_RWKV)], name="prescan_bwd")
    d_pss = pre_b[0]
    d_kk_, d_w0f, d_w0b, d_a0f, d_a0b, d_kaf, d_kab, d_wupf, d_wupb, d_aupf, d_aupb, d_gup = pre_b[1:]
    dp, d_mu, d_conv = _shift_conv_bwd(p, d_pss, do, mu, conv_w, seq, name="shift_conv_bwd")
    dh1 = _mm(dp, w_in, tb=True, name="mm_in_dx")
    g_w_in = _mm(h1, dp, ta=True, name="mm_in_dw")
    dx, d_norm1 = _rowwise(rms_bwd, [x2d, dh1, dx1], [norm1], [D_MODEL], [(1, D_MODEL)], name="rms1_bwd")

    grads = {
        "norm1_w": d_norm1, "w_in": _unpad_cols(g_w_in, P_SEGS)[None], "mu_shift": _unpad_cols(d_mu, S_SEGS),
        "w_up_f": d_wupf[None, :D_LORA], "w0_f": d_w0f, "w_up_b": d_wupb[None, :D_LORA], "w0_b": d_w0b,
        "a_up_f": d_aupf[None, :D_LORA], "a0_f": d_a0f, "a_up_b": d_aupb[None, :D_LORA], "a0_b": d_a0b,
        "g_up": d_gup[None, :D_GATE], "k_k": d_kk_, "k_a_f": d_kaf, "k_a_b": d_kab,
        "r_k_f": d_rkf, "r_k_b": d_rkb, "gn_w": d_gn_w, "gn_b": d_gn_b, "conv_w": d_conv[None, :3],
        "w_out": g_w_out[None], "norm2_w": d_norm2, "w_gate": g_w_gate[None], "w_up": g_w_up[None],
        "w_down": g_w_down[None], "norm_f_w": d_normf,
    }
    return loss_acc[0, 0], dx.reshape(bsz, seq, D_MODEL), grads, dict(zip(LATE, late_parts))


def _hbm_specs(n):
    return [pl.BlockSpec(memory_space=pl.ANY)] * n


def _all_gather(arrs, *, name):
    n = len(arrs)

    def body(*refs):
        x_refs, out_refs = refs[:n], refs[n:2 * n]
        send_sems, recv_sems, local_sems = refs[2 * n:]
        x, y, c = lax.axis_index("x"), lax.axis_index("y"), lax.axis_index("c")
        me, sibling = (x, y, c), (x, y, 1 - c)
        chips = [(1 - x, y), (x, 1 - y), (1 - x, 1 - y)]

        def slot(a, px, py, pc):
            return out_refs[a].at[4 * px + 2 * py + pc]

        def copy(a, k, block, to, src=None):
            return pltpu.make_async_remote_copy(
                src_ref=slot(a, *block) if src is None else src, dst_ref=slot(a, *block),
                send_sem=send_sems.at[k, a], recv_sem=recv_sems.at[k, a],
                device_id=to, device_id_type=pl.DeviceIdType.MESH)

        mine = [pltpu.make_async_copy(x_refs[a], slot(a, *me), local_sems.at[a]) for a in range(n)]
        for cp in mine:
            cp.start()
        first = []
        for a in range(n):
            first.append(copy(a, 0, me, sibling, src=x_refs[a]))
            first += [copy(a, 1 + j, me, (*chip, c), src=x_refs[a]) for j, chip in enumerate(chips)]
        for cp in first:
            cp.start()
        passed = []
        for j, chip in enumerate(chips):
            for a in range(n):
                copy(a, 1 + j, (*chip, c), me).wait_recv()
                cp = copy(a, 4 + j, (*chip, c), sibling)
                cp.start()
                passed.append(cp)
        for a in range(n):
            copy(a, 0, sibling, me).wait_recv()
            for j, chip in enumerate(chips):
                copy(a, 4 + j, (*chip, 1 - c), me).wait_recv()
        for cp in first + passed:
            cp.wait_send()
        for cp in mine:
            cp.wait()

    return pl.pallas_call(
        body, out_shape=[jax.ShapeDtypeStruct((N_DEV,) + a.shape, a.dtype) for a in arrs],
        in_specs=_hbm_specs(n), out_specs=_hbm_specs(n),
        scratch_shapes=[pltpu.SemaphoreType.DMA((7, n)), pltpu.SemaphoreType.DMA((7, n)),
                        pltpu.SemaphoreType.DMA((n,))],
        name=name)(*arrs)


def _exchange(sliced, whole, *, name):
    arrs = list(sliced) + list(whole)
    n, n_sliced = len(arrs), len(sliced)

    def body(*refs):
        copies = _exchange_copies(refs[:n], refs[n:2 * n], n_sliced, *refs[2 * n:])
        for cp in copies:
            cp.start()
        for cp in copies:
            cp.wait()

    return pl.pallas_call(
        body, out_shape=_exchange_out_shapes(arrs, n_sliced), in_specs=_hbm_specs(n), out_specs=_hbm_specs(n),
        scratch_shapes=_exchange_sems(n), name=name)(*arrs)


def _exchange_out_shapes(arrs, n_sliced):
    return [jax.ShapeDtypeStruct(a.shape if i < n_sliced else (N_DEV,) + a.shape, a.dtype)
            for i, a in enumerate(arrs)]


def _exchange_sems(n):
    return [pltpu.SemaphoreType.DMA((7, n)), pltpu.SemaphoreType.DMA((7, n)), pltpu.SemaphoreType.DMA((n,))]


def _exchange_copies(in_refs, out_refs, n_sliced, send_sems, recv_sems, local_sems):
    n = len(in_refs)
    x, y, c = lax.axis_index("x"), lax.axis_index("y"), lax.axis_index("c")
    me = 4 * x + 2 * y + c

    def src(a, dev):
        return in_refs[a].at[dev] if a < n_sliced else in_refs[a]

    copies = [pltpu.make_async_copy(src(a, me), out_refs[a].at[me], local_sems.at[a]) for a in range(n)]
    for k in range(1, N_DEV):
        px = 1 - x if k & 4 else x
        py = 1 - y if k & 2 else y
        pc = 1 - c if k & 1 else c
        for a in range(n):
            copies.append(pltpu.make_async_remote_copy(
                src_ref=src(a, 4 * px + 2 * py + pc), dst_ref=out_refs[a].at[me],
                send_sem=send_sems.at[k - 1, a], recv_sem=recv_sems.at[k - 1, a],
                device_id=(px, py, pc), device_id_type=pl.DeviceIdType.MESH))
    return copies


def _adam_math(g, w, m, v):
    nm = ADAM_B1 * m + (1.0 - ADAM_B1) * g
    nv = ADAM_B2 * v + (1.0 - ADAM_B2) * (g * g)
    m_hat = nm / (1.0 - ADAM_B1 ** ADAM_STEP)
    v_hat = nv / (1.0 - ADAM_B2 ** ADAM_STEP)
    return -ADAM_LR * (m_hat / (jnp.sqrt(v_hat) + ADAM_EPS) + ADAM_WD * w), nm, nv


def _slot_sum(ref):
    g = ref[0].astype(F32)
    for s in range(1, N_DEV):
        g = g + ref[s].astype(F32)
    return g


def _adamw_big(parts, w, m, v, *, name):
    _, rws, cols = w.shape
    tr = _tile(rws, (256, 176, 128))

    def kern(p_ref, w_ref, m_ref, v_ref, g_ref, d_ref, nm_ref, nv_ref):
        g = _slot_sum(p_ref)
        g_ref[...] = g
        d_ref[...], nm_ref[...], nv_ref[...] = _adam_math(g, w_ref[...], m_ref[...], v_ref[...])

    spec = pl.BlockSpec((1, tr, cols), lambda i: (0, i, 0))
    return pl.pallas_call(
        kern, out_shape=[jax.ShapeDtypeStruct(w.shape, F32)] * 4, grid=(rws // tr,),
        in_specs=[pl.BlockSpec((N_DEV, 1, tr, cols), lambda i: (0, 0, i, 0)), spec, spec, spec],
        out_specs=[spec] * 4, compiler_params=_params(("parallel",)), name=name)(parts, w, m, v)


def _adamw_small(lora_parts, vec_parts, wide_parts, wmv, *, name):
    names = LORA + VEC + WIDE
    n_l, n = len(LORA), len(names)
    flat = [a for trip in wmv for a in trip]

    def kern(*refs):
        l_refs, vec_ref, wide_ref = refs[:n_l], refs[n_l], refs[n_l + 1]
        in_refs = refs[n_l + 2:n_l + 2 + 3 * n]
        out_refs = refs[n_l + 2 + 3 * n:]
        vec_sum, wide_sum = _slot_sum(vec_ref), _slot_sum(wide_ref)
        for i, nm in enumerate(names):
            w_ref, m_ref, v_ref = in_refs[3 * i:3 * i + 3]
            if i < n_l:
                g = _slot_sum(l_refs[i])
            elif nm in VEC:
                g = vec_sum[i - n_l:i - n_l + 1, :]
            else:
                g = wide_sum[WIDE.index(nm):WIDE.index(nm) + 1, :w_ref.shape[-1]]
            o = out_refs[4 * i:4 * i + 4]
            o[0][...] = g
            o[1][...], o[2][...], o[3][...] = _adam_math(g, w_ref[...], m_ref[...], v_ref[...])

    out_shape = [jax.ShapeDtypeStruct(trip[0].shape, F32) for trip in wmv for _ in range(4)]
    outs = pl.pallas_call(kern, out_shape=out_shape, name=name,
                          compiler_params=pltpu.CompilerParams(vmem_limit_bytes=VMEM_LIMIT))(
        *lora_parts, vec_parts, wide_parts, *flat)
    return [tuple(outs[4 * i:4 * i + 4]) for i in range(n)]


def _to_slots(g, axis):
    _, rws, cols = g.shape
    if axis == 1:
        return g.reshape(N_DEV, 1, rws // N_DEV, cols)
    return g.reshape(1, rws, N_DEV, cols // N_DEV).transpose(2, 0, 1, 3)


def _from_slots(got, axis):
    _, _, rws, cols = got.shape
    if axis == 1:
        return got.reshape(1, N_DEV * rws, cols)
    return got.transpose(1, 2, 0, 3).reshape(1, rws, N_DEV * cols)


def _pad_lanes(a, width):
    return jnp.concatenate([a, jnp.zeros(a.shape[:-1] + (width - a.shape[-1],), a.dtype)], axis=-1)


def kernel(x, norm1_w, w_in, mu_shift, w_up_f, w0_f, w_up_b, w0_b, a_up_f, a0_f, a_up_b, a0_b, g_up, k_k, k_a_f, k_a_b, r_k_f, r_k_b, gn_w, gn_b, conv_w, w_out, norm2_w, w_gate, w_up, w_down, norm_f_w, loss_target, m_norm1_w, m_w_in, m_mu_shift, m_w_up_f, m_w0_f, m_w_up_b, m_w0_b, m_a_up_f, m_a0_f, m_a_up_b, m_a0_b, m_g_up, m_k_k, m_k_a_f, m_k_a_b, m_r_k_f, m_r_k_b, m_gn_w, m_gn_b, m_conv_w, m_w_out, m_norm2_w, m_w_gate, m_w_up, m_w_down, m_norm_f_w, v_norm1_w, v_w_in, v_mu_shift, v_w_up_f, v_w0_f, v_w_up_b, v_w0_b, v_a_up_f, v_a0_f, v_a_up_b, v_a0_b, v_g_up, v_k_k, v_k_a_f, v_k_a_b, v_r_k_f, v_r_k_b, v_gn_w, v_gn_b, v_conv_w, v_w_out, v_norm2_w, v_w_gate, v_w_up, v_w_down, v_norm_f_w):
    local = dict(norm1_w=norm1_w, w_in=w_in, mu_shift=mu_shift, w_up_f=w_up_f, w0_f=w0_f, w_up_b=w_up_b,
                 w0_b=w0_b, a_up_f=a_up_f, a0_f=a0_f, a_up_b=a_up_b, a0_b=a0_b, g_up=g_up, k_k=k_k, k_a_f=k_a_f,
                 k_a_b=k_a_b, r_k_f=r_k_f, r_k_b=r_k_b, gn_w=gn_w, gn_b=gn_b, conv_w=conv_w, w_out=w_out,
                 norm2_w=norm2_w, w_gate=w_gate, w_up=w_up, w_down=w_down, norm_f_w=norm_f_w)
    mom_m = dict(norm1_w=m_norm1_w, w_in=m_w_in, mu_shift=m_mu_shift, w_up_f=m_w_up_f, w0_f=m_w0_f,
                 w_up_b=m_w_up_b, w0_b=m_w0_b, a_up_f=m_a_up_f, a0_f=m_a0_f, a_up_b=m_a_up_b, a0_b=m_a0_b,
                 g_up=m_g_up, k_k=m_k_k, k_a_f=m_k_a_f, k_a_b=m_k_a_b, r_k_f=m_r_k_f, r_k_b=m_r_k_b,
                 gn_w=m_gn_w, gn_b=m_gn_b, conv_w=m_conv_w, w_out=m_w_out, norm2_w=m_norm2_w, w_gate=m_w_gate,
                 w_up=m_w_up, w_down=m_w_down, norm_f_w=m_norm_f_w)
    mom_v = dict(norm1_w=v_norm1_w, w_in=v_w_in, mu_shift=v_mu_shift, w_up_f=v_w_up_f, w0_f=v_w0_f,
                 w_up_b=v_w_up_b, w0_b=v_w0_b, a_up_f=v_a_up_f, a0_f=v_a0_f, a_up_b=v_a_up_b, a0_b=v_a0_b,
                 g_up=v_g_up, k_k=v_k_k, k_a_f=v_k_a_f, k_a_b=v_k_a_b, r_k_f=v_r_k_f, r_k_b=v_r_k_b,
                 gn_w=v_gn_w, gn_b=v_gn_b, conv_w=v_conv_w, w_out=v_w_out, norm2_w=v_norm2_w, w_gate=v_w_gate,
                 w_up=v_w_up, w_down=v_w_down, norm_f_w=v_norm_f_w)

    early = ("w_in",) + LORA
    got = _all_gather([local["w_in"].astype(BF16)] + [local[n] for n in LORA], name="gather")
    full = dict(local)
    full.update({n: _from_slots(a, SHARD_AXIS[n]) for n, a in zip(early, got)})

    loss_part, grad_x, grads, late_parts = _local_step(x, loss_target, full,
                                                       late={n: local[n].astype(BF16) for n in LATE})
    loss = lax.psum(loss_part, ("x", "y", "c"))

    vec_rows = jnp.concatenate([grads[n] for n in VEC] + [jnp.zeros((16 - len(VEC), D_RWKV), F32)], axis=0)
    wide_rows = jnp.concatenate([_pad_lanes(grads[n], WIDE_ROW) for n in WIDE]
                                + [jnp.zeros((SUBLANES - len(WIDE), WIDE_ROW), F32)], axis=0)
    slots = [_to_slots(grads[n], SHARD_AXIS[n]).astype(BF16 if n in BIG else F32) for n in early]
    recv = _exchange(slots, [vec_rows, wide_rows], name="grad_exchange")
    out = {}
    for n in BIG:
        parts = recv[0] if n == "w_in" else late_parts[n]
        out[n] = _adamw_big(parts, local[n], mom_m[n], mom_v[n], name="adamw_" + n)

    def small_form(n, a):
        if n in LORA:
            return a
        a = a.reshape(1, -1)
        return _pad_lanes(a, WIDE_ROW) if n == "mu_shift" else a

    small = LORA + VEC + WIDE
    res = _adamw_small(recv[1:len(early)], recv[len(early)], recv[len(early) + 1],
                       [tuple(small_form(n, d[n]) for d in (local, mom_m, mom_v)) for n in small],
                       name="adamw_small")
    for n, quad in zip(small, res):
        out[n] = tuple(a[..., :local[n].size].reshape(local[n].shape) if n not in LORA else a for a in quad)
    return (loss, grad_x, *[out[n][i] for i in range(4) for n in WEIGHTS])
```

```python
import functools

import jax
import jax.numpy as jnp
from jax import lax
from jax.experimental import pallas as pl
from jax.experimental.pallas import tpu as pltpu

F32 = jnp.float32
BF16 = jnp.bfloat16
HIGHEST = lax.Precision.HIGHEST

N_DEV = 8
D_MODEL = 1024
D_RWKV = 512
D_CONV = 512
HEAD = 64
N_HEAD = D_RWKV // HEAD
D_LORA = 64
D_GATE = 160
D_FF = 2816
D_SHIFTED = 3 * D_RWKV + 2 * D_LORA + D_GATE
D_IN = D_SHIFTED + 3 * D_CONV
XW0, XA0, XG0 = 1536, 1664, 1792
D_SP = 2048
D_INP = D_SP + 3 * D_CONV
LOG_DECAY_SCALE = 0.606531
RMS_EPS = 1e-6
GN_EPS = 64e-5
NORM_EPS = 1e-12
ADAM_LR, ADAM_B1, ADAM_B2, ADAM_EPS, ADAM_WD, ADAM_STEP = 0.001, 0.9, 0.999, 1e-08, 0.01, 10

LANES = 128
SUBLANES = 8
VMEM_LIMIT = 48 * 1024 * 1024
SCAN_CHUNK = 16
ROW_TILE = 128

BIG = ("w_in", "w_out", "w_gate", "w_up", "w_down")
LORA = ("w_up_f", "w_up_b", "a_up_f", "a_up_b", "g_up", "conv_w")
SHARD_AXIS = {"w_in": 2, "w_out": 1, "w_gate": 2, "w_up": 2, "w_down": 1, "w_up_f": 2, "w_up_b": 2,
              "a_up_f": 2, "a_up_b": 2, "g_up": 2, "conv_w": 2}
VEC = ("w0_f", "w0_b", "a0_f", "a0_b", "k_k", "k_a_f", "k_a_b", "r_k_f", "r_k_b", "gn_w", "gn_b")
WIDE = ("mu_shift", "norm1_w", "norm2_w", "norm_f_w")
WIDE_ROW = 2048
WEIGHTS = ("norm1_w", "w_in", "mu_shift", "w_up_f", "w0_f", "w_up_b", "w0_b", "a_up_f", "a0_f", "a_up_b",
           "a0_b", "g_up", "k_k", "k_a_f", "k_a_b", "r_k_f", "r_k_b", "gn_w", "gn_b", "conv_w", "w_out",
           "norm2_w", "w_gate", "w_up", "w_down", "norm_f_w")


def _params(sem, limit=VMEM_LIMIT):
    return pltpu.CompilerParams(dimension_semantics=sem, vmem_limit_bytes=limit)


def _tile(n, cands):
    for c in cands:
        if n % c == 0:
            return c
    raise ValueError(f"no tile for {n}")


def _mm(a, b, *, ta=False, tb=False, add=None, name):
    (k_dim, m) = a.shape if ta else a.shape[::-1]
    (k2, n) = b.shape[::-1] if tb else b.shape
    assert k_dim == k2, (a.shape, b.shape, ta, tb)
    tm = _tile(m, (1408, 1024, 512, 256, 128))
    tn = _tile(n, (1408, 1024, 896, 512, 256, 128))
    tk = k_dim if k_dim <= 1024 else _tile(k_dim, (1408, 896, 512, 256, 128))
    nk = k_dim // tk
    dims = (((0 if ta else 1,), (1 if tb else 0,)), ((), ()))

    def kern(*refs):
        if add is None:
            a_ref, b_ref, o_ref, acc_ref = refs
        else:
            a_ref, b_ref, add_ref, o_ref, acc_ref = refs
        k = pl.program_id(2)

        @pl.when(k == 0)
        def _():
            acc_ref[...] = jnp.zeros_like(acc_ref)

        acc_ref[...] += lax.dot_general(a_ref[...].astype(BF16), b_ref[...].astype(BF16), dims,
                                        preferred_element_type=F32)

        @pl.when(k == nk - 1)
        def _():
            if add is None:
                o_ref[...] = acc_ref[...]
            else:
                o_ref[...] = acc_ref[...] + add_ref[...]

    a_spec = (pl.BlockSpec((tk, tm), lambda i, j, k: (k, i)) if ta
              else pl.BlockSpec((tm, tk), lambda i, j, k: (i, k)))
    b_spec = (pl.BlockSpec((tn, tk), lambda i, j, k: (j, k)) if tb
              else pl.BlockSpec((tk, tn), lambda i, j, k: (k, j)))
    o_spec = pl.BlockSpec((tm, tn), lambda i, j, k: (i, j))
    in_specs = [a_spec, b_spec] + ([o_spec] if add is not None else [])
    args = (a, b) + ((add,) if add is not None else ())
    return pl.pallas_call(
        kern, out_shape=jax.ShapeDtypeStruct((m, n), F32), grid=(m // tm, n // tn, nk),
        in_specs=in_specs, out_specs=o_spec, scratch_shapes=[pltpu.VMEM((tm, tn), F32)],
        compiler_params=_params(("parallel", "parallel", "arbitrary")), name=name)(*args)


def _rowwise(fn, rows, consts, out_rows, out_accs, *, name, tb=ROW_TILE, out_dtype=F32):
    t = (rows[0][0] if isinstance(rows[0], tuple) else rows[0]).shape[0]
    n_r, n_c, n_o, n_a = len(rows), len(consts), len(out_rows), len(out_accs)
    pieces = [w if isinstance(w, (list, tuple)) else [w] for w in out_rows]

    def kern(*refs):
        r_refs = refs[:n_r]
        c_refs = refs[n_r:n_r + n_c]
        o_refs = refs[n_r + n_c:n_r + n_c + n_o]
        a_refs = refs[n_r + n_c + n_o:]
        vals = fn(*[r[...] for r in r_refs], *[c[...] for c in c_refs])
        vals = list(vals) if isinstance(vals, (tuple, list)) else [vals]
        pos = 0
        for o_ref, ws in zip(o_refs, pieces):
            off = 0
            for w in ws:
                o_ref[:, off:off + w] = vals[pos].astype(o_ref.dtype)
                off += w
                pos += 1
        if n_a:
            @pl.when(pl.program_id(0) == 0)
            def _():
                for a_ref in a_refs:
                    a_ref[...] = jnp.zeros_like(a_ref)
            for a_ref, v in zip(a_refs, vals[pos:]):
                a_ref[...] += v

    in_specs, args = [], []
    for r in rows:
        if isinstance(r, tuple):
            arr, blk, w = r
            in_specs.append(pl.BlockSpec((tb, w), functools.partial(lambda i, blk: (i, blk), blk=blk)))
        else:
            arr = r
            in_specs.append(pl.BlockSpec((tb, arr.shape[1]), lambda i: (i, 0)))
        args.append(arr)
    for c in consts:
        in_specs.append(pl.BlockSpec(c.shape, lambda i: (0, 0)))
        args.append(c)
    out_shape = [jax.ShapeDtypeStruct((t, sum(ws)), out_dtype) for ws in pieces]
    out_specs = [pl.BlockSpec((tb, sum(ws)), lambda i: (i, 0)) for ws in pieces]
    for shp in out_accs:
        out_shape.append(jax.ShapeDtypeStruct(shp, F32))
        out_specs.append(pl.BlockSpec(shp, lambda i: (0, 0)))
    res = pl.pallas_call(
        kern, out_shape=out_shape, grid=(t // tb,), in_specs=in_specs, out_specs=out_specs,
        compiler_params=_params(("arbitrary",) if n_a else ("parallel",)), name=name)(*args)
    return res


def _rms(x, w):
    return x * lax.rsqrt(jnp.mean(x * x, axis=-1, keepdims=True) + RMS_EPS) * w


def _seg_sum(x, bd):
    return jnp.concatenate(
        [jnp.dot(x[:, LANES * j:LANES * (j + 1)], bd, precision=HIGHEST, preferred_element_type=F32)
         for j in range(x.shape[1] // LANES)], axis=1)


@jax.custom_vjp
def _seg(x, bd):
    return _seg_sum(x, bd)


_seg.defvjp(lambda x, bd: (_seg_sum(x, bd), bd), lambda bd, ct: (_seg_sum(ct, bd), jnp.zeros_like(bd)))


def _colsum(x):
    return jnp.sum(x, axis=0, keepdims=True)


def _prescan_math(r, k, xw, xa, xg, k_k, w0f, w0b, a0f, a0b, kaf, kab, wupf, wupb, aupf, aupb, gup, bd):
    kkr = k * k_k
    norm = jnp.sqrt(_seg(kkr * kkr, bd))
    kk = kkr / jnp.maximum(norm, NORM_EPS)
    th = jnp.tanh(xw)

    def direction(w0, wup, a0, aup, ka):
        logit = w0 + jnp.dot(th, wup, preferred_element_type=F32)
        w = jnp.exp(-LOG_DECAY_SCALE * jax.nn.sigmoid(logit))
        a = jax.nn.sigmoid(a0 + jnp.dot(xa, aup, preferred_element_type=F32))
        kd = k * (1.0 + (a - 1.0) * ka)
        return w, kd, kk * a

    wf, kdf, bf = direction(w0f, wupf, a0f, aupf, kaf)
    wb, kdb, bb = direction(w0b, wupb, a0b, aupb, kab)
    g = jnp.dot(jax.nn.sigmoid(xg), gup, preferred_element_type=F32)
    return kk, r, wf, wb, bf, bb, kdf, kdb, g


def _postscan_math(y, r, v, kdf, kdb, g, gn_w, gn_b, rkf, rkb, bd):
    mean = _seg(y, bd) * (1.0 / HEAD)
    yc = y - mean
    var = _seg(yc * yc, bd) * (1.0 / HEAD)
    yg = yc * lax.rsqrt(var + GN_EPS) * gn_w + gn_b
    bonus = (_seg(r * kdf * rkf, bd) + _seg(r * kdb * rkb, bd)) * v
    return (yg + bonus) * g


def _halo_specs(width, col_blk, tb, t):
    nb = t // SUBLANES
    step = tb // SUBLANES
    main = pl.BlockSpec((tb, width), lambda i: (i, col_blk))
    prev = pl.BlockSpec((SUBLANES, width), lambda i: (jnp.maximum(i * step - 1, 0), col_blk))
    nxt = pl.BlockSpec((SUBLANES, width), lambda i: (jnp.minimum((i + 1) * step, nb - 1), col_blk))
    return [main, prev, nxt]


def _neighbours(z, prev8, next8, first, last):
    tb = z.shape[0]
    row = lax.broadcasted_iota(jnp.int32, z.shape, 0)
    prow = jnp.where(first, 0.0, prev8[SUBLANES - 1:SUBLANES, :])
    nrow = jnp.where(last, 0.0, next8[0:1, :])
    down = jnp.where(row == 0, prow, pltpu.roll(z, 1, 0))
    up = jnp.where(row == tb - 1, nrow, pltpu.roll(z, tb - 1, 0))
    return down, up


def _shift_conv_fwd(p, mu, conv_w, seq, *, name, tb=ROW_TILE):
    t = p.shape[0]
    per_seq = seq // tb

    def kern(p_ref, pp_ref, pn_ref, mu_ref, cw_ref, pss_ref, oc_ref):
        i = pl.program_id(0)
        first = (i % per_seq) == 0
        last = (i % per_seq) == per_seq - 1
        ps = p_ref[:, :D_SP]
        down, up = _neighbours(ps, pp_ref[:, :D_SP], pn_ref[:, :D_SP], first, last)
        pss_ref[...] = ps + mu_ref[...] * (0.5 * (down + up) - ps)
        gb = p_ref[:, D_SP:D_SP + D_CONV]
        u = p_ref[:, D_SP + D_CONV:D_SP + 2 * D_CONV] * p_ref[:, D_SP + 2 * D_CONV:]
        u_p = pp_ref[:, D_SP + D_CONV:D_SP + 2 * D_CONV] * pp_ref[:, D_SP + 2 * D_CONV:]
        u_n = pn_ref[:, D_SP + D_CONV:D_SP + 2 * D_CONV] * pn_ref[:, D_SP + 2 * D_CONV:]
        udown, uup = _neighbours(u, u_p, u_n, first, last)
        oc_ref[...] = gb * (cw_ref[0:1, :] * udown + cw_ref[1:2, :] * u + cw_ref[2:3, :] * uup)

    return pl.pallas_call(
        kern,
        out_shape=[jax.ShapeDtypeStruct((t, D_SP), F32), jax.ShapeDtypeStruct((t, D_CONV), F32)],
        grid=(t // tb,),
        in_specs=_halo_specs(D_INP, 0, tb, t) + [pl.BlockSpec((1, D_SP), lambda i: (0, 0)),
                                                 pl.BlockSpec((SUBLANES, D_CONV), lambda i: (0, 0))],
        out_specs=[pl.BlockSpec((tb, D_SP), lambda i: (i, 0)), pl.BlockSpec((tb, D_CONV), lambda i: (i, 0))],
        compiler_params=_params(("parallel",)), name=name)(p, p, p, mu, conv_w)


def _shift_conv_bwd(p, d_pss, d_o, mu, conv_w, seq, *, name, tb=ROW_TILE):
    t = p.shape[0]
    per_seq = seq // tb

    def kern(p_ref, pp_ref, pn_ref, d_ref, dp_ref, dn_ref, do_ref, dop_ref, don_ref, mu_ref, cw_ref,
             out_ref, dmu_ref, dcw_ref):
        i = pl.program_id(0)
        first = (i % per_seq) == 0
        last = (i % per_seq) == per_seq - 1

        @pl.when(i == 0)
        def _():
            dmu_ref[...] = jnp.zeros_like(dmu_ref)
            dcw_ref[...] = jnp.zeros_like(dcw_ref)

        mu_v = mu_ref[...]
        ps = p_ref[:, :D_SP]
        down, up = _neighbours(ps, pp_ref[:, :D_SP], pn_ref[:, :D_SP], first, last)
        d = d_ref[...]
        ddown, dup = _neighbours(d, dp_ref[...], dn_ref[...], first, last)
        out_ref[:, :D_SP] = (d - mu_v * d + 0.5 * (mu_v * ddown + mu_v * dup)).astype(out_ref.dtype)
        dmu_ref[...] += _colsum(d * (0.5 * (down + up) - ps))

        def parts(ref):
            return (ref[:, D_SP:D_SP + D_CONV], ref[:, D_SP + D_CONV:D_SP + 2 * D_CONV],
                    ref[:, D_SP + 2 * D_CONV:])

        gb, gc, hh = parts(p_ref)
        gb_p, gc_p, hh_p = parts(pp_ref)
        gb_n, gc_n, hh_n = parts(pn_ref)
        u = gc * hh
        udown, uup = _neighbours(u, gc_p * hh_p, gc_n * hh_n, first, last)
        cw0, cw1, cw2 = cw_ref[0:1, :], cw_ref[1:2, :], cw_ref[2:3, :]
        do = do_ref[...]
        duc = do * gb
        ducdown, ducup = _neighbours(duc, dop_ref[...] * gb_p, don_ref[...] * gb_n, first, last)
        du = cw0 * ducup + cw1 * duc + cw2 * ducdown
        out_ref[:, D_SP:D_SP + D_CONV] = (do * (cw0 * udown + cw1 * u + cw2 * uup)).astype(out_ref.dtype)
        out_ref[:, D_SP + D_CONV:D_SP + 2 * D_CONV] = (du * hh).astype(out_ref.dtype)
        out_ref[:, D_SP + 2 * D_CONV:] = (du * gc).astype(out_ref.dtype)
        dcw_ref[0:1, :] += _colsum(duc * udown)
        dcw_ref[1:2, :] += _colsum(duc * u)
        dcw_ref[2:3, :] += _colsum(duc * uup)

    return pl.pallas_call(
        kern,
        out_shape=[jax.ShapeDtypeStruct((t, D_INP), BF16), jax.ShapeDtypeStruct((1, D_SP), F32),
                   jax.ShapeDtypeStruct((SUBLANES, D_CONV), F32)],
        grid=(t // tb,),
        in_specs=(_halo_specs(D_INP, 0, tb, t) + _halo_specs(D_SP, 0, tb, t) + _halo_specs(D_CONV, 1, tb, t)
                  + [pl.BlockSpec((1, D_SP), lambda i: (0, 0)),
                     pl.BlockSpec((SUBLANES, D_CONV), lambda i: (0, 0))]),
        out_specs=[pl.BlockSpec((tb, D_INP), lambda i: (i, 0)), pl.BlockSpec((1, D_SP), lambda i: (0, 0)),
                   pl.BlockSpec((SUBLANES, D_CONV), lambda i: (0, 0))],
        compiler_params=_params(("arbitrary",)), name=name)(p, p, p, d_pss, d_pss, d_pss, d_o, d_o, d_o, mu, conv_w)


N_CHAIN = 16
V_LO = LANES // N_CHAIN
V_HI = HEAD // V_LO
N_GROUP = LANES // N_CHAIN
G_KK, G_R, G_W, G_B, G_KD = 0, 1, (2, 3), (4, 5), (6, 7)


def _group(x, j, lane):
    g = pltpu.roll(x, (LANES - N_CHAIN * j) % LANES, 1) if j else x
    g = jnp.where(lane < N_CHAIN, g, pltpu.roll(g, N_CHAIN, 1))
    g = jnp.where(lane < 2 * N_CHAIN, g, pltpu.roll(g, 2 * N_CHAIN, 1))
    return jnp.where(lane < 4 * N_CHAIN, g, pltpu.roll(g, 4 * N_CHAIN, 1))


def _scan_inputs(x, d, lane):
    return [_group(x, j, lane) for j in (G_KK, G_R, G_W[d], G_B[d], G_KD[d])]


def _lane_scan_fwd(xall, v_l, *, name):
    steps = xall.shape[0]
    nc = steps // SCAN_CHUNK
    mirror = lambda c: nc - 1 - c

    def kern(xf_ref, xb_ref, vf_ref, vb_ref, yf_ref, yb_ref, hist_ref, fin_ref, st_ref):
        c = pl.program_id(0)

        @pl.when(c == 0)
        def _():
            st_ref[...] = jnp.zeros_like(st_ref)

        row = lax.broadcasted_iota(jnp.int32, (V_HI, LANES), 0)
        lane = lax.broadcasted_iota(jnp.int32, (HEAD, LANES), 1)

        def step(i, carry):
            j = SCAN_CHUNK - 1 - i
            for d, (x_t, v_t, y_ref, at) in enumerate(((xf_ref[i], vf_ref[i], yf_ref, i),
                                                       (xb_ref[j], vb_ref[j], yb_ref, j))):
                kk_t, r_t, w_t, b_t, kd_t = _scan_inputs(x_t, d, lane)
                y_t = jnp.zeros((V_HI, LANES), F32)
                for vh in range(V_HI):
                    tile = d * V_HI + vh
                    state = st_ref[tile]
                    hist_ref[i, tile] = state
                    sa = _colsum(state * kk_t)
                    state = state * w_t - sa * b_t + v_t[vh:vh + 1, :] * kd_t
                    st_ref[tile] = state
                    y_t = jnp.where(row == vh, _colsum(state * r_t), y_t)
                y_ref[at] = y_t
            return carry

        lax.fori_loop(0, SCAN_CHUNK, step, 0)

        @pl.when(c == nc - 1)
        def _():
            fin_ref[...] = st_ref[...]

    def k_spec(fn):
        return pl.BlockSpec((SCAN_CHUNK, HEAD, LANES), lambda c: (fn(c), 0, 0))

    def v_spec(fn):
        return pl.BlockSpec((SCAN_CHUNK, V_HI, LANES), lambda c: (fn(c), 0, 0))

    same = lambda c: c
    st_shape = (2 * V_HI, HEAD, LANES)
    return pl.pallas_call(
        kern,
        out_shape=[jax.ShapeDtypeStruct((steps, V_HI, LANES), F32)] * 2
        + [jax.ShapeDtypeStruct((steps,) + st_shape, F32), jax.ShapeDtypeStruct(st_shape, F32)],
        grid=(nc,), in_specs=[k_spec(same), k_spec(mirror), v_spec(same), v_spec(mirror)],
        out_specs=[v_spec(same), v_spec(mirror),
                   pl.BlockSpec((SCAN_CHUNK,) + st_shape, lambda c: (c, 0, 0, 0)),
                   pl.BlockSpec(st_shape, lambda c: (0, 0, 0))],
        scratch_shapes=[pltpu.VMEM(st_shape, F32)],
        compiler_params=_params(("arbitrary",)), name=name)(xall, xall, v_l, v_l)


def _lane_scan_bwd(xall, v_l, dy_l, hist, fin, *, name):
    steps = xall.shape[0]
    nc = steps // SCAN_CHUNK
    back = lambda c: nc - 1 - c
    same = lambda c: c

    def kern(xf_ref, xb_ref, vf_ref, vb_ref, dyf_ref, dyb_ref, hist_ref, fin_ref,
             gf_ref, gb_ref, dvf_ref, dvb_ref, ds_ref, after_ref):
        c = pl.program_id(0)

        @pl.when(c == 0)
        def _():
            ds_ref[...] = jnp.zeros_like(ds_ref)
            after_ref[...] = fin_ref[...]

        row = lax.broadcasted_iota(jnp.int32, (V_HI, LANES), 0)
        lane = lax.broadcasted_iota(jnp.int32, (HEAD, LANES), 1)
        grp = lax.shift_right_logical(lane, jnp.full_like(lane, 4))

        def group_sum(x):
            x = x + pltpu.roll(x, 4 * N_CHAIN, 1)
            x = x + pltpu.roll(x, 2 * N_CHAIN, 1)
            return x + pltpu.roll(x, N_CHAIN, 1)

        def step(ii, carry):
            i = SCAN_CHUNK - 1 - ii
            j = ii
            for d, (x_t, v_t, dy_t, g_ref, dv_ref, at) in enumerate((
                    (xf_ref[i], vf_ref[i], dyf_ref[i], gf_ref, dvf_ref, i),
                    (xb_ref[j], vb_ref[j], dyb_ref[j], gb_ref, dvb_ref, j))):
                kk_t, r_t, w_t, b_t, kd_t = _scan_inputs(x_t, d, lane)
                dv_t = jnp.zeros((V_HI, LANES), F32)
                zero = jnp.zeros((HEAD, LANES), F32)
                dkk, dr, dw, db, dkd = zero, zero, zero, zero, zero
                for vh in range(V_HI):
                    tile = d * V_HI + vh
                    before = hist_ref[i, tile]
                    dy_r, v_r = dy_t[vh:vh + 1, :], v_t[vh:vh + 1, :]
                    g = ds_ref[tile] + dy_r * r_t
                    sa = _colsum(before * kk_t)
                    dsa = -_colsum(g * b_t)
                    dv_t = jnp.where(row == vh, _colsum(g * kd_t), dv_t)
                    dr = dr + after_ref[tile] * dy_r
                    dw = dw + g * before
                    dkd = dkd + g * v_r
                    db = db - g * sa
                    dkk = dkk + before * dsa
                    ds_ref[tile] = g * w_t + dsa * kk_t
                    after_ref[tile] = before
                out = jnp.where(grp == G_KK, group_sum(dkk), 0.0)
                out = jnp.where(grp == G_R, group_sum(dr), out)
                out = jnp.where(grp == G_W[d], group_sum(dw), out)
                out = jnp.where(grp == G_B[d], group_sum(db), out)
                out = jnp.where(grp == G_KD[d], group_sum(dkd), out)
                g_ref[at] = out
                dv_ref[at] = dv_t
            return carry

        lax.fori_loop(0, SCAN_CHUNK, step, 0)

    def k_spec(fn):
        return pl.BlockSpec((SCAN_CHUNK, HEAD, LANES), lambda c: (fn(c), 0, 0))

    def v_spec(fn):
        return pl.BlockSpec((SCAN_CHUNK, V_HI, LANES), lambda c: (fn(c), 0, 0))

    st_shape = (2 * V_HI, HEAD, LANES)
    return pl.pallas_call(
        kern,
        out_shape=[jax.ShapeDtypeStruct((steps, HEAD, LANES), F32)] * 2
        + [jax.ShapeDtypeStruct((steps, V_HI, LANES), F32)] * 2,
        grid=(nc,),
        in_specs=[k_spec(back), k_spec(same), v_spec(back), v_spec(same), v_spec(back), v_spec(same),
                  pl.BlockSpec((SCAN_CHUNK,) + st_shape, lambda c: (back(c), 0, 0, 0)),
                  pl.BlockSpec(st_shape, lambda c: (0, 0, 0))],
        out_specs=[k_spec(back), k_spec(same), v_spec(back), v_spec(same)],
        scratch_shapes=[pltpu.VMEM(st_shape, F32), pltpu.VMEM(st_shape, F32)],
        compiler_params=_params(("arbitrary",)), name=name)(xall, xall, v_l, v_l, dy_l, dy_l, hist, fin)


def _to_key_lanes(wide, bsz, seq):
    z = wide.reshape(bsz, seq, N_GROUP, N_HEAD, HEAD).transpose(1, 4, 2, 0, 3)
    return z.reshape(seq, HEAD, LANES)


def _from_key_lanes(g, bsz, seq):
    z = g.reshape(seq, HEAD, N_GROUP, bsz, N_HEAD).transpose(3, 0, 2, 4, 1)
    return z.reshape(bsz * seq, N_GROUP * D_RWKV)


def _to_value_lanes(a, bsz, seq):
    z = a.reshape(bsz, seq, N_HEAD, V_HI, V_LO).transpose(1, 3, 4, 0, 2)
    return z.reshape(seq, V_HI, LANES)


def _from_value_lanes(y, bsz, seq):
    z = y.reshape(seq, V_HI, V_LO, bsz, N_HEAD).transpose(3, 0, 4, 1, 2)
    return z.reshape(bsz * seq, D_RWKV)


K_HI = HEAD // SUBLANES


def _lane_group_sum(x):
    x = x + pltpu.roll(x, 4 * N_CHAIN, 1)
    x = x + pltpu.roll(x, 2 * N_CHAIN, 1)
    return x + pltpu.roll(x, N_CHAIN, 1)


def _key_rows(x_t, d):
    out = []
    for grp in (G_KK, G_R, G_W[d], G_B[d], G_KD[d]):
        blk = x_t[SUBLANES * grp:SUBLANES * (grp + 1), :]
        out.append([jnp.broadcast_to(blk[kh:kh + 1, :], (SUBLANES, LANES)) for kh in range(K_HI)])
    return out


def _tree_sum(terms):
    terms = list(terms)
    while len(terms) > 1:
        terms = [a + b for a, b in zip(terms[::2], terms[1::2])]
    return terms[0]


def _kscan_specs(nc):
    same = lambda c: c
    mirror = lambda c: nc - 1 - c

    def k_spec(fn):
        return pl.BlockSpec((SCAN_CHUNK, HEAD, LANES), lambda c: (fn(c), 0, 0))

    def v_spec(fn):
        return pl.BlockSpec((SCAN_CHUNK, SUBLANES, LANES), lambda c: (fn(c), 0, 0))

    return same, mirror, k_spec, v_spec


ST_SHAPE = (2, K_HI, V_HI, SUBLANES, LANES)


def _lane_group_index():
    lane = lax.broadcasted_iota(jnp.int32, (SUBLANES, LANES), 1)
    return lax.shift_right_logical(lane, jnp.full_like(lane, 4))


def _spread_groups(x, grp):
    rolled = [x] + [pltpu.roll(x, s * N_CHAIN, 1) for s in range(1, N_GROUP)]
    out = []
    for j in range(N_GROUP):
        t = rolled[(0 - j) % N_GROUP]
        for g in range(1, N_GROUP):
            t = jnp.where(grp == g, rolled[(g - j) % N_GROUP], t)
        out.append(t)
    return out


def _gather_groups(tiles, grp):
    total = None
    for s in range(N_GROUP):
        b = tiles[s % N_GROUP]
        for g in range(1, N_GROUP):
            b = jnp.where(grp == g, tiles[(g + s) % N_GROUP], b)
        b = pltpu.roll(b, s * N_CHAIN, 1) if s else b
        total = b if total is None else total + b
    return total


def _lane_group_sum_short(x):
    return _tree_sum([x] + [pltpu.roll(x, k * N_CHAIN, 1) for k in range(1, N_GROUP)])


def _kscan_fwd(xall, v_c, *, name):
    steps = xall.shape[0]
    nc = steps // SCAN_CHUNK
    same, mirror, k_spec, v_spec = _kscan_specs(nc)

    def kern(xf_ref, xb_ref, vf_ref, vb_ref, yf_ref, yb_ref, hist_ref, fin_ref, st_ref):
        c = pl.program_id(0)

        @pl.when(c == 0)
        def _():
            st_ref[...] = jnp.zeros_like(st_ref)

        grp = _lane_group_index()

        def step(i, carry):
            j = SCAN_CHUNK - 1 - i
            for d, (x_t, v_t, y_ref, at) in enumerate(((xf_ref[i], vf_ref[i], yf_ref, i),
                                                       (xb_ref[j], vb_ref[j], yb_ref, j))):
                kk_r, r_r, w_r, b_r, kd_r = _key_rows(x_t, d)
                v_b = _spread_groups(v_t, grp)
                y_p = []
                for vh in range(V_HI):
                    st = [st_ref[d, kh, vh] for kh in range(K_HI)]
                    for kh in range(K_HI):
                        hist_ref[i, d, kh, vh] = st[kh]
                    sa = _lane_group_sum_short(_tree_sum(st[kh] * kk_r[kh] for kh in range(K_HI)))
                    new = [st[kh] * w_r[kh] - sa * b_r[kh] + v_b[vh] * kd_r[kh] for kh in range(K_HI)]
                    for kh in range(K_HI):
                        st_ref[d, kh, vh] = new[kh]
                    y_p.append(_tree_sum(new[kh] * r_r[kh] for kh in range(K_HI)))
                y_ref[at] = _gather_groups(y_p, grp)
            return carry

        lax.fori_loop(0, SCAN_CHUNK, step, 0)

        @pl.when(c == nc - 1)
        def _():
            fin_ref[...] = st_ref[...]

    return pl.pallas_call(
        kern,
        out_shape=[jax.ShapeDtypeStruct((steps, SUBLANES, LANES), F32)] * 2
        + [jax.ShapeDtypeStruct((steps,) + ST_SHAPE, F32), jax.ShapeDtypeStruct(ST_SHAPE, F32)],
        grid=(nc,), in_specs=[k_spec(same), k_spec(mirror), v_spec(same), v_spec(mirror)],
        out_specs=[v_spec(same), v_spec(mirror),
                   pl.BlockSpec((SCAN_CHUNK,) + ST_SHAPE, lambda c: (c, 0, 0, 0, 0, 0)),
                   pl.BlockSpec(ST_SHAPE, lambda c: (0, 0, 0, 0, 0))],
        scratch_shapes=[pltpu.VMEM(ST_SHAPE, F32)],
        compiler_params=_params(("arbitrary",)), name=name)(xall, xall, v_c, v_c)


def _kscan_bwd(xall, v_c, dy_c, hist, fin, *, name):
    steps = xall.shape[0]
    nc = steps // SCAN_CHUNK
    same, back, k_spec, v_spec = _kscan_specs(nc)

    def kern(xf_ref, xb_ref, vf_ref, vb_ref, dyf_ref, dyb_ref, hist_ref, fin_ref,
             gf_ref, gb_ref, dvf_ref, dvb_ref, ds_ref, after_ref):
        c = pl.program_id(0)

        @pl.when(c == 0)
        def _():
            ds_ref[...] = jnp.zeros_like(ds_ref)
            after_ref[...] = fin_ref[...]

        grp = _lane_group_index()
        row = lax.broadcasted_iota(jnp.int32, (SUBLANES, LANES), 0)

        def step(ii, carry):
            i = SCAN_CHUNK - 1 - ii
            j = ii
            for d, (x_t, v_t, dy_t, g_ref, dv_ref, at) in enumerate((
                    (xf_ref[i], vf_ref[i], dyf_ref[i], gf_ref, dvf_ref, i),
                    (xb_ref[j], vb_ref[j], dyb_ref[j], gb_ref, dvb_ref, j))):
                kk_r, r_r, w_r, b_r, kd_r = _key_rows(x_t, d)
                v_s, dy_s = _spread_groups(v_t, grp), _spread_groups(dy_t, grp)
                ks = range(K_HI)
                zero = jnp.zeros((SUBLANES, LANES), F32)
                dkk, dr, dw, db, dkd = ([zero] * K_HI for _ in range(5))
                dv_p = []
                for vh in range(V_HI):
                    v_b, dy_b = v_s[vh], dy_s[vh]
                    before = [hist_ref[i, d, kh, vh] for kh in ks]
                    g = [ds_ref[d, kh, vh] + dy_b * r_r[kh] for kh in ks]
                    dsa = -_lane_group_sum_short(_tree_sum(g[kh] * b_r[kh] for kh in ks))
                    sa = _lane_group_sum(_tree_sum(before[kh] * kk_r[kh] for kh in ks))
                    dv_p.append(_tree_sum(g[kh] * kd_r[kh] for kh in ks))
                    dr = [dr[kh] + after_ref[d, kh, vh] * dy_b for kh in ks]
                    dw = [dw[kh] + g[kh] * before[kh] for kh in ks]
                    dkd = [dkd[kh] + g[kh] * v_b for kh in ks]
                    db = [db[kh] - g[kh] * sa for kh in ks]
                    dkk = [dkk[kh] + before[kh] * dsa for kh in ks]
                    for kh in ks:
                        ds_ref[d, kh, vh] = g[kh] * w_r[kh] + dsa * kk_r[kh]
                        after_ref[d, kh, vh] = before[kh]
                dv_ref[at] = _gather_groups(dv_p, grp)
                blocks = {G_KK: dkk, G_R: dr, G_W[d]: dw, G_B[d]: db, G_KD[d]: dkd}
                for gi in range(N_GROUP):
                    blk = zero
                    if gi in blocks:
                        for kh in ks:
                            blk = jnp.where(row == kh, _colsum(blocks[gi][kh]), blk)
                    g_ref[at, SUBLANES * gi:SUBLANES * (gi + 1), :] = blk
            return carry

        lax.fori_loop(0, SCAN_CHUNK, step, 0)

    return pl.pallas_call(
        kern,
        out_shape=[jax.ShapeDtypeStruct((steps, HEAD, LANES), F32)] * 2
        + [jax.ShapeDtypeStruct((steps, SUBLANES, LANES), F32)] * 2,
        grid=(nc,),
        in_specs=[k_spec(back), k_spec(same), v_spec(back), v_spec(same), v_spec(back), v_spec(same),
                  pl.BlockSpec((SCAN_CHUNK,) + ST_SHAPE, lambda c: (back(c), 0, 0, 0, 0, 0)),
                  pl.BlockSpec(ST_SHAPE, lambda c: (0, 0, 0, 0, 0))],
        out_specs=[k_spec(back), k_spec(same), v_spec(back), v_spec(same)],
        scratch_shapes=[pltpu.VMEM(ST_SHAPE, F32), pltpu.VMEM(ST_SHAPE, F32)],
        compiler_params=_params(("arbitrary",)), name=name)(xall, xall, v_c, v_c, dy_c, dy_c, hist, fin)


def _key_row(x_t, grp, kh):
    r = SUBLANES * grp + kh
    return jnp.broadcast_to(x_t[r:r + 1, :], (SUBLANES, LANES))


def _acc(total, term):
    return term if total is None else total + term


SA_SHAPE = (2, V_HI, SUBLANES, LANES)


def _scan_fwd(xall, v_c, *, gather=(), name):
    steps = xall.shape[0]
    nc = steps // SCAN_CHUNK
    same, mirror, k_spec, v_spec = _kscan_specs(nc)
    last = SCAN_CHUNK - 1
    n_x = len(gather)

    def kern(*refs):
        xf_ref, xb_ref, vf_ref, vb_ref = refs[:4]
        yf_ref, yb_ref, hist_ref, fin_ref, sa_ref = refs[4 + n_x:9 + n_x]
        st_ref = refs[9 + 2 * n_x]
        c = pl.program_id(0)

        def riders():
            return _exchange_copies(refs[4:4 + n_x], refs[9 + n_x:9 + 2 * n_x], 0, *refs[10 + 2 * n_x:])

        @pl.when(c == 0)
        def _():
            st_ref[...] = jnp.zeros_like(st_ref)
            if n_x:
                for cp in riders():
                    cp.start()

        hist_ref[0] = st_ref[...]
        grp = _lane_group_index()

        def body(i, put):
            j = last - i
            for d, (x_t, v_t, y_ref, at) in enumerate(((xf_ref[i], vf_ref[i], yf_ref, i),
                                                       (xb_ref[j], vb_ref[j], yb_ref, j))):
                v_b = _spread_groups(v_t, grp)
                part = [None] * V_HI
                for kh in range(K_HI):
                    kk_r = _key_row(x_t, G_KK, kh)
                    for vh in range(V_HI):
                        part[vh] = _acc(part[vh], hist_ref[i, d, kh, vh] * kk_r)
                sa = [_lane_group_sum_short(p) for p in part]
                for vh in range(V_HI):
                    sa_ref[i, d, vh] = sa[vh]
                y_p = [None] * V_HI
                for kh in range(K_HI):
                    r_r, w_r = _key_row(x_t, G_R, kh), _key_row(x_t, G_W[d], kh)
                    b_r, kd_r = _key_row(x_t, G_B[d], kh), _key_row(x_t, G_KD[d], kh)
                    for vh in range(V_HI):
                        new = hist_ref[i, d, kh, vh] * w_r - sa[vh] * b_r + v_b[vh] * kd_r
                        put(d, kh, vh, new)
                        y_p[vh] = _acc(y_p[vh], new * r_r)
                y_ref[at] = _gather_groups(y_p, grp)

        def step(i, carry):
            def put(d, kh, vh, val):
                hist_ref[i + 1, d, kh, vh] = val
            body(i, put)
            return carry

        lax.fori_loop(0, last, step, 0)

        def put_carry(d, kh, vh, val):
            st_ref[d, kh, vh] = val

        body(last, put_carry)

        @pl.when(c == nc - 1)
        def _():
            fin_ref[...] = st_ref[...]
            if n_x:
                for cp in riders():
                    cp.wait()

    return pl.pallas_call(
        kern,
        out_shape=[jax.ShapeDtypeStruct((steps, SUBLANES, LANES), F32)] * 2
        + [jax.ShapeDtypeStruct((steps,) + ST_SHAPE, F32), jax.ShapeDtypeStruct(ST_SHAPE, F32),
           jax.ShapeDtypeStruct((steps,) + SA_SHAPE, F32)]
        + _exchange_out_shapes(gather, 0),
        grid=(nc,), in_specs=[k_spec(same), k_spec(mirror), v_spec(same), v_spec(mirror)] + _hbm_specs(n_x),
        out_specs=[v_spec(same), v_spec(mirror),
                   pl.BlockSpec((SCAN_CHUNK,) + ST_SHAPE, lambda c: (c, 0, 0, 0, 0, 0)),
                   pl.BlockSpec(ST_SHAPE, lambda c: (0, 0, 0, 0, 0)),
                   pl.BlockSpec((SCAN_CHUNK,) + SA_SHAPE, lambda c: (c, 0, 0, 0, 0))] + _hbm_specs(n_x),
        scratch_shapes=[pltpu.VMEM(ST_SHAPE, F32)] + (_exchange_sems(n_x) if n_x else []),
        compiler_params=_params(("arbitrary",)), name=name)(xall, xall, v_c, v_c, *gather)


def _scan_bwd(xall, v_c, dy_c, hist, fin, sa, *, exchange=(), name):
    steps = xall.shape[0]
    nc = steps // SCAN_CHUNK
    same, back, k_spec, v_spec = _kscan_specs(nc)
    last = SCAN_CHUNK - 1
    n_x = len(exchange)

    def kern(*refs):
        xf_ref, xb_ref, vf_ref, vb_ref, dyf_ref, dyb_ref, hist_ref, fin_ref, sa_ref = refs[:9]
        gf_ref, gb_ref, dvf_ref, dvb_ref = refs[9 + n_x:13 + n_x]
        ds_ref, after_ref = refs[13 + 2 * n_x:15 + 2 * n_x]
        c = pl.program_id(0)

        def riders():
            return _exchange_copies(refs[9:9 + n_x], refs[13 + n_x:13 + 2 * n_x], n_x, *refs[15 + 2 * n_x:])

        @pl.when(c == 0)
        def _():
            ds_ref[...] = jnp.zeros_like(ds_ref)
            after_ref[...] = fin_ref[...]
            if n_x:
                for cp in riders():
                    cp.start()

        grp = _lane_group_index()
        row = lax.broadcasted_iota(jnp.int32, (SUBLANES, LANES), 0)
        zero = jnp.zeros((SUBLANES, LANES), F32)

        def body(i, after):
            j = last - i
            for d, (x_t, v_t, dy_t, g_ref, dv_ref, at) in enumerate((
                    (xf_ref[i], vf_ref[i], dyf_ref[i], gf_ref, dvf_ref, i),
                    (xb_ref[j], vb_ref[j], dyb_ref[j], gb_ref, dvb_ref, j))):
                v_s, dy_s = _spread_groups(v_t, grp), _spread_groups(dy_t, grp)
                dsa_p, dv_p = [None] * V_HI, [None] * V_HI
                for kh in range(K_HI):
                    r_r = _key_row(x_t, G_R, kh)
                    b_r, kd_r = _key_row(x_t, G_B[d], kh), _key_row(x_t, G_KD[d], kh)
                    for vh in range(V_HI):
                        g = ds_ref[d, kh, vh] + dy_s[vh] * r_r
                        ds_ref[d, kh, vh] = g
                        dsa_p[vh] = _acc(dsa_p[vh], g * b_r)
                        dv_p[vh] = _acc(dv_p[vh], g * kd_r)
                dsa = [-_lane_group_sum_short(p) for p in dsa_p]
                sa = [sa_ref[i, d, vh] for vh in range(V_HI)]
                dv_ref[at] = _gather_groups(dv_p, grp)
                blocks = {G_KK: zero, G_R: zero, G_W[d]: zero, G_B[d]: zero, G_KD[d]: zero}
                for kh in range(K_HI):
                    w_r, kk_r = _key_row(x_t, G_W[d], kh), _key_row(x_t, G_KK, kh)
                    dkk = dr = dw = db = dkd = None
                    for vh in range(V_HI):
                        g, before = ds_ref[d, kh, vh], hist_ref[i, d, kh, vh]
                        dr = _acc(dr, after(d, kh, vh) * dy_s[vh])
                        dw = _acc(dw, g * before)
                        dkd = _acc(dkd, g * v_s[vh])
                        db = _acc(db, g * sa[vh])
                        dkk = _acc(dkk, before * dsa[vh])
                        ds_ref[d, kh, vh] = g * w_r + dsa[vh] * kk_r
                    for gi, a in ((G_KK, dkk), (G_R, dr), (G_W[d], dw), (G_B[d], -db), (G_KD[d], dkd)):
                        blocks[gi] = jnp.where(row == kh, _colsum(a), blocks[gi])
                for gi in range(N_GROUP):
                    g_ref[at, SUBLANES * gi:SUBLANES * (gi + 1), :] = blocks.get(gi, zero)

        body(last, lambda d, kh, vh: after_ref[d, kh, vh])

        def step(ii, carry):
            i = last - ii
            body(i, lambda d, kh, vh: hist_ref[i + 1, d, kh, vh])
            return carry

        lax.fori_loop(1, SCAN_CHUNK, step, 0)
        after_ref[...] = hist_ref[0]

        if n_x:
            @pl.when(c == nc - 1)
            def _():
                for cp in riders():
                    cp.wait()

    return pl.pallas_call(
        kern,
        out_shape=[jax.ShapeDtypeStruct((steps, HEAD, LANES), F32)] * 2
        + [jax.ShapeDtypeStruct((steps, SUBLANES, LANES), F32)] * 2 + _exchange_out_shapes(exchange, n_x),
        grid=(nc,),
        in_specs=[k_spec(back), k_spec(same), v_spec(back), v_spec(same), v_spec(back), v_spec(same),
                  pl.BlockSpec((SCAN_CHUNK,) + ST_SHAPE, lambda c: (back(c), 0, 0, 0, 0, 0)),
                  pl.BlockSpec(ST_SHAPE, lambda c: (0, 0, 0, 0, 0)),
                  pl.BlockSpec((SCAN_CHUNK,) + SA_SHAPE, lambda c: (back(c), 0, 0, 0, 0))] + _hbm_specs(n_x),
        out_specs=[k_spec(back), k_spec(same), v_spec(back), v_spec(same)] + _hbm_specs(n_x),
        scratch_shapes=[pltpu.VMEM(ST_SHAPE, F32), pltpu.VMEM(ST_SHAPE, F32)]
        + (_exchange_sems(n_x) if n_x else []),
        compiler_params=_params(("arbitrary",)), name=name)(xall, xall, v_c, v_c, dy_c, dy_c, hist, fin, sa,
                                                            *exchange)


def _to_key_rows(wide, bsz, seq):
    z = wide.reshape(bsz, seq, N_GROUP, N_HEAD, K_HI, SUBLANES).transpose(1, 2, 4, 5, 0, 3)
    return z.reshape(seq, HEAD, LANES)


def _from_key_rows(g, bsz, seq):
    z = g.reshape(seq, N_GROUP, K_HI, SUBLANES, bsz, N_HEAD).transpose(4, 0, 1, 5, 2, 3)
    return z.reshape(bsz * seq, N_GROUP * D_RWKV)


def _to_value_rows(a, bsz, seq):
    z = a.reshape(bsz, seq, N_HEAD, V_HI, SUBLANES).transpose(1, 4, 3, 0, 2)
    return z.reshape(seq, SUBLANES, LANES)


def _from_value_rows(y, bsz, seq):
    z = y.reshape(seq, SUBLANES, V_HI, bsz, N_HEAD).transpose(3, 0, 4, 2, 1)
    return z.reshape(bsz * seq, D_RWKV)


def _pad_cols(a, segs):
    out, off = [], 0
    for w, wp in segs:
        out.append(a[..., off:off + w])
        if wp > w:
            out.append(jnp.zeros(a.shape[:-1] + (wp - w,), a.dtype))
        off += w
    return jnp.concatenate(out, axis=-1)


def _unpad_cols(a, segs):
    out, off = [], 0
    for w, wp in segs:
        out.append(a[..., off:off + w])
        off += wp
    return jnp.concatenate(out, axis=-1)


P_SEGS = ((3 * D_RWKV, 3 * D_RWKV), (D_LORA, 128), (D_LORA, 128), (D_GATE, 256), (3 * D_CONV, 3 * D_CONV))
S_SEGS = P_SEGS[:4]


def _pad_rows(a, rows):
    return jnp.concatenate([a, jnp.zeros((rows - a.shape[0], a.shape[1]), a.dtype)], axis=0)


LATE = ("w_out", "w_gate", "w_up", "w_down")


def _local_step(x, target, w, late=None):
    bsz, seq, _ = x.shape
    t = bsz * seq
    x2d = x.reshape(t, D_MODEL)
    tg2d = target.reshape(t, D_MODEL)
    row = lambda a: a.reshape(1, -1).astype(F32)

    w_in = _pad_cols(w["w_in"][0], P_SEGS)
    mu = _pad_cols(row(w["mu_shift"]), S_SEGS)
    wupf, wupb, aupf, aupb = (_pad_rows(w[n][0].astype(F32), 128) for n in ("w_up_f", "w_up_b", "a_up_f", "a_up_b"))
    gup = _pad_rows(w["g_up"][0].astype(F32), 256)
    conv_w = _pad_rows(w["conv_w"][0].astype(F32), SUBLANES)
    norm1, norm2, normf = row(w["norm1_w"]), row(w["norm2_w"]), row(w["norm_f_w"])
    vec = {n: row(w[n]) for n in VEC}
    head_of = jnp.arange(LANES) // HEAD
    bd = (head_of[:, None] == head_of[None, :]).astype(F32)
    pre_consts = [vec["k_k"], vec["w0_f"], vec["w0_b"], vec["a0_f"], vec["a0_b"], vec["k_a_f"], vec["k_a_b"],
                  wupf, wupb, aupf, aupb, gup, bd]
    post_consts = [vec["gn_w"], vec["gn_b"], vec["r_k_f"], vec["r_k_b"], bd]

    h1, = _rowwise(_rms, [x2d], [norm1], [D_MODEL], [], name="rms1_fwd", out_dtype=BF16)
    p = _mm(h1, w_in, name="mm_in")
    pss, oconv = _shift_conv_fwd(p, mu, conv_w, seq, name="shift_conv_fwd")
    pre_rows = [(pss, 0, 512), (pss, 1, 512), (pss, XW0 // 128, 128), (pss, XA0 // 128, 128), (pss, XG0 // 256, 256)]
    sc, g = _rowwise(_prescan_math, pre_rows, pre_consts, [[D_RWKV] * N_GROUP, D_RWKV], [], name="prescan_fwd")
    xall = _to_key_rows(sc, bsz, seq)
    v_l = _to_value_rows(pss[:, 2 * D_RWKV:3 * D_RWKV], bsz, seq)
    y_f, y_b, hist, fin, sa, *gathered = _scan_fwd(xall, v_l, gather=[late[n] for n in LATE] if late else (),
                                                   name="scan_fwd")
    w_out, w_gate, w_up, w_down = (
        (_from_slots(a, SHARD_AXIS[n]) if late else w[n])[0] for n, a in zip(LATE, gathered or LATE))
    y = _from_value_rows(y_f + y_b, bsz, seq)
    post_rows = [y, (pss, 0, 512), (pss, 2, 512), (sc, G_KD[0], 512), (sc, G_KD[1], 512), g]

    def post_fwd(y_, r_, v_, kdf_, kdb_, g_, oc_, *consts):
        return _postscan_math(y_, r_, v_, kdf_, kdb_, g_, *consts), oc_

    o, = _rowwise(post_fwd, post_rows + [oconv], post_consts, [[D_RWKV, D_CONV]], [], name="postscan_fwd",
                  out_dtype=BF16)
    x1 = _mm(o, w_out, add=x2d, name="mm_out")
    h2, = _rowwise(_rms, [x1], [norm2], [D_MODEL], [], name="rms2_fwd", out_dtype=BF16)
    gg = _mm(h2, w_gate, name="mm_gate")
    uu = _mm(h2, w_up, name="mm_up")
    ff, = _rowwise(lambda a, c: jax.nn.silu(a) * c, [gg, uu], [], [D_FF], [], name="swiglu_fwd", out_dtype=BF16)
    x2 = _mm(ff, w_down, add=x1, name="mm_down")

    def final(x_, tg_, wn_):
        yo, vjp = jax.vjp(_rms, x_, wn_)
        err = yo - tg_
        dx_, dwn_ = vjp(err * (1.0 / D_MODEL))
        part = jnp.sum(jnp.sum(err * err, axis=1, keepdims=True), axis=0, keepdims=True) * (0.5 / D_MODEL)
        return dx_, part + jnp.zeros((1, LANES), F32), dwn_

    dx2, loss_acc, d_normf = _rowwise(final, [x2, tg2d], [normf], [D_MODEL], [(1, LANES), (1, D_MODEL)],
                                      name="loss_head")
    dff = _mm(dx2, w_down, tb=True, name="mm_down_dx")
    g_w_down = _mm(ff, dx2, ta=True, name="mm_down_dw")

    def swiglu_bwd(a, c, d):
        _, vjp = jax.vjp(lambda a_, c_: jax.nn.silu(a_) * c_, a, c)
        return vjp(d)

    dgg, duu = _rowwise(swiglu_bwd, [gg, uu, dff], [], [D_FF, D_FF], [], name="swiglu_bwd", out_dtype=BF16)
    dh2 = _mm(dgg, w_gate, tb=True, name="mm_gate_dx")
    dh2 = _mm(duu, w_up, tb=True, add=dh2, name="mm_up_dx")
    g_w_gate = _mm(h2, dgg, ta=True, name="mm_gate_dw")
    g_w_up = _mm(h2, duu, ta=True, name="mm_up_dw")

    def rms_bwd(x_, dh_, dres_, wn_):
        _, vjp = jax.vjp(_rms, x_, wn_)
        dx_, dwn_ = vjp(dh_)
        return dx_ + dres_, dwn_

    dx1, d_norm2 = _rowwise(rms_bwd, [x1, dh2, dx2], [norm2], [D_MODEL], [(1, D_MODEL)], name="rms2_bwd")
    do = _mm(dx1, w_out, tb=True, name="mm_out_dx")
    g_w_out = _mm(o, dx1, ta=True, name="mm_out_dw")

    def post_bwd(y_, r_, v_, kdf_, kdb_, g_, do_, *consts):
        _, vjp = jax.vjp(lambda *a: _postscan_math(*a, consts[4]), y_, r_, v_, kdf_, kdb_, g_, *consts[:4])
        return vjp(do_)

    (dy, dr_c, dv_c, dkdf_c, dkdb_c, dg, d_gn_w, d_gn_b, d_rkf, d_rkb) = _rowwise(
        post_bwd, post_rows + [(do, 0, 512)], post_consts, [D_RWKV] * 6, [(1, D_RWKV)] * 4, name="postscan_bwd")
    dy_l = _to_value_rows(dy, bsz, seq)
    late_grads = {"w_out": g_w_out[None], "w_gate": g_w_gate[None], "w_up": g_w_up[None], "w_down": g_w_down[None]}
    g_f, g_b, dv_f, dv_b, *late_parts = _scan_bwd(
        xall, v_l, dy_l, hist, fin, sa, name="scan_bwd",
        exchange=[_to_slots(late_grads[n], SHARD_AXIS[n]).astype(BF16) for n in LATE] if late else ())
    dsc = _from_key_rows(g_f + g_b, bsz, seq)
    dv_s = _from_value_rows(dv_f + dv_b, bsz, seq)

    def pre_bwd(r_, k_, xw_, xa_, xg_, dkk_, dr_s, dwf_, dwb_, dbf_, dbb_, dkdf_s, dkdb_s,
                dr_c_, dv_c_, dv_s_, dkdf_c_, dkdb_c_, dg_, *consts):
        _, vjp = jax.vjp(lambda *a: _prescan_math(*a, consts[-1]), r_, k_, xw_, xa_, xg_, *consts[:-1])
        grads = vjp((dkk_, dr_s + dr_c_, dwf_, dwb_, dbf_, dbb_, dkdf_s + dkdf_c_, dkdb_s + dkdb_c_, dg_))
        dr_, dk_, dxw_, dxa_, dxg_ = grads[:5]
        return (dr_, dk_, dv_c_ + dv_s_, dxw_, dxa_, dxg_) + tuple(grads[5:])

    pre_b_rows = (pre_rows + [(dsc, j, 512) for j in range(N_GROUP)]
                  + [dr_c, dv_c, dv_s, dkdf_c, dkdb_c, dg])
    pre_b = _rowwise(pre_bwd, pre_b_rows, pre_consts, [[512, 512, 512, 128, 128, 256]],
                     [(1, D_RWKV)] * 7 + [(128, D_RWKV)] * 4 + [(256, D_RWKV)], name="prescan_bwd")
    d_pss = pre_b[0]
    d_kk_, d_w0f, d_w0b, d_a0f, d_a0b, d_kaf, d_kab, d_wupf, d_wupb, d_aupf, d_aupb, d_gup = pre_b[1:]
    dp, d_mu, d_conv = _shift_conv_bwd(p, d_pss, do, mu, conv_w, seq, name="shift_conv_bwd")
    dh1 = _mm(dp, w_in, tb=True, name="mm_in_dx")
    g_w_in = _mm(h1, dp, ta=True, name="mm_in_dw")
    dx, d_norm1 = _rowwise(rms_bwd, [x2d, dh1, dx1], [norm1], [D_MODEL], [(1, D_MODEL)], name="rms1_bwd")

    grads = {
        "norm1_w": d_norm1, "w_in": _unpad_cols(g_w_in, P_SEGS)[None], "mu_shift": _unpad_cols(d_mu, S_SEGS),
        "w_up_f": d_wupf[None, :D_LORA], "w0_f": d_w0f, "w_up_b": d_wupb[None, :D_LORA], "w0_b": d_w0b,
        "a_up_f": d_aupf[None, :D_LORA], "a0_f": d_a0f, "a_up_b": d_aupb[None, :D_LORA], "a0_b": d_a0b,
        "g_up": d_gup[None, :D_GATE], "k_k": d_kk_, "k_a_f": d_kaf, "k_a_b": d_kab,
        "r_k_f": d_rkf, "r_k_b": d_rkb, "gn_w": d_gn_w, "gn_b": d_gn_b, "conv_w": d_conv[None, :3],
        "w_out": g_w_out[None], "norm2_w": d_norm2, "w_gate": g_w_gate[None], "w_up": g_w_up[None],
        "w_down": g_w_down[None], "norm_f_w": d_normf,
    }
    return loss_acc[0, 0], dx.reshape(bsz, seq, D_MODEL), grads, dict(zip(LATE, late_parts))


def _hbm_specs(n):
    return [pl.BlockSpec(memory_space=pl.ANY)] * n


def _all_gather(arrs, *, name):
    n = len(arrs)

    def body(*refs):
        x_refs, out_refs = refs[:n], refs[n:2 * n]
        send_sems, recv_sems, local_sems = refs[2 * n:]
        x, y, c = lax.axis_index("x"), lax.axis_index("y"), lax.axis_index("c")
        me, sibling = (x, y, c), (x, y, 1 - c)
        chips = [(1 - x, y), (x, 1 - y), (1 - x, 1 - y)]

        def slot(a, px, py, pc):
            return out_refs[a].at[4 * px + 2 * py + pc]

        def copy(a, k, block, to, src=None):
            return pltpu.make_async_remote_copy(
                src_ref=slot(a, *block) if src is None else src, dst_ref=slot(a, *block),
                send_sem=send_sems.at[k, a], recv_sem=recv_sems.at[k, a],
                device_id=to, device_id_type=pl.DeviceIdType.MESH)

        mine = [pltpu.make_async_copy(x_refs[a], slot(a, *me), local_sems.at[a]) for a in range(n)]
        for cp in mine:
            cp.start()
        first = []
        for a in range(n):
            first.append(copy(a, 0, me, sibling, src=x_refs[a]))
            first += [copy(a, 1 + j, me, (*chip, c), src=x_refs[a]) for j, chip in enumerate(chips)]
        for cp in first:
            cp.start()
        passed = []
        for j, chip in enumerate(chips):
            for a in range(n):
                copy(a, 1 + j, (*chip, c), me).wait_recv()
                cp = copy(a, 4 + j, (*chip, c), sibling)
                cp.start()
                passed.append(cp)
        for a in range(n):
            copy(a, 0, sibling, me).wait_recv()
            for j, chip in enumerate(chips):
                copy(a, 4 + j, (*chip, 1 - c), me).wait_recv()
        for cp in first + passed:
            cp.wait_send()
        for cp in mine:
            cp.wait()

    return pl.pallas_call(
        body, out_shape=[jax.ShapeDtypeStruct((N_DEV,) + a.shape, a.dtype) for a in arrs],
        in_specs=_hbm_specs(n), out_specs=_hbm_specs(n),
        scratch_shapes=[pltpu.SemaphoreType.DMA((7, n)), pltpu.SemaphoreType.DMA((7, n)),
                        pltpu.SemaphoreType.DMA((n,))],
        name=name)(*arrs)


def _exchange(sliced, whole, *, name):
    arrs = list(sliced) + list(whole)
    n, n_sliced = len(arrs), len(sliced)

    def body(*refs):
        copies = _exchange_copies(refs[:n], refs[n:2 * n], n_sliced, *refs[2 * n:])
        for cp in copies:
            cp.start()
        for cp in copies:
            cp.wait()

    return pl.pallas_call(
        body, out_shape=_exchange_out_shapes(arrs, n_sliced), in_specs=_hbm_specs(n), out_specs=_hbm_specs(n),
        scratch_shapes=_exchange_sems(n), name=name)(*arrs)


def _exchange_out_shapes(arrs, n_sliced):
    return [jax.ShapeDtypeStruct(a.shape if i < n_sliced else (N_DEV,) + a.shape, a.dtype)
            for i, a in enumerate(arrs)]


def _exchange_sems(n):
    return [pltpu.SemaphoreType.DMA((7, n)), pltpu.SemaphoreType.DMA((7, n)), pltpu.SemaphoreType.DMA((n,))]


def _exchange_copies(in_refs, out_refs, n_sliced, send_sems, recv_sems, local_sems):
    n = len(in_refs)
    x, y, c = lax.axis_index("x"), lax.axis_index("y"), lax.axis_index("c")
    me = 4 * x + 2 * y + c

    def src(a, dev):
        return in_refs[a].at[dev] if a < n_sliced else in_refs[a]

    copies = [pltpu.make_async_copy(src(a, me), out_refs[a].at[me], local_sems.at[a]) for a in range(n)]
    for k in range(1, N_DEV):
        px = 1 - x if k & 4 else x
        py = 1 - y if k & 2 else y
        pc = 1 - c if k & 1 else c
        for a in range(n):
            copies.append(pltpu.make_async_remote_copy(
                src_ref=src(a, 4 * px + 2 * py + pc), dst_ref=out_refs[a].at[me],
                send_sem=send_sems.at[k - 1, a], recv_sem=recv_sems.at[k - 1, a],
                device_id=(px, py, pc), device_id_type=pl.DeviceIdType.MESH))
    return copies


def _adam_math(g, w, m, v):
    nm = ADAM_B1 * m + (1.0 - ADAM_B1) * g
    nv = ADAM_B2 * v + (1.0 - ADAM_B2) * (g * g)
    m_hat = nm / (1.0 - ADAM_B1 ** ADAM_STEP)
    v_hat = nv / (1.0 - ADAM_B2 ** ADAM_STEP)
    return -ADAM_LR * (m_hat / (jnp.sqrt(v_hat) + ADAM_EPS) + ADAM_WD * w), nm, nv


def _slot_sum(ref):
    g = ref[0].astype(F32)
    for s in range(1, N_DEV):
        g = g + ref[s].astype(F32)
    return g


def _adamw_big(parts, w, m, v, *, name):
    _, rws, cols = w.shape
    tr = _tile(rws, (256, 176, 128))

    def kern(p_ref, w_ref, m_ref, v_ref, g_ref, d_ref, nm_ref, nv_ref):
        g = _slot_sum(p_ref)
        g_ref[...] = g
        d_ref[...], nm_ref[...], nv_ref[...] = _adam_math(g, w_ref[...], m_ref[...], v_ref[...])

    spec = pl.BlockSpec((1, tr, cols), lambda i: (0, i, 0))
    return pl.pallas_call(
        kern, out_shape=[jax.ShapeDtypeStruct(w.shape, F32)] * 4, grid=(rws // tr,),
        in_specs=[pl.BlockSpec((N_DEV, 1, tr, cols), lambda i: (0, 0, i, 0)), spec, spec, spec],
        out_specs=[spec] * 4, compiler_params=_params(("parallel",)), name=name)(parts, w, m, v)


def _adamw_small(lora_parts, vec_parts, wide_parts, wmv, *, name):
    names = LORA + VEC + WIDE
    n_l, n = len(LORA), len(names)
    flat = [a for trip in wmv for a in trip]

    def kern(*refs):
        l_refs, vec_ref, wide_ref = refs[:n_l], refs[n_l], refs[n_l + 1]
        in_refs = refs[n_l + 2:n_l + 2 + 3 * n]
        out_refs = refs[n_l + 2 + 3 * n:]
        vec_sum, wide_sum = _slot_sum(vec_ref), _slot_sum(wide_ref)
        for i, nm in enumerate(names):
            w_ref, m_ref, v_ref = in_refs[3 * i:3 * i + 3]
            if i < n_l:
                g = _slot_sum(l_refs[i])
            elif nm in VEC:
                g = vec_sum[i - n_l:i - n_l + 1, :]
            else:
                g = wide_sum[WIDE.index(nm):WIDE.index(nm) + 1, :w_ref.shape[-1]]
            o = out_refs[4 * i:4 * i + 4]
            o[0][...] = g
            o[1][...], o[2][...], o[3][...] = _adam_math(g, w_ref[...], m_ref[...], v_ref[...])

    out_shape = [jax.ShapeDtypeStruct(trip[0].shape, F32) for trip in wmv for _ in range(4)]
    outs = pl.pallas_call(kern, out_shape=out_shape, name=name,
                          compiler_params=pltpu.CompilerParams(vmem_limit_bytes=VMEM_LIMIT))(
        *lora_parts, vec_parts, wide_parts, *flat)
    return [tuple(outs[4 * i:4 * i + 4]) for i in range(n)]


def _to_slots(g, axis):
    _, rws, cols = g.shape
    if axis == 1:
        return g.reshape(N_DEV, 1, rws // N_DEV, cols)
    return g.reshape(1, rws, N_DEV, cols // N_DEV).transpose(2, 0, 1, 3)


def _from_slots(got, axis):
    _, _, rws, cols = got.shape
    if axis == 1:
        return got.reshape(1, N_DEV * rws, cols)
    return got.transpose(1, 2, 0, 3).reshape(1, rws, N_DEV * cols)


def _pad_lanes(a, width):
    return jnp.concatenate([a, jnp.zeros(a.shape[:-1] + (width - a.shape[-1],), a.dtype)], axis=-1)


def kernel(x, norm1_w, w_in, mu_shift, w_up_f, w0_f, w_up_b, w0_b, a_up_f, a0_f, a_up_b, a0_b, g_up, k_k, k_a_f, k_a_b, r_k_f, r_k_b, gn_w, gn_b, conv_w, w_out, norm2_w, w_gate, w_up, w_down, norm_f_w, loss_target, m_norm1_w, m_w_in, m_mu_shift, m_w_up_f, m_w0_f, m_w_up_b, m_w0_b, m_a_up_f, m_a0_f, m_a_up_b, m_a0_b, m_g_up, m_k_k, m_k_a_f, m_k_a_b, m_r_k_f, m_r_k_b, m_gn_w, m_gn_b, m_conv_w, m_w_out, m_norm2_w, m_w_gate, m_w_up, m_w_down, m_norm_f_w, v_norm1_w, v_w_in, v_mu_shift, v_w_up_f, v_w0_f, v_w_up_b, v_w0_b, v_a_up_f, v_a0_f, v_a_up_b, v_a0_b, v_g_up, v_k_k, v_k_a_f, v_k_a_b, v_r_k_f, v_r_k_b, v_gn_w, v_gn_b, v_conv_w, v_w_out, v_norm2_w, v_w_gate, v_w_up, v_w_down, v_norm_f_w):
    local = dict(norm1_w=norm1_w, w_in=w_in, mu_shift=mu_shift, w_up_f=w_up_f, w0_f=w0_f, w_up_b=w_up_b,
                 w0_b=w0_b, a_up_f=a_up_f, a0_f=a0_f, a_up_b=a_up_b, a0_b=a0_b, g_up=g_up, k_k=k_k, k_a_f=k_a_f,
                 k_a_b=k_a_b, r_k_f=r_k_f, r_k_b=r_k_b, gn_w=gn_w, gn_b=gn_b, conv_w=conv_w, w_out=w_out,
                 norm2_w=norm2_w, w_gate=w_gate, w_up=w_up, w_down=w_down, norm_f_w=norm_f_w)
    mom_m = dict(norm1_w=m_norm1_w, w_in=m_w_in, mu_shift=m_mu_shift, w_up_f=m_w_up_f, w0_f=m_w0_f,
                 w_up_b=m_w_up_b, w0_b=m_w0_b, a_up_f=m_a_up_f, a0_f=m_a0_f, a_up_b=m_a_up_b, a0_b=m_a0_b,
                 g_up=m_g_up, k_k=m_k_k, k_a_f=m_k_a_f, k_a_b=m_k_a_b, r_k_f=m_r_k_f, r_k_b=m_r_k_b,
                 gn_w=m_gn_w, gn_b=m_gn_b, conv_w=m_conv_w, w_out=m_w_out, norm2_w=m_norm2_w, w_gate=m_w_gate,
                 w_up=m_w_up, w_down=m_w_down, norm_f_w=m_norm_f_w)
    mom_v = dict(norm1_w=v_norm1_w, w_in=v_w_in, mu_shift=v_mu_shift, w_up_f=v_w_up_f, w0_f=v_w0_f,
                 w_up_b=v_w_up_b, w0_b=v_w0_b, a_up_f=v_a_up_f, a0_f=v_a0_f, a_up_b=v_a_up_b, a0_b=v_a0_b,
                 g_up=v_g_up, k_k=v_k_k, k_a_f=v_k_a_f, k_a_b=v_k_a_b, r_k_f=v_r_k_f, r_k_b=v_r_k_b,
                 gn_w=v_gn_w, gn_b=v_gn_b, conv_w=v_conv_w, w_out=v_w_out, norm2_w=v_norm2_w, w_gate=v_w_gate,
                 w_up=v_w_up, w_down=v_w_down, norm_f_w=v_norm_f_w)

    early = ("w_in",) + LORA
    got = _all_gather([local["w_in"].astype(BF16)] + [local[n] for n in LORA], name="gather")
    full = dict(local)
    full.update({n: _from_slots(a, SHARD_AXIS[n]) for n, a in zip(early, got)})

    loss_part, grad_x, grads, late_parts = _local_step(x, loss_target, full,
                                                       late={n: local[n].astype(BF16) for n in LATE})
    loss = lax.psum(loss_part, ("x", "y", "c"))

    vec_rows = jnp.concatenate([grads[n] for n in VEC] + [jnp.zeros((16 - len(VEC), D_RWKV), F32)], axis=0)
    wide_rows = jnp.concatenate([_pad_lanes(grads[n], WIDE_ROW) for n in WIDE]
                                + [jnp.zeros((SUBLANES - len(WIDE), WIDE_ROW), F32)], axis=0)
    slots = [_to_slots(grads[n], SHARD_AXIS[n]).astype(BF16 if n in BIG else F32) for n in early]
    recv = _exchange(slots, [vec_rows, wide_rows], name="grad_exchange")
    out = {}
    for n in BIG:
        parts = recv[0] if n == "w_in" else late_parts[n]
        out[n] = _adamw_big(parts, local[n], mom_m[n], mom_v[n], name="adamw_" + n)

    def small_form(n, a):
        if n in LORA:
            return a
        a = a.reshape(1, -1)
        return _pad_lanes(a, WIDE_ROW) if n == "mu_shift" else a

    small = LORA + VEC + WIDE
    res = _adamw_small(recv[1:len(early)], recv[len(early)], recv[len(early) + 1],
                       [tuple(small_form(n, d[n]) for d in (local, mom_m, mom_v)) for n in small],
                       name="adamw_small")
    for n, quad in zip(small, res):
        out[n] = tuple(a[..., :local[n].size].reshape(local[n].shape) if n not in LORA else a for a in quad)
    return (loss, grad_x, *[out[n][i] for i in range(4) for n in WEIGHTS])
```

```python
import functools

import jax
import jax.numpy as jnp
from jax import lax
from jax.experimental import pallas as pl
from jax.experimental.pallas import tpu as pltpu

F32 = jnp.float32
BF16 = jnp.bfloat16
HIGHEST = lax.Precision.HIGHEST

N_DEV = 8
D_MODEL = 1024
D_RWKV = 512
D_CONV = 512
HEAD = 64
N_HEAD = D_RWKV // HEAD
D_LORA = 64
D_GATE = 160
D_FF = 2816
D_SHIFTED = 3 * D_RWKV + 2 * D_LORA + D_GATE
D_IN = D_SHIFTED + 3 * D_CONV
XW0, XA0, XG0 = 1536, 1664, 1792
D_SP = 2048
D_INP = D_SP + 3 * D_CONV
LOG_DECAY_SCALE = 0.606531
RMS_EPS = 1e-6
GN_EPS = 64e-5
NORM_EPS = 1e-12
ADAM_LR, ADAM_B1, ADAM_B2, ADAM_EPS, ADAM_WD, ADAM_STEP = 0.001, 0.9, 0.999, 1e-08, 0.01, 10

LANES = 128
SUBLANES = 8
VMEM_LIMIT = 48 * 1024 * 1024
SCAN_CHUNK = 16
SCAN_UNROLL = 3
ROW_TILE = 128
WIDE_TILE = 256

BIG = ("w_in", "w_out", "w_gate", "w_up", "w_down")
LORA = ("w_up_f", "w_up_b", "a_up_f", "a_up_b", "g_up", "conv_w")
SHARD_AXIS = {"w_in": 2, "w_out": 1, "w_gate": 2, "w_up": 2, "w_down": 1, "w_up_f": 2, "w_up_b": 2,
              "a_up_f": 2, "a_up_b": 2, "g_up": 2, "conv_w": 2}
VEC = ("w0_f", "w0_b", "a0_f", "a0_b", "k_k", "k_a_f", "k_a_b", "r_k_f", "r_k_b", "gn_w", "gn_b")
WIDE = ("mu_shift", "norm1_w", "norm2_w", "norm_f_w")
WIDE_ROW = 2048
WEIGHTS = ("norm1_w", "w_in", "mu_shift", "w_up_f", "w0_f", "w_up_b", "w0_b", "a_up_f", "a0_f", "a_up_b",
           "a0_b", "g_up", "k_k", "k_a_f", "k_a_b", "r_k_f", "r_k_b", "gn_w", "gn_b", "conv_w", "w_out",
           "norm2_w", "w_gate", "w_up", "w_down", "norm_f_w")


def _params(sem, limit=VMEM_LIMIT):
    return pltpu.CompilerParams(dimension_semantics=sem, vmem_limit_bytes=limit)


def _tile(n, cands):
    for c in cands:
        if n % c == 0:
            return c
    raise ValueError(f"no tile for {n}")


def _mm(a, b, *, ta=False, tb=False, add=None, name):
    (k_dim, m) = a.shape if ta else a.shape[::-1]
    (k2, n) = b.shape[::-1] if tb else b.shape
    assert k_dim == k2, (a.shape, b.shape, ta, tb)
    tm = _tile(m, (1408, 1024, 512, 256, 128))
    tn = _tile(n, (1408, 1024, 896, 512, 256, 128))
    tk = k_dim if k_dim <= 1024 else _tile(k_dim, (1408, 896, 512, 256, 128))
    nk = k_dim // tk
    dims = (((0 if ta else 1,), (1 if tb else 0,)), ((), ()))

    def kern(*refs):
        if add is None:
            a_ref, b_ref, o_ref, acc_ref = refs
        else:
            a_ref, b_ref, add_ref, o_ref, acc_ref = refs
        k = pl.program_id(2)

        @pl.when(k == 0)
        def _():
            acc_ref[...] = jnp.zeros_like(acc_ref)

        acc_ref[...] += lax.dot_general(a_ref[...].astype(BF16), b_ref[...].astype(BF16), dims,
                                        preferred_element_type=F32)

        @pl.when(k == nk - 1)
        def _():
            if add is None:
                o_ref[...] = acc_ref[...]
            else:
                o_ref[...] = acc_ref[...] + add_ref[...]

    a_spec = (pl.BlockSpec((tk, tm), lambda i, j, k: (k, i)) if ta
              else pl.BlockSpec((tm, tk), lambda i, j, k: (i, k)))
    b_spec = (pl.BlockSpec((tn, tk), lambda i, j, k: (j, k)) if tb
              else pl.BlockSpec((tk, tn), lambda i, j, k: (k, j)))
    o_spec = pl.BlockSpec((tm, tn), lambda i, j, k: (i, j))
    in_specs = [a_spec, b_spec] + ([o_spec] if add is not None else [])
    args = (a, b) + ((add,) if add is not None else ())
    return pl.pallas_call(
        kern, out_shape=jax.ShapeDtypeStruct((m, n), F32), grid=(m // tm, n // tn, nk),
        in_specs=in_specs, out_specs=o_spec, scratch_shapes=[pltpu.VMEM((tm, tn), F32)],
        compiler_params=_params(("parallel", "parallel", "arbitrary")), name=name)(*args)


def _swiglu(g, u):
    return jax.nn.silu(g) * u


FFN_TN = 256


def _mm_swiglu(h, w_gate, w_up, *, name):
    m, k_dim = h.shape
    n = w_gate.shape[1]
    tm = _tile(m, (1024, 512, 256, 128))

    def kern(h_ref, wg_ref, wu_ref, g_ref, u_ref, f_ref):
        hv = h_ref[...].astype(BF16)
        g = jnp.dot(hv, wg_ref[...].astype(BF16), preferred_element_type=F32)
        u = jnp.dot(hv, wu_ref[...].astype(BF16), preferred_element_type=F32)
        g_ref[...] = g
        u_ref[...] = u
        f_ref[...] = _swiglu(g, u).astype(f_ref.dtype)

    w_spec = pl.BlockSpec((k_dim, FFN_TN), lambda i, j: (0, j))
    o_spec = pl.BlockSpec((tm, FFN_TN), lambda i, j: (i, j))
    return pl.pallas_call(
        kern, out_shape=[jax.ShapeDtypeStruct((m, n), F32)] * 2 + [jax.ShapeDtypeStruct((m, n), BF16)],
        grid=(m // tm, n // FFN_TN), in_specs=[pl.BlockSpec((tm, k_dim), lambda i, j: (i, 0)), w_spec, w_spec],
        out_specs=[o_spec] * 3, compiler_params=_params(("parallel", "parallel")), name=name)(h, w_gate, w_up)


def _mm_swiglu_bwd(dx, w_down, g, u, *, name):
    m, k_dim = dx.shape
    n = w_down.shape[0]
    tm = _tile(m, (1024, 512, 256, 128))

    def kern(dx_ref, w_ref, g_ref, u_ref, dg_ref, du_ref):
        df = lax.dot_general(dx_ref[...].astype(BF16), w_ref[...].astype(BF16), (((1,), (1,)), ((), ())),
                             preferred_element_type=F32)
        _, vjp = jax.vjp(_swiglu, g_ref[...], u_ref[...])
        dg, du = vjp(df)
        dg_ref[...] = dg.astype(dg_ref.dtype)
        du_ref[...] = du.astype(du_ref.dtype)

    o_spec = pl.BlockSpec((tm, FFN_TN), lambda i, j: (i, j))
    return pl.pallas_call(
        kern, out_shape=[jax.ShapeDtypeStruct((m, n), BF16)] * 2, grid=(m // tm, n // FFN_TN),
        in_specs=[pl.BlockSpec((tm, k_dim), lambda i, j: (i, 0)), pl.BlockSpec((FFN_TN, k_dim), lambda i, j: (j, 0)),
                  o_spec, o_spec],
        out_specs=[o_spec] * 2, compiler_params=_params(("parallel", "parallel")), name=name)(dx, w_down, g, u)


def _rowwise(fn, rows, consts, out_rows, out_accs, *, name, tb=ROW_TILE, out_dtype=F32):
    t = (rows[0][0] if isinstance(rows[0], tuple) else rows[0]).shape[0]
    n_r, n_c, n_o, n_a = len(rows), len(consts), len(out_rows), len(out_accs)
    pieces = [w if isinstance(w, (list, tuple)) else [w] for w in out_rows]

    def kern(*refs):
        r_refs = refs[:n_r]
        c_refs = refs[n_r:n_r + n_c]
        o_refs = refs[n_r + n_c:n_r + n_c + n_o]
        a_refs = refs[n_r + n_c + n_o:]
        vals = fn(*[r[...] for r in r_refs], *[c[...] for c in c_refs])
        vals = list(vals) if isinstance(vals, (tuple, list)) else [vals]
        pos = 0
        for o_ref, ws in zip(o_refs, pieces):
            off = 0
            for w in ws:
                o_ref[:, off:off + w] = vals[pos].astype(o_ref.dtype)
                off += w
                pos += 1
        if n_a:
            @pl.when(pl.program_id(0) == 0)
            def _():
                for a_ref in a_refs:
                    a_ref[...] = jnp.zeros_like(a_ref)
            for a_ref, v in zip(a_refs, vals[pos:]):
                a_ref[...] += v

    in_specs, args = [], []
    for r in rows:
        if isinstance(r, tuple):
            arr, blk, w = r
            in_specs.append(pl.BlockSpec((tb, w), functools.partial(lambda i, blk: (i, blk), blk=blk)))
        else:
            arr = r
            in_specs.append(pl.BlockSpec((tb, arr.shape[1]), lambda i: (i, 0)))
        args.append(arr)
    for c in consts:
        in_specs.append(pl.BlockSpec(c.shape, lambda i: (0, 0)))
        args.append(c)
    out_shape = [jax.ShapeDtypeStruct((t, sum(ws)), out_dtype) for ws in pieces]
    out_specs = [pl.BlockSpec((tb, sum(ws)), lambda i: (i, 0)) for ws in pieces]
    for shp in out_accs:
        out_shape.append(jax.ShapeDtypeStruct(shp, F32))
        out_specs.append(pl.BlockSpec(shp, lambda i: (0, 0)))
    res = pl.pallas_call(
        kern, out_shape=out_shape, grid=(t // tb,), in_specs=in_specs, out_specs=out_specs,
        compiler_params=_params(("arbitrary",) if n_a else ("parallel",)), name=name)(*args)
    return res


def _rms(x, w):
    return x * lax.rsqrt(jnp.mean(x * x, axis=-1, keepdims=True) + RMS_EPS) * w


def _seg_sum(x, bd):
    return jnp.concatenate(
        [jnp.dot(x[:, LANES * j:LANES * (j + 1)], bd, precision=HIGHEST, preferred_element_type=F32)
         for j in range(x.shape[1] // LANES)], axis=1)


@jax.custom_vjp
def _seg(x, bd):
    return _seg_sum(x, bd)


_seg.defvjp(lambda x, bd: (_seg_sum(x, bd), bd), lambda bd, ct: (_seg_sum(ct, bd), jnp.zeros_like(bd)))


def _colsum(x):
    return jnp.sum(x, axis=0, keepdims=True)


def _prescan_math(r, k, xw, xa, xg, k_k, w0f, w0b, a0f, a0b, kaf, kab, wupf, wupb, aupf, aupb, gup, bd):
    kkr = k * k_k
    norm = jnp.sqrt(_seg(kkr * kkr, bd))
    kk = kkr / jnp.maximum(norm, NORM_EPS)
    th = jnp.tanh(xw)

    def direction(w0, wup, a0, aup, ka):
        logit = w0 + jnp.dot(th, wup, preferred_element_type=F32)
        w = jnp.exp(-LOG_DECAY_SCALE * jax.nn.sigmoid(logit))
        a = jax.nn.sigmoid(a0 + jnp.dot(xa, aup, preferred_element_type=F32))
        kd = k * (1.0 + (a - 1.0) * ka)
        return w, kd, kk * a

    wf, kdf, bf = direction(w0f, wupf, a0f, aupf, kaf)
    wb, kdb, bb = direction(w0b, wupb, a0b, aupb, kab)
    g = jnp.dot(jax.nn.sigmoid(xg), gup, preferred_element_type=F32)
    return kk, r, wf, wb, bf, bb, kdf, kdb, g


def _postscan_math(y, r, v, kdf, kdb, g, gn_w, gn_b, rkf, rkb, bd):
    mean = _seg(y, bd) * (1.0 / HEAD)
    yc = y - mean
    var = _seg(yc * yc, bd) * (1.0 / HEAD)
    yg = yc * lax.rsqrt(var + GN_EPS) * gn_w + gn_b
    bonus = (_seg(r * kdf * rkf, bd) + _seg(r * kdb * rkb, bd)) * v
    return (yg + bonus) * g


def _halo_specs(width, col_blk, tb, t):
    nb = t // SUBLANES
    step = tb // SUBLANES
    main = pl.BlockSpec((tb, width), lambda i: (i, col_blk))
    prev = pl.BlockSpec((SUBLANES, width), lambda i: (jnp.maximum(i * step - 1, 0), col_blk))
    nxt = pl.BlockSpec((SUBLANES, width), lambda i: (jnp.minimum((i + 1) * step, nb - 1), col_blk))
    return [main, prev, nxt]


def _neighbours(z, prev8, next8, first, last):
    tb = z.shape[0]
    row = lax.broadcasted_iota(jnp.int32, z.shape, 0)
    prow = jnp.where(first, 0.0, prev8[SUBLANES - 1:SUBLANES, :])
    nrow = jnp.where(last, 0.0, next8[0:1, :])
    down = jnp.where(row == 0, prow, pltpu.roll(z, 1, 0))
    up = jnp.where(row == tb - 1, nrow, pltpu.roll(z, tb - 1, 0))
    return down, up


def _shift_conv_fwd(p, mu, conv_w, seq, *, name, tb=ROW_TILE):
    t = p.shape[0]
    per_seq = seq // tb

    def kern(p_ref, pp_ref, pn_ref, mu_ref, cw_ref, pss_ref, oc_ref):
        i = pl.program_id(0)
        first = (i % per_seq) == 0
        last = (i % per_seq) == per_seq - 1
        ps = p_ref[:, :D_SP]
        down, up = _neighbours(ps, pp_ref[:, :D_SP], pn_ref[:, :D_SP], first, last)
        pss_ref[...] = ps + mu_ref[...] * (0.5 * (down + up) - ps)
        gb = p_ref[:, D_SP:D_SP + D_CONV]
        u = p_ref[:, D_SP + D_CONV:D_SP + 2 * D_CONV] * p_ref[:, D_SP + 2 * D_CONV:]
        u_p = pp_ref[:, D_SP + D_CONV:D_SP + 2 * D_CONV] * pp_ref[:, D_SP + 2 * D_CONV:]
        u_n = pn_ref[:, D_SP + D_CONV:D_SP + 2 * D_CONV] * pn_ref[:, D_SP + 2 * D_CONV:]
        udown, uup = _neighbours(u, u_p, u_n, first, last)
        oc_ref[...] = gb * (cw_ref[0:1, :] * udown + cw_ref[1:2, :] * u + cw_ref[2:3, :] * uup)

    return pl.pallas_call(
        kern,
        out_shape=[jax.ShapeDtypeStruct((t, D_SP), F32), jax.ShapeDtypeStruct((t, D_CONV), F32)],
        grid=(t // tb,),
        in_specs=_halo_specs(D_INP, 0, tb, t) + [pl.BlockSpec((1, D_SP), lambda i: (0, 0)),
                                                 pl.BlockSpec((SUBLANES, D_CONV), lambda i: (0, 0))],
        out_specs=[pl.BlockSpec((tb, D_SP), lambda i: (i, 0)), pl.BlockSpec((tb, D_CONV), lambda i: (i, 0))],
        compiler_params=_params(("parallel",)), name=name)(p, p, p, mu, conv_w)


def _shift_conv_bwd(p, d_pss, d_o, mu, conv_w, seq, *, name, tb=ROW_TILE):
    t = p.shape[0]
    per_seq = seq // tb

    def kern(p_ref, pp_ref, pn_ref, d_ref, dp_ref, dn_ref, do_ref, dop_ref, don_ref, mu_ref, cw_ref,
             out_ref, dmu_ref, dcw_ref):
        i = pl.program_id(0)
        first = (i % per_seq) == 0
        last = (i % per_seq) == per_seq - 1

        @pl.when(i == 0)
        def _():
            dmu_ref[...] = jnp.zeros_like(dmu_ref)
            dcw_ref[...] = jnp.zeros_like(dcw_ref)

        mu_v = mu_ref[...]
        ps = p_ref[:, :D_SP]
        down, up = _neighbours(ps, pp_ref[:, :D_SP], pn_ref[:, :D_SP], first, last)
        d = d_ref[...]
        ddown, dup = _neighbours(d, dp_ref[...], dn_ref[...], first, last)
        out_ref[:, :D_SP] = (d - mu_v * d + 0.5 * (mu_v * ddown + mu_v * dup)).astype(out_ref.dtype)
        dmu_ref[...] += _colsum(d * (0.5 * (down + up) - ps))

        def parts(ref):
            return (ref[:, D_SP:D_SP + D_CONV], ref[:, D_SP + D_CONV:D_SP + 2 * D_CONV],
                    ref[:, D_SP + 2 * D_CONV:])

        gb, gc, hh = parts(p_ref)
        gb_p, gc_p, hh_p = parts(pp_ref)
        gb_n, gc_n, hh_n = parts(pn_ref)
        u = gc * hh
        udown, uup = _neighbours(u, gc_p * hh_p, gc_n * hh_n, first, last)
        cw0, cw1, cw2 = cw_ref[0:1, :], cw_ref[1:2, :], cw_ref[2:3, :]
        do = do_ref[...]
        duc = do * gb
        ducdown, ducup = _neighbours(duc, dop_ref[...] * gb_p, don_ref[...] * gb_n, first, last)
        du = cw0 * ducup + cw1 * duc + cw2 * ducdown
        out_ref[:, D_SP:D_SP + D_CONV] = (do * (cw0 * udown + cw1 * u + cw2 * uup)).astype(out_ref.dtype)
        out_ref[:, D_SP + D_CONV:D_SP + 2 * D_CONV] = (du * hh).astype(out_ref.dtype)
        out_ref[:, D_SP + 2 * D_CONV:] = (du * gc).astype(out_ref.dtype)
        dcw_ref[0:1, :] += _colsum(duc * udown)
        dcw_ref[1:2, :] += _colsum(duc * u)
        dcw_ref[2:3, :] += _colsum(duc * uup)

    return pl.pallas_call(
        kern,
        out_shape=[jax.ShapeDtypeStruct((t, D_INP), BF16), jax.ShapeDtypeStruct((1, D_SP), F32),
                   jax.ShapeDtypeStruct((SUBLANES, D_CONV), F32)],
        grid=(t // tb,),
        in_specs=(_halo_specs(D_INP, 0, tb, t) + _halo_specs(D_SP, 0, tb, t) + _halo_specs(D_CONV, 1, tb, t)
                  + [pl.BlockSpec((1, D_SP), lambda i: (0, 0)),
                     pl.BlockSpec((SUBLANES, D_CONV), lambda i: (0, 0))]),
        out_specs=[pl.BlockSpec((tb, D_INP), lambda i: (i, 0)), pl.BlockSpec((1, D_SP), lambda i: (0, 0)),
                   pl.BlockSpec((SUBLANES, D_CONV), lambda i: (0, 0))],
        compiler_params=_params(("arbitrary",)), name=name)(p, p, p, d_pss, d_pss, d_pss, d_o, d_o, d_o, mu, conv_w)


N_CHAIN = 16
V_LO = LANES // N_CHAIN
V_HI = HEAD // V_LO
N_GROUP = LANES // N_CHAIN
G_KK, G_R, G_W, G_B, G_KD = 0, 1, (2, 3), (4, 5), (6, 7)


def _group(x, j, lane):
    g = pltpu.roll(x, (LANES - N_CHAIN * j) % LANES, 1) if j else x
    g = jnp.where(lane < N_CHAIN, g, pltpu.roll(g, N_CHAIN, 1))
    g = jnp.where(lane < 2 * N_CHAIN, g, pltpu.roll(g, 2 * N_CHAIN, 1))
    return jnp.where(lane < 4 * N_CHAIN, g, pltpu.roll(g, 4 * N_CHAIN, 1))


def _scan_inputs(x, d, lane):
    return [_group(x, j, lane) for j in (G_KK, G_R, G_W[d], G_B[d], G_KD[d])]


def _lane_scan_fwd(xall, v_l, *, name):
    steps = xall.shape[0]
    nc = steps // SCAN_CHUNK
    mirror = lambda c: nc - 1 - c

    def kern(xf_ref, xb_ref, vf_ref, vb_ref, yf_ref, yb_ref, hist_ref, fin_ref, st_ref):
        c = pl.program_id(0)

        @pl.when(c == 0)
        def _():
            st_ref[...] = jnp.zeros_like(st_ref)

        row = lax.broadcasted_iota(jnp.int32, (V_HI, LANES), 0)
        lane = lax.broadcasted_iota(jnp.int32, (HEAD, LANES), 1)

        def step(i, carry):
            j = SCAN_CHUNK - 1 - i
            for d, (x_t, v_t, y_ref, at) in enumerate(((xf_ref[i], vf_ref[i], yf_ref, i),
                                                       (xb_ref[j], vb_ref[j], yb_ref, j))):
                kk_t, r_t, w_t, b_t, kd_t = _scan_inputs(x_t, d, lane)
                y_t = jnp.zeros((V_HI, LANES), F32)
                for vh in range(V_HI):
                    tile = d * V_HI + vh
                    state = st_ref[tile]
                    hist_ref[i, tile] = state
                    sa = _colsum(state * kk_t)
                    state = state * w_t - sa * b_t + v_t[vh:vh + 1, :] * kd_t
                    st_ref[tile] = state
                    y_t = jnp.where(row == vh, _colsum(state * r_t), y_t)
                y_ref[at] = y_t
            return carry

        lax.fori_loop(0, SCAN_CHUNK, step, 0)

        @pl.when(c == nc - 1)
        def _():
            fin_ref[...] = st_ref[...]

    def k_spec(fn):
        return pl.BlockSpec((SCAN_CHUNK, HEAD, LANES), lambda c: (fn(c), 0, 0))

    def v_spec(fn):
        return pl.BlockSpec((SCAN_CHUNK, V_HI, LANES), lambda c: (fn(c), 0, 0))

    same = lambda c: c
    st_shape = (2 * V_HI, HEAD, LANES)
    return pl.pallas_call(
        kern,
        out_shape=[jax.ShapeDtypeStruct((steps, V_HI, LANES), F32)] * 2
        + [jax.ShapeDtypeStruct((steps,) + st_shape, F32), jax.ShapeDtypeStruct(st_shape, F32)],
        grid=(nc,), in_specs=[k_spec(same), k_spec(mirror), v_spec(same), v_spec(mirror)],
        out_specs=[v_spec(same), v_spec(mirror),
                   pl.BlockSpec((SCAN_CHUNK,) + st_shape, lambda c: (c, 0, 0, 0)),
                   pl.BlockSpec(st_shape, lambda c: (0, 0, 0))],
        scratch_shapes=[pltpu.VMEM(st_shape, F32)],
        compiler_params=_params(("arbitrary",)), name=name)(xall, xall, v_l, v_l)


def _lane_scan_bwd(xall, v_l, dy_l, hist, fin, *, name):
    steps = xall.shape[0]
    nc = steps // SCAN_CHUNK
    back = lambda c: nc - 1 - c
    same = lambda c: c

    def kern(xf_ref, xb_ref, vf_ref, vb_ref, dyf_ref, dyb_ref, hist_ref, fin_ref,
             gf_ref, gb_ref, dvf_ref, dvb_ref, ds_ref, after_ref):
        c = pl.program_id(0)

        @pl.when(c == 0)
        def _():
            ds_ref[...] = jnp.zeros_like(ds_ref)
            after_ref[...] = fin_ref[...]

        row = lax.broadcasted_iota(jnp.int32, (V_HI, LANES), 0)
        lane = lax.broadcasted_iota(jnp.int32, (HEAD, LANES), 1)
        grp = lax.shift_right_logical(lane, jnp.full_like(lane, 4))

        def group_sum(x):
            x = x + pltpu.roll(x, 4 * N_CHAIN, 1)
            x = x + pltpu.roll(x, 2 * N_CHAIN, 1)
            return x + pltpu.roll(x, N_CHAIN, 1)

        def step(ii, carry):
            i = SCAN_CHUNK - 1 - ii
            j = ii
            for d, (x_t, v_t, dy_t, g_ref, dv_ref, at) in enumerate((
                    (xf_ref[i], vf_ref[i], dyf_ref[i], gf_ref, dvf_ref, i),
                    (xb_ref[j], vb_ref[j], dyb_ref[j], gb_ref, dvb_ref, j))):
                kk_t, r_t, w_t, b_t, kd_t = _scan_inputs(x_t, d, lane)
                dv_t = jnp.zeros((V_HI, LANES), F32)
                zero = jnp.zeros((HEAD, LANES), F32)
                dkk, dr, dw, db, dkd = zero, zero, zero, zero, zero
                for vh in range(V_HI):
                    tile = d * V_HI + vh
                    before = hist_ref[i, tile]
                    dy_r, v_r = dy_t[vh:vh + 1, :], v_t[vh:vh + 1, :]
                    g = ds_ref[tile] + dy_r * r_t
                    sa = _colsum(before * kk_t)
                    dsa = -_colsum(g * b_t)
                    dv_t = jnp.where(row == vh, _colsum(g * kd_t), dv_t)
                    dr = dr + after_ref[tile] * dy_r
                    dw = dw + g * before
                    dkd = dkd + g * v_r
                    db = db - g * sa
                    dkk = dkk + before * dsa
                    ds_ref[tile] = g * w_t + dsa * kk_t
                    after_ref[tile] = before
                out = jnp.where(grp == G_KK, group_sum(dkk), 0.0)
                out = jnp.where(grp == G_R, group_sum(dr), out)
                out = jnp.where(grp == G_W[d], group_sum(dw), out)
                out = jnp.where(grp == G_B[d], group_sum(db), out)
                out = jnp.where(grp == G_KD[d], group_sum(dkd), out)
                g_ref[at] = out
                dv_ref[at] = dv_t
            return carry

        lax.fori_loop(0, SCAN_CHUNK, step, 0)

    def k_spec(fn):
        return pl.BlockSpec((SCAN_CHUNK, HEAD, LANES), lambda c: (fn(c), 0, 0))

    def v_spec(fn):
        return pl.BlockSpec((SCAN_CHUNK, V_HI, LANES), lambda c: (fn(c), 0, 0))

    st_shape = (2 * V_HI, HEAD, LANES)
    return pl.pallas_call(
        kern,
        out_shape=[jax.ShapeDtypeStruct((steps, HEAD, LANES), F32)] * 2
        + [jax.ShapeDtypeStruct((steps, V_HI, LANES), F32)] * 2,
        grid=(nc,),
        in_specs=[k_spec(back), k_spec(same), v_spec(back), v_spec(same), v_spec(back), v_spec(same),
                  pl.BlockSpec((SCAN_CHUNK,) + st_shape, lambda c: (back(c), 0, 0, 0)),
                  pl.BlockSpec(st_shape, lambda c: (0, 0, 0))],
        out_specs=[k_spec(back), k_spec(same), v_spec(back), v_spec(same)],
        scratch_shapes=[pltpu.VMEM(st_shape, F32), pltpu.VMEM(st_shape, F32)],
        compiler_params=_params(("arbitrary",)), name=name)(xall, xall, v_l, v_l, dy_l, dy_l, hist, fin)


def _to_key_lanes(wide, bsz, seq):
    z = wide.reshape(bsz, seq, N_GROUP, N_HEAD, HEAD).transpose(1, 4, 2, 0, 3)
    return z.reshape(seq, HEAD, LANES)


def _from_key_lanes(g, bsz, seq):
    z = g.reshape(seq, HEAD, N_GROUP, bsz, N_HEAD).transpose(3, 0, 2, 4, 1)
    return z.reshape(bsz * seq, N_GROUP * D_RWKV)


def _to_value_lanes(a, bsz, seq):
    z = a.reshape(bsz, seq, N_HEAD, V_HI, V_LO).transpose(1, 3, 4, 0, 2)
    return z.reshape(seq, V_HI, LANES)


def _from_value_lanes(y, bsz, seq):
    z = y.reshape(seq, V_HI, V_LO, bsz, N_HEAD).transpose(3, 0, 4, 1, 2)
    return z.reshape(bsz * seq, D_RWKV)


K_HI = HEAD // SUBLANES


def _lane_group_sum(x):
    x = x + pltpu.roll(x, 4 * N_CHAIN, 1)
    x = x + pltpu.roll(x, 2 * N_CHAIN, 1)
    return x + pltpu.roll(x, N_CHAIN, 1)


def _key_rows(x_t, d):
    out = []
    for grp in (G_KK, G_R, G_W[d], G_B[d], G_KD[d]):
        blk = x_t[SUBLANES * grp:SUBLANES * (grp + 1), :]
        out.append([jnp.broadcast_to(blk[kh:kh + 1, :], (SUBLANES, LANES)) for kh in range(K_HI)])
    return out


def _tree_sum(terms):
    terms = list(terms)
    while len(terms) > 1:
        terms = [a + b for a, b in zip(terms[::2], terms[1::2])]
    return terms[0]


def _kscan_specs(nc):
    same = lambda c: c
    mirror = lambda c: nc - 1 - c

    def k_spec(fn):
        return pl.BlockSpec((SCAN_CHUNK, HEAD, LANES), lambda c: (fn(c), 0, 0))

    def v_spec(fn):
        return pl.BlockSpec((SCAN_CHUNK, SUBLANES, LANES), lambda c: (fn(c), 0, 0))

    return same, mirror, k_spec, v_spec


ST_SHAPE = (2, K_HI, V_HI, SUBLANES, LANES)


def _lane_group_index():
    lane = lax.broadcasted_iota(jnp.int32, (SUBLANES, LANES), 1)
    return lax.shift_right_logical(lane, jnp.full_like(lane, 4))


def _spread_groups(x, grp):
    rolled = [x] + [pltpu.roll(x, s * N_CHAIN, 1) for s in range(1, N_GROUP)]
    out = []
    for j in range(N_GROUP):
        t = rolled[(0 - j) % N_GROUP]
        for g in range(1, N_GROUP):
            t = jnp.where(grp == g, rolled[(g - j) % N_GROUP], t)
        out.append(t)
    return out


def _gather_groups(tiles, grp):
    total = None
    for s in range(N_GROUP):
        b = tiles[s % N_GROUP]
        for g in range(1, N_GROUP):
            b = jnp.where(grp == g, tiles[(g + s) % N_GROUP], b)
        b = pltpu.roll(b, s * N_CHAIN, 1) if s else b
        total = b if total is None else total + b
    return total


def _lane_group_sum_short(x):
    return _tree_sum([x] + [pltpu.roll(x, k * N_CHAIN, 1) for k in range(1, N_GROUP)])


def _kscan_fwd(xall, v_c, *, name):
    steps = xall.shape[0]
    nc = steps // SCAN_CHUNK
    same, mirror, k_spec, v_spec = _kscan_specs(nc)

    def kern(xf_ref, xb_ref, vf_ref, vb_ref, yf_ref, yb_ref, hist_ref, fin_ref, st_ref):
        c = pl.program_id(0)

        @pl.when(c == 0)
        def _():
            st_ref[...] = jnp.zeros_like(st_ref)

        grp = _lane_group_index()

        def step(i, carry):
            j = SCAN_CHUNK - 1 - i
            for d, (x_t, v_t, y_ref, at) in enumerate(((xf_ref[i], vf_ref[i], yf_ref, i),
                                                       (xb_ref[j], vb_ref[j], yb_ref, j))):
                kk_r, r_r, w_r, b_r, kd_r = _key_rows(x_t, d)
                v_b = _spread_groups(v_t, grp)
                y_p = []
                for vh in range(V_HI):
                    st = [st_ref[d, kh, vh] for kh in range(K_HI)]
                    for kh in range(K_HI):
                        hist_ref[i, d, kh, vh] = st[kh]
                    sa = _lane_group_sum_short(_tree_sum(st[kh] * kk_r[kh] for kh in range(K_HI)))
                    new = [st[kh] * w_r[kh] - sa * b_r[kh] + v_b[vh] * kd_r[kh] for kh in range(K_HI)]
                    for kh in range(K_HI):
                        st_ref[d, kh, vh] = new[kh]
                    y_p.append(_tree_sum(new[kh] * r_r[kh] for kh in range(K_HI)))
                y_ref[at] = _gather_groups(y_p, grp)
            return carry

        lax.fori_loop(0, SCAN_CHUNK, step, 0)

        @pl.when(c == nc - 1)
        def _():
            fin_ref[...] = st_ref[...]

    return pl.pallas_call(
        kern,
        out_shape=[jax.ShapeDtypeStruct((steps, SUBLANES, LANES), F32)] * 2
        + [jax.ShapeDtypeStruct((steps,) + ST_SHAPE, F32), jax.ShapeDtypeStruct(ST_SHAPE, F32)],
        grid=(nc,), in_specs=[k_spec(same), k_spec(mirror), v_spec(same), v_spec(mirror)],
        out_specs=[v_spec(same), v_spec(mirror),
                   pl.BlockSpec((SCAN_CHUNK,) + ST_SHAPE, lambda c: (c, 0, 0, 0, 0, 0)),
                   pl.BlockSpec(ST_SHAPE, lambda c: (0, 0, 0, 0, 0))],
        scratch_shapes=[pltpu.VMEM(ST_SHAPE, F32)],
        compiler_params=_params(("arbitrary",)), name=name)(xall, xall, v_c, v_c)


def _kscan_bwd(xall, v_c, dy_c, hist, fin, *, name):
    steps = xall.shape[0]
    nc = steps // SCAN_CHUNK
    same, back, k_spec, v_spec = _kscan_specs(nc)

    def kern(xf_ref, xb_ref, vf_ref, vb_ref, dyf_ref, dyb_ref, hist_ref, fin_ref,
             gf_ref, gb_ref, dvf_ref, dvb_ref, ds_ref, after_ref):
        c = pl.program_id(0)

        @pl.when(c == 0)
        def _():
            ds_ref[...] = jnp.zeros_like(ds_ref)
            after_ref[...] = fin_ref[...]

        grp = _lane_group_index()
        row = lax.broadcasted_iota(jnp.int32, (SUBLANES, LANES), 0)

        def step(ii, carry):
            i = SCAN_CHUNK - 1 - ii
            j = ii
            for d, (x_t, v_t, dy_t, g_ref, dv_ref, at) in enumerate((
                    (xf_ref[i], vf_ref[i], dyf_ref[i], gf_ref, dvf_ref, i),
                    (xb_ref[j], vb_ref[j], dyb_ref[j], gb_ref, dvb_ref, j))):
                kk_r, r_r, w_r, b_r, kd_r = _key_rows(x_t, d)
                v_s, dy_s = _spread_groups(v_t, grp), _spread_groups(dy_t, grp)
                ks = range(K_HI)
                zero = jnp.zeros((SUBLANES, LANES), F32)
                dkk, dr, dw, db, dkd = ([zero] * K_HI for _ in range(5))
                dv_p = []
                for vh in range(V_HI):
                    v_b, dy_b = v_s[vh], dy_s[vh]
                    before = [hist_ref[i, d, kh, vh] for kh in ks]
                    g = [ds_ref[d, kh, vh] + dy_b * r_r[kh] for kh in ks]
                    dsa = -_lane_group_sum_short(_tree_sum(g[kh] * b_r[kh] for kh in ks))
                    sa = _lane_group_sum(_tree_sum(before[kh] * kk_r[kh] for kh in ks))
                    dv_p.append(_tree_sum(g[kh] * kd_r[kh] for kh in ks))
                    dr = [dr[kh] + after_ref[d, kh, vh] * dy_b for kh in ks]
                    dw = [dw[kh] + g[kh] * before[kh] for kh in ks]
                    dkd = [dkd[kh] + g[kh] * v_b for kh in ks]
                    db = [db[kh] - g[kh] * sa for kh in ks]
                    dkk = [dkk[kh] + before[kh] * dsa for kh in ks]
                    for kh in ks:
                        ds_ref[d, kh, vh] = g[kh] * w_r[kh] + dsa * kk_r[kh]
                        after_ref[d, kh, vh] = before[kh]
                dv_ref[at] = _gather_groups(dv_p, grp)
                blocks = {G_KK: dkk, G_R: dr, G_W[d]: dw, G_B[d]: db, G_KD[d]: dkd}
                for gi in range(N_GROUP):
                    blk = zero
                    if gi in blocks:
                        for kh in ks:
                            blk = jnp.where(row == kh, _colsum(blocks[gi][kh]), blk)
                    g_ref[at, SUBLANES * gi:SUBLANES * (gi + 1), :] = blk
            return carry

        lax.fori_loop(0, SCAN_CHUNK, step, 0)

    return pl.pallas_call(
        kern,
        out_shape=[jax.ShapeDtypeStruct((steps, HEAD, LANES), F32)] * 2
        + [jax.ShapeDtypeStruct((steps, SUBLANES, LANES), F32)] * 2,
        grid=(nc,),
        in_specs=[k_spec(back), k_spec(same), v_spec(back), v_spec(same), v_spec(back), v_spec(same),
                  pl.BlockSpec((SCAN_CHUNK,) + ST_SHAPE, lambda c: (back(c), 0, 0, 0, 0, 0)),
                  pl.BlockSpec(ST_SHAPE, lambda c: (0, 0, 0, 0, 0))],
        out_specs=[k_spec(back), k_spec(same), v_spec(back), v_spec(same)],
        scratch_shapes=[pltpu.VMEM(ST_SHAPE, F32), pltpu.VMEM(ST_SHAPE, F32)],
        compiler_params=_params(("arbitrary",)), name=name)(xall, xall, v_c, v_c, dy_c, dy_c, hist, fin)


def _key_row(x_t, grp, kh):
    r = SUBLANES * grp + kh
    return jnp.broadcast_to(x_t[r:r + 1, :], (SUBLANES, LANES))


def _acc(total, term):
    return term if total is None else total + term


SA_SHAPE = (2, V_HI, SUBLANES, LANES)


def _scan_fwd(xall, v_c, *, gather=(), name):
    steps = xall.shape[0]
    nc = steps // SCAN_CHUNK
    same, mirror, k_spec, v_spec = _kscan_specs(nc)
    last = SCAN_CHUNK - 1
    n_x = len(gather)

    def kern(*refs):
        xf_ref, xb_ref, vf_ref, vb_ref = refs[:4]
        yf_ref, yb_ref, hist_ref, fin_ref, sa_ref = refs[4 + n_x:9 + n_x]
        st_ref = refs[9 + 2 * n_x]
        c = pl.program_id(0)

        def riders():
            return _exchange_copies(refs[4:4 + n_x], refs[9 + n_x:9 + 2 * n_x], 0, *refs[10 + 2 * n_x:])

        @pl.when(c == 0)
        def _():
            st_ref[...] = jnp.zeros_like(st_ref)
            if n_x:
                for cp in riders():
                    cp.start()

        hist_ref[0] = st_ref[...]
        grp = _lane_group_index()

        def body(i, put):
            j = last - i
            for d, (x_t, v_t, y_ref, at) in enumerate(((xf_ref[i], vf_ref[i], yf_ref, i),
                                                       (xb_ref[j], vb_ref[j], yb_ref, j))):
                v_b = _spread_groups(v_t, grp)
                part = [None] * V_HI
                for kh in range(K_HI):
                    kk_r = _key_row(x_t, G_KK, kh)
                    for vh in range(V_HI):
                        part[vh] = _acc(part[vh], hist_ref[i, d, kh, vh] * kk_r)
                sa = [_lane_group_sum_short(p) for p in part]
                for vh in range(V_HI):
                    sa_ref[i, d, vh] = sa[vh]
                y_p = [None] * V_HI
                for kh in range(K_HI):
                    r_r, w_r = _key_row(x_t, G_R, kh), _key_row(x_t, G_W[d], kh)
                    b_r, kd_r = _key_row(x_t, G_B[d], kh), _key_row(x_t, G_KD[d], kh)
                    for vh in range(V_HI):
                        new = hist_ref[i, d, kh, vh] * w_r - sa[vh] * b_r + v_b[vh] * kd_r
                        put(d, kh, vh, new)
                        y_p[vh] = _acc(y_p[vh], new * r_r)
                y_ref[at] = _gather_groups(y_p, grp)

        def step(i, carry):
            def put(d, kh, vh, val):
                hist_ref[i + 1, d, kh, vh] = val
            body(i, put)
            return carry

        lax.fori_loop(0, last, step, 0, unroll=SCAN_UNROLL)

        def put_carry(d, kh, vh, val):
            st_ref[d, kh, vh] = val

        body(last, put_carry)

        @pl.when(c == nc - 1)
        def _():
            fin_ref[...] = st_ref[...]
            if n_x:
                for cp in riders():
                    cp.wait()

    return pl.pallas_call(
        kern,
        out_shape=[jax.ShapeDtypeStruct((steps, SUBLANES, LANES), F32)] * 2
        + [jax.ShapeDtypeStruct((steps,) + ST_SHAPE, F32), jax.ShapeDtypeStruct(ST_SHAPE, F32),
           jax.ShapeDtypeStruct((steps,) + SA_SHAPE, F32)]
        + _exchange_out_shapes(gather, 0),
        grid=(nc,), in_specs=[k_spec(same), k_spec(mirror), v_spec(same), v_spec(mirror)] + _hbm_specs(n_x),
        out_specs=[v_spec(same), v_spec(mirror),
                   pl.BlockSpec((SCAN_CHUNK,) + ST_SHAPE, lambda c: (c, 0, 0, 0, 0, 0)),
                   pl.BlockSpec(ST_SHAPE, lambda c: (0, 0, 0, 0, 0)),
                   pl.BlockSpec((SCAN_CHUNK,) + SA_SHAPE, lambda c: (c, 0, 0, 0, 0))] + _hbm_specs(n_x),
        scratch_shapes=[pltpu.VMEM(ST_SHAPE, F32)] + (_exchange_sems(n_x) if n_x else []),
        compiler_params=_params(("arbitrary",)), name=name)(xall, xall, v_c, v_c, *gather)


def _scan_bwd(xall, v_c, dy_c, hist, fin, sa, *, exchange=(), name):
    steps = xall.shape[0]
    nc = steps // SCAN_CHUNK
    same, back, k_spec, v_spec = _kscan_specs(nc)
    last = SCAN_CHUNK - 1
    n_x = len(exchange)

    def kern(*refs):
        xf_ref, xb_ref, vf_ref, vb_ref, dyf_ref, dyb_ref, hist_ref, fin_ref, sa_ref = refs[:9]
        gf_ref, gb_ref, dvf_ref, dvb_ref = refs[9 + n_x:13 + n_x]
        ds_ref, after_ref = refs[13 + 2 * n_x:15 + 2 * n_x]
        c = pl.program_id(0)

        def riders():
            return _exchange_copies(refs[9:9 + n_x], refs[13 + n_x:13 + 2 * n_x], n_x, *refs[15 + 2 * n_x:])

        @pl.when(c == 0)
        def _():
            ds_ref[...] = jnp.zeros_like(ds_ref)
            after_ref[...] = fin_ref[...]
            if n_x:
                for cp in riders():
                    cp.start()

        grp = _lane_group_index()
        row = lax.broadcasted_iota(jnp.int32, (SUBLANES, LANES), 0)
        zero = jnp.zeros((SUBLANES, LANES), F32)

        def body(i, after):
            j = last - i
            for d, (x_t, v_t, dy_t, g_ref, dv_ref, at) in enumerate((
                    (xf_ref[i], vf_ref[i], dyf_ref[i], gf_ref, dvf_ref, i),
                    (xb_ref[j], vb_ref[j], dyb_ref[j], gb_ref, dvb_ref, j))):
                v_s, dy_s = _spread_groups(v_t, grp), _spread_groups(dy_t, grp)
                dsa_p, dv_p = [None] * V_HI, [None] * V_HI
                for kh in range(K_HI):
                    r_r = _key_row(x_t, G_R, kh)
                    b_r, kd_r = _key_row(x_t, G_B[d], kh), _key_row(x_t, G_KD[d], kh)
                    for vh in range(V_HI):
                        g = ds_ref[d, kh, vh] + dy_s[vh] * r_r
                        ds_ref[d, kh, vh] = g
                        dsa_p[vh] = _acc(dsa_p[vh], g * b_r)
                        dv_p[vh] = _acc(dv_p[vh], g * kd_r)
                dsa = [-_lane_group_sum_short(p) for p in dsa_p]
                sa = [sa_ref[i, d, vh] for vh in range(V_HI)]
                dv_ref[at] = _gather_groups(dv_p, grp)
                blocks = {G_KK: zero, G_R: zero, G_W[d]: zero, G_B[d]: zero, G_KD[d]: zero}
                for kh in range(K_HI):
                    w_r, kk_r = _key_row(x_t, G_W[d], kh), _key_row(x_t, G_KK, kh)
                    dkk = dr = dw = db = dkd = None
                    for vh in range(V_HI):
                        g, before = ds_ref[d, kh, vh], hist_ref[i, d, kh, vh]
                        dr = _acc(dr, after(d, kh, vh) * dy_s[vh])
                        dw = _acc(dw, g * before)
                        dkd = _acc(dkd, g * v_s[vh])
                        db = _acc(db, g * sa[vh])
                        dkk = _acc(dkk, before * dsa[vh])
                        ds_ref[d, kh, vh] = g * w_r + dsa[vh] * kk_r
                    for gi, a in ((G_KK, dkk), (G_R, dr), (G_W[d], dw), (G_B[d], -db), (G_KD[d], dkd)):
                        blocks[gi] = jnp.where(row == kh, _colsum(a), blocks[gi])
                for gi in range(N_GROUP):
                    g_ref[at, SUBLANES * gi:SUBLANES * (gi + 1), :] = blocks.get(gi, zero)

        body(last, lambda d, kh, vh: after_ref[d, kh, vh])

        def step(ii, carry):
            i = last - ii
            body(i, lambda d, kh, vh: hist_ref[i + 1, d, kh, vh])
            return carry

        lax.fori_loop(1, SCAN_CHUNK, step, 0, unroll=SCAN_UNROLL)
        after_ref[...] = hist_ref[0]

        if n_x:
            @pl.when(c == nc - 1)
            def _():
                for cp in riders():
                    cp.wait()

    return pl.pallas_call(
        kern,
        out_shape=[jax.ShapeDtypeStruct((steps, HEAD, LANES), F32)] * 2
        + [jax.ShapeDtypeStruct((steps, SUBLANES, LANES), F32)] * 2 + _exchange_out_shapes(exchange, n_x),
        grid=(nc,),
        in_specs=[k_spec(back), k_spec(same), v_spec(back), v_spec(same), v_spec(back), v_spec(same),
                  pl.BlockSpec((SCAN_CHUNK,) + ST_SHAPE, lambda c: (back(c), 0, 0, 0, 0, 0)),
                  pl.BlockSpec(ST_SHAPE, lambda c: (0, 0, 0, 0, 0)),
                  pl.BlockSpec((SCAN_CHUNK,) + SA_SHAPE, lambda c: (back(c), 0, 0, 0, 0))] + _hbm_specs(n_x),
        out_specs=[k_spec(back), k_spec(same), v_spec(back), v_spec(same)] + _hbm_specs(n_x),
        scratch_shapes=[pltpu.VMEM(ST_SHAPE, F32), pltpu.VMEM(ST_SHAPE, F32)]
        + (_exchange_sems(n_x) if n_x else []),
        compiler_params=_params(("arbitrary",)), name=name)(xall, xall, v_c, v_c, dy_c, dy_c, hist, fin, sa,
                                                            *exchange)


def _to_key_rows(wide, bsz, seq):
    z = wide.reshape(bsz, seq, N_GROUP, N_HEAD, K_HI, SUBLANES).transpose(1, 2, 4, 5, 0, 3)
    return z.reshape(seq, HEAD, LANES)


def _from_key_rows(g, bsz, seq):
    z = g.reshape(seq, N_GROUP, K_HI, SUBLANES, bsz, N_HEAD).transpose(4, 0, 1, 5, 2, 3)
    return z.reshape(bsz * seq, N_GROUP * D_RWKV)


def _to_value_rows(a, bsz, seq):
    z = a.reshape(bsz, seq, N_HEAD, V_HI, SUBLANES).transpose(1, 4, 3, 0, 2)
    return z.reshape(seq, SUBLANES, LANES)


def _from_value_rows(y, bsz, seq):
    z = y.reshape(seq, SUBLANES, V_HI, bsz, N_HEAD).transpose(3, 0, 4, 2, 1)
    return z.reshape(bsz * seq, D_RWKV)


def _pad_cols(a, segs):
    out, off = [], 0
    for w, wp in segs:
        out.append(a[..., off:off + w])
        if wp > w:
            out.append(jnp.zeros(a.shape[:-1] + (wp - w,), a.dtype))
        off += w
    return jnp.concatenate(out, axis=-1)


def _unpad_cols(a, segs):
    out, off = [], 0
    for w, wp in segs:
        out.append(a[..., off:off + w])
        off += wp
    return jnp.concatenate(out, axis=-1)


P_SEGS = ((3 * D_RWKV, 3 * D_RWKV), (D_LORA, 128), (D_LORA, 128), (D_GATE, 256), (3 * D_CONV, 3 * D_CONV))
S_SEGS = P_SEGS[:4]


def _pad_rows(a, rows):
    return jnp.concatenate([a, jnp.zeros((rows - a.shape[0], a.shape[1]), a.dtype)], axis=0)


LATE = ("w_out", "w_gate", "w_up", "w_down")


def _local_step(x, target, w, late=None):
    bsz, seq, _ = x.shape
    t = bsz * seq
    x2d = x.reshape(t, D_MODEL)
    tg2d = target.reshape(t, D_MODEL)
    row = lambda a: a.reshape(1, -1).astype(F32)

    w_in = _pad_cols(w["w_in"][0], P_SEGS)
    mu = _pad_cols(row(w["mu_shift"]), S_SEGS)
    wupf, wupb, aupf, aupb = (_pad_rows(w[n][0].astype(F32), 128) for n in ("w_up_f", "w_up_b", "a_up_f", "a_up_b"))
    gup = _pad_rows(w["g_up"][0].astype(F32), 256)
    conv_w = _pad_rows(w["conv_w"][0].astype(F32), SUBLANES)
    norm1, norm2, normf = row(w["norm1_w"]), row(w["norm2_w"]), row(w["norm_f_w"])
    vec = {n: row(w[n]) for n in VEC}
    head_of = jnp.arange(LANES) // HEAD
    bd = (head_of[:, None] == head_of[None, :]).astype(F32)
    pre_consts = [vec["k_k"], vec["w0_f"], vec["w0_b"], vec["a0_f"], vec["a0_b"], vec["k_a_f"], vec["k_a_b"],
                  wupf, wupb, aupf, aupb, gup, bd]
    post_consts = [vec["gn_w"], vec["gn_b"], vec["r_k_f"], vec["r_k_b"], bd]

    h1, = _rowwise(_rms, [x2d], [norm1], [D_MODEL], [], name="rms1_fwd", out_dtype=BF16, tb=WIDE_TILE)
    p = _mm(h1, w_in, name="mm_in")
    pss, oconv = _shift_conv_fwd(p, mu, conv_w, seq, name="shift_conv_fwd")
    pre_rows = [(pss, 0, 512), (pss, 1, 512), (pss, XW0 // 128, 128), (pss, XA0 // 128, 128), (pss, XG0 // 256, 256)]
    sc, g = _rowwise(_prescan_math, pre_rows, pre_consts, [[D_RWKV] * N_GROUP, D_RWKV], [], name="prescan_fwd")
    xall = _to_key_rows(sc, bsz, seq)
    v_l = _to_value_rows(pss[:, 2 * D_RWKV:3 * D_RWKV], bsz, seq)
    y_f, y_b, hist, fin, sa, *gathered = _scan_fwd(xall, v_l, gather=[late[n] for n in LATE] if late else (),
                                                   name="scan_fwd")
    w_out, w_gate, w_up, w_down = (
        (_from_slots(a, SHARD_AXIS[n]) if late else w[n])[0] for n, a in zip(LATE, gathered or LATE))
    y = _from_value_rows(y_f + y_b, bsz, seq)
    post_rows = [y, (pss, 0, 512), (pss, 2, 512), (sc, G_KD[0], 512), (sc, G_KD[1], 512), g]

    def post_fwd(y_, r_, v_, kdf_, kdb_, g_, oc_, *consts):
        return _postscan_math(y_, r_, v_, kdf_, kdb_, g_, *consts), oc_

    o, = _rowwise(post_fwd, post_rows + [oconv], post_consts, [[D_RWKV, D_CONV]], [], name="postscan_fwd",
                  out_dtype=BF16)
    x1 = _mm(o, w_out, add=x2d, name="mm_out")
    h2, = _rowwise(_rms, [x1], [norm2], [D_MODEL], [], name="rms2_fwd", out_dtype=BF16, tb=WIDE_TILE)
    gg, uu, ff = _mm_swiglu(h2, w_gate, w_up, name="mm_gate_up")
    x2 = _mm(ff, w_down, add=x1, name="mm_down")

    def final(x_, tg_, wn_):
        yo, vjp = jax.vjp(_rms, x_, wn_)
        err = yo - tg_
        dx_, dwn_ = vjp(err * (1.0 / D_MODEL))
        part = jnp.sum(jnp.sum(err * err, axis=1, keepdims=True), axis=0, keepdims=True) * (0.5 / D_MODEL)
        return dx_, part + jnp.zeros((1, LANES), F32), dwn_

    dx2, loss_acc, d_normf = _rowwise(final, [x2, tg2d], [normf], [D_MODEL], [(1, LANES), (1, D_MODEL)],
                                      name="loss_head", tb=WIDE_TILE)
    dgg, duu = _mm_swiglu_bwd(dx2, w_down, gg, uu, name="mm_down_dx")
    g_w_down = _mm(ff, dx2, ta=True, name="mm_down_dw")
    dh2 = _mm(dgg, w_gate, tb=True, name="mm_gate_dx")
    dh2 = _mm(duu, w_up, tb=True, add=dh2, name="mm_up_dx")
    g_w_gate = _mm(h2, dgg, ta=True, name="mm_gate_dw")
    g_w_up = _mm(h2, duu, ta=True, name="mm_up_dw")

    def rms_bwd(x_, dh_, dres_, wn_):
        _, vjp = jax.vjp(_rms, x_, wn_)
        dx_, dwn_ = vjp(dh_)
        return dx_ + dres_, dwn_

    dx1, d_norm2 = _rowwise(rms_bwd, [x1, dh2, dx2], [norm2], [D_MODEL], [(1, D_MODEL)], name="rms2_bwd", tb=WIDE_TILE)
    do = _mm(dx1, w_out, tb=True, name="mm_out_dx")
    g_w_out = _mm(o, dx1, ta=True, name="mm_out_dw")

    def post_bwd(y_, r_, v_, kdf_, kdb_, g_, do_, *consts):
        _, vjp = jax.vjp(lambda *a: _postscan_math(*a, consts[4]), y_, r_, v_, kdf_, kdb_, g_, *consts[:4])
        return vjp(do_)

    (dy, dr_c, dv_c, dkdf_c, dkdb_c, dg, d_gn_w, d_gn_b, d_rkf, d_rkb) = _rowwise(
        post_bwd, post_rows + [(do, 0, 512)], post_consts, [D_RWKV] * 6, [(1, D_RWKV)] * 4, name="postscan_bwd")
    dy_l = _to_value_rows(dy, bsz, seq)
    late_grads = {"w_out": g_w_out[None], "w_gate": g_w_gate[None], "w_up": g_w_up[None], "w_down": g_w_down[None]}
    g_f, g_b, dv_f, dv_b, *late_parts = _scan_bwd(
        xall, v_l, dy_l, hist, fin, sa, name="scan_bwd",
        exchange=[_to_slots(late_grads[n], SHARD_AXIS[n]).astype(BF16) for n in LATE] if late else ())
    dsc = _from_key_rows(g_f + g_b, bsz, seq)
    dv_s = _from_value_rows(dv_f + dv_b, bsz, seq)

    def pre_bwd(r_, k_, xw_, xa_, xg_, dkk_, dr_s, dwf_, dwb_, dbf_, dbb_, dkdf_s, dkdb_s,
                dr_c_, dv_c_, dv_s_, dkdf_c_, dkdb_c_, dg_, *consts):
        _, vjp = jax.vjp(lambda *a: _prescan_math(*a, consts[-1]), r_, k_, xw_, xa_, xg_, *consts[:-1])
        grads = vjp((dkk_, dr_s + dr_c_, dwf_, dwb_, dbf_, dbb_, dkdf_s + dkdf_c_, dkdb_s + dkdb_c_, dg_))
        dr_, dk_, dxw_, dxa_, dxg_ = grads[:5]
        return (dr_, dk_, dv_c_ + dv_s_, dxw_, dxa_, dxg_) + tuple(grads[5:])

    pre_b_rows = (pre_rows + [(dsc, j, 512) for j in range(N_GROUP)]
                  + [dr_c, dv_c, dv_s, dkdf_c, dkdb_c, dg])
    pre_b = _rowwise(pre_bwd, pre_b_rows, pre_consts, [[512, 512, 512, 128, 128, 256]],
                     [(1, D_RWKV)] * 7 + [(128, D_RWKV)] * 4 + [(256, D_RWKV)], name="prescan_bwd")
    d_pss = pre_b[0]
    d_kk_, d_w0f, d_w0b, d_a0f, d_a0b, d_kaf, d_kab, d_wupf, d_wupb, d_aupf, d_aupb, d_gup = pre_b[1:]
    dp, d_mu, d_conv = _shift_conv_bwd(p, d_pss, do, mu, conv_w, seq, name="shift_conv_bwd")
    dh1 = _mm(dp, w_in, tb=True, name="mm_in_dx")
    g_w_in = _mm(h1, dp, ta=True, name="mm_in_dw")
    dx, d_norm1 = _rowwise(rms_bwd, [x2d, dh1, dx1], [norm1], [D_MODEL], [(1, D_MODEL)], name="rms1_bwd", tb=WIDE_TILE)

    grads = {
        "norm1_w": d_norm1, "w_in": _unpad_cols(g_w_in, P_SEGS)[None], "mu_shift": _unpad_cols(d_mu, S_SEGS),
        "w_up_f": d_wupf[None, :D_LORA], "w0_f": d_w0f, "w_up_b": d_wupb[None, :D_LORA], "w0_b": d_w0b,
        "a_up_f": d_aupf[None, :D_LORA], "a0_f": d_a0f, "a_up_b": d_aupb[None, :D_LORA], "a0_b": d_a0b,
        "g_up": d_gup[None, :D_GATE], "k_k": d_kk_, "k_a_f": d_kaf, "k_a_b": d_kab,
        "r_k_f": d_rkf, "r_k_b": d_rkb, "gn_w": d_gn_w, "gn_b": d_gn_b, "conv_w": d_conv[None, :3],
        "w_out": g_w_out[None], "norm2_w": d_norm2, "w_gate": g_w_gate[None], "w_up": g_w_up[None],
        "w_down": g_w_down[None], "norm_f_w": d_normf,
    }
    return loss_acc[0, 0], dx.reshape(bsz, seq, D_MODEL), grads, dict(zip(LATE, late_parts))


def _hbm_specs(n):
    return [pl.BlockSpec(memory_space=pl.ANY)] * n


def _all_gather(arrs, *, name):
    n = len(arrs)

    def body(*refs):
        x_refs, out_refs = refs[:n], refs[n:2 * n]
        send_sems, recv_sems, local_sems = refs[2 * n:]
        x, y, c = lax.axis_index("x"), lax.axis_index("y"), lax.axis_index("c")
        me, sibling = (x, y, c), (x, y, 1 - c)
        chips = [(1 - x, y), (x, 1 - y), (1 - x, 1 - y)]

        def slot(a, px, py, pc):
            return out_refs[a].at[4 * px + 2 * py + pc]

        def copy(a, k, block, to, src=None):
            return pltpu.make_async_remote_copy(
                src_ref=slot(a, *block) if src is None else src, dst_ref=slot(a, *block),
                send_sem=send_sems.at[k, a], recv_sem=recv_sems.at[k, a],
                device_id=to, device_id_type=pl.DeviceIdType.MESH)

        mine = [pltpu.make_async_copy(x_refs[a], slot(a, *me), local_sems.at[a]) for a in range(n)]
        for cp in mine:
            cp.start()
        first = []
        for a in range(n):
            first.append(copy(a, 0, me, sibling, src=x_refs[a]))
            first += [copy(a, 1 + j, me, (*chip, c), src=x_refs[a]) for j, chip in enumerate(chips)]
        for cp in first:
            cp.start()
        passed = []
        for j, chip in enumerate(chips):
            for a in range(n):
                copy(a, 1 + j, (*chip, c), me).wait_recv()
                cp = copy(a, 4 + j, (*chip, c), sibling)
                cp.start()
                passed.append(cp)
        for a in range(n):
            copy(a, 0, sibling, me).wait_recv()
            for j, chip in enumerate(chips):
                copy(a, 4 + j, (*chip, 1 - c), me).wait_recv()
        for cp in first + passed:
            cp.wait_send()
        for cp in mine:
            cp.wait()

    return pl.pallas_call(
        body, out_shape=[jax.ShapeDtypeStruct((N_DEV,) + a.shape, a.dtype) for a in arrs],
        in_specs=_hbm_specs(n), out_specs=_hbm_specs(n),
        scratch_shapes=[pltpu.SemaphoreType.DMA((7, n)), pltpu.SemaphoreType.DMA((7, n)),
                        pltpu.SemaphoreType.DMA((n,))],
        name=name)(*arrs)


def _exchange(sliced, whole, *, name):
    arrs = list(sliced) + list(whole)
    n, n_sliced = len(arrs), len(sliced)

    def body(*refs):
        copies = _exchange_copies(refs[:n], refs[n:2 * n], n_sliced, *refs[2 * n:])
        for cp in copies:
            cp.start()
        for cp in copies:
            cp.wait()

    return pl.pallas_call(
        body, out_shape=_exchange_out_shapes(arrs, n_sliced), in_specs=_hbm_specs(n), out_specs=_hbm_specs(n),
        scratch_shapes=_exchange_sems(n), name=name)(*arrs)


def _exchange_out_shapes(arrs, n_sliced):
    return [jax.ShapeDtypeStruct(a.shape if i < n_sliced else (N_DEV,) + a.shape, a.dtype)
            for i, a in enumerate(arrs)]


def _exchange_sems(n):
    return [pltpu.SemaphoreType.DMA((7, n)), pltpu.SemaphoreType.DMA((7, n)), pltpu.SemaphoreType.DMA((n,))]


def _exchange_copies(in_refs, out_refs, n_sliced, send_sems, recv_sems, local_sems):
    n = len(in_refs)
    x, y, c = lax.axis_index("x"), lax.axis_index("y"), lax.axis_index("c")
    me = 4 * x + 2 * y + c

    def src(a, dev):
        return in_refs[a].at[dev] if a < n_sliced else in_refs[a]

    copies = [pltpu.make_async_copy(src(a, me), out_refs[a].at[me], local_sems.at[a]) for a in range(n)]
    for k in range(1, N_DEV):
        px = 1 - x if k & 4 else x
        py = 1 - y if k & 2 else y
        pc = 1 - c if k & 1 else c
        for a in range(n):
            copies.append(pltpu.make_async_remote_copy(
                src_ref=src(a, 4 * px + 2 * py + pc), dst_ref=out_refs[a].at[me],
                send_sem=send_sems.at[k - 1, a], recv_sem=recv_sems.at[k - 1, a],
                device_id=(px, py, pc), device_id_type=pl.DeviceIdType.MESH))
    return copies


def _adam_math(g, w, m, v):
    nm = ADAM_B1 * m + (1.0 - ADAM_B1) * g
    nv = ADAM_B2 * v + (1.0 - ADAM_B2) * (g * g)
    m_hat = nm / (1.0 - ADAM_B1 ** ADAM_STEP)
    v_hat = nv / (1.0 - ADAM_B2 ** ADAM_STEP)
    return -ADAM_LR * (m_hat / (jnp.sqrt(v_hat) + ADAM_EPS) + ADAM_WD * w), nm, nv


def _slot_sum(ref):
    g = ref[0].astype(F32)
    for s in range(1, N_DEV):
        g = g + ref[s].astype(F32)
    return g


def _adamw_big(parts, w, m, v, *, name):
    _, rws, cols = w.shape
    tr = _tile(rws, (256, 176, 128))

    def kern(p_ref, w_ref, m_ref, v_ref, g_ref, d_ref, nm_ref, nv_ref):
        g = _slot_sum(p_ref)
        g_ref[...] = g
        d_ref[...], nm_ref[...], nv_ref[...] = _adam_math(g, w_ref[...], m_ref[...], v_ref[...])

    spec = pl.BlockSpec((1, tr, cols), lambda i: (0, i, 0))
    return pl.pallas_call(
        kern, out_shape=[jax.ShapeDtypeStruct(w.shape, F32)] * 4, grid=(rws // tr,),
        in_specs=[pl.BlockSpec((N_DEV, 1, tr, cols), lambda i: (0, 0, i, 0)), spec, spec, spec],
        out_specs=[spec] * 4, compiler_params=_params(("parallel",)), name=name)(parts, w, m, v)


def _adamw_small(lora_parts, vec_parts, wide_parts, wmv, *, name):
    names = LORA + VEC + WIDE
    n_l, n = len(LORA), len(names)
    flat = [a for trip in wmv for a in trip]

    def kern(*refs):
        l_refs, vec_ref, wide_ref = refs[:n_l], refs[n_l], refs[n_l + 1]
        in_refs = refs[n_l + 2:n_l + 2 + 3 * n]
        out_refs = refs[n_l + 2 + 3 * n:]
        vec_sum, wide_sum = _slot_sum(vec_ref), _slot_sum(wide_ref)
        for i, nm in enumerate(names):
            w_ref, m_ref, v_ref = in_refs[3 * i:3 * i + 3]
            if i < n_l:
                g = _slot_sum(l_refs[i])
            elif nm in VEC:
                g = vec_sum[i - n_l:i - n_l + 1, :]
            else:
                g = wide_sum[WIDE.index(nm):WIDE.index(nm) + 1, :w_ref.shape[-1]]
            o = out_refs[4 * i:4 * i + 4]
            o[0][...] = g
            o[1][...], o[2][...], o[3][...] = _adam_math(g, w_ref[...], m_ref[...], v_ref[...])

    out_shape = [jax.ShapeDtypeStruct(trip[0].shape, F32) for trip in wmv for _ in range(4)]
    outs = pl.pallas_call(kern, out_shape=out_shape, name=name,
                          compiler_params=pltpu.CompilerParams(vmem_limit_bytes=VMEM_LIMIT))(
        *lora_parts, vec_parts, wide_parts, *flat)
    return [tuple(outs[4 * i:4 * i + 4]) for i in range(n)]


def _to_slots(g, axis):
    _, rws, cols = g.shape
    if axis == 1:
        return g.reshape(N_DEV, 1, rws // N_DEV, cols)
    return g.reshape(1, rws, N_DEV, cols // N_DEV).transpose(2, 0, 1, 3)


def _from_slots(got, axis):
    _, _, rws, cols = got.shape
    if axis == 1:
        return got.reshape(1, N_DEV * rws, cols)
    return got.transpose(1, 2, 0, 3).reshape(1, rws, N_DEV * cols)


def _pad_lanes(a, width):
    return jnp.concatenate([a, jnp.zeros(a.shape[:-1] + (width - a.shape[-1],), a.dtype)], axis=-1)


def kernel(x, norm1_w, w_in, mu_shift, w_up_f, w0_f, w_up_b, w0_b, a_up_f, a0_f, a_up_b, a0_b, g_up, k_k, k_a_f, k_a_b, r_k_f, r_k_b, gn_w, gn_b, conv_w, w_out, norm2_w, w_gate, w_up, w_down, norm_f_w, loss_target, m_norm1_w, m_w_in, m_mu_shift, m_w_up_f, m_w0_f, m_w_up_b, m_w0_b, m_a_up_f, m_a0_f, m_a_up_b, m_a0_b, m_g_up, m_k_k, m_k_a_f, m_k_a_b, m_r_k_f, m_r_k_b, m_gn_w, m_gn_b, m_conv_w, m_w_out, m_norm2_w, m_w_gate, m_w_up, m_w_down, m_norm_f_w, v_norm1_w, v_w_in, v_mu_shift, v_w_up_f, v_w0_f, v_w_up_b, v_w0_b, v_a_up_f, v_a0_f, v_a_up_b, v_a0_b, v_g_up, v_k_k, v_k_a_f, v_k_a_b, v_r_k_f, v_r_k_b, v_gn_w, v_gn_b, v_conv_w, v_w_out, v_norm2_w, v_w_gate, v_w_up, v_w_down, v_norm_f_w):
    local = dict(norm1_w=norm1_w, w_in=w_in, mu_shift=mu_shift, w_up_f=w_up_f, w0_f=w0_f, w_up_b=w_up_b,
                 w0_b=w0_b, a_up_f=a_up_f, a0_f=a0_f, a_up_b=a_up_b, a0_b=a0_b, g_up=g_up, k_k=k_k, k_a_f=k_a_f,
                 k_a_b=k_a_b, r_k_f=r_k_f, r_k_b=r_k_b, gn_w=gn_w, gn_b=gn_b, conv_w=conv_w, w_out=w_out,
                 norm2_w=norm2_w, w_gate=w_gate, w_up=w_up, w_down=w_down, norm_f_w=norm_f_w)
    mom_m = dict(norm1_w=m_norm1_w, w_in=m_w_in, mu_shift=m_mu_shift, w_up_f=m_w_up_f, w0_f=m_w0_f,
                 w_up_b=m_w_up_b, w0_b=m_w0_b, a_up_f=m_a_up_f, a0_f=m_a0_f, a_up_b=m_a_up_b, a0_b=m_a0_b,
                 g_up=m_g_up, k_k=m_k_k, k_a_f=m_k_a_f, k_a_b=m_k_a_b, r_k_f=m_r_k_f, r_k_b=m_r_k_b,
                 gn_w=m_gn_w, gn_b=m_gn_b, conv_w=m_conv_w, w_out=m_w_out, norm2_w=m_norm2_w, w_gate=m_w_gate,
                 w_up=m_w_up, w_down=m_w_down, norm_f_w=m_norm_f_w)
    mom_v = dict(norm1_w=v_norm1_w, w_in=v_w_in, mu_shift=v_mu_shift, w_up_f=v_w_up_f, w0_f=v_w0_f,
                 w_up_b=v_w_up_b, w0_b=v_w0_b, a_up_f=v_a_up_f, a0_f=v_a0_f, a_up_b=v_a_up_b, a0_b=v_a0_b,
                 g_up=v_g_up, k_k=v_k_k, k_a_f=v_k_a_f, k_a_b=v_k_a_b, r_k_f=v_r_k_f, r_k_b=v_r_k_b,
                 gn_w=v_gn_w, gn_b=v_gn_b, conv_w=v_conv_w, w_out=v_w_out, norm2_w=v_norm2_w, w_gate=v_w_gate,
                 w_up=v_w_up, w_down=v_w_down, norm_f_w=v_norm_f_w)

    early = ("w_in",) + LORA
    got = _all_gather([local["w_in"].astype(BF16)] + [local[n] for n in LORA], name="gather")
    full = dict(local)
    full.update({n: _from_slots(a, SHARD_AXIS[n]) for n, a in zip(early, got)})

    loss_part, grad_x, grads, late_parts = _local_step(x, loss_target, full,
                                                       late={n: local[n].astype(BF16) for n in LATE})
    loss = lax.psum(loss_part, ("x", "y", "c"))

    vec_rows = jnp.concatenate([grads[n] for n in VEC] + [jnp.zeros((16 - len(VEC), D_RWKV), F32)], axis=0)
    wide_rows = jnp.concatenate([_pad_lanes(grads[n], WIDE_ROW) for n in WIDE]
                                + [jnp.zeros((SUBLANES - len(WIDE), WIDE_ROW), F32)], axis=0)
    slots = [_to_slots(grads[n], SHARD_AXIS[n]).astype(BF16 if n in BIG else F32) for n in early]
    recv = _exchange(slots, [vec_rows, wide_rows], name="grad_exchange")
    out = {}
    for n in BIG:
        parts = recv[0] if n == "w_in" else late_parts[n]
        out[n] = _adamw_big(parts, local[n], mom_m[n], mom_v[n], name="adamw_" + n)

    def small_form(n, a):
        if n in LORA:
            return a
        a = a.reshape(1, -1)
        return _pad_lanes(a, WIDE_ROW) if n == "mu_shift" else a

    small = LORA + VEC + WIDE
    res = _adamw_small(recv[1:len(early)], recv[len(early)], recv[len(early) + 1],
                       [tuple(small_form(n, d[n]) for d in (local, mom_m, mom_v)) for n in small],
                       name="adamw_small")
    for n, quad in zip(small, res):
        out[n] = tuple(a[..., :local[n].size].reshape(local[n].shape) if n not in LORA else a for a in quad)
    return (loss, grad_x, *[out[n][i] for i in range(4) for n in WEIGHTS])
```

```python
import functools

import jax
import jax.numpy as jnp
from jax import lax
from jax.experimental import pallas as pl
from jax.experimental.pallas import tpu as pltpu

F32 = jnp.float32
BF16 = jnp.bfloat16
HIGHEST = lax.Precision.HIGHEST

N_DEV = 8
D_MODEL = 1024
D_RWKV = 512
D_CONV = 512
HEAD = 64
N_HEAD = D_RWKV // HEAD
D_LORA = 64
D_GATE = 160
D_FF = 2816
D_SHIFTED = 3 * D_RWKV + 2 * D_LORA + D_GATE
D_IN = D_SHIFTED + 3 * D_CONV
XW0, XA0, XG0 = 1536, 1664, 1792
D_SP = 2048
D_INP = D_SP + 3 * D_CONV
LOG_DECAY_SCALE = 0.606531
RMS_EPS = 1e-6
GN_EPS = 64e-5
NORM_EPS = 1e-12
ADAM_LR, ADAM_B1, ADAM_B2, ADAM_EPS, ADAM_WD, ADAM_STEP = 0.001, 0.9, 0.999, 1e-08, 0.01, 10

LANES = 128
SUBLANES = 8
VMEM_LIMIT = 48 * 1024 * 1024
SCAN_CHUNK = 16
SCAN_UNROLL = 3
ROW_TILE = 128
WIDE_TILE = 256

BIG = ("w_in", "w_out", "w_gate", "w_up", "w_down")
LORA = ("w_up_f", "w_up_b", "a_up_f", "a_up_b", "g_up", "conv_w")
SHARD_AXIS = {"w_in": 2, "w_out": 1, "w_gate": 2, "w_up": 2, "w_down": 1, "w_up_f": 2, "w_up_b": 2,
              "a_up_f": 2, "a_up_b": 2, "g_up": 2, "conv_w": 2}
VEC = ("w0_f", "w0_b", "a0_f", "a0_b", "k_k", "k_a_f", "k_a_b", "r_k_f", "r_k_b", "gn_w", "gn_b")
WIDE = ("mu_shift", "norm1_w", "norm2_w", "norm_f_w")
WIDE_ROW = 2048
WEIGHTS = ("norm1_w", "w_in", "mu_shift", "w_up_f", "w0_f", "w_up_b", "w0_b", "a_up_f", "a0_f", "a_up_b",
           "a0_b", "g_up", "k_k", "k_a_f", "k_a_b", "r_k_f", "r_k_b", "gn_w", "gn_b", "conv_w", "w_out",
           "norm2_w", "w_gate", "w_up", "w_down", "norm_f_w")


def _params(sem, limit=VMEM_LIMIT):
    return pltpu.CompilerParams(dimension_semantics=sem, vmem_limit_bytes=limit)


def _tile(n, cands):
    for c in cands:
        if n % c == 0:
            return c
    raise ValueError(f"no tile for {n}")


def _mm(a, b, *, ta=False, tb=False, add=None, exchange=None, name):
    (k_dim, m) = a.shape if ta else a.shape[::-1]
    (k2, n) = b.shape[::-1] if tb else b.shape
    assert k_dim == k2, (a.shape, b.shape, ta, tb)
    tm = _tile(m, (1408, 1024, 512, 256, 128))
    tn = _tile(n, (1408, 1024, 896, 512, 256, 128))
    tk = k_dim if k_dim <= 1024 else _tile(k_dim, (1408, 896, 512, 256, 128))
    nk = k_dim // tk
    grid = (m // tm, n // tn, nk)
    dims = (((0 if ta else 1,), (1 if tb else 0,)), ((), ()))
    sliced, whole = exchange or ((), ())
    riders = list(sliced) + list(whole)
    n_x, n_in = len(riders), 2 + (add is not None)

    def kern(*refs):
        a_ref, b_ref = refs[:2]
        add_ref = refs[2] if add is not None else None
        o_ref, acc_ref = refs[n_in + n_x], refs[n_in + 2 * n_x + 1]
        k = pl.program_id(2)
        step = (pl.program_id(0) * grid[1] + pl.program_id(1)) * nk + k

        def copies():
            return _exchange_copies(refs[n_in:n_in + n_x], refs[n_in + n_x + 1:n_in + 2 * n_x + 1], len(sliced),
                                    *refs[n_in + 2 * n_x + 2:])

        if n_x:
            @pl.when(step == 0)
            def _():
                for cp in copies():
                    cp.start()

        @pl.when(k == 0)
        def _():
            acc_ref[...] = jnp.zeros_like(acc_ref)

        acc_ref[...] += lax.dot_general(a_ref[...].astype(BF16), b_ref[...].astype(BF16), dims,
                                        preferred_element_type=F32)

        @pl.when(k == nk - 1)
        def _():
            if add is None:
                o_ref[...] = acc_ref[...]
            else:
                o_ref[...] = acc_ref[...] + add_ref[...]

        if n_x:
            @pl.when(step == grid[0] * grid[1] * nk - 1)
            def _():
                for cp in copies():
                    cp.wait()

    a_spec = (pl.BlockSpec((tk, tm), lambda i, j, k: (k, i)) if ta
              else pl.BlockSpec((tm, tk), lambda i, j, k: (i, k)))
    b_spec = (pl.BlockSpec((tn, tk), lambda i, j, k: (j, k)) if tb
              else pl.BlockSpec((tk, tn), lambda i, j, k: (k, j)))
    o_spec = pl.BlockSpec((tm, tn), lambda i, j, k: (i, j))
    in_specs = [a_spec, b_spec] + ([o_spec] if add is not None else []) + _hbm_specs(n_x)
    args = (a, b) + ((add,) if add is not None else ()) + tuple(riders)
    out = pl.pallas_call(
        kern, out_shape=[jax.ShapeDtypeStruct((m, n), F32)] + _exchange_out_shapes(riders, len(sliced)), grid=grid,
        in_specs=in_specs, out_specs=[o_spec] + _hbm_specs(n_x),
        scratch_shapes=[pltpu.VMEM((tm, tn), F32)] + (_exchange_sems(n_x) if n_x else []),
        compiler_params=_params(("arbitrary",) * 3 if n_x else ("parallel", "parallel", "arbitrary")),
        name=name)(*args)
    return out if n_x else out[0]


def _swiglu(g, u):
    return jax.nn.silu(g) * u


FFN_TN = 256


def _mm_swiglu(h, w_gate, w_up, *, name):
    m, k_dim = h.shape
    n = w_gate.shape[1]
    tm = _tile(m, (1024, 512, 256, 128))

    def kern(h_ref, wg_ref, wu_ref, g_ref, u_ref, f_ref):
        hv = h_ref[...].astype(BF16)
        g = jnp.dot(hv, wg_ref[...].astype(BF16), preferred_element_type=F32)
        u = jnp.dot(hv, wu_ref[...].astype(BF16), preferred_element_type=F32)
        g_ref[...] = g
        u_ref[...] = u
        f_ref[...] = _swiglu(g, u).astype(f_ref.dtype)

    w_spec = pl.BlockSpec((k_dim, FFN_TN), lambda i, j: (0, j))
    o_spec = pl.BlockSpec((tm, FFN_TN), lambda i, j: (i, j))
    return pl.pallas_call(
        kern, out_shape=[jax.ShapeDtypeStruct((m, n), F32)] * 2 + [jax.ShapeDtypeStruct((m, n), BF16)],
        grid=(m // tm, n // FFN_TN), in_specs=[pl.BlockSpec((tm, k_dim), lambda i, j: (i, 0)), w_spec, w_spec],
        out_specs=[o_spec] * 3, compiler_params=_params(("parallel", "parallel")), name=name)(h, w_gate, w_up)


def _mm_swiglu_bwd(dx, w_down, g, u, *, name):
    m, k_dim = dx.shape
    n = w_down.shape[0]
    tm = _tile(m, (1024, 512, 256, 128))

    def kern(dx_ref, w_ref, g_ref, u_ref, dg_ref, du_ref):
        df = lax.dot_general(dx_ref[...].astype(BF16), w_ref[...].astype(BF16), (((1,), (1,)), ((), ())),
                             preferred_element_type=F32)
        _, vjp = jax.vjp(_swiglu, g_ref[...], u_ref[...])
        dg, du = vjp(df)
        dg_ref[...] = dg.astype(dg_ref.dtype)
        du_ref[...] = du.astype(du_ref.dtype)

    o_spec = pl.BlockSpec((tm, FFN_TN), lambda i, j: (i, j))
    return pl.pallas_call(
        kern, out_shape=[jax.ShapeDtypeStruct((m, n), BF16)] * 2, grid=(m // tm, n // FFN_TN),
        in_specs=[pl.BlockSpec((tm, k_dim), lambda i, j: (i, 0)), pl.BlockSpec((FFN_TN, k_dim), lambda i, j: (j, 0)),
                  o_spec, o_spec],
        out_specs=[o_spec] * 2, compiler_params=_params(("parallel", "parallel")), name=name)(dx, w_down, g, u)


def _rowwise(fn, rows, consts, out_rows, out_accs, *, name, tb=ROW_TILE, out_dtype=F32):
    t = (rows[0][0] if isinstance(rows[0], tuple) else rows[0]).shape[0]
    n_r, n_c, n_o, n_a = len(rows), len(consts), len(out_rows), len(out_accs)
    pieces = [w if isinstance(w, (list, tuple)) else [w] for w in out_rows]

    def kern(*refs):
        r_refs = refs[:n_r]
        c_refs = refs[n_r:n_r + n_c]
        o_refs = refs[n_r + n_c:n_r + n_c + n_o]
        a_refs = refs[n_r + n_c + n_o:]
        vals = fn(*[r[...] for r in r_refs], *[c[...] for c in c_refs])
        vals = list(vals) if isinstance(vals, (tuple, list)) else [vals]
        pos = 0
        for o_ref, ws in zip(o_refs, pieces):
            off = 0
            for w in ws:
                o_ref[:, off:off + w] = vals[pos].astype(o_ref.dtype)
                off += w
                pos += 1
        if n_a:
            @pl.when(pl.program_id(0) == 0)
            def _():
                for a_ref in a_refs:
                    a_ref[...] = jnp.zeros_like(a_ref)
            for a_ref, v in zip(a_refs, vals[pos:]):
                a_ref[...] += v

    in_specs, args = [], []
    for r in rows:
        if isinstance(r, tuple):
            arr, blk, w = r
            in_specs.append(pl.BlockSpec((tb, w), functools.partial(lambda i, blk: (i, blk), blk=blk)))
        else:
            arr = r
            in_specs.append(pl.BlockSpec((tb, arr.shape[1]), lambda i: (i, 0)))
        args.append(arr)
    for c in consts:
        in_specs.append(pl.BlockSpec(c.shape, lambda i: (0, 0)))
        args.append(c)
    out_shape = [jax.ShapeDtypeStruct((t, sum(ws)), out_dtype) for ws in pieces]
    out_specs = [pl.BlockSpec((tb, sum(ws)), lambda i: (i, 0)) for ws in pieces]
    for shp in out_accs:
        out_shape.append(jax.ShapeDtypeStruct(shp, F32))
        out_specs.append(pl.BlockSpec(shp, lambda i: (0, 0)))
    res = pl.pallas_call(
        kern, out_shape=out_shape, grid=(t // tb,), in_specs=in_specs, out_specs=out_specs,
        compiler_params=_params(("arbitrary",) if n_a else ("parallel",)), name=name)(*args)
    return res


def _rms(x, w):
    return x * lax.rsqrt(jnp.mean(x * x, axis=-1, keepdims=True) + RMS_EPS) * w


def _seg_sum(x, bd):
    return jnp.concatenate(
        [jnp.dot(x[:, LANES * j:LANES * (j + 1)], bd, precision=HIGHEST, preferred_element_type=F32)
         for j in range(x.shape[1] // LANES)], axis=1)


@jax.custom_vjp
def _seg(x, bd):
    return _seg_sum(x, bd)


_seg.defvjp(lambda x, bd: (_seg_sum(x, bd), bd), lambda bd, ct: (_seg_sum(ct, bd), jnp.zeros_like(bd)))


def _colsum(x):
    return jnp.sum(x, axis=0, keepdims=True)


def _prescan_math(r, k, xw, xa, xg, k_k, w0f, w0b, a0f, a0b, kaf, kab, wupf, wupb, aupf, aupb, gup, bd):
    kkr = k * k_k
    norm = jnp.sqrt(_seg(kkr * kkr, bd))
    kk = kkr / jnp.maximum(norm, NORM_EPS)
    th = jnp.tanh(xw)

    def direction(w0, wup, a0, aup, ka):
        logit = w0 + jnp.dot(th, wup, preferred_element_type=F32)
        w = jnp.exp(-LOG_DECAY_SCALE * jax.nn.sigmoid(logit))
        a = jax.nn.sigmoid(a0 + jnp.dot(xa, aup, preferred_element_type=F32))
        kd = k * (1.0 + (a - 1.0) * ka)
        return w, kd, kk * a

    wf, kdf, bf = direction(w0f, wupf, a0f, aupf, kaf)
    wb, kdb, bb = direction(w0b, wupb, a0b, aupb, kab)
    g = jnp.dot(jax.nn.sigmoid(xg), gup, preferred_element_type=F32)
    return kk, r, wf, wb, bf, bb, kdf, kdb, g


def _postscan_math(y, r, v, kdf, kdb, g, gn_w, gn_b, rkf, rkb, bd):
    mean = _seg(y, bd) * (1.0 / HEAD)
    yc = y - mean
    var = _seg(yc * yc, bd) * (1.0 / HEAD)
    yg = yc * lax.rsqrt(var + GN_EPS) * gn_w + gn_b
    bonus = (_seg(r * kdf * rkf, bd) + _seg(r * kdb * rkb, bd)) * v
    return (yg + bonus) * g


def _halo_specs(width, col_blk, tb, t):
    nb = t // SUBLANES
    step = tb // SUBLANES
    main = pl.BlockSpec((tb, width), lambda i: (i, col_blk))
    prev = pl.BlockSpec((SUBLANES, width), lambda i: (jnp.maximum(i * step - 1, 0), col_blk))
    nxt = pl.BlockSpec((SUBLANES, width), lambda i: (jnp.minimum((i + 1) * step, nb - 1), col_blk))
    return [main, prev, nxt]


def _neighbours(z, prev8, next8, first, last):
    tb = z.shape[0]
    row = lax.broadcasted_iota(jnp.int32, z.shape, 0)
    prow = jnp.where(first, 0.0, prev8[SUBLANES - 1:SUBLANES, :])
    nrow = jnp.where(last, 0.0, next8[0:1, :])
    down = jnp.where(row == 0, prow, pltpu.roll(z, 1, 0))
    up = jnp.where(row == tb - 1, nrow, pltpu.roll(z, tb - 1, 0))
    return down, up


def _shift_conv_fwd(p, mu, conv_w, seq, *, name, tb=ROW_TILE):
    t = p.shape[0]
    per_seq = seq // tb

    def kern(p_ref, pp_ref, pn_ref, mu_ref, cw_ref, pss_ref, oc_ref):
        i = pl.program_id(0)
        first = (i % per_seq) == 0
        last = (i % per_seq) == per_seq - 1
        ps = p_ref[:, :D_SP]
        down, up = _neighbours(ps, pp_ref[:, :D_SP], pn_ref[:, :D_SP], first, last)
        pss_ref[...] = ps + mu_ref[...] * (0.5 * (down + up) - ps)
        gb = p_ref[:, D_SP:D_SP + D_CONV]
        u = p_ref[:, D_SP + D_CONV:D_SP + 2 * D_CONV] * p_ref[:, D_SP + 2 * D_CONV:]
        u_p = pp_ref[:, D_SP + D_CONV:D_SP + 2 * D_CONV] * pp_ref[:, D_SP + 2 * D_CONV:]
        u_n = pn_ref[:, D_SP + D_CONV:D_SP + 2 * D_CONV] * pn_ref[:, D_SP + 2 * D_CONV:]
        udown, uup = _neighbours(u, u_p, u_n, first, last)
        oc_ref[...] = gb * (cw_ref[0:1, :] * udown + cw_ref[1:2, :] * u + cw_ref[2:3, :] * uup)

    return pl.pallas_call(
        kern,
        out_shape=[jax.ShapeDtypeStruct((t, D_SP), F32), jax.ShapeDtypeStruct((t, D_CONV), F32)],
        grid=(t // tb,),
        in_specs=_halo_specs(D_INP, 0, tb, t) + [pl.BlockSpec((1, D_SP), lambda i: (0, 0)),
                                                 pl.BlockSpec((SUBLANES, D_CONV), lambda i: (0, 0))],
        out_specs=[pl.BlockSpec((tb, D_SP), lambda i: (i, 0)), pl.BlockSpec((tb, D_CONV), lambda i: (i, 0))],
        compiler_params=_params(("parallel",)), name=name)(p, p, p, mu, conv_w)


def _shift_conv_bwd(p, d_pss, d_o, mu, conv_w, seq, *, name, tb=ROW_TILE):
    t = p.shape[0]
    per_seq = seq // tb

    def kern(p_ref, pp_ref, pn_ref, d_ref, dp_ref, dn_ref, do_ref, dop_ref, don_ref, mu_ref, cw_ref,
             out_ref, dmu_ref, dcw_ref):
        i = pl.program_id(0)
        first = (i % per_seq) == 0
        last = (i % per_seq) == per_seq - 1

        @pl.when(i == 0)
        def _():
            dmu_ref[...] = jnp.zeros_like(dmu_ref)
            dcw_ref[...] = jnp.zeros_like(dcw_ref)

        mu_v = mu_ref[...]
        ps = p_ref[:, :D_SP]
        down, up = _neighbours(ps, pp_ref[:, :D_SP], pn_ref[:, :D_SP], first, last)
        d = d_ref[...]
        ddown, dup = _neighbours(d, dp_ref[...], dn_ref[...], first, last)
        out_ref[:, :D_SP] = (d - mu_v * d + 0.5 * (mu_v * ddown + mu_v * dup)).astype(out_ref.dtype)
        dmu_ref[...] += _colsum(d * (0.5 * (down + up) - ps))

        def parts(ref):
            return (ref[:, D_SP:D_SP + D_CONV], ref[:, D_SP + D_CONV:D_SP + 2 * D_CONV],
                    ref[:, D_SP + 2 * D_CONV:])

        gb, gc, hh = parts(p_ref)
        gb_p, gc_p, hh_p = parts(pp_ref)
        gb_n, gc_n, hh_n = parts(pn_ref)
        u = gc * hh
        udown, uup = _neighbours(u, gc_p * hh_p, gc_n * hh_n, first, last)
        cw0, cw1, cw2 = cw_ref[0:1, :], cw_ref[1:2, :], cw_ref[2:3, :]
        do = do_ref[...]
        duc = do * gb
        ducdown, ducup = _neighbours(duc, dop_ref[...] * gb_p, don_ref[...] * gb_n, first, last)
        du = cw0 * ducup + cw1 * duc + cw2 * ducdown
        out_ref[:, D_SP:D_SP + D_CONV] = (do * (cw0 * udown + cw1 * u + cw2 * uup)).astype(out_ref.dtype)
        out_ref[:, D_SP + D_CONV:D_SP + 2 * D_CONV] = (du * hh).astype(out_ref.dtype)
        out_ref[:, D_SP + 2 * D_CONV:] = (du * gc).astype(out_ref.dtype)
        dcw_ref[0:1, :] += _colsum(duc * udown)
        dcw_ref[1:2, :] += _colsum(duc * u)
        dcw_ref[2:3, :] += _colsum(duc * uup)

    return pl.pallas_call(
        kern,
        out_shape=[jax.ShapeDtypeStruct((t, D_INP), BF16), jax.ShapeDtypeStruct((1, D_SP), F32),
                   jax.ShapeDtypeStruct((SUBLANES, D_CONV), F32)],
        grid=(t // tb,),
        in_specs=(_halo_specs(D_INP, 0, tb, t) + _halo_specs(D_SP, 0, tb, t) + _halo_specs(D_CONV, 1, tb, t)
                  + [pl.BlockSpec((1, D_SP), lambda i: (0, 0)),
                     pl.BlockSpec((SUBLANES, D_CONV), lambda i: (0, 0))]),
        out_specs=[pl.BlockSpec((tb, D_INP), lambda i: (i, 0)), pl.BlockSpec((1, D_SP), lambda i: (0, 0)),
                   pl.BlockSpec((SUBLANES, D_CONV), lambda i: (0, 0))],
        compiler_params=_params(("arbitrary",)), name=name)(p, p, p, d_pss, d_pss, d_pss, d_o, d_o, d_o, mu, conv_w)


N_CHAIN = 16
V_LO = LANES // N_CHAIN
V_HI = HEAD // V_LO
N_GROUP = LANES // N_CHAIN
G_KK, G_R, G_W, G_B, G_KD = 0, 1, (2, 3), (4, 5), (6, 7)


def _group(x, j, lane):
    g = pltpu.roll(x, (LANES - N_CHAIN * j) % LANES, 1) if j else x
    g = jnp.where(lane < N_CHAIN, g, pltpu.roll(g, N_CHAIN, 1))
    g = jnp.where(lane < 2 * N_CHAIN, g, pltpu.roll(g, 2 * N_CHAIN, 1))
    return jnp.where(lane < 4 * N_CHAIN, g, pltpu.roll(g, 4 * N_CHAIN, 1))


def _scan_inputs(x, d, lane):
    return [_group(x, j, lane) for j in (G_KK, G_R, G_W[d], G_B[d], G_KD[d])]


def _lane_scan_fwd(xall, v_l, *, name):
    steps = xall.shape[0]
    nc = steps // SCAN_CHUNK
    mirror = lambda c: nc - 1 - c

    def kern(xf_ref, xb_ref, vf_ref, vb_ref, yf_ref, yb_ref, hist_ref, fin_ref, st_ref):
        c = pl.program_id(0)

        @pl.when(c == 0)
        def _():
            st_ref[...] = jnp.zeros_like(st_ref)

        row = lax.broadcasted_iota(jnp.int32, (V_HI, LANES), 0)
        lane = lax.broadcasted_iota(jnp.int32, (HEAD, LANES), 1)

        def step(i, carry):
            j = SCAN_CHUNK - 1 - i
            for d, (x_t, v_t, y_ref, at) in enumerate(((xf_ref[i], vf_ref[i], yf_ref, i),
                                                       (xb_ref[j], vb_ref[j], yb_ref, j))):
                kk_t, r_t, w_t, b_t, kd_t = _scan_inputs(x_t, d, lane)
                y_t = jnp.zeros((V_HI, LANES), F32)
                for vh in range(V_HI):
                    tile = d * V_HI + vh
                    state = st_ref[tile]
                    hist_ref[i, tile] = state
                    sa = _colsum(state * kk_t)
                    state = state * w_t - sa * b_t + v_t[vh:vh + 1, :] * kd_t
                    st_ref[tile] = state
                    y_t = jnp.where(row == vh, _colsum(state * r_t), y_t)
                y_ref[at] = y_t
            return carry

        lax.fori_loop(0, SCAN_CHUNK, step, 0)

        @pl.when(c == nc - 1)
        def _():
            fin_ref[...] = st_ref[...]

    def k_spec(fn):
        return pl.BlockSpec((SCAN_CHUNK, HEAD, LANES), lambda c: (fn(c), 0, 0))

    def v_spec(fn):
        return pl.BlockSpec((SCAN_CHUNK, V_HI, LANES), lambda c: (fn(c), 0, 0))

    same = lambda c: c
    st_shape = (2 * V_HI, HEAD, LANES)
    return pl.pallas_call(
        kern,
        out_shape=[jax.ShapeDtypeStruct((steps, V_HI, LANES), F32)] * 2
        + [jax.ShapeDtypeStruct((steps,) + st_shape, F32), jax.ShapeDtypeStruct(st_shape, F32)],
        grid=(nc,), in_specs=[k_spec(same), k_spec(mirror), v_spec(same), v_spec(mirror)],
        out_specs=[v_spec(same), v_spec(mirror),
                   pl.BlockSpec((SCAN_CHUNK,) + st_shape, lambda c: (c, 0, 0, 0)),
                   pl.BlockSpec(st_shape, lambda c: (0, 0, 0))],
        scratch_shapes=[pltpu.VMEM(st_shape, F32)],
        compiler_params=_params(("arbitrary",)), name=name)(xall, xall, v_l, v_l)


def _lane_scan_bwd(xall, v_l, dy_l, hist, fin, *, name):
    steps = xall.shape[0]
    nc = steps // SCAN_CHUNK
    back = lambda c: nc - 1 - c
    same = lambda c: c

    def kern(xf_ref, xb_ref, vf_ref, vb_ref, dyf_ref, dyb_ref, hist_ref, fin_ref,
             gf_ref, gb_ref, dvf_ref, dvb_ref, ds_ref, after_ref):
        c = pl.program_id(0)

        @pl.when(c == 0)
        def _():
            ds_ref[...] = jnp.zeros_like(ds_ref)
            after_ref[...] = fin_ref[...]

        row = lax.broadcasted_iota(jnp.int32, (V_HI, LANES), 0)
        lane = lax.broadcasted_iota(jnp.int32, (HEAD, LANES), 1)
        grp = lax.shift_right_logical(lane, jnp.full_like(lane, 4))

        def group_sum(x):
            x = x + pltpu.roll(x, 4 * N_CHAIN, 1)
            x = x + pltpu.roll(x, 2 * N_CHAIN, 1)
            return x + pltpu.roll(x, N_CHAIN, 1)

        def step(ii, carry):
            i = SCAN_CHUNK - 1 - ii
            j = ii
            for d, (x_t, v_t, dy_t, g_ref, dv_ref, at) in enumerate((
                    (xf_ref[i], vf_ref[i], dyf_ref[i], gf_ref, dvf_ref, i),
                    (xb_ref[j], vb_ref[j], dyb_ref[j], gb_ref, dvb_ref, j))):
                kk_t, r_t, w_t, b_t, kd_t = _scan_inputs(x_t, d, lane)
                dv_t = jnp.zeros((V_HI, LANES), F32)
                zero = jnp.zeros((HEAD, LANES), F32)
                dkk, dr, dw, db, dkd = zero, zero, zero, zero, zero
                for vh in range(V_HI):
                    tile = d * V_HI + vh
                    before = hist_ref[i, tile]
                    dy_r, v_r = dy_t[vh:vh + 1, :], v_t[vh:vh + 1, :]
                    g = ds_ref[tile] + dy_r * r_t
                    sa = _colsum(before * kk_t)
                    dsa = -_colsum(g * b_t)
                    dv_t = jnp.where(row == vh, _colsum(g * kd_t), dv_t)
                    dr = dr + after_ref[tile] * dy_r
                    dw = dw + g * before
                    dkd = dkd + g * v_r
                    db = db - g * sa
                    dkk = dkk + before * dsa
                    ds_ref[tile] = g * w_t + dsa * kk_t
                    after_ref[tile] = before
                out = jnp.where(grp == G_KK, group_sum(dkk), 0.0)
                out = jnp.where(grp == G_R, group_sum(dr), out)
                out = jnp.where(grp == G_W[d], group_sum(dw), out)
                out = jnp.where(grp == G_B[d], group_sum(db), out)
                out = jnp.where(grp == G_KD[d], group_sum(dkd), out)
                g_ref[at] = out
                dv_ref[at] = dv_t
            return carry

        lax.fori_loop(0, SCAN_CHUNK, step, 0)

    def k_spec(fn):
        return pl.BlockSpec((SCAN_CHUNK, HEAD, LANES), lambda c: (fn(c), 0, 0))

    def v_spec(fn):
        return pl.BlockSpec((SCAN_CHUNK, V_HI, LANES), lambda c: (fn(c), 0, 0))

    st_shape = (2 * V_HI, HEAD, LANES)
    return pl.pallas_call(
        kern,
        out_shape=[jax.ShapeDtypeStruct((steps, HEAD, LANES), F32)] * 2
        + [jax.ShapeDtypeStruct((steps, V_HI, LANES), F32)] * 2,
        grid=(nc,),
        in_specs=[k_spec(back), k_spec(same), v_spec(back), v_spec(same), v_spec(back), v_spec(same),
                  pl.BlockSpec((SCAN_CHUNK,) + st_shape, lambda c: (back(c), 0, 0, 0)),
                  pl.BlockSpec(st_shape, lambda c: (0, 0, 0))],
        out_specs=[k_spec(back), k_spec(same), v_spec(back), v_spec(same)],
        scratch_shapes=[pltpu.VMEM(st_shape, F32), pltpu.VMEM(st_shape, F32)],
        compiler_params=_params(("arbitrary",)), name=name)(xall, xall, v_l, v_l, dy_l, dy_l, hist, fin)


def _to_key_lanes(wide, bsz, seq):
    z = wide.reshape(bsz, seq, N_GROUP, N_HEAD, HEAD).transpose(1, 4, 2, 0, 3)
    return z.reshape(seq, HEAD, LANES)


def _from_key_lanes(g, bsz, seq):
    z = g.reshape(seq, HEAD, N_GROUP, bsz, N_HEAD).transpose(3, 0, 2, 4, 1)
    return z.reshape(bsz * seq, N_GROUP * D_RWKV)


def _to_value_lanes(a, bsz, seq):
    z = a.reshape(bsz, seq, N_HEAD, V_HI, V_LO).transpose(1, 3, 4, 0, 2)
    return z.reshape(seq, V_HI, LANES)


def _from_value_lanes(y, bsz, seq):
    z = y.reshape(seq, V_HI, V_LO, bsz, N_HEAD).transpose(3, 0, 4, 1, 2)
    return z.reshape(bsz * seq, D_RWKV)


K_HI = HEAD // SUBLANES


def _lane_group_sum(x):
    x = x + pltpu.roll(x, 4 * N_CHAIN, 1)
    x = x + pltpu.roll(x, 2 * N_CHAIN, 1)
    return x + pltpu.roll(x, N_CHAIN, 1)


def _key_rows(x_t, d):
    out = []
    for grp in (G_KK, G_R, G_W[d], G_B[d], G_KD[d]):
        blk = x_t[SUBLANES * grp:SUBLANES * (grp + 1), :]
        out.append([jnp.broadcast_to(blk[kh:kh + 1, :], (SUBLANES, LANES)) for kh in range(K_HI)])
    return out


def _tree_sum(terms):
    terms = list(terms)
    while len(terms) > 1:
        terms = [a + b for a, b in zip(terms[::2], terms[1::2])]
    return terms[0]


def _kscan_specs(nc):
    same = lambda c: c
    mirror = lambda c: nc - 1 - c

    def k_spec(fn):
        return pl.BlockSpec((SCAN_CHUNK, HEAD, LANES), lambda c: (fn(c), 0, 0))

    def v_spec(fn):
        return pl.BlockSpec((SCAN_CHUNK, SUBLANES, LANES), lambda c: (fn(c), 0, 0))

    return same, mirror, k_spec, v_spec


ST_SHAPE = (2, K_HI, V_HI, SUBLANES, LANES)


def _lane_group_index():
    lane = lax.broadcasted_iota(jnp.int32, (SUBLANES, LANES), 1)
    return lax.shift_right_logical(lane, jnp.full_like(lane, 4))


def _spread_groups(x, grp):
    rolled = [x] + [pltpu.roll(x, s * N_CHAIN, 1) for s in range(1, N_GROUP)]
    out = []
    for j in range(N_GROUP):
        t = rolled[(0 - j) % N_GROUP]
        for g in range(1, N_GROUP):
            t = jnp.where(grp == g, rolled[(g - j) % N_GROUP], t)
        out.append(t)
    return out


def _gather_groups(tiles, grp):
    total = None
    for s in range(N_GROUP):
        b = tiles[s % N_GROUP]
        for g in range(1, N_GROUP):
            b = jnp.where(grp == g, tiles[(g + s) % N_GROUP], b)
        b = pltpu.roll(b, s * N_CHAIN, 1) if s else b
        total = b if total is None else total + b
    return total


def _lane_group_sum_short(x):
    return _tree_sum([x] + [pltpu.roll(x, k * N_CHAIN, 1) for k in range(1, N_GROUP)])


def _kscan_fwd(xall, v_c, *, name):
    steps = xall.shape[0]
    nc = steps // SCAN_CHUNK
    same, mirror, k_spec, v_spec = _kscan_specs(nc)

    def kern(xf_ref, xb_ref, vf_ref, vb_ref, yf_ref, yb_ref, hist_ref, fin_ref, st_ref):
        c = pl.program_id(0)

        @pl.when(c == 0)
        def _():
            st_ref[...] = jnp.zeros_like(st_ref)

        grp = _lane_group_index()

        def step(i, carry):
            j = SCAN_CHUNK - 1 - i
            for d, (x_t, v_t, y_ref, at) in enumerate(((xf_ref[i], vf_ref[i], yf_ref, i),
                                                       (xb_ref[j], vb_ref[j], yb_ref, j))):
                kk_r, r_r, w_r, b_r, kd_r = _key_rows(x_t, d)
                v_b = _spread_groups(v_t, grp)
                y_p = []
                for vh in range(V_HI):
                    st = [st_ref[d, kh, vh] for kh in range(K_HI)]
                    for kh in range(K_HI):
                        hist_ref[i, d, kh, vh] = st[kh]
                    sa = _lane_group_sum_short(_tree_sum(st[kh] * kk_r[kh] for kh in range(K_HI)))
                    new = [st[kh] * w_r[kh] - sa * b_r[kh] + v_b[vh] * kd_r[kh] for kh in range(K_HI)]
                    for kh in range(K_HI):
                        st_ref[d, kh, vh] = new[kh]
                    y_p.append(_tree_sum(new[kh] * r_r[kh] for kh in range(K_HI)))
                y_ref[at] = _gather_groups(y_p, grp)
            return carry

        lax.fori_loop(0, SCAN_CHUNK, step, 0)

        @pl.when(c == nc - 1)
        def _():
            fin_ref[...] = st_ref[...]

    return pl.pallas_call(
        kern,
        out_shape=[jax.ShapeDtypeStruct((steps, SUBLANES, LANES), F32)] * 2
        + [jax.ShapeDtypeStruct((steps,) + ST_SHAPE, F32), jax.ShapeDtypeStruct(ST_SHAPE, F32)],
        grid=(nc,), in_specs=[k_spec(same), k_spec(mirror), v_spec(same), v_spec(mirror)],
        out_specs=[v_spec(same), v_spec(mirror),
                   pl.BlockSpec((SCAN_CHUNK,) + ST_SHAPE, lambda c: (c, 0, 0, 0, 0, 0)),
                   pl.BlockSpec(ST_SHAPE, lambda c: (0, 0, 0, 0, 0))],
        scratch_shapes=[pltpu.VMEM(ST_SHAPE, F32)],
        compiler_params=_params(("arbitrary",)), name=name)(xall, xall, v_c, v_c)


def _kscan_bwd(xall, v_c, dy_c, hist, fin, *, name):
    steps = xall.shape[0]
    nc = steps // SCAN_CHUNK
    same, back, k_spec, v_spec = _kscan_specs(nc)

    def kern(xf_ref, xb_ref, vf_ref, vb_ref, dyf_ref, dyb_ref, hist_ref, fin_ref,
             gf_ref, gb_ref, dvf_ref, dvb_ref, ds_ref, after_ref):
        c = pl.program_id(0)

        @pl.when(c == 0)
        def _():
            ds_ref[...] = jnp.zeros_like(ds_ref)
            after_ref[...] = fin_ref[...]

        grp = _lane_group_index()
        row = lax.broadcasted_iota(jnp.int32, (SUBLANES, LANES), 0)

        def step(ii, carry):
            i = SCAN_CHUNK - 1 - ii
            j = ii
            for d, (x_t, v_t, dy_t, g_ref, dv_ref, at) in enumerate((
                    (xf_ref[i], vf_ref[i], dyf_ref[i], gf_ref, dvf_ref, i),
                    (xb_ref[j], vb_ref[j], dyb_ref[j], gb_ref, dvb_ref, j))):
                kk_r, r_r, w_r, b_r, kd_r = _key_rows(x_t, d)
                v_s, dy_s = _spread_groups(v_t, grp), _spread_groups(dy_t, grp)
                ks = range(K_HI)
                zero = jnp.zeros((SUBLANES, LANES), F32)
                dkk, dr, dw, db, dkd = ([zero] * K_HI for _ in range(5))
                dv_p = []
                for vh in range(V_HI):
                    v_b, dy_b = v_s[vh], dy_s[vh]
                    before = [hist_ref[i, d, kh, vh] for kh in ks]
                    g = [ds_ref[d, kh, vh] + dy_b * r_r[kh] for kh in ks]
                    dsa = -_lane_group_sum_short(_tree_sum(g[kh] * b_r[kh] for kh in ks))
                    sa = _lane_group_sum(_tree_sum(before[kh] * kk_r[kh] for kh in ks))
                    dv_p.append(_tree_sum(g[kh] * kd_r[kh] for kh in ks))
                    dr = [dr[kh] + after_ref[d, kh, vh] * dy_b for kh in ks]
                    dw = [dw[kh] + g[kh] * before[kh] for kh in ks]
                    dkd = [dkd[kh] + g[kh] * v_b for kh in ks]
                    db = [db[kh] - g[kh] * sa for kh in ks]
                    dkk = [dkk[kh] + before[kh] * dsa for kh in ks]
                    for kh in ks:
                        ds_ref[d, kh, vh] = g[kh] * w_r[kh] + dsa * kk_r[kh]
                        after_ref[d, kh, vh] = before[kh]
                dv_ref[at] = _gather_groups(dv_p, grp)
                blocks = {G_KK: dkk, G_R: dr, G_W[d]: dw, G_B[d]: db, G_KD[d]: dkd}
                for gi in range(N_GROUP):
                    blk = zero
                    if gi in blocks:
                        for kh in ks:
                            blk = jnp.where(row == kh, _colsum(blocks[gi][kh]), blk)
                    g_ref[at, SUBLANES * gi:SUBLANES * (gi + 1), :] = blk
            return carry

        lax.fori_loop(0, SCAN_CHUNK, step, 0)

    return pl.pallas_call(
        kern,
        out_shape=[jax.ShapeDtypeStruct((steps, HEAD, LANES), F32)] * 2
        + [jax.ShapeDtypeStruct((steps, SUBLANES, LANES), F32)] * 2,
        grid=(nc,),
        in_specs=[k_spec(back), k_spec(same), v_spec(back), v_spec(same), v_spec(back), v_spec(same),
                  pl.BlockSpec((SCAN_CHUNK,) + ST_SHAPE, lambda c: (back(c), 0, 0, 0, 0, 0)),
                  pl.BlockSpec(ST_SHAPE, lambda c: (0, 0, 0, 0, 0))],
        out_specs=[k_spec(back), k_spec(same), v_spec(back), v_spec(same)],
        scratch_shapes=[pltpu.VMEM(ST_SHAPE, F32), pltpu.VMEM(ST_SHAPE, F32)],
        compiler_params=_params(("arbitrary",)), name=name)(xall, xall, v_c, v_c, dy_c, dy_c, hist, fin)


def _key_row(x_t, grp, kh):
    r = SUBLANES * grp + kh
    return jnp.broadcast_to(x_t[r:r + 1, :], (SUBLANES, LANES))


def _acc(total, term):
    return term if total is None else total + term


SA_SHAPE = (2, V_HI, SUBLANES, LANES)


def _scan_fwd(xall, v_c, *, gather=(), name):
    steps = xall.shape[0]
    nc = steps // SCAN_CHUNK
    same, mirror, k_spec, v_spec = _kscan_specs(nc)
    last = SCAN_CHUNK - 1
    n_x = len(gather)

    def kern(*refs):
        xf_ref, xb_ref, vf_ref, vb_ref = refs[:4]
        yf_ref, yb_ref, hist_ref, fin_ref, sa_ref = refs[4 + n_x:9 + n_x]
        st_ref = refs[9 + 2 * n_x]
        c = pl.program_id(0)

        def riders():
            return _exchange_copies(refs[4:4 + n_x], refs[9 + n_x:9 + 2 * n_x], 0, *refs[10 + 2 * n_x:])

        @pl.when(c == 0)
        def _():
            st_ref[...] = jnp.zeros_like(st_ref)
            if n_x:
                for cp in riders():
                    cp.start()

        hist_ref[0] = st_ref[...]
        grp = _lane_group_index()

        def body(i, put):
            j = last - i
            for d, (x_t, v_t, y_ref, at) in enumerate(((xf_ref[i], vf_ref[i], yf_ref, i),
                                                       (xb_ref[j], vb_ref[j], yb_ref, j))):
                v_b = _spread_groups(v_t, grp)
                part = [None] * V_HI
                for kh in range(K_HI):
                    kk_r = _key_row(x_t, G_KK, kh)
                    for vh in range(V_HI):
                        part[vh] = _acc(part[vh], hist_ref[i, d, kh, vh] * kk_r)
                sa = [_lane_group_sum_short(p) for p in part]
                for vh in range(V_HI):
                    sa_ref[i, d, vh] = sa[vh]
                y_p = [None] * V_HI
                for kh in range(K_HI):
                    r_r, w_r = _key_row(x_t, G_R, kh), _key_row(x_t, G_W[d], kh)
                    b_r, kd_r = _key_row(x_t, G_B[d], kh), _key_row(x_t, G_KD[d], kh)
                    for vh in range(V_HI):
                        new = hist_ref[i, d, kh, vh] * w_r - sa[vh] * b_r + v_b[vh] * kd_r
                        put(d, kh, vh, new)
                        y_p[vh] = _acc(y_p[vh], new * r_r)
                y_ref[at] = _gather_groups(y_p, grp)

        def step(i, carry):
            def put(d, kh, vh, val):
                hist_ref[i + 1, d, kh, vh] = val
            body(i, put)
            return carry

        lax.fori_loop(0, last, step, 0, unroll=SCAN_UNROLL)

        def put_carry(d, kh, vh, val):
            st_ref[d, kh, vh] = val

        body(last, put_carry)

        @pl.when(c == nc - 1)
        def _():
            fin_ref[...] = st_ref[...]
            if n_x:
                for cp in riders():
                    cp.wait()

    return pl.pallas_call(
        kern,
        out_shape=[jax.ShapeDtypeStruct((steps, SUBLANES, LANES), F32)] * 2
        + [jax.ShapeDtypeStruct((steps,) + ST_SHAPE, F32), jax.ShapeDtypeStruct(ST_SHAPE, F32),
           jax.ShapeDtypeStruct((steps,) + SA_SHAPE, F32)]
        + _exchange_out_shapes(gather, 0),
        grid=(nc,), in_specs=[k_spec(same), k_spec(mirror), v_spec(same), v_spec(mirror)] + _hbm_specs(n_x),
        out_specs=[v_spec(same), v_spec(mirror),
                   pl.BlockSpec((SCAN_CHUNK,) + ST_SHAPE, lambda c: (c, 0, 0, 0, 0, 0)),
                   pl.BlockSpec(ST_SHAPE, lambda c: (0, 0, 0, 0, 0)),
                   pl.BlockSpec((SCAN_CHUNK,) + SA_SHAPE, lambda c: (c, 0, 0, 0, 0))] + _hbm_specs(n_x),
        scratch_shapes=[pltpu.VMEM(ST_SHAPE, F32)] + (_exchange_sems(n_x) if n_x else []),
        compiler_params=_params(("arbitrary",)), name=name)(xall, xall, v_c, v_c, *gather)


def _scan_bwd(xall, v_c, dy_c, hist, fin, sa, *, exchange=(), name):
    steps = xall.shape[0]
    nc = steps // SCAN_CHUNK
    same, back, k_spec, v_spec = _kscan_specs(nc)
    last = SCAN_CHUNK - 1
    n_x = len(exchange)

    def kern(*refs):
        xf_ref, xb_ref, vf_ref, vb_ref, dyf_ref, dyb_ref, hist_ref, fin_ref, sa_ref = refs[:9]
        gf_ref, gb_ref, dvf_ref, dvb_ref = refs[9 + n_x:13 + n_x]
        ds_ref, after_ref = refs[13 + 2 * n_x:15 + 2 * n_x]
        c = pl.program_id(0)

        def riders():
            return _exchange_copies(refs[9:9 + n_x], refs[13 + n_x:13 + 2 * n_x], n_x, *refs[15 + 2 * n_x:])

        @pl.when(c == 0)
        def _():
            ds_ref[...] = jnp.zeros_like(ds_ref)
            after_ref[...] = fin_ref[...]
            if n_x:
                for cp in riders():
                    cp.start()

        grp = _lane_group_index()
        row = lax.broadcasted_iota(jnp.int32, (SUBLANES, LANES), 0)
        zero = jnp.zeros((SUBLANES, LANES), F32)

        def body(i, after):
            j = last - i
            for d, (x_t, v_t, dy_t, g_ref, dv_ref, at) in enumerate((
                    (xf_ref[i], vf_ref[i], dyf_ref[i], gf_ref, dvf_ref, i),
                    (xb_ref[j], vb_ref[j], dyb_ref[j], gb_ref, dvb_ref, j))):
                v_s, dy_s = _spread_groups(v_t, grp), _spread_groups(dy_t, grp)
                dsa_p, dv_p = [None] * V_HI, [None] * V_HI
                for kh in range(K_HI):
                    r_r = _key_row(x_t, G_R, kh)
                    b_r, kd_r = _key_row(x_t, G_B[d], kh), _key_row(x_t, G_KD[d], kh)
                    for vh in range(V_HI):
                        g = ds_ref[d, kh, vh] + dy_s[vh] * r_r
                        ds_ref[d, kh, vh] = g
                        dsa_p[vh] = _acc(dsa_p[vh], g * b_r)
                        dv_p[vh] = _acc(dv_p[vh], g * kd_r)
                dsa = [-_lane_group_sum_short(p) for p in dsa_p]
                sa = [sa_ref[i, d, vh] for vh in range(V_HI)]
                dv_ref[at] = _gather_groups(dv_p, grp)
                blocks = {G_KK: zero, G_R: zero, G_W[d]: zero, G_B[d]: zero, G_KD[d]: zero}
                for kh in range(K_HI):
                    w_r, kk_r = _key_row(x_t, G_W[d], kh), _key_row(x_t, G_KK, kh)
                    dkk = dr = dw = db = dkd = None
                    for vh in range(V_HI):
                        g, before = ds_ref[d, kh, vh], hist_ref[i, d, kh, vh]
                        dr = _acc(dr, after(d, kh, vh) * dy_s[vh])
                        dw = _acc(dw, g * before)
                        dkd = _acc(dkd, g * v_s[vh])
                        db = _acc(db, g * sa[vh])
                        dkk = _acc(dkk, before * dsa[vh])
                        ds_ref[d, kh, vh] = g * w_r + dsa[vh] * kk_r
                    for gi, a in ((G_KK, dkk), (G_R, dr), (G_W[d], dw), (G_B[d], -db), (G_KD[d], dkd)):
                        blocks[gi] = jnp.where(row == kh, _colsum(a), blocks[gi])
                for gi in range(N_GROUP):
                    g_ref[at, SUBLANES * gi:SUBLANES * (gi + 1), :] = blocks.get(gi, zero)

        body(last, lambda d, kh, vh: after_ref[d, kh, vh])

        def step(ii, carry):
            i = last - ii
            body(i, lambda d, kh, vh: hist_ref[i + 1, d, kh, vh])
            return carry

        lax.fori_loop(1, SCAN_CHUNK, step, 0, unroll=SCAN_UNROLL)
        after_ref[...] = hist_ref[0]

        if n_x:
            @pl.when(c == nc - 1)
            def _():
                for cp in riders():
                    cp.wait()

    return pl.pallas_call(
        kern,
        out_shape=[jax.ShapeDtypeStruct((steps, HEAD, LANES), F32)] * 2
        + [jax.ShapeDtypeStruct((steps, SUBLANES, LANES), F32)] * 2 + _exchange_out_shapes(exchange, n_x),
        grid=(nc,),
        in_specs=[k_spec(back), k_spec(same), v_spec(back), v_spec(same), v_spec(back), v_spec(same),
                  pl.BlockSpec((SCAN_CHUNK,) + ST_SHAPE, lambda c: (back(c), 0, 0, 0, 0, 0)),
                  pl.BlockSpec(ST_SHAPE, lambda c: (0, 0, 0, 0, 0)),
                  pl.BlockSpec((SCAN_CHUNK,) + SA_SHAPE, lambda c: (back(c), 0, 0, 0, 0))] + _hbm_specs(n_x),
        out_specs=[k_spec(back), k_spec(same), v_spec(back), v_spec(same)] + _hbm_specs(n_x),
        scratch_shapes=[pltpu.VMEM(ST_SHAPE, F32), pltpu.VMEM(ST_SHAPE, F32)]
        + (_exchange_sems(n_x) if n_x else []),
        compiler_params=_params(("arbitrary",)), name=name)(xall, xall, v_c, v_c, dy_c, dy_c, hist, fin, sa,
                                                            *exchange)


def _to_key_rows(wide, bsz, seq):
    z = wide.reshape(bsz, seq, N_GROUP, N_HEAD, K_HI, SUBLANES).transpose(1, 2, 4, 5, 0, 3)
    return z.reshape(seq, HEAD, LANES)


def _from_key_rows(g, bsz, seq):
    z = g.reshape(seq, N_GROUP, K_HI, SUBLANES, bsz, N_HEAD).transpose(4, 0, 1, 5, 2, 3)
    return z.reshape(bsz * seq, N_GROUP * D_RWKV)


def _to_value_rows(a, bsz, seq):
    z = a.reshape(bsz, seq, N_HEAD, V_HI, SUBLANES).transpose(1, 4, 3, 0, 2)
    return z.reshape(seq, SUBLANES, LANES)


def _from_value_rows(y, bsz, seq):
    z = y.reshape(seq, SUBLANES, V_HI, bsz, N_HEAD).transpose(3, 0, 4, 2, 1)
    return z.reshape(bsz * seq, D_RWKV)


def _pad_cols(a, segs):
    out, off = [], 0
    for w, wp in segs:
        out.append(a[..., off:off + w])
        if wp > w:
            out.append(jnp.zeros(a.shape[:-1] + (wp - w,), a.dtype))
        off += w
    return jnp.concatenate(out, axis=-1)


def _unpad_cols(a, segs):
    out, off = [], 0
    for w, wp in segs:
        out.append(a[..., off:off + w])
        off += wp
    return jnp.concatenate(out, axis=-1)


P_SEGS = ((3 * D_RWKV, 3 * D_RWKV), (D_LORA, 128), (D_LORA, 128), (D_GATE, 256), (3 * D_CONV, 3 * D_CONV))
S_SEGS = P_SEGS[:4]


def _pad_rows(a, rows):
    return jnp.concatenate([a, jnp.zeros((rows - a.shape[0], a.shape[1]), a.dtype)], axis=0)


LATE = ("w_out", "w_gate", "w_up", "w_down")


def _local_step(x, target, w, late=None):
    bsz, seq, _ = x.shape
    t = bsz * seq
    x2d = x.reshape(t, D_MODEL)
    tg2d = target.reshape(t, D_MODEL)
    row = lambda a: a.reshape(1, -1).astype(F32)

    w_in = _pad_cols(w["w_in"][0], P_SEGS)
    mu = _pad_cols(row(w["mu_shift"]), S_SEGS)
    wupf, wupb, aupf, aupb = (_pad_rows(w[n][0].astype(F32), 128) for n in ("w_up_f", "w_up_b", "a_up_f", "a_up_b"))
    gup = _pad_rows(w["g_up"][0].astype(F32), 256)
    conv_w = _pad_rows(w["conv_w"][0].astype(F32), SUBLANES)
    norm1, norm2, normf = row(w["norm1_w"]), row(w["norm2_w"]), row(w["norm_f_w"])
    vec = {n: row(w[n]) for n in VEC}
    head_of = jnp.arange(LANES) // HEAD
    bd = (head_of[:, None] == head_of[None, :]).astype(F32)
    pre_consts = [vec["k_k"], vec["w0_f"], vec["w0_b"], vec["a0_f"], vec["a0_b"], vec["k_a_f"], vec["k_a_b"],
                  wupf, wupb, aupf, aupb, gup, bd]
    post_consts = [vec["gn_w"], vec["gn_b"], vec["r_k_f"], vec["r_k_b"], bd]

    h1, = _rowwise(_rms, [x2d], [norm1], [D_MODEL], [], name="rms1_fwd", out_dtype=BF16, tb=WIDE_TILE)
    p = _mm(h1, w_in, name="mm_in")
    pss, oconv = _shift_conv_fwd(p, mu, conv_w, seq, name="shift_conv_fwd")
    pre_rows = [(pss, 0, 512), (pss, 1, 512), (pss, XW0 // 128, 128), (pss, XA0 // 128, 128), (pss, XG0 // 256, 256)]
    sc, g = _rowwise(_prescan_math, pre_rows, pre_consts, [[D_RWKV] * N_GROUP, D_RWKV], [], name="prescan_fwd")
    xall = _to_key_rows(sc, bsz, seq)
    v_l = _to_value_rows(pss[:, 2 * D_RWKV:3 * D_RWKV], bsz, seq)
    y_f, y_b, hist, fin, sa, *gathered = _scan_fwd(xall, v_l, gather=[late[n] for n in LATE] if late else (),
                                                   name="scan_fwd")
    w_out, w_gate, w_up, w_down = (
        (_from_slots(a, SHARD_AXIS[n]) if late else w[n])[0] for n, a in zip(LATE, gathered or LATE))
    y = _from_value_rows(y_f + y_b, bsz, seq)
    post_rows = [y, (pss, 0, 512), (pss, 2, 512), (sc, G_KD[0], 512), (sc, G_KD[1], 512), g]

    def post_fwd(y_, r_, v_, kdf_, kdb_, g_, oc_, *consts):
        return _postscan_math(y_, r_, v_, kdf_, kdb_, g_, *consts), oc_

    o, = _rowwise(post_fwd, post_rows + [oconv], post_consts, [[D_RWKV, D_CONV]], [], name="postscan_fwd",
                  out_dtype=BF16)
    x1 = _mm(o, w_out, add=x2d, name="mm_out")
    h2, = _rowwise(_rms, [x1], [norm2], [D_MODEL], [], name="rms2_fwd", out_dtype=BF16, tb=WIDE_TILE)
    gg, uu, ff = _mm_swiglu(h2, w_gate, w_up, name="mm_gate_up")
    x2 = _mm(ff, w_down, add=x1, name="mm_down")

    def final(x_, tg_, wn_):
        yo, vjp = jax.vjp(_rms, x_, wn_)
        err = yo - tg_
        dx_, dwn_ = vjp(err * (1.0 / D_MODEL))
        part = jnp.sum(jnp.sum(err * err, axis=1, keepdims=True), axis=0, keepdims=True) * (0.5 / D_MODEL)
        return dx_, part + jnp.zeros((1, LANES), F32), dwn_

    dx2, loss_acc, d_normf = _rowwise(final, [x2, tg2d], [normf], [D_MODEL], [(1, LANES), (1, D_MODEL)],
                                      name="loss_head", tb=WIDE_TILE)
    dgg, duu = _mm_swiglu_bwd(dx2, w_down, gg, uu, name="mm_down_dx")
    g_w_down = _mm(ff, dx2, ta=True, name="mm_down_dw")
    dh2 = _mm(dgg, w_gate, tb=True, name="mm_gate_dx")
    dh2 = _mm(duu, w_up, tb=True, add=dh2, name="mm_up_dx")
    g_w_gate = _mm(h2, dgg, ta=True, name="mm_gate_dw")
    g_w_up = _mm(h2, duu, ta=True, name="mm_up_dw")

    def rms_bwd(x_, dh_, dres_, wn_):
        _, vjp = jax.vjp(_rms, x_, wn_)
        dx_, dwn_ = vjp(dh_)
        return dx_ + dres_, dwn_

    dx1, d_norm2 = _rowwise(rms_bwd, [x1, dh2, dx2], [norm2], [D_MODEL], [(1, D_MODEL)], name="rms2_bwd", tb=WIDE_TILE)
    do = _mm(dx1, w_out, tb=True, name="mm_out_dx")
    g_w_out = _mm(o, dx1, ta=True, name="mm_out_dw")

    def post_bwd(y_, r_, v_, kdf_, kdb_, g_, do_, *consts):
        _, vjp = jax.vjp(lambda *a: _postscan_math(*a, consts[4]), y_, r_, v_, kdf_, kdb_, g_, *consts[:4])
        return vjp(do_)

    (dy, dr_c, dv_c, dkdf_c, dkdb_c, dg, d_gn_w, d_gn_b, d_rkf, d_rkb) = _rowwise(
        post_bwd, post_rows + [(do, 0, 512)], post_consts, [D_RWKV] * 6, [(1, D_RWKV)] * 4, name="postscan_bwd")
    dy_l = _to_value_rows(dy, bsz, seq)
    late_grads = {"w_out": g_w_out[None], "w_gate": g_w_gate[None], "w_up": g_w_up[None], "w_down": g_w_down[None]}
    g_f, g_b, dv_f, dv_b, *late_parts = _scan_bwd(
        xall, v_l, dy_l, hist, fin, sa, name="scan_bwd",
        exchange=[_to_slots(late_grads[n], SHARD_AXIS[n]).astype(BF16) for n in LATE] if late else ())
    dsc = _from_key_rows(g_f + g_b, bsz, seq)
    dv_s = _from_value_rows(dv_f + dv_b, bsz, seq)

    def pre_bwd(r_, k_, xw_, xa_, xg_, dkk_, dr_s, dwf_, dwb_, dbf_, dbb_, dkdf_s, dkdb_s,
                dr_c_, dv_c_, dv_s_, dkdf_c_, dkdb_c_, dg_, *consts):
        _, vjp = jax.vjp(lambda *a: _prescan_math(*a, consts[-1]), r_, k_, xw_, xa_, xg_, *consts[:-1])
        grads = vjp((dkk_, dr_s + dr_c_, dwf_, dwb_, dbf_, dbb_, dkdf_s + dkdf_c_, dkdb_s + dkdb_c_, dg_))
        dr_, dk_, dxw_, dxa_, dxg_ = grads[:5]
        return (dr_, dk_, dv_c_ + dv_s_, dxw_, dxa_, dxg_) + tuple(grads[5:])

    pre_b_rows = (pre_rows + [(dsc, j, 512) for j in range(N_GROUP)]
                  + [dr_c, dv_c, dv_s, dkdf_c, dkdb_c, dg])
    pre_b = _rowwise(pre_bwd, pre_b_rows, pre_consts, [[512, 512, 512, 128, 128, 256]],
                     [(1, D_RWKV)] * 7 + [(128, D_RWKV)] * 4 + [(256, D_RWKV)], name="prescan_bwd")
    d_pss = pre_b[0]
    d_kk_, d_w0f, d_w0b, d_a0f, d_a0b, d_kaf, d_kab, d_wupf, d_wupb, d_aupf, d_aupb, d_gup = pre_b[1:]
    dp, d_mu, d_conv = _shift_conv_bwd(p, d_pss, do, mu, conv_w, seq, name="shift_conv_bwd")
    g_w_in = _mm(h1, dp, ta=True, name="mm_in_dw")
    grads = {
        "w_in": _unpad_cols(g_w_in, P_SEGS)[None], "mu_shift": _unpad_cols(d_mu, S_SEGS),
        "w_up_f": d_wupf[None, :D_LORA], "w0_f": d_w0f, "w_up_b": d_wupb[None, :D_LORA], "w0_b": d_w0b,
        "a_up_f": d_aupf[None, :D_LORA], "a0_f": d_a0f, "a_up_b": d_aupb[None, :D_LORA], "a0_b": d_a0b,
        "g_up": d_gup[None, :D_GATE], "k_k": d_kk_, "k_a_f": d_kaf, "k_a_b": d_kab,
        "r_k_f": d_rkf, "r_k_b": d_rkb, "gn_w": d_gn_w, "gn_b": d_gn_b, "conv_w": d_conv[None, :3],
        "w_out": g_w_out[None], "norm2_w": d_norm2, "w_gate": g_w_gate[None], "w_up": g_w_up[None],
        "w_down": g_w_down[None], "norm_f_w": d_normf,
    }
    early = ("w_in",) + LORA
    parts = dict(zip(LATE, late_parts))
    if late:
        vec_rows = jnp.concatenate([grads[n] for n in VEC] + [jnp.zeros((16 - len(VEC), D_RWKV), F32)], axis=0)
        slots = [_to_slots(grads[n], SHARD_AXIS[n]).astype(BF16 if n in BIG else F32) for n in early]
        dh1, *recv = _mm(dp, w_in, tb=True, exchange=(slots, [vec_rows]), name="mm_in_dx")
        parts.update(zip(early + ("vec",), recv))
    else:
        dh1 = _mm(dp, w_in, tb=True, name="mm_in_dx")
    dx, grads["norm1_w"] = _rowwise(rms_bwd, [x2d, dh1, dx1], [norm1], [D_MODEL], [(1, D_MODEL)], name="rms1_bwd",
                                    tb=WIDE_TILE)
    return loss_acc[0, 0], dx.reshape(bsz, seq, D_MODEL), grads, parts


def _hbm_specs(n):
    return [pl.BlockSpec(memory_space=pl.ANY)] * n


def _all_gather(arrs, *, name):
    n = len(arrs)

    def body(*refs):
        x_refs, out_refs = refs[:n], refs[n:2 * n]
        send_sems, recv_sems, local_sems = refs[2 * n:]
        x, y, c = lax.axis_index("x"), lax.axis_index("y"), lax.axis_index("c")
        me, sibling = (x, y, c), (x, y, 1 - c)
        chips = [(1 - x, y), (x, 1 - y), (1 - x, 1 - y)]

        def slot(a, px, py, pc):
            return out_refs[a].at[4 * px + 2 * py + pc]

        def copy(a, k, block, to, src=None):
            return pltpu.make_async_remote_copy(
                src_ref=slot(a, *block) if src is None else src, dst_ref=slot(a, *block),
                send_sem=send_sems.at[k, a], recv_sem=recv_sems.at[k, a],
                device_id=to, device_id_type=pl.DeviceIdType.MESH)

        mine = [pltpu.make_async_copy(x_refs[a], slot(a, *me), local_sems.at[a]) for a in range(n)]
        for cp in mine:
            cp.start()
        first = []
        for a in range(n):
            first.append(copy(a, 0, me, sibling, src=x_refs[a]))
            first += [copy(a, 1 + j, me, (*chip, c), src=x_refs[a]) for j, chip in enumerate(chips)]
        for cp in first:
            cp.start()
        passed = []
        for j, chip in enumerate(chips):
            for a in range(n):
                copy(a, 1 + j, (*chip, c), me).wait_recv()
                cp = copy(a, 4 + j, (*chip, c), sibling)
                cp.start()
                passed.append(cp)
        for a in range(n):
            copy(a, 0, sibling, me).wait_recv()
            for j, chip in enumerate(chips):
                copy(a, 4 + j, (*chip, 1 - c), me).wait_recv()
        for cp in first + passed:
            cp.wait_send()
        for cp in mine:
            cp.wait()

    return pl.pallas_call(
        body, out_shape=[jax.ShapeDtypeStruct((N_DEV,) + a.shape, a.dtype) for a in arrs],
        in_specs=_hbm_specs(n), out_specs=_hbm_specs(n),
        scratch_shapes=[pltpu.SemaphoreType.DMA((7, n)), pltpu.SemaphoreType.DMA((7, n)),
                        pltpu.SemaphoreType.DMA((n,))],
        name=name)(*arrs)


def _exchange(sliced, whole, *, name):
    arrs = list(sliced) + list(whole)
    n, n_sliced = len(arrs), len(sliced)

    def body(*refs):
        copies = _exchange_copies(refs[:n], refs[n:2 * n], n_sliced, *refs[2 * n:])
        for cp in copies:
            cp.start()
        for cp in copies:
            cp.wait()

    return pl.pallas_call(
        body, out_shape=_exchange_out_shapes(arrs, n_sliced), in_specs=_hbm_specs(n), out_specs=_hbm_specs(n),
        scratch_shapes=_exchange_sems(n), name=name)(*arrs)


def _exchange_out_shapes(arrs, n_sliced):
    return [jax.ShapeDtypeStruct(a.shape if i < n_sliced else (N_DEV,) + a.shape, a.dtype)
            for i, a in enumerate(arrs)]


def _exchange_sems(n):
    return [pltpu.SemaphoreType.DMA((7, n)), pltpu.SemaphoreType.DMA((7, n)), pltpu.SemaphoreType.DMA((n,))]


def _exchange_copies(in_refs, out_refs, n_sliced, send_sems, recv_sems, local_sems):
    n = len(in_refs)
    x, y, c = lax.axis_index("x"), lax.axis_index("y"), lax.axis_index("c")
    me = 4 * x + 2 * y + c

    def src(a, dev):
        return in_refs[a].at[dev] if a < n_sliced else in_refs[a]

    copies = [pltpu.make_async_copy(src(a, me), out_refs[a].at[me], local_sems.at[a]) for a in range(n)]
    for k in range(1, N_DEV):
        px = 1 - x if k & 4 else x
        py = 1 - y if k & 2 else y
        pc = 1 - c if k & 1 else c
        for a in range(n):
            copies.append(pltpu.make_async_remote_copy(
                src_ref=src(a, 4 * px + 2 * py + pc), dst_ref=out_refs[a].at[me],
                send_sem=send_sems.at[k - 1, a], recv_sem=recv_sems.at[k - 1, a],
                device_id=(px, py, pc), device_id_type=pl.DeviceIdType.MESH))
    return copies


def _adam_math(g, w, m, v):
    nm = ADAM_B1 * m + (1.0 - ADAM_B1) * g
    nv = ADAM_B2 * v + (1.0 - ADAM_B2) * (g * g)
    m_hat = nm / (1.0 - ADAM_B1 ** ADAM_STEP)
    v_hat = nv / (1.0 - ADAM_B2 ** ADAM_STEP)
    return -ADAM_LR * (m_hat / (jnp.sqrt(v_hat) + ADAM_EPS) + ADAM_WD * w), nm, nv


def _slot_sum(ref):
    g = ref[0].astype(F32)
    for s in range(1, N_DEV):
        g = g + ref[s].astype(F32)
    return g


def _adamw_big(parts, w, m, v, *, name):
    _, rws, cols = w.shape
    tr = _tile(rws, (256, 176, 128))

    def kern(p_ref, w_ref, m_ref, v_ref, g_ref, d_ref, nm_ref, nv_ref):
        g = _slot_sum(p_ref)
        g_ref[...] = g
        d_ref[...], nm_ref[...], nv_ref[...] = _adam_math(g, w_ref[...], m_ref[...], v_ref[...])

    spec = pl.BlockSpec((1, tr, cols), lambda i: (0, i, 0))
    return pl.pallas_call(
        kern, out_shape=[jax.ShapeDtypeStruct(w.shape, F32)] * 4, grid=(rws // tr,),
        in_specs=[pl.BlockSpec((N_DEV, 1, tr, cols), lambda i: (0, 0, i, 0)), spec, spec, spec],
        out_specs=[spec] * 4, compiler_params=_params(("parallel",)), name=name)(parts, w, m, v)


def _adamw_small(lora_parts, vec_parts, wide_parts, wmv, *, name):
    names = LORA + VEC + WIDE
    n_l, n = len(LORA), len(names)
    flat = [a for trip in wmv for a in trip]

    def kern(*refs):
        l_refs, vec_ref, wide_ref = refs[:n_l], refs[n_l], refs[n_l + 1]
        in_refs = refs[n_l + 2:n_l + 2 + 3 * n]
        out_refs = refs[n_l + 2 + 3 * n:]
        vec_sum, wide_sum = _slot_sum(vec_ref), _slot_sum(wide_ref)
        for i, nm in enumerate(names):
            w_ref, m_ref, v_ref = in_refs[3 * i:3 * i + 3]
            if i < n_l:
                g = _slot_sum(l_refs[i])
            elif nm in VEC:
                g = vec_sum[i - n_l:i - n_l + 1, :]
            else:
                g = wide_sum[WIDE.index(nm):WIDE.index(nm) + 1, :w_ref.shape[-1]]
            o = out_refs[4 * i:4 * i + 4]
            o[0][...] = g
            o[1][...], o[2][...], o[3][...] = _adam_math(g, w_ref[...], m_ref[...], v_ref[...])

    out_shape = [jax.ShapeDtypeStruct(trip[0].shape, F32) for trip in wmv for _ in range(4)]
    outs = pl.pallas_call(kern, out_shape=out_shape, name=name,
                          compiler_params=pltpu.CompilerParams(vmem_limit_bytes=VMEM_LIMIT))(
        *lora_parts, vec_parts, wide_parts, *flat)
    return [tuple(outs[4 * i:4 * i + 4]) for i in range(n)]


def _to_slots(g, axis):
    _, rws, cols = g.shape
    if axis == 1:
        return g.reshape(N_DEV, 1, rws // N_DEV, cols)
    return g.reshape(1, rws, N_DEV, cols // N_DEV).transpose(2, 0, 1, 3)


def _from_slots(got, axis):
    _, _, rws, cols = got.shape
    if axis == 1:
        return got.reshape(1, N_DEV * rws, cols)
    return got.transpose(1, 2, 0, 3).reshape(1, rws, N_DEV * cols)


def _pad_lanes(a, width):
    return jnp.concatenate([a, jnp.zeros(a.shape[:-1] + (width - a.shape[-1],), a.dtype)], axis=-1)


def kernel(x, norm1_w, w_in, mu_shift, w_up_f, w0_f, w_up_b, w0_b, a_up_f, a0_f, a_up_b, a0_b, g_up, k_k, k_a_f, k_a_b, r_k_f, r_k_b, gn_w, gn_b, conv_w, w_out, norm2_w, w_gate, w_up, w_down, norm_f_w, loss_target, m_norm1_w, m_w_in, m_mu_shift, m_w_up_f, m_w0_f, m_w_up_b, m_w0_b, m_a_up_f, m_a0_f, m_a_up_b, m_a0_b, m_g_up, m_k_k, m_k_a_f, m_k_a_b, m_r_k_f, m_r_k_b, m_gn_w, m_gn_b, m_conv_w, m_w_out, m_norm2_w, m_w_gate, m_w_up, m_w_down, m_norm_f_w, v_norm1_w, v_w_in, v_mu_shift, v_w_up_f, v_w0_f, v_w_up_b, v_w0_b, v_a_up_f, v_a0_f, v_a_up_b, v_a0_b, v_g_up, v_k_k, v_k_a_f, v_k_a_b, v_r_k_f, v_r_k_b, v_gn_w, v_gn_b, v_conv_w, v_w_out, v_norm2_w, v_w_gate, v_w_up, v_w_down, v_norm_f_w):
    local = dict(norm1_w=norm1_w, w_in=w_in, mu_shift=mu_shift, w_up_f=w_up_f, w0_f=w0_f, w_up_b=w_up_b,
                 w0_b=w0_b, a_up_f=a_up_f, a0_f=a0_f, a_up_b=a_up_b, a0_b=a0_b, g_up=g_up, k_k=k_k, k_a_f=k_a_f,
                 k_a_b=k_a_b, r_k_f=r_k_f, r_k_b=r_k_b, gn_w=gn_w, gn_b=gn_b, conv_w=conv_w, w_out=w_out,
                 norm2_w=norm2_w, w_gate=w_gate, w_up=w_up, w_down=w_down, norm_f_w=norm_f_w)
    mom_m = dict(norm1_w=m_norm1_w, w_in=m_w_in, mu_shift=m_mu_shift, w_up_f=m_w_up_f, w0_f=m_w0_f,
                 w_up_b=m_w_up_b, w0_b=m_w0_b, a_up_f=m_a_up_f, a0_f=m_a0_f, a_up_b=m_a_up_b, a0_b=m_a0_b,
                 g_up=m_g_up, k_k=m_k_k, k_a_f=m_k_a_f, k_a_b=m_k_a_b, r_k_f=m_r_k_f, r_k_b=m_r_k_b,
                 gn_w=m_gn_w, gn_b=m_gn_b, conv_w=m_conv_w, w_out=m_w_out, norm2_w=m_norm2_w, w_gate=m_w_gate,
                 w_up=m_w_up, w_down=m_w_down, norm_f_w=m_norm_f_w)
    mom_v = dict(norm1_w=v_norm1_w, w_in=v_w_in, mu_shift=v_mu_shift, w_up_f=v_w_up_f, w0_f=v_w0_f,
                 w_up_b=v_w_up_b, w0_b=v_w0_b, a_up_f=v_a_up_f, a0_f=v_a0_f, a_up_b=v_a_up_b, a0_b=v_a0_b,
                 g_up=v_g_up, k_k=v_k_k, k_a_f=v_k_a_f, k_a_b=v_k_a_b, r_k_f=v_r_k_f, r_k_b=v_r_k_b,
                 gn_w=v_gn_w, gn_b=v_gn_b, conv_w=v_conv_w, w_out=v_w_out, norm2_w=v_norm2_w, w_gate=v_w_gate,
                 w_up=v_w_up, w_down=v_w_down, norm_f_w=v_norm_f_w)

    early = ("w_in",) + LORA
    got = _all_gather([local["w_in"].astype(BF16)] + [local[n] for n in LORA], name="gather")
    full = dict(local)
    full.update({n: _from_slots(a, SHARD_AXIS[n]) for n, a in zip(early, got)})

    loss_part, grad_x, grads, parts = _local_step(x, loss_target, full,
                                                  late={n: local[n].astype(BF16) for n in LATE})
    loss = lax.psum(loss_part, ("x", "y", "c"))

    wide_rows = jnp.concatenate([_pad_lanes(grads[n], WIDE_ROW) for n in WIDE]
                                + [jnp.zeros((SUBLANES - len(WIDE), WIDE_ROW), F32)], axis=0)
    wide_parts, = _exchange([], [wide_rows], name="grad_exchange")
    out = {}
    for n in BIG:
        out[n] = _adamw_big(parts[n], local[n], mom_m[n], mom_v[n], name="adamw_" + n)

    def small_form(n, a):
        if n in LORA:
            return a
        a = a.reshape(1, -1)
        return _pad_lanes(a, WIDE_ROW) if n == "mu_shift" else a

    small = LORA + VEC + WIDE
    res = _adamw_small([parts[n] for n in LORA], parts["vec"], wide_parts,
                       [tuple(small_form(n, d[n]) for d in (local, mom_m, mom_v)) for n in small],
                       name="adamw_small")
    for n, quad in zip(small, res):
        out[n] = tuple(a[..., :local[n].size].reshape(local[n].shape) if n not in LORA else a for a in quad)
    return (loss, grad_x, *[out[n][i] for i in range(4) for n in WEIGHTS])
```

```python
import functools

import jax
import jax.numpy as jnp
from jax import lax
from jax.experimental import pallas as pl
from jax.experimental.pallas import tpu as pltpu

F32 = jnp.float32
BF16 = jnp.bfloat16
HIGHEST = lax.Precision.HIGHEST

N_DEV = 8
D_MODEL = 1024
D_RWKV = 512
D_CONV = 512
HEAD = 64
N_HEAD = D_RWKV // HEAD
D_LORA = 64
D_GATE = 160
D_SHIFTED = 3 * D_RWKV + 2 * D_LORA + D_GATE
XW0, XA0, XG0 = 1536, 1664, 1792
D_SP = 2048
D_INP = D_SP + 3 * D_CONV
LOG_DECAY_SCALE = 0.606531
RMS_EPS = 1e-6
GN_EPS = 64e-5
NORM_EPS = 1e-12
ADAM_LR, ADAM_B1, ADAM_B2, ADAM_EPS, ADAM_WD, ADAM_STEP = 0.001, 0.9, 0.999, 1e-08, 0.01, 10

LANES = 128
SUBLANES = 8
VMEM_LIMIT = 48 * 1024 * 1024
SCAN_CHUNK = 16
SCAN_UNROLL = 3
ROW_TILE = 128
WIDE_TILE = 256

BIG = ("w_in", "w_out", "w_gate", "w_up", "w_down")
LORA = ("w_up_f", "w_up_b", "a_up_f", "a_up_b", "g_up", "conv_w")
SHARD_AXIS = {"w_in": 2, "w_out": 1, "w_gate": 2, "w_up": 2, "w_down": 1, "w_up_f": 2, "w_up_b": 2,
              "a_up_f": 2, "a_up_b": 2, "g_up": 2, "conv_w": 2}
VEC = ("w0_f", "w0_b", "a0_f", "a0_b", "k_k", "k_a_f", "k_a_b", "r_k_f", "r_k_b", "gn_w", "gn_b")
WIDE = ("mu_shift", "norm1_w", "norm2_w", "norm_f_w")
WIDE_ROW = 2048
WEIGHTS = ("norm1_w", "w_in", "mu_shift", "w_up_f", "w0_f", "w_up_b", "w0_b", "a_up_f", "a0_f", "a_up_b",
           "a0_b", "g_up", "k_k", "k_a_f", "k_a_b", "r_k_f", "r_k_b", "gn_w", "gn_b", "conv_w", "w_out",
           "norm2_w", "w_gate", "w_up", "w_down", "norm_f_w")


def _params(sem, limit=VMEM_LIMIT):
    return pltpu.CompilerParams(dimension_semantics=sem, vmem_limit_bytes=limit)


def _tile(n, cands):
    for c in cands:
        if n % c == 0:
            return c
    raise ValueError(f"no tile for {n}")


def _mm(a, b, *, ta=False, tb=False, add=None, exchange=None, name):
    (k_dim, m) = a.shape if ta else a.shape[::-1]
    (k2, n) = b.shape[::-1] if tb else b.shape
    assert k_dim == k2, (a.shape, b.shape, ta, tb)
    tm = _tile(m, (1408, 1024, 512, 256, 128))
    tn = _tile(n, (1408, 1024, 896, 512, 256, 128))
    tk = _tile(k_dim, (1408, 1024, 896, 512, 256, 128))
    nk = k_dim // tk
    grid = (m // tm, n // tn, nk)
    dims = (((0 if ta else 1,), (1 if tb else 0,)), ((), ()))
    sliced, whole = exchange or ((), ())
    riders = list(sliced) + list(whole)
    n_x, n_in = len(riders), 2 + (add is not None)

    def kern(*refs):
        a_ref, b_ref = refs[:2]
        add_ref = refs[2] if add is not None else None
        o_ref, acc_ref = refs[n_in + n_x], refs[n_in + 2 * n_x + 1]
        k = pl.program_id(2)
        step = (pl.program_id(0) * grid[1] + pl.program_id(1)) * nk + k

        def copies():
            return _exchange_copies(refs[n_in:n_in + n_x], refs[n_in + n_x + 1:n_in + 2 * n_x + 1], len(sliced),
                                    *refs[n_in + 2 * n_x + 2:])

        if n_x:
            @pl.when(step == 0)
            def _():
                for cp in copies():
                    cp.start()

        @pl.when(k == 0)
        def _():
            acc_ref[...] = jnp.zeros_like(acc_ref)

        acc_ref[...] += lax.dot_general(a_ref[...].astype(BF16), b_ref[...].astype(BF16), dims,
                                        preferred_element_type=F32)

        @pl.when(k == nk - 1)
        def _():
            if add is None:
                o_ref[...] = acc_ref[...]
            else:
                o_ref[...] = acc_ref[...] + add_ref[...]

        if n_x:
            @pl.when(step == grid[0] * grid[1] * nk - 1)
            def _():
                for cp in copies():
                    cp.wait()

    a_spec = (pl.BlockSpec((tk, tm), lambda i, j, k: (k, i)) if ta
              else pl.BlockSpec((tm, tk), lambda i, j, k: (i, k)))
    b_spec = (pl.BlockSpec((tn, tk), lambda i, j, k: (j, k)) if tb
              else pl.BlockSpec((tk, tn), lambda i, j, k: (k, j)))
    o_spec = pl.BlockSpec((tm, tn), lambda i, j, k: (i, j))
    in_specs = [a_spec, b_spec] + ([o_spec] if add is not None else []) + _hbm_specs(n_x)
    args = (a, b) + ((add,) if add is not None else ()) + tuple(riders)
    out = pl.pallas_call(
        kern, out_shape=[jax.ShapeDtypeStruct((m, n), F32)] + _exchange_out_shapes(riders, len(sliced)), grid=grid,
        in_specs=in_specs, out_specs=[o_spec] + _hbm_specs(n_x),
        scratch_shapes=[pltpu.VMEM((tm, tn), F32)] + (_exchange_sems(n_x) if n_x else []),
        compiler_params=_params(("arbitrary",) * 3 if n_x else ("parallel", "parallel", "arbitrary")),
        name=name)(*args)
    return out if n_x else out[0]


def _swiglu(g, u):
    return jax.nn.silu(g) * u


FFN_TN = 256


def _mm_swiglu(h, w_gate, w_up, *, name):
    m, k_dim = h.shape
    n = w_gate.shape[1]
    tm = _tile(m, (1024, 512, 256, 128))

    def kern(h_ref, wg_ref, wu_ref, g_ref, u_ref, f_ref):
        hv = h_ref[...].astype(BF16)
        g = jnp.dot(hv, wg_ref[...].astype(BF16), preferred_element_type=F32)
        u = jnp.dot(hv, wu_ref[...].astype(BF16), preferred_element_type=F32)
        g_ref[...] = g
        u_ref[...] = u
        f_ref[...] = _swiglu(g, u).astype(f_ref.dtype)

    w_spec = pl.BlockSpec((k_dim, FFN_TN), lambda i, j: (0, j))
    o_spec = pl.BlockSpec((tm, FFN_TN), lambda i, j: (i, j))
    return pl.pallas_call(
        kern, out_shape=[jax.ShapeDtypeStruct((m, n), F32)] * 2 + [jax.ShapeDtypeStruct((m, n), BF16)],
        grid=(m // tm, n // FFN_TN), in_specs=[pl.BlockSpec((tm, k_dim), lambda i, j: (i, 0)), w_spec, w_spec],
        out_specs=[o_spec] * 3, compiler_params=_params(("parallel", "parallel")), name=name)(h, w_gate, w_up)


def _mm_swiglu_bwd(dx, w_down, g, u, *, name):
    m, k_dim = dx.shape
    n = w_down.shape[0]
    tm = _tile(m, (1024, 512, 256, 128))

    def kern(dx_ref, w_ref, g_ref, u_ref, dg_ref, du_ref):
        df = lax.dot_general(dx_ref[...].astype(BF16), w_ref[...].astype(BF16), (((1,), (1,)), ((), ())),
                             preferred_element_type=F32)
        _, vjp = jax.vjp(_swiglu, g_ref[...], u_ref[...])
        dg, du = vjp(df)
        dg_ref[...] = dg.astype(dg_ref.dtype)
        du_ref[...] = du.astype(du_ref.dtype)

    o_spec = pl.BlockSpec((tm, FFN_TN), lambda i, j: (i, j))
    return pl.pallas_call(
        kern, out_shape=[jax.ShapeDtypeStruct((m, n), BF16)] * 2, grid=(m // tm, n // FFN_TN),
        in_specs=[pl.BlockSpec((tm, k_dim), lambda i, j: (i, 0)), pl.BlockSpec((FFN_TN, k_dim), lambda i, j: (j, 0)),
                  o_spec, o_spec],
        out_specs=[o_spec] * 2, compiler_params=_params(("parallel", "parallel")), name=name)(dx, w_down, g, u)


def _rowwise(fn, rows, consts, out_rows, out_accs, *, name, tb=ROW_TILE, out_dtype=F32):
    t = (rows[0][0] if isinstance(rows[0], tuple) else rows[0]).shape[0]
    n_r, n_c, n_o, n_a = len(rows), len(consts), len(out_rows), len(out_accs)
    pieces = [w if isinstance(w, (list, tuple)) else [w] for w in out_rows]

    def kern(*refs):
        r_refs = refs[:n_r]
        c_refs = refs[n_r:n_r + n_c]
        o_refs = refs[n_r + n_c:n_r + n_c + n_o]
        a_refs = refs[n_r + n_c + n_o:]
        vals = fn(*[r[...] for r in r_refs], *[c[...] for c in c_refs])
        vals = list(vals) if isinstance(vals, (tuple, list)) else [vals]
        pos = 0
        for o_ref, ws in zip(o_refs, pieces):
            off = 0
            for w in ws:
                o_ref[:, off:off + w] = vals[pos].astype(o_ref.dtype)
                off += w
                pos += 1
        if n_a:
            @pl.when(pl.program_id(0) == 0)
            def _():
                for a_ref in a_refs:
                    a_ref[...] = jnp.zeros_like(a_ref)
            for a_ref, v in zip(a_refs, vals[pos:]):
                a_ref[...] += v

    in_specs, args = [], []
    for r in rows:
        if isinstance(r, tuple):
            arr, blk, w = r
            in_specs.append(pl.BlockSpec((tb, w), functools.partial(lambda i, blk: (i, blk), blk=blk)))
        else:
            arr = r
            in_specs.append(pl.BlockSpec((tb, arr.shape[1]), lambda i: (i, 0)))
        args.append(arr)
    for c in consts:
        in_specs.append(pl.BlockSpec(c.shape, lambda i: (0, 0)))
        args.append(c)
    out_shape = [jax.ShapeDtypeStruct((t, sum(ws)), out_dtype) for ws in pieces]
    out_specs = [pl.BlockSpec((tb, sum(ws)), lambda i: (i, 0)) for ws in pieces]
    for shp in out_accs:
        out_shape.append(jax.ShapeDtypeStruct(shp, F32))
        out_specs.append(pl.BlockSpec(shp, lambda i: (0, 0)))
    res = pl.pallas_call(
        kern, out_shape=out_shape, grid=(t // tb,), in_specs=in_specs, out_specs=out_specs,
        compiler_params=_params(("arbitrary",) if n_a else ("parallel",)), name=name)(*args)
    return res


def _rms(x, w):
    return x * lax.rsqrt(jnp.mean(x * x, axis=-1, keepdims=True) + RMS_EPS) * w


def _seg_sum(x, bd):
    return jnp.concatenate(
        [jnp.dot(x[:, LANES * j:LANES * (j + 1)], bd, precision=HIGHEST, preferred_element_type=F32)
         for j in range(x.shape[1] // LANES)], axis=1)


@jax.custom_vjp
def _seg(x, bd):
    return _seg_sum(x, bd)


_seg.defvjp(lambda x, bd: (_seg_sum(x, bd), bd), lambda bd, ct: (_seg_sum(ct, bd), jnp.zeros_like(bd)))


def _colsum(x):
    return jnp.sum(x, axis=0, keepdims=True)


def _prescan_math(r, k, xw, xa, xg, k_k, w0f, w0b, a0f, a0b, kaf, kab, wupf, wupb, aupf, aupb, gup, bd):
    kkr = k * k_k
    norm = jnp.sqrt(_seg(kkr * kkr, bd))
    kk = kkr / jnp.maximum(norm, NORM_EPS)
    th = jnp.tanh(xw)

    def direction(w0, wup, a0, aup, ka):
        logit = w0 + jnp.dot(th, wup, preferred_element_type=F32)
        w = jnp.exp(-LOG_DECAY_SCALE * jax.nn.sigmoid(logit))
        a = jax.nn.sigmoid(a0 + jnp.dot(xa, aup, preferred_element_type=F32))
        kd = k * (1.0 + (a - 1.0) * ka)
        return w, kd, kk * a

    wf, kdf, bf = direction(w0f, wupf, a0f, aupf, kaf)
    wb, kdb, bb = direction(w0b, wupb, a0b, aupb, kab)
    g = jnp.dot(jax.nn.sigmoid(xg), gup, preferred_element_type=F32)
    return kk, r, wf, wb, bf, bb, kdf, kdb, g


def _postscan_math(y, r, v, kdf, kdb, g, gn_w, gn_b, rkf, rkb, bd):
    mean = _seg(y, bd) * (1.0 / HEAD)
    yc = y - mean
    var = _seg(yc * yc, bd) * (1.0 / HEAD)
    yg = yc * lax.rsqrt(var + GN_EPS) * gn_w + gn_b
    bonus = (_seg(r * kdf * rkf, bd) + _seg(r * kdb * rkb, bd)) * v
    return (yg + bonus) * g


def _halo_specs(width, col_blk, tb, t):
    nb = t // SUBLANES
    step = tb // SUBLANES
    main = pl.BlockSpec((tb, width), lambda i: (i, col_blk))
    prev = pl.BlockSpec((SUBLANES, width), lambda i: (jnp.maximum(i * step - 1, 0), col_blk))
    nxt = pl.BlockSpec((SUBLANES, width), lambda i: (jnp.minimum((i + 1) * step, nb - 1), col_blk))
    return [main, prev, nxt]


def _neighbours(z, prev8, next8, first, last):
    tb = z.shape[0]
    row = lax.broadcasted_iota(jnp.int32, z.shape, 0)
    prow = jnp.where(first, 0.0, prev8[SUBLANES - 1:SUBLANES, :])
    nrow = jnp.where(last, 0.0, next8[0:1, :])
    down = jnp.where(row == 0, prow, pltpu.roll(z, 1, 0))
    up = jnp.where(row == tb - 1, nrow, pltpu.roll(z, tb - 1, 0))
    return down, up


def _shift_conv_fwd(p, mu, conv_w, seq, *, name, tb=ROW_TILE):
    t = p.shape[0]
    per_seq = seq // tb

    def kern(p_ref, pp_ref, pn_ref, mu_ref, cw_ref, pss_ref, oc_ref):
        i = pl.program_id(0)
        first = (i % per_seq) == 0
        last = (i % per_seq) == per_seq - 1
        ps = p_ref[:, :D_SP]
        down, up = _neighbours(ps, pp_ref[:, :D_SP], pn_ref[:, :D_SP], first, last)
        pss_ref[...] = ps + mu_ref[...] * (0.5 * (down + up) - ps)
        gb = p_ref[:, D_SP:D_SP + D_CONV]
        u = p_ref[:, D_SP + D_CONV:D_SP + 2 * D_CONV] * p_ref[:, D_SP + 2 * D_CONV:]
        u_p = pp_ref[:, D_SP + D_CONV:D_SP + 2 * D_CONV] * pp_ref[:, D_SP + 2 * D_CONV:]
        u_n = pn_ref[:, D_SP + D_CONV:D_SP + 2 * D_CONV] * pn_ref[:, D_SP + 2 * D_CONV:]
        udown, uup = _neighbours(u, u_p, u_n, first, last)
        oc_ref[...] = gb * (cw_ref[0:1, :] * udown + cw_ref[1:2, :] * u + cw_ref[2:3, :] * uup)

    return pl.pallas_call(
        kern,
        out_shape=[jax.ShapeDtypeStruct((t, D_SP), F32), jax.ShapeDtypeStruct((t, D_CONV), F32)],
        grid=(t // tb,),
        in_specs=_halo_specs(D_INP, 0, tb, t) + [pl.BlockSpec((1, D_SP), lambda i: (0, 0)),
                                                 pl.BlockSpec((SUBLANES, D_CONV), lambda i: (0, 0))],
        out_specs=[pl.BlockSpec((tb, D_SP), lambda i: (i, 0)), pl.BlockSpec((tb, D_CONV), lambda i: (i, 0))],
        compiler_params=_params(("parallel",)), name=name)(p, p, p, mu, conv_w)


def _shift_conv_bwd(p, d_pss, d_o, mu, conv_w, seq, *, name, tb=ROW_TILE):
    t = p.shape[0]
    per_seq = seq // tb

    def kern(p_ref, pp_ref, pn_ref, d_ref, dp_ref, dn_ref, do_ref, dop_ref, don_ref, mu_ref, cw_ref,
             out_ref, dmu_ref, dcw_ref):
        i = pl.program_id(0)
        first = (i % per_seq) == 0
        last = (i % per_seq) == per_seq - 1

        @pl.when(i == 0)
        def _():
            dmu_ref[...] = jnp.zeros_like(dmu_ref)
            dcw_ref[...] = jnp.zeros_like(dcw_ref)

        mu_v = mu_ref[...]
        ps = p_ref[:, :D_SP]
        down, up = _neighbours(ps, pp_ref[:, :D_SP], pn_ref[:, :D_SP], first, last)
        d = d_ref[...]
        ddown, dup = _neighbours(d, dp_ref[...], dn_ref[...], first, last)
        out_ref[:, :D_SP] = (d - mu_v * d + 0.5 * (mu_v * ddown + mu_v * dup)).astype(out_ref.dtype)
        dmu_ref[...] += _colsum(d * (0.5 * (down + up) - ps))

        def parts(ref):
            return (ref[:, D_SP:D_SP + D_CONV], ref[:, D_SP + D_CONV:D_SP + 2 * D_CONV],
                    ref[:, D_SP + 2 * D_CONV:])

        gb, gc, hh = parts(p_ref)
        gb_p, gc_p, hh_p = parts(pp_ref)
        gb_n, gc_n, hh_n = parts(pn_ref)
        u = gc * hh
        udown, uup = _neighbours(u, gc_p * hh_p, gc_n * hh_n, first, last)
        cw0, cw1, cw2 = cw_ref[0:1, :], cw_ref[1:2, :], cw_ref[2:3, :]
        do = do_ref[...]
        duc = do * gb
        ducdown, ducup = _neighbours(duc, dop_ref[...] * gb_p, don_ref[...] * gb_n, first, last)
        du = cw0 * ducup + cw1 * duc + cw2 * ducdown
        out_ref[:, D_SP:D_SP + D_CONV] = (do * (cw0 * udown + cw1 * u + cw2 * uup)).astype(out_ref.dtype)
        out_ref[:, D_SP + D_CONV:D_SP + 2 * D_CONV] = (du * hh).astype(out_ref.dtype)
        out_ref[:, D_SP + 2 * D_CONV:] = (du * gc).astype(out_ref.dtype)
        dcw_ref[0:1, :] += _colsum(duc * udown)
        dcw_ref[1:2, :] += _colsum(duc * u)
        dcw_ref[2:3, :] += _colsum(duc * uup)

    return pl.pallas_call(
        kern,
        out_shape=[jax.ShapeDtypeStruct((t, D_INP), BF16), jax.ShapeDtypeStruct((1, D_SP), F32),
                   jax.ShapeDtypeStruct((SUBLANES, D_CONV), F32)],
        grid=(t // tb,),
        in_specs=(_halo_specs(D_INP, 0, tb, t) + _halo_specs(D_SP, 0, tb, t) + _halo_specs(D_CONV, 1, tb, t)
                  + [pl.BlockSpec((1, D_SP), lambda i: (0, 0)),
                     pl.BlockSpec((SUBLANES, D_CONV), lambda i: (0, 0))]),
        out_specs=[pl.BlockSpec((tb, D_INP), lambda i: (i, 0)), pl.BlockSpec((1, D_SP), lambda i: (0, 0)),
                   pl.BlockSpec((SUBLANES, D_CONV), lambda i: (0, 0))],
        compiler_params=_params(("arbitrary",)), name=name)(p, p, p, d_pss, d_pss, d_pss, d_o, d_o, d_o, mu, conv_w)


N_CHAIN = 16
N_GROUP = LANES // N_CHAIN
V_HI = HEAD // SUBLANES
G_KK, G_R, G_W, G_B, G_KD = 0, 1, (2, 3), (4, 5), (6, 7)


K_HI = HEAD // SUBLANES


def _tree_sum(terms):
    terms = list(terms)
    while len(terms) > 1:
        terms = [a + b for a, b in zip(terms[::2], terms[1::2])]
    return terms[0]


def _kscan_specs(nc):
    same = lambda c: c
    mirror = lambda c: nc - 1 - c

    def k_spec(fn):
        return pl.BlockSpec((SCAN_CHUNK, HEAD, LANES), lambda c: (fn(c), 0, 0))

    def v_spec(fn):
        return pl.BlockSpec((SCAN_CHUNK, SUBLANES, LANES), lambda c: (fn(c), 0, 0))

    return same, mirror, k_spec, v_spec


ST_SHAPE = (2, K_HI, V_HI, SUBLANES, LANES)


def _lane_group_index():
    lane = lax.broadcasted_iota(jnp.int32, (SUBLANES, LANES), 1)
    return lax.shift_right_logical(lane, jnp.full_like(lane, 4))


def _spread_groups(x, grp):
    rolled = [x] + [pltpu.roll(x, s * N_CHAIN, 1) for s in range(1, N_GROUP)]
    out = []
    for j in range(N_GROUP):
        t = rolled[(0 - j) % N_GROUP]
        for g in range(1, N_GROUP):
            t = jnp.where(grp == g, rolled[(g - j) % N_GROUP], t)
        out.append(t)
    return out


def _gather_groups(tiles, grp):
    total = None
    for s in range(N_GROUP):
        b = tiles[s % N_GROUP]
        for g in range(1, N_GROUP):
            b = jnp.where(grp == g, tiles[(g + s) % N_GROUP], b)
        b = pltpu.roll(b, s * N_CHAIN, 1) if s else b
        total = b if total is None else total + b
    return total


def _lane_group_sum(x):
    return _tree_sum([x] + [pltpu.roll(x, k * N_CHAIN, 1) for k in range(1, N_GROUP)])


def _key_row(x_t, grp, kh):
    r = SUBLANES * grp + kh
    return jnp.broadcast_to(x_t[r:r + 1, :], (SUBLANES, LANES))


def _acc(total, term):
    return term if total is None else total + term


SA_SHAPE = (2, V_HI, SUBLANES, LANES)


def _scan_fwd(xall, v_c, *, gather=(), name):
    steps = xall.shape[0]
    nc = steps // SCAN_CHUNK
    same, mirror, k_spec, v_spec = _kscan_specs(nc)
    last = SCAN_CHUNK - 1
    n_x = len(gather)

    def kern(*refs):
        xf_ref, xb_ref, vf_ref, vb_ref = refs[:4]
        yf_ref, yb_ref, hist_ref, fin_ref, sa_ref = refs[4 + n_x:9 + n_x]
        st_ref = refs[9 + 2 * n_x]
        c = pl.program_id(0)

        def riders():
            return _exchange_copies(refs[4:4 + n_x], refs[9 + n_x:9 + 2 * n_x], 0, *refs[10 + 2 * n_x:])

        @pl.when(c == 0)
        def _():
            st_ref[...] = jnp.zeros_like(st_ref)
            if n_x:
                for cp in riders():
                    cp.start()

        hist_ref[0] = st_ref[...]
        grp = _lane_group_index()

        def body(i, put):
            j = last - i
            for d, (x_t, v_t, y_ref, at) in enumerate(((xf_ref[i], vf_ref[i], yf_ref, i),
                                                       (xb_ref[j], vb_ref[j], yb_ref, j))):
                v_b = _spread_groups(v_t, grp)
                part = [None] * V_HI
                for kh in range(K_HI):
                    kk_r = _key_row(x_t, G_KK, kh)
                    for vh in range(V_HI):
                        part[vh] = _acc(part[vh], hist_ref[i, d, kh, vh] * kk_r)
                sa = [_lane_group_sum(p) for p in part]
                for vh in range(V_HI):
                    sa_ref[i, d, vh] = sa[vh]
                y_p = [None] * V_HI
                for kh in range(K_HI):
                    r_r, w_r = _key_row(x_t, G_R, kh), _key_row(x_t, G_W[d], kh)
                    b_r, kd_r = _key_row(x_t, G_B[d], kh), _key_row(x_t, G_KD[d], kh)
                    for vh in range(V_HI):
                        new = hist_ref[i, d, kh, vh] * w_r - sa[vh] * b_r + v_b[vh] * kd_r
                        put(d, kh, vh, new)
                        y_p[vh] = _acc(y_p[vh], new * r_r)
                y_ref[at] = _gather_groups(y_p, grp)

        def step(i, carry):
            def put(d, kh, vh, val):
                hist_ref[i + 1, d, kh, vh] = val
            body(i, put)
            return carry

        lax.fori_loop(0, last, step, 0, unroll=SCAN_UNROLL)

        def put_carry(d, kh, vh, val):
            st_ref[d, kh, vh] = val

        body(last, put_carry)

        @pl.when(c == nc - 1)
        def _():
            fin_ref[...] = st_ref[...]
            if n_x:
                for cp in riders():
                    cp.wait()

    return pl.pallas_call(
        kern,
        out_shape=[jax.ShapeDtypeStruct((steps, SUBLANES, LANES), F32)] * 2
        + [jax.ShapeDtypeStruct((steps,) + ST_SHAPE, F32), jax.ShapeDtypeStruct(ST_SHAPE, F32),
           jax.ShapeDtypeStruct((steps,) + SA_SHAPE, F32)]
        + _exchange_out_shapes(gather, 0),
        grid=(nc,), in_specs=[k_spec(same), k_spec(mirror), v_spec(same), v_spec(mirror)] + _hbm_specs(n_x),
        out_specs=[v_spec(same), v_spec(mirror),
                   pl.BlockSpec((SCAN_CHUNK,) + ST_SHAPE, lambda c: (c, 0, 0, 0, 0, 0)),
                   pl.BlockSpec(ST_SHAPE, lambda c: (0, 0, 0, 0, 0)),
                   pl.BlockSpec((SCAN_CHUNK,) + SA_SHAPE, lambda c: (c, 0, 0, 0, 0))] + _hbm_specs(n_x),
        scratch_shapes=[pltpu.VMEM(ST_SHAPE, F32)] + (_exchange_sems(n_x) if n_x else []),
        compiler_params=_params(("arbitrary",)), name=name)(xall, xall, v_c, v_c, *gather)


def _scan_bwd(xall, v_c, dy_c, hist, fin, sa, *, exchange=(), name):
    steps = xall.shape[0]
    nc = steps // SCAN_CHUNK
    same, back, k_spec, v_spec = _kscan_specs(nc)
    last = SCAN_CHUNK - 1
    n_x = len(exchange)

    def kern(*refs):
        xf_ref, xb_ref, vf_ref, vb_ref, dyf_ref, dyb_ref, hist_ref, fin_ref, sa_ref = refs[:9]
        gf_ref, gb_ref, dvf_ref, dvb_ref = refs[9 + n_x:13 + n_x]
        ds_ref, after_ref = refs[13 + 2 * n_x:15 + 2 * n_x]
        c = pl.program_id(0)

        def riders():
            return _exchange_copies(refs[9:9 + n_x], refs[13 + n_x:13 + 2 * n_x], n_x, *refs[15 + 2 * n_x:])

        @pl.when(c == 0)
        def _():
            ds_ref[...] = jnp.zeros_like(ds_ref)
            after_ref[...] = fin_ref[...]
            if n_x:
                for cp in riders():
                    cp.start()

        grp = _lane_group_index()
        row = lax.broadcasted_iota(jnp.int32, (SUBLANES, LANES), 0)
        zero = jnp.zeros((SUBLANES, LANES), F32)

        def body(i, after):
            j = last - i
            for d, (x_t, v_t, dy_t, g_ref, dv_ref, at) in enumerate((
                    (xf_ref[i], vf_ref[i], dyf_ref[i], gf_ref, dvf_ref, i),
                    (xb_ref[j], vb_ref[j], dyb_ref[j], gb_ref, dvb_ref, j))):
                v_s, dy_s = _spread_groups(v_t, grp), _spread_groups(dy_t, grp)
                dsa_p, dv_p = [None] * V_HI, [None] * V_HI
                for kh in range(K_HI):
                    r_r = _key_row(x_t, G_R, kh)
                    b_r, kd_r = _key_row(x_t, G_B[d], kh), _key_row(x_t, G_KD[d], kh)
                    for vh in range(V_HI):
                        g = ds_ref[d, kh, vh] + dy_s[vh] * r_r
                        ds_ref[d, kh, vh] = g
                        dsa_p[vh] = _acc(dsa_p[vh], g * b_r)
                        dv_p[vh] = _acc(dv_p[vh], g * kd_r)
                dsa = [-_lane_group_sum(p) for p in dsa_p]
                sa = [sa_ref[i, d, vh] for vh in range(V_HI)]
                dv_ref[at] = _gather_groups(dv_p, grp)
                blocks = {G_KK: zero, G_R: zero, G_W[d]: zero, G_B[d]: zero, G_KD[d]: zero}
                for kh in range(K_HI):
                    w_r, kk_r = _key_row(x_t, G_W[d], kh), _key_row(x_t, G_KK, kh)
                    dkk = dr = dw = db = dkd = None
                    for vh in range(V_HI):
                        g, before = ds_ref[d, kh, vh], hist_ref[i, d, kh, vh]
                        dr = _acc(dr, after(d, kh, vh) * dy_s[vh])
                        dw = _acc(dw, g * before)
                        dkd = _acc(dkd, g * v_s[vh])
                        db = _acc(db, g * sa[vh])
                        dkk = _acc(dkk, before * dsa[vh])
                        ds_ref[d, kh, vh] = g * w_r + dsa[vh] * kk_r
                    for gi, a in ((G_KK, dkk), (G_R, dr), (G_W[d], dw), (G_B[d], -db), (G_KD[d], dkd)):
                        blocks[gi] = jnp.where(row == kh, _colsum(a), blocks[gi])
                for gi in range(N_GROUP):
                    g_ref[at, SUBLANES * gi:SUBLANES * (gi + 1), :] = blocks.get(gi, zero)

        body(last, lambda d, kh, vh: after_ref[d, kh, vh])

        def step(ii, carry):
            i = last - ii
            body(i, lambda d, kh, vh: hist_ref[i + 1, d, kh, vh])
            return carry

        lax.fori_loop(1, SCAN_CHUNK, step, 0, unroll=SCAN_UNROLL)
        after_ref[...] = hist_ref[0]

        if n_x:
            @pl.when(c == nc - 1)
            def _():
                for cp in riders():
                    cp.wait()

    return pl.pallas_call(
        kern,
        out_shape=[jax.ShapeDtypeStruct((steps, HEAD, LANES), F32)] * 2
        + [jax.ShapeDtypeStruct((steps, SUBLANES, LANES), F32)] * 2 + _exchange_out_shapes(exchange, n_x),
        grid=(nc,),
        in_specs=[k_spec(back), k_spec(same), v_spec(back), v_spec(same), v_spec(back), v_spec(same),
                  pl.BlockSpec((SCAN_CHUNK,) + ST_SHAPE, lambda c: (back(c), 0, 0, 0, 0, 0)),
                  pl.BlockSpec(ST_SHAPE, lambda c: (0, 0, 0, 0, 0)),
                  pl.BlockSpec((SCAN_CHUNK,) + SA_SHAPE, lambda c: (back(c), 0, 0, 0, 0))] + _hbm_specs(n_x),
        out_specs=[k_spec(back), k_spec(same), v_spec(back), v_spec(same)] + _hbm_specs(n_x),
        scratch_shapes=[pltpu.VMEM(ST_SHAPE, F32), pltpu.VMEM(ST_SHAPE, F32)]
        + (_exchange_sems(n_x) if n_x else []),
        compiler_params=_params(("arbitrary",)), name=name)(xall, xall, v_c, v_c, dy_c, dy_c, hist, fin, sa,
                                                            *exchange)


def _to_key_rows(wide, bsz, seq):
    z = wide.reshape(bsz, seq, N_GROUP, N_HEAD, K_HI, SUBLANES).transpose(1, 2, 4, 5, 0, 3)
    return z.reshape(seq, HEAD, LANES)


def _from_key_rows(g, bsz, seq):
    z = g.reshape(seq, N_GROUP, K_HI, SUBLANES, bsz, N_HEAD).transpose(4, 0, 1, 5, 2, 3)
    return z.reshape(bsz * seq, N_GROUP * D_RWKV)


def _to_value_rows(a, bsz, seq):
    z = a.reshape(bsz, seq, N_HEAD, V_HI, SUBLANES).transpose(1, 4, 3, 0, 2)
    return z.reshape(seq, SUBLANES, LANES)


def _from_value_rows(y, bsz, seq):
    z = y.reshape(seq, SUBLANES, V_HI, bsz, N_HEAD).transpose(3, 0, 4, 2, 1)
    return z.reshape(bsz * seq, D_RWKV)


def _pad_cols(a, segs):
    out, off = [], 0
    for w, wp in segs:
        out.append(a[..., off:off + w])
        if wp > w:
            out.append(jnp.zeros(a.shape[:-1] + (wp - w,), a.dtype))
        off += w
    return jnp.concatenate(out, axis=-1)


def _unpad_cols(a, segs):
    out, off = [], 0
    for w, wp in segs:
        out.append(a[..., off:off + w])
        off += wp
    return jnp.concatenate(out, axis=-1)


P_SEGS = ((3 * D_RWKV, 3 * D_RWKV), (D_LORA, 128), (D_LORA, 128), (D_GATE, 256), (3 * D_CONV, 3 * D_CONV))
S_SEGS = P_SEGS[:4]


def _pad_rows(a, rows):
    return jnp.concatenate([a, jnp.zeros((rows - a.shape[0], a.shape[1]), a.dtype)], axis=0)


LATE = ("w_out", "w_gate", "w_up", "w_down")


def _local_step(x, target, w, late=None):
    bsz, seq, _ = x.shape
    t = bsz * seq
    x2d = x.reshape(t, D_MODEL)
    tg2d = target.reshape(t, D_MODEL)
    row = lambda a: a.reshape(1, -1).astype(F32)

    w_in = _pad_cols(w["w_in"][0], P_SEGS)
    mu = _pad_cols(row(w["mu_shift"]), S_SEGS)
    wupf, wupb, aupf, aupb = (_pad_rows(w[n][0].astype(F32), 128) for n in ("w_up_f", "w_up_b", "a_up_f", "a_up_b"))
    gup = _pad_rows(w["g_up"][0].astype(F32), 256)
    conv_w = _pad_rows(w["conv_w"][0].astype(F32), SUBLANES)
    norm1, norm2, normf = row(w["norm1_w"]), row(w["norm2_w"]), row(w["norm_f_w"])
    vec = {n: row(w[n]) for n in VEC}
    head_of = jnp.arange(LANES) // HEAD
    bd = (head_of[:, None] == head_of[None, :]).astype(F32)
    pre_consts = [vec["k_k"], vec["w0_f"], vec["w0_b"], vec["a0_f"], vec["a0_b"], vec["k_a_f"], vec["k_a_b"],
                  wupf, wupb, aupf, aupb, gup, bd]
    post_consts = [vec["gn_w"], vec["gn_b"], vec["r_k_f"], vec["r_k_b"], bd]

    h1, = _rowwise(_rms, [x2d], [norm1], [D_MODEL], [], name="rms1_fwd", out_dtype=BF16, tb=WIDE_TILE)
    p = _mm(h1, w_in, name="mm_in")
    pss, oconv = _shift_conv_fwd(p, mu, conv_w, seq, name="shift_conv_fwd")
    pre_rows = [(pss, 0, 512), (pss, 1, 512), (pss, XW0 // 128, 128), (pss, XA0 // 128, 128), (pss, XG0 // 256, 256)]
    sc, g = _rowwise(_prescan_math, pre_rows, pre_consts, [[D_RWKV] * N_GROUP, D_RWKV], [], name="prescan_fwd")
    xall = _to_key_rows(sc, bsz, seq)
    v_l = _to_value_rows(pss[:, 2 * D_RWKV:3 * D_RWKV], bsz, seq)
    y_f, y_b, hist, fin, sa, *gathered = _scan_fwd(xall, v_l, gather=[late[n] for n in LATE] if late else (),
                                                   name="scan_fwd")
    w_out, w_gate, w_up, w_down = (
        (_from_slots(a, SHARD_AXIS[n]) if late else w[n])[0] for n, a in zip(LATE, gathered or LATE))
    y = _from_value_rows(y_f + y_b, bsz, seq)
    post_rows = [y, (pss, 0, 512), (pss, 2, 512), (sc, G_KD[0], 512), (sc, G_KD[1], 512), g]

    def post_fwd(y_, r_, v_, kdf_, kdb_, g_, oc_, *consts):
        return _postscan_math(y_, r_, v_, kdf_, kdb_, g_, *consts), oc_

    o, = _rowwise(post_fwd, post_rows + [oconv], post_consts, [[D_RWKV, D_CONV]], [], name="postscan_fwd",
                  out_dtype=BF16)
    x1 = _mm(o, w_out, add=x2d, name="mm_out")
    h2, = _rowwise(_rms, [x1], [norm2], [D_MODEL], [], name="rms2_fwd", out_dtype=BF16, tb=WIDE_TILE)
    gg, uu, ff = _mm_swiglu(h2, w_gate, w_up, name="mm_gate_up")
    x2 = _mm(ff, w_down, add=x1, name="mm_down")

    def final(x_, tg_, wn_):
        yo, vjp = jax.vjp(_rms, x_, wn_)
        err = yo - tg_
        dx_, dwn_ = vjp(err * (1.0 / D_MODEL))
        part = jnp.sum(jnp.sum(err * err, axis=1, keepdims=True), axis=0, keepdims=True) * (0.5 / D_MODEL)
        return dx_, part + jnp.zeros((1, LANES), F32), dwn_

    dx2, loss_acc, d_normf = _rowwise(final, [x2, tg2d], [normf], [D_MODEL], [(1, LANES), (1, D_MODEL)],
                                      name="loss_head", tb=WIDE_TILE)
    dgg, duu = _mm_swiglu_bwd(dx2, w_down, gg, uu, name="mm_down_dx")
    g_w_down = _mm(ff, dx2, ta=True, name="mm_down_dw")
    dh2 = _mm(dgg, w_gate, tb=True, name="mm_gate_dx")
    dh2 = _mm(duu, w_up, tb=True, add=dh2, name="mm_up_dx")
    g_w_gate = _mm(h2, dgg, ta=True, name="mm_gate_dw")
    g_w_up = _mm(h2, duu, ta=True, name="mm_up_dw")

    def rms_bwd(x_, dh_, dres_, wn_):
        _, vjp = jax.vjp(_rms, x_, wn_)
        dx_, dwn_ = vjp(dh_)
        return dx_ + dres_, dwn_

    dx1, d_norm2 = _rowwise(rms_bwd, [x1, dh2, dx2], [norm2], [D_MODEL], [(1, D_MODEL)], name="rms2_bwd", tb=WIDE_TILE)
    do = _mm(dx1, w_out, tb=True, name="mm_out_dx")
    g_w_out = _mm(o, dx1, ta=True, name="mm_out_dw")

    def post_bwd(y_, r_, v_, kdf_, kdb_, g_, do_, *consts):
        _, vjp = jax.vjp(lambda *a: _postscan_math(*a, consts[4]), y_, r_, v_, kdf_, kdb_, g_, *consts[:4])
        return vjp(do_)

    (dy, dr_c, dv_c, dkdf_c, dkdb_c, dg, d_gn_w, d_gn_b, d_rkf, d_rkb) = _rowwise(
        post_bwd, post_rows + [(do, 0, 512)], post_consts, [D_RWKV] * 6, [(1, D_RWKV)] * 4, name="postscan_bwd")
    dy_l = _to_value_rows(dy, bsz, seq)
    late_grads = {"w_out": g_w_out[None], "w_gate": g_w_gate[None], "w_up": g_w_up[None], "w_down": g_w_down[None]}
    g_f, g_b, dv_f, dv_b, *late_parts = _scan_bwd(
        xall, v_l, dy_l, hist, fin, sa, name="scan_bwd",
        exchange=[_to_slots(late_grads[n], SHARD_AXIS[n]).astype(BF16) for n in LATE] if late else ())
    dsc = _from_key_rows(g_f + g_b, bsz, seq)
    dv_s = _from_value_rows(dv_f + dv_b, bsz, seq)

    def pre_bwd(r_, k_, xw_, xa_, xg_, dkk_, dr_s, dwf_, dwb_, dbf_, dbb_, dkdf_s, dkdb_s,
                dr_c_, dv_c_, dv_s_, dkdf_c_, dkdb_c_, dg_, *consts):
        _, vjp = jax.vjp(lambda *a: _prescan_math(*a, consts[-1]), r_, k_, xw_, xa_, xg_, *consts[:-1])
        grads = vjp((dkk_, dr_s + dr_c_, dwf_, dwb_, dbf_, dbb_, dkdf_s + dkdf_c_, dkdb_s + dkdb_c_, dg_))
        dr_, dk_, dxw_, dxa_, dxg_ = grads[:5]
        return (dr_, dk_, dv_c_ + dv_s_, dxw_, dxa_, dxg_) + tuple(grads[5:])

    pre_b_rows = (pre_rows + [(dsc, j, 512) for j in range(N_GROUP)]
                  + [dr_c, dv_c, dv_s, dkdf_c, dkdb_c, dg])
    pre_b = _rowwise(pre_bwd, pre_b_rows, pre_consts, [[512, 512, 512, 128, 128, 256]],
                     [(1, D_RWKV)] * 7 + [(128, D_RWKV)] * 4 + [(256, D_RWKV)], name="prescan_bwd")
    d_pss = pre_b[0]
    d_kk_, d_w0f, d_w0b, d_a0f, d_a0b, d_kaf, d_kab, d_wupf, d_wupb, d_aupf, d_aupb, d_gup = pre_b[1:]
    dp, d_mu, d_conv = _shift_conv_bwd(p, d_pss, do, mu, conv_w, seq, name="shift_conv_bwd")
    g_w_in = _mm(h1, dp, ta=True, name="mm_in_dw")
    grads = {
        "w_in": _unpad_cols(g_w_in, P_SEGS)[None], "mu_shift": _unpad_cols(d_mu, S_SEGS),
        "w_up_f": d_wupf[None, :D_LORA], "w0_f": d_w0f, "w_up_b": d_wupb[None, :D_LORA], "w0_b": d_w0b,
        "a_up_f": d_aupf[None, :D_LORA], "a0_f": d_a0f, "a_up_b": d_aupb[None, :D_LORA], "a0_b": d_a0b,
        "g_up": d_gup[None, :D_GATE], "k_k": d_kk_, "k_a_f": d_kaf, "k_a_b": d_kab,
        "r_k_f": d_rkf, "r_k_b": d_rkb, "gn_w": d_gn_w, "gn_b": d_gn_b, "conv_w": d_conv[None, :3],
        "w_out": g_w_out[None], "norm2_w": d_norm2, "w_gate": g_w_gate[None], "w_up": g_w_up[None],
        "w_down": g_w_down[None], "norm_f_w": d_normf,
    }
    early = ("w_in",) + LORA
    parts = dict(zip(LATE, late_parts))
    if late:
        vec_rows = jnp.concatenate([grads[n] for n in VEC] + [jnp.zeros((16 - len(VEC), D_RWKV), F32)], axis=0)
        slots = [_to_slots(grads[n], SHARD_AXIS[n]).astype(BF16 if n in BIG else F32) for n in early]
        dh1, *recv = _mm(dp, w_in, tb=True, exchange=(slots, [vec_rows]), name="mm_in_dx")
        parts.update(zip(early + ("vec",), recv))
    else:
        dh1 = _mm(dp, w_in, tb=True, name="mm_in_dx")
    dx, grads["norm1_w"] = _rowwise(rms_bwd, [x2d, dh1, dx1], [norm1], [D_MODEL], [(1, D_MODEL)], name="rms1_bwd",
                                    tb=WIDE_TILE)
    return loss_acc, dx.reshape(bsz, seq, D_MODEL), grads, parts


def _hbm_specs(n):
    return [pl.BlockSpec(memory_space=pl.ANY)] * n


def _all_gather(arrs, *, name):
    n = len(arrs)

    def body(*refs):
        x_refs, out_refs = refs[:n], refs[n:2 * n]
        send_sems, recv_sems, local_sems = refs[2 * n:]
        x, y, c = lax.axis_index("x"), lax.axis_index("y"), lax.axis_index("c")
        me, sibling = (x, y, c), (x, y, 1 - c)
        chips = [(1 - x, y), (x, 1 - y), (1 - x, 1 - y)]

        def slot(a, px, py, pc):
            return out_refs[a].at[4 * px + 2 * py + pc]

        def copy(a, k, block, to, src=None):
            return pltpu.make_async_remote_copy(
                src_ref=slot(a, *block) if src is None else src, dst_ref=slot(a, *block),
                send_sem=send_sems.at[k, a], recv_sem=recv_sems.at[k, a],
                device_id=to, device_id_type=pl.DeviceIdType.MESH)

        mine = [pltpu.make_async_copy(x_refs[a], slot(a, *me), local_sems.at[a]) for a in range(n)]
        for cp in mine:
            cp.start()
        first = []
        for a in range(n):
            first.append(copy(a, 0, me, sibling, src=x_refs[a]))
            first += [copy(a, 1 + j, me, (*chip, c), src=x_refs[a]) for j, chip in enumerate(chips)]
        for cp in first:
            cp.start()
        passed = []
        for j, chip in enumerate(chips):
            for a in range(n):
                copy(a, 1 + j, (*chip, c), me).wait_recv()
                cp = copy(a, 4 + j, (*chip, c), sibling)
                cp.start()
                passed.append(cp)
        for a in range(n):
            copy(a, 0, sibling, me).wait_recv()
            for j, chip in enumerate(chips):
                copy(a, 4 + j, (*chip, 1 - c), me).wait_recv()
        for cp in first + passed:
            cp.wait_send()
        for cp in mine:
            cp.wait()

    return pl.pallas_call(
        body, out_shape=[jax.ShapeDtypeStruct((N_DEV,) + a.shape, a.dtype) for a in arrs],
        in_specs=_hbm_specs(n), out_specs=_hbm_specs(n),
        scratch_shapes=[pltpu.SemaphoreType.DMA((7, n)), pltpu.SemaphoreType.DMA((7, n)),
                        pltpu.SemaphoreType.DMA((n,))],
        name=name)(*arrs)


def _exchange(sliced, whole, *, name):
    arrs = list(sliced) + list(whole)
    n, n_sliced = len(arrs), len(sliced)

    def body(*refs):
        copies = _exchange_copies(refs[:n], refs[n:2 * n], n_sliced, *refs[2 * n:])
        for cp in copies:
            cp.start()
        for cp in copies:
            cp.wait()

    return pl.pallas_call(
        body, out_shape=_exchange_out_shapes(arrs, n_sliced), in_specs=_hbm_specs(n), out_specs=_hbm_specs(n),
        scratch_shapes=_exchange_sems(n), name=name)(*arrs)


def _exchange_out_shapes(arrs, n_sliced):
    return [jax.ShapeDtypeStruct(a.shape if i < n_sliced else (N_DEV,) + a.shape, a.dtype)
            for i, a in enumerate(arrs)]


def _exchange_sems(n):
    return [pltpu.SemaphoreType.DMA((7, n)), pltpu.SemaphoreType.DMA((7, n)), pltpu.SemaphoreType.DMA((n,))]


def _exchange_copies(in_refs, out_refs, n_sliced, send_sems, recv_sems, local_sems):
    n = len(in_refs)
    x, y, c = lax.axis_index("x"), lax.axis_index("y"), lax.axis_index("c")
    me = 4 * x + 2 * y + c

    def src(a, dev):
        return in_refs[a].at[dev] if a < n_sliced else in_refs[a]

    copies = [pltpu.make_async_copy(src(a, me), out_refs[a].at[me], local_sems.at[a]) for a in range(n)]
    for k in range(1, N_DEV):
        px = 1 - x if k & 4 else x
        py = 1 - y if k & 2 else y
        pc = 1 - c if k & 1 else c
        for a in range(n):
            copies.append(pltpu.make_async_remote_copy(
                src_ref=src(a, 4 * px + 2 * py + pc), dst_ref=out_refs[a].at[me],
                send_sem=send_sems.at[k - 1, a], recv_sem=recv_sems.at[k - 1, a],
                device_id=(px, py, pc), device_id_type=pl.DeviceIdType.MESH))
    return copies


def _adam_math(g, w, m, v):
    nm = ADAM_B1 * m + (1.0 - ADAM_B1) * g
    nv = ADAM_B2 * v + (1.0 - ADAM_B2) * (g * g)
    m_hat = nm / (1.0 - ADAM_B1 ** ADAM_STEP)
    v_hat = nv / (1.0 - ADAM_B2 ** ADAM_STEP)
    return -ADAM_LR * (m_hat / (jnp.sqrt(v_hat) + ADAM_EPS) + ADAM_WD * w), nm, nv


def _slot_sum(ref):
    g = ref[0].astype(F32)
    for s in range(1, N_DEV):
        g = g + ref[s].astype(F32)
    return g


def _adamw_big(parts, w, m, v, *, name):
    _, rws, cols = w.shape
    tr = _tile(rws, (256, 176, 128))

    def kern(p_ref, w_ref, m_ref, v_ref, g_ref, d_ref, nm_ref, nv_ref):
        g = _slot_sum(p_ref)
        g_ref[...] = g
        d_ref[...], nm_ref[...], nv_ref[...] = _adam_math(g, w_ref[...], m_ref[...], v_ref[...])

    spec = pl.BlockSpec((1, tr, cols), lambda i: (0, i, 0))
    return pl.pallas_call(
        kern, out_shape=[jax.ShapeDtypeStruct(w.shape, F32)] * 4, grid=(rws // tr,),
        in_specs=[pl.BlockSpec((N_DEV, 1, tr, cols), lambda i: (0, 0, i, 0)), spec, spec, spec],
        out_specs=[spec] * 4, compiler_params=_params(("parallel",)), name=name)(parts, w, m, v)


def _adamw_small(lora_parts, vec_parts, wide_parts, wmv, *, name):
    names = LORA + VEC + WIDE
    n_l, n = len(LORA), len(names)
    flat = [a for trip in wmv for a in trip]

    def kern(*refs):
        l_refs, vec_ref, wide_ref = refs[:n_l], refs[n_l], refs[n_l + 1]
        in_refs = refs[n_l + 2:n_l + 2 + 3 * n]
        out_refs = refs[n_l + 2 + 3 * n:]
        vec_sum, wide_sum = _slot_sum(vec_ref), _slot_sum(wide_ref)
        for i, nm in enumerate(names):
            w_ref, m_ref, v_ref = in_refs[3 * i:3 * i + 3]
            if i < n_l:
                g = _slot_sum(l_refs[i])
            elif nm in VEC:
                g = vec_sum[i - n_l:i - n_l + 1, :]
            else:
                g = wide_sum[WIDE.index(nm):WIDE.index(nm) + 1, :w_ref.shape[-1]]
            o = out_refs[4 * i:4 * i + 4]
            o[0][...] = g
            o[1][...], o[2][...], o[3][...] = _adam_math(g, w_ref[...], m_ref[...], v_ref[...])

    out_shape = [jax.ShapeDtypeStruct(trip[0].shape, F32) for trip in wmv for _ in range(4)]
    outs = pl.pallas_call(kern, out_shape=out_shape, name=name,
                          compiler_params=pltpu.CompilerParams(vmem_limit_bytes=VMEM_LIMIT))(
        *lora_parts, vec_parts, wide_parts, *flat)
    return [tuple(outs[4 * i:4 * i + 4]) for i in range(n)]


def _to_slots(g, axis):
    _, rws, cols = g.shape
    if axis == 1:
        return g.reshape(N_DEV, 1, rws // N_DEV, cols)
    return g.reshape(1, rws, N_DEV, cols // N_DEV).transpose(2, 0, 1, 3)


def _from_slots(got, axis):
    _, _, rws, cols = got.shape
    if axis == 1:
        return got.reshape(1, N_DEV * rws, cols)
    return got.transpose(1, 2, 0, 3).reshape(1, rws, N_DEV * cols)


def _pad_lanes(a, width):
    return jnp.concatenate([a, jnp.zeros(a.shape[:-1] + (width - a.shape[-1],), a.dtype)], axis=-1)


def kernel(x, norm1_w, w_in, mu_shift, w_up_f, w0_f, w_up_b, w0_b, a_up_f, a0_f, a_up_b, a0_b, g_up, k_k, k_a_f, k_a_b, r_k_f, r_k_b, gn_w, gn_b, conv_w, w_out, norm2_w, w_gate, w_up, w_down, norm_f_w, loss_target, m_norm1_w, m_w_in, m_mu_shift, m_w_up_f, m_w0_f, m_w_up_b, m_w0_b, m_a_up_f, m_a0_f, m_a_up_b, m_a0_b, m_g_up, m_k_k, m_k_a_f, m_k_a_b, m_r_k_f, m_r_k_b, m_gn_w, m_gn_b, m_conv_w, m_w_out, m_norm2_w, m_w_gate, m_w_up, m_w_down, m_norm_f_w, v_norm1_w, v_w_in, v_mu_shift, v_w_up_f, v_w0_f, v_w_up_b, v_w0_b, v_a_up_f, v_a0_f, v_a_up_b, v_a0_b, v_g_up, v_k_k, v_k_a_f, v_k_a_b, v_r_k_f, v_r_k_b, v_gn_w, v_gn_b, v_conv_w, v_w_out, v_norm2_w, v_w_gate, v_w_up, v_w_down, v_norm_f_w):
    local = dict(norm1_w=norm1_w, w_in=w_in, mu_shift=mu_shift, w_up_f=w_up_f, w0_f=w0_f, w_up_b=w_up_b,
                 w0_b=w0_b, a_up_f=a_up_f, a0_f=a0_f, a_up_b=a_up_b, a0_b=a0_b, g_up=g_up, k_k=k_k, k_a_f=k_a_f,
                 k_a_b=k_a_b, r_k_f=r_k_f, r_k_b=r_k_b, gn_w=gn_w, gn_b=gn_b, conv_w=conv_w, w_out=w_out,
                 norm2_w=norm2_w, w_gate=w_gate, w_up=w_up, w_down=w_down, norm_f_w=norm_f_w)
    mom_m = dict(norm1_w=m_norm1_w, w_in=m_w_in, mu_shift=m_mu_shift, w_up_f=m_w_up_f, w0_f=m_w0_f,
                 w_up_b=m_w_up_b, w0_b=m_w0_b, a_up_f=m_a_up_f, a0_f=m_a0_f, a_up_b=m_a_up_b, a0_b=m_a0_b,
                 g_up=m_g_up, k_k=m_k_k, k_a_f=m_k_a_f, k_a_b=m_k_a_b, r_k_f=m_r_k_f, r_k_b=m_r_k_b,
                 gn_w=m_gn_w, gn_b=m_gn_b, conv_w=m_conv_w, w_out=m_w_out, norm2_w=m_norm2_w, w_gate=m_w_gate,
                 w_up=m_w_up, w_down=m_w_down, norm_f_w=m_norm_f_w)
    mom_v = dict(norm1_w=v_norm1_w, w_in=v_w_in, mu_shift=v_mu_shift, w_up_f=v_w_up_f, w0_f=v_w0_f,
                 w_up_b=v_w_up_b, w0_b=v_w0_b, a_up_f=v_a_up_f, a0_f=v_a0_f, a_up_b=v_a_up_b, a0_b=v_a0_b,
                 g_up=v_g_up, k_k=v_k_k, k_a_f=v_k_a_f, k_a_b=v_k_a_b, r_k_f=v_r_k_f, r_k_b=v_r_k_b,
                 gn_w=v_gn_w, gn_b=v_gn_b, conv_w=v_conv_w, w_out=v_w_out, norm2_w=v_norm2_w, w_gate=v_w_gate,
                 w_up=v_w_up, w_down=v_w_down, norm_f_w=v_norm_f_w)

    early = ("w_in",) + LORA
    got = _all_gather([local["w_in"].astype(BF16)] + [local[n] for n in LORA], name="gather")
    full = dict(local)
    full.update({n: _from_slots(a, SHARD_AXIS[n]) for n, a in zip(early, got)})

    loss_part, grad_x, grads, parts = _local_step(x, loss_target, full,
                                                  late={n: local[n].astype(BF16) for n in LATE})

    wide_rows = jnp.concatenate([_pad_lanes(a, WIDE_ROW) for a in [grads[n] for n in WIDE] + [loss_part]]
                                + [jnp.zeros((SUBLANES - len(WIDE) - 1, WIDE_ROW), F32)], axis=0)
    wide_parts, = _exchange([], [wide_rows], name="grad_exchange")
    loss = jnp.sum(wide_parts[:, len(WIDE), 0])
    out = {}
    for n in BIG:
        out[n] = _adamw_big(parts[n], local[n], mom_m[n], mom_v[n], name="adamw_" + n)

    def small_form(n, a):
        if n in LORA:
            return a
        a = a.reshape(1, -1)
        return _pad_lanes(a, WIDE_ROW) if n == "mu_shift" else a

    small = LORA + VEC + WIDE
    res = _adamw_small([parts[n] for n in LORA], parts["vec"], wide_parts,
                       [tuple(small_form(n, d[n]) for d in (local, mom_m, mom_v)) for n in small],
                       name="adamw_small")
    for n, quad in zip(small, res):
        out[n] = tuple(a[..., :local[n].size].reshape(local[n].shape) if n not in LORA else a for a in quad)
    return (loss, grad_x, *[out[n][i] for i in range(4) for n in WEIGHTS])
```

```python
import functools

import jax
import jax.numpy as jnp
from jax import lax
from jax.experimental import pallas as pl
from jax.experimental.pallas import tpu as pltpu

F32 = jnp.float32
BF16 = jnp.bfloat16
HIGHEST = lax.Precision.HIGHEST

N_DEV = 8
D_MODEL = 1024
D_RWKV = 512
D_CONV = 512
HEAD = 64
N_HEAD = D_RWKV // HEAD
D_LORA = 64
D_GATE = 160
D_SHIFTED = 3 * D_RWKV + 2 * D_LORA + D_GATE
XW0, XA0, XG0 = 1536, 1664, 1792
D_SP = 2048
D_INP = D_SP + 3 * D_CONV
LOG_DECAY_SCALE = 0.606531
RMS_EPS = 1e-6
GN_EPS = 64e-5
NORM_EPS = 1e-12
ADAM_LR, ADAM_B1, ADAM_B2, ADAM_EPS, ADAM_WD, ADAM_STEP = 0.001, 0.9, 0.999, 1e-08, 0.01, 10

LANES = 128
SUBLANES = 8
VMEM_LIMIT = 48 * 1024 * 1024
SCAN_CHUNK = 16
SCAN_UNROLL = 3
ROW_TILE = 128
WIDE_TILE = 256

BIG = ("w_in", "w_out", "w_gate", "w_up", "w_down")
TRANSPOSED = ("w_in", "w_gate", "w_up")
LORA = ("w_up_f", "w_up_b", "a_up_f", "a_up_b", "g_up", "conv_w")
SHARD_AXIS = {"w_in": 2, "w_out": 1, "w_gate": 2, "w_up": 2, "w_down": 1, "w_up_f": 2, "w_up_b": 2,
              "a_up_f": 2, "a_up_b": 2, "g_up": 2, "conv_w": 2}
VEC = ("w0_f", "w0_b", "a0_f", "a0_b", "k_k", "k_a_f", "k_a_b", "r_k_f", "r_k_b", "gn_w", "gn_b")
WIDE = ("mu_shift", "norm1_w", "norm2_w", "norm_f_w")
WIDE_ROW = 2048
WEIGHTS = ("norm1_w", "w_in", "mu_shift", "w_up_f", "w0_f", "w_up_b", "w0_b", "a_up_f", "a0_f", "a_up_b",
           "a0_b", "g_up", "k_k", "k_a_f", "k_a_b", "r_k_f", "r_k_b", "gn_w", "gn_b", "conv_w", "w_out",
           "norm2_w", "w_gate", "w_up", "w_down", "norm_f_w")


def _params(sem, limit=VMEM_LIMIT):
    return pltpu.CompilerParams(dimension_semantics=sem, vmem_limit_bytes=limit)


def _tile(n, cands):
    for c in cands:
        if n % c == 0:
            return c
    raise ValueError(f"no tile for {n}")


def _mm(a, b, *, ta=False, tb=False, add=None, exchange=None, name):
    (k_dim, m) = a.shape if ta else a.shape[::-1]
    (k2, n) = b.shape[::-1] if tb else b.shape
    assert k_dim == k2, (a.shape, b.shape, ta, tb)
    tm = _tile(m, (1408, 1024, 512, 256, 128))
    tn = _tile(n, (1408, 1024, 896, 512, 256, 128))
    tk = _tile(k_dim, (1408, 1024, 896, 512, 256, 128))
    nk = k_dim // tk
    grid = (m // tm, n // tn, nk)
    dims = (((0 if ta else 1,), (1 if tb else 0,)), ((), ()))
    sliced, whole = exchange or ((), ())
    riders = list(sliced) + list(whole)
    n_x, n_in = len(riders), 2 + (add is not None)

    def kern(*refs):
        a_ref, b_ref = refs[:2]
        add_ref = refs[2] if add is not None else None
        o_ref, acc_ref = refs[n_in + n_x], refs[n_in + 2 * n_x + 1]
        k = pl.program_id(2)
        step = (pl.program_id(0) * grid[1] + pl.program_id(1)) * nk + k

        def copies():
            return _exchange_copies(refs[n_in:n_in + n_x], refs[n_in + n_x + 1:n_in + 2 * n_x + 1], len(sliced),
                                    *refs[n_in + 2 * n_x + 2:])

        if n_x:
            @pl.when(step == 0)
            def _():
                for cp in copies():
                    cp.start()

        @pl.when(k == 0)
        def _():
            acc_ref[...] = jnp.zeros_like(acc_ref)

        acc_ref[...] += lax.dot_general(a_ref[...].astype(BF16), b_ref[...].astype(BF16), dims,
                                        preferred_element_type=F32)

        @pl.when(k == nk - 1)
        def _():
            if add is None:
                o_ref[...] = acc_ref[...]
            else:
                o_ref[...] = acc_ref[...] + add_ref[...]

        if n_x:
            @pl.when(step == grid[0] * grid[1] * nk - 1)
            def _():
                for cp in copies():
                    cp.wait()

    a_spec = (pl.BlockSpec((tk, tm), lambda i, j, k: (k, i)) if ta
              else pl.BlockSpec((tm, tk), lambda i, j, k: (i, k)))
    b_spec = (pl.BlockSpec((tn, tk), lambda i, j, k: (j, k)) if tb
              else pl.BlockSpec((tk, tn), lambda i, j, k: (k, j)))
    o_spec = pl.BlockSpec((tm, tn), lambda i, j, k: (i, j))
    in_specs = [a_spec, b_spec] + ([o_spec] if add is not None else []) + _hbm_specs(n_x)
    args = (a, b) + ((add,) if add is not None else ()) + tuple(riders)
    out = pl.pallas_call(
        kern, out_shape=[jax.ShapeDtypeStruct((m, n), F32)] + _exchange_out_shapes(riders, len(sliced)), grid=grid,
        in_specs=in_specs, out_specs=[o_spec] + _hbm_specs(n_x),
        scratch_shapes=[pltpu.VMEM((tm, tn), F32)] + (_exchange_sems(n_x) if n_x else []),
        compiler_params=_params(("arbitrary",) * 3 if n_x else ("parallel", "parallel", "arbitrary")),
        name=name)(*args)
    return out if n_x else out[0]


def _swiglu(g, u):
    return jax.nn.silu(g) * u


FFN_TN = 256


def _mm_swiglu(h, w_gate_t, w_up_t, *, name):
    m, k_dim = h.shape
    n = w_gate_t.shape[0]
    tm = _tile(m, (1024, 512, 256, 128))
    dims = (((1,), (1,)), ((), ()))

    def kern(h_ref, wg_ref, wu_ref, g_ref, u_ref, f_ref):
        hv = h_ref[...].astype(BF16)
        g = lax.dot_general(hv, wg_ref[...].astype(BF16), dims, preferred_element_type=F32)
        u = lax.dot_general(hv, wu_ref[...].astype(BF16), dims, preferred_element_type=F32)
        g_ref[...] = g
        u_ref[...] = u
        f_ref[...] = _swiglu(g, u).astype(f_ref.dtype)

    w_spec = pl.BlockSpec((FFN_TN, k_dim), lambda i, j: (j, 0))
    o_spec = pl.BlockSpec((tm, FFN_TN), lambda i, j: (i, j))
    return pl.pallas_call(
        kern, out_shape=[jax.ShapeDtypeStruct((m, n), F32)] * 2 + [jax.ShapeDtypeStruct((m, n), BF16)],
        grid=(m // tm, n // FFN_TN), in_specs=[pl.BlockSpec((tm, k_dim), lambda i, j: (i, 0)), w_spec, w_spec],
        out_specs=[o_spec] * 3, compiler_params=_params(("parallel", "parallel")), name=name)(h, w_gate_t, w_up_t)


def _mm_swiglu_bwd(dx, w_down, g, u, *, name):
    m, k_dim = dx.shape
    n = w_down.shape[0]
    tm = _tile(m, (1024, 512, 256, 128))

    def kern(dx_ref, w_ref, g_ref, u_ref, dg_ref, du_ref):
        df = lax.dot_general(dx_ref[...].astype(BF16), w_ref[...].astype(BF16), (((1,), (1,)), ((), ())),
                             preferred_element_type=F32)
        _, vjp = jax.vjp(_swiglu, g_ref[...], u_ref[...])
        dg, du = vjp(df)
        dg_ref[...] = dg.astype(dg_ref.dtype)
        du_ref[...] = du.astype(du_ref.dtype)

    o_spec = pl.BlockSpec((tm, FFN_TN), lambda i, j: (i, j))
    return pl.pallas_call(
        kern, out_shape=[jax.ShapeDtypeStruct((m, n), BF16)] * 2, grid=(m // tm, n // FFN_TN),
        in_specs=[pl.BlockSpec((tm, k_dim), lambda i, j: (i, 0)), pl.BlockSpec((FFN_TN, k_dim), lambda i, j: (j, 0)),
                  o_spec, o_spec],
        out_specs=[o_spec] * 2, compiler_params=_params(("parallel", "parallel")), name=name)(dx, w_down, g, u)


def _rowwise(fn, rows, consts, out_rows, out_accs, *, name, tb=ROW_TILE, out_dtype=F32):
    t = (rows[0][0] if isinstance(rows[0], tuple) else rows[0]).shape[0]
    n_r, n_c, n_o, n_a = len(rows), len(consts), len(out_rows), len(out_accs)
    pieces = [w if isinstance(w, (list, tuple)) else [w] for w in out_rows]

    def kern(*refs):
        r_refs = refs[:n_r]
        c_refs = refs[n_r:n_r + n_c]
        o_refs = refs[n_r + n_c:n_r + n_c + n_o]
        a_refs = refs[n_r + n_c + n_o:]
        vals = fn(*[r[...] for r in r_refs], *[c[...] for c in c_refs])
        vals = list(vals) if isinstance(vals, (tuple, list)) else [vals]
        pos = 0
        for o_ref, ws in zip(o_refs, pieces):
            off = 0
            for w in ws:
                o_ref[:, off:off + w] = vals[pos].astype(o_ref.dtype)
                off += w
                pos += 1
        if n_a:
            @pl.when(pl.program_id(0) == 0)
            def _():
                for a_ref in a_refs:
                    a_ref[...] = jnp.zeros_like(a_ref)
            for a_ref, v in zip(a_refs, vals[pos:]):
                a_ref[...] += v

    in_specs, args = [], []
    for r in rows:
        if isinstance(r, tuple):
            arr, blk, w = r
            in_specs.append(pl.BlockSpec((tb, w), functools.partial(lambda i, blk: (i, blk), blk=blk)))
        else:
            arr = r
            in_specs.append(pl.BlockSpec((tb, arr.shape[1]), lambda i: (i, 0)))
        args.append(arr)
    for c in consts:
        in_specs.append(pl.BlockSpec(c.shape, lambda i: (0, 0)))
        args.append(c)
    out_shape = [jax.ShapeDtypeStruct((t, sum(ws)), out_dtype) for ws in pieces]
    out_specs = [pl.BlockSpec((tb, sum(ws)), lambda i: (i, 0)) for ws in pieces]
    for shp in out_accs:
        out_shape.append(jax.ShapeDtypeStruct(shp, F32))
        out_specs.append(pl.BlockSpec(shp, lambda i: (0, 0)))
    res = pl.pallas_call(
        kern, out_shape=out_shape, grid=(t // tb,), in_specs=in_specs, out_specs=out_specs,
        compiler_params=_params(("arbitrary",) if n_a else ("parallel",)), name=name)(*args)
    return res


def _rms(x, w):
    return x * lax.rsqrt(jnp.mean(x * x, axis=-1, keepdims=True) + RMS_EPS) * w


def _seg_sum(x, bd):
    return jnp.concatenate(
        [jnp.dot(x[:, LANES * j:LANES * (j + 1)], bd, precision=HIGHEST, preferred_element_type=F32)
         for j in range(x.shape[1] // LANES)], axis=1)


@jax.custom_vjp
def _seg(x, bd):
    return _seg_sum(x, bd)


_seg.defvjp(lambda x, bd: (_seg_sum(x, bd), bd), lambda bd, ct: (_seg_sum(ct, bd), jnp.zeros_like(bd)))


def _colsum(x):
    return jnp.sum(x, axis=0, keepdims=True)


def _prescan_math(r, k, xw, xa, xg, k_k, w0f, w0b, a0f, a0b, kaf, kab, wupf, wupb, aupf, aupb, gup, bd):
    kkr = k * k_k
    norm = jnp.sqrt(_seg(kkr * kkr, bd))
    kk = kkr / jnp.maximum(norm, NORM_EPS)
    th = jnp.tanh(xw)

    def direction(w0, wup, a0, aup, ka):
        logit = w0 + jnp.dot(th, wup, preferred_element_type=F32)
        w = jnp.exp(-LOG_DECAY_SCALE * jax.nn.sigmoid(logit))
        a = jax.nn.sigmoid(a0 + jnp.dot(xa, aup, preferred_element_type=F32))
        kd = k * (1.0 + (a - 1.0) * ka)
        return w, kd, kk * a

    wf, kdf, bf = direction(w0f, wupf, a0f, aupf, kaf)
    wb, kdb, bb = direction(w0b, wupb, a0b, aupb, kab)
    g = jnp.dot(jax.nn.sigmoid(xg), gup, preferred_element_type=F32)
    return kk, r, wf, wb, bf, bb, kdf, kdb, g


def _postscan_math(y, r, v, kdf, kdb, g, gn_w, gn_b, rkf, rkb, bd):
    mean = _seg(y, bd) * (1.0 / HEAD)
    yc = y - mean
    var = _seg(yc * yc, bd) * (1.0 / HEAD)
    yg = yc * lax.rsqrt(var + GN_EPS) * gn_w + gn_b
    bonus = (_seg(r * kdf * rkf, bd) + _seg(r * kdb * rkb, bd)) * v
    return (yg + bonus) * g


def _halo_specs(width, col_blk, tb, t):
    nb = t // SUBLANES
    step = tb // SUBLANES
    main = pl.BlockSpec((tb, width), lambda i: (i, col_blk))
    prev = pl.BlockSpec((SUBLANES, width), lambda i: (jnp.maximum(i * step - 1, 0), col_blk))
    nxt = pl.BlockSpec((SUBLANES, width), lambda i: (jnp.minimum((i + 1) * step, nb - 1), col_blk))
    return [main, prev, nxt]


def _neighbours(z, prev8, next8, first, last):
    tb = z.shape[0]
    row = lax.broadcasted_iota(jnp.int32, z.shape, 0)
    prow = jnp.where(first, 0.0, prev8[SUBLANES - 1:SUBLANES, :])
    nrow = jnp.where(last, 0.0, next8[0:1, :])
    down = jnp.where(row == 0, prow, pltpu.roll(z, 1, 0))
    up = jnp.where(row == tb - 1, nrow, pltpu.roll(z, tb - 1, 0))
    return down, up


def _shift_conv_fwd(p, mu, conv_w, seq, *, name, tb=ROW_TILE):
    t = p.shape[0]
    per_seq = seq // tb

    def kern(p_ref, pp_ref, pn_ref, mu_ref, cw_ref, pss_ref, oc_ref):
        i = pl.program_id(0)
        first = (i % per_seq) == 0
        last = (i % per_seq) == per_seq - 1
        ps = p_ref[:, :D_SP]
        down, up = _neighbours(ps, pp_ref[:, :D_SP], pn_ref[:, :D_SP], first, last)
        pss_ref[...] = ps + mu_ref[...] * (0.5 * (down + up) - ps)
        gb = p_ref[:, D_SP:D_SP + D_CONV]
        u = p_ref[:, D_SP + D_CONV:D_SP + 2 * D_CONV] * p_ref[:, D_SP + 2 * D_CONV:]
        u_p = pp_ref[:, D_SP + D_CONV:D_SP + 2 * D_CONV] * pp_ref[:, D_SP + 2 * D_CONV:]
        u_n = pn_ref[:, D_SP + D_CONV:D_SP + 2 * D_CONV] * pn_ref[:, D_SP + 2 * D_CONV:]
        udown, uup = _neighbours(u, u_p, u_n, first, last)
        oc_ref[...] = gb * (cw_ref[0:1, :] * udown + cw_ref[1:2, :] * u + cw_ref[2:3, :] * uup)

    return pl.pallas_call(
        kern,
        out_shape=[jax.ShapeDtypeStruct((t, D_SP), F32), jax.ShapeDtypeStruct((t, D_CONV), F32)],
        grid=(t // tb,),
        in_specs=_halo_specs(D_INP, 0, tb, t) + [pl.BlockSpec((1, D_SP), lambda i: (0, 0)),
                                                 pl.BlockSpec((SUBLANES, D_CONV), lambda i: (0, 0))],
        out_specs=[pl.BlockSpec((tb, D_SP), lambda i: (i, 0)), pl.BlockSpec((tb, D_CONV), lambda i: (i, 0))],
        compiler_params=_params(("parallel",)), name=name)(p, p, p, mu, conv_w)


def _shift_conv_bwd(p, d_pss, d_o, mu, conv_w, seq, *, name, tb=ROW_TILE):
    t = p.shape[0]
    per_seq = seq // tb

    def kern(p_ref, pp_ref, pn_ref, d_ref, dp_ref, dn_ref, do_ref, dop_ref, don_ref, mu_ref, cw_ref,
             out_ref, dmu_ref, dcw_ref):
        i = pl.program_id(0)
        first = (i % per_seq) == 0
        last = (i % per_seq) == per_seq - 1

        @pl.when(i == 0)
        def _():
            dmu_ref[...] = jnp.zeros_like(dmu_ref)
            dcw_ref[...] = jnp.zeros_like(dcw_ref)

        mu_v = mu_ref[...]
        ps = p_ref[:, :D_SP]
        down, up = _neighbours(ps, pp_ref[:, :D_SP], pn_ref[:, :D_SP], first, last)
        d = d_ref[...]
        ddown, dup = _neighbours(d, dp_ref[...], dn_ref[...], first, last)
        out_ref[:, :D_SP] = (d - mu_v * d + 0.5 * (mu_v * ddown + mu_v * dup)).astype(out_ref.dtype)
        dmu_ref[...] += _colsum(d * (0.5 * (down + up) - ps))

        def parts(ref):
            return (ref[:, D_SP:D_SP + D_CONV], ref[:, D_SP + D_CONV:D_SP + 2 * D_CONV],
                    ref[:, D_SP + 2 * D_CONV:])

        gb, gc, hh = parts(p_ref)
        gb_p, gc_p, hh_p = parts(pp_ref)
        gb_n, gc_n, hh_n = parts(pn_ref)
        u = gc * hh
        udown, uup = _neighbours(u, gc_p * hh_p, gc_n * hh_n, first, last)
        cw0, cw1, cw2 = cw_ref[0:1, :], cw_ref[1:2, :], cw_ref[2:3, :]
        do = do_ref[...]
        duc = do * gb
        ducdown, ducup = _neighbours(duc, dop_ref[...] * gb_p, don_ref[...] * gb_n, first, last)
        du = cw0 * ducup + cw1 * duc + cw2 * ducdown
        out_ref[:, D_SP:D_SP + D_CONV] = (do * (cw0 * udown + cw1 * u + cw2 * uup)).astype(out_ref.dtype)
        out_ref[:, D_SP + D_CONV:D_SP + 2 * D_CONV] = (du * hh).astype(out_ref.dtype)
        out_ref[:, D_SP + 2 * D_CONV:] = (du * gc).astype(out_ref.dtype)
        dcw_ref[0:1, :] += _colsum(duc * udown)
        dcw_ref[1:2, :] += _colsum(duc * u)
        dcw_ref[2:3, :] += _colsum(duc * uup)

    return pl.pallas_call(
        kern,
        out_shape=[jax.ShapeDtypeStruct((t, D_INP), BF16), jax.ShapeDtypeStruct((1, D_SP), F32),
                   jax.ShapeDtypeStruct((SUBLANES, D_CONV), F32)],
        grid=(t // tb,),
        in_specs=(_halo_specs(D_INP, 0, tb, t) + _halo_specs(D_SP, 0, tb, t) + _halo_specs(D_CONV, 1, tb, t)
                  + [pl.BlockSpec((1, D_SP), lambda i: (0, 0)),
                     pl.BlockSpec((SUBLANES, D_CONV), lambda i: (0, 0))]),
        out_specs=[pl.BlockSpec((tb, D_INP), lambda i: (i, 0)), pl.BlockSpec((1, D_SP), lambda i: (0, 0)),
                   pl.BlockSpec((SUBLANES, D_CONV), lambda i: (0, 0))],
        compiler_params=_params(("arbitrary",)), name=name)(p, p, p, d_pss, d_pss, d_pss, d_o, d_o, d_o, mu, conv_w)


N_CHAIN = 16
N_GROUP = LANES // N_CHAIN
V_HI = HEAD // SUBLANES
G_KK, G_R, G_W, G_B, G_KD = 0, 1, (2, 3), (4, 5), (6, 7)


K_HI = HEAD // SUBLANES


def _tree_sum(terms):
    terms = list(terms)
    while len(terms) > 1:
        terms = [a + b for a, b in zip(terms[::2], terms[1::2])]
    return terms[0]


def _kscan_specs(nc):
    same = lambda c: c
    mirror = lambda c: nc - 1 - c

    def k_spec(fn):
        return pl.BlockSpec((SCAN_CHUNK, HEAD, LANES), lambda c: (fn(c), 0, 0))

    def v_spec(fn):
        return pl.BlockSpec((SCAN_CHUNK, SUBLANES, LANES), lambda c: (fn(c), 0, 0))

    return same, mirror, k_spec, v_spec


ST_SHAPE = (2, K_HI, V_HI, SUBLANES, LANES)


def _lane_group_index():
    lane = lax.broadcasted_iota(jnp.int32, (SUBLANES, LANES), 1)
    return lax.shift_right_logical(lane, jnp.full_like(lane, 4))


def _spread_groups(x, grp):
    rolled = [x] + [pltpu.roll(x, s * N_CHAIN, 1) for s in range(1, N_GROUP)]
    out = []
    for j in range(N_GROUP):
        t = rolled[(0 - j) % N_GROUP]
        for g in range(1, N_GROUP):
            t = jnp.where(grp == g, rolled[(g - j) % N_GROUP], t)
        out.append(t)
    return out


def _gather_groups(tiles, grp):
    total = None
    for s in range(N_GROUP):
        b = tiles[s % N_GROUP]
        for g in range(1, N_GROUP):
            b = jnp.where(grp == g, tiles[(g + s) % N_GROUP], b)
        b = pltpu.roll(b, s * N_CHAIN, 1) if s else b
        total = b if total is None else total + b
    return total


def _lane_group_sum(x):
    return _tree_sum([x] + [pltpu.roll(x, k * N_CHAIN, 1) for k in range(1, N_GROUP)])


def _key_row(x_t, grp, kh):
    r = SUBLANES * grp + kh
    return jnp.broadcast_to(x_t[r:r + 1, :], (SUBLANES, LANES))


def _acc(total, term):
    return term if total is None else total + term


SA_SHAPE = (2, V_HI, SUBLANES, LANES)


def _scan_fwd(xall, v_c, *, gather=(), name):
    steps = xall.shape[0]
    nc = steps // SCAN_CHUNK
    same, mirror, k_spec, v_spec = _kscan_specs(nc)
    last = SCAN_CHUNK - 1
    n_x = len(gather)

    def kern(*refs):
        xf_ref, xb_ref, vf_ref, vb_ref = refs[:4]
        yf_ref, yb_ref, hist_ref, fin_ref, sa_ref = refs[4 + n_x:9 + n_x]
        st_ref = refs[9 + 2 * n_x]
        c = pl.program_id(0)

        def riders():
            return _exchange_copies(refs[4:4 + n_x], refs[9 + n_x:9 + 2 * n_x], 0, *refs[10 + 2 * n_x:])

        @pl.when(c == 0)
        def _():
            st_ref[...] = jnp.zeros_like(st_ref)
            if n_x:
                for cp in riders():
                    cp.start()

        hist_ref[0] = st_ref[...]
        grp = _lane_group_index()

        def body(i, put):
            j = last - i
            for d, (x_t, v_t, y_ref, at) in enumerate(((xf_ref[i], vf_ref[i], yf_ref, i),
                                                       (xb_ref[j], vb_ref[j], yb_ref, j))):
                v_b = _spread_groups(v_t, grp)
                part = [None] * V_HI
                for kh in range(K_HI):
                    kk_r = _key_row(x_t, G_KK, kh)
                    for vh in range(V_HI):
                        part[vh] = _acc(part[vh], hist_ref[i, d, kh, vh] * kk_r)
                sa = [_lane_group_sum(p) for p in part]
                for vh in range(V_HI):
                    sa_ref[i, d, vh] = sa[vh]
                y_p = [None] * V_HI
                for kh in range(K_HI):
                    r_r, w_r = _key_row(x_t, G_R, kh), _key_row(x_t, G_W[d], kh)
                    b_r, kd_r = _key_row(x_t, G_B[d], kh), _key_row(x_t, G_KD[d], kh)
                    for vh in range(V_HI):
                        new = hist_ref[i, d, kh, vh] * w_r - sa[vh] * b_r + v_b[vh] * kd_r
                        put(d, kh, vh, new)
                        y_p[vh] = _acc(y_p[vh], new * r_r)
                y_ref[at] = _gather_groups(y_p, grp)

        def step(i, carry):
            def put(d, kh, vh, val):
                hist_ref[i + 1, d, kh, vh] = val
            body(i, put)
            return carry

        lax.fori_loop(0, last, step, 0, unroll=SCAN_UNROLL)

        def put_carry(d, kh, vh, val):
            st_ref[d, kh, vh] = val

        body(last, put_carry)

        @pl.when(c == nc - 1)
        def _():
            fin_ref[...] = st_ref[...]
            if n_x:
                for cp in riders():
                    cp.wait()

    return pl.pallas_call(
        kern,
        out_shape=[jax.ShapeDtypeStruct((steps, SUBLANES, LANES), F32)] * 2
        + [jax.ShapeDtypeStruct((steps,) + ST_SHAPE, F32), jax.ShapeDtypeStruct(ST_SHAPE, F32),
           jax.ShapeDtypeStruct((steps,) + SA_SHAPE, F32)]
        + _exchange_out_shapes(gather, 0),
        grid=(nc,), in_specs=[k_spec(same), k_spec(mirror), v_spec(same), v_spec(mirror)] + _hbm_specs(n_x),
        out_specs=[v_spec(same), v_spec(mirror),
                   pl.BlockSpec((SCAN_CHUNK,) + ST_SHAPE, lambda c: (c, 0, 0, 0, 0, 0)),
                   pl.BlockSpec(ST_SHAPE, lambda c: (0, 0, 0, 0, 0)),
                   pl.BlockSpec((SCAN_CHUNK,) + SA_SHAPE, lambda c: (c, 0, 0, 0, 0))] + _hbm_specs(n_x),
        scratch_shapes=[pltpu.VMEM(ST_SHAPE, F32)] + (_exchange_sems(n_x) if n_x else []),
        compiler_params=_params(("arbitrary",)), name=name)(xall, xall, v_c, v_c, *gather)


def _scan_bwd(xall, v_c, dy_c, hist, fin, sa, *, exchange=(), name):
    steps = xall.shape[0]
    nc = steps // SCAN_CHUNK
    same, back, k_spec, v_spec = _kscan_specs(nc)
    last = SCAN_CHUNK - 1
    n_x = len(exchange)

    def kern(*refs):
        xf_ref, xb_ref, vf_ref, vb_ref, dyf_ref, dyb_ref, hist_ref, fin_ref, sa_ref = refs[:9]
        gf_ref, gb_ref, dvf_ref, dvb_ref = refs[9 + n_x:13 + n_x]
        ds_ref, after_ref = refs[13 + 2 * n_x:15 + 2 * n_x]
        c = pl.program_id(0)

        def riders():
            return _exchange_copies(refs[9:9 + n_x], refs[13 + n_x:13 + 2 * n_x], n_x, *refs[15 + 2 * n_x:])

        @pl.when(c == 0)
        def _():
            ds_ref[...] = jnp.zeros_like(ds_ref)
            after_ref[...] = fin_ref[...]
            if n_x:
                for cp in riders():
                    cp.start()

        grp = _lane_group_index()
        row = lax.broadcasted_iota(jnp.int32, (SUBLANES, LANES), 0)
        zero = jnp.zeros((SUBLANES, LANES), F32)

        def body(i, after):
            j = last - i
            for d, (x_t, v_t, dy_t, g_ref, dv_ref, at) in enumerate((
                    (xf_ref[i], vf_ref[i], dyf_ref[i], gf_ref, dvf_ref, i),
                    (xb_ref[j], vb_ref[j], dyb_ref[j], gb_ref, dvb_ref, j))):
                v_s, dy_s = _spread_groups(v_t, grp), _spread_groups(dy_t, grp)
                dsa_p, dv_p = [None] * V_HI, [None] * V_HI
                for kh in range(K_HI):
                    r_r = _key_row(x_t, G_R, kh)
                    b_r, kd_r = _key_row(x_t, G_B[d], kh), _key_row(x_t, G_KD[d], kh)
                    for vh in range(V_HI):
                        g = ds_ref[d, kh, vh] + dy_s[vh] * r_r
                        ds_ref[d, kh, vh] = g
                        dsa_p[vh] = _acc(dsa_p[vh], g * b_r)
                        dv_p[vh] = _acc(dv_p[vh], g * kd_r)
                dsa = [-_lane_group_sum(p) for p in dsa_p]
                sa = [sa_ref[i, d, vh] for vh in range(V_HI)]
                dv_ref[at] = _gather_groups(dv_p, grp)
                blocks = {G_KK: zero, G_R: zero, G_W[d]: zero, G_B[d]: zero, G_KD[d]: zero}
                for kh in range(K_HI):
                    w_r, kk_r = _key_row(x_t, G_W[d], kh), _key_row(x_t, G_KK, kh)
                    dkk = dr = dw = db = dkd = None
                    for vh in range(V_HI):
                        g, before = ds_ref[d, kh, vh], hist_ref[i, d, kh, vh]
                        dr = _acc(dr, after(d, kh, vh) * dy_s[vh])
                        dw = _acc(dw, g * before)
                        dkd = _acc(dkd, g * v_s[vh])
                        db = _acc(db, g * sa[vh])
                        dkk = _acc(dkk, before * dsa[vh])
                        ds_ref[d, kh, vh] = g * w_r + dsa[vh] * kk_r
                    for gi, a in ((G_KK, dkk), (G_R, dr), (G_W[d], dw), (G_B[d], -db), (G_KD[d], dkd)):
                        blocks[gi] = jnp.where(row == kh, _colsum(a), blocks[gi])
                for gi in range(N_GROUP):
                    g_ref[at, SUBLANES * gi:SUBLANES * (gi + 1), :] = blocks.get(gi, zero)

        body(last, lambda d, kh, vh: after_ref[d, kh, vh])

        def step(ii, carry):
            i = last - ii
            body(i, lambda d, kh, vh: hist_ref[i + 1, d, kh, vh])
            return carry

        lax.fori_loop(1, SCAN_CHUNK, step, 0, unroll=SCAN_UNROLL)
        after_ref[...] = hist_ref[0]

        if n_x:
            @pl.when(c == nc - 1)
            def _():
                for cp in riders():
                    cp.wait()

    return pl.pallas_call(
        kern,
        out_shape=[jax.ShapeDtypeStruct((steps, HEAD, LANES), F32)] * 2
        + [jax.ShapeDtypeStruct((steps, SUBLANES, LANES), F32)] * 2 + _exchange_out_shapes(exchange, n_x),
        grid=(nc,),
        in_specs=[k_spec(back), k_spec(same), v_spec(back), v_spec(same), v_spec(back), v_spec(same),
                  pl.BlockSpec((SCAN_CHUNK,) + ST_SHAPE, lambda c: (back(c), 0, 0, 0, 0, 0)),
                  pl.BlockSpec(ST_SHAPE, lambda c: (0, 0, 0, 0, 0)),
                  pl.BlockSpec((SCAN_CHUNK,) + SA_SHAPE, lambda c: (back(c), 0, 0, 0, 0))] + _hbm_specs(n_x),
        out_specs=[k_spec(back), k_spec(same), v_spec(back), v_spec(same)] + _hbm_specs(n_x),
        scratch_shapes=[pltpu.VMEM(ST_SHAPE, F32), pltpu.VMEM(ST_SHAPE, F32)]
        + (_exchange_sems(n_x) if n_x else []),
        compiler_params=_params(("arbitrary",)), name=name)(xall, xall, v_c, v_c, dy_c, dy_c, hist, fin, sa,
                                                            *exchange)


def _to_key_rows(wide, bsz, seq):
    z = wide.reshape(bsz, seq, N_GROUP, N_HEAD, K_HI, SUBLANES).transpose(1, 2, 4, 5, 0, 3)
    return z.reshape(seq, HEAD, LANES)


def _from_key_rows(g, bsz, seq):
    z = g.reshape(seq, N_GROUP, K_HI, SUBLANES, bsz, N_HEAD).transpose(4, 0, 1, 5, 2, 3)
    return z.reshape(bsz * seq, N_GROUP * D_RWKV)


def _to_value_rows(a, bsz, seq):
    z = a.reshape(bsz, seq, N_HEAD, V_HI, SUBLANES).transpose(1, 4, 3, 0, 2)
    return z.reshape(seq, SUBLANES, LANES)


def _from_value_rows(y, bsz, seq):
    z = y.reshape(seq, SUBLANES, V_HI, bsz, N_HEAD).transpose(3, 0, 4, 2, 1)
    return z.reshape(bsz * seq, D_RWKV)


def _pad_cols(a, segs, axis=-1):
    out, off = [], 0
    for w, wp in segs:
        out.append(lax.slice_in_dim(a, off, off + w, axis=axis))
        if wp > w:
            pad = list(a.shape)
            pad[axis] = wp - w
            out.append(jnp.zeros(pad, a.dtype))
        off += w
    return jnp.concatenate(out, axis=axis)


def _unpad_cols(a, segs, axis=-1):
    out, off = [], 0
    for w, wp in segs:
        out.append(lax.slice_in_dim(a, off, off + w, axis=axis))
        off += wp
    return jnp.concatenate(out, axis=axis)


P_SEGS = ((3 * D_RWKV, 3 * D_RWKV), (D_LORA, 128), (D_LORA, 128), (D_GATE, 256), (3 * D_CONV, 3 * D_CONV))
S_SEGS = P_SEGS[:4]


def _pad_rows(a, rows):
    return jnp.concatenate([a, jnp.zeros((rows - a.shape[0], a.shape[1]), a.dtype)], axis=0)


LATE = ("w_out", "w_gate", "w_up", "w_down")


def _local_step(x, target, w, late=None):
    bsz, seq, _ = x.shape
    t = bsz * seq
    x2d = x.reshape(t, D_MODEL)
    tg2d = target.reshape(t, D_MODEL)
    row = lambda a: a.reshape(1, -1).astype(F32)

    w_in_t = _pad_cols(w["w_in"], P_SEGS, axis=0)
    mu = _pad_cols(row(w["mu_shift"]), S_SEGS)
    wupf, wupb, aupf, aupb = (_pad_rows(w[n][0].astype(F32), 128) for n in ("w_up_f", "w_up_b", "a_up_f", "a_up_b"))
    gup = _pad_rows(w["g_up"][0].astype(F32), 256)
    conv_w = _pad_rows(w["conv_w"][0].astype(F32), SUBLANES)
    norm1, norm2, normf = row(w["norm1_w"]), row(w["norm2_w"]), row(w["norm_f_w"])
    vec = {n: row(w[n]) for n in VEC}
    head_of = jnp.arange(LANES) // HEAD
    bd = (head_of[:, None] == head_of[None, :]).astype(F32)
    pre_consts = [vec["k_k"], vec["w0_f"], vec["w0_b"], vec["a0_f"], vec["a0_b"], vec["k_a_f"], vec["k_a_b"],
                  wupf, wupb, aupf, aupb, gup, bd]
    post_consts = [vec["gn_w"], vec["gn_b"], vec["r_k_f"], vec["r_k_b"], bd]

    h1, = _rowwise(_rms, [x2d], [norm1], [D_MODEL], [], name="rms1_fwd", out_dtype=BF16, tb=WIDE_TILE)
    p = _mm(h1, w_in_t, tb=True, name="mm_in")
    pss, oconv = _shift_conv_fwd(p, mu, conv_w, seq, name="shift_conv_fwd")
    pre_rows = [(pss, 0, 512), (pss, 1, 512), (pss, XW0 // 128, 128), (pss, XA0 // 128, 128), (pss, XG0 // 256, 256)]
    sc, g = _rowwise(_prescan_math, pre_rows, pre_consts, [[D_RWKV] * N_GROUP, D_RWKV], [], name="prescan_fwd")
    xall = _to_key_rows(sc, bsz, seq)
    v_l = _to_value_rows(pss[:, 2 * D_RWKV:3 * D_RWKV], bsz, seq)
    y_f, y_b, hist, fin, sa, *gathered = _scan_fwd(xall, v_l, gather=[late[n] for n in LATE] if late else (),
                                                   name="scan_fwd")
    w_out, w_gate_t, w_up_t, w_down = (_from_slots(a, n) if late else w[n] for n, a in zip(LATE, gathered or LATE))
    w_out, w_down = w_out[0], w_down[0]
    y = _from_value_rows(y_f + y_b, bsz, seq)
    post_rows = [y, (pss, 0, 512), (pss, 2, 512), (sc, G_KD[0], 512), (sc, G_KD[1], 512), g]

    def post_fwd(y_, r_, v_, kdf_, kdb_, g_, oc_, *consts):
        return _postscan_math(y_, r_, v_, kdf_, kdb_, g_, *consts), oc_

    o, = _rowwise(post_fwd, post_rows + [oconv], post_consts, [[D_RWKV, D_CONV]], [], name="postscan_fwd",
                  out_dtype=BF16)
    x1 = _mm(o, w_out, add=x2d, name="mm_out")
    h2, = _rowwise(_rms, [x1], [norm2], [D_MODEL], [], name="rms2_fwd", out_dtype=BF16, tb=WIDE_TILE)
    gg, uu, ff = _mm_swiglu(h2, w_gate_t, w_up_t, name="mm_gate_up")
    x2 = _mm(ff, w_down, add=x1, name="mm_down")

    def final(x_, tg_, wn_):
        yo, vjp = jax.vjp(_rms, x_, wn_)
        err = yo - tg_
        dx_, dwn_ = vjp(err * (1.0 / D_MODEL))
        part = jnp.sum(jnp.sum(err * err, axis=1, keepdims=True), axis=0, keepdims=True) * (0.5 / D_MODEL)
        return dx_, part + jnp.zeros((1, LANES), F32), dwn_

    dx2, loss_acc, d_normf = _rowwise(final, [x2, tg2d], [normf], [D_MODEL], [(1, LANES), (1, D_MODEL)],
                                      name="loss_head", tb=WIDE_TILE)
    dgg, duu = _mm_swiglu_bwd(dx2, w_down, gg, uu, name="mm_down_dx")
    g_w_down = _mm(ff, dx2, ta=True, name="mm_down_dw")
    dh2 = _mm(dgg, w_gate_t, name="mm_gate_dx")
    dh2 = _mm(duu, w_up_t, add=dh2, name="mm_up_dx")
    g_w_gate_t = _mm(dgg, h2, ta=True, name="mm_gate_dw")
    g_w_up_t = _mm(duu, h2, ta=True, name="mm_up_dw")

    def rms_bwd(x_, dh_, dres_, wn_):
        _, vjp = jax.vjp(_rms, x_, wn_)
        dx_, dwn_ = vjp(dh_)
        return dx_ + dres_, dwn_

    dx1, d_norm2 = _rowwise(rms_bwd, [x1, dh2, dx2], [norm2], [D_MODEL], [(1, D_MODEL)], name="rms2_bwd", tb=WIDE_TILE)
    do = _mm(dx1, w_out, tb=True, name="mm_out_dx")
    g_w_out = _mm(o, dx1, ta=True, name="mm_out_dw")

    def post_bwd(y_, r_, v_, kdf_, kdb_, g_, do_, *consts):
        _, vjp = jax.vjp(lambda *a: _postscan_math(*a, consts[4]), y_, r_, v_, kdf_, kdb_, g_, *consts[:4])
        return vjp(do_)

    (dy, dr_c, dv_c, dkdf_c, dkdb_c, dg, d_gn_w, d_gn_b, d_rkf, d_rkb) = _rowwise(
        post_bwd, post_rows + [(do, 0, 512)], post_consts, [D_RWKV] * 6, [(1, D_RWKV)] * 4, name="postscan_bwd")
    dy_l = _to_value_rows(dy, bsz, seq)
    late_grads = {"w_out": g_w_out[None], "w_gate": g_w_gate_t, "w_up": g_w_up_t, "w_down": g_w_down[None]}
    g_f, g_b, dv_f, dv_b, *late_parts = _scan_bwd(
        xall, v_l, dy_l, hist, fin, sa, name="scan_bwd",
        exchange=[_to_slots(late_grads[n], n).astype(BF16) for n in LATE] if late else ())
    dsc = _from_key_rows(g_f + g_b, bsz, seq)
    dv_s = _from_value_rows(dv_f + dv_b, bsz, seq)

    def pre_bwd(r_, k_, xw_, xa_, xg_, dkk_, dr_s, dwf_, dwb_, dbf_, dbb_, dkdf_s, dkdb_s,
                dr_c_, dv_c_, dv_s_, dkdf_c_, dkdb_c_, dg_, *consts):
        _, vjp = jax.vjp(lambda *a: _prescan_math(*a, consts[-1]), r_, k_, xw_, xa_, xg_, *consts[:-1])
        grads = vjp((dkk_, dr_s + dr_c_, dwf_, dwb_, dbf_, dbb_, dkdf_s + dkdf_c_, dkdb_s + dkdb_c_, dg_))
        dr_, dk_, dxw_, dxa_, dxg_ = grads[:5]
        return (dr_, dk_, dv_c_ + dv_s_, dxw_, dxa_, dxg_) + tuple(grads[5:])

    pre_b_rows = (pre_rows + [(dsc, j, 512) for j in range(N_GROUP)]
                  + [dr_c, dv_c, dv_s, dkdf_c, dkdb_c, dg])
    pre_b = _rowwise(pre_bwd, pre_b_rows, pre_consts, [[512, 512, 512, 128, 128, 256]],
                     [(1, D_RWKV)] * 7 + [(128, D_RWKV)] * 4 + [(256, D_RWKV)], name="prescan_bwd")
    d_pss = pre_b[0]
    d_kk_, d_w0f, d_w0b, d_a0f, d_a0b, d_kaf, d_kab, d_wupf, d_wupb, d_aupf, d_aupb, d_gup = pre_b[1:]
    dp, d_mu, d_conv = _shift_conv_bwd(p, d_pss, do, mu, conv_w, seq, name="shift_conv_bwd")
    g_w_in_t = _mm(dp, h1, ta=True, name="mm_in_dw")
    grads = {
        "w_in": _unpad_cols(g_w_in_t, P_SEGS, axis=0), "mu_shift": _unpad_cols(d_mu, S_SEGS),
        "w_up_f": d_wupf[None, :D_LORA], "w0_f": d_w0f, "w_up_b": d_wupb[None, :D_LORA], "w0_b": d_w0b,
        "a_up_f": d_aupf[None, :D_LORA], "a0_f": d_a0f, "a_up_b": d_aupb[None, :D_LORA], "a0_b": d_a0b,
        "g_up": d_gup[None, :D_GATE], "k_k": d_kk_, "k_a_f": d_kaf, "k_a_b": d_kab,
        "r_k_f": d_rkf, "r_k_b": d_rkb, "gn_w": d_gn_w, "gn_b": d_gn_b, "conv_w": d_conv[None, :3],
        "w_out": g_w_out[None], "norm2_w": d_norm2, "w_gate": g_w_gate_t, "w_up": g_w_up_t,
        "w_down": g_w_down[None], "norm_f_w": d_normf,
    }
    early = ("w_in",) + LORA
    parts = dict(zip(LATE, late_parts))
    if late:
        vec_rows = jnp.concatenate([grads[n] for n in VEC] + [jnp.zeros((16 - len(VEC), D_RWKV), F32)], axis=0)
        slots = [_to_slots(grads[n], n).astype(BF16 if n in BIG else F32) for n in early]
        dh1, *recv = _mm(dp, w_in_t, exchange=(slots, [vec_rows]), name="mm_in_dx")
        parts.update(zip(early + ("vec",), recv))
    else:
        dh1 = _mm(dp, w_in_t, name="mm_in_dx")
    dx, grads["norm1_w"] = _rowwise(rms_bwd, [x2d, dh1, dx1], [norm1], [D_MODEL], [(1, D_MODEL)], name="rms1_bwd",
                                    tb=WIDE_TILE)
    return loss_acc, dx.reshape(bsz, seq, D_MODEL), grads, parts


def _hbm_specs(n):
    return [pl.BlockSpec(memory_space=pl.ANY)] * n


def _all_gather(arrs, *, name):
    n = len(arrs)

    def body(*refs):
        x_refs, out_refs = refs[:n], refs[n:2 * n]
        send_sems, recv_sems, local_sems = refs[2 * n:]
        x, y, c = lax.axis_index("x"), lax.axis_index("y"), lax.axis_index("c")
        me, sibling = (x, y, c), (x, y, 1 - c)
        chips = [(1 - x, y), (x, 1 - y), (1 - x, 1 - y)]

        def slot(a, px, py, pc):
            return out_refs[a].at[4 * px + 2 * py + pc]

        def copy(a, k, block, to, src=None):
            return pltpu.make_async_remote_copy(
                src_ref=slot(a, *block) if src is None else src, dst_ref=slot(a, *block),
                send_sem=send_sems.at[k, a], recv_sem=recv_sems.at[k, a],
                device_id=to, device_id_type=pl.DeviceIdType.MESH)

        mine = [pltpu.make_async_copy(x_refs[a], slot(a, *me), local_sems.at[a]) for a in range(n)]
        for cp in mine:
            cp.start()
        first = []
        for a in range(n):
            first.append(copy(a, 0, me, sibling, src=x_refs[a]))
            first += [copy(a, 1 + j, me, (*chip, c), src=x_refs[a]) for j, chip in enumerate(chips)]
        for cp in first:
            cp.start()
        passed = []
        for j, chip in enumerate(chips):
            for a in range(n):
                copy(a, 1 + j, (*chip, c), me).wait_recv()
                cp = copy(a, 4 + j, (*chip, c), sibling)
                cp.start()
                passed.append(cp)
        for a in range(n):
            copy(a, 0, sibling, me).wait_recv()
            for j, chip in enumerate(chips):
                copy(a, 4 + j, (*chip, 1 - c), me).wait_recv()
        for cp in first + passed:
            cp.wait_send()
        for cp in mine:
            cp.wait()

    return pl.pallas_call(
        body, out_shape=[jax.ShapeDtypeStruct((N_DEV,) + a.shape, a.dtype) for a in arrs],
        in_specs=_hbm_specs(n), out_specs=_hbm_specs(n),
        scratch_shapes=[pltpu.SemaphoreType.DMA((7, n)), pltpu.SemaphoreType.DMA((7, n)),
                        pltpu.SemaphoreType.DMA((n,))],
        name=name)(*arrs)


def _exchange(sliced, whole, *, name):
    arrs = list(sliced) + list(whole)
    n, n_sliced = len(arrs), len(sliced)

    def body(*refs):
        copies = _exchange_copies(refs[:n], refs[n:2 * n], n_sliced, *refs[2 * n:])
        for cp in copies:
            cp.start()
        for cp in copies:
            cp.wait()

    return pl.pallas_call(
        body, out_shape=_exchange_out_shapes(arrs, n_sliced), in_specs=_hbm_specs(n), out_specs=_hbm_specs(n),
        scratch_shapes=_exchange_sems(n), name=name)(*arrs)


def _exchange_out_shapes(arrs, n_sliced):
    return [jax.ShapeDtypeStruct(a.shape if i < n_sliced else (N_DEV,) + a.shape, a.dtype)
            for i, a in enumerate(arrs)]


def _exchange_sems(n):
    return [pltpu.SemaphoreType.DMA((7, n)), pltpu.SemaphoreType.DMA((7, n)), pltpu.SemaphoreType.DMA((n,))]


def _exchange_copies(in_refs, out_refs, n_sliced, send_sems, recv_sems, local_sems):
    n = len(in_refs)
    x, y, c = lax.axis_index("x"), lax.axis_index("y"), lax.axis_index("c")
    me = 4 * x + 2 * y + c

    def src(a, dev):
        return in_refs[a].at[dev] if a < n_sliced else in_refs[a]

    copies = [pltpu.make_async_copy(src(a, me), out_refs[a].at[me], local_sems.at[a]) for a in range(n)]
    for k in range(1, N_DEV):
        px = 1 - x if k & 4 else x
        py = 1 - y if k & 2 else y
        pc = 1 - c if k & 1 else c
        for a in range(n):
            copies.append(pltpu.make_async_remote_copy(
                src_ref=src(a, 4 * px + 2 * py + pc), dst_ref=out_refs[a].at[me],
                send_sem=send_sems.at[k - 1, a], recv_sem=recv_sems.at[k - 1, a],
                device_id=(px, py, pc), device_id_type=pl.DeviceIdType.MESH))
    return copies


def _adam_math(g, w, m, v):
    nm = ADAM_B1 * m + (1.0 - ADAM_B1) * g
    nv = ADAM_B2 * v + (1.0 - ADAM_B2) * (g * g)
    m_hat = nm / (1.0 - ADAM_B1 ** ADAM_STEP)
    v_hat = nv / (1.0 - ADAM_B2 ** ADAM_STEP)
    return -ADAM_LR * (m_hat / (jnp.sqrt(v_hat) + ADAM_EPS) + ADAM_WD * w), nm, nv


def _slot_sum(ref):
    g = ref[0].astype(F32)
    for s in range(1, N_DEV):
        g = g + ref[s].astype(F32)
    return g


def _adamw_big(parts, w, m, v, *, name):
    if w.shape[0] == 1:
        steps = w.shape[1] // _tile(w.shape[1], (256, 176, 128))
        block = (1, w.shape[1] // steps, w.shape[2])
        at = lambda i: (0, i, 0)
    else:
        slabs = max(d for d in range(1, 25) if w.shape[0] % d == 0)
        steps = w.shape[0] // slabs
        block = (slabs, 1, w.shape[2])
        at = lambda i: (i, 0, 0)

    def kern(p_ref, w_ref, m_ref, v_ref, g_ref, d_ref, nm_ref, nv_ref):
        g = _slot_sum(p_ref)
        g_ref[...] = g
        d_ref[...], nm_ref[...], nv_ref[...] = _adam_math(g, w_ref[...], m_ref[...], v_ref[...])

    spec = pl.BlockSpec(block, at)
    return pl.pallas_call(
        kern, out_shape=[jax.ShapeDtypeStruct(w.shape, F32)] * 4, grid=(steps,),
        in_specs=[pl.BlockSpec((N_DEV,) + block, lambda i: (0,) + at(i)), spec, spec, spec],
        out_specs=[spec] * 4, compiler_params=_params(("parallel",)), name=name)(parts, w, m, v)


def _adamw_small(lora_parts, vec_parts, wide_parts, wmv, *, name):
    names = LORA + VEC + WIDE
    n_l, n = len(LORA), len(names)
    flat = [a for trip in wmv for a in trip]

    def kern(*refs):
        l_refs, vec_ref, wide_ref = refs[:n_l], refs[n_l], refs[n_l + 1]
        in_refs = refs[n_l + 2:n_l + 2 + 3 * n]
        out_refs = refs[n_l + 2 + 3 * n:]
        vec_sum, wide_sum = _slot_sum(vec_ref), _slot_sum(wide_ref)
        for i, nm in enumerate(names):
            w_ref, m_ref, v_ref = in_refs[3 * i:3 * i + 3]
            if i < n_l:
                g = _slot_sum(l_refs[i])
            elif nm in VEC:
                g = vec_sum[i - n_l:i - n_l + 1, :]
            else:
                g = wide_sum[WIDE.index(nm):WIDE.index(nm) + 1, :w_ref.shape[-1]]
            o = out_refs[4 * i:4 * i + 4]
            o[0][...] = g
            o[1][...], o[2][...], o[3][...] = _adam_math(g, w_ref[...], m_ref[...], v_ref[...])

    out_shape = [jax.ShapeDtypeStruct(trip[0].shape, F32) for trip in wmv for _ in range(4)]
    outs = pl.pallas_call(kern, out_shape=out_shape, name=name,
                          compiler_params=pltpu.CompilerParams(vmem_limit_bytes=VMEM_LIMIT))(
        *lora_parts, vec_parts, wide_parts, *flat)
    return [tuple(outs[4 * i:4 * i + 4]) for i in range(n)]


def _to_slots(g, name):
    if name in TRANSPOSED:
        cols, rws = g.shape
        return g.reshape(N_DEV, cols // N_DEV, 1, rws)
    _, rws, cols = g.shape
    if SHARD_AXIS[name] == 1:
        return g.reshape(N_DEV, 1, rws // N_DEV, cols)
    return g.reshape(1, rws, N_DEV, cols // N_DEV).transpose(2, 0, 1, 3)


def _from_slots(got, name):
    if name in TRANSPOSED:
        return got.reshape(N_DEV * got.shape[1], got.shape[3])
    _, _, rws, cols = got.shape
    if SHARD_AXIS[name] == 1:
        return got.reshape(1, N_DEV * rws, cols)
    return got.transpose(1, 2, 0, 3).reshape(1, rws, N_DEV * cols)


def _shard_form(name, a):
    return a.transpose(2, 0, 1) if name in TRANSPOSED else a


def _pad_lanes(a, width):
    return jnp.concatenate([a, jnp.zeros(a.shape[:-1] + (width - a.shape[-1],), a.dtype)], axis=-1)


def kernel(x, norm1_w, w_in, mu_shift, w_up_f, w0_f, w_up_b, w0_b, a_up_f, a0_f, a_up_b, a0_b, g_up, k_k, k_a_f, k_a_b, r_k_f, r_k_b, gn_w, gn_b, conv_w, w_out, norm2_w, w_gate, w_up, w_down, norm_f_w, loss_target, m_norm1_w, m_w_in, m_mu_shift, m_w_up_f, m_w0_f, m_w_up_b, m_w0_b, m_a_up_f, m_a0_f, m_a_up_b, m_a0_b, m_g_up, m_k_k, m_k_a_f, m_k_a_b, m_r_k_f, m_r_k_b, m_gn_w, m_gn_b, m_conv_w, m_w_out, m_norm2_w, m_w_gate, m_w_up, m_w_down, m_norm_f_w, v_norm1_w, v_w_in, v_mu_shift, v_w_up_f, v_w0_f, v_w_up_b, v_w0_b, v_a_up_f, v_a0_f, v_a_up_b, v_a0_b, v_g_up, v_k_k, v_k_a_f, v_k_a_b, v_r_k_f, v_r_k_b, v_gn_w, v_gn_b, v_conv_w, v_w_out, v_norm2_w, v_w_gate, v_w_up, v_w_down, v_norm_f_w):
    local = dict(norm1_w=norm1_w, w_in=w_in, mu_shift=mu_shift, w_up_f=w_up_f, w0_f=w0_f, w_up_b=w_up_b,
                 w0_b=w0_b, a_up_f=a_up_f, a0_f=a0_f, a_up_b=a_up_b, a0_b=a0_b, g_up=g_up, k_k=k_k, k_a_f=k_a_f,
                 k_a_b=k_a_b, r_k_f=r_k_f, r_k_b=r_k_b, gn_w=gn_w, gn_b=gn_b, conv_w=conv_w, w_out=w_out,
                 norm2_w=norm2_w, w_gate=w_gate, w_up=w_up, w_down=w_down, norm_f_w=norm_f_w)
    mom_m = dict(norm1_w=m_norm1_w, w_in=m_w_in, mu_shift=m_mu_shift, w_up_f=m_w_up_f, w0_f=m_w0_f,
                 w_up_b=m_w_up_b, w0_b=m_w0_b, a_up_f=m_a_up_f, a0_f=m_a0_f, a_up_b=m_a_up_b, a0_b=m_a0_b,
                 g_up=m_g_up, k_k=m_k_k, k_a_f=m_k_a_f, k_a_b=m_k_a_b, r_k_f=m_r_k_f, r_k_b=m_r_k_b,
                 gn_w=m_gn_w, gn_b=m_gn_b, conv_w=m_conv_w, w_out=m_w_out, norm2_w=m_norm2_w, w_gate=m_w_gate,
                 w_up=m_w_up, w_down=m_w_down, norm_f_w=m_norm_f_w)
    mom_v = dict(norm1_w=v_norm1_w, w_in=v_w_in, mu_shift=v_mu_shift, w_up_f=v_w_up_f, w0_f=v_w0_f,
                 w_up_b=v_w_up_b, w0_b=v_w0_b, a_up_f=v_a_up_f, a0_f=v_a0_f, a_up_b=v_a_up_b, a0_b=v_a0_b,
                 g_up=v_g_up, k_k=v_k_k, k_a_f=v_k_a_f, k_a_b=v_k_a_b, r_k_f=v_r_k_f, r_k_b=v_r_k_b,
                 gn_w=v_gn_w, gn_b=v_gn_b, conv_w=v_conv_w, w_out=v_w_out, norm2_w=v_norm2_w, w_gate=v_w_gate,
                 w_up=v_w_up, w_down=v_w_down, norm_f_w=v_norm_f_w)

    early = ("w_in",) + LORA
    got = _all_gather([_shard_form("w_in", local["w_in"]).astype(BF16)] + [local[n] for n in LORA], name="gather")
    full = dict(local)
    full.update({n: _from_slots(a, n) for n, a in zip(early, got)})

    loss_part, grad_x, grads, parts = _local_step(
        x, loss_target, full, late={n: _shard_form(n, local[n]).astype(BF16) for n in LATE})

    wide_rows = jnp.concatenate([_pad_lanes(a, WIDE_ROW) for a in [grads[n] for n in WIDE] + [loss_part]]
                                + [jnp.zeros((SUBLANES - len(WIDE) - 1, WIDE_ROW), F32)], axis=0)
    wide_parts, = _exchange([], [wide_rows], name="grad_exchange")
    loss = jnp.sum(wide_parts[:, len(WIDE), 0])
    out = {}
    for n in BIG:
        quad = _adamw_big(parts[n], *(_shard_form(n, d[n]) for d in (local, mom_m, mom_v)), name="adamw_" + n)
        out[n] = tuple(a.transpose(1, 2, 0) for a in quad) if n in TRANSPOSED else quad

    def small_form(n, a):
        if n in LORA:
            return a
        a = a.reshape(1, -1)
        return _pad_lanes(a, WIDE_ROW) if n == "mu_shift" else a

    small = LORA + VEC + WIDE
    res = _adamw_small([parts[n] for n in LORA], parts["vec"], wide_parts,
                       [tuple(small_form(n, d[n]) for d in (local, mom_m, mom_v)) for n in small],
                       name="adamw_small")
    for n, quad in zip(small, res):
        out[n] = tuple(a[..., :local[n].size].reshape(local[n].shape) if n not in LORA else a for a in quad)
    return (loss, grad_x, *[out[n][i] for i in range(4) for n in WEIGHTS])
```

```python
import functools

import jax
import jax.numpy as jnp
from jax import lax
from jax.experimental import pallas as pl
from jax.experimental.pallas import tpu as pltpu

F32 = jnp.float32
BF16 = jnp.bfloat16
HIGHEST = lax.Precision.HIGHEST

N_DEV = 8
D_MODEL = 1024
D_RWKV = 512
D_CONV = 512
HEAD = 64
N_HEAD = D_RWKV // HEAD
D_LORA = 64
D_GATE = 160
D_SHIFTED = 3 * D_RWKV + 2 * D_LORA + D_GATE
XW0, XA0, XG0 = 1536, 1664, 1792
D_SP = 2048
D_INP = D_SP + 3 * D_CONV
LOG_DECAY_SCALE = 0.606531
RMS_EPS = 1e-6
GN_EPS = 64e-5
NORM_EPS = 1e-12
ADAM_LR, ADAM_B1, ADAM_B2, ADAM_EPS, ADAM_WD, ADAM_STEP = 0.001, 0.9, 0.999, 1e-08, 0.01, 10

LANES = 128
SUBLANES = 8
VMEM_LIMIT = 48 * 1024 * 1024
SCAN_CHUNK = 16
SCAN_UNROLL = 3
ROW_TILE = 128
WIDE_TILE = 256

BIG = ("w_in", "w_out", "w_gate", "w_up", "w_down")
LORA = ("w_up_f", "w_up_b", "a_up_f", "a_up_b", "g_up", "conv_w")
SHARD_AXIS = {"w_in": 2, "w_out": 1, "w_gate": 2, "w_up": 2, "w_down": 1, "w_up_f": 2, "w_up_b": 2,
              "a_up_f": 2, "a_up_b": 2, "g_up": 2, "conv_w": 2}
VEC = ("w0_f", "w0_b", "a0_f", "a0_b", "k_k", "k_a_f", "k_a_b", "r_k_f", "r_k_b", "gn_w", "gn_b")
WIDE = ("mu_shift", "norm1_w", "norm2_w", "norm_f_w")
WIDE_ROW = 2048
WEIGHTS = ("norm1_w", "w_in", "mu_shift", "w_up_f", "w0_f", "w_up_b", "w0_b", "a_up_f", "a0_f", "a_up_b",
           "a0_b", "g_up", "k_k", "k_a_f", "k_a_b", "r_k_f", "r_k_b", "gn_w", "gn_b", "conv_w", "w_out",
           "norm2_w", "w_gate", "w_up", "w_down", "norm_f_w")


def _params(sem, limit=VMEM_LIMIT):
    return pltpu.CompilerParams(dimension_semantics=sem, vmem_limit_bytes=limit)


def _tile(n, cands):
    for c in cands:
        if n % c == 0:
            return c
    raise ValueError(f"no tile for {n}")


def _mm(a, b, *, ta=False, tb=False, add=None, exchange=None, name):
    (k_dim, m) = a.shape if ta else a.shape[::-1]
    (k2, n) = b.shape[::-1] if tb else b.shape
    assert k_dim == k2, (a.shape, b.shape, ta, tb)
    tm = _tile(m, (1408, 1024, 512, 256, 128))
    tn = _tile(n, (1408, 1024, 896, 512, 256, 128))
    tk = _tile(k_dim, (1408, 1024, 896, 512, 256, 128))
    nk = k_dim // tk
    grid = (m // tm, n // tn, nk)
    dims = (((0 if ta else 1,), (1 if tb else 0,)), ((), ()))
    sliced, whole = exchange or ((), ())
    riders = list(sliced) + list(whole)
    n_x, n_in = len(riders), 2 + (add is not None)

    def kern(*refs):
        a_ref, b_ref = refs[:2]
        add_ref = refs[2] if add is not None else None
        o_ref, acc_ref = refs[n_in + n_x], refs[n_in + 2 * n_x + 1]
        k = pl.program_id(2)
        step = (pl.program_id(0) * grid[1] + pl.program_id(1)) * nk + k

        def copies():
            return _exchange_copies(refs[n_in:n_in + n_x], refs[n_in + n_x + 1:n_in + 2 * n_x + 1], len(sliced),
                                    *refs[n_in + 2 * n_x + 2:])

        if n_x:
            @pl.when(step == 0)
            def _():
                for cp in copies():
                    cp.start()

        prod = lax.dot_general(a_ref[...].astype(BF16), b_ref[...].astype(BF16), dims, preferred_element_type=F32)

        def finish(total):
            o_ref[...] = total if add is None else total + add_ref[...]

        if nk == 1:
            finish(prod)
        else:
            @pl.when(k == 0)
            def _():
                acc_ref[...] = prod

            @pl.when((k > 0) & (k < nk - 1))
            def _():
                acc_ref[...] += prod

            @pl.when(k == nk - 1)
            def _():
                finish(acc_ref[...] + prod)

        if n_x:
            @pl.when(step == grid[0] * grid[1] * nk - 1)
            def _():
                for cp in copies():
                    cp.wait()

    a_spec = (pl.BlockSpec((tk, tm), lambda i, j, k: (k, i)) if ta
              else pl.BlockSpec((tm, tk), lambda i, j, k: (i, k)))
    b_spec = (pl.BlockSpec((tn, tk), lambda i, j, k: (j, k)) if tb
              else pl.BlockSpec((tk, tn), lambda i, j, k: (k, j)))
    o_spec = pl.BlockSpec((tm, tn), lambda i, j, k: (i, j))
    in_specs = [a_spec, b_spec] + ([o_spec] if add is not None else []) + _hbm_specs(n_x)
    args = (a, b) + ((add,) if add is not None else ()) + tuple(riders)
    out = pl.pallas_call(
        kern, out_shape=[jax.ShapeDtypeStruct((m, n), F32)] + _exchange_out_shapes(riders, len(sliced)), grid=grid,
        in_specs=in_specs, out_specs=[o_spec] + _hbm_specs(n_x),
        scratch_shapes=[pltpu.VMEM((tm, tn), F32)] + (_exchange_sems(n_x) if n_x else []),
        compiler_params=_params(("arbitrary",) * 3 if n_x else ("parallel", "parallel", "arbitrary")),
        name=name)(*args)
    return out if n_x else out[0]


def _swiglu(g, u):
    return jax.nn.silu(g) * u


FFN_TN = 256


def _mm_swiglu(h, w_gate, w_up, *, name):
    m, k_dim = h.shape
    n = w_gate.shape[1]
    tm = _tile(m, (1024, 512, 256, 128))

    def kern(h_ref, wg_ref, wu_ref, g_ref, u_ref, f_ref):
        hv = h_ref[...].astype(BF16)
        g = jnp.dot(hv, wg_ref[...].astype(BF16), preferred_element_type=F32)
        u = jnp.dot(hv, wu_ref[...].astype(BF16), preferred_element_type=F32)
        g_ref[...] = g
        u_ref[...] = u
        f_ref[...] = _swiglu(g, u).astype(f_ref.dtype)

    w_spec = pl.BlockSpec((k_dim, FFN_TN), lambda i, j: (0, j))
    o_spec = pl.BlockSpec((tm, FFN_TN), lambda i, j: (i, j))
    return pl.pallas_call(
        kern, out_shape=[jax.ShapeDtypeStruct((m, n), F32)] * 2 + [jax.ShapeDtypeStruct((m, n), BF16)],
        grid=(m // tm, n // FFN_TN), in_specs=[pl.BlockSpec((tm, k_dim), lambda i, j: (i, 0)), w_spec, w_spec],
        out_specs=[o_spec] * 3, compiler_params=_params(("parallel", "parallel")), name=name)(h, w_gate, w_up)


def _mm_swiglu_bwd(dx, w_down, g, u, *, name):
    m, k_dim = dx.shape
    n = w_down.shape[0]
    tm = _tile(m, (1024, 512, 256, 128))

    def kern(dx_ref, w_ref, g_ref, u_ref, dg_ref, du_ref):
        df = lax.dot_general(dx_ref[...].astype(BF16), w_ref[...].astype(BF16), (((1,), (1,)), ((), ())),
                             preferred_element_type=F32)
        _, vjp = jax.vjp(_swiglu, g_ref[...], u_ref[...])
        dg, du = vjp(df)
        dg_ref[...] = dg.astype(dg_ref.dtype)
        du_ref[...] = du.astype(du_ref.dtype)

    o_spec = pl.BlockSpec((tm, FFN_TN), lambda i, j: (i, j))
    return pl.pallas_call(
        kern, out_shape=[jax.ShapeDtypeStruct((m, n), BF16)] * 2, grid=(m // tm, n // FFN_TN),
        in_specs=[pl.BlockSpec((tm, k_dim), lambda i, j: (i, 0)), pl.BlockSpec((FFN_TN, k_dim), lambda i, j: (j, 0)),
                  o_spec, o_spec],
        out_specs=[o_spec] * 2, compiler_params=_params(("parallel", "parallel")), name=name)(dx, w_down, g, u)


def _rowwise(fn, rows, consts, out_rows, out_accs, *, name, tb=ROW_TILE, out_dtype=F32):
    t = (rows[0][0] if isinstance(rows[0], tuple) else rows[0]).shape[0]
    n_r, n_c, n_o, n_a = len(rows), len(consts), len(out_rows), len(out_accs)
    pieces = [w if isinstance(w, (list, tuple)) else [w] for w in out_rows]

    def kern(*refs):
        r_refs = refs[:n_r]
        c_refs = refs[n_r:n_r + n_c]
        o_refs = refs[n_r + n_c:n_r + n_c + n_o]
        a_refs = refs[n_r + n_c + n_o:]
        vals = fn(*[r[...] for r in r_refs], *[c[...] for c in c_refs])
        vals = list(vals) if isinstance(vals, (tuple, list)) else [vals]
        pos = 0
        for o_ref, ws in zip(o_refs, pieces):
            off = 0
            for w in ws:
                o_ref[:, off:off + w] = vals[pos].astype(o_ref.dtype)
                off += w
                pos += 1
        if n_a:
            @pl.when(pl.program_id(0) == 0)
            def _():
                for a_ref in a_refs:
                    a_ref[...] = jnp.zeros_like(a_ref)
            for a_ref, v in zip(a_refs, vals[pos:]):
                a_ref[...] += v

    in_specs, args = [], []
    for r in rows:
        if isinstance(r, tuple):
            arr, blk, w = r
            in_specs.append(pl.BlockSpec((tb, w), functools.partial(lambda i, blk: (i, blk), blk=blk)))
        else:
            arr = r
            in_specs.append(pl.BlockSpec((tb, arr.shape[1]), lambda i: (i, 0)))
        args.append(arr)
    for c in consts:
        in_specs.append(pl.BlockSpec(c.shape, lambda i: (0, 0)))
        args.append(c)
    out_shape = [jax.ShapeDtypeStruct((t, sum(ws)), out_dtype) for ws in pieces]
    out_specs = [pl.BlockSpec((tb, sum(ws)), lambda i: (i, 0)) for ws in pieces]
    for shp in out_accs:
        out_shape.append(jax.ShapeDtypeStruct(shp, F32))
        out_specs.append(pl.BlockSpec(shp, lambda i: (0, 0)))
    res = pl.pallas_call(
        kern, out_shape=out_shape, grid=(t // tb,), in_specs=in_specs, out_specs=out_specs,
        compiler_params=_params(("arbitrary",) if n_a else ("parallel",)), name=name)(*args)
    return res


def _rms(x, w):
    return x * lax.rsqrt(jnp.mean(x * x, axis=-1, keepdims=True) + RMS_EPS) * w


def _seg_sum(x, bd):
    return jnp.concatenate(
        [jnp.dot(x[:, LANES * j:LANES * (j + 1)], bd, precision=HIGHEST, preferred_element_type=F32)
         for j in range(x.shape[1] // LANES)], axis=1)


@jax.custom_vjp
def _seg(x, bd):
    return _seg_sum(x, bd)


_seg.defvjp(lambda x, bd: (_seg_sum(x, bd), bd), lambda bd, ct: (_seg_sum(ct, bd), jnp.zeros_like(bd)))


def _colsum(x):
    return jnp.sum(x, axis=0, keepdims=True)


def _prescan_math(r, k, xw, xa, xg, k_k, w0f, w0b, a0f, a0b, kaf, kab, wupf, wupb, aupf, aupb, gup, bd):
    kkr = k * k_k
    norm = jnp.sqrt(_seg(kkr * kkr, bd))
    kk = kkr / jnp.maximum(norm, NORM_EPS)
    th = jnp.tanh(xw)

    def direction(w0, wup, a0, aup, ka):
        logit = w0 + jnp.dot(th, wup, preferred_element_type=F32)
        w = jnp.exp(-LOG_DECAY_SCALE * jax.nn.sigmoid(logit))
        a = jax.nn.sigmoid(a0 + jnp.dot(xa, aup, preferred_element_type=F32))
        kd = k * (1.0 + (a - 1.0) * ka)
        return w, kd, kk * a

    wf, kdf, bf = direction(w0f, wupf, a0f, aupf, kaf)
    wb, kdb, bb = direction(w0b, wupb, a0b, aupb, kab)
    g = jnp.dot(jax.nn.sigmoid(xg), gup, preferred_element_type=F32)
    return kk, r, wf, wb, bf, bb, kdf, kdb, g


def _postscan_math(y, r, v, kdf, kdb, g, gn_w, gn_b, rkf, rkb, bd):
    mean = _seg(y, bd) * (1.0 / HEAD)
    yc = y - mean
    var = _seg(yc * yc, bd) * (1.0 / HEAD)
    yg = yc * lax.rsqrt(var + GN_EPS) * gn_w + gn_b
    bonus = (_seg(r * kdf * rkf, bd) + _seg(r * kdb * rkb, bd)) * v
    return (yg + bonus) * g


def _halo_specs(width, col_blk, tb, t):
    nb = t // SUBLANES
    step = tb // SUBLANES
    main = pl.BlockSpec((tb, width), lambda i: (i, col_blk))
    prev = pl.BlockSpec((SUBLANES, width), lambda i: (jnp.maximum(i * step - 1, 0), col_blk))
    nxt = pl.BlockSpec((SUBLANES, width), lambda i: (jnp.minimum((i + 1) * step, nb - 1), col_blk))
    return [main, prev, nxt]


def _neighbours(z, prev8, next8, first, last):
    tb = z.shape[0]
    row = lax.broadcasted_iota(jnp.int32, z.shape, 0)
    prow = jnp.where(first, 0.0, prev8[SUBLANES - 1:SUBLANES, :])
    nrow = jnp.where(last, 0.0, next8[0:1, :])
    down = jnp.where(row == 0, prow, pltpu.roll(z, 1, 0))
    up = jnp.where(row == tb - 1, nrow, pltpu.roll(z, tb - 1, 0))
    return down, up


def _shift_conv_fwd(p, mu, conv_w, seq, *, name, tb=ROW_TILE):
    t = p.shape[0]
    per_seq = seq // tb

    def kern(p_ref, pp_ref, pn_ref, mu_ref, cw_ref, pss_ref, oc_ref):
        i = pl.program_id(0)
        first = (i % per_seq) == 0
        last = (i % per_seq) == per_seq - 1
        ps = p_ref[:, :D_SP]
        down, up = _neighbours(ps, pp_ref[:, :D_SP], pn_ref[:, :D_SP], first, last)
        pss_ref[...] = ps + mu_ref[...] * (0.5 * (down + up) - ps)
        gb = p_ref[:, D_SP:D_SP + D_CONV]
        u = p_ref[:, D_SP + D_CONV:D_SP + 2 * D_CONV] * p_ref[:, D_SP + 2 * D_CONV:]
        u_p = pp_ref[:, D_SP + D_CONV:D_SP + 2 * D_CONV] * pp_ref[:, D_SP + 2 * D_CONV:]
        u_n = pn_ref[:, D_SP + D_CONV:D_SP + 2 * D_CONV] * pn_ref[:, D_SP + 2 * D_CONV:]
        udown, uup = _neighbours(u, u_p, u_n, first, last)
        oc_ref[...] = gb * (cw_ref[0:1, :] * udown + cw_ref[1:2, :] * u + cw_ref[2:3, :] * uup)

    return pl.pallas_call(
        kern,
        out_shape=[jax.ShapeDtypeStruct((t, D_SP), F32), jax.ShapeDtypeStruct((t, D_CONV), F32)],
        grid=(t // tb,),
        in_specs=_halo_specs(D_INP, 0, tb, t) + [pl.BlockSpec((1, D_SP), lambda i: (0, 0)),
                                                 pl.BlockSpec((SUBLANES, D_CONV), lambda i: (0, 0))],
        out_specs=[pl.BlockSpec((tb, D_SP), lambda i: (i, 0)), pl.BlockSpec((tb, D_CONV), lambda i: (i, 0))],
        compiler_params=_params(("parallel",)), name=name)(p, p, p, mu, conv_w)


def _shift_conv_bwd(p, d_pss, d_o, mu, conv_w, seq, *, name, tb=ROW_TILE):
    t = p.shape[0]
    per_seq = seq // tb

    def kern(p_ref, pp_ref, pn_ref, d_ref, dp_ref, dn_ref, do_ref, dop_ref, don_ref, mu_ref, cw_ref,
             out_ref, dmu_ref, dcw_ref):
        i = pl.program_id(0)
        first = (i % per_seq) == 0
        last = (i % per_seq) == per_seq - 1

        @pl.when(i == 0)
        def _():
            dmu_ref[...] = jnp.zeros_like(dmu_ref)
            dcw_ref[...] = jnp.zeros_like(dcw_ref)

        mu_v = mu_ref[...]
        ps = p_ref[:, :D_SP]
        down, up = _neighbours(ps, pp_ref[:, :D_SP], pn_ref[:, :D_SP], first, last)
        d = d_ref[...]
        ddown, dup = _neighbours(d, dp_ref[...], dn_ref[...], first, last)
        out_ref[:, :D_SP] = (d - mu_v * d + 0.5 * (mu_v * ddown + mu_v * dup)).astype(out_ref.dtype)
        dmu_ref[...] += _colsum(d * (0.5 * (down + up) - ps))

        def parts(ref):
            return (ref[:, D_SP:D_SP + D_CONV], ref[:, D_SP + D_CONV:D_SP + 2 * D_CONV],
                    ref[:, D_SP + 2 * D_CONV:])

        gb, gc, hh = parts(p_ref)
        gb_p, gc_p, hh_p = parts(pp_ref)
        gb_n, gc_n, hh_n = parts(pn_ref)
        u = gc * hh
        udown, uup = _neighbours(u, gc_p * hh_p, gc_n * hh_n, first, last)
        cw0, cw1, cw2 = cw_ref[0:1, :], cw_ref[1:2, :], cw_ref[2:3, :]
        do = do_ref[...]
        duc = do * gb
        ducdown, ducup = _neighbours(duc, dop_ref[...] * gb_p, don_ref[...] * gb_n, first, last)
        du = cw0 * ducup + cw1 * duc + cw2 * ducdown
        out_ref[:, D_SP:D_SP + D_CONV] = (do * (cw0 * udown + cw1 * u + cw2 * uup)).astype(out_ref.dtype)
        out_ref[:, D_SP + D_CONV:D_SP + 2 * D_CONV] = (du * hh).astype(out_ref.dtype)
        out_ref[:, D_SP + 2 * D_CONV:] = (du * gc).astype(out_ref.dtype)
        dcw_ref[0:1, :] += _colsum(duc * udown)
        dcw_ref[1:2, :] += _colsum(duc * u)
        dcw_ref[2:3, :] += _colsum(duc * uup)

    return pl.pallas_call(
        kern,
        out_shape=[jax.ShapeDtypeStruct((t, D_INP), BF16), jax.ShapeDtypeStruct((1, D_SP), F32),
                   jax.ShapeDtypeStruct((SUBLANES, D_CONV), F32)],
        grid=(t // tb,),
        in_specs=(_halo_specs(D_INP, 0, tb, t) + _halo_specs(D_SP, 0, tb, t) + _halo_specs(D_CONV, 1, tb, t)
                  + [pl.BlockSpec((1, D_SP), lambda i: (0, 0)),
                     pl.BlockSpec((SUBLANES, D_CONV), lambda i: (0, 0))]),
        out_specs=[pl.BlockSpec((tb, D_INP), lambda i: (i, 0)), pl.BlockSpec((1, D_SP), lambda i: (0, 0)),
                   pl.BlockSpec((SUBLANES, D_CONV), lambda i: (0, 0))],
        compiler_params=_params(("arbitrary",)), name=name)(p, p, p, d_pss, d_pss, d_pss, d_o, d_o, d_o, mu, conv_w)


N_CHAIN = 16
N_GROUP = LANES // N_CHAIN
V_HI = HEAD // SUBLANES
G_KK, G_R, G_W, G_B, G_KD = 0, 1, (2, 3), (4, 5), (6, 7)


K_HI = HEAD // SUBLANES


def _tree_sum(terms):
    terms = list(terms)
    while len(terms) > 1:
        terms = [a + b for a, b in zip(terms[::2], terms[1::2])]
    return terms[0]


def _kscan_specs(nc):
    same = lambda c: c
    mirror = lambda c: nc - 1 - c

    def k_spec(fn):
        return pl.BlockSpec((SCAN_CHUNK, HEAD, LANES), lambda c: (fn(c), 0, 0))

    def v_spec(fn):
        return pl.BlockSpec((SCAN_CHUNK, SUBLANES, LANES), lambda c: (fn(c), 0, 0))

    return same, mirror, k_spec, v_spec


ST_SHAPE = (2, K_HI, V_HI, SUBLANES, LANES)


def _lane_group_index():
    lane = lax.broadcasted_iota(jnp.int32, (SUBLANES, LANES), 1)
    return lax.shift_right_logical(lane, jnp.full_like(lane, 4))


def _spread_groups(x, grp):
    rolled = [x] + [pltpu.roll(x, s * N_CHAIN, 1) for s in range(1, N_GROUP)]
    out = []
    for j in range(N_GROUP):
        t = rolled[(0 - j) % N_GROUP]
        for g in range(1, N_GROUP):
            t = jnp.where(grp == g, rolled[(g - j) % N_GROUP], t)
        out.append(t)
    return out


def _gather_groups(tiles, grp):
    total = None
    for s in range(N_GROUP):
        b = tiles[s % N_GROUP]
        for g in range(1, N_GROUP):
            b = jnp.where(grp == g, tiles[(g + s) % N_GROUP], b)
        b = pltpu.roll(b, s * N_CHAIN, 1) if s else b
        total = b if total is None else total + b
    return total


def _lane_group_sum(x):
    return _tree_sum([x] + [pltpu.roll(x, k * N_CHAIN, 1) for k in range(1, N_GROUP)])


def _key_row(x_t, grp, kh):
    r = SUBLANES * grp + kh
    return jnp.broadcast_to(x_t[r:r + 1, :], (SUBLANES, LANES))


def _acc(total, term):
    return term if total is None else total + term


SA_SHAPE = (2, V_HI, SUBLANES, LANES)


def _scan_fwd(xall, v_c, *, gather=(), name):
    steps = xall.shape[0]
    nc = steps // SCAN_CHUNK
    same, mirror, k_spec, v_spec = _kscan_specs(nc)
    last = SCAN_CHUNK - 1
    n_x = len(gather)

    def kern(*refs):
        xf_ref, xb_ref, vf_ref, vb_ref = refs[:4]
        yf_ref, yb_ref, hist_ref, fin_ref, sa_ref = refs[4 + n_x:9 + n_x]
        st_ref = refs[9 + 2 * n_x]
        c = pl.program_id(0)

        def riders():
            return _exchange_copies(refs[4:4 + n_x], refs[9 + n_x:9 + 2 * n_x], 0, *refs[10 + 2 * n_x:])

        @pl.when(c == 0)
        def _():
            st_ref[...] = jnp.zeros_like(st_ref)
            if n_x:
                for cp in riders():
                    cp.start()

        hist_ref[0] = st_ref[...]
        grp = _lane_group_index()

        def body(i, put):
            j = last - i
            for d, (x_t, v_t, y_ref, at) in enumerate(((xf_ref[i], vf_ref[i], yf_ref, i),
                                                       (xb_ref[j], vb_ref[j], yb_ref, j))):
                v_b = _spread_groups(v_t, grp)
                part = [None] * V_HI
                for kh in range(K_HI):
                    kk_r = _key_row(x_t, G_KK, kh)
                    for vh in range(V_HI):
                        part[vh] = _acc(part[vh], hist_ref[i, d, kh, vh] * kk_r)
                sa = [_lane_group_sum(p) for p in part]
                for vh in range(V_HI):
                    sa_ref[i, d, vh] = sa[vh]
                y_p = [None] * V_HI
                for kh in range(K_HI):
                    r_r, w_r = _key_row(x_t, G_R, kh), _key_row(x_t, G_W[d], kh)
                    b_r, kd_r = _key_row(x_t, G_B[d], kh), _key_row(x_t, G_KD[d], kh)
                    for vh in range(V_HI):
                        new = hist_ref[i, d, kh, vh] * w_r - sa[vh] * b_r + v_b[vh] * kd_r
                        put(d, kh, vh, new)
                        y_p[vh] = _acc(y_p[vh], new * r_r)
                y_ref[at] = _gather_groups(y_p, grp)

        def step(i, carry):
            def put(d, kh, vh, val):
                hist_ref[i + 1, d, kh, vh] = val
            body(i, put)
            return carry

        lax.fori_loop(0, last, step, 0, unroll=SCAN_UNROLL)

        def put_carry(d, kh, vh, val):
            st_ref[d, kh, vh] = val

        body(last, put_carry)

        @pl.when(c == nc - 1)
        def _():
            fin_ref[...] = st_ref[...]
            if n_x:
                for cp in riders():
                    cp.wait()

    return pl.pallas_call(
        kern,
        out_shape=[jax.ShapeDtypeStruct((steps, SUBLANES, LANES), F32)] * 2
        + [jax.ShapeDtypeStruct((steps,) + ST_SHAPE, F32), jax.ShapeDtypeStruct(ST_SHAPE, F32),
           jax.ShapeDtypeStruct((steps,) + SA_SHAPE, F32)]
        + _exchange_out_shapes(gather, 0),
        grid=(nc,), in_specs=[k_spec(same), k_spec(mirror), v_spec(same), v_spec(mirror)] + _hbm_specs(n_x),
        out_specs=[v_spec(same), v_spec(mirror),
                   pl.BlockSpec((SCAN_CHUNK,) + ST_SHAPE, lambda c: (c, 0, 0, 0, 0, 0)),
                   pl.BlockSpec(ST_SHAPE, lambda c: (0, 0, 0, 0, 0)),
                   pl.BlockSpec((SCAN_CHUNK,) + SA_SHAPE, lambda c: (c, 0, 0, 0, 0))] + _hbm_specs(n_x),
        scratch_shapes=[pltpu.VMEM(ST_SHAPE, F32)] + (_exchange_sems(n_x) if n_x else []),
        compiler_params=_params(("arbitrary",)), name=name)(xall, xall, v_c, v_c, *gather)


def _scan_bwd(xall, v_c, dy_c, hist, fin, sa, *, exchange=(), name):
    steps = xall.shape[0]
    nc = steps // SCAN_CHUNK
    same, back, k_spec, v_spec = _kscan_specs(nc)
    last = SCAN_CHUNK - 1
    n_x = len(exchange)

    def kern(*refs):
        xf_ref, xb_ref, vf_ref, vb_ref, dyf_ref, dyb_ref, hist_ref, fin_ref, sa_ref = refs[:9]
        gf_ref, gb_ref, dvf_ref, dvb_ref = refs[9 + n_x:13 + n_x]
        ds_ref, after_ref = refs[13 + 2 * n_x:15 + 2 * n_x]
        c = pl.program_id(0)

        def riders():
            return _exchange_copies(refs[9:9 + n_x], refs[13 + n_x:13 + 2 * n_x], n_x, *refs[15 + 2 * n_x:])

        @pl.when(c == 0)
        def _():
            ds_ref[...] = jnp.zeros_like(ds_ref)
            after_ref[...] = fin_ref[...]
            if n_x:
                for cp in riders():
                    cp.start()

        grp = _lane_group_index()
        row = lax.broadcasted_iota(jnp.int32, (SUBLANES, LANES), 0)
        zero = jnp.zeros((SUBLANES, LANES), F32)

        def body(i, after):
            j = last - i
            for d, (x_t, v_t, dy_t, g_ref, dv_ref, at) in enumerate((
                    (xf_ref[i], vf_ref[i], dyf_ref[i], gf_ref, dvf_ref, i),
                    (xb_ref[j], vb_ref[j], dyb_ref[j], gb_ref, dvb_ref, j))):
                v_s, dy_s = _spread_groups(v_t, grp), _spread_groups(dy_t, grp)
                dsa_p, dv_p = [None] * V_HI, [None] * V_HI
                for kh in range(K_HI):
                    r_r = _key_row(x_t, G_R, kh)
                    b_r, kd_r = _key_row(x_t, G_B[d], kh), _key_row(x_t, G_KD[d], kh)
                    for vh in range(V_HI):
                        g = ds_ref[d, kh, vh] + dy_s[vh] * r_r
                        ds_ref[d, kh, vh] = g
                        dsa_p[vh] = _acc(dsa_p[vh], g * b_r)
                        dv_p[vh] = _acc(dv_p[vh], g * kd_r)
                dsa = [-_lane_group_sum(p) for p in dsa_p]
                sa = [sa_ref[i, d, vh] for vh in range(V_HI)]
                dv_ref[at] = _gather_groups(dv_p, grp)
                blocks = {G_KK: zero, G_R: zero, G_W[d]: zero, G_B[d]: zero, G_KD[d]: zero}
                for kh in range(K_HI):
                    w_r, kk_r = _key_row(x_t, G_W[d], kh), _key_row(x_t, G_KK, kh)
                    dkk = dr = dw = db = dkd = None
                    for vh in range(V_HI):
                        g, before = ds_ref[d, kh, vh], hist_ref[i, d, kh, vh]
                        dr = _acc(dr, after(d, kh, vh) * dy_s[vh])
                        dw = _acc(dw, g * before)
                        dkd = _acc(dkd, g * v_s[vh])
                        db = _acc(db, g * sa[vh])
                        dkk = _acc(dkk, before * dsa[vh])
                        ds_ref[d, kh, vh] = g * w_r + dsa[vh] * kk_r
                    for gi, a in ((G_KK, dkk), (G_R, dr), (G_W[d], dw), (G_B[d], -db), (G_KD[d], dkd)):
                        blocks[gi] = jnp.where(row == kh, _colsum(a), blocks[gi])
                for gi in range(N_GROUP):
                    g_ref[at, SUBLANES * gi:SUBLANES * (gi + 1), :] = blocks.get(gi, zero)

        body(last, lambda d, kh, vh: after_ref[d, kh, vh])

        def step(ii, carry):
            i = last - ii
            body(i, lambda d, kh, vh: hist_ref[i + 1, d, kh, vh])
            return carry

        lax.fori_loop(1, SCAN_CHUNK, step, 0, unroll=SCAN_UNROLL)
        after_ref[...] = hist_ref[0]

        if n_x:
            @pl.when(c == nc - 1)
            def _():
                for cp in riders():
                    cp.wait()

    return pl.pallas_call(
        kern,
        out_shape=[jax.ShapeDtypeStruct((steps, HEAD, LANES), F32)] * 2
        + [jax.ShapeDtypeStruct((steps, SUBLANES, LANES), F32)] * 2 + _exchange_out_shapes(exchange, n_x),
        grid=(nc,),
        in_specs=[k_spec(back), k_spec(same), v_spec(back), v_spec(same), v_spec(back), v_spec(same),
                  pl.BlockSpec((SCAN_CHUNK,) + ST_SHAPE, lambda c: (back(c), 0, 0, 0, 0, 0)),
                  pl.BlockSpec(ST_SHAPE, lambda c: (0, 0, 0, 0, 0)),
                  pl.BlockSpec((SCAN_CHUNK,) + SA_SHAPE, lambda c: (back(c), 0, 0, 0, 0))] + _hbm_specs(n_x),
        out_specs=[k_spec(back), k_spec(same), v_spec(back), v_spec(same)] + _hbm_specs(n_x),
        scratch_shapes=[pltpu.VMEM(ST_SHAPE, F32), pltpu.VMEM(ST_SHAPE, F32)]
        + (_exchange_sems(n_x) if n_x else []),
        compiler_params=_params(("arbitrary",)), name=name)(xall, xall, v_c, v_c, dy_c, dy_c, hist, fin, sa,
                                                            *exchange)


def _to_key_rows(wide, bsz, seq):
    z = wide.reshape(bsz, seq, N_GROUP, N_HEAD, K_HI, SUBLANES).transpose(1, 2, 4, 5, 0, 3)
    return z.reshape(seq, HEAD, LANES)


def _from_key_rows(g, bsz, seq):
    z = g.reshape(seq, N_GROUP, K_HI, SUBLANES, bsz, N_HEAD).transpose(4, 0, 1, 5, 2, 3)
    return z.reshape(bsz * seq, N_GROUP * D_RWKV)


def _to_value_rows(a, bsz, seq):
    z = a.reshape(bsz, seq, N_HEAD, V_HI, SUBLANES).transpose(1, 4, 3, 0, 2)
    return z.reshape(seq, SUBLANES, LANES)


def _from_value_rows(y, bsz, seq):
    z = y.reshape(seq, SUBLANES, V_HI, bsz, N_HEAD).transpose(3, 0, 4, 2, 1)
    return z.reshape(bsz * seq, D_RWKV)


def _pad_cols(a, segs):
    out, off = [], 0
    for w, wp in segs:
        out.append(a[..., off:off + w])
        if wp > w:
            out.append(jnp.zeros(a.shape[:-1] + (wp - w,), a.dtype))
        off += w
    return jnp.concatenate(out, axis=-1)


def _unpad_cols(a, segs):
    out, off = [], 0
    for w, wp in segs:
        out.append(a[..., off:off + w])
        off += wp
    return jnp.concatenate(out, axis=-1)


P_SEGS = ((3 * D_RWKV, 3 * D_RWKV), (D_LORA, 128), (D_LORA, 128), (D_GATE, 256), (3 * D_CONV, 3 * D_CONV))
S_SEGS = P_SEGS[:4]


def _pad_rows(a, rows):
    return jnp.concatenate([a, jnp.zeros((rows - a.shape[0], a.shape[1]), a.dtype)], axis=0)


LATE = ("w_out", "w_gate", "w_up", "w_down")


def _local_step(x, target, w, late=None):
    bsz, seq, _ = x.shape
    t = bsz * seq
    x2d = x.reshape(t, D_MODEL)
    tg2d = target.reshape(t, D_MODEL)
    row = lambda a: a.reshape(1, -1).astype(F32)

    w_in = _pad_cols(w["w_in"][0], P_SEGS)
    mu = _pad_cols(row(w["mu_shift"]), S_SEGS)
    wupf, wupb, aupf, aupb = (_pad_rows(w[n][0].astype(F32), 128) for n in ("w_up_f", "w_up_b", "a_up_f", "a_up_b"))
    gup = _pad_rows(w["g_up"][0].astype(F32), 256)
    conv_w = _pad_rows(w["conv_w"][0].astype(F32), SUBLANES)
    norm1, norm2, normf = row(w["norm1_w"]), row(w["norm2_w"]), row(w["norm_f_w"])
    vec = {n: row(w[n]) for n in VEC}
    head_of = jnp.arange(LANES) // HEAD
    bd = (head_of[:, None] == head_of[None, :]).astype(F32)
    pre_consts = [vec["k_k"], vec["w0_f"], vec["w0_b"], vec["a0_f"], vec["a0_b"], vec["k_a_f"], vec["k_a_b"],
                  wupf, wupb, aupf, aupb, gup, bd]
    post_consts = [vec["gn_w"], vec["gn_b"], vec["r_k_f"], vec["r_k_b"], bd]

    h1, = _rowwise(_rms, [x2d], [norm1], [D_MODEL], [], name="rms1_fwd", out_dtype=BF16, tb=WIDE_TILE)
    p = _mm(h1, w_in, name="mm_in")
    pss, oconv = _shift_conv_fwd(p, mu, conv_w, seq, name="shift_conv_fwd")
    pre_rows = [(pss, 0, 512), (pss, 1, 512), (pss, XW0 // 128, 128), (pss, XA0 // 128, 128), (pss, XG0 // 256, 256)]
    sc, g = _rowwise(_prescan_math, pre_rows, pre_consts, [[D_RWKV] * N_GROUP, D_RWKV], [], name="prescan_fwd")
    xall = _to_key_rows(sc, bsz, seq)
    v_l = _to_value_rows(pss[:, 2 * D_RWKV:3 * D_RWKV], bsz, seq)
    y_f, y_b, hist, fin, sa, *gathered = _scan_fwd(xall, v_l, gather=[late[n] for n in LATE] if late else (),
                                                   name="scan_fwd")
    w_out, w_gate, w_up, w_down = (
        (_from_slots(a, SHARD_AXIS[n]) if late else w[n])[0] for n, a in zip(LATE, gathered or LATE))
    y = _from_value_rows(y_f + y_b, bsz, seq)
    post_rows = [y, (pss, 0, 512), (pss, 2, 512), (sc, G_KD[0], 512), (sc, G_KD[1], 512), g]

    def post_fwd(y_, r_, v_, kdf_, kdb_, g_, oc_, *consts):
        return _postscan_math(y_, r_, v_, kdf_, kdb_, g_, *consts), oc_

    o, = _rowwise(post_fwd, post_rows + [oconv], post_consts, [[D_RWKV, D_CONV]], [], name="postscan_fwd",
                  out_dtype=BF16)
    x1 = _mm(o, w_out, add=x2d, name="mm_out")
    h2, = _rowwise(_rms, [x1], [norm2], [D_MODEL], [], name="rms2_fwd", out_dtype=BF16, tb=WIDE_TILE)
    gg, uu, ff = _mm_swiglu(h2, w_gate, w_up, name="mm_gate_up")
    x2 = _mm(ff, w_down, add=x1, name="mm_down")

    def final(x_, tg_, wn_):
        yo, vjp = jax.vjp(_rms, x_, wn_)
        err = yo - tg_
        dx_, dwn_ = vjp(err * (1.0 / D_MODEL))
        part = jnp.sum(jnp.sum(err * err, axis=1, keepdims=True), axis=0, keepdims=True) * (0.5 / D_MODEL)
        return dx_, part + jnp.zeros((1, LANES), F32), dwn_

    dx2, loss_acc, d_normf = _rowwise(final, [x2, tg2d], [normf], [D_MODEL], [(1, LANES), (1, D_MODEL)],
                                      name="loss_head", tb=WIDE_TILE)
    dgg, duu = _mm_swiglu_bwd(dx2, w_down, gg, uu, name="mm_down_dx")
    g_w_down = _mm(ff, dx2, ta=True, name="mm_down_dw")
    dh2 = _mm(dgg, w_gate, tb=True, name="mm_gate_dx")
    dh2 = _mm(duu, w_up, tb=True, add=dh2, name="mm_up_dx")
    g_w_gate = _mm(h2, dgg, ta=True, name="mm_gate_dw")
    g_w_up = _mm(h2, duu, ta=True, name="mm_up_dw")

    def rms_bwd(x_, dh_, dres_, wn_):
        _, vjp = jax.vjp(_rms, x_, wn_)
        dx_, dwn_ = vjp(dh_)
        return dx_ + dres_, dwn_

    dx1, d_norm2 = _rowwise(rms_bwd, [x1, dh2, dx2], [norm2], [D_MODEL], [(1, D_MODEL)], name="rms2_bwd", tb=WIDE_TILE)
    do = _mm(dx1, w_out, tb=True, name="mm_out_dx")
    g_w_out = _mm(o, dx1, ta=True, name="mm_out_dw")

    def post_bwd(y_, r_, v_, kdf_, kdb_, g_, do_, *consts):
        _, vjp = jax.vjp(lambda *a: _postscan_math(*a, consts[4]), y_, r_, v_, kdf_, kdb_, g_, *consts[:4])
        return vjp(do_)

    (dy, dr_c, dv_c, dkdf_c, dkdb_c, dg, d_gn_w, d_gn_b, d_rkf, d_rkb) = _rowwise(
        post_bwd, post_rows + [(do, 0, 512)], post_consts, [D_RWKV] * 6, [(1, D_RWKV)] * 4, name="postscan_bwd")
    dy_l = _to_value_rows(dy, bsz, seq)
    late_grads = {"w_out": g_w_out[None], "w_gate": g_w_gate[None], "w_up": g_w_up[None], "w_down": g_w_down[None]}
    g_f, g_b, dv_f, dv_b, *late_parts = _scan_bwd(
        xall, v_l, dy_l, hist, fin, sa, name="scan_bwd",
        exchange=[_to_slots(late_grads[n], SHARD_AXIS[n]).astype(BF16) for n in LATE] if late else ())
    dsc = _from_key_rows(g_f + g_b, bsz, seq)
    dv_s = _from_value_rows(dv_f + dv_b, bsz, seq)

    def pre_bwd(r_, k_, xw_, xa_, xg_, dkk_, dr_s, dwf_, dwb_, dbf_, dbb_, dkdf_s, dkdb_s,
                dr_c_, dv_c_, dv_s_, dkdf_c_, dkdb_c_, dg_, *consts):
        _, vjp = jax.vjp(lambda *a: _prescan_math(*a, consts[-1]), r_, k_, xw_, xa_, xg_, *consts[:-1])
        grads = vjp((dkk_, dr_s + dr_c_, dwf_, dwb_, dbf_, dbb_, dkdf_s + dkdf_c_, dkdb_s + dkdb_c_, dg_))
        dr_, dk_, dxw_, dxa_, dxg_ = grads[:5]
        return (dr_, dk_, dv_c_ + dv_s_, dxw_, dxa_, dxg_) + tuple(grads[5:])

    pre_b_rows = (pre_rows + [(dsc, j, 512) for j in range(N_GROUP)]
                  + [dr_c, dv_c, dv_s, dkdf_c, dkdb_c, dg])
    pre_b = _rowwise(pre_bwd, pre_b_rows, pre_consts, [[512, 512, 512, 128, 128, 256]],
                     [(1, D_RWKV)] * 7 + [(128, D_RWKV)] * 4 + [(256, D_RWKV)], name="prescan_bwd")
    d_pss = pre_b[0]
    d_kk_, d_w0f, d_w0b, d_a0f, d_a0b, d_kaf, d_kab, d_wupf, d_wupb, d_aupf, d_aupb, d_gup = pre_b[1:]
    dp, d_mu, d_conv = _shift_conv_bwd(p, d_pss, do, mu, conv_w, seq, name="shift_conv_bwd")
    g_w_in = _mm(h1, dp, ta=True, name="mm_in_dw")
    grads = {
        "w_in": _unpad_cols(g_w_in, P_SEGS)[None], "mu_shift": _unpad_cols(d_mu, S_SEGS),
        "w_up_f": d_wupf[None, :D_LORA], "w0_f": d_w0f, "w_up_b": d_wupb[None, :D_LORA], "w0_b": d_w0b,
        "a_up_f": d_aupf[None, :D_LORA], "a0_f": d_a0f, "a_up_b": d_aupb[None, :D_LORA], "a0_b": d_a0b,
        "g_up": d_gup[None, :D_GATE], "k_k": d_kk_, "k_a_f": d_kaf, "k_a_b": d_kab,
        "r_k_f": d_rkf, "r_k_b": d_rkb, "gn_w": d_gn_w, "gn_b": d_gn_b, "conv_w": d_conv[None, :3],
        "w_out": g_w_out[None], "norm2_w": d_norm2, "w_gate": g_w_gate[None], "w_up": g_w_up[None],
        "w_down": g_w_down[None], "norm_f_w": d_normf,
    }
    early = ("w_in",) + LORA
    parts = dict(zip(LATE, late_parts))
    if late:
        vec_rows = jnp.concatenate([grads[n] for n in VEC] + [jnp.zeros((16 - len(VEC), D_RWKV), F32)], axis=0)
        slots = [_to_slots(grads[n], SHARD_AXIS[n]).astype(BF16 if n in BIG else F32) for n in early]
        dh1, *recv = _mm(dp, w_in, tb=True, exchange=(slots, [vec_rows]), name="mm_in_dx")
        parts.update(zip(early + ("vec",), recv))
    else:
        dh1 = _mm(dp, w_in, tb=True, name="mm_in_dx")
    dx, grads["norm1_w"] = _rowwise(rms_bwd, [x2d, dh1, dx1], [norm1], [D_MODEL], [(1, D_MODEL)], name="rms1_bwd",
                                    tb=WIDE_TILE)
    return loss_acc, dx.reshape(bsz, seq, D_MODEL), grads, parts


def _hbm_specs(n):
    return [pl.BlockSpec(memory_space=pl.ANY)] * n


def _all_gather(arrs, *, name):
    n = len(arrs)

    def body(*refs):
        x_refs, out_refs = refs[:n], refs[n:2 * n]
        send_sems, recv_sems, local_sems = refs[2 * n:]
        x, y, c = lax.axis_index("x"), lax.axis_index("y"), lax.axis_index("c")
        me, sibling = (x, y, c), (x, y, 1 - c)
        chips = [(1 - x, y), (x, 1 - y), (1 - x, 1 - y)]

        def slot(a, px, py, pc):
            return out_refs[a].at[4 * px + 2 * py + pc]

        def copy(a, k, block, to, src=None):
            return pltpu.make_async_remote_copy(
                src_ref=slot(a, *block) if src is None else src, dst_ref=slot(a, *block),
                send_sem=send_sems.at[k, a], recv_sem=recv_sems.at[k, a],
                device_id=to, device_id_type=pl.DeviceIdType.MESH)

        mine = [pltpu.make_async_copy(x_refs[a], slot(a, *me), local_sems.at[a]) for a in range(n)]
        for cp in mine:
            cp.start()
        first = []
        for a in range(n):
            first.append(copy(a, 0, me, sibling, src=x_refs[a]))
            first += [copy(a, 1 + j, me, (*chip, c), src=x_refs[a]) for j, chip in enumerate(chips)]
        for cp in first:
            cp.start()
        passed = []
        for j, chip in enumerate(chips):
            for a in range(n):
                copy(a, 1 + j, (*chip, c), me).wait_recv()
                cp = copy(a, 4 + j, (*chip, c), sibling)
                cp.start()
                passed.append(cp)
        for a in range(n):
            copy(a, 0, sibling, me).wait_recv()
            for j, chip in enumerate(chips):
                copy(a, 4 + j, (*chip, 1 - c), me).wait_recv()
        for cp in first + passed:
            cp.wait_send()
        for cp in mine:
            cp.wait()

    return pl.pallas_call(
        body, out_shape=[jax.ShapeDtypeStruct((N_DEV,) + a.shape, a.dtype) for a in arrs],
        in_specs=_hbm_specs(n), out_specs=_hbm_specs(n),
        scratch_shapes=[pltpu.SemaphoreType.DMA((7, n)), pltpu.SemaphoreType.DMA((7, n)),
                        pltpu.SemaphoreType.DMA((n,))],
        name=name)(*arrs)


def _exchange(sliced, whole, *, name):
    arrs = list(sliced) + list(whole)
    n, n_sliced = len(arrs), len(sliced)

    def body(*refs):
        copies = _exchange_copies(refs[:n], refs[n:2 * n], n_sliced, *refs[2 * n:])
        for cp in copies:
            cp.start()
        for cp in copies:
            cp.wait()

    return pl.pallas_call(
        body, out_shape=_exchange_out_shapes(arrs, n_sliced), in_specs=_hbm_specs(n), out_specs=_hbm_specs(n),
        scratch_shapes=_exchange_sems(n), name=name)(*arrs)


def _exchange_out_shapes(arrs, n_sliced):
    return [jax.ShapeDtypeStruct(a.shape if i < n_sliced else (N_DEV,) + a.shape, a.dtype)
            for i, a in enumerate(arrs)]


def _exchange_sems(n):
    return [pltpu.SemaphoreType.DMA((7, n)), pltpu.SemaphoreType.DMA((7, n)), pltpu.SemaphoreType.DMA((n,))]


def _exchange_copies(in_refs, out_refs, n_sliced, send_sems, recv_sems, local_sems):
    n = len(in_refs)
    x, y, c = lax.axis_index("x"), lax.axis_index("y"), lax.axis_index("c")
    me = 4 * x + 2 * y + c

    def src(a, dev):
        return in_refs[a].at[dev] if a < n_sliced else in_refs[a]

    copies = [pltpu.make_async_copy(src(a, me), out_refs[a].at[me], local_sems.at[a]) for a in range(n)]
    for k in range(1, N_DEV):
        px = 1 - x if k & 4 else x
        py = 1 - y if k & 2 else y
        pc = 1 - c if k & 1 else c
        for a in range(n):
            copies.append(pltpu.make_async_remote_copy(
                src_ref=src(a, 4 * px + 2 * py + pc), dst_ref=out_refs[a].at[me],
                send_sem=send_sems.at[k - 1, a], recv_sem=recv_sems.at[k - 1, a],
                device_id=(px, py, pc), device_id_type=pl.DeviceIdType.MESH))
    return copies


def _adam_math(g, w, m, v):
    nm = ADAM_B1 * m + (1.0 - ADAM_B1) * g
    nv = ADAM_B2 * v + (1.0 - ADAM_B2) * (g * g)
    m_hat = nm / (1.0 - ADAM_B1 ** ADAM_STEP)
    v_hat = nv / (1.0 - ADAM_B2 ** ADAM_STEP)
    return -ADAM_LR * (m_hat / (jnp.sqrt(v_hat) + ADAM_EPS) + ADAM_WD * w), nm, nv


def _slot_sum(ref):
    g = ref[0].astype(F32)
    for s in range(1, N_DEV):
        g = g + ref[s].astype(F32)
    return g


def _adamw_big(parts, w, m, v, *, name):
    _, rws, cols = w.shape
    tr = _tile(rws, (256, 176, 128))

    def kern(p_ref, w_ref, m_ref, v_ref, g_ref, d_ref, nm_ref, nv_ref):
        g = _slot_sum(p_ref)
        g_ref[...] = g
        d_ref[...], nm_ref[...], nv_ref[...] = _adam_math(g, w_ref[...], m_ref[...], v_ref[...])

    spec = pl.BlockSpec((1, tr, cols), lambda i: (0, i, 0))
    return pl.pallas_call(
        kern, out_shape=[jax.ShapeDtypeStruct(w.shape, F32)] * 4, grid=(rws // tr,),
        in_specs=[pl.BlockSpec((N_DEV, 1, tr, cols), lambda i: (0, 0, i, 0)), spec, spec, spec],
        out_specs=[spec] * 4, compiler_params=_params(("parallel",)), name=name)(parts, w, m, v)


def _adamw_small(lora_parts, vec_parts, wide_parts, wmv, *, name):
    names = LORA + VEC + WIDE
    n_l, n = len(LORA), len(names)
    flat = [a for trip in wmv for a in trip]

    def kern(*refs):
        l_refs, vec_ref, wide_ref = refs[:n_l], refs[n_l], refs[n_l + 1]
        in_refs = refs[n_l + 2:n_l + 2 + 3 * n]
        out_refs = refs[n_l + 2 + 3 * n:]
        vec_sum, wide_sum = _slot_sum(vec_ref), _slot_sum(wide_ref)
        for i, nm in enumerate(names):
            w_ref, m_ref, v_ref = in_refs[3 * i:3 * i + 3]
            if i < n_l:
                g = _slot_sum(l_refs[i])
            elif nm in VEC:
                g = vec_sum[i - n_l:i - n_l + 1, :]
            else:
                g = wide_sum[WIDE.index(nm):WIDE.index(nm) + 1, :w_ref.shape[-1]]
            o = out_refs[4 * i:4 * i + 4]
            o[0][...] = g
            o[1][...], o[2][...], o[3][...] = _adam_math(g, w_ref[...], m_ref[...], v_ref[...])

    out_shape = [jax.ShapeDtypeStruct(trip[0].shape, F32) for trip in wmv for _ in range(4)]
    outs = pl.pallas_call(kern, out_shape=out_shape, name=name,
                          compiler_params=pltpu.CompilerParams(vmem_limit_bytes=VMEM_LIMIT))(
        *lora_parts, vec_parts, wide_parts, *flat)
    return [tuple(outs[4 * i:4 * i + 4]) for i in range(n)]


def _to_slots(g, axis):
    _, rws, cols = g.shape
    if axis == 1:
        return g.reshape(N_DEV, 1, rws // N_DEV, cols)
    return g.reshape(1, rws, N_DEV, cols // N_DEV).transpose(2, 0, 1, 3)


def _from_slots(got, axis):
    _, _, rws, cols = got.shape
    if axis == 1:
        return got.reshape(1, N_DEV * rws, cols)
    return got.transpose(1, 2, 0, 3).reshape(1, rws, N_DEV * cols)


def _pad_lanes(a, width):
    return jnp.concatenate([a, jnp.zeros(a.shape[:-1] + (width - a.shape[-1],), a.dtype)], axis=-1)


def kernel(x, norm1_w, w_in, mu_shift, w_up_f, w0_f, w_up_b, w0_b, a_up_f, a0_f, a_up_b, a0_b, g_up, k_k, k_a_f, k_a_b, r_k_f, r_k_b, gn_w, gn_b, conv_w, w_out, norm2_w, w_gate, w_up, w_down, norm_f_w, loss_target, m_norm1_w, m_w_in, m_mu_shift, m_w_up_f, m_w0_f, m_w_up_b, m_w0_b, m_a_up_f, m_a0_f, m_a_up_b, m_a0_b, m_g_up, m_k_k, m_k_a_f, m_k_a_b, m_r_k_f, m_r_k_b, m_gn_w, m_gn_b, m_conv_w, m_w_out, m_norm2_w, m_w_gate, m_w_up, m_w_down, m_norm_f_w, v_norm1_w, v_w_in, v_mu_shift, v_w_up_f, v_w0_f, v_w_up_b, v_w0_b, v_a_up_f, v_a0_f, v_a_up_b, v_a0_b, v_g_up, v_k_k, v_k_a_f, v_k_a_b, v_r_k_f, v_r_k_b, v_gn_w, v_gn_b, v_conv_w, v_w_out, v_norm2_w, v_w_gate, v_w_up, v_w_down, v_norm_f_w):
    local = dict(norm1_w=norm1_w, w_in=w_in, mu_shift=mu_shift, w_up_f=w_up_f, w0_f=w0_f, w_up_b=w_up_b,
                 w0_b=w0_b, a_up_f=a_up_f, a0_f=a0_f, a_up_b=a_up_b, a0_b=a0_b, g_up=g_up, k_k=k_k, k_a_f=k_a_f,
                 k_a_b=k_a_b, r_k_f=r_k_f, r_k_b=r_k_b, gn_w=gn_w, gn_b=gn_b, conv_w=conv_w, w_out=w_out,
                 norm2_w=norm2_w, w_gate=w_gate, w_up=w_up, w_down=w_down, norm_f_w=norm_f_w)
    mom_m = dict(norm1_w=m_norm1_w, w_in=m_w_in, mu_shift=m_mu_shift, w_up_f=m_w_up_f, w0_f=m_w0_f,
                 w_up_b=m_w_up_b, w0_b=m_w0_b, a_up_f=m_a_up_f, a0_f=m_a0_f, a_up_b=m_a_up_b, a0_b=m_a0_b,
                 g_up=m_g_up, k_k=m_k_k, k_a_f=m_k_a_f, k_a_b=m_k_a_b, r_k_f=m_r_k_f, r_k_b=m_r_k_b,
                 gn_w=m_gn_w, gn_b=m_gn_b, conv_w=m_conv_w, w_out=m_w_out, norm2_w=m_norm2_w, w_gate=m_w_gate,
                 w_up=m_w_up, w_down=m_w_down, norm_f_w=m_norm_f_w)
    mom_v = dict(norm1_w=v_norm1_w, w_in=v_w_in, mu_shift=v_mu_shift, w_up_f=v_w_up_f, w0_f=v_w0_f,
                 w_up_b=v_w_up_b, w0_b=v_w0_b, a_up_f=v_a_up_f, a0_f=v_a0_f, a_up_b=v_a_up_b, a0_b=v_a0_b,
                 g_up=v_g_up, k_k=v_k_k, k_a_f=v_k_a_f, k_a_b=v_k_a_b, r_k_f=v_r_k_f, r_k_b=v_r_k_b,
                 gn_w=v_gn_w, gn_b=v_gn_b, conv_w=v_conv_w, w_out=v_w_out, norm2_w=v_norm2_w, w_gate=v_w_gate,
                 w_up=v_w_up, w_down=v_w_down, norm_f_w=v_norm_f_w)

    early = ("w_in",) + LORA
    got = _all_gather([local["w_in"].astype(BF16)] + [local[n] for n in LORA], name="gather")
    full = dict(local)
    full.update({n: _from_slots(a, SHARD_AXIS[n]) for n, a in zip(early, got)})

    loss_part, grad_x, grads, parts = _local_step(x, loss_target, full,
                                                  late={n: local[n].astype(BF16) for n in LATE})

    wide_rows = jnp.concatenate([_pad_lanes(a, WIDE_ROW) for a in [grads[n] for n in WIDE] + [loss_part]]
                                + [jnp.zeros((SUBLANES - len(WIDE) - 1, WIDE_ROW), F32)], axis=0)
    wide_parts, = _exchange([], [wide_rows], name="grad_exchange")
    loss = jnp.sum(wide_parts[:, len(WIDE), 0])
    out = {}
    for n in BIG:
        out[n] = _adamw_big(parts[n], local[n], mom_m[n], mom_v[n], name="adamw_" + n)

    def small_form(n, a):
        if n in LORA:
            return a
        a = a.reshape(1, -1)
        return _pad_lanes(a, WIDE_ROW) if n == "mu_shift" else a

    small = LORA + VEC + WIDE
    res = _adamw_small([parts[n] for n in LORA], parts["vec"], wide_parts,
                       [tuple(small_form(n, d[n]) for d in (local, mom_m, mom_v)) for n in small],
                       name="adamw_small")
    for n, quad in zip(small, res):
        out[n] = tuple(a[..., :local[n].size].reshape(local[n].shape) if n not in LORA else a for a in quad)
    return (loss, grad_x, *[out[n][i] for i in range(4) for n in WEIGHTS])
```

```python
import functools

import jax
import jax.numpy as jnp
from jax import lax
from jax.experimental import pallas as pl
from jax.experimental.pallas import tpu as pltpu

F32 = jnp.float32
BF16 = jnp.bfloat16
HIGHEST = lax.Precision.HIGHEST

N_DEV = 8
D_MODEL = 1024
D_RWKV = 512
D_CONV = 512
HEAD = 64
N_HEAD = D_RWKV // HEAD
D_LORA = 64
D_GATE = 160
D_SHIFTED = 3 * D_RWKV + 2 * D_LORA + D_GATE
XW0, XA0, XG0 = 1536, 1664, 1792
D_SP = 2048
D_INP = D_SP + 3 * D_CONV
LOG_DECAY_SCALE = 0.606531
RMS_EPS = 1e-6
GN_EPS = 64e-5
NORM_EPS = 1e-12
ADAM_LR, ADAM_B1, ADAM_B2, ADAM_EPS, ADAM_WD, ADAM_STEP = 0.001, 0.9, 0.999, 1e-08, 0.01, 10

LANES = 128
SUBLANES = 8
VMEM_LIMIT = 48 * 1024 * 1024
SCAN_CHUNK = 32
SCAN_VMEM_LIMIT = 58 * 1024 * 1024
SCAN_UNROLL = 3
ROW_TILE = 128
WIDE_TILE = 256

BIG = ("w_in", "w_out", "w_gate", "w_up", "w_down")
LORA = ("w_up_f", "w_up_b", "a_up_f", "a_up_b", "g_up", "conv_w")
SHARD_AXIS = {"w_in": 2, "w_out": 1, "w_gate": 2, "w_up": 2, "w_down": 1, "w_up_f": 2, "w_up_b": 2,
              "a_up_f": 2, "a_up_b": 2, "g_up": 2, "conv_w": 2}
VEC = ("w0_f", "w0_b", "a0_f", "a0_b", "k_k", "k_a_f", "k_a_b", "r_k_f", "r_k_b", "gn_w", "gn_b")
WIDE = ("mu_shift", "norm1_w", "norm2_w", "norm_f_w")
WIDE_ROW = 2048
WEIGHTS = ("norm1_w", "w_in", "mu_shift", "w_up_f", "w0_f", "w_up_b", "w0_b", "a_up_f", "a0_f", "a_up_b",
           "a0_b", "g_up", "k_k", "k_a_f", "k_a_b", "r_k_f", "r_k_b", "gn_w", "gn_b", "conv_w", "w_out",
           "norm2_w", "w_gate", "w_up", "w_down", "norm_f_w")


def _params(sem, limit=VMEM_LIMIT):
    return pltpu.CompilerParams(dimension_semantics=sem, vmem_limit_bytes=limit)


def _tile(n, cands):
    for c in cands:
        if n % c == 0:
            return c
    raise ValueError(f"no tile for {n}")


def _mm(a, b, *, ta=False, tb=False, add=None, exchange=None, name):
    (k_dim, m) = a.shape if ta else a.shape[::-1]
    (k2, n) = b.shape[::-1] if tb else b.shape
    assert k_dim == k2, (a.shape, b.shape, ta, tb)
    tm = _tile(m, (1408, 1024, 512, 256, 128))
    tn = _tile(n, (1408, 1024, 896, 512, 256, 128))
    tk = _tile(k_dim, (1408, 1024, 896, 512, 256, 128))
    nk = k_dim // tk
    grid = (m // tm, n // tn, nk)
    dims = (((0 if ta else 1,), (1 if tb else 0,)), ((), ()))
    sliced, whole = exchange or ((), ())
    riders = list(sliced) + list(whole)
    n_x, n_in = len(riders), 2 + (add is not None)

    def kern(*refs):
        a_ref, b_ref = refs[:2]
        add_ref = refs[2] if add is not None else None
        o_ref, acc_ref = refs[n_in + n_x], refs[n_in + 2 * n_x + 1]
        k = pl.program_id(2)
        step = (pl.program_id(0) * grid[1] + pl.program_id(1)) * nk + k

        def copies():
            return _exchange_copies(refs[n_in:n_in + n_x], refs[n_in + n_x + 1:n_in + 2 * n_x + 1], len(sliced),
                                    *refs[n_in + 2 * n_x + 2:])

        if n_x:
            @pl.when(step == 0)
            def _():
                for cp in copies():
                    cp.start()

        @pl.when(k == 0)
        def _():
            acc_ref[...] = jnp.zeros_like(acc_ref)

        acc_ref[...] += lax.dot_general(a_ref[...].astype(BF16), b_ref[...].astype(BF16), dims,
                                        preferred_element_type=F32)

        @pl.when(k == nk - 1)
        def _():
            if add is None:
                o_ref[...] = acc_ref[...]
            else:
                o_ref[...] = acc_ref[...] + add_ref[...]

        if n_x:
            @pl.when(step == grid[0] * grid[1] * nk - 1)
            def _():
                for cp in copies():
                    cp.wait()

    a_spec = (pl.BlockSpec((tk, tm), lambda i, j, k: (k, i)) if ta
              else pl.BlockSpec((tm, tk), lambda i, j, k: (i, k)))
    b_spec = (pl.BlockSpec((tn, tk), lambda i, j, k: (j, k)) if tb
              else pl.BlockSpec((tk, tn), lambda i, j, k: (k, j)))
    o_spec = pl.BlockSpec((tm, tn), lambda i, j, k: (i, j))
    in_specs = [a_spec, b_spec] + ([o_spec] if add is not None else []) + _hbm_specs(n_x)
    args = (a, b) + ((add,) if add is not None else ()) + tuple(riders)
    out = pl.pallas_call(
        kern, out_shape=[jax.ShapeDtypeStruct((m, n), F32)] + _exchange_out_shapes(riders, len(sliced)), grid=grid,
        in_specs=in_specs, out_specs=[o_spec] + _hbm_specs(n_x),
        scratch_shapes=[pltpu.VMEM((tm, tn), F32)] + (_exchange_sems(n_x) if n_x else []),
        compiler_params=_params(("arbitrary",) * 3 if n_x else ("parallel", "parallel", "arbitrary")),
        name=name)(*args)
    return out if n_x else out[0]


def _swiglu(g, u):
    return jax.nn.silu(g) * u


FFN_TN = 256


def _mm_swiglu(h, w_gate, w_up, *, name):
    m, k_dim = h.shape
    n = w_gate.shape[1]
    tm = _tile(m, (1024, 512, 256, 128))

    def kern(h_ref, wg_ref, wu_ref, g_ref, u_ref, f_ref):
        hv = h_ref[...].astype(BF16)
        g = jnp.dot(hv, wg_ref[...].astype(BF16), preferred_element_type=F32)
        u = jnp.dot(hv, wu_ref[...].astype(BF16), preferred_element_type=F32)
        g_ref[...] = g
        u_ref[...] = u
        f_ref[...] = _swiglu(g, u).astype(f_ref.dtype)

    w_spec = pl.BlockSpec((k_dim, FFN_TN), lambda i, j: (0, j))
    o_spec = pl.BlockSpec((tm, FFN_TN), lambda i, j: (i, j))
    return pl.pallas_call(
        kern, out_shape=[jax.ShapeDtypeStruct((m, n), F32)] * 2 + [jax.ShapeDtypeStruct((m, n), BF16)],
        grid=(m // tm, n // FFN_TN), in_specs=[pl.BlockSpec((tm, k_dim), lambda i, j: (i, 0)), w_spec, w_spec],
        out_specs=[o_spec] * 3, compiler_params=_params(("parallel", "parallel")), name=name)(h, w_gate, w_up)


def _mm_swiglu_bwd(dx, w_down, g, u, *, name):
    m, k_dim = dx.shape
    n = w_down.shape[0]
    tm = _tile(m, (1024, 512, 256, 128))

    def kern(dx_ref, w_ref, g_ref, u_ref, dg_ref, du_ref):
        df = lax.dot_general(dx_ref[...].astype(BF16), w_ref[...].astype(BF16), (((1,), (1,)), ((), ())),
                             preferred_element_type=F32)
        _, vjp = jax.vjp(_swiglu, g_ref[...], u_ref[...])
        dg, du = vjp(df)
        dg_ref[...] = dg.astype(dg_ref.dtype)
        du_ref[...] = du.astype(du_ref.dtype)

    o_spec = pl.BlockSpec((tm, FFN_TN), lambda i, j: (i, j))
    return pl.pallas_call(
        kern, out_shape=[jax.ShapeDtypeStruct((m, n), BF16)] * 2, grid=(m // tm, n // FFN_TN),
        in_specs=[pl.BlockSpec((tm, k_dim), lambda i, j: (i, 0)), pl.BlockSpec((FFN_TN, k_dim), lambda i, j: (j, 0)),
                  o_spec, o_spec],
        out_specs=[o_spec] * 2, compiler_params=_params(("parallel", "parallel")), name=name)(dx, w_down, g, u)


def _rowwise(fn, rows, consts, out_rows, out_accs, *, name, tb=ROW_TILE, out_dtype=F32):
    t = (rows[0][0] if isinstance(rows[0], tuple) else rows[0]).shape[0]
    n_r, n_c, n_o, n_a = len(rows), len(consts), len(out_rows), len(out_accs)
    pieces = [w if isinstance(w, (list, tuple)) else [w] for w in out_rows]

    def kern(*refs):
        r_refs = refs[:n_r]
        c_refs = refs[n_r:n_r + n_c]
        o_refs = refs[n_r + n_c:n_r + n_c + n_o]
        a_refs = refs[n_r + n_c + n_o:]
        vals = fn(*[r[...] for r in r_refs], *[c[...] for c in c_refs])
        vals = list(vals) if isinstance(vals, (tuple, list)) else [vals]
        pos = 0
        for o_ref, ws in zip(o_refs, pieces):
            off = 0
            for w in ws:
                o_ref[:, off:off + w] = vals[pos].astype(o_ref.dtype)
                off += w
                pos += 1
        if n_a:
            @pl.when(pl.program_id(0) == 0)
            def _():
                for a_ref in a_refs:
                    a_ref[...] = jnp.zeros_like(a_ref)
            for a_ref, v in zip(a_refs, vals[pos:]):
                a_ref[...] += v

    in_specs, args = [], []
    for r in rows:
        if isinstance(r, tuple):
            arr, blk, w = r
            in_specs.append(pl.BlockSpec((tb, w), functools.partial(lambda i, blk: (i, blk), blk=blk)))
        else:
            arr = r
            in_specs.append(pl.BlockSpec((tb, arr.shape[1]), lambda i: (i, 0)))
        args.append(arr)
    for c in consts:
        in_specs.append(pl.BlockSpec(c.shape, lambda i: (0, 0)))
        args.append(c)
    out_shape = [jax.ShapeDtypeStruct((t, sum(ws)), out_dtype) for ws in pieces]
    out_specs = [pl.BlockSpec((tb, sum(ws)), lambda i: (i, 0)) for ws in pieces]
    for shp in out_accs:
        out_shape.append(jax.ShapeDtypeStruct(shp, F32))
        out_specs.append(pl.BlockSpec(shp, lambda i: (0, 0)))
    res = pl.pallas_call(
        kern, out_shape=out_shape, grid=(t // tb,), in_specs=in_specs, out_specs=out_specs,
        compiler_params=_params(("arbitrary",) if n_a else ("parallel",)), name=name)(*args)
    return res


def _rms(x, w):
    return x * lax.rsqrt(jnp.mean(x * x, axis=-1, keepdims=True) + RMS_EPS) * w


def _seg_sum(x, bd):
    return jnp.concatenate(
        [jnp.dot(x[:, LANES * j:LANES * (j + 1)], bd, precision=HIGHEST, preferred_element_type=F32)
         for j in range(x.shape[1] // LANES)], axis=1)


@jax.custom_vjp
def _seg(x, bd):
    return _seg_sum(x, bd)


_seg.defvjp(lambda x, bd: (_seg_sum(x, bd), bd), lambda bd, ct: (_seg_sum(ct, bd), jnp.zeros_like(bd)))


def _colsum(x):
    return jnp.sum(x, axis=0, keepdims=True)


def _prescan_math(r, k, xw, xa, xg, k_k, w0f, w0b, a0f, a0b, kaf, kab, wupf, wupb, aupf, aupb, gup, bd):
    kkr = k * k_k
    norm = jnp.sqrt(_seg(kkr * kkr, bd))
    kk = kkr / jnp.maximum(norm, NORM_EPS)
    th = jnp.tanh(xw)

    def direction(w0, wup, a0, aup, ka):
        logit = w0 + jnp.dot(th, wup, preferred_element_type=F32)
        w = jnp.exp(-LOG_DECAY_SCALE * jax.nn.sigmoid(logit))
        a = jax.nn.sigmoid(a0 + jnp.dot(xa, aup, preferred_element_type=F32))
        kd = k * (1.0 + (a - 1.0) * ka)
        return w, kd, kk * a

    wf, kdf, bf = direction(w0f, wupf, a0f, aupf, kaf)
    wb, kdb, bb = direction(w0b, wupb, a0b, aupb, kab)
    g = jnp.dot(jax.nn.sigmoid(xg), gup, preferred_element_type=F32)
    return kk, r, wf, wb, bf, bb, kdf, kdb, g


def _postscan_math(y, r, v, kdf, kdb, g, gn_w, gn_b, rkf, rkb, bd):
    mean = _seg(y, bd) * (1.0 / HEAD)
    yc = y - mean
    var = _seg(yc * yc, bd) * (1.0 / HEAD)
    yg = yc * lax.rsqrt(var + GN_EPS) * gn_w + gn_b
    bonus = (_seg(r * kdf * rkf, bd) + _seg(r * kdb * rkb, bd)) * v
    return (yg + bonus) * g


def _halo_specs(width, col_blk, tb, t):
    nb = t // SUBLANES
    step = tb // SUBLANES
    main = pl.BlockSpec((tb, width), lambda i: (i, col_blk))
    prev = pl.BlockSpec((SUBLANES, width), lambda i: (jnp.maximum(i * step - 1, 0), col_blk))
    nxt = pl.BlockSpec((SUBLANES, width), lambda i: (jnp.minimum((i + 1) * step, nb - 1), col_blk))
    return [main, prev, nxt]


def _neighbours(z, prev8, next8, first, last):
    tb = z.shape[0]
    row = lax.broadcasted_iota(jnp.int32, z.shape, 0)
    prow = jnp.where(first, 0.0, prev8[SUBLANES - 1:SUBLANES, :])
    nrow = jnp.where(last, 0.0, next8[0:1, :])
    down = jnp.where(row == 0, prow, pltpu.roll(z, 1, 0))
    up = jnp.where(row == tb - 1, nrow, pltpu.roll(z, tb - 1, 0))
    return down, up


def _shift_conv_fwd(p, mu, conv_w, seq, *, name, tb=ROW_TILE):
    t = p.shape[0]
    per_seq = seq // tb

    def kern(p_ref, pp_ref, pn_ref, mu_ref, cw_ref, pss_ref, oc_ref):
        i = pl.program_id(0)
        first = (i % per_seq) == 0
        last = (i % per_seq) == per_seq - 1
        ps = p_ref[:, :D_SP]
        down, up = _neighbours(ps, pp_ref[:, :D_SP], pn_ref[:, :D_SP], first, last)
        pss_ref[...] = ps + mu_ref[...] * (0.5 * (down + up) - ps)
        gb = p_ref[:, D_SP:D_SP + D_CONV]
        u = p_ref[:, D_SP + D_CONV:D_SP + 2 * D_CONV] * p_ref[:, D_SP + 2 * D_CONV:]
        u_p = pp_ref[:, D_SP + D_CONV:D_SP + 2 * D_CONV] * pp_ref[:, D_SP + 2 * D_CONV:]
        u_n = pn_ref[:, D_SP + D_CONV:D_SP + 2 * D_CONV] * pn_ref[:, D_SP + 2 * D_CONV:]
        udown, uup = _neighbours(u, u_p, u_n, first, last)
        oc_ref[...] = gb * (cw_ref[0:1, :] * udown + cw_ref[1:2, :] * u + cw_ref[2:3, :] * uup)

    return pl.pallas_call(
        kern,
        out_shape=[jax.ShapeDtypeStruct((t, D_SP), F32), jax.ShapeDtypeStruct((t, D_CONV), F32)],
        grid=(t // tb,),
        in_specs=_halo_specs(D_INP, 0, tb, t) + [pl.BlockSpec((1, D_SP), lambda i: (0, 0)),
                                                 pl.BlockSpec((SUBLANES, D_CONV), lambda i: (0, 0))],
        out_specs=[pl.BlockSpec((tb, D_SP), lambda i: (i, 0)), pl.BlockSpec((tb, D_CONV), lambda i: (i, 0))],
        compiler_params=_params(("parallel",)), name=name)(p, p, p, mu, conv_w)


def _shift_conv_bwd(p, d_pss, d_o, mu, conv_w, seq, *, name, tb=ROW_TILE):
    t = p.shape[0]
    per_seq = seq // tb

    def kern(p_ref, pp_ref, pn_ref, d_ref, dp_ref, dn_ref, do_ref, dop_ref, don_ref, mu_ref, cw_ref,
             out_ref, dmu_ref, dcw_ref):
        i = pl.program_id(0)
        first = (i % per_seq) == 0
        last = (i % per_seq) == per_seq - 1

        @pl.when(i == 0)
        def _():
            dmu_ref[...] = jnp.zeros_like(dmu_ref)
            dcw_ref[...] = jnp.zeros_like(dcw_ref)

        mu_v = mu_ref[...]
        ps = p_ref[:, :D_SP]
        down, up = _neighbours(ps, pp_ref[:, :D_SP], pn_ref[:, :D_SP], first, last)
        d = d_ref[...]
        ddown, dup = _neighbours(d, dp_ref[...], dn_ref[...], first, last)
        out_ref[:, :D_SP] = (d - mu_v * d + 0.5 * (mu_v * ddown + mu_v * dup)).astype(out_ref.dtype)
        dmu_ref[...] += _colsum(d * (0.5 * (down + up) - ps))

        def parts(ref):
            return (ref[:, D_SP:D_SP + D_CONV], ref[:, D_SP + D_CONV:D_SP + 2 * D_CONV],
                    ref[:, D_SP + 2 * D_CONV:])

        gb, gc, hh = parts(p_ref)
        gb_p, gc_p, hh_p = parts(pp_ref)
        gb_n, gc_n, hh_n = parts(pn_ref)
        u = gc * hh
        udown, uup = _neighbours(u, gc_p * hh_p, gc_n * hh_n, first, last)
        cw0, cw1, cw2 = cw_ref[0:1, :], cw_ref[1:2, :], cw_ref[2:3, :]
        do = do_ref[...]
        duc = do * gb
        ducdown, ducup = _neighbours(duc, dop_ref[...] * gb_p, don_ref[...] * gb_n, first, last)
        du = cw0 * ducup + cw1 * duc + cw2 * ducdown
        out_ref[:, D_SP:D_SP + D_CONV] = (do * (cw0 * udown + cw1 * u + cw2 * uup)).astype(out_ref.dtype)
        out_ref[:, D_SP + D_CONV:D_SP + 2 * D_CONV] = (du * hh).astype(out_ref.dtype)
        out_ref[:, D_SP + 2 * D_CONV:] = (du * gc).astype(out_ref.dtype)
        dcw_ref[0:1, :] += _colsum(duc * udown)
        dcw_ref[1:2, :] += _colsum(duc * u)
        dcw_ref[2:3, :] += _colsum(duc * uup)

    return pl.pallas_call(
        kern,
        out_shape=[jax.ShapeDtypeStruct((t, D_INP), BF16), jax.ShapeDtypeStruct((1, D_SP), F32),
                   jax.ShapeDtypeStruct((SUBLANES, D_CONV), F32)],
        grid=(t // tb,),
        in_specs=(_halo_specs(D_INP, 0, tb, t) + _halo_specs(D_SP, 0, tb, t) + _halo_specs(D_CONV, 1, tb, t)
                  + [pl.BlockSpec((1, D_SP), lambda i: (0, 0)),
                     pl.BlockSpec((SUBLANES, D_CONV), lambda i: (0, 0))]),
        out_specs=[pl.BlockSpec((tb, D_INP), lambda i: (i, 0)), pl.BlockSpec((1, D_SP), lambda i: (0, 0)),
                   pl.BlockSpec((SUBLANES, D_CONV), lambda i: (0, 0))],
        compiler_params=_params(("arbitrary",)), name=name)(p, p, p, d_pss, d_pss, d_pss, d_o, d_o, d_o, mu, conv_w)


N_CHAIN = 16
N_GROUP = LANES // N_CHAIN
V_HI = HEAD // SUBLANES
G_KK, G_R, G_W, G_B, G_KD = 0, 1, (2, 3), (4, 5), (6, 7)


K_HI = HEAD // SUBLANES


def _tree_sum(terms):
    terms = list(terms)
    while len(terms) > 1:
        terms = [a + b for a, b in zip(terms[::2], terms[1::2])]
    return terms[0]


def _kscan_specs(nc):
    same = lambda c: c
    mirror = lambda c: nc - 1 - c

    def k_spec(fn):
        return pl.BlockSpec((SCAN_CHUNK, HEAD, LANES), lambda c: (fn(c), 0, 0))

    def v_spec(fn):
        return pl.BlockSpec((SCAN_CHUNK, SUBLANES, LANES), lambda c: (fn(c), 0, 0))

    return same, mirror, k_spec, v_spec


ST_SHAPE = (2, K_HI, V_HI, SUBLANES, LANES)


def _lane_group_index():
    lane = lax.broadcasted_iota(jnp.int32, (SUBLANES, LANES), 1)
    return lax.shift_right_logical(lane, jnp.full_like(lane, 4))


def _spread_groups(x, grp):
    rolled = [x] + [pltpu.roll(x, s * N_CHAIN, 1) for s in range(1, N_GROUP)]
    out = []
    for j in range(N_GROUP):
        t = rolled[(0 - j) % N_GROUP]
        for g in range(1, N_GROUP):
            t = jnp.where(grp == g, rolled[(g - j) % N_GROUP], t)
        out.append(t)
    return out


def _gather_groups(tiles, grp):
    total = None
    for s in range(N_GROUP):
        b = tiles[s % N_GROUP]
        for g in range(1, N_GROUP):
            b = jnp.where(grp == g, tiles[(g + s) % N_GROUP], b)
        b = pltpu.roll(b, s * N_CHAIN, 1) if s else b
        total = b if total is None else total + b
    return total


def _lane_group_sum(x):
    return _tree_sum([x] + [pltpu.roll(x, k * N_CHAIN, 1) for k in range(1, N_GROUP)])


def _key_row(x_t, grp, kh):
    r = SUBLANES * grp + kh
    return jnp.broadcast_to(x_t[r:r + 1, :], (SUBLANES, LANES))


def _acc(total, term):
    return term if total is None else total + term


SA_SHAPE = (2, V_HI, SUBLANES, LANES)


def _scan_fwd(xall, v_c, *, gather=(), name):
    steps = xall.shape[0]
    nc = steps // SCAN_CHUNK
    same, mirror, k_spec, v_spec = _kscan_specs(nc)
    last = SCAN_CHUNK - 1
    n_x = len(gather)

    def kern(*refs):
        xf_ref, xb_ref, vf_ref, vb_ref = refs[:4]
        yf_ref, yb_ref, hist_ref, fin_ref, sa_ref = refs[4 + n_x:9 + n_x]
        st_ref = refs[9 + 2 * n_x]
        c = pl.program_id(0)

        def riders():
            return _exchange_copies(refs[4:4 + n_x], refs[9 + n_x:9 + 2 * n_x], 0, *refs[10 + 2 * n_x:])

        @pl.when(c == 0)
        def _():
            st_ref[...] = jnp.zeros_like(st_ref)
            if n_x:
                for cp in riders():
                    cp.start()

        hist_ref[0] = st_ref[...]
        grp = _lane_group_index()

        def body(i, put):
            j = last - i
            for d, (x_t, v_t, y_ref, at) in enumerate(((xf_ref[i], vf_ref[i], yf_ref, i),
                                                       (xb_ref[j], vb_ref[j], yb_ref, j))):
                v_b = _spread_groups(v_t, grp)
                part = [None] * V_HI
                for kh in range(K_HI):
                    kk_r = _key_row(x_t, G_KK, kh)
                    for vh in range(V_HI):
                        part[vh] = _acc(part[vh], hist_ref[i, d, kh, vh] * kk_r)
                sa = [_lane_group_sum(p) for p in part]
                for vh in range(V_HI):
                    sa_ref[i, d, vh] = sa[vh]
                y_p = [None] * V_HI
                for kh in range(K_HI):
                    r_r, w_r = _key_row(x_t, G_R, kh), _key_row(x_t, G_W[d], kh)
                    b_r, kd_r = _key_row(x_t, G_B[d], kh), _key_row(x_t, G_KD[d], kh)
                    for vh in range(V_HI):
                        new = hist_ref[i, d, kh, vh] * w_r - sa[vh] * b_r + v_b[vh] * kd_r
                        put(d, kh, vh, new)
                        y_p[vh] = _acc(y_p[vh], new * r_r)
                y_ref[at] = _gather_groups(y_p, grp)

        def step(i, carry):
            def put(d, kh, vh, val):
                hist_ref[i + 1, d, kh, vh] = val
            body(i, put)
            return carry

        lax.fori_loop(0, last, step, 0, unroll=SCAN_UNROLL)

        def put_carry(d, kh, vh, val):
            st_ref[d, kh, vh] = val

        body(last, put_carry)

        @pl.when(c == nc - 1)
        def _():
            fin_ref[...] = st_ref[...]
            if n_x:
                for cp in riders():
                    cp.wait()

    return pl.pallas_call(
        kern,
        out_shape=[jax.ShapeDtypeStruct((steps, SUBLANES, LANES), F32)] * 2
        + [jax.ShapeDtypeStruct((steps,) + ST_SHAPE, F32), jax.ShapeDtypeStruct(ST_SHAPE, F32),
           jax.ShapeDtypeStruct((steps,) + SA_SHAPE, F32)]
        + _exchange_out_shapes(gather, 0),
        grid=(nc,), in_specs=[k_spec(same), k_spec(mirror), v_spec(same), v_spec(mirror)] + _hbm_specs(n_x),
        out_specs=[v_spec(same), v_spec(mirror),
                   pl.BlockSpec((SCAN_CHUNK,) + ST_SHAPE, lambda c: (c, 0, 0, 0, 0, 0)),
                   pl.BlockSpec(ST_SHAPE, lambda c: (0, 0, 0, 0, 0)),
                   pl.BlockSpec((SCAN_CHUNK,) + SA_SHAPE, lambda c: (c, 0, 0, 0, 0))] + _hbm_specs(n_x),
        scratch_shapes=[pltpu.VMEM(ST_SHAPE, F32)] + (_exchange_sems(n_x) if n_x else []),
        compiler_params=_params(("arbitrary",), SCAN_VMEM_LIMIT), name=name)(xall, xall, v_c, v_c, *gather)


def _scan_bwd(xall, v_c, dy_c, hist, fin, sa, *, exchange=(), name):
    steps = xall.shape[0]
    nc = steps // SCAN_CHUNK
    same, back, k_spec, v_spec = _kscan_specs(nc)
    last = SCAN_CHUNK - 1
    n_x = len(exchange)

    def kern(*refs):
        xf_ref, xb_ref, vf_ref, vb_ref, dyf_ref, dyb_ref, hist_ref, fin_ref, sa_ref = refs[:9]
        gf_ref, gb_ref, dvf_ref, dvb_ref = refs[9 + n_x:13 + n_x]
        ds_ref, after_ref = refs[13 + 2 * n_x:15 + 2 * n_x]
        c = pl.program_id(0)

        def riders():
            return _exchange_copies(refs[9:9 + n_x], refs[13 + n_x:13 + 2 * n_x], n_x, *refs[15 + 2 * n_x:])

        @pl.when(c == 0)
        def _():
            ds_ref[...] = jnp.zeros_like(ds_ref)
            after_ref[...] = fin_ref[...]
            if n_x:
                for cp in riders():
                    cp.start()

        grp = _lane_group_index()
        row = lax.broadcasted_iota(jnp.int32, (SUBLANES, LANES), 0)
        zero = jnp.zeros((SUBLANES, LANES), F32)

        def body(i, after):
            j = last - i
            for d, (x_t, v_t, dy_t, g_ref, dv_ref, at) in enumerate((
                    (xf_ref[i], vf_ref[i], dyf_ref[i], gf_ref, dvf_ref, i),
                    (xb_ref[j], vb_ref[j], dyb_ref[j], gb_ref, dvb_ref, j))):
                v_s, dy_s = _spread_groups(v_t, grp), _spread_groups(dy_t, grp)
                dsa_p, dv_p = [None] * V_HI, [None] * V_HI
                for kh in range(K_HI):
                    r_r = _key_row(x_t, G_R, kh)
                    b_r, kd_r = _key_row(x_t, G_B[d], kh), _key_row(x_t, G_KD[d], kh)
                    for vh in range(V_HI):
                        g = ds_ref[d, kh, vh] + dy_s[vh] * r_r
                        ds_ref[d, kh, vh] = g
                        dsa_p[vh] = _acc(dsa_p[vh], g * b_r)
                        dv_p[vh] = _acc(dv_p[vh], g * kd_r)
                dsa = [-_lane_group_sum(p) for p in dsa_p]
                sa = [sa_ref[i, d, vh] for vh in range(V_HI)]
                dv_ref[at] = _gather_groups(dv_p, grp)
                blocks = {G_KK: zero, G_R: zero, G_W[d]: zero, G_B[d]: zero, G_KD[d]: zero}
                for kh in range(K_HI):
                    w_r, kk_r = _key_row(x_t, G_W[d], kh), _key_row(x_t, G_KK, kh)
                    dkk = dr = dw = db = dkd = None
                    for vh in range(V_HI):
                        g, before = ds_ref[d, kh, vh], hist_ref[i, d, kh, vh]
                        dr = _acc(dr, after(d, kh, vh) * dy_s[vh])
                        dw = _acc(dw, g * before)
                        dkd = _acc(dkd, g * v_s[vh])
                        db = _acc(db, g * sa[vh])
                        dkk = _acc(dkk, before * dsa[vh])
                        ds_ref[d, kh, vh] = g * w_r + dsa[vh] * kk_r
                    for gi, a in ((G_KK, dkk), (G_R, dr), (G_W[d], dw), (G_B[d], -db), (G_KD[d], dkd)):
                        blocks[gi] = jnp.where(row == kh, _colsum(a), blocks[gi])
                for gi in range(N_GROUP):
                    g_ref[at, SUBLANES * gi:SUBLANES * (gi + 1), :] = blocks.get(gi, zero)

        body(last, lambda d, kh, vh: after_ref[d, kh, vh])

        def step(ii, carry):
            i = last - ii
            body(i, lambda d, kh, vh: hist_ref[i + 1, d, kh, vh])
            return carry

        lax.fori_loop(1, SCAN_CHUNK, step, 0, unroll=SCAN_UNROLL)
        after_ref[...] = hist_ref[0]

        if n_x:
            @pl.when(c == nc - 1)
            def _():
                for cp in riders():
                    cp.wait()

    return pl.pallas_call(
        kern,
        out_shape=[jax.ShapeDtypeStruct((steps, HEAD, LANES), F32)] * 2
        + [jax.ShapeDtypeStruct((steps, SUBLANES, LANES), F32)] * 2 + _exchange_out_shapes(exchange, n_x),
        grid=(nc,),
        in_specs=[k_spec(back), k_spec(same), v_spec(back), v_spec(same), v_spec(back), v_spec(same),
                  pl.BlockSpec((SCAN_CHUNK,) + ST_SHAPE, lambda c: (back(c), 0, 0, 0, 0, 0)),
                  pl.BlockSpec(ST_SHAPE, lambda c: (0, 0, 0, 0, 0)),
                  pl.BlockSpec((SCAN_CHUNK,) + SA_SHAPE, lambda c: (back(c), 0, 0, 0, 0))] + _hbm_specs(n_x),
        out_specs=[k_spec(back), k_spec(same), v_spec(back), v_spec(same)] + _hbm_specs(n_x),
        scratch_shapes=[pltpu.VMEM(ST_SHAPE, F32), pltpu.VMEM(ST_SHAPE, F32)]
        + (_exchange_sems(n_x) if n_x else []),
        compiler_params=_params(("arbitrary",), SCAN_VMEM_LIMIT), name=name)(xall, xall, v_c, v_c, dy_c, dy_c, hist, fin, sa,
                                                            *exchange)


def _to_key_rows(wide, bsz, seq):
    z = wide.reshape(bsz, seq, N_GROUP, N_HEAD, K_HI, SUBLANES).transpose(1, 2, 4, 5, 0, 3)
    return z.reshape(seq, HEAD, LANES)


def _from_key_rows(g, bsz, seq):
    z = g.reshape(seq, N_GROUP, K_HI, SUBLANES, bsz, N_HEAD).transpose(4, 0, 1, 5, 2, 3)
    return z.reshape(bsz * seq, N_GROUP * D_RWKV)


def _to_value_rows(a, bsz, seq):
    z = a.reshape(bsz, seq, N_HEAD, V_HI, SUBLANES).transpose(1, 4, 3, 0, 2)
    return z.reshape(seq, SUBLANES, LANES)


def _from_value_rows(y, bsz, seq):
    z = y.reshape(seq, SUBLANES, V_HI, bsz, N_HEAD).transpose(3, 0, 4, 2, 1)
    return z.reshape(bsz * seq, D_RWKV)


def _pad_cols(a, segs):
    out, off = [], 0
    for w, wp in segs:
        out.append(a[..., off:off + w])
        if wp > w:
            out.append(jnp.zeros(a.shape[:-1] + (wp - w,), a.dtype))
        off += w
    return jnp.concatenate(out, axis=-1)


def _unpad_cols(a, segs):
    out, off = [], 0
    for w, wp in segs:
        out.append(a[..., off:off + w])
        off += wp
    return jnp.concatenate(out, axis=-1)


P_SEGS = ((3 * D_RWKV, 3 * D_RWKV), (D_LORA, 128), (D_LORA, 128), (D_GATE, 256), (3 * D_CONV, 3 * D_CONV))
S_SEGS = P_SEGS[:4]


def _pad_rows(a, rows):
    return jnp.concatenate([a, jnp.zeros((rows - a.shape[0], a.shape[1]), a.dtype)], axis=0)


LATE = ("w_out", "w_gate", "w_up", "w_down")


def _local_step(x, target, w, late=None):
    bsz, seq, _ = x.shape
    t = bsz * seq
    x2d = x.reshape(t, D_MODEL)
    tg2d = target.reshape(t, D_MODEL)
    row = lambda a: a.reshape(1, -1).astype(F32)

    w_in = _pad_cols(w["w_in"][0], P_SEGS)
    mu = _pad_cols(row(w["mu_shift"]), S_SEGS)
    wupf, wupb, aupf, aupb = (_pad_rows(w[n][0].astype(F32), 128) for n in ("w_up_f", "w_up_b", "a_up_f", "a_up_b"))
    gup = _pad_rows(w["g_up"][0].astype(F32), 256)
    conv_w = _pad_rows(w["conv_w"][0].astype(F32), SUBLANES)
    norm1, norm2, normf = row(w["norm1_w"]), row(w["norm2_w"]), row(w["norm_f_w"])
    vec = {n: row(w[n]) for n in VEC}
    head_of = jnp.arange(LANES) // HEAD
    bd = (head_of[:, None] == head_of[None, :]).astype(F32)
    pre_consts = [vec["k_k"], vec["w0_f"], vec["w0_b"], vec["a0_f"], vec["a0_b"], vec["k_a_f"], vec["k_a_b"],
                  wupf, wupb, aupf, aupb, gup, bd]
    post_consts = [vec["gn_w"], vec["gn_b"], vec["r_k_f"], vec["r_k_b"], bd]

    h1, = _rowwise(_rms, [x2d], [norm1], [D_MODEL], [], name="rms1_fwd", out_dtype=BF16, tb=WIDE_TILE)
    p = _mm(h1, w_in, name="mm_in")
    pss, oconv = _shift_conv_fwd(p, mu, conv_w, seq, name="shift_conv_fwd")
    pre_rows = [(pss, 0, 512), (pss, 1, 512), (pss, XW0 // 128, 128), (pss, XA0 // 128, 128), (pss, XG0 // 256, 256)]
    sc, g = _rowwise(_prescan_math, pre_rows, pre_consts, [[D_RWKV] * N_GROUP, D_RWKV], [], name="prescan_fwd")
    xall = _to_key_rows(sc, bsz, seq)
    v_l = _to_value_rows(pss[:, 2 * D_RWKV:3 * D_RWKV], bsz, seq)
    y_f, y_b, hist, fin, sa, *gathered = _scan_fwd(xall, v_l, gather=[late[n] for n in LATE] if late else (),
                                                   name="scan_fwd")
    w_out, w_gate, w_up, w_down = (
        (_from_slots(a, SHARD_AXIS[n]) if late else w[n])[0] for n, a in zip(LATE, gathered or LATE))
    y = _from_value_rows(y_f + y_b, bsz, seq)
    post_rows = [y, (pss, 0, 512), (pss, 2, 512), (sc, G_KD[0], 512), (sc, G_KD[1], 512), g]

    def post_fwd(y_, r_, v_, kdf_, kdb_, g_, oc_, *consts):
        return _postscan_math(y_, r_, v_, kdf_, kdb_, g_, *consts), oc_

    o, = _rowwise(post_fwd, post_rows + [oconv], post_consts, [[D_RWKV, D_CONV]], [], name="postscan_fwd",
                  out_dtype=BF16)
    x1 = _mm(o, w_out, add=x2d, name="mm_out")
    h2, = _rowwise(_rms, [x1], [norm2], [D_MODEL], [], name="rms2_fwd", out_dtype=BF16, tb=WIDE_TILE)
    gg, uu, ff = _mm_swiglu(h2, w_gate, w_up, name="mm_gate_up")
    x2 = _mm(ff, w_down, add=x1, name="mm_down")

    def final(x_, tg_, wn_):
        yo, vjp = jax.vjp(_rms, x_, wn_)
        err = yo - tg_
        dx_, dwn_ = vjp(err * (1.0 / D_MODEL))
        part = jnp.sum(jnp.sum(err * err, axis=1, keepdims=True), axis=0, keepdims=True) * (0.5 / D_MODEL)
        return dx_, part + jnp.zeros((1, LANES), F32), dwn_

    dx2, loss_acc, d_normf = _rowwise(final, [x2, tg2d], [normf], [D_MODEL], [(1, LANES), (1, D_MODEL)],
                                      name="loss_head", tb=WIDE_TILE)
    dgg, duu = _mm_swiglu_bwd(dx2, w_down, gg, uu, name="mm_down_dx")
    g_w_down = _mm(ff, dx2, ta=True, name="mm_down_dw")
    dh2 = _mm(dgg, w_gate, tb=True, name="mm_gate_dx")
    dh2 = _mm(duu, w_up, tb=True, add=dh2, name="mm_up_dx")
    g_w_gate = _mm(h2, dgg, ta=True, name="mm_gate_dw")
    g_w_up = _mm(h2, duu, ta=True, name="mm_up_dw")

    def rms_bwd(x_, dh_, dres_, wn_):
        _, vjp = jax.vjp(_rms, x_, wn_)
        dx_, dwn_ = vjp(dh_)
        return dx_ + dres_, dwn_

    dx1, d_norm2 = _rowwise(rms_bwd, [x1, dh2, dx2], [norm2], [D_MODEL], [(1, D_MODEL)], name="rms2_bwd", tb=WIDE_TILE)
    do = _mm(dx1, w_out, tb=True, name="mm_out_dx")
    g_w_out = _mm(o, dx1, ta=True, name="mm_out_dw")

    def post_bwd(y_, r_, v_, kdf_, kdb_, g_, do_, *consts):
        _, vjp = jax.vjp(lambda *a: _postscan_math(*a, consts[4]), y_, r_, v_, kdf_, kdb_, g_, *consts[:4])
        return vjp(do_)

    (dy, dr_c, dv_c, dkdf_c, dkdb_c, dg, d_gn_w, d_gn_b, d_rkf, d_rkb) = _rowwise(
        post_bwd, post_rows + [(do, 0, 512)], post_consts, [D_RWKV] * 6, [(1, D_RWKV)] * 4, name="postscan_bwd")
    dy_l = _to_value_rows(dy, bsz, seq)
    late_grads = {"w_out": g_w_out[None], "w_gate": g_w_gate[None], "w_up": g_w_up[None], "w_down": g_w_down[None]}
    g_f, g_b, dv_f, dv_b, *late_parts = _scan_bwd(
        xall, v_l, dy_l, hist, fin, sa, name="scan_bwd",
        exchange=[_to_slots(late_grads[n], SHARD_AXIS[n]).astype(BF16) for n in LATE] if late else ())
    dsc = _from_key_rows(g_f + g_b, bsz, seq)
    dv_s = _from_value_rows(dv_f + dv_b, bsz, seq)

    def pre_bwd(r_, k_, xw_, xa_, xg_, dkk_, dr_s, dwf_, dwb_, dbf_, dbb_, dkdf_s, dkdb_s,
                dr_c_, dv_c_, dv_s_, dkdf_c_, dkdb_c_, dg_, *consts):
        _, vjp = jax.vjp(lambda *a: _prescan_math(*a, consts[-1]), r_, k_, xw_, xa_, xg_, *consts[:-1])
        grads = vjp((dkk_, dr_s + dr_c_, dwf_, dwb_, dbf_, dbb_, dkdf_s + dkdf_c_, dkdb_s + dkdb_c_, dg_))
        dr_, dk_, dxw_, dxa_, dxg_ = grads[:5]
        return (dr_, dk_, dv_c_ + dv_s_, dxw_, dxa_, dxg_) + tuple(grads[5:])

    pre_b_rows = (pre_rows + [(dsc, j, 512) for j in range(N_GROUP)]
                  + [dr_c, dv_c, dv_s, dkdf_c, dkdb_c, dg])
    pre_b = _rowwise(pre_bwd, pre_b_rows, pre_consts, [[512, 512, 512, 128, 128, 256]],
                     [(1, D_RWKV)] * 7 + [(128, D_RWKV)] * 4 + [(256, D_RWKV)], name="prescan_bwd")
    d_pss = pre_b[0]
    d_kk_, d_w0f, d_w0b, d_a0f, d_a0b, d_kaf, d_kab, d_wupf, d_wupb, d_aupf, d_aupb, d_gup = pre_b[1:]
    dp, d_mu, d_conv = _shift_conv_bwd(p, d_pss, do, mu, conv_w, seq, name="shift_conv_bwd")
    g_w_in = _mm(h1, dp, ta=True, name="mm_in_dw")
    grads = {
        "w_in": _unpad_cols(g_w_in, P_SEGS)[None], "mu_shift": _unpad_cols(d_mu, S_SEGS),
        "w_up_f": d_wupf[None, :D_LORA], "w0_f": d_w0f, "w_up_b": d_wupb[None, :D_LORA], "w0_b": d_w0b,
        "a_up_f": d_aupf[None, :D_LORA], "a0_f": d_a0f, "a_up_b": d_aupb[None, :D_LORA], "a0_b": d_a0b,
        "g_up": d_gup[None, :D_GATE], "k_k": d_kk_, "k_a_f": d_kaf, "k_a_b": d_kab,
        "r_k_f": d_rkf, "r_k_b": d_rkb, "gn_w": d_gn_w, "gn_b": d_gn_b, "conv_w": d_conv[None, :3],
        "w_out": g_w_out[None], "norm2_w": d_norm2, "w_gate": g_w_gate[None], "w_up": g_w_up[None],
        "w_down": g_w_down[None], "norm_f_w": d_normf,
    }
    early = ("w_in",) + LORA
    parts = dict(zip(LATE, late_parts))
    if late:
        vec_rows = jnp.concatenate([grads[n] for n in VEC] + [jnp.zeros((16 - len(VEC), D_RWKV), F32)], axis=0)
        slots = [_to_slots(grads[n], SHARD_AXIS[n]).astype(BF16 if n in BIG else F32) for n in early]
        dh1, *recv = _mm(dp, w_in, tb=True, exchange=(slots, [vec_rows]), name="mm_in_dx")
        parts.update(zip(early + ("vec",), recv))
    else:
        dh1 = _mm(dp, w_in, tb=True, name="mm_in_dx")
    dx, grads["norm1_w"] = _rowwise(rms_bwd, [x2d, dh1, dx1], [norm1], [D_MODEL], [(1, D_MODEL)], name="rms1_bwd",
                                    tb=WIDE_TILE)
    return loss_acc, dx.reshape(bsz, seq, D_MODEL), grads, parts


def _hbm_specs(n):
    return [pl.BlockSpec(memory_space=pl.ANY)] * n


def _all_gather(arrs, *, name):
    n = len(arrs)

    def body(*refs):
        x_refs, out_refs = refs[:n], refs[n:2 * n]
        send_sems, recv_sems, local_sems = refs[2 * n:]
        x, y, c = lax.axis_index("x"), lax.axis_index("y"), lax.axis_index("c")
        me, sibling = (x, y, c), (x, y, 1 - c)
        chips = [(1 - x, y), (x, 1 - y), (1 - x, 1 - y)]

        def slot(a, px, py, pc):
            return out_refs[a].at[4 * px + 2 * py + pc]

        def copy(a, k, block, to, src=None):
            return pltpu.make_async_remote_copy(
                src_ref=slot(a, *block) if src is None else src, dst_ref=slot(a, *block),
                send_sem=send_sems.at[k, a], recv_sem=recv_sems.at[k, a],
                device_id=to, device_id_type=pl.DeviceIdType.MESH)

        mine = [pltpu.make_async_copy(x_refs[a], slot(a, *me), local_sems.at[a]) for a in range(n)]
        for cp in mine:
            cp.start()
        first = []
        for a in range(n):
            first.append(copy(a, 0, me, sibling, src=x_refs[a]))
            first += [copy(a, 1 + j, me, (*chip, c), src=x_refs[a]) for j, chip in enumerate(chips)]
        for cp in first:
            cp.start()
        passed = []
        for j, chip in enumerate(chips):
            for a in range(n):
                copy(a, 1 + j, (*chip, c), me).wait_recv()
                cp = copy(a, 4 + j, (*chip, c), sibling)
                cp.start()
                passed.append(cp)
        for a in range(n):
            copy(a, 0, sibling, me).wait_recv()
            for j, chip in enumerate(chips):
                copy(a, 4 + j, (*chip, 1 - c), me).wait_recv()
        for cp in first + passed:
            cp.wait_send()
        for cp in mine:
            cp.wait()

    return pl.pallas_call(
        body, out_shape=[jax.ShapeDtypeStruct((N_DEV,) + a.shape, a.dtype) for a in arrs],
        in_specs=_hbm_specs(n), out_specs=_hbm_specs(n),
        scratch_shapes=[pltpu.SemaphoreType.DMA((7, n)), pltpu.SemaphoreType.DMA((7, n)),
                        pltpu.SemaphoreType.DMA((n,))],
        name=name)(*arrs)


def _exchange(sliced, whole, *, name):
    arrs = list(sliced) + list(whole)
    n, n_sliced = len(arrs), len(sliced)

    def body(*refs):
        copies = _exchange_copies(refs[:n], refs[n:2 * n], n_sliced, *refs[2 * n:])
        for cp in copies:
            cp.start()
        for cp in copies:
            cp.wait()

    return pl.pallas_call(
        body, out_shape=_exchange_out_shapes(arrs, n_sliced), in_specs=_hbm_specs(n), out_specs=_hbm_specs(n),
        scratch_shapes=_exchange_sems(n), name=name)(*arrs)


def _exchange_out_shapes(arrs, n_sliced):
    return [jax.ShapeDtypeStruct(a.shape if i < n_sliced else (N_DEV,) + a.shape, a.dtype)
            for i, a in enumerate(arrs)]


def _exchange_sems(n):
    return [pltpu.SemaphoreType.DMA((7, n)), pltpu.SemaphoreType.DMA((7, n)), pltpu.SemaphoreType.DMA((n,))]


def _exchange_copies(in_refs, out_refs, n_sliced, send_sems, recv_sems, local_sems):
    n = len(in_refs)
    x, y, c = lax.axis_index("x"), lax.axis_index("y"), lax.axis_index("c")
    me = 4 * x + 2 * y + c

    def src(a, dev):
        return in_refs[a].at[dev] if a < n_sliced else in_refs[a]

    copies = [pltpu.make_async_copy(src(a, me), out_refs[a].at[me], local_sems.at[a]) for a in range(n)]
    for k in range(1, N_DEV):
        px = 1 - x if k & 4 else x
        py = 1 - y if k & 2 else y
        pc = 1 - c if k & 1 else c
        for a in range(n):
            copies.append(pltpu.make_async_remote_copy(
                src_ref=src(a, 4 * px + 2 * py + pc), dst_ref=out_refs[a].at[me],
                send_sem=send_sems.at[k - 1, a], recv_sem=recv_sems.at[k - 1, a],
                device_id=(px, py, pc), device_id_type=pl.DeviceIdType.MESH))
    return copies


def _adam_math(g, w, m, v):
    nm = ADAM_B1 * m + (1.0 - ADAM_B1) * g
    nv = ADAM_B2 * v + (1.0 - ADAM_B2) * (g * g)
    m_hat = nm / (1.0 - ADAM_B1 ** ADAM_STEP)
    v_hat = nv / (1.0 - ADAM_B2 ** ADAM_STEP)
    return -ADAM_LR * (m_hat / (jnp.sqrt(v_hat) + ADAM_EPS) + ADAM_WD * w), nm, nv


def _slot_sum(ref):
    g = ref[0].astype(F32)
    for s in range(1, N_DEV):
        g = g + ref[s].astype(F32)
    return g


def _adamw_big(parts, w, m, v, *, name):
    _, rws, cols = w.shape
    tr = _tile(rws, (256, 176, 128))

    def kern(p_ref, w_ref, m_ref, v_ref, g_ref, d_ref, nm_ref, nv_ref):
        g = _slot_sum(p_ref)
        g_ref[...] = g
        d_ref[...], nm_ref[...], nv_ref[...] = _adam_math(g, w_ref[...], m_ref[...], v_ref[...])

    spec = pl.BlockSpec((1, tr, cols), lambda i: (0, i, 0))
    return pl.pallas_call(
        kern, out_shape=[jax.ShapeDtypeStruct(w.shape, F32)] * 4, grid=(rws // tr,),
        in_specs=[pl.BlockSpec((N_DEV, 1, tr, cols), lambda i: (0, 0, i, 0)), spec, spec, spec],
        out_specs=[spec] * 4, compiler_params=_params(("parallel",)), name=name)(parts, w, m, v)


def _adamw_small(lora_parts, vec_parts, wide_parts, wmv, *, name):
    names = LORA + VEC + WIDE
    n_l, n = len(LORA), len(names)
    flat = [a for trip in wmv for a in trip]

    def kern(*refs):
        l_refs, vec_ref, wide_ref = refs[:n_l], refs[n_l], refs[n_l + 1]
        in_refs = refs[n_l + 2:n_l + 2 + 3 * n]
        out_refs = refs[n_l + 2 + 3 * n:]
        vec_sum, wide_sum = _slot_sum(vec_ref), _slot_sum(wide_ref)
        for i, nm in enumerate(names):
            w_ref, m_ref, v_ref = in_refs[3 * i:3 * i + 3]
            if i < n_l:
                g = _slot_sum(l_refs[i])
            elif nm in VEC:
                g = vec_sum[i - n_l:i - n_l + 1, :]
            else:
                g = wide_sum[WIDE.index(nm):WIDE.index(nm) + 1, :w_ref.shape[-1]]
            o = out_refs[4 * i:4 * i + 4]
            o[0][...] = g
            o[1][...], o[2][...], o[3][...] = _adam_math(g, w_ref[...], m_ref[...], v_ref[...])

    out_shape = [jax.ShapeDtypeStruct(trip[0].shape, F32) for trip in wmv for _ in range(4)]
    outs = pl.pallas_call(kern, out_shape=out_shape, name=name,
                          compiler_params=pltpu.CompilerParams(vmem_limit_bytes=VMEM_LIMIT))(
        *lora_parts, vec_parts, wide_parts, *flat)
    return [tuple(outs[4 * i:4 * i + 4]) for i in range(n)]


def _to_slots(g, axis):
    _, rws, cols = g.shape
    if axis == 1:
        return g.reshape(N_DEV, 1, rws // N_DEV, cols)
    return g.reshape(1, rws, N_DEV, cols // N_DEV).transpose(2, 0, 1, 3)


def _from_slots(got, axis):
    _, _, rws, cols = got.shape
    if axis == 1:
        return got.reshape(1, N_DEV * rws, cols)
    return got.transpose(1, 2, 0, 3).reshape(1, rws, N_DEV * cols)


def _pad_lanes(a, width):
    return jnp.concatenate([a, jnp.zeros(a.shape[:-1] + (width - a.shape[-1],), a.dtype)], axis=-1)


def kernel(x, norm1_w, w_in, mu_shift, w_up_f, w0_f, w_up_b, w0_b, a_up_f, a0_f, a_up_b, a0_b, g_up, k_k, k_a_f, k_a_b, r_k_f, r_k_b, gn_w, gn_b, conv_w, w_out, norm2_w, w_gate, w_up, w_down, norm_f_w, loss_target, m_norm1_w, m_w_in, m_mu_shift, m_w_up_f, m_w0_f, m_w_up_b, m_w0_b, m_a_up_f, m_a0_f, m_a_up_b, m_a0_b, m_g_up, m_k_k, m_k_a_f, m_k_a_b, m_r_k_f, m_r_k_b, m_gn_w, m_gn_b, m_conv_w, m_w_out, m_norm2_w, m_w_gate, m_w_up, m_w_down, m_norm_f_w, v_norm1_w, v_w_in, v_mu_shift, v_w_up_f, v_w0_f, v_w_up_b, v_w0_b, v_a_up_f, v_a0_f, v_a_up_b, v_a0_b, v_g_up, v_k_k, v_k_a_f, v_k_a_b, v_r_k_f, v_r_k_b, v_gn_w, v_gn_b, v_conv_w, v_w_out, v_norm2_w, v_w_gate, v_w_up, v_w_down, v_norm_f_w):
    local = dict(norm1_w=norm1_w, w_in=w_in, mu_shift=mu_shift, w_up_f=w_up_f, w0_f=w0_f, w_up_b=w_up_b,
                 w0_b=w0_b, a_up_f=a_up_f, a0_f=a0_f, a_up_b=a_up_b, a0_b=a0_b, g_up=g_up, k_k=k_k, k_a_f=k_a_f,
                 k_a_b=k_a_b, r_k_f=r_k_f, r_k_b=r_k_b, gn_w=gn_w, gn_b=gn_b, conv_w=conv_w, w_out=w_out,
                 norm2_w=norm2_w, w_gate=w_gate, w_up=w_up, w_down=w_down, norm_f_w=norm_f_w)
    mom_m = dict(norm1_w=m_norm1_w, w_in=m_w_in, mu_shift=m_mu_shift, w_up_f=m_w_up_f, w0_f=m_w0_f,
                 w_up_b=m_w_up_b, w0_b=m_w0_b, a_up_f=m_a_up_f, a0_f=m_a0_f, a_up_b=m_a_up_b, a0_b=m_a0_b,
                 g_up=m_g_up, k_k=m_k_k, k_a_f=m_k_a_f, k_a_b=m_k_a_b, r_k_f=m_r_k_f, r_k_b=m_r_k_b,
                 gn_w=m_gn_w, gn_b=m_gn_b, conv_w=m_conv_w, w_out=m_w_out, norm2_w=m_norm2_w, w_gate=m_w_gate,
                 w_up=m_w_up, w_down=m_w_down, norm_f_w=m_norm_f_w)
    mom_v = dict(norm1_w=v_norm1_w, w_in=v_w_in, mu_shift=v_mu_shift, w_up_f=v_w_up_f, w0_f=v_w0_f,
                 w_up_b=v_w_up_b, w0_b=v_w0_b, a_up_f=v_a_up_f, a0_f=v_a0_f, a_up_b=v_a_up_b, a0_b=v_a0_b,
                 g_up=v_g_up, k_k=v_k_k, k_a_f=v_k_a_f, k_a_b=v_k_a_b, r_k_f=v_r_k_f, r_k_b=v_r_k_b,
                 gn_w=v_gn_w, gn_b=v_gn_b, conv_w=v_conv_w, w_out=v_w_out, norm2_w=v_norm2_w, w_gate=v_w_gate,
                 w_up=v_w_up, w_down=v_w_down, norm_f_w=v_norm_f_w)

    early = ("w_in",) + LORA
    got = _all_gather([local["w_in"].astype(BF16)] + [local[n] for n in LORA], name="gather")
    full = dict(local)
    full.update({n: _from_slots(a, SHARD_AXIS[n]) for n, a in zip(early, got)})

    loss_part, grad_x, grads, parts = _local_step(x, loss_target, full,
                                                  late={n: local[n].astype(BF16) for n in LATE})

    wide_rows = jnp.concatenate([_pad_lanes(a, WIDE_ROW) for a in [grads[n] for n in WIDE] + [loss_part]]
                                + [jnp.zeros((SUBLANES - len(WIDE) - 1, WIDE_ROW), F32)], axis=0)
    wide_parts, = _exchange([], [wide_rows], name="grad_exchange")
    loss = jnp.sum(wide_parts[:, len(WIDE), 0])
    out = {}
    for n in BIG:
        out[n] = _adamw_big(parts[n], local[n], mom_m[n], mom_v[n], name="adamw_" + n)

    def small_form(n, a):
        if n in LORA:
            return a
        a = a.reshape(1, -1)
        return _pad_lanes(a, WIDE_ROW) if n == "mu_shift" else a

    small = LORA + VEC + WIDE
    res = _adamw_small([parts[n] for n in LORA], parts["vec"], wide_parts,
                       [tuple(small_form(n, d[n]) for d in (local, mom_m, mom_v)) for n in small],
                       name="adamw_small")
    for n, quad in zip(small, res):
        out[n] = tuple(a[..., :local[n].size].reshape(local[n].shape) if n not in LORA else a for a in quad)
    return (loss, grad_x, *[out[n][i] for i in range(4) for n in WEIGHTS])
```

```python
import functools

import jax
import jax.numpy as jnp
from jax import lax
from jax.experimental import pallas as pl
from jax.experimental.pallas import tpu as pltpu

F32 = jnp.float32
BF16 = jnp.bfloat16
HIGHEST = lax.Precision.HIGHEST

N_DEV = 8
D_MODEL = 1024
D_RWKV = 512
D_CONV = 512
HEAD = 64
N_HEAD = D_RWKV // HEAD
D_LORA = 64
D_GATE = 160
D_SHIFTED = 3 * D_RWKV + 2 * D_LORA + D_GATE
XW0, XA0, XG0 = 1536, 1664, 1792
D_SP = 2048
D_INP = D_SP + 3 * D_CONV
LOG_DECAY_SCALE = 0.606531
RMS_EPS = 1e-6
GN_EPS = 64e-5
NORM_EPS = 1e-12
ADAM_LR, ADAM_B1, ADAM_B2, ADAM_EPS, ADAM_WD, ADAM_STEP = 0.001, 0.9, 0.999, 1e-08, 0.01, 10

LANES = 128
SUBLANES = 8
VMEM_LIMIT = 48 * 1024 * 1024
SCAN_CHUNK = 32
SCAN_VMEM_LIMIT = 58 * 1024 * 1024
SCAN_UNROLL = 3
ROW_TILE = 128
WIDE_TILE = 256
RELAYOUT_TILE = 512

BIG = ("w_in", "w_out", "w_gate", "w_up", "w_down")
LORA = ("w_up_f", "w_up_b", "a_up_f", "a_up_b", "g_up", "conv_w")
SHARD_AXIS = {"w_in": 2, "w_out": 1, "w_gate": 2, "w_up": 2, "w_down": 1, "w_up_f": 2, "w_up_b": 2,
              "a_up_f": 2, "a_up_b": 2, "g_up": 2, "conv_w": 2}
VEC = ("w0_f", "w0_b", "a0_f", "a0_b", "k_k", "k_a_f", "k_a_b", "r_k_f", "r_k_b", "gn_w", "gn_b")
WIDE = ("mu_shift", "norm1_w", "norm2_w", "norm_f_w")
WIDE_ROW = 2048
WEIGHTS = ("norm1_w", "w_in", "mu_shift", "w_up_f", "w0_f", "w_up_b", "w0_b", "a_up_f", "a0_f", "a_up_b",
           "a0_b", "g_up", "k_k", "k_a_f", "k_a_b", "r_k_f", "r_k_b", "gn_w", "gn_b", "conv_w", "w_out",
           "norm2_w", "w_gate", "w_up", "w_down", "norm_f_w")


def _params(sem, limit=VMEM_LIMIT):
    return pltpu.CompilerParams(dimension_semantics=sem, vmem_limit_bytes=limit)


def _tile(n, cands):
    for c in cands:
        if n % c == 0:
            return c
    raise ValueError(f"no tile for {n}")


def _mm(a, b, *, ta=False, tb=False, add=None, exchange=None, name):
    (k_dim, m) = a.shape if ta else a.shape[::-1]
    (k2, n) = b.shape[::-1] if tb else b.shape
    assert k_dim == k2, (a.shape, b.shape, ta, tb)
    tm = _tile(m, (1408, 1024, 512, 256, 128))
    tn = _tile(n, (1408, 1024, 896, 512, 256, 128))
    tk = _tile(k_dim, (1408, 1024, 896, 512, 256, 128))
    nk = k_dim // tk
    grid = (m // tm, n // tn, nk)
    dims = (((0 if ta else 1,), (1 if tb else 0,)), ((), ()))
    sliced, whole = exchange or ((), ())
    riders = list(sliced) + list(whole)
    n_x, n_in = len(riders), 2 + (add is not None)

    def kern(*refs):
        a_ref, b_ref = refs[:2]
        add_ref = refs[2] if add is not None else None
        o_ref, acc_ref = refs[n_in + n_x], refs[n_in + 2 * n_x + 1]
        k = pl.program_id(2)
        step = (pl.program_id(0) * grid[1] + pl.program_id(1)) * nk + k

        def copies():
            return _exchange_copies(refs[n_in:n_in + n_x], refs[n_in + n_x + 1:n_in + 2 * n_x + 1], len(sliced),
                                    *refs[n_in + 2 * n_x + 2:])

        if n_x:
            @pl.when(step == 0)
            def _():
                for cp in copies():
                    cp.start()

        @pl.when(k == 0)
        def _():
            acc_ref[...] = jnp.zeros_like(acc_ref)

        acc_ref[...] += lax.dot_general(a_ref[...].astype(BF16), b_ref[...].astype(BF16), dims,
                                        preferred_element_type=F32)

        @pl.when(k == nk - 1)
        def _():
            if add is None:
                o_ref[...] = acc_ref[...]
            else:
                o_ref[...] = acc_ref[...] + add_ref[...]

        if n_x:
            @pl.when(step == grid[0] * grid[1] * nk - 1)
            def _():
                for cp in copies():
                    cp.wait()

    a_spec = (pl.BlockSpec((tk, tm), lambda i, j, k: (k, i)) if ta
              else pl.BlockSpec((tm, tk), lambda i, j, k: (i, k)))
    b_spec = (pl.BlockSpec((tn, tk), lambda i, j, k: (j, k)) if tb
              else pl.BlockSpec((tk, tn), lambda i, j, k: (k, j)))
    o_spec = pl.BlockSpec((tm, tn), lambda i, j, k: (i, j))
    in_specs = [a_spec, b_spec] + ([o_spec] if add is not None else []) + _hbm_specs(n_x)
    args = (a, b) + ((add,) if add is not None else ()) + tuple(riders)
    out = pl.pallas_call(
        kern, out_shape=[jax.ShapeDtypeStruct((m, n), F32)] + _exchange_out_shapes(riders, len(sliced)), grid=grid,
        in_specs=in_specs, out_specs=[o_spec] + _hbm_specs(n_x),
        scratch_shapes=[pltpu.VMEM((tm, tn), F32)] + (_exchange_sems(n_x) if n_x else []),
        compiler_params=_params(("arbitrary",) * 3 if n_x else ("parallel", "parallel", "arbitrary")),
        name=name)(*args)
    return out if n_x else out[0]


def _swiglu(g, u):
    return jax.nn.silu(g) * u


FFN_TN = 256


def _mm_swiglu(h, w_gate, w_up, *, name):
    m, k_dim = h.shape
    n = w_gate.shape[1]
    tm = _tile(m, (1024, 512, 256, 128))

    def kern(h_ref, wg_ref, wu_ref, g_ref, u_ref, f_ref):
        hv = h_ref[...].astype(BF16)
        g = jnp.dot(hv, wg_ref[...].astype(BF16), preferred_element_type=F32)
        u = jnp.dot(hv, wu_ref[...].astype(BF16), preferred_element_type=F32)
        g_ref[...] = g
        u_ref[...] = u
        f_ref[...] = _swiglu(g, u).astype(f_ref.dtype)

    w_spec = pl.BlockSpec((k_dim, FFN_TN), lambda i, j: (0, j))
    o_spec = pl.BlockSpec((tm, FFN_TN), lambda i, j: (i, j))
    return pl.pallas_call(
        kern, out_shape=[jax.ShapeDtypeStruct((m, n), F32)] * 2 + [jax.ShapeDtypeStruct((m, n), BF16)],
        grid=(m // tm, n // FFN_TN), in_specs=[pl.BlockSpec((tm, k_dim), lambda i, j: (i, 0)), w_spec, w_spec],
        out_specs=[o_spec] * 3, compiler_params=_params(("parallel", "parallel")), name=name)(h, w_gate, w_up)


def _mm_swiglu_bwd(dx, w_down, g, u, *, name):
    m, k_dim = dx.shape
    n = w_down.shape[0]
    tm = _tile(m, (1024, 512, 256, 128))

    def kern(dx_ref, w_ref, g_ref, u_ref, dg_ref, du_ref):
        df = lax.dot_general(dx_ref[...].astype(BF16), w_ref[...].astype(BF16), (((1,), (1,)), ((), ())),
                             preferred_element_type=F32)
        _, vjp = jax.vjp(_swiglu, g_ref[...], u_ref[...])
        dg, du = vjp(df)
        dg_ref[...] = dg.astype(dg_ref.dtype)
        du_ref[...] = du.astype(du_ref.dtype)

    o_spec = pl.BlockSpec((tm, FFN_TN), lambda i, j: (i, j))
    return pl.pallas_call(
        kern, out_shape=[jax.ShapeDtypeStruct((m, n), BF16)] * 2, grid=(m // tm, n // FFN_TN),
        in_specs=[pl.BlockSpec((tm, k_dim), lambda i, j: (i, 0)), pl.BlockSpec((FFN_TN, k_dim), lambda i, j: (j, 0)),
                  o_spec, o_spec],
        out_specs=[o_spec] * 2, compiler_params=_params(("parallel", "parallel")), name=name)(dx, w_down, g, u)


def _rowwise(fn, rows, consts, out_rows, out_accs, *, name, tb=ROW_TILE, out_dtype=F32):
    t = (rows[0][0] if isinstance(rows[0], tuple) else rows[0]).shape[0]
    n_r, n_c, n_o, n_a = len(rows), len(consts), len(out_rows), len(out_accs)
    pieces = [w if isinstance(w, (list, tuple)) else [w] for w in out_rows]

    def kern(*refs):
        r_refs = refs[:n_r]
        c_refs = refs[n_r:n_r + n_c]
        o_refs = refs[n_r + n_c:n_r + n_c + n_o]
        a_refs = refs[n_r + n_c + n_o:]
        vals = fn(*[r[...] for r in r_refs], *[c[...] for c in c_refs])
        vals = list(vals) if isinstance(vals, (tuple, list)) else [vals]
        pos = 0
        for o_ref, ws in zip(o_refs, pieces):
            off = 0
            for w in ws:
                o_ref[:, off:off + w] = vals[pos].astype(o_ref.dtype)
                off += w
                pos += 1
        if n_a:
            @pl.when(pl.program_id(0) == 0)
            def _():
                for a_ref in a_refs:
                    a_ref[...] = jnp.zeros_like(a_ref)
            for a_ref, v in zip(a_refs, vals[pos:]):
                a_ref[...] += v

    in_specs, args = [], []
    for r in rows:
        if isinstance(r, tuple):
            arr, blk, w = r
            in_specs.append(pl.BlockSpec((tb, w), functools.partial(lambda i, blk: (i, blk), blk=blk)))
        else:
            arr = r
            in_specs.append(pl.BlockSpec((tb, arr.shape[1]), lambda i: (i, 0)))
        args.append(arr)
    for c in consts:
        in_specs.append(pl.BlockSpec(c.shape, lambda i: (0, 0)))
        args.append(c)
    out_shape = [jax.ShapeDtypeStruct((t, sum(ws)), out_dtype) for ws in pieces]
    out_specs = [pl.BlockSpec((tb, sum(ws)), lambda i: (i, 0)) for ws in pieces]
    for shp in out_accs:
        out_shape.append(jax.ShapeDtypeStruct(shp, F32))
        out_specs.append(pl.BlockSpec(shp, lambda i: (0, 0)))
    res = pl.pallas_call(
        kern, out_shape=out_shape, grid=(t // tb,), in_specs=in_specs, out_specs=out_specs,
        compiler_params=_params(("arbitrary",) if n_a else ("parallel",)), name=name)(*args)
    return res


def _rms(x, w):
    return x * lax.rsqrt(jnp.mean(x * x, axis=-1, keepdims=True) + RMS_EPS) * w


def _seg_sum(x, bd):
    return jnp.concatenate(
        [jnp.dot(x[:, LANES * j:LANES * (j + 1)], bd, precision=HIGHEST, preferred_element_type=F32)
         for j in range(x.shape[1] // LANES)], axis=1)


@jax.custom_vjp
def _seg(x, bd):
    return _seg_sum(x, bd)


_seg.defvjp(lambda x, bd: (_seg_sum(x, bd), bd), lambda bd, ct: (_seg_sum(ct, bd), jnp.zeros_like(bd)))


def _colsum(x):
    return jnp.sum(x, axis=0, keepdims=True)


def _prescan_math(r, k, xw, xa, xg, k_k, w0f, w0b, a0f, a0b, kaf, kab, wupf, wupb, aupf, aupb, gup, bd):
    kkr = k * k_k
    norm = jnp.sqrt(_seg(kkr * kkr, bd))
    kk = kkr / jnp.maximum(norm, NORM_EPS)
    th = jnp.tanh(xw)

    def direction(w0, wup, a0, aup, ka):
        logit = w0 + jnp.dot(th, wup, preferred_element_type=F32)
        w = jnp.exp(-LOG_DECAY_SCALE * jax.nn.sigmoid(logit))
        a = jax.nn.sigmoid(a0 + jnp.dot(xa, aup, preferred_element_type=F32))
        kd = k * (1.0 + (a - 1.0) * ka)
        return w, kd, kk * a

    wf, kdf, bf = direction(w0f, wupf, a0f, aupf, kaf)
    wb, kdb, bb = direction(w0b, wupb, a0b, aupb, kab)
    g = jnp.dot(jax.nn.sigmoid(xg), gup, preferred_element_type=F32)
    return kk, r, wf, wb, bf, bb, kdf, kdb, g


def _postscan_math(y, r, v, kdf, kdb, g, gn_w, gn_b, rkf, rkb, bd):
    mean = _seg(y, bd) * (1.0 / HEAD)
    yc = y - mean
    var = _seg(yc * yc, bd) * (1.0 / HEAD)
    yg = yc * lax.rsqrt(var + GN_EPS) * gn_w + gn_b
    bonus = (_seg(r * kdf * rkf, bd) + _seg(r * kdb * rkb, bd)) * v
    return (yg + bonus) * g


def _halo_specs(width, col_blk, tb, t):
    nb = t // SUBLANES
    step = tb // SUBLANES
    main = pl.BlockSpec((tb, width), lambda i: (i, col_blk))
    prev = pl.BlockSpec((SUBLANES, width), lambda i: (jnp.maximum(i * step - 1, 0), col_blk))
    nxt = pl.BlockSpec((SUBLANES, width), lambda i: (jnp.minimum((i + 1) * step, nb - 1), col_blk))
    return [main, prev, nxt]


def _neighbours(z, prev8, next8, first, last):
    tb = z.shape[0]
    row = lax.broadcasted_iota(jnp.int32, z.shape, 0)
    prow = jnp.where(first, 0.0, prev8[SUBLANES - 1:SUBLANES, :])
    nrow = jnp.where(last, 0.0, next8[0:1, :])
    down = jnp.where(row == 0, prow, pltpu.roll(z, 1, 0))
    up = jnp.where(row == tb - 1, nrow, pltpu.roll(z, tb - 1, 0))
    return down, up


def _shift_conv_fwd(p, mu, conv_w, seq, *, name, tb=ROW_TILE):
    t = p.shape[0]
    per_seq = seq // tb

    def kern(p_ref, pp_ref, pn_ref, mu_ref, cw_ref, pss_ref, oc_ref):
        i = pl.program_id(0)
        first = (i % per_seq) == 0
        last = (i % per_seq) == per_seq - 1
        ps = p_ref[:, :D_SP]
        down, up = _neighbours(ps, pp_ref[:, :D_SP], pn_ref[:, :D_SP], first, last)
        pss_ref[...] = ps + mu_ref[...] * (0.5 * (down + up) - ps)
        gb = p_ref[:, D_SP:D_SP + D_CONV]
        u = p_ref[:, D_SP + D_CONV:D_SP + 2 * D_CONV] * p_ref[:, D_SP + 2 * D_CONV:]
        u_p = pp_ref[:, D_SP + D_CONV:D_SP + 2 * D_CONV] * pp_ref[:, D_SP + 2 * D_CONV:]
        u_n = pn_ref[:, D_SP + D_CONV:D_SP + 2 * D_CONV] * pn_ref[:, D_SP + 2 * D_CONV:]
        udown, uup = _neighbours(u, u_p, u_n, first, last)
        oc_ref[...] = gb * (cw_ref[0:1, :] * udown + cw_ref[1:2, :] * u + cw_ref[2:3, :] * uup)

    return pl.pallas_call(
        kern,
        out_shape=[jax.ShapeDtypeStruct((t, D_SP), F32), jax.ShapeDtypeStruct((t, D_CONV), F32)],
        grid=(t // tb,),
        in_specs=_halo_specs(D_INP, 0, tb, t) + [pl.BlockSpec((1, D_SP), lambda i: (0, 0)),
                                                 pl.BlockSpec((SUBLANES, D_CONV), lambda i: (0, 0))],
        out_specs=[pl.BlockSpec((tb, D_SP), lambda i: (i, 0)), pl.BlockSpec((tb, D_CONV), lambda i: (i, 0))],
        compiler_params=_params(("parallel",)), name=name)(p, p, p, mu, conv_w)


def _shift_conv_bwd(p, d_pss, d_o, mu, conv_w, seq, *, name, tb=ROW_TILE):
    t = p.shape[0]
    per_seq = seq // tb

    def kern(p_ref, pp_ref, pn_ref, d_ref, dp_ref, dn_ref, do_ref, dop_ref, don_ref, mu_ref, cw_ref,
             out_ref, dmu_ref, dcw_ref):
        i = pl.program_id(0)
        first = (i % per_seq) == 0
        last = (i % per_seq) == per_seq - 1

        @pl.when(i == 0)
        def _():
            dmu_ref[...] = jnp.zeros_like(dmu_ref)
            dcw_ref[...] = jnp.zeros_like(dcw_ref)

        mu_v = mu_ref[...]
        ps = p_ref[:, :D_SP]
        down, up = _neighbours(ps, pp_ref[:, :D_SP], pn_ref[:, :D_SP], first, last)
        d = d_ref[...]
        ddown, dup = _neighbours(d, dp_ref[...], dn_ref[...], first, last)
        out_ref[:, :D_SP] = (d - mu_v * d + 0.5 * (mu_v * ddown + mu_v * dup)).astype(out_ref.dtype)
        dmu_ref[...] += _colsum(d * (0.5 * (down + up) - ps))

        def parts(ref):
            return (ref[:, D_SP:D_SP + D_CONV], ref[:, D_SP + D_CONV:D_SP + 2 * D_CONV],
                    ref[:, D_SP + 2 * D_CONV:])

        gb, gc, hh = parts(p_ref)
        gb_p, gc_p, hh_p = parts(pp_ref)
        gb_n, gc_n, hh_n = parts(pn_ref)
        u = gc * hh
        udown, uup = _neighbours(u, gc_p * hh_p, gc_n * hh_n, first, last)
        cw0, cw1, cw2 = cw_ref[0:1, :], cw_ref[1:2, :], cw_ref[2:3, :]
        do = do_ref[...]
        duc = do * gb
        ducdown, ducup = _neighbours(duc, dop_ref[...] * gb_p, don_ref[...] * gb_n, first, last)
        du = cw0 * ducup + cw1 * duc + cw2 * ducdown
        out_ref[:, D_SP:D_SP + D_CONV] = (do * (cw0 * udown + cw1 * u + cw2 * uup)).astype(out_ref.dtype)
        out_ref[:, D_SP + D_CONV:D_SP + 2 * D_CONV] = (du * hh).astype(out_ref.dtype)
        out_ref[:, D_SP + 2 * D_CONV:] = (du * gc).astype(out_ref.dtype)
        dcw_ref[0:1, :] += _colsum(duc * udown)
        dcw_ref[1:2, :] += _colsum(duc * u)
        dcw_ref[2:3, :] += _colsum(duc * uup)

    return pl.pallas_call(
        kern,
        out_shape=[jax.ShapeDtypeStruct((t, D_INP), BF16), jax.ShapeDtypeStruct((1, D_SP), F32),
                   jax.ShapeDtypeStruct((SUBLANES, D_CONV), F32)],
        grid=(t // tb,),
        in_specs=(_halo_specs(D_INP, 0, tb, t) + _halo_specs(D_SP, 0, tb, t) + _halo_specs(D_CONV, 1, tb, t)
                  + [pl.BlockSpec((1, D_SP), lambda i: (0, 0)),
                     pl.BlockSpec((SUBLANES, D_CONV), lambda i: (0, 0))]),
        out_specs=[pl.BlockSpec((tb, D_INP), lambda i: (i, 0)), pl.BlockSpec((1, D_SP), lambda i: (0, 0)),
                   pl.BlockSpec((SUBLANES, D_CONV), lambda i: (0, 0))],
        compiler_params=_params(("arbitrary",)), name=name)(p, p, p, d_pss, d_pss, d_pss, d_o, d_o, d_o, mu, conv_w)


N_CHAIN = 16
N_GROUP = LANES // N_CHAIN
V_HI = HEAD // SUBLANES
G_KK, G_R, G_W, G_B, G_KD = 0, 1, (2, 3), (4, 5), (6, 7)


K_HI = HEAD // SUBLANES


def _tree_sum(terms):
    terms = list(terms)
    while len(terms) > 1:
        terms = [a + b for a, b in zip(terms[::2], terms[1::2])]
    return terms[0]


def _kscan_specs(nc):
    same = lambda c: c
    mirror = lambda c: nc - 1 - c

    def k_spec(fn):
        return pl.BlockSpec((SCAN_CHUNK, HEAD, LANES), lambda c: (fn(c), 0, 0))

    def v_spec(fn):
        return pl.BlockSpec((SCAN_CHUNK, SUBLANES, LANES), lambda c: (fn(c), 0, 0))

    return same, mirror, k_spec, v_spec


ST_SHAPE = (2, K_HI, V_HI, SUBLANES, LANES)


def _lane_group_index():
    lane = lax.broadcasted_iota(jnp.int32, (SUBLANES, LANES), 1)
    return lax.shift_right_logical(lane, jnp.full_like(lane, 4))


def _spread_groups(x, grp):
    rolled = [x] + [pltpu.roll(x, s * N_CHAIN, 1) for s in range(1, N_GROUP)]
    out = []
    for j in range(N_GROUP):
        t = rolled[(0 - j) % N_GROUP]
        for g in range(1, N_GROUP):
            t = jnp.where(grp == g, rolled[(g - j) % N_GROUP], t)
        out.append(t)
    return out


def _gather_groups(tiles, grp):
    total = None
    for s in range(N_GROUP):
        b = tiles[s % N_GROUP]
        for g in range(1, N_GROUP):
            b = jnp.where(grp == g, tiles[(g + s) % N_GROUP], b)
        b = pltpu.roll(b, s * N_CHAIN, 1) if s else b
        total = b if total is None else total + b
    return total


def _lane_group_sum(x):
    return _tree_sum([x] + [pltpu.roll(x, k * N_CHAIN, 1) for k in range(1, N_GROUP)])


def _key_row(x_t, grp, kh):
    r = SUBLANES * grp + kh
    return jnp.broadcast_to(x_t[r:r + 1, :], (SUBLANES, LANES))


def _acc(total, term):
    return term if total is None else total + term


SA_SHAPE = (2, V_HI, SUBLANES, LANES)


def _scan_fwd(xall, v_c, *, gather=(), name):
    steps = xall.shape[0]
    nc = steps // SCAN_CHUNK
    same, mirror, k_spec, v_spec = _kscan_specs(nc)
    last = SCAN_CHUNK - 1
    n_x = len(gather)

    def kern(*refs):
        xf_ref, xb_ref, vf_ref, vb_ref = refs[:4]
        yf_ref, yb_ref, hist_ref, fin_ref, sa_ref = refs[4 + n_x:9 + n_x]
        st_ref = refs[9 + 2 * n_x]
        c = pl.program_id(0)

        def riders():
            return _exchange_copies(refs[4:4 + n_x], refs[9 + n_x:9 + 2 * n_x], 0, *refs[10 + 2 * n_x:])

        @pl.when(c == 0)
        def _():
            st_ref[...] = jnp.zeros_like(st_ref)
            if n_x:
                for cp in riders():
                    cp.start()

        hist_ref[0] = st_ref[...]
        grp = _lane_group_index()

        def body(i, put):
            j = last - i
            for d, (x_t, v_t, y_ref, at) in enumerate(((xf_ref[i], vf_ref[i], yf_ref, i),
                                                       (xb_ref[j], vb_ref[j], yb_ref, j))):
                v_b = _spread_groups(v_t, grp)
                part = [None] * V_HI
                for kh in range(K_HI):
                    kk_r = _key_row(x_t, G_KK, kh)
                    for vh in range(V_HI):
                        part[vh] = _acc(part[vh], hist_ref[i, d, kh, vh] * kk_r)
                sa = [_lane_group_sum(p) for p in part]
                for vh in range(V_HI):
                    sa_ref[i, d, vh] = sa[vh]
                y_p = [None] * V_HI
                for kh in range(K_HI):
                    r_r, w_r = _key_row(x_t, G_R, kh), _key_row(x_t, G_W[d], kh)
                    b_r, kd_r = _key_row(x_t, G_B[d], kh), _key_row(x_t, G_KD[d], kh)
                    for vh in range(V_HI):
                        new = hist_ref[i, d, kh, vh] * w_r - sa[vh] * b_r + v_b[vh] * kd_r
                        put(d, kh, vh, new)
                        y_p[vh] = _acc(y_p[vh], new * r_r)
                y_ref[at] = _gather_groups(y_p, grp)

        def step(i, carry):
            def put(d, kh, vh, val):
                hist_ref[i + 1, d, kh, vh] = val
            body(i, put)
            return carry

        lax.fori_loop(0, last, step, 0, unroll=SCAN_UNROLL)

        def put_carry(d, kh, vh, val):
            st_ref[d, kh, vh] = val

        body(last, put_carry)

        @pl.when(c == nc - 1)
        def _():
            fin_ref[...] = st_ref[...]
            if n_x:
                for cp in riders():
                    cp.wait()

    return pl.pallas_call(
        kern,
        out_shape=[jax.ShapeDtypeStruct((steps, SUBLANES, LANES), F32)] * 2
        + [jax.ShapeDtypeStruct((steps,) + ST_SHAPE, F32), jax.ShapeDtypeStruct(ST_SHAPE, F32),
           jax.ShapeDtypeStruct((steps,) + SA_SHAPE, F32)]
        + _exchange_out_shapes(gather, 0),
        grid=(nc,), in_specs=[k_spec(same), k_spec(mirror), v_spec(same), v_spec(mirror)] + _hbm_specs(n_x),
        out_specs=[v_spec(same), v_spec(mirror),
                   pl.BlockSpec((SCAN_CHUNK,) + ST_SHAPE, lambda c: (c, 0, 0, 0, 0, 0)),
                   pl.BlockSpec(ST_SHAPE, lambda c: (0, 0, 0, 0, 0)),
                   pl.BlockSpec((SCAN_CHUNK,) + SA_SHAPE, lambda c: (c, 0, 0, 0, 0))] + _hbm_specs(n_x),
        scratch_shapes=[pltpu.VMEM(ST_SHAPE, F32)] + (_exchange_sems(n_x) if n_x else []),
        compiler_params=_params(("arbitrary",), SCAN_VMEM_LIMIT), name=name)(xall, xall, v_c, v_c, *gather)


def _scan_bwd(xall, v_c, dy_c, hist, fin, sa, *, exchange=(), name):
    steps = xall.shape[0]
    nc = steps // SCAN_CHUNK
    same, back, k_spec, v_spec = _kscan_specs(nc)
    last = SCAN_CHUNK - 1
    n_x = len(exchange)

    def kern(*refs):
        xf_ref, xb_ref, vf_ref, vb_ref, dyf_ref, dyb_ref, hist_ref, fin_ref, sa_ref = refs[:9]
        gf_ref, gb_ref, dvf_ref, dvb_ref = refs[9 + n_x:13 + n_x]
        ds_ref, after_ref = refs[13 + 2 * n_x:15 + 2 * n_x]
        c = pl.program_id(0)

        def riders():
            return _exchange_copies(refs[9:9 + n_x], refs[13 + n_x:13 + 2 * n_x], n_x, *refs[15 + 2 * n_x:])

        @pl.when(c == 0)
        def _():
            ds_ref[...] = jnp.zeros_like(ds_ref)
            after_ref[...] = fin_ref[...]
            if n_x:
                for cp in riders():
                    cp.start()

        grp = _lane_group_index()
        row = lax.broadcasted_iota(jnp.int32, (SUBLANES, LANES), 0)
        zero = jnp.zeros((SUBLANES, LANES), F32)

        def body(i, after):
            j = last - i
            for d, (x_t, v_t, dy_t, g_ref, dv_ref, at) in enumerate((
                    (xf_ref[i], vf_ref[i], dyf_ref[i], gf_ref, dvf_ref, i),
                    (xb_ref[j], vb_ref[j], dyb_ref[j], gb_ref, dvb_ref, j))):
                v_s, dy_s = _spread_groups(v_t, grp), _spread_groups(dy_t, grp)
                dsa_p, dv_p = [None] * V_HI, [None] * V_HI
                for kh in range(K_HI):
                    r_r = _key_row(x_t, G_R, kh)
                    b_r, kd_r = _key_row(x_t, G_B[d], kh), _key_row(x_t, G_KD[d], kh)
                    for vh in range(V_HI):
                        g = ds_ref[d, kh, vh] + dy_s[vh] * r_r
                        ds_ref[d, kh, vh] = g
                        dsa_p[vh] = _acc(dsa_p[vh], g * b_r)
                        dv_p[vh] = _acc(dv_p[vh], g * kd_r)
                dsa = [-_lane_group_sum(p) for p in dsa_p]
                sa = [sa_ref[i, d, vh] for vh in range(V_HI)]
                dv_ref[at] = _gather_groups(dv_p, grp)
                blocks = {G_KK: zero, G_R: zero, G_W[d]: zero, G_B[d]: zero, G_KD[d]: zero}
                for kh in range(K_HI):
                    w_r, kk_r = _key_row(x_t, G_W[d], kh), _key_row(x_t, G_KK, kh)
                    dkk = dr = dw = db = dkd = None
                    for vh in range(V_HI):
                        g, before = ds_ref[d, kh, vh], hist_ref[i, d, kh, vh]
                        dr = _acc(dr, after(d, kh, vh) * dy_s[vh])
                        dw = _acc(dw, g * before)
                        dkd = _acc(dkd, g * v_s[vh])
                        db = _acc(db, g * sa[vh])
                        dkk = _acc(dkk, before * dsa[vh])
                        ds_ref[d, kh, vh] = g * w_r + dsa[vh] * kk_r
                    for gi, a in ((G_KK, dkk), (G_R, dr), (G_W[d], dw), (G_B[d], -db), (G_KD[d], dkd)):
                        blocks[gi] = jnp.where(row == kh, _colsum(a), blocks[gi])
                for gi in range(N_GROUP):
                    g_ref[at, SUBLANES * gi:SUBLANES * (gi + 1), :] = blocks.get(gi, zero)

        body(last, lambda d, kh, vh: after_ref[d, kh, vh])

        def step(ii, carry):
            i = last - ii
            body(i, lambda d, kh, vh: hist_ref[i + 1, d, kh, vh])
            return carry

        lax.fori_loop(1, SCAN_CHUNK, step, 0, unroll=SCAN_UNROLL)
        after_ref[...] = hist_ref[0]

        if n_x:
            @pl.when(c == nc - 1)
            def _():
                for cp in riders():
                    cp.wait()

    return pl.pallas_call(
        kern,
        out_shape=[jax.ShapeDtypeStruct((steps, HEAD, LANES), F32)] * 2
        + [jax.ShapeDtypeStruct((steps, SUBLANES, LANES), F32)] * 2 + _exchange_out_shapes(exchange, n_x),
        grid=(nc,),
        in_specs=[k_spec(back), k_spec(same), v_spec(back), v_spec(same), v_spec(back), v_spec(same),
                  pl.BlockSpec((SCAN_CHUNK,) + ST_SHAPE, lambda c: (back(c), 0, 0, 0, 0, 0)),
                  pl.BlockSpec(ST_SHAPE, lambda c: (0, 0, 0, 0, 0)),
                  pl.BlockSpec((SCAN_CHUNK,) + SA_SHAPE, lambda c: (back(c), 0, 0, 0, 0))] + _hbm_specs(n_x),
        out_specs=[k_spec(back), k_spec(same), v_spec(back), v_spec(same)] + _hbm_specs(n_x),
        scratch_shapes=[pltpu.VMEM(ST_SHAPE, F32), pltpu.VMEM(ST_SHAPE, F32)]
        + (_exchange_sems(n_x) if n_x else []),
        compiler_params=_params(("arbitrary",), SCAN_VMEM_LIMIT), name=name)(xall, xall, v_c, v_c, dy_c, dy_c, hist, fin, sa,
                                                            *exchange)


def _bf16_pieces(x):
    hi = x.astype(BF16)
    rest = x - hi.astype(F32)
    mid = rest.astype(BF16)
    return hi, mid, (rest - mid.astype(F32)).astype(BF16)


def _to_key_rows(wide, bsz, seq, *, name):
    assert bsz == 2
    perm = _key_row_maps()
    tt = min(RELAYOUT_TILE, seq)
    per_seq = seq // tt

    def kern(x0_ref, x1_ref, p0_ref, p1_ref, o_ref):
        total = None
        for x_ref, p_ref in ((x0_ref, p0_ref), (x1_ref, p1_ref)):
            for piece in _bf16_pieces(x_ref[...]):
                term = jnp.dot(piece, p_ref[...], preferred_element_type=F32)
                total = term if total is None else total + term
        for r in range(K_HI):
            o_ref[:, r, :] = total[:, LANES * r:LANES * (r + 1)]

    p_spec = pl.BlockSpec((D_RWKV, K_HI * LANES), lambda i, a: (0, 0))
    return pl.pallas_call(
        kern, out_shape=jax.ShapeDtypeStruct((seq, HEAD, LANES), F32), grid=(per_seq, N_GROUP),
        in_specs=[pl.BlockSpec((tt, D_RWKV), lambda i, a: (i, a)),
                  pl.BlockSpec((tt, D_RWKV), lambda i, a: (per_seq + i, a)), p_spec, p_spec],
        out_specs=pl.BlockSpec((tt, K_HI, LANES), lambda i, a: (i, a, 0)),
        compiler_params=_params(("parallel", "parallel")), name=name)(wide, wide, *perm)


def _key_row_maps():
    src = jnp.arange(D_RWKV)
    head, kh, kl = src // HEAD, (src // SUBLANES) % K_HI, src % SUBLANES
    dst = jnp.arange(K_HI * LANES)
    return [((kh[:, None] == dst[None, :] // LANES) & (kl[:, None] == (dst[None, :] // N_CHAIN) % SUBLANES)
             & ((dst[None, :] // N_HEAD) % 2 == b) & (head[:, None] == dst[None, :] % N_HEAD)).astype(BF16)
            for b in range(2)]


def _from_key_rows(g_f, g_b, bsz, seq, *, name):
    assert bsz == 2
    maps = jnp.stack([m.T for m in _key_row_maps()])
    tt = min(RELAYOUT_TILE, seq)
    per_seq = seq // tt

    def kern(gf_ref, gb_ref, q_ref, o_ref):
        g = jnp.concatenate([gf_ref[:, r, :] + gb_ref[:, r, :] for r in range(K_HI)], axis=1)
        total = None
        for piece in _bf16_pieces(g):
            term = jnp.dot(piece, q_ref[0], preferred_element_type=F32)
            total = term if total is None else total + term
        o_ref[...] = total

    g_spec = pl.BlockSpec((tt, K_HI, LANES), lambda b, i, a: (i, a, 0))
    return pl.pallas_call(
        kern, out_shape=jax.ShapeDtypeStruct((bsz * seq, N_GROUP * D_RWKV), F32), grid=(bsz, per_seq, N_GROUP),
        in_specs=[g_spec, g_spec, pl.BlockSpec((1, K_HI * LANES, D_RWKV), lambda b, i, a: (b, 0, 0))],
        out_specs=pl.BlockSpec((tt, D_RWKV), lambda b, i, a: (b * per_seq + i, a)),
        compiler_params=_params(("parallel", "parallel", "parallel")), name=name)(g_f, g_b, maps)


def _to_value_rows(a, bsz, seq):
    z = a.reshape(bsz, seq, N_HEAD, V_HI, SUBLANES).transpose(1, 4, 3, 0, 2)
    return z.reshape(seq, SUBLANES, LANES)


def _from_value_rows(y, bsz, seq):
    z = y.reshape(seq, SUBLANES, V_HI, bsz, N_HEAD).transpose(3, 0, 4, 2, 1)
    return z.reshape(bsz * seq, D_RWKV)


def _pad_cols(a, segs):
    out, off = [], 0
    for w, wp in segs:
        out.append(a[..., off:off + w])
        if wp > w:
            out.append(jnp.zeros(a.shape[:-1] + (wp - w,), a.dtype))
        off += w
    return jnp.concatenate(out, axis=-1)


def _unpad_cols(a, segs):
    out, off = [], 0
    for w, wp in segs:
        out.append(a[..., off:off + w])
        off += wp
    return jnp.concatenate(out, axis=-1)


P_SEGS = ((3 * D_RWKV, 3 * D_RWKV), (D_LORA, 128), (D_LORA, 128), (D_GATE, 256), (3 * D_CONV, 3 * D_CONV))
S_SEGS = P_SEGS[:4]


def _pad_rows(a, rows):
    return jnp.concatenate([a, jnp.zeros((rows - a.shape[0], a.shape[1]), a.dtype)], axis=0)


LATE = ("w_out", "w_gate", "w_up", "w_down")


def _local_step(x, target, w, late=None):
    bsz, seq, _ = x.shape
    t = bsz * seq
    x2d = x.reshape(t, D_MODEL)
    tg2d = target.reshape(t, D_MODEL)
    row = lambda a: a.reshape(1, -1).astype(F32)

    w_in = _pad_cols(w["w_in"][0], P_SEGS)
    mu = _pad_cols(row(w["mu_shift"]), S_SEGS)
    wupf, wupb, aupf, aupb = (_pad_rows(w[n][0].astype(F32), 128) for n in ("w_up_f", "w_up_b", "a_up_f", "a_up_b"))
    gup = _pad_rows(w["g_up"][0].astype(F32), 256)
    conv_w = _pad_rows(w["conv_w"][0].astype(F32), SUBLANES)
    norm1, norm2, normf = row(w["norm1_w"]), row(w["norm2_w"]), row(w["norm_f_w"])
    vec = {n: row(w[n]) for n in VEC}
    head_of = jnp.arange(LANES) // HEAD
    bd = (head_of[:, None] == head_of[None, :]).astype(F32)
    pre_consts = [vec["k_k"], vec["w0_f"], vec["w0_b"], vec["a0_f"], vec["a0_b"], vec["k_a_f"], vec["k_a_b"],
                  wupf, wupb, aupf, aupb, gup, bd]
    post_consts = [vec["gn_w"], vec["gn_b"], vec["r_k_f"], vec["r_k_b"], bd]

    h1, = _rowwise(_rms, [x2d], [norm1], [D_MODEL], [], name="rms1_fwd", out_dtype=BF16, tb=WIDE_TILE)
    p = _mm(h1, w_in, name="mm_in")
    pss, oconv = _shift_conv_fwd(p, mu, conv_w, seq, name="shift_conv_fwd")
    pre_rows = [(pss, 0, 512), (pss, 1, 512), (pss, XW0 // 128, 128), (pss, XA0 // 128, 128), (pss, XG0 // 256, 256)]
    sc, g = _rowwise(_prescan_math, pre_rows, pre_consts, [[D_RWKV] * N_GROUP, D_RWKV], [], name="prescan_fwd")
    xall = _to_key_rows(sc, bsz, seq, name="to_key_rows")
    v_l = _to_value_rows(pss[:, 2 * D_RWKV:3 * D_RWKV], bsz, seq)
    y_f, y_b, hist, fin, sa, *gathered = _scan_fwd(xall, v_l, gather=[late[n] for n in LATE] if late else (),
                                                   name="scan_fwd")
    w_out, w_gate, w_up, w_down = (
        (_from_slots(a, SHARD_AXIS[n]) if late else w[n])[0] for n, a in zip(LATE, gathered or LATE))
    y = _from_value_rows(y_f + y_b, bsz, seq)
    post_rows = [y, (pss, 0, 512), (pss, 2, 512), (sc, G_KD[0], 512), (sc, G_KD[1], 512), g]

    def post_fwd(y_, r_, v_, kdf_, kdb_, g_, oc_, *consts):
        return _postscan_math(y_, r_, v_, kdf_, kdb_, g_, *consts), oc_

    o, = _rowwise(post_fwd, post_rows + [oconv], post_consts, [[D_RWKV, D_CONV]], [], name="postscan_fwd",
                  out_dtype=BF16)
    x1 = _mm(o, w_out, add=x2d, name="mm_out")
    h2, = _rowwise(_rms, [x1], [norm2], [D_MODEL], [], name="rms2_fwd", out_dtype=BF16, tb=WIDE_TILE)
    gg, uu, ff = _mm_swiglu(h2, w_gate, w_up, name="mm_gate_up")
    x2 = _mm(ff, w_down, add=x1, name="mm_down")

    def final(x_, tg_, wn_):
        yo, vjp = jax.vjp(_rms, x_, wn_)
        err = yo - tg_
        dx_, dwn_ = vjp(err * (1.0 / D_MODEL))
        part = jnp.sum(jnp.sum(err * err, axis=1, keepdims=True), axis=0, keepdims=True) * (0.5 / D_MODEL)
        return dx_, part + jnp.zeros((1, LANES), F32), dwn_

    dx2, loss_acc, d_normf = _rowwise(final, [x2, tg2d], [normf], [D_MODEL], [(1, LANES), (1, D_MODEL)],
                                      name="loss_head", tb=WIDE_TILE)
    dgg, duu = _mm_swiglu_bwd(dx2, w_down, gg, uu, name="mm_down_dx")
    g_w_down = _mm(ff, dx2, ta=True, name="mm_down_dw")
    dh2 = _mm(dgg, w_gate, tb=True, name="mm_gate_dx")
    dh2 = _mm(duu, w_up, tb=True, add=dh2, name="mm_up_dx")
    g_w_gate = _mm(h2, dgg, ta=True, name="mm_gate_dw")
    g_w_up = _mm(h2, duu, ta=True, name="mm_up_dw")

    def rms_bwd(x_, dh_, dres_, wn_):
        _, vjp = jax.vjp(_rms, x_, wn_)
        dx_, dwn_ = vjp(dh_)
        return dx_ + dres_, dwn_

    dx1, d_norm2 = _rowwise(rms_bwd, [x1, dh2, dx2], [norm2], [D_MODEL], [(1, D_MODEL)], name="rms2_bwd", tb=WIDE_TILE)
    do = _mm(dx1, w_out, tb=True, name="mm_out_dx")
    g_w_out = _mm(o, dx1, ta=True, name="mm_out_dw")

    def post_bwd(y_, r_, v_, kdf_, kdb_, g_, do_, *consts):
        _, vjp = jax.vjp(lambda *a: _postscan_math(*a, consts[4]), y_, r_, v_, kdf_, kdb_, g_, *consts[:4])
        return vjp(do_)

    (dy, dr_c, dv_c, dkdf_c, dkdb_c, dg, d_gn_w, d_gn_b, d_rkf, d_rkb) = _rowwise(
        post_bwd, post_rows + [(do, 0, 512)], post_consts, [D_RWKV] * 6, [(1, D_RWKV)] * 4, name="postscan_bwd")
    dy_l = _to_value_rows(dy, bsz, seq)
    late_grads = {"w_out": g_w_out[None], "w_gate": g_w_gate[None], "w_up": g_w_up[None], "w_down": g_w_down[None]}
    g_f, g_b, dv_f, dv_b, *late_parts = _scan_bwd(
        xall, v_l, dy_l, hist, fin, sa, name="scan_bwd",
        exchange=[_to_slots(late_grads[n], SHARD_AXIS[n]).astype(BF16) for n in LATE] if late else ())
    dsc = _from_key_rows(g_f, g_b, bsz, seq, name="from_key_rows")
    dv_s = _from_value_rows(dv_f + dv_b, bsz, seq)

    def pre_bwd(r_, k_, xw_, xa_, xg_, dkk_, dr_s, dwf_, dwb_, dbf_, dbb_, dkdf_s, dkdb_s,
                dr_c_, dv_c_, dv_s_, dkdf_c_, dkdb_c_, dg_, *consts):
        _, vjp = jax.vjp(lambda *a: _prescan_math(*a, consts[-1]), r_, k_, xw_, xa_, xg_, *consts[:-1])
        grads = vjp((dkk_, dr_s + dr_c_, dwf_, dwb_, dbf_, dbb_, dkdf_s + dkdf_c_, dkdb_s + dkdb_c_, dg_))
        dr_, dk_, dxw_, dxa_, dxg_ = grads[:5]
        return (dr_, dk_, dv_c_ + dv_s_, dxw_, dxa_, dxg_) + tuple(grads[5:])

    pre_b_rows = (pre_rows + [(dsc, j, 512) for j in range(N_GROUP)]
                  + [dr_c, dv_c, dv_s, dkdf_c, dkdb_c, dg])
    pre_b = _rowwise(pre_bwd, pre_b_rows, pre_consts, [[512, 512, 512, 128, 128, 256]],
                     [(1, D_RWKV)] * 7 + [(128, D_RWKV)] * 4 + [(256, D_RWKV)], name="prescan_bwd")
    d_pss = pre_b[0]
    d_kk_, d_w0f, d_w0b, d_a0f, d_a0b, d_kaf, d_kab, d_wupf, d_wupb, d_aupf, d_aupb, d_gup = pre_b[1:]
    dp, d_mu, d_conv = _shift_conv_bwd(p, d_pss, do, mu, conv_w, seq, name="shift_conv_bwd")
    g_w_in = _mm(h1, dp, ta=True, name="mm_in_dw")
    grads = {
        "w_in": _unpad_cols(g_w_in, P_SEGS)[None], "mu_shift": _unpad_cols(d_mu, S_SEGS),
        "w_up_f": d_wupf[None, :D_LORA], "w0_f": d_w0f, "w_up_b": d_wupb[None, :D_LORA], "w0_b": d_w0b,
        "a_up_f": d_aupf[None, :D_LORA], "a0_f": d_a0f, "a_up_b": d_aupb[None, :D_LORA], "a0_b": d_a0b,
        "g_up": d_gup[None, :D_GATE], "k_k": d_kk_, "k_a_f": d_kaf, "k_a_b": d_kab,
        "r_k_f": d_rkf, "r_k_b": d_rkb, "gn_w": d_gn_w, "gn_b": d_gn_b, "conv_w": d_conv[None, :3],
        "w_out": g_w_out[None], "norm2_w": d_norm2, "w_gate": g_w_gate[None], "w_up": g_w_up[None],
        "w_down": g_w_down[None], "norm_f_w": d_normf,
    }
    early = ("w_in",) + LORA
    parts = dict(zip(LATE, late_parts))
    if late:
        vec_rows = jnp.concatenate([grads[n] for n in VEC] + [jnp.zeros((16 - len(VEC), D_RWKV), F32)], axis=0)
        slots = [_to_slots(grads[n], SHARD_AXIS[n]).astype(BF16 if n in BIG else F32) for n in early]
        dh1, *recv = _mm(dp, w_in, tb=True, exchange=(slots, [vec_rows]), name="mm_in_dx")
        parts.update(zip(early + ("vec",), recv))
    else:
        dh1 = _mm(dp, w_in, tb=True, name="mm_in_dx")
    dx, grads["norm1_w"] = _rowwise(rms_bwd, [x2d, dh1, dx1], [norm1], [D_MODEL], [(1, D_MODEL)], name="rms1_bwd",
                                    tb=WIDE_TILE)
    return loss_acc, dx.reshape(bsz, seq, D_MODEL), grads, parts


def _hbm_specs(n):
    return [pl.BlockSpec(memory_space=pl.ANY)] * n


def _all_gather(arrs, *, name):
    n = len(arrs)

    def body(*refs):
        x_refs, out_refs = refs[:n], refs[n:2 * n]
        send_sems, recv_sems, local_sems = refs[2 * n:]
        x, y, c = lax.axis_index("x"), lax.axis_index("y"), lax.axis_index("c")
        me, sibling = (x, y, c), (x, y, 1 - c)
        chips = [(1 - x, y), (x, 1 - y), (1 - x, 1 - y)]

        def slot(a, px, py, pc):
            return out_refs[a].at[4 * px + 2 * py + pc]

        def copy(a, k, block, to, src=None):
            return pltpu.make_async_remote_copy(
                src_ref=slot(a, *block) if src is None else src, dst_ref=slot(a, *block),
                send_sem=send_sems.at[k, a], recv_sem=recv_sems.at[k, a],
                device_id=to, device_id_type=pl.DeviceIdType.MESH)

        mine = [pltpu.make_async_copy(x_refs[a], slot(a, *me), local_sems.at[a]) for a in range(n)]
        for cp in mine:
            cp.start()
        first = []
        for a in range(n):
            first.append(copy(a, 0, me, sibling, src=x_refs[a]))
            first += [copy(a, 1 + j, me, (*chip, c), src=x_refs[a]) for j, chip in enumerate(chips)]
        for cp in first:
            cp.start()
        passed = []
        for j, chip in enumerate(chips):
            for a in range(n):
                copy(a, 1 + j, (*chip, c), me).wait_recv()
                cp = copy(a, 4 + j, (*chip, c), sibling)
                cp.start()
                passed.append(cp)
        for a in range(n):
            copy(a, 0, sibling, me).wait_recv()
            for j, chip in enumerate(chips):
                copy(a, 4 + j, (*chip, 1 - c), me).wait_recv()
        for cp in first + passed:
            cp.wait_send()
        for cp in mine:
            cp.wait()

    return pl.pallas_call(
        body, out_shape=[jax.ShapeDtypeStruct((N_DEV,) + a.shape, a.dtype) for a in arrs],
        in_specs=_hbm_specs(n), out_specs=_hbm_specs(n),
        scratch_shapes=[pltpu.SemaphoreType.DMA((7, n)), pltpu.SemaphoreType.DMA((7, n)),
                        pltpu.SemaphoreType.DMA((n,))],
        name=name)(*arrs)


def _exchange(sliced, whole, *, name):
    arrs = list(sliced) + list(whole)
    n, n_sliced = len(arrs), len(sliced)

    def body(*refs):
        copies = _exchange_copies(refs[:n], refs[n:2 * n], n_sliced, *refs[2 * n:])
        for cp in copies:
            cp.start()
        for cp in copies:
            cp.wait()

    return pl.pallas_call(
        body, out_shape=_exchange_out_shapes(arrs, n_sliced), in_specs=_hbm_specs(n), out_specs=_hbm_specs(n),
        scratch_shapes=_exchange_sems(n), name=name)(*arrs)


def _exchange_out_shapes(arrs, n_sliced):
    return [jax.ShapeDtypeStruct(a.shape if i < n_sliced else (N_DEV,) + a.shape, a.dtype)
            for i, a in enumerate(arrs)]


def _exchange_sems(n):
    return [pltpu.SemaphoreType.DMA((7, n)), pltpu.SemaphoreType.DMA((7, n)), pltpu.SemaphoreType.DMA((n,))]


def _exchange_copies(in_refs, out_refs, n_sliced, send_sems, recv_sems, local_sems):
    n = len(in_refs)
    x, y, c = lax.axis_index("x"), lax.axis_index("y"), lax.axis_index("c")
    me = 4 * x + 2 * y + c

    def src(a, dev):
        return in_refs[a].at[dev] if a < n_sliced else in_refs[a]

    copies = [pltpu.make_async_copy(src(a, me), out_refs[a].at[me], local_sems.at[a]) for a in range(n)]
    for k in range(1, N_DEV):
        px = 1 - x if k & 4 else x
        py = 1 - y if k & 2 else y
        pc = 1 - c if k & 1 else c
        for a in range(n):
            copies.append(pltpu.make_async_remote_copy(
                src_ref=src(a, 4 * px + 2 * py + pc), dst_ref=out_refs[a].at[me],
                send_sem=send_sems.at[k - 1, a], recv_sem=recv_sems.at[k - 1, a],
                device_id=(px, py, pc), device_id_type=pl.DeviceIdType.MESH))
    return copies


def _adam_math(g, w, m, v):
    nm = ADAM_B1 * m + (1.0 - ADAM_B1) * g
    nv = ADAM_B2 * v + (1.0 - ADAM_B2) * (g * g)
    m_hat = nm / (1.0 - ADAM_B1 ** ADAM_STEP)
    v_hat = nv / (1.0 - ADAM_B2 ** ADAM_STEP)
    return -ADAM_LR * (m_hat / (jnp.sqrt(v_hat) + ADAM_EPS) + ADAM_WD * w), nm, nv


def _slot_sum(ref):
    g = ref[0].astype(F32)
    for s in range(1, N_DEV):
        g = g + ref[s].astype(F32)
    return g


def _adamw_big(parts, w, m, v, *, name):
    _, rws, cols = w.shape
    tr = _tile(rws, (256, 176, 128))

    def kern(p_ref, w_ref, m_ref, v_ref, g_ref, d_ref, nm_ref, nv_ref):
        g = _slot_sum(p_ref)
        g_ref[...] = g
        d_ref[...], nm_ref[...], nv_ref[...] = _adam_math(g, w_ref[...], m_ref[...], v_ref[...])

    spec = pl.BlockSpec((1, tr, cols), lambda i: (0, i, 0))
    return pl.pallas_call(
        kern, out_shape=[jax.ShapeDtypeStruct(w.shape, F32)] * 4, grid=(rws // tr,),
        in_specs=[pl.BlockSpec((N_DEV, 1, tr, cols), lambda i: (0, 0, i, 0)), spec, spec, spec],
        out_specs=[spec] * 4, compiler_params=_params(("parallel",)), name=name)(parts, w, m, v)


def _adamw_small(lora_parts, vec_parts, wide_parts, wmv, *, name):
    names = LORA + VEC + WIDE
    n_l, n = len(LORA), len(names)
    flat = [a for trip in wmv for a in trip]

    def kern(*refs):
        l_refs, vec_ref, wide_ref = refs[:n_l], refs[n_l], refs[n_l + 1]
        in_refs = refs[n_l + 2:n_l + 2 + 3 * n]
        out_refs = refs[n_l + 2 + 3 * n:]
        vec_sum, wide_sum = _slot_sum(vec_ref), _slot_sum(wide_ref)
        for i, nm in enumerate(names):
            w_ref, m_ref, v_ref = in_refs[3 * i:3 * i + 3]
            if i < n_l:
                g = _slot_sum(l_refs[i])
            elif nm in VEC:
                g = vec_sum[i - n_l:i - n_l + 1, :]
            else:
                g = wide_sum[WIDE.index(nm):WIDE.index(nm) + 1, :w_ref.shape[-1]]
            o = out_refs[4 * i:4 * i + 4]
            o[0][...] = g
            o[1][...], o[2][...], o[3][...] = _adam_math(g, w_ref[...], m_ref[...], v_ref[...])

    out_shape = [jax.ShapeDtypeStruct(trip[0].shape, F32) for trip in wmv for _ in range(4)]
    outs = pl.pallas_call(kern, out_shape=out_shape, name=name,
                          compiler_params=pltpu.CompilerParams(vmem_limit_bytes=VMEM_LIMIT))(
        *lora_parts, vec_parts, wide_parts, *flat)
    return [tuple(outs[4 * i:4 * i + 4]) for i in range(n)]


def _to_slots(g, axis):
    _, rws, cols = g.shape
    if axis == 1:
        return g.reshape(N_DEV, 1, rws // N_DEV, cols)
    return g.reshape(1, rws, N_DEV, cols // N_DEV).transpose(2, 0, 1, 3)


def _from_slots(got, axis):
    _, _, rws, cols = got.shape
    if axis == 1:
        return got.reshape(1, N_DEV * rws, cols)
    return got.transpose(1, 2, 0, 3).reshape(1, rws, N_DEV * cols)


def _pad_lanes(a, width):
    return jnp.concatenate([a, jnp.zeros(a.shape[:-1] + (width - a.shape[-1],), a.dtype)], axis=-1)


def kernel(x, norm1_w, w_in, mu_shift, w_up_f, w0_f, w_up_b, w0_b, a_up_f, a0_f, a_up_b, a0_b, g_up, k_k, k_a_f, k_a_b, r_k_f, r_k_b, gn_w, gn_b, conv_w, w_out, norm2_w, w_gate, w_up, w_down, norm_f_w, loss_target, m_norm1_w, m_w_in, m_mu_shift, m_w_up_f, m_w0_f, m_w_up_b, m_w0_b, m_a_up_f, m_a0_f, m_a_up_b, m_a0_b, m_g_up, m_k_k, m_k_a_f, m_k_a_b, m_r_k_f, m_r_k_b, m_gn_w, m_gn_b, m_conv_w, m_w_out, m_norm2_w, m_w_gate, m_w_up, m_w_down, m_norm_f_w, v_norm1_w, v_w_in, v_mu_shift, v_w_up_f, v_w0_f, v_w_up_b, v_w0_b, v_a_up_f, v_a0_f, v_a_up_b, v_a0_b, v_g_up, v_k_k, v_k_a_f, v_k_a_b, v_r_k_f, v_r_k_b, v_gn_w, v_gn_b, v_conv_w, v_w_out, v_norm2_w, v_w_gate, v_w_up, v_w_down, v_norm_f_w):
    local = dict(norm1_w=norm1_w, w_in=w_in, mu_shift=mu_shift, w_up_f=w_up_f, w0_f=w0_f, w_up_b=w_up_b,
                 w0_b=w0_b, a_up_f=a_up_f, a0_f=a0_f, a_up_b=a_up_b, a0_b=a0_b, g_up=g_up, k_k=k_k, k_a_f=k_a_f,
                 k_a_b=k_a_b, r_k_f=r_k_f, r_k_b=r_k_b, gn_w=gn_w, gn_b=gn_b, conv_w=conv_w, w_out=w_out,
                 norm2_w=norm2_w, w_gate=w_gate, w_up=w_up, w_down=w_down, norm_f_w=norm_f_w)
    mom_m = dict(norm1_w=m_norm1_w, w_in=m_w_in, mu_shift=m_mu_shift, w_up_f=m_w_up_f, w0_f=m_w0_f,
                 w_up_b=m_w_up_b, w0_b=m_w0_b, a_up_f=m_a_up_f, a0_f=m_a0_f, a_up_b=m_a_up_b, a0_b=m_a0_b,
                 g_up=m_g_up, k_k=m_k_k, k_a_f=m_k_a_f, k_a_b=m_k_a_b, r_k_f=m_r_k_f, r_k_b=m_r_k_b,
                 gn_w=m_gn_w, gn_b=m_gn_b, conv_w=m_conv_w, w_out=m_w_out, norm2_w=m_norm2_w, w_gate=m_w_gate,
                 w_up=m_w_up, w_down=m_w_down, norm_f_w=m_norm_f_w)
    mom_v = dict(norm1_w=v_norm1_w, w_in=v_w_in, mu_shift=v_mu_shift, w_up_f=v_w_up_f, w0_f=v_w0_f,
                 w_up_b=v_w_up_b, w0_b=v_w0_b, a_up_f=v_a_up_f, a0_f=v_a0_f, a_up_b=v_a_up_b, a0_b=v_a0_b,
                 g_up=v_g_up, k_k=v_k_k, k_a_f=v_k_a_f, k_a_b=v_k_a_b, r_k_f=v_r_k_f, r_k_b=v_r_k_b,
                 gn_w=v_gn_w, gn_b=v_gn_b, conv_w=v_conv_w, w_out=v_w_out, norm2_w=v_norm2_w, w_gate=v_w_gate,
                 w_up=v_w_up, w_down=v_w_down, norm_f_w=v_norm_f_w)

    early = ("w_in",) + LORA
    got = _all_gather([local["w_in"].astype(BF16)] + [local[n] for n in LORA], name="gather")
    full = dict(local)
    full.update({n: _from_slots(a, SHARD_AXIS[n]) for n, a in zip(early, got)})

    loss_part, grad_x, grads, parts = _local_step(x, loss_target, full,
                                                  late={n: local[n].astype(BF16) for n in LATE})

    wide_rows = jnp.concatenate([_pad_lanes(a, WIDE_ROW) for a in [grads[n] for n in WIDE] + [loss_part]]
                                + [jnp.zeros((SUBLANES - len(WIDE) - 1, WIDE_ROW), F32)], axis=0)
    wide_parts, = _exchange([], [wide_rows], name="grad_exchange")
    loss = jnp.sum(wide_parts[:, len(WIDE), 0])
    out = {}
    for n in BIG:
        out[n] = _adamw_big(parts[n], local[n], mom_m[n], mom_v[n], name="adamw_" + n)

    def small_form(n, a):
        if n in LORA:
            return a
        a = a.reshape(1, -1)
        return _pad_lanes(a, WIDE_ROW) if n == "mu_shift" else a

    small = LORA + VEC + WIDE
    res = _adamw_small([parts[n] for n in LORA], parts["vec"], wide_parts,
                       [tuple(small_form(n, d[n]) for d in (local, mom_m, mom_v)) for n in small],
                       name="adamw_small")
    for n, quad in zip(small, res):
        out[n] = tuple(a[..., :local[n].size].reshape(local[n].shape) if n not in LORA else a for a in quad)
    return (loss, grad_x, *[out[n][i] for i in range(4) for n in WEIGHTS])
```

```python
import functools

import jax
import jax.numpy as jnp
from jax import lax
from jax.experimental import pallas as pl
from jax.experimental.pallas import tpu as pltpu

F32 = jnp.float32
BF16 = jnp.bfloat16
HIGHEST = lax.Precision.HIGHEST

N_DEV = 8
D_MODEL = 1024
D_RWKV = 512
D_CONV = 512
HEAD = 64
N_HEAD = D_RWKV // HEAD
D_LORA = 64
D_GATE = 160
D_SHIFTED = 3 * D_RWKV + 2 * D_LORA + D_GATE
XW0, XA0, XG0 = 1536, 1664, 1792
D_SP = 2048
D_INP = D_SP + 3 * D_CONV
LOG_DECAY_SCALE = 0.606531
RMS_EPS = 1e-6
GN_EPS = 64e-5
NORM_EPS = 1e-12
ADAM_LR, ADAM_B1, ADAM_B2, ADAM_EPS, ADAM_WD, ADAM_STEP = 0.001, 0.9, 0.999, 1e-08, 0.01, 10

LANES = 128
SUBLANES = 8
VMEM_LIMIT = 48 * 1024 * 1024
SCAN_CHUNK = 32
SCAN_VMEM_LIMIT = 58 * 1024 * 1024
SCAN_UNROLL = 3
ROW_TILE = 128
WIDE_TILE = 256
RELAYOUT_TILE = 512

BIG = ("w_in", "w_out", "w_gate", "w_up", "w_down")
LORA = ("w_up_f", "w_up_b", "a_up_f", "a_up_b", "g_up", "conv_w")
SHARD_AXIS = {"w_in": 2, "w_out": 1, "w_gate": 2, "w_up": 2, "w_down": 1, "w_up_f": 2, "w_up_b": 2,
              "a_up_f": 2, "a_up_b": 2, "g_up": 2, "conv_w": 2}
VEC = ("w0_f", "w0_b", "a0_f", "a0_b", "k_k", "k_a_f", "k_a_b", "r_k_f", "r_k_b", "gn_w", "gn_b")
WIDE = ("mu_shift", "norm1_w", "norm2_w", "norm_f_w")
WIDE_ROW = 2048
WEIGHTS = ("norm1_w", "w_in", "mu_shift", "w_up_f", "w0_f", "w_up_b", "w0_b", "a_up_f", "a0_f", "a_up_b",
           "a0_b", "g_up", "k_k", "k_a_f", "k_a_b", "r_k_f", "r_k_b", "gn_w", "gn_b", "conv_w", "w_out",
           "norm2_w", "w_gate", "w_up", "w_down", "norm_f_w")


def _params(sem, limit=VMEM_LIMIT):
    return pltpu.CompilerParams(dimension_semantics=sem, vmem_limit_bytes=limit)


def _tile(n, cands):
    for c in cands:
        if n % c == 0:
            return c
    raise ValueError(f"no tile for {n}")


def _mm(a, b, *, ta=False, tb=False, add=None, exchange=None, name):
    (k_dim, m) = a.shape if ta else a.shape[::-1]
    (k2, n) = b.shape[::-1] if tb else b.shape
    assert k_dim == k2, (a.shape, b.shape, ta, tb)
    tm = _tile(m, (1408, 1024, 512, 256, 128))
    tn = _tile(n, (1408, 1024, 896, 512, 256, 128))
    tk = _tile(k_dim, (1408, 1024, 896, 512, 256, 128))
    nk = k_dim // tk
    grid = (m // tm, n // tn, nk)
    dims = (((0 if ta else 1,), (1 if tb else 0,)), ((), ()))
    sliced, whole = exchange or ((), ())
    riders = list(sliced) + list(whole)
    n_x, n_in = len(riders), 2 + (add is not None)

    def kern(*refs):
        a_ref, b_ref = refs[:2]
        add_ref = refs[2] if add is not None else None
        o_ref, acc_ref = refs[n_in + n_x], refs[n_in + 2 * n_x + 1]
        k = pl.program_id(2)
        step = (pl.program_id(0) * grid[1] + pl.program_id(1)) * nk + k

        def copies():
            return _exchange_copies(refs[n_in:n_in + n_x], refs[n_in + n_x + 1:n_in + 2 * n_x + 1], len(sliced),
                                    *refs[n_in + 2 * n_x + 2:])

        if n_x:
            @pl.when(step == 0)
            def _():
                for cp in copies():
                    cp.start()

        @pl.when(k == 0)
        def _():
            acc_ref[...] = jnp.zeros_like(acc_ref)

        acc_ref[...] += lax.dot_general(a_ref[...].astype(BF16), b_ref[...].astype(BF16), dims,
                                        preferred_element_type=F32)

        @pl.when(k == nk - 1)
        def _():
            if add is None:
                o_ref[...] = acc_ref[...]
            else:
                o_ref[...] = acc_ref[...] + add_ref[...]

        if n_x:
            @pl.when(step == grid[0] * grid[1] * nk - 1)
            def _():
                for cp in copies():
                    cp.wait()

    a_spec = (pl.BlockSpec((tk, tm), lambda i, j, k: (k, i)) if ta
              else pl.BlockSpec((tm, tk), lambda i, j, k: (i, k)))
    b_spec = (pl.BlockSpec((tn, tk), lambda i, j, k: (j, k)) if tb
              else pl.BlockSpec((tk, tn), lambda i, j, k: (k, j)))
    o_spec = pl.BlockSpec((tm, tn), lambda i, j, k: (i, j))
    in_specs = [a_spec, b_spec] + ([o_spec] if add is not None else []) + _hbm_specs(n_x)
    args = (a, b) + ((add,) if add is not None else ()) + tuple(riders)
    out = pl.pallas_call(
        kern, out_shape=[jax.ShapeDtypeStruct((m, n), F32)] + _exchange_out_shapes(riders, len(sliced)), grid=grid,
        in_specs=in_specs, out_specs=[o_spec] + _hbm_specs(n_x),
        scratch_shapes=[pltpu.VMEM((tm, tn), F32)] + (_exchange_sems(n_x) if n_x else []),
        compiler_params=_params(("arbitrary",) * 3 if n_x else ("parallel", "parallel", "arbitrary")),
        name=name)(*args)
    return out if n_x else out[0]


def _swiglu(g, u):
    return jax.nn.silu(g) * u


FFN_TN = 256


def _mm_swiglu(h, w_gate, w_up, *, name):
    m, k_dim = h.shape
    n = w_gate.shape[1]
    tm = _tile(m, (1024, 512, 256, 128))

    def kern(h_ref, wg_ref, wu_ref, g_ref, u_ref, f_ref):
        hv = h_ref[...].astype(BF16)
        g = jnp.dot(hv, wg_ref[...].astype(BF16), preferred_element_type=F32)
        u = jnp.dot(hv, wu_ref[...].astype(BF16), preferred_element_type=F32)
        g_ref[...] = g
        u_ref[...] = u
        f_ref[...] = _swiglu(g, u).astype(f_ref.dtype)

    w_spec = pl.BlockSpec((k_dim, FFN_TN), lambda i, j: (0, j))
    o_spec = pl.BlockSpec((tm, FFN_TN), lambda i, j: (i, j))
    return pl.pallas_call(
        kern, out_shape=[jax.ShapeDtypeStruct((m, n), F32)] * 2 + [jax.ShapeDtypeStruct((m, n), BF16)],
        grid=(m // tm, n // FFN_TN), in_specs=[pl.BlockSpec((tm, k_dim), lambda i, j: (i, 0)), w_spec, w_spec],
        out_specs=[o_spec] * 3, compiler_params=_params(("parallel", "parallel")), name=name)(h, w_gate, w_up)


def _mm_swiglu_bwd(dx, w_down, g, u, *, name):
    m, k_dim = dx.shape
    n = w_down.shape[0]
    tm = _tile(m, (1024, 512, 256, 128))

    def kern(dx_ref, w_ref, g_ref, u_ref, dg_ref, du_ref):
        df = lax.dot_general(dx_ref[...].astype(BF16), w_ref[...].astype(BF16), (((1,), (1,)), ((), ())),
                             preferred_element_type=F32)
        _, vjp = jax.vjp(_swiglu, g_ref[...], u_ref[...])
        dg, du = vjp(df)
        dg_ref[...] = dg.astype(dg_ref.dtype)
        du_ref[...] = du.astype(du_ref.dtype)

    o_spec = pl.BlockSpec((tm, FFN_TN), lambda i, j: (i, j))
    return pl.pallas_call(
        kern, out_shape=[jax.ShapeDtypeStruct((m, n), BF16)] * 2, grid=(m // tm, n // FFN_TN),
        in_specs=[pl.BlockSpec((tm, k_dim), lambda i, j: (i, 0)), pl.BlockSpec((FFN_TN, k_dim), lambda i, j: (j, 0)),
                  o_spec, o_spec],
        out_specs=[o_spec] * 2, compiler_params=_params(("parallel", "parallel")), name=name)(dx, w_down, g, u)


def _rowwise(fn, rows, consts, out_rows, out_accs, *, name, tb=ROW_TILE, out_dtype=F32):
    t = (rows[0][0] if isinstance(rows[0], tuple) else rows[0]).shape[0]
    n_r, n_c, n_o, n_a = len(rows), len(consts), len(out_rows), len(out_accs)
    pieces = [w if isinstance(w, (list, tuple)) else [w] for w in out_rows]

    def kern(*refs):
        r_refs = refs[:n_r]
        c_refs = refs[n_r:n_r + n_c]
        o_refs = refs[n_r + n_c:n_r + n_c + n_o]
        a_refs = refs[n_r + n_c + n_o:]
        vals = fn(*[r[...] for r in r_refs], *[c[...] for c in c_refs])
        vals = list(vals) if isinstance(vals, (tuple, list)) else [vals]
        pos = 0
        for o_ref, ws in zip(o_refs, pieces):
            off = 0
            for w in ws:
                o_ref[:, off:off + w] = vals[pos].astype(o_ref.dtype)
                off += w
                pos += 1
        if n_a:
            @pl.when(pl.program_id(0) == 0)
            def _():
                for a_ref in a_refs:
                    a_ref[...] = jnp.zeros_like(a_ref)
            for a_ref, v in zip(a_refs, vals[pos:]):
                a_ref[...] += v

    in_specs, args = [], []
    for r in rows:
        if isinstance(r, tuple):
            arr, blk, w = r
            in_specs.append(pl.BlockSpec((tb, w), functools.partial(lambda i, blk: (i, blk), blk=blk)))
        else:
            arr = r
            in_specs.append(pl.BlockSpec((tb, arr.shape[1]), lambda i: (i, 0)))
        args.append(arr)
    for c in consts:
        in_specs.append(pl.BlockSpec(c.shape, lambda i: (0, 0)))
        args.append(c)
    out_shape = [jax.ShapeDtypeStruct((t, sum(ws)), out_dtype) for ws in pieces]
    out_specs = [pl.BlockSpec((tb, sum(ws)), lambda i: (i, 0)) for ws in pieces]
    for shp in out_accs:
        out_shape.append(jax.ShapeDtypeStruct(shp, F32))
        out_specs.append(pl.BlockSpec(shp, lambda i: (0, 0)))
    res = pl.pallas_call(
        kern, out_shape=out_shape, grid=(t // tb,), in_specs=in_specs, out_specs=out_specs,
        compiler_params=_params(("arbitrary",) if n_a else ("parallel",)), name=name)(*args)
    return res


def _rms(x, w):
    return x * lax.rsqrt(jnp.mean(x * x, axis=-1, keepdims=True) + RMS_EPS) * w


def _seg_sum(x, bd):
    return jnp.concatenate(
        [jnp.dot(x[:, LANES * j:LANES * (j + 1)], bd, precision=HIGHEST, preferred_element_type=F32)
         for j in range(x.shape[1] // LANES)], axis=1)


@jax.custom_vjp
def _seg(x, bd):
    return _seg_sum(x, bd)


_seg.defvjp(lambda x, bd: (_seg_sum(x, bd), bd), lambda bd, ct: (_seg_sum(ct, bd), jnp.zeros_like(bd)))


def _colsum(x):
    return jnp.sum(x, axis=0, keepdims=True)


def _prescan_math(r, k, xw, xa, xg, k_k, w0f, w0b, a0f, a0b, kaf, kab, wupf, wupb, aupf, aupb, gup, bd):
    kkr = k * k_k
    norm = jnp.sqrt(_seg(kkr * kkr, bd))
    kk = kkr / jnp.maximum(norm, NORM_EPS)
    th = jnp.tanh(xw)

    def direction(w0, wup, a0, aup, ka):
        logit = w0 + jnp.dot(th, wup, preferred_element_type=F32)
        w = jnp.exp(-LOG_DECAY_SCALE * jax.nn.sigmoid(logit))
        a = jax.nn.sigmoid(a0 + jnp.dot(xa, aup, preferred_element_type=F32))
        kd = k * (1.0 + (a - 1.0) * ka)
        return w, kd, kk * a

    wf, kdf, bf = direction(w0f, wupf, a0f, aupf, kaf)
    wb, kdb, bb = direction(w0b, wupb, a0b, aupb, kab)
    g = jnp.dot(jax.nn.sigmoid(xg), gup, preferred_element_type=F32)
    return kk, r, wf, wb, bf, bb, kdf, kdb, g


def _postscan_math(y, r, v, kdf, kdb, g, gn_w, gn_b, rkf, rkb, bd):
    mean = _seg(y, bd) * (1.0 / HEAD)
    yc = y - mean
    var = _seg(yc * yc, bd) * (1.0 / HEAD)
    yg = yc * lax.rsqrt(var + GN_EPS) * gn_w + gn_b
    bonus = (_seg(r * kdf * rkf, bd) + _seg(r * kdb * rkb, bd)) * v
    return (yg + bonus) * g


def _halo_specs(width, col_blk, tb, t):
    nb = t // SUBLANES
    step = tb // SUBLANES
    main = pl.BlockSpec((tb, width), lambda i: (i, col_blk))
    prev = pl.BlockSpec((SUBLANES, width), lambda i: (jnp.maximum(i * step - 1, 0), col_blk))
    nxt = pl.BlockSpec((SUBLANES, width), lambda i: (jnp.minimum((i + 1) * step, nb - 1), col_blk))
    return [main, prev, nxt]


def _neighbours(z, prev8, next8, first, last):
    tb = z.shape[0]
    row = lax.broadcasted_iota(jnp.int32, z.shape, 0)
    prow = jnp.where(first, 0.0, prev8[SUBLANES - 1:SUBLANES, :])
    nrow = jnp.where(last, 0.0, next8[0:1, :])
    down = jnp.where(row == 0, prow, pltpu.roll(z, 1, 0))
    up = jnp.where(row == tb - 1, nrow, pltpu.roll(z, tb - 1, 0))
    return down, up


def _shift_conv_fwd(p, mu, conv_w, seq, *, name, tb=ROW_TILE):
    t = p.shape[0]
    per_seq = seq // tb

    def kern(p_ref, pp_ref, pn_ref, mu_ref, cw_ref, pss_ref, oc_ref):
        i = pl.program_id(0)
        first = (i % per_seq) == 0
        last = (i % per_seq) == per_seq - 1
        ps = p_ref[:, :D_SP]
        down, up = _neighbours(ps, pp_ref[:, :D_SP], pn_ref[:, :D_SP], first, last)
        pss_ref[...] = ps + mu_ref[...] * (0.5 * (down + up) - ps)
        gb = p_ref[:, D_SP:D_SP + D_CONV]
        u = p_ref[:, D_SP + D_CONV:D_SP + 2 * D_CONV] * p_ref[:, D_SP + 2 * D_CONV:]
        u_p = pp_ref[:, D_SP + D_CONV:D_SP + 2 * D_CONV] * pp_ref[:, D_SP + 2 * D_CONV:]
        u_n = pn_ref[:, D_SP + D_CONV:D_SP + 2 * D_CONV] * pn_ref[:, D_SP + 2 * D_CONV:]
        udown, uup = _neighbours(u, u_p, u_n, first, last)
        oc_ref[...] = gb * (cw_ref[0:1, :] * udown + cw_ref[1:2, :] * u + cw_ref[2:3, :] * uup)

    return pl.pallas_call(
        kern,
        out_shape=[jax.ShapeDtypeStruct((t, D_SP), F32), jax.ShapeDtypeStruct((t, D_CONV), F32)],
        grid=(t // tb,),
        in_specs=_halo_specs(D_INP, 0, tb, t) + [pl.BlockSpec((1, D_SP), lambda i: (0, 0)),
                                                 pl.BlockSpec((SUBLANES, D_CONV), lambda i: (0, 0))],
        out_specs=[pl.BlockSpec((tb, D_SP), lambda i: (i, 0)), pl.BlockSpec((tb, D_CONV), lambda i: (i, 0))],
        compiler_params=_params(("parallel",)), name=name)(p, p, p, mu, conv_w)


def _shift_conv_bwd(p, d_pss, d_o, mu, conv_w, seq, *, name, tb=ROW_TILE):
    t = p.shape[0]
    per_seq = seq // tb

    def kern(p_ref, pp_ref, pn_ref, d_ref, dp_ref, dn_ref, do_ref, dop_ref, don_ref, mu_ref, cw_ref,
             out_ref, dmu_ref, dcw_ref):
        i = pl.program_id(0)
        first = (i % per_seq) == 0
        last = (i % per_seq) == per_seq - 1

        @pl.when(i == 0)
        def _():
            dmu_ref[...] = jnp.zeros_like(dmu_ref)
            dcw_ref[...] = jnp.zeros_like(dcw_ref)

        mu_v = mu_ref[...]
        ps = p_ref[:, :D_SP]
        down, up = _neighbours(ps, pp_ref[:, :D_SP], pn_ref[:, :D_SP], first, last)
        d = d_ref[...]
        ddown, dup = _neighbours(d, dp_ref[...], dn_ref[...], first, last)
        out_ref[:, :D_SP] = (d - mu_v * d + 0.5 * (mu_v * ddown + mu_v * dup)).astype(out_ref.dtype)
        dmu_ref[...] += _colsum(d * (0.5 * (down + up) - ps))

        def parts(ref):
            return (ref[:, D_SP:D_SP + D_CONV], ref[:, D_SP + D_CONV:D_SP + 2 * D_CONV],
                    ref[:, D_SP + 2 * D_CONV:])

        gb, gc, hh = parts(p_ref)
        gb_p, gc_p, hh_p = parts(pp_ref)
        gb_n, gc_n, hh_n = parts(pn_ref)
        u = gc * hh
        udown, uup = _neighbours(u, gc_p * hh_p, gc_n * hh_n, first, last)
        cw0, cw1, cw2 = cw_ref[0:1, :], cw_ref[1:2, :], cw_ref[2:3, :]
        do = do_ref[...]
        duc = do * gb
        ducdown, ducup = _neighbours(duc, dop_ref[...] * gb_p, don_ref[...] * gb_n, first, last)
        du = cw0 * ducup + cw1 * duc + cw2 * ducdown
        out_ref[:, D_SP:D_SP + D_CONV] = (do * (cw0 * udown + cw1 * u + cw2 * uup)).astype(out_ref.dtype)
        out_ref[:, D_SP + D_CONV:D_SP + 2 * D_CONV] = (du * hh).astype(out_ref.dtype)
        out_ref[:, D_SP + 2 * D_CONV:] = (du * gc).astype(out_ref.dtype)
        dcw_ref[0:1, :] += _colsum(duc * udown)
        dcw_ref[1:2, :] += _colsum(duc * u)
        dcw_ref[2:3, :] += _colsum(duc * uup)

    return pl.pallas_call(
        kern,
        out_shape=[jax.ShapeDtypeStruct((t, D_INP), BF16), jax.ShapeDtypeStruct((1, D_SP), F32),
                   jax.ShapeDtypeStruct((SUBLANES, D_CONV), F32)],
        grid=(t // tb,),
        in_specs=(_halo_specs(D_INP, 0, tb, t) + _halo_specs(D_SP, 0, tb, t) + _halo_specs(D_CONV, 1, tb, t)
                  + [pl.BlockSpec((1, D_SP), lambda i: (0, 0)),
                     pl.BlockSpec((SUBLANES, D_CONV), lambda i: (0, 0))]),
        out_specs=[pl.BlockSpec((tb, D_INP), lambda i: (i, 0)), pl.BlockSpec((1, D_SP), lambda i: (0, 0)),
                   pl.BlockSpec((SUBLANES, D_CONV), lambda i: (0, 0))],
        compiler_params=_params(("arbitrary",)), name=name)(p, p, p, d_pss, d_pss, d_pss, d_o, d_o, d_o, mu, conv_w)


N_CHAIN = 16
N_GROUP = LANES // N_CHAIN
V_HI = HEAD // SUBLANES
G_KK, G_R, G_W, G_B, G_KD = 0, 1, (2, 3), (4, 5), (6, 7)


K_HI = HEAD // SUBLANES


def _tree_sum(terms):
    terms = list(terms)
    while len(terms) > 1:
        terms = [a + b for a, b in zip(terms[::2], terms[1::2])]
    return terms[0]


def _kscan_specs(nc):
    same = lambda c: c
    mirror = lambda c: nc - 1 - c

    def k_spec(fn):
        return pl.BlockSpec((SCAN_CHUNK, HEAD, LANES), lambda c: (fn(c), 0, 0))

    def v_spec(fn):
        return pl.BlockSpec((SCAN_CHUNK, SUBLANES, LANES), lambda c: (fn(c), 0, 0))

    return same, mirror, k_spec, v_spec


ST_SHAPE = (2, K_HI, V_HI, SUBLANES, LANES)


def _lane_group_index():
    lane = lax.broadcasted_iota(jnp.int32, (SUBLANES, LANES), 1)
    return lax.shift_right_logical(lane, jnp.full_like(lane, 4))


def _spread_groups(x, grp):
    rolled = [x] + [pltpu.roll(x, s * N_CHAIN, 1) for s in range(1, N_GROUP)]
    out = []
    for j in range(N_GROUP):
        t = rolled[(0 - j) % N_GROUP]
        for g in range(1, N_GROUP):
            t = jnp.where(grp == g, rolled[(g - j) % N_GROUP], t)
        out.append(t)
    return out


def _gather_groups(tiles, grp):
    total = None
    for s in range(N_GROUP):
        b = tiles[s % N_GROUP]
        for g in range(1, N_GROUP):
            b = jnp.where(grp == g, tiles[(g + s) % N_GROUP], b)
        b = pltpu.roll(b, s * N_CHAIN, 1) if s else b
        total = b if total is None else total + b
    return total


def _lane_group_sum(x):
    return _tree_sum([x] + [pltpu.roll(x, k * N_CHAIN, 1) for k in range(1, N_GROUP)])


def _key_row(x_t, grp, kh):
    r = SUBLANES * grp + kh
    return jnp.broadcast_to(x_t[r:r + 1, :], (SUBLANES, LANES))


def _acc(total, term):
    return term if total is None else total + term


SA_SHAPE = (2, V_HI, SUBLANES, LANES)


def _scan_fwd(xall, v_c, *, gather=(), name):
    steps = xall.shape[0]
    nc = steps // SCAN_CHUNK
    same, mirror, k_spec, v_spec = _kscan_specs(nc)
    last = SCAN_CHUNK - 1
    n_x = len(gather)

    def kern(*refs):
        xf_ref, xb_ref, vf_ref, vb_ref = refs[:4]
        yf_ref, yb_ref, hist_ref, fin_ref, sa_ref = refs[4 + n_x:9 + n_x]
        st_ref = refs[9 + 2 * n_x]
        c = pl.program_id(0)

        def riders():
            return _exchange_copies(refs[4:4 + n_x], refs[9 + n_x:9 + 2 * n_x], 0, *refs[10 + 2 * n_x:])

        @pl.when(c == 0)
        def _():
            st_ref[...] = jnp.zeros_like(st_ref)
            if n_x:
                for cp in riders():
                    cp.start()

        hist_ref[0] = st_ref[...]
        grp = _lane_group_index()

        def body(i, put):
            j = last - i
            for d, (x_t, v_t, y_ref, at) in enumerate(((xf_ref[i], vf_ref[i], yf_ref, i),
                                                       (xb_ref[j], vb_ref[j], yb_ref, j))):
                v_b = _spread_groups(v_t, grp)
                part = [None] * V_HI
                for kh in range(K_HI):
                    kk_r = _key_row(x_t, G_KK, kh)
                    for vh in range(V_HI):
                        part[vh] = _acc(part[vh], hist_ref[i, d, kh, vh] * kk_r)
                sa = [_lane_group_sum(p) for p in part]
                for vh in range(V_HI):
                    sa_ref[i, d, vh] = sa[vh]
                y_p = [None] * V_HI
                for kh in range(K_HI):
                    r_r, w_r = _key_row(x_t, G_R, kh), _key_row(x_t, G_W[d], kh)
                    b_r, kd_r = _key_row(x_t, G_B[d], kh), _key_row(x_t, G_KD[d], kh)
                    for vh in range(V_HI):
                        new = hist_ref[i, d, kh, vh] * w_r - sa[vh] * b_r + v_b[vh] * kd_r
                        put(d, kh, vh, new)
                        y_p[vh] = _acc(y_p[vh], new * r_r)
                y_ref[at] = _gather_groups(y_p, grp)

        def step(i, carry):
            def put(d, kh, vh, val):
                hist_ref[i + 1, d, kh, vh] = val
            body(i, put)
            return carry

        lax.fori_loop(0, last, step, 0, unroll=SCAN_UNROLL)

        def put_carry(d, kh, vh, val):
            st_ref[d, kh, vh] = val

        body(last, put_carry)

        @pl.when(c == nc - 1)
        def _():
            fin_ref[...] = st_ref[...]
            if n_x:
                for cp in riders():
                    cp.wait()

    return pl.pallas_call(
        kern,
        out_shape=[jax.ShapeDtypeStruct((steps, SUBLANES, LANES), F32)] * 2
        + [jax.ShapeDtypeStruct((steps,) + ST_SHAPE, F32), jax.ShapeDtypeStruct(ST_SHAPE, F32),
           jax.ShapeDtypeStruct((steps,) + SA_SHAPE, F32)]
        + _exchange_out_shapes(gather, 0),
        grid=(nc,), in_specs=[k_spec(same), k_spec(mirror), v_spec(same), v_spec(mirror)] + _hbm_specs(n_x),
        out_specs=[v_spec(same), v_spec(mirror),
                   pl.BlockSpec((SCAN_CHUNK,) + ST_SHAPE, lambda c: (c, 0, 0, 0, 0, 0)),
                   pl.BlockSpec(ST_SHAPE, lambda c: (0, 0, 0, 0, 0)),
                   pl.BlockSpec((SCAN_CHUNK,) + SA_SHAPE, lambda c: (c, 0, 0, 0, 0))] + _hbm_specs(n_x),
        scratch_shapes=[pltpu.VMEM(ST_SHAPE, F32)] + (_exchange_sems(n_x) if n_x else []),
        compiler_params=_params(("arbitrary",), SCAN_VMEM_LIMIT), name=name)(xall, xall, v_c, v_c, *gather)


def _scan_bwd(xall, v_c, dy_c, hist, fin, sa, *, exchange=(), name):
    steps = xall.shape[0]
    nc = steps // SCAN_CHUNK
    same, back, k_spec, v_spec = _kscan_specs(nc)
    last = SCAN_CHUNK - 1
    n_x = len(exchange)

    def kern(*refs):
        xf_ref, xb_ref, vf_ref, vb_ref, dyf_ref, dyb_ref, hist_ref, fin_ref, sa_ref = refs[:9]
        gf_ref, gb_ref, dvf_ref, dvb_ref = refs[9 + n_x:13 + n_x]
        ds_ref, after_ref = refs[13 + 2 * n_x:15 + 2 * n_x]
        c = pl.program_id(0)

        def riders():
            return _exchange_copies(refs[9:9 + n_x], refs[13 + n_x:13 + 2 * n_x], n_x, *refs[15 + 2 * n_x:])

        @pl.when(c == 0)
        def _():
            ds_ref[...] = jnp.zeros_like(ds_ref)
            after_ref[...] = fin_ref[...]
            if n_x:
                for cp in riders():
                    cp.start()

        grp = _lane_group_index()
        row = lax.broadcasted_iota(jnp.int32, (SUBLANES, LANES), 0)
        zero = jnp.zeros((SUBLANES, LANES), F32)

        def body(i, after):
            j = last - i
            for d, (x_t, v_t, dy_t, g_ref, dv_ref, at) in enumerate((
                    (xf_ref[i], vf_ref[i], dyf_ref[i], gf_ref, dvf_ref, i),
                    (xb_ref[j], vb_ref[j], dyb_ref[j], gb_ref, dvb_ref, j))):
                v_s, dy_s = _spread_groups(v_t, grp), _spread_groups(dy_t, grp)
                dsa_p, dv_p = [None] * V_HI, [None] * V_HI
                for kh in range(K_HI):
                    r_r = _key_row(x_t, G_R, kh)
                    b_r, kd_r = _key_row(x_t, G_B[d], kh), _key_row(x_t, G_KD[d], kh)
                    for vh in range(V_HI):
                        g = ds_ref[d, kh, vh] + dy_s[vh] * r_r
                        ds_ref[d, kh, vh] = g
                        dsa_p[vh] = _acc(dsa_p[vh], g * b_r)
                        dv_p[vh] = _acc(dv_p[vh], g * kd_r)
                dsa = [-_lane_group_sum(p) for p in dsa_p]
                sa = [sa_ref[i, d, vh] for vh in range(V_HI)]
                dv_ref[at] = _gather_groups(dv_p, grp)
                blocks = {G_KK: zero, G_R: zero, G_W[d]: zero, G_B[d]: zero, G_KD[d]: zero}
                for kh in range(K_HI):
                    w_r, kk_r = _key_row(x_t, G_W[d], kh), _key_row(x_t, G_KK, kh)
                    dkk = dr = dw = db = dkd = None
                    for vh in range(V_HI):
                        g, before = ds_ref[d, kh, vh], hist_ref[i, d, kh, vh]
                        dr = _acc(dr, after(d, kh, vh) * dy_s[vh])
                        dw = _acc(dw, g * before)
                        dkd = _acc(dkd, g * v_s[vh])
                        db = _acc(db, g * sa[vh])
                        dkk = _acc(dkk, before * dsa[vh])
                        ds_ref[d, kh, vh] = g * w_r + dsa[vh] * kk_r
                    for gi, a in ((G_KK, dkk), (G_R, dr), (G_W[d], dw), (G_B[d], -db), (G_KD[d], dkd)):
                        blocks[gi] = jnp.where(row == kh, _colsum(a), blocks[gi])
                for gi in range(N_GROUP):
                    g_ref[at, SUBLANES * gi:SUBLANES * (gi + 1), :] = blocks.get(gi, zero)

        body(last, lambda d, kh, vh: after_ref[d, kh, vh])

        def step(ii, carry):
            i = last - ii
            body(i, lambda d, kh, vh: hist_ref[i + 1, d, kh, vh])
            return carry

        lax.fori_loop(1, SCAN_CHUNK, step, 0, unroll=SCAN_UNROLL)
        after_ref[...] = hist_ref[0]

        if n_x:
            @pl.when(c == nc - 1)
            def _():
                for cp in riders():
                    cp.wait()

    return pl.pallas_call(
        kern,
        out_shape=[jax.ShapeDtypeStruct((steps, HEAD, LANES), F32)] * 2
        + [jax.ShapeDtypeStruct((steps, SUBLANES, LANES), F32)] * 2 + _exchange_out_shapes(exchange, n_x),
        grid=(nc,),
        in_specs=[k_spec(back), k_spec(same), v_spec(back), v_spec(same), v_spec(back), v_spec(same),
                  pl.BlockSpec((SCAN_CHUNK,) + ST_SHAPE, lambda c: (back(c), 0, 0, 0, 0, 0)),
                  pl.BlockSpec(ST_SHAPE, lambda c: (0, 0, 0, 0, 0)),
                  pl.BlockSpec((SCAN_CHUNK,) + SA_SHAPE, lambda c: (back(c), 0, 0, 0, 0))] + _hbm_specs(n_x),
        out_specs=[k_spec(back), k_spec(same), v_spec(back), v_spec(same)] + _hbm_specs(n_x),
        scratch_shapes=[pltpu.VMEM(ST_SHAPE, F32), pltpu.VMEM(ST_SHAPE, F32)]
        + (_exchange_sems(n_x) if n_x else []),
        compiler_params=_params(("arbitrary",), SCAN_VMEM_LIMIT), name=name)(xall, xall, v_c, v_c, dy_c, dy_c, hist, fin, sa,
                                                            *exchange)


def _bf16_pieces(x):
    hi = x.astype(BF16)
    rest = x - hi.astype(F32)
    mid = rest.astype(BF16)
    return hi, mid, (rest - mid.astype(F32)).astype(BF16)


def _to_key_rows(wide, bsz, seq, *, name):
    assert bsz == 2
    perm = _key_row_maps()
    tt = min(RELAYOUT_TILE, seq)
    per_seq = seq // tt

    def kern(x0_ref, x1_ref, p0_ref, p1_ref, o_ref):
        total = None
        for x_ref, p_ref in ((x0_ref, p0_ref), (x1_ref, p1_ref)):
            for piece in _bf16_pieces(x_ref[...]):
                term = jnp.dot(piece, p_ref[...], preferred_element_type=F32)
                total = term if total is None else total + term
        for r in range(K_HI):
            o_ref[:, r, :] = total[:, LANES * r:LANES * (r + 1)]

    p_spec = pl.BlockSpec((D_RWKV, K_HI * LANES), lambda i, a: (0, 0))
    return pl.pallas_call(
        kern, out_shape=jax.ShapeDtypeStruct((seq, HEAD, LANES), F32), grid=(per_seq, N_GROUP),
        in_specs=[pl.BlockSpec((tt, D_RWKV), lambda i, a: (i, a)),
                  pl.BlockSpec((tt, D_RWKV), lambda i, a: (per_seq + i, a)), p_spec, p_spec],
        out_specs=pl.BlockSpec((tt, K_HI, LANES), lambda i, a: (i, a, 0)),
        compiler_params=_params(("parallel", "parallel")), name=name)(wide, wide, *perm)


def _key_row_maps():
    src = jnp.arange(D_RWKV)
    head, kh, kl = src // HEAD, (src // SUBLANES) % K_HI, src % SUBLANES
    dst = jnp.arange(K_HI * LANES)
    return [((kh[:, None] == dst[None, :] // LANES) & (kl[:, None] == (dst[None, :] // N_CHAIN) % SUBLANES)
             & ((dst[None, :] // N_HEAD) % 2 == b) & (head[:, None] == dst[None, :] % N_HEAD)).astype(BF16)
            for b in range(2)]


def _from_key_rows(g_f, g_b, bsz, seq, *, name):
    assert bsz == 2
    maps = jnp.stack([m.T for m in _key_row_maps()])
    tt = min(RELAYOUT_TILE, seq)
    per_seq = seq // tt

    def kern(gf_ref, gb_ref, q_ref, o_ref):
        a = pl.program_id(2)
        shared = a <= G_R
        from_f = shared | (a % 2 == G_W[0] % 2)

        def rearranged(g_ref):
            g = jnp.concatenate([g_ref[:, r, :] for r in range(K_HI)], axis=1)
            hi, mid, lo = (jnp.dot(piece, q_ref[0], preferred_element_type=F32) for piece in _bf16_pieces(g))
            return hi + mid + lo

        @pl.when(from_f)
        def _():
            o_ref[...] = rearranged(gf_ref)

        @pl.when(jnp.logical_not(from_f))
        def _():
            o_ref[...] = rearranged(gb_ref)

        @pl.when(shared)
        def _():
            o_ref[...] += rearranged(gb_ref)

    g_spec = pl.BlockSpec((tt, K_HI, LANES), lambda b, i, a: (i, a, 0))
    return pl.pallas_call(
        kern, out_shape=jax.ShapeDtypeStruct((bsz * seq, N_GROUP * D_RWKV), F32), grid=(bsz, per_seq, N_GROUP),
        in_specs=[g_spec, g_spec, pl.BlockSpec((1, K_HI * LANES, D_RWKV), lambda b, i, a: (b, 0, 0))],
        out_specs=pl.BlockSpec((tt, D_RWKV), lambda b, i, a: (b * per_seq + i, a)),
        compiler_params=_params(("parallel", "parallel", "parallel")), name=name)(g_f, g_b, maps)


def _to_value_rows(a, bsz, seq):
    z = a.reshape(bsz, seq, N_HEAD, V_HI, SUBLANES).transpose(1, 4, 3, 0, 2)
    return z.reshape(seq, SUBLANES, LANES)


def _from_value_rows(y, bsz, seq):
    z = y.reshape(seq, SUBLANES, V_HI, bsz, N_HEAD).transpose(3, 0, 4, 2, 1)
    return z.reshape(bsz * seq, D_RWKV)


def _pad_cols(a, segs):
    out, off = [], 0
    for w, wp in segs:
        out.append(a[..., off:off + w])
        if wp > w:
            out.append(jnp.zeros(a.shape[:-1] + (wp - w,), a.dtype))
        off += w
    return jnp.concatenate(out, axis=-1)


def _unpad_cols(a, segs):
    out, off = [], 0
    for w, wp in segs:
        out.append(a[..., off:off + w])
        off += wp
    return jnp.concatenate(out, axis=-1)


P_SEGS = ((3 * D_RWKV, 3 * D_RWKV), (D_LORA, 128), (D_LORA, 128), (D_GATE, 256), (3 * D_CONV, 3 * D_CONV))
S_SEGS = P_SEGS[:4]


def _pad_rows(a, rows):
    return jnp.concatenate([a, jnp.zeros((rows - a.shape[0], a.shape[1]), a.dtype)], axis=0)


LATE = ("w_out", "w_gate", "w_up", "w_down")


def _local_step(x, target, w, late=None):
    bsz, seq, _ = x.shape
    t = bsz * seq
    x2d = x.reshape(t, D_MODEL)
    tg2d = target.reshape(t, D_MODEL)
    row = lambda a: a.reshape(1, -1).astype(F32)

    w_in = _pad_cols(w["w_in"][0], P_SEGS)
    mu = _pad_cols(row(w["mu_shift"]), S_SEGS)
    wupf, wupb, aupf, aupb = (_pad_rows(w[n][0].astype(F32), 128) for n in ("w_up_f", "w_up_b", "a_up_f", "a_up_b"))
    gup = _pad_rows(w["g_up"][0].astype(F32), 256)
    conv_w = _pad_rows(w["conv_w"][0].astype(F32), SUBLANES)
    norm1, norm2, normf = row(w["norm1_w"]), row(w["norm2_w"]), row(w["norm_f_w"])
    vec = {n: row(w[n]) for n in VEC}
    head_of = jnp.arange(LANES) // HEAD
    bd = (head_of[:, None] == head_of[None, :]).astype(F32)
    pre_consts = [vec["k_k"], vec["w0_f"], vec["w0_b"], vec["a0_f"], vec["a0_b"], vec["k_a_f"], vec["k_a_b"],
                  wupf, wupb, aupf, aupb, gup, bd]
    post_consts = [vec["gn_w"], vec["gn_b"], vec["r_k_f"], vec["r_k_b"], bd]

    h1, = _rowwise(_rms, [x2d], [norm1], [D_MODEL], [], name="rms1_fwd", out_dtype=BF16, tb=WIDE_TILE)
    p = _mm(h1, w_in, name="mm_in")
    pss, oconv = _shift_conv_fwd(p, mu, conv_w, seq, name="shift_conv_fwd")
    pre_rows = [(pss, 0, 512), (pss, 1, 512), (pss, XW0 // 128, 128), (pss, XA0 // 128, 128), (pss, XG0 // 256, 256)]
    sc, g = _rowwise(_prescan_math, pre_rows, pre_consts, [[D_RWKV] * N_GROUP, D_RWKV], [], name="prescan_fwd")
    xall = _to_key_rows(sc, bsz, seq, name="to_key_rows")
    v_l = _to_value_rows(pss[:, 2 * D_RWKV:3 * D_RWKV], bsz, seq)
    y_f, y_b, hist, fin, sa, *gathered = _scan_fwd(xall, v_l, gather=[late[n] for n in LATE] if late else (),
                                                   name="scan_fwd")
    w_out, w_gate, w_up, w_down = (
        (_from_slots(a, SHARD_AXIS[n]) if late else w[n])[0] for n, a in zip(LATE, gathered or LATE))
    y = _from_value_rows(y_f + y_b, bsz, seq)
    post_rows = [y, (pss, 0, 512), (pss, 2, 512), (sc, G_KD[0], 512), (sc, G_KD[1], 512), g]

    def post_fwd(y_, r_, v_, kdf_, kdb_, g_, oc_, *consts):
        return _postscan_math(y_, r_, v_, kdf_, kdb_, g_, *consts), oc_

    o, = _rowwise(post_fwd, post_rows + [oconv], post_consts, [[D_RWKV, D_CONV]], [], name="postscan_fwd",
                  out_dtype=BF16)
    x1 = _mm(o, w_out, add=x2d, name="mm_out")
    h2, = _rowwise(_rms, [x1], [norm2], [D_MODEL], [], name="rms2_fwd", out_dtype=BF16, tb=WIDE_TILE)
    gg, uu, ff = _mm_swiglu(h2, w_gate, w_up, name="mm_gate_up")
    x2 = _mm(ff, w_down, add=x1, name="mm_down")

    def final(x_, tg_, wn_):
        yo, vjp = jax.vjp(_rms, x_, wn_)
        err = yo - tg_
        dx_, dwn_ = vjp(err * (1.0 / D_MODEL))
        part = jnp.sum(jnp.sum(err * err, axis=1, keepdims=True), axis=0, keepdims=True) * (0.5 / D_MODEL)
        return dx_, part + jnp.zeros((1, LANES), F32), dwn_

    dx2, loss_acc, d_normf = _rowwise(final, [x2, tg2d], [normf], [D_MODEL], [(1, LANES), (1, D_MODEL)],
                                      name="loss_head", tb=WIDE_TILE)
    dgg, duu = _mm_swiglu_bwd(dx2, w_down, gg, uu, name="mm_down_dx")
    g_w_down = _mm(ff, dx2, ta=True, name="mm_down_dw")
    dh2 = _mm(dgg, w_gate, tb=True, name="mm_gate_dx")
    dh2 = _mm(duu, w_up, tb=True, add=dh2, name="mm_up_dx")
    g_w_gate = _mm(h2, dgg, ta=True, name="mm_gate_dw")
    g_w_up = _mm(h2, duu, ta=True, name="mm_up_dw")

    def rms_bwd(x_, dh_, dres_, wn_):
        _, vjp = jax.vjp(_rms, x_, wn_)
        dx_, dwn_ = vjp(dh_)
        return dx_ + dres_, dwn_

    dx1, d_norm2 = _rowwise(rms_bwd, [x1, dh2, dx2], [norm2], [D_MODEL], [(1, D_MODEL)], name="rms2_bwd", tb=WIDE_TILE)
    do = _mm(dx1, w_out, tb=True, name="mm_out_dx")
    g_w_out = _mm(o, dx1, ta=True, name="mm_out_dw")

    def post_bwd(y_, r_, v_, kdf_, kdb_, g_, do_, *consts):
        _, vjp = jax.vjp(lambda *a: _postscan_math(*a, consts[4]), y_, r_, v_, kdf_, kdb_, g_, *consts[:4])
        return vjp(do_)

    (dy, dr_c, dv_c, dkdf_c, dkdb_c, dg, d_gn_w, d_gn_b, d_rkf, d_rkb) = _rowwise(
        post_bwd, post_rows + [(do, 0, 512)], post_consts, [D_RWKV] * 6, [(1, D_RWKV)] * 4, name="postscan_bwd")
    dy_l = _to_value_rows(dy, bsz, seq)
    late_grads = {"w_out": g_w_out[None], "w_gate": g_w_gate[None], "w_up": g_w_up[None], "w_down": g_w_down[None]}
    g_f, g_b, dv_f, dv_b, *late_parts = _scan_bwd(
        xall, v_l, dy_l, hist, fin, sa, name="scan_bwd",
        exchange=[_to_slots(late_grads[n], SHARD_AXIS[n]).astype(BF16) for n in LATE] if late else ())
    dsc = _from_key_rows(g_f, g_b, bsz, seq, name="from_key_rows")
    dv_s = _from_value_rows(dv_f + dv_b, bsz, seq)

    def pre_bwd(r_, k_, xw_, xa_, xg_, dkk_, dr_s, dwf_, dwb_, dbf_, dbb_, dkdf_s, dkdb_s,
                dr_c_, dv_c_, dv_s_, dkdf_c_, dkdb_c_, dg_, *consts):
        _, vjp = jax.vjp(lambda *a: _prescan_math(*a, consts[-1]), r_, k_, xw_, xa_, xg_, *consts[:-1])
        grads = vjp((dkk_, dr_s + dr_c_, dwf_, dwb_, dbf_, dbb_, dkdf_s + dkdf_c_, dkdb_s + dkdb_c_, dg_))
        dr_, dk_, dxw_, dxa_, dxg_ = grads[:5]
        return (dr_, dk_, dv_c_ + dv_s_, dxw_, dxa_, dxg_) + tuple(grads[5:])

    pre_b_rows = (pre_rows + [(dsc, j, 512) for j in range(N_GROUP)]
                  + [dr_c, dv_c, dv_s, dkdf_c, dkdb_c, dg])
    pre_b = _rowwise(pre_bwd, pre_b_rows, pre_consts, [[512, 512, 512, 128, 128, 256]],
                     [(1, D_RWKV)] * 7 + [(128, D_RWKV)] * 4 + [(256, D_RWKV)], name="prescan_bwd")
    d_pss = pre_b[0]
    d_kk_, d_w0f, d_w0b, d_a0f, d_a0b, d_kaf, d_kab, d_wupf, d_wupb, d_aupf, d_aupb, d_gup = pre_b[1:]
    dp, d_mu, d_conv = _shift_conv_bwd(p, d_pss, do, mu, conv_w, seq, name="shift_conv_bwd")
    g_w_in = _mm(h1, dp, ta=True, name="mm_in_dw")
    grads = {
        "w_in": _unpad_cols(g_w_in, P_SEGS)[None], "mu_shift": _unpad_cols(d_mu, S_SEGS),
        "w_up_f": d_wupf[None, :D_LORA], "w0_f": d_w0f, "w_up_b": d_wupb[None, :D_LORA], "w0_b": d_w0b,
        "a_up_f": d_aupf[None, :D_LORA], "a0_f": d_a0f, "a_up_b": d_aupb[None, :D_LORA], "a0_b": d_a0b,
        "g_up": d_gup[None, :D_GATE], "k_k": d_kk_, "k_a_f": d_kaf, "k_a_b": d_kab,
        "r_k_f": d_rkf, "r_k_b": d_rkb, "gn_w": d_gn_w, "gn_b": d_gn_b, "conv_w": d_conv[None, :3],
        "w_out": g_w_out[None], "norm2_w": d_norm2, "w_gate": g_w_gate[None], "w_up": g_w_up[None],
        "w_down": g_w_down[None], "norm_f_w": d_normf,
    }
    early = ("w_in",) + LORA
    parts = dict(zip(LATE, late_parts))
    if late:
        vec_rows = jnp.concatenate([grads[n] for n in VEC] + [jnp.zeros((16 - len(VEC), D_RWKV), F32)], axis=0)
        slots = [_to_slots(grads[n], SHARD_AXIS[n]).astype(BF16 if n in BIG else F32) for n in early]
        dh1, *recv = _mm(dp, w_in, tb=True, exchange=(slots, [vec_rows]), name="mm_in_dx")
        parts.update(zip(early + ("vec",), recv))
    else:
        dh1 = _mm(dp, w_in, tb=True, name="mm_in_dx")
    dx, grads["norm1_w"] = _rowwise(rms_bwd, [x2d, dh1, dx1], [norm1], [D_MODEL], [(1, D_MODEL)], name="rms1_bwd",
                                    tb=WIDE_TILE)
    return loss_acc, dx.reshape(bsz, seq, D_MODEL), grads, parts


def _hbm_specs(n):
    return [pl.BlockSpec(memory_space=pl.ANY)] * n


def _all_gather(arrs, *, name):
    n = len(arrs)

    def body(*refs):
        x_refs, out_refs = refs[:n], refs[n:2 * n]
        send_sems, recv_sems, local_sems = refs[2 * n:]
        x, y, c = lax.axis_index("x"), lax.axis_index("y"), lax.axis_index("c")
        me, sibling = (x, y, c), (x, y, 1 - c)
        chips = [(1 - x, y), (x, 1 - y), (1 - x, 1 - y)]

        def slot(a, px, py, pc):
            return out_refs[a].at[4 * px + 2 * py + pc]

        def copy(a, k, block, to, src=None):
            return pltpu.make_async_remote_copy(
                src_ref=slot(a, *block) if src is None else src, dst_ref=slot(a, *block),
                send_sem=send_sems.at[k, a], recv_sem=recv_sems.at[k, a],
                device_id=to, device_id_type=pl.DeviceIdType.MESH)

        mine = [pltpu.make_async_copy(x_refs[a], slot(a, *me), local_sems.at[a]) for a in range(n)]
        for cp in mine:
            cp.start()
        first = []
        for a in range(n):
            first.append(copy(a, 0, me, sibling, src=x_refs[a]))
            first += [copy(a, 1 + j, me, (*chip, c), src=x_refs[a]) for j, chip in enumerate(chips)]
        for cp in first:
            cp.start()
        passed = []
        for j, chip in enumerate(chips):
            for a in range(n):
                copy(a, 1 + j, (*chip, c), me).wait_recv()
                cp = copy(a, 4 + j, (*chip, c), sibling)
                cp.start()
                passed.append(cp)
        for a in range(n):
            copy(a, 0, sibling, me).wait_recv()
            for j, chip in enumerate(chips):
                copy(a, 4 + j, (*chip, 1 - c), me).wait_recv()
        for cp in first + passed:
            cp.wait_send()
        for cp in mine:
            cp.wait()

    return pl.pallas_call(
        body, out_shape=[jax.ShapeDtypeStruct((N_DEV,) + a.shape, a.dtype) for a in arrs],
        in_specs=_hbm_specs(n), out_specs=_hbm_specs(n),
        scratch_shapes=[pltpu.SemaphoreType.DMA((7, n)), pltpu.SemaphoreType.DMA((7, n)),
                        pltpu.SemaphoreType.DMA((n,))],
        name=name)(*arrs)


def _exchange(sliced, whole, *, name):
    arrs = list(sliced) + list(whole)
    n, n_sliced = len(arrs), len(sliced)

    def body(*refs):
        copies = _exchange_copies(refs[:n], refs[n:2 * n], n_sliced, *refs[2 * n:])
        for cp in copies:
            cp.start()
        for cp in copies:
            cp.wait()

    return pl.pallas_call(
        body, out_shape=_exchange_out_shapes(arrs, n_sliced), in_specs=_hbm_specs(n), out_specs=_hbm_specs(n),
        scratch_shapes=_exchange_sems(n), name=name)(*arrs)


def _exchange_out_shapes(arrs, n_sliced):
    return [jax.ShapeDtypeStruct(a.shape if i < n_sliced else (N_DEV,) + a.shape, a.dtype)
            for i, a in enumerate(arrs)]


def _exchange_sems(n):
    return [pltpu.SemaphoreType.DMA((7, n)), pltpu.SemaphoreType.DMA((7, n)), pltpu.SemaphoreType.DMA((n,))]


def _exchange_copies(in_refs, out_refs, n_sliced, send_sems, recv_sems, local_sems):
    n = len(in_refs)
    x, y, c = lax.axis_index("x"), lax.axis_index("y"), lax.axis_index("c")
    me = 4 * x + 2 * y + c

    def src(a, dev):
        return in_refs[a].at[dev] if a < n_sliced else in_refs[a]

    copies = [pltpu.make_async_copy(src(a, me), out_refs[a].at[me], local_sems.at[a]) for a in range(n)]
    for k in range(1, N_DEV):
        px = 1 - x if k & 4 else x
        py = 1 - y if k & 2 else y
        pc = 1 - c if k & 1 else c
        for a in range(n):
            copies.append(pltpu.make_async_remote_copy(
                src_ref=src(a, 4 * px + 2 * py + pc), dst_ref=out_refs[a].at[me],
                send_sem=send_sems.at[k - 1, a], recv_sem=recv_sems.at[k - 1, a],
                device_id=(px, py, pc), device_id_type=pl.DeviceIdType.MESH))
    return copies


def _adam_math(g, w, m, v):
    nm = ADAM_B1 * m + (1.0 - ADAM_B1) * g
    nv = ADAM_B2 * v + (1.0 - ADAM_B2) * (g * g)
    m_hat = nm / (1.0 - ADAM_B1 ** ADAM_STEP)
    v_hat = nv / (1.0 - ADAM_B2 ** ADAM_STEP)
    return -ADAM_LR * (m_hat / (jnp.sqrt(v_hat) + ADAM_EPS) + ADAM_WD * w), nm, nv


def _slot_sum(ref):
    g = ref[0].astype(F32)
    for s in range(1, N_DEV):
        g = g + ref[s].astype(F32)
    return g


def _adamw_big(parts, w, m, v, *, name):
    _, rws, cols = w.shape
    tr = _tile(rws, (256, 176, 128))

    def kern(p_ref, w_ref, m_ref, v_ref, g_ref, d_ref, nm_ref, nv_ref):
        g = _slot_sum(p_ref)
        g_ref[...] = g
        d_ref[...], nm_ref[...], nv_ref[...] = _adam_math(g, w_ref[...], m_ref[...], v_ref[...])

    spec = pl.BlockSpec((1, tr, cols), lambda i: (0, i, 0))
    return pl.pallas_call(
        kern, out_shape=[jax.ShapeDtypeStruct(w.shape, F32)] * 4, grid=(rws // tr,),
        in_specs=[pl.BlockSpec((N_DEV, 1, tr, cols), lambda i: (0, 0, i, 0)), spec, spec, spec],
        out_specs=[spec] * 4, compiler_params=_params(("parallel",)), name=name)(parts, w, m, v)


def _adamw_small(lora_parts, vec_parts, wide_parts, wmv, *, name):
    names = LORA + VEC + WIDE
    n_l, n = len(LORA), len(names)
    flat = [a for trip in wmv for a in trip]

    def kern(*refs):
        l_refs, vec_ref, wide_ref = refs[:n_l], refs[n_l], refs[n_l + 1]
        in_refs = refs[n_l + 2:n_l + 2 + 3 * n]
        out_refs = refs[n_l + 2 + 3 * n:]
        vec_sum, wide_sum = _slot_sum(vec_ref), _slot_sum(wide_ref)
        for i, nm in enumerate(names):
            w_ref, m_ref, v_ref = in_refs[3 * i:3 * i + 3]
            if i < n_l:
                g = _slot_sum(l_refs[i])
            elif nm in VEC:
                g = vec_sum[i - n_l:i - n_l + 1, :]
            else:
                g = wide_sum[WIDE.index(nm):WIDE.index(nm) + 1, :w_ref.shape[-1]]
            o = out_refs[4 * i:4 * i + 4]
            o[0][...] = g
            o[1][...], o[2][...], o[3][...] = _adam_math(g, w_ref[...], m_ref[...], v_ref[...])

    out_shape = [jax.ShapeDtypeStruct(trip[0].shape, F32) for trip in wmv for _ in range(4)]
    outs = pl.pallas_call(kern, out_shape=out_shape, name=name,
                          compiler_params=pltpu.CompilerParams(vmem_limit_bytes=VMEM_LIMIT))(
        *lora_parts, vec_parts, wide_parts, *flat)
    return [tuple(outs[4 * i:4 * i + 4]) for i in range(n)]


def _to_slots(g, axis):
    _, rws, cols = g.shape
    if axis == 1:
        return g.reshape(N_DEV, 1, rws // N_DEV, cols)
    return g.reshape(1, rws, N_DEV, cols // N_DEV).transpose(2, 0, 1, 3)


def _from_slots(got, axis):
    _, _, rws, cols = got.shape
    if axis == 1:
        return got.reshape(1, N_DEV * rws, cols)
    return got.transpose(1, 2, 0, 3).reshape(1, rws, N_DEV * cols)


def _pad_lanes(a, width):
    return jnp.concatenate([a, jnp.zeros(a.shape[:-1] + (width - a.shape[-1],), a.dtype)], axis=-1)


def kernel(x, norm1_w, w_in, mu_shift, w_up_f, w0_f, w_up_b, w0_b, a_up_f, a0_f, a_up_b, a0_b, g_up, k_k, k_a_f, k_a_b, r_k_f, r_k_b, gn_w, gn_b, conv_w, w_out, norm2_w, w_gate, w_up, w_down, norm_f_w, loss_target, m_norm1_w, m_w_in, m_mu_shift, m_w_up_f, m_w0_f, m_w_up_b, m_w0_b, m_a_up_f, m_a0_f, m_a_up_b, m_a0_b, m_g_up, m_k_k, m_k_a_f, m_k_a_b, m_r_k_f, m_r_k_b, m_gn_w, m_gn_b, m_conv_w, m_w_out, m_norm2_w, m_w_gate, m_w_up, m_w_down, m_norm_f_w, v_norm1_w, v_w_in, v_mu_shift, v_w_up_f, v_w0_f, v_w_up_b, v_w0_b, v_a_up_f, v_a0_f, v_a_up_b, v_a0_b, v_g_up, v_k_k, v_k_a_f, v_k_a_b, v_r_k_f, v_r_k_b, v_gn_w, v_gn_b, v_conv_w, v_w_out, v_norm2_w, v_w_gate, v_w_up, v_w_down, v_norm_f_w):
    local = dict(norm1_w=norm1_w, w_in=w_in, mu_shift=mu_shift, w_up_f=w_up_f, w0_f=w0_f, w_up_b=w_up_b,
                 w0_b=w0_b, a_up_f=a_up_f, a0_f=a0_f, a_up_b=a_up_b, a0_b=a0_b, g_up=g_up, k_k=k_k, k_a_f=k_a_f,
                 k_a_b=k_a_b, r_k_f=r_k_f, r_k_b=r_k_b, gn_w=gn_w, gn_b=gn_b, conv_w=conv_w, w_out=w_out,
                 norm2_w=norm2_w, w_gate=w_gate, w_up=w_up, w_down=w_down, norm_f_w=norm_f_w)
    mom_m = dict(norm1_w=m_norm1_w, w_in=m_w_in, mu_shift=m_mu_shift, w_up_f=m_w_up_f, w0_f=m_w0_f,
                 w_up_b=m_w_up_b, w0_b=m_w0_b, a_up_f=m_a_up_f, a0_f=m_a0_f, a_up_b=m_a_up_b, a0_b=m_a0_b,
                 g_up=m_g_up, k_k=m_k_k, k_a_f=m_k_a_f, k_a_b=m_k_a_b, r_k_f=m_r_k_f, r_k_b=m_r_k_b,
                 gn_w=m_gn_w, gn_b=m_gn_b, conv_w=m_conv_w, w_out=m_w_out, norm2_w=m_norm2_w, w_gate=m_w_gate,
                 w_up=m_w_up, w_down=m_w_down, norm_f_w=m_norm_f_w)
    mom_v = dict(norm1_w=v_norm1_w, w_in=v_w_in, mu_shift=v_mu_shift, w_up_f=v_w_up_f, w0_f=v_w0_f,
                 w_up_b=v_w_up_b, w0_b=v_w0_b, a_up_f=v_a_up_f, a0_f=v_a0_f, a_up_b=v_a_up_b, a0_b=v_a0_b,
                 g_up=v_g_up, k_k=v_k_k, k_a_f=v_k_a_f, k_a_b=v_k_a_b, r_k_f=v_r_k_f, r_k_b=v_r_k_b,
                 gn_w=v_gn_w, gn_b=v_gn_b, conv_w=v_conv_w, w_out=v_w_out, norm2_w=v_norm2_w, w_gate=v_w_gate,
                 w_up=v_w_up, w_down=v_w_down, norm_f_w=v_norm_f_w)

    early = ("w_in",) + LORA
    got = _all_gather([local["w_in"].astype(BF16)] + [local[n] for n in LORA], name="gather")
    full = dict(local)
    full.update({n: _from_slots(a, SHARD_AXIS[n]) for n, a in zip(early, got)})

    loss_part, grad_x, grads, parts = _local_step(x, loss_target, full,
                                                  late={n: local[n].astype(BF16) for n in LATE})

    wide_rows = jnp.concatenate([_pad_lanes(a, WIDE_ROW) for a in [grads[n] for n in WIDE] + [loss_part]]
                                + [jnp.zeros((SUBLANES - len(WIDE) - 1, WIDE_ROW), F32)], axis=0)
    wide_parts, = _exchange([], [wide_rows], name="grad_exchange")
    loss = jnp.sum(wide_parts[:, len(WIDE), 0])
    out = {}
    for n in BIG:
        out[n] = _adamw_big(parts[n], local[n], mom_m[n], mom_v[n], name="adamw_" + n)

    def small_form(n, a):
        if n in LORA:
            return a
        a = a.reshape(1, -1)
        return _pad_lanes(a, WIDE_ROW) if n == "mu_shift" else a

    small = LORA + VEC + WIDE
    res = _adamw_small([parts[n] for n in LORA], parts["vec"], wide_parts,
                       [tuple(small_form(n, d[n]) for d in (local, mom_m, mom_v)) for n in small],
                       name="adamw_small")
    for n, quad in zip(small, res):
        out[n] = tuple(a[..., :local[n].size].reshape(local[n].shape) if n not in LORA else a for a in quad)
    return (loss, grad_x, *[out[n][i] for i in range(4) for n in WEIGHTS])
```

```python
import functools

import jax
import jax.numpy as jnp
from jax import lax
from jax.experimental import pallas as pl
from jax.experimental.pallas import tpu as pltpu

F32 = jnp.float32
BF16 = jnp.bfloat16
HIGHEST = lax.Precision.HIGHEST

N_DEV = 8
D_MODEL = 1024
D_RWKV = 512
D_CONV = 512
HEAD = 64
N_HEAD = D_RWKV // HEAD
D_LORA = 64
D_GATE = 160
D_SHIFTED = 3 * D_RWKV + 2 * D_LORA + D_GATE
XW0, XA0, XG0 = 1536, 1664, 1792
D_SP = 2048
D_INP = D_SP + 3 * D_CONV
LOG_DECAY_SCALE = 0.606531
RMS_EPS = 1e-6
GN_EPS = 64e-5
NORM_EPS = 1e-12
ADAM_LR, ADAM_B1, ADAM_B2, ADAM_EPS, ADAM_WD, ADAM_STEP = 0.001, 0.9, 0.999, 1e-08, 0.01, 10

LANES = 128
SUBLANES = 8
VMEM_LIMIT = 48 * 1024 * 1024
SCAN_CHUNK = 32
SCAN_VMEM_LIMIT = 58 * 1024 * 1024
SCAN_UNROLL = 3
ROW_TILE = 128
WIDE_TILE = 256
RELAYOUT_TILE = 512

BIG = ("w_in", "w_out", "w_gate", "w_up", "w_down")
LORA = ("w_up_f", "w_up_b", "a_up_f", "a_up_b", "g_up", "conv_w")
SHARD_AXIS = {"w_in": 2, "w_out": 1, "w_gate": 2, "w_up": 2, "w_down": 1, "w_up_f": 2, "w_up_b": 2,
              "a_up_f": 2, "a_up_b": 2, "g_up": 2, "conv_w": 2}
VEC = ("w0_f", "w0_b", "a0_f", "a0_b", "k_k", "k_a_f", "k_a_b", "r_k_f", "r_k_b", "gn_w", "gn_b")
WIDE = ("mu_shift", "norm1_w", "norm2_w", "norm_f_w")
WIDE_ROW = 2048
WEIGHTS = ("norm1_w", "w_in", "mu_shift", "w_up_f", "w0_f", "w_up_b", "w0_b", "a_up_f", "a0_f", "a_up_b",
           "a0_b", "g_up", "k_k", "k_a_f", "k_a_b", "r_k_f", "r_k_b", "gn_w", "gn_b", "conv_w", "w_out",
           "norm2_w", "w_gate", "w_up", "w_down", "norm_f_w")


def _params(sem, limit=VMEM_LIMIT):
    return pltpu.CompilerParams(dimension_semantics=sem, vmem_limit_bytes=limit)


def _tile(n, cands):
    for c in cands:
        if n % c == 0:
            return c
    raise ValueError(f"no tile for {n}")


def _mm(a, b, *, ta=False, tb=False, add=None, exchange=None, name):
    (k_dim, m) = a.shape if ta else a.shape[::-1]
    (k2, n) = b.shape[::-1] if tb else b.shape
    assert k_dim == k2, (a.shape, b.shape, ta, tb)
    tm = _tile(m, (1408, 1024, 512, 256, 128))
    tn = _tile(n, (1408, 1024, 896, 512, 256, 128))
    tk = _tile(k_dim, (1408, 1024, 896, 512, 256, 128))
    nk = k_dim // tk
    grid = (m // tm, n // tn, nk)
    dims = (((0 if ta else 1,), (1 if tb else 0,)), ((), ()))
    sliced, whole = exchange or ((), ())
    riders = list(sliced) + list(whole)
    n_x, n_in = len(riders), 2 + (add is not None)

    def kern(*refs):
        a_ref, b_ref = refs[:2]
        add_ref = refs[2] if add is not None else None
        o_ref, acc_ref = refs[n_in + n_x], refs[n_in + 2 * n_x + 1]
        k = pl.program_id(2)
        step = (pl.program_id(0) * grid[1] + pl.program_id(1)) * nk + k

        def copies():
            return _exchange_copies(refs[n_in:n_in + n_x], refs[n_in + n_x + 1:n_in + 2 * n_x + 1], len(sliced),
                                    *refs[n_in + 2 * n_x + 2:])

        if n_x:
            @pl.when(step == 0)
            def _():
                for cp in copies():
                    cp.start()

        @pl.when(k == 0)
        def _():
            acc_ref[...] = jnp.zeros_like(acc_ref)

        acc_ref[...] += lax.dot_general(a_ref[...].astype(BF16), b_ref[...].astype(BF16), dims,
                                        preferred_element_type=F32)

        @pl.when(k == nk - 1)
        def _():
            if add is None:
                o_ref[...] = acc_ref[...]
            else:
                o_ref[...] = acc_ref[...] + add_ref[...]

        if n_x:
            @pl.when(step == grid[0] * grid[1] * nk - 1)
            def _():
                for cp in copies():
                    cp.wait()

    a_spec = (pl.BlockSpec((tk, tm), lambda i, j, k: (k, i)) if ta
              else pl.BlockSpec((tm, tk), lambda i, j, k: (i, k)))
    b_spec = (pl.BlockSpec((tn, tk), lambda i, j, k: (j, k)) if tb
              else pl.BlockSpec((tk, tn), lambda i, j, k: (k, j)))
    o_spec = pl.BlockSpec((tm, tn), lambda i, j, k: (i, j))
    in_specs = [a_spec, b_spec] + ([o_spec] if add is not None else []) + _hbm_specs(n_x)
    args = (a, b) + ((add,) if add is not None else ()) + tuple(riders)
    out = pl.pallas_call(
        kern, out_shape=[jax.ShapeDtypeStruct((m, n), F32)] + _exchange_out_shapes(riders, len(sliced)), grid=grid,
        in_specs=in_specs, out_specs=[o_spec] + _hbm_specs(n_x),
        scratch_shapes=[pltpu.VMEM((tm, tn), F32)] + (_exchange_sems(n_x) if n_x else []),
        compiler_params=_params(("arbitrary",) * 3 if n_x else ("parallel", "parallel", "arbitrary")),
        name=name)(*args)
    return out if n_x else out[0]


def _swiglu(g, u):
    return jax.nn.silu(g) * u


FFN_TN = 256


def _mm_swiglu(h, w_gate, w_up, *, name):
    m, k_dim = h.shape
    n = w_gate.shape[1]
    tm = _tile(m, (1024, 512, 256, 128))

    def kern(h_ref, wg_ref, wu_ref, g_ref, u_ref, f_ref):
        hv = h_ref[...].astype(BF16)
        g = jnp.dot(hv, wg_ref[...].astype(BF16), preferred_element_type=F32)
        u = jnp.dot(hv, wu_ref[...].astype(BF16), preferred_element_type=F32)
        g_ref[...] = g
        u_ref[...] = u
        f_ref[...] = _swiglu(g, u).astype(f_ref.dtype)

    w_spec = pl.BlockSpec((k_dim, FFN_TN), lambda i, j: (0, j))
    o_spec = pl.BlockSpec((tm, FFN_TN), lambda i, j: (i, j))
    return pl.pallas_call(
        kern, out_shape=[jax.ShapeDtypeStruct((m, n), F32)] * 2 + [jax.ShapeDtypeStruct((m, n), BF16)],
        grid=(m // tm, n // FFN_TN), in_specs=[pl.BlockSpec((tm, k_dim), lambda i, j: (i, 0)), w_spec, w_spec],
        out_specs=[o_spec] * 3, compiler_params=_params(("parallel", "parallel")), name=name)(h, w_gate, w_up)


def _mm_swiglu_bwd(dx, w_down, g, u, *, name):
    m, k_dim = dx.shape
    n = w_down.shape[0]
    tm = _tile(m, (1024, 512, 256, 128))

    def kern(dx_ref, w_ref, g_ref, u_ref, dg_ref, du_ref):
        df = lax.dot_general(dx_ref[...].astype(BF16), w_ref[...].astype(BF16), (((1,), (1,)), ((), ())),
                             preferred_element_type=F32)
        _, vjp = jax.vjp(_swiglu, g_ref[...], u_ref[...])
        dg, du = vjp(df)
        dg_ref[...] = dg.astype(dg_ref.dtype)
        du_ref[...] = du.astype(du_ref.dtype)

    o_spec = pl.BlockSpec((tm, FFN_TN), lambda i, j: (i, j))
    return pl.pallas_call(
        kern, out_shape=[jax.ShapeDtypeStruct((m, n), BF16)] * 2, grid=(m // tm, n // FFN_TN),
        in_specs=[pl.BlockSpec((tm, k_dim), lambda i, j: (i, 0)), pl.BlockSpec((FFN_TN, k_dim), lambda i, j: (j, 0)),
                  o_spec, o_spec],
        out_specs=[o_spec] * 2, compiler_params=_params(("parallel", "parallel")), name=name)(dx, w_down, g, u)


def _rowwise(fn, rows, consts, out_rows, out_accs, *, name, tb=ROW_TILE, out_dtype=F32):
    t = (rows[0][0] if isinstance(rows[0], tuple) else rows[0]).shape[0]
    n_r, n_c, n_o, n_a = len(rows), len(consts), len(out_rows), len(out_accs)
    pieces = [w if isinstance(w, (list, tuple)) else [w] for w in out_rows]

    def kern(*refs):
        r_refs = refs[:n_r]
        c_refs = refs[n_r:n_r + n_c]
        o_refs = refs[n_r + n_c:n_r + n_c + n_o]
        a_refs = refs[n_r + n_c + n_o:]
        vals = fn(*[r[...] for r in r_refs], *[c[...] for c in c_refs])
        vals = list(vals) if isinstance(vals, (tuple, list)) else [vals]
        pos = 0
        for o_ref, ws in zip(o_refs, pieces):
            off = 0
            for w in ws:
                o_ref[:, off:off + w] = vals[pos].astype(o_ref.dtype)
                off += w
                pos += 1
        if n_a:
            @pl.when(pl.program_id(0) == 0)
            def _():
                for a_ref in a_refs:
                    a_ref[...] = jnp.zeros_like(a_ref)
            for a_ref, v in zip(a_refs, vals[pos:]):
                a_ref[...] += v

    in_specs, args = [], []
    for r in rows:
        if isinstance(r, tuple):
            arr, blk, w = r
            in_specs.append(pl.BlockSpec((tb, w), functools.partial(lambda i, blk: (i, blk), blk=blk)))
        else:
            arr = r
            in_specs.append(pl.BlockSpec((tb, arr.shape[1]), lambda i: (i, 0)))
        args.append(arr)
    for c in consts:
        in_specs.append(pl.BlockSpec(c.shape, lambda i: (0, 0)))
        args.append(c)
    out_shape = [jax.ShapeDtypeStruct((t, sum(ws)), out_dtype) for ws in pieces]
    out_specs = [pl.BlockSpec((tb, sum(ws)), lambda i: (i, 0)) for ws in pieces]
    for shp in out_accs:
        out_shape.append(jax.ShapeDtypeStruct(shp, F32))
        out_specs.append(pl.BlockSpec(shp, lambda i: (0, 0)))
    res = pl.pallas_call(
        kern, out_shape=out_shape, grid=(t // tb,), in_specs=in_specs, out_specs=out_specs,
        compiler_params=_params(("arbitrary",) if n_a else ("parallel",)), name=name)(*args)
    return res


def _rms(x, w):
    return x * lax.rsqrt(jnp.mean(x * x, axis=-1, keepdims=True) + RMS_EPS) * w


def _seg_sum(x, bd):
    return jnp.concatenate(
        [jnp.dot(x[:, LANES * j:LANES * (j + 1)], bd, precision=HIGHEST, preferred_element_type=F32)
         for j in range(x.shape[1] // LANES)], axis=1)


@jax.custom_vjp
def _seg(x, bd):
    return _seg_sum(x, bd)


_seg.defvjp(lambda x, bd: (_seg_sum(x, bd), bd), lambda bd, ct: (_seg_sum(ct, bd), jnp.zeros_like(bd)))


def _colsum(x):
    return jnp.sum(x, axis=0, keepdims=True)


def _prescan_math(r, k, xw, xa, xg, k_k, w0f, w0b, a0f, a0b, kaf, kab, wupf, wupb, aupf, aupb, gup, bd):
    kkr = k * k_k
    norm = jnp.sqrt(_seg(kkr * kkr, bd))
    kk = kkr / jnp.maximum(norm, NORM_EPS)
    th = jnp.tanh(xw)

    def direction(w0, wup, a0, aup, ka):
        logit = w0 + jnp.dot(th, wup, preferred_element_type=F32)
        w = jnp.exp(-LOG_DECAY_SCALE * jax.nn.sigmoid(logit))
        a = jax.nn.sigmoid(a0 + jnp.dot(xa, aup, preferred_element_type=F32))
        kd = k * (1.0 + (a - 1.0) * ka)
        return w, kd, kk * a

    wf, kdf, bf = direction(w0f, wupf, a0f, aupf, kaf)
    wb, kdb, bb = direction(w0b, wupb, a0b, aupb, kab)
    g = jnp.dot(jax.nn.sigmoid(xg), gup, preferred_element_type=F32)
    return kk, r, wf, wb, bf, bb, kdf, kdb, g


def _postscan_math(y, r, v, kdf, kdb, g, gn_w, gn_b, rkf, rkb, bd):
    mean = _seg(y, bd) * (1.0 / HEAD)
    yc = y - mean
    var = _seg(yc * yc, bd) * (1.0 / HEAD)
    yg = yc * lax.rsqrt(var + GN_EPS) * gn_w + gn_b
    bonus = (_seg(r * kdf * rkf, bd) + _seg(r * kdb * rkb, bd)) * v
    return (yg + bonus) * g


def _halo_specs(width, col_blk, tb, t):
    nb = t // SUBLANES
    step = tb // SUBLANES
    main = pl.BlockSpec((tb, width), lambda i: (i, col_blk))
    prev = pl.BlockSpec((SUBLANES, width), lambda i: (jnp.maximum(i * step - 1, 0), col_blk))
    nxt = pl.BlockSpec((SUBLANES, width), lambda i: (jnp.minimum((i + 1) * step, nb - 1), col_blk))
    return [main, prev, nxt]


def _neighbours(z, prev8, next8, first, last):
    tb = z.shape[0]
    row = lax.broadcasted_iota(jnp.int32, z.shape, 0)
    prow = jnp.where(first, 0.0, prev8[SUBLANES - 1:SUBLANES, :])
    nrow = jnp.where(last, 0.0, next8[0:1, :])
    down = jnp.where(row == 0, prow, pltpu.roll(z, 1, 0))
    up = jnp.where(row == tb - 1, nrow, pltpu.roll(z, tb - 1, 0))
    return down, up


def _shift_conv_fwd(p, mu, conv_w, seq, *, name, tb=ROW_TILE):
    t = p.shape[0]
    per_seq = seq // tb

    def kern(p_ref, pp_ref, pn_ref, mu_ref, cw_ref, pss_ref, oc_ref):
        i = pl.program_id(0)
        first = (i % per_seq) == 0
        last = (i % per_seq) == per_seq - 1
        ps = p_ref[:, :D_SP]
        down, up = _neighbours(ps, pp_ref[:, :D_SP], pn_ref[:, :D_SP], first, last)
        pss_ref[...] = ps + mu_ref[...] * (0.5 * (down + up) - ps)
        gb = p_ref[:, D_SP:D_SP + D_CONV]
        u = p_ref[:, D_SP + D_CONV:D_SP + 2 * D_CONV] * p_ref[:, D_SP + 2 * D_CONV:]
        u_p = pp_ref[:, D_SP + D_CONV:D_SP + 2 * D_CONV] * pp_ref[:, D_SP + 2 * D_CONV:]
        u_n = pn_ref[:, D_SP + D_CONV:D_SP + 2 * D_CONV] * pn_ref[:, D_SP + 2 * D_CONV:]
        udown, uup = _neighbours(u, u_p, u_n, first, last)
        oc_ref[...] = gb * (cw_ref[0:1, :] * udown + cw_ref[1:2, :] * u + cw_ref[2:3, :] * uup)

    return pl.pallas_call(
        kern,
        out_shape=[jax.ShapeDtypeStruct((t, D_SP), F32), jax.ShapeDtypeStruct((t, D_CONV), F32)],
        grid=(t // tb,),
        in_specs=_halo_specs(D_INP, 0, tb, t) + [pl.BlockSpec((1, D_SP), lambda i: (0, 0)),
                                                 pl.BlockSpec((SUBLANES, D_CONV), lambda i: (0, 0))],
        out_specs=[pl.BlockSpec((tb, D_SP), lambda i: (i, 0)), pl.BlockSpec((tb, D_CONV), lambda i: (i, 0))],
        compiler_params=_params(("parallel",)), name=name)(p, p, p, mu, conv_w)


def _shift_conv_bwd(p, d_pss, d_o, mu, conv_w, seq, *, name, tb=ROW_TILE):
    t = p.shape[0]
    per_seq = seq // tb

    def kern(p_ref, pp_ref, pn_ref, d_ref, dp_ref, dn_ref, do_ref, dop_ref, don_ref, mu_ref, cw_ref,
             out_ref, dmu_ref, dcw_ref):
        i = pl.program_id(0)
        first = (i % per_seq) == 0
        last = (i % per_seq) == per_seq - 1

        @pl.when(i == 0)
        def _():
            dmu_ref[...] = jnp.zeros_like(dmu_ref)
            dcw_ref[...] = jnp.zeros_like(dcw_ref)

        mu_v = mu_ref[...]
        ps = p_ref[:, :D_SP]
        down, up = _neighbours(ps, pp_ref[:, :D_SP], pn_ref[:, :D_SP], first, last)
        d = d_ref[...]
        ddown, dup = _neighbours(d, dp_ref[...], dn_ref[...], first, last)
        out_ref[:, :D_SP] = (d - mu_v * d + 0.5 * (mu_v * ddown + mu_v * dup)).astype(out_ref.dtype)
        dmu_ref[...] += _colsum(d * (0.5 * (down + up) - ps))

        def parts(ref):
            return (ref[:, D_SP:D_SP + D_CONV], ref[:, D_SP + D_CONV:D_SP + 2 * D_CONV],
                    ref[:, D_SP + 2 * D_CONV:])

        gb, gc, hh = parts(p_ref)
        gb_p, gc_p, hh_p = parts(pp_ref)
        gb_n, gc_n, hh_n = parts(pn_ref)
        u = gc * hh
        udown, uup = _neighbours(u, gc_p * hh_p, gc_n * hh_n, first, last)
        cw0, cw1, cw2 = cw_ref[0:1, :], cw_ref[1:2, :], cw_ref[2:3, :]
        do = do_ref[...]
        duc = do * gb
        ducdown, ducup = _neighbours(duc, dop_ref[...] * gb_p, don_ref[...] * gb_n, first, last)
        du = cw0 * ducup + cw1 * duc + cw2 * ducdown
        out_ref[:, D_SP:D_SP + D_CONV] = (do * (cw0 * udown + cw1 * u + cw2 * uup)).astype(out_ref.dtype)
        out_ref[:, D_SP + D_CONV:D_SP + 2 * D_CONV] = (du * hh).astype(out_ref.dtype)
        out_ref[:, D_SP + 2 * D_CONV:] = (du * gc).astype(out_ref.dtype)
        dcw_ref[0:1, :] += _colsum(duc * udown)
        dcw_ref[1:2, :] += _colsum(duc * u)
        dcw_ref[2:3, :] += _colsum(duc * uup)

    return pl.pallas_call(
        kern,
        out_shape=[jax.ShapeDtypeStruct((t, D_INP), BF16), jax.ShapeDtypeStruct((1, D_SP), F32),
                   jax.ShapeDtypeStruct((SUBLANES, D_CONV), F32)],
        grid=(t // tb,),
        in_specs=(_halo_specs(D_INP, 0, tb, t) + _halo_specs(D_SP, 0, tb, t) + _halo_specs(D_CONV, 1, tb, t)
                  + [pl.BlockSpec((1, D_SP), lambda i: (0, 0)),
                     pl.BlockSpec((SUBLANES, D_CONV), lambda i: (0, 0))]),
        out_specs=[pl.BlockSpec((tb, D_INP), lambda i: (i, 0)), pl.BlockSpec((1, D_SP), lambda i: (0, 0)),
                   pl.BlockSpec((SUBLANES, D_CONV), lambda i: (0, 0))],
        compiler_params=_params(("arbitrary",)), name=name)(p, p, p, d_pss, d_pss, d_pss, d_o, d_o, d_o, mu, conv_w)


N_CHAIN = 16
N_GROUP = LANES // N_CHAIN
V_HI = HEAD // SUBLANES
G_KK, G_R, G_W, G_B, G_KD = 0, 1, (2, 3), (4, 5), (6, 7)


K_HI = HEAD // SUBLANES


def _tree_sum(terms):
    terms = list(terms)
    while len(terms) > 1:
        terms = [a + b for a, b in zip(terms[::2], terms[1::2])]
    return terms[0]


def _kscan_specs(nc):
    same = lambda c: c
    mirror = lambda c: nc - 1 - c

    def k_spec(fn):
        return pl.BlockSpec((SCAN_CHUNK, HEAD, LANES), lambda c: (fn(c), 0, 0))

    def v_spec(fn):
        return pl.BlockSpec((SCAN_CHUNK, SUBLANES, LANES), lambda c: (fn(c), 0, 0))

    return same, mirror, k_spec, v_spec


ST_SHAPE = (2, K_HI, V_HI, SUBLANES, LANES)


def _lane_group_index():
    lane = lax.broadcasted_iota(jnp.int32, (SUBLANES, LANES), 1)
    return lax.shift_right_logical(lane, jnp.full_like(lane, 4))


def _spread_groups(x, grp):
    rolled = [x] + [pltpu.roll(x, s * N_CHAIN, 1) for s in range(1, N_GROUP)]
    out = []
    for j in range(N_GROUP):
        t = rolled[(0 - j) % N_GROUP]
        for g in range(1, N_GROUP):
            t = jnp.where(grp == g, rolled[(g - j) % N_GROUP], t)
        out.append(t)
    return out


def _gather_groups(tiles, grp):
    total = None
    for s in range(N_GROUP):
        b = tiles[s % N_GROUP]
        for g in range(1, N_GROUP):
            b = jnp.where(grp == g, tiles[(g + s) % N_GROUP], b)
        b = pltpu.roll(b, s * N_CHAIN, 1) if s else b
        total = b if total is None else total + b
    return total


def _lane_group_sum(x):
    return _tree_sum([x] + [pltpu.roll(x, k * N_CHAIN, 1) for k in range(1, N_GROUP)])


def _key_row(x_t, grp, kh):
    r = SUBLANES * grp + kh
    return jnp.broadcast_to(x_t[r:r + 1, :], (SUBLANES, LANES))


def _acc(total, term):
    return term if total is None else total + term


SA_SHAPE = (2, V_HI, SUBLANES, LANES)


def _scan_fwd(xall, v_c, *, gather=(), name):
    steps = xall.shape[0]
    nc = steps // SCAN_CHUNK
    same, mirror, k_spec, v_spec = _kscan_specs(nc)
    last = SCAN_CHUNK - 1
    n_x = len(gather)

    def kern(*refs):
        xf_ref, xb_ref, vf_ref, vb_ref = refs[:4]
        yf_ref, yb_ref, hist_ref, fin_ref, sa_ref = refs[4 + n_x:9 + n_x]
        st_ref = refs[9 + 2 * n_x]
        c = pl.program_id(0)

        def riders():
            return _exchange_copies(refs[4:4 + n_x], refs[9 + n_x:9 + 2 * n_x], 0, *refs[10 + 2 * n_x:])

        @pl.when(c == 0)
        def _():
            st_ref[...] = jnp.zeros_like(st_ref)
            if n_x:
                for cp in riders():
                    cp.start()

        hist_ref[0] = st_ref[...]
        grp = _lane_group_index()

        def body(i, put):
            j = last - i
            for d, (x_t, v_t, y_ref, at) in enumerate(((xf_ref[i], vf_ref[i], yf_ref, i),
                                                       (xb_ref[j], vb_ref[j], yb_ref, j))):
                v_b = _spread_groups(v_t, grp)
                part = [None] * V_HI
                for kh in range(K_HI):
                    kk_r = _key_row(x_t, G_KK, kh)
                    for vh in range(V_HI):
                        part[vh] = _acc(part[vh], hist_ref[i, d, kh, vh] * kk_r)
                sa = [_lane_group_sum(p) for p in part]
                for vh in range(V_HI):
                    sa_ref[i, d, vh] = sa[vh]
                y_p = [None] * V_HI
                for kh in range(K_HI):
                    r_r, w_r = _key_row(x_t, G_R, kh), _key_row(x_t, G_W[d], kh)
                    b_r, kd_r = _key_row(x_t, G_B[d], kh), _key_row(x_t, G_KD[d], kh)
                    for vh in range(V_HI):
                        new = hist_ref[i, d, kh, vh] * w_r - sa[vh] * b_r + v_b[vh] * kd_r
                        put(d, kh, vh, new)
                        y_p[vh] = _acc(y_p[vh], new * r_r)
                y_ref[at] = _gather_groups(y_p, grp)

        def step(i, carry):
            def put(d, kh, vh, val):
                hist_ref[i + 1, d, kh, vh] = val
            body(i, put)
            return carry

        lax.fori_loop(0, last, step, 0, unroll=SCAN_UNROLL)

        def put_carry(d, kh, vh, val):
            st_ref[d, kh, vh] = val

        body(last, put_carry)

        @pl.when(c == nc - 1)
        def _():
            fin_ref[...] = st_ref[...]
            if n_x:
                for cp in riders():
                    cp.wait()

    return pl.pallas_call(
        kern,
        out_shape=[jax.ShapeDtypeStruct((steps, SUBLANES, LANES), F32)] * 2
        + [jax.ShapeDtypeStruct((steps,) + ST_SHAPE, F32), jax.ShapeDtypeStruct(ST_SHAPE, F32),
           jax.ShapeDtypeStruct((steps,) + SA_SHAPE, F32)]
        + _exchange_out_shapes(gather, 0),
        grid=(nc,), in_specs=[k_spec(same), k_spec(mirror), v_spec(same), v_spec(mirror)] + _hbm_specs(n_x),
        out_specs=[v_spec(same), v_spec(mirror),
                   pl.BlockSpec((SCAN_CHUNK,) + ST_SHAPE, lambda c: (c, 0, 0, 0, 0, 0)),
                   pl.BlockSpec(ST_SHAPE, lambda c: (0, 0, 0, 0, 0)),
                   pl.BlockSpec((SCAN_CHUNK,) + SA_SHAPE, lambda c: (c, 0, 0, 0, 0))] + _hbm_specs(n_x),
        scratch_shapes=[pltpu.VMEM(ST_SHAPE, F32)] + (_exchange_sems(n_x) if n_x else []),
        compiler_params=_params(("arbitrary",), SCAN_VMEM_LIMIT), name=name)(xall, xall, v_c, v_c, *gather)


def _scan_bwd(xall, v_c, dy_c, hist, fin, sa, *, exchange=(), name):
    steps = xall.shape[0]
    nc = steps // SCAN_CHUNK
    same, back, k_spec, v_spec = _kscan_specs(nc)
    last = SCAN_CHUNK - 1
    n_x = len(exchange)

    def kern(*refs):
        xf_ref, xb_ref, vf_ref, vb_ref, dyf_ref, dyb_ref, hist_ref, fin_ref, sa_ref = refs[:9]
        gf_ref, gb_ref, dvf_ref, dvb_ref = refs[9 + n_x:13 + n_x]
        ds_ref, after_ref = refs[13 + 2 * n_x:15 + 2 * n_x]
        c = pl.program_id(0)

        def riders():
            return _exchange_copies(refs[9:9 + n_x], refs[13 + n_x:13 + 2 * n_x], n_x, *refs[15 + 2 * n_x:])

        @pl.when(c == 0)
        def _():
            ds_ref[...] = jnp.zeros_like(ds_ref)
            after_ref[...] = fin_ref[...]
            if n_x:
                for cp in riders():
                    cp.start()

        grp = _lane_group_index()
        row = lax.broadcasted_iota(jnp.int32, (SUBLANES, LANES), 0)
        zero = jnp.zeros((SUBLANES, LANES), F32)

        def body(i, after):
            j = last - i
            for d, (x_t, v_t, dy_t, g_ref, dv_ref, at) in enumerate((
                    (xf_ref[i], vf_ref[i], dyf_ref[i], gf_ref, dvf_ref, i),
                    (xb_ref[j], vb_ref[j], dyb_ref[j], gb_ref, dvb_ref, j))):
                v_s, dy_s = _spread_groups(v_t, grp), _spread_groups(dy_t, grp)
                dsa_p, dv_p = [None] * V_HI, [None] * V_HI
                for kh in range(K_HI):
                    r_r = _key_row(x_t, G_R, kh)
                    b_r, kd_r = _key_row(x_t, G_B[d], kh), _key_row(x_t, G_KD[d], kh)
                    for vh in range(V_HI):
                        g = ds_ref[d, kh, vh] + dy_s[vh] * r_r
                        ds_ref[d, kh, vh] = g
                        dsa_p[vh] = _acc(dsa_p[vh], g * b_r)
                        dv_p[vh] = _acc(dv_p[vh], g * kd_r)
                dsa = [-_lane_group_sum(p) for p in dsa_p]
                sa = [sa_ref[i, d, vh] for vh in range(V_HI)]
                dv_ref[at] = _gather_groups(dv_p, grp)
                blocks = {G_KK: zero, G_R: zero, G_W[d]: zero, G_B[d]: zero, G_KD[d]: zero}
                for kh in range(K_HI):
                    w_r, kk_r = _key_row(x_t, G_W[d], kh), _key_row(x_t, G_KK, kh)
                    dkk = dr = dw = db = dkd = None
                    for vh in range(V_HI):
                        g, before = ds_ref[d, kh, vh], hist_ref[i, d, kh, vh]
                        dr = _acc(dr, after(d, kh, vh) * dy_s[vh])
                        dw = _acc(dw, g * before)
                        dkd = _acc(dkd, g * v_s[vh])
                        db = _acc(db, g * sa[vh])
                        dkk = _acc(dkk, before * dsa[vh])
                        ds_ref[d, kh, vh] = g * w_r + dsa[vh] * kk_r
                    for gi, a in ((G_KK, dkk), (G_R, dr), (G_W[d], dw), (G_B[d], -db), (G_KD[d], dkd)):
                        blocks[gi] = jnp.where(row == kh, _colsum(a), blocks[gi])
                for gi in range(N_GROUP):
                    g_ref[at, SUBLANES * gi:SUBLANES * (gi + 1), :] = blocks.get(gi, zero)

        body(last, lambda d, kh, vh: after_ref[d, kh, vh])

        def step(ii, carry):
            i = last - ii
            body(i, lambda d, kh, vh: hist_ref[i + 1, d, kh, vh])
            return carry

        lax.fori_loop(1, SCAN_CHUNK, step, 0, unroll=SCAN_UNROLL)
        after_ref[...] = hist_ref[0]

        if n_x:
            @pl.when(c == nc - 1)
            def _():
                for cp in riders():
                    cp.wait()

    return pl.pallas_call(
        kern,
        out_shape=[jax.ShapeDtypeStruct((steps, HEAD, LANES), F32)] * 2
        + [jax.ShapeDtypeStruct((steps, SUBLANES, LANES), F32)] * 2 + _exchange_out_shapes(exchange, n_x),
        grid=(nc,),
        in_specs=[k_spec(back), k_spec(same), v_spec(back), v_spec(same), v_spec(back), v_spec(same),
                  pl.BlockSpec((SCAN_CHUNK,) + ST_SHAPE, lambda c: (back(c), 0, 0, 0, 0, 0)),
                  pl.BlockSpec(ST_SHAPE, lambda c: (0, 0, 0, 0, 0)),
                  pl.BlockSpec((SCAN_CHUNK,) + SA_SHAPE, lambda c: (back(c), 0, 0, 0, 0))] + _hbm_specs(n_x),
        out_specs=[k_spec(back), k_spec(same), v_spec(back), v_spec(same)] + _hbm_specs(n_x),
        scratch_shapes=[pltpu.VMEM(ST_SHAPE, F32), pltpu.VMEM(ST_SHAPE, F32)]
        + (_exchange_sems(n_x) if n_x else []),
        compiler_params=_params(("arbitrary",), SCAN_VMEM_LIMIT), name=name)(xall, xall, v_c, v_c, dy_c, dy_c, hist, fin, sa,
                                                            *exchange)


def _bf16_pieces(x):
    hi = x.astype(BF16)
    rest = x - hi.astype(F32)
    mid = rest.astype(BF16)
    return hi, mid, (rest - mid.astype(F32)).astype(BF16)


def _to_key_rows(wide, bsz, seq, *, name):
    assert bsz == 2
    perm = _key_row_maps()
    tt = min(RELAYOUT_TILE, seq)
    per_seq = seq // tt

    def kern(x0_ref, x1_ref, p0_ref, p1_ref, o_ref):
        total = None
        for x_ref, p_ref in ((x0_ref, p0_ref), (x1_ref, p1_ref)):
            for piece in _bf16_pieces(x_ref[...]):
                term = jnp.dot(piece, p_ref[...], preferred_element_type=F32)
                total = term if total is None else total + term
        for r in range(K_HI):
            o_ref[:, r, :] = total[:, LANES * r:LANES * (r + 1)]

    p_spec = pl.BlockSpec((D_RWKV, K_HI * LANES), lambda i, a: (0, 0))
    return pl.pallas_call(
        kern, out_shape=jax.ShapeDtypeStruct((seq, HEAD, LANES), F32), grid=(per_seq, N_GROUP),
        in_specs=[pl.BlockSpec((tt, D_RWKV), lambda i, a: (i, a)),
                  pl.BlockSpec((tt, D_RWKV), lambda i, a: (per_seq + i, a)), p_spec, p_spec],
        out_specs=pl.BlockSpec((tt, K_HI, LANES), lambda i, a: (i, a, 0)),
        compiler_params=_params(("parallel", "parallel")), name=name)(wide, wide, *perm)


def _key_row_maps():
    src = jnp.arange(D_RWKV)
    head, kh, kl = src // HEAD, (src // SUBLANES) % K_HI, src % SUBLANES
    dst = jnp.arange(K_HI * LANES)
    return [((kh[:, None] == dst[None, :] // LANES) & (kl[:, None] == (dst[None, :] // N_CHAIN) % SUBLANES)
             & ((dst[None, :] // N_HEAD) % 2 == b) & (head[:, None] == dst[None, :] % N_HEAD)).astype(BF16)
            for b in range(2)]


def _from_key_rows(g_f, g_b, bsz, seq, *, name):
    assert bsz == 2
    maps = jnp.stack([m.T for m in _key_row_maps()])
    tt = min(RELAYOUT_TILE, seq)
    per_seq = seq // tt

    def kern(gf_ref, gb_ref, q_ref, o_ref):
        a = pl.program_id(2)
        shared = a <= G_R
        from_f = shared | (a % 2 == G_W[0] % 2)

        def rearranged(g_ref):
            g = jnp.concatenate([g_ref[:, r, :] for r in range(K_HI)], axis=1)
            hi, mid = (jnp.dot(piece, q_ref[0], preferred_element_type=F32) for piece in _bf16_pieces(g)[:2])
            return hi + mid

        @pl.when(from_f)
        def _():
            o_ref[...] = rearranged(gf_ref)

        @pl.when(jnp.logical_not(from_f))
        def _():
            o_ref[...] = rearranged(gb_ref)

        @pl.when(shared)
        def _():
            o_ref[...] += rearranged(gb_ref)

    g_spec = pl.BlockSpec((tt, K_HI, LANES), lambda b, i, a: (i, a, 0))
    return pl.pallas_call(
        kern, out_shape=jax.ShapeDtypeStruct((bsz * seq, N_GROUP * D_RWKV), F32), grid=(bsz, per_seq, N_GROUP),
        in_specs=[g_spec, g_spec, pl.BlockSpec((1, K_HI * LANES, D_RWKV), lambda b, i, a: (b, 0, 0))],
        out_specs=pl.BlockSpec((tt, D_RWKV), lambda b, i, a: (b * per_seq + i, a)),
        compiler_params=_params(("parallel", "parallel", "parallel")), name=name)(g_f, g_b, maps)


def _to_value_rows(a, bsz, seq):
    z = a.reshape(bsz, seq, N_HEAD, V_HI, SUBLANES).transpose(1, 4, 3, 0, 2)
    return z.reshape(seq, SUBLANES, LANES)


def _from_value_rows(y, bsz, seq):
    z = y.reshape(seq, SUBLANES, V_HI, bsz, N_HEAD).transpose(3, 0, 4, 2, 1)
    return z.reshape(bsz * seq, D_RWKV)


def _pad_cols(a, segs):
    out, off = [], 0
    for w, wp in segs:
        out.append(a[..., off:off + w])
        if wp > w:
            out.append(jnp.zeros(a.shape[:-1] + (wp - w,), a.dtype))
        off += w
    return jnp.concatenate(out, axis=-1)


def _unpad_cols(a, segs):
    out, off = [], 0
    for w, wp in segs:
        out.append(a[..., off:off + w])
        off += wp
    return jnp.concatenate(out, axis=-1)


P_SEGS = ((3 * D_RWKV, 3 * D_RWKV), (D_LORA, 128), (D_LORA, 128), (D_GATE, 256), (3 * D_CONV, 3 * D_CONV))
S_SEGS = P_SEGS[:4]


def _pad_rows(a, rows):
    return jnp.concatenate([a, jnp.zeros((rows - a.shape[0], a.shape[1]), a.dtype)], axis=0)


LATE = ("w_out", "w_gate", "w_up", "w_down")


def _local_step(x, target, w, late=None):
    bsz, seq, _ = x.shape
    t = bsz * seq
    x2d = x.reshape(t, D_MODEL)
    tg2d = target.reshape(t, D_MODEL)
    row = lambda a: a.reshape(1, -1).astype(F32)

    w_in = _pad_cols(w["w_in"][0], P_SEGS)
    mu = _pad_cols(row(w["mu_shift"]), S_SEGS)
    wupf, wupb, aupf, aupb = (_pad_rows(w[n][0].astype(F32), 128) for n in ("w_up_f", "w_up_b", "a_up_f", "a_up_b"))
    gup = _pad_rows(w["g_up"][0].astype(F32), 256)
    conv_w = _pad_rows(w["conv_w"][0].astype(F32), SUBLANES)
    norm1, norm2, normf = row(w["norm1_w"]), row(w["norm2_w"]), row(w["norm_f_w"])
    vec = {n: row(w[n]) for n in VEC}
    head_of = jnp.arange(LANES) // HEAD
    bd = (head_of[:, None] == head_of[None, :]).astype(F32)
    pre_consts = [vec["k_k"], vec["w0_f"], vec["w0_b"], vec["a0_f"], vec["a0_b"], vec["k_a_f"], vec["k_a_b"],
                  wupf, wupb, aupf, aupb, gup, bd]
    post_consts = [vec["gn_w"], vec["gn_b"], vec["r_k_f"], vec["r_k_b"], bd]

    h1, = _rowwise(_rms, [x2d], [norm1], [D_MODEL], [], name="rms1_fwd", out_dtype=BF16, tb=WIDE_TILE)
    p = _mm(h1, w_in, name="mm_in")
    pss, oconv = _shift_conv_fwd(p, mu, conv_w, seq, name="shift_conv_fwd")
    pre_rows = [(pss, 0, 512), (pss, 1, 512), (pss, XW0 // 128, 128), (pss, XA0 // 128, 128), (pss, XG0 // 256, 256)]
    sc, g = _rowwise(_prescan_math, pre_rows, pre_consts, [[D_RWKV] * N_GROUP, D_RWKV], [], name="prescan_fwd")
    xall = _to_key_rows(sc, bsz, seq, name="to_key_rows")
    v_l = _to_value_rows(pss[:, 2 * D_RWKV:3 * D_RWKV], bsz, seq)
    y_f, y_b, hist, fin, sa, *gathered = _scan_fwd(xall, v_l, gather=[late[n] for n in LATE] if late else (),
                                                   name="scan_fwd")
    w_out, w_gate, w_up, w_down = (
        (_from_slots(a, SHARD_AXIS[n]) if late else w[n])[0] for n, a in zip(LATE, gathered or LATE))
    y = _from_value_rows(y_f + y_b, bsz, seq)
    post_rows = [y, (pss, 0, 512), (pss, 2, 512), (sc, G_KD[0], 512), (sc, G_KD[1], 512), g]

    def post_fwd(y_, r_, v_, kdf_, kdb_, g_, oc_, *consts):
        return _postscan_math(y_, r_, v_, kdf_, kdb_, g_, *consts), oc_

    o, = _rowwise(post_fwd, post_rows + [oconv], post_consts, [[D_RWKV, D_CONV]], [], name="postscan_fwd",
                  out_dtype=BF16)
    x1 = _mm(o, w_out, add=x2d, name="mm_out")
    h2, = _rowwise(_rms, [x1], [norm2], [D_MODEL], [], name="rms2_fwd", out_dtype=BF16, tb=WIDE_TILE)
    gg, uu, ff = _mm_swiglu(h2, w_gate, w_up, name="mm_gate_up")
    x2 = _mm(ff, w_down, add=x1, name="mm_down")

    def final(x_, tg_, wn_):
        yo, vjp = jax.vjp(_rms, x_, wn_)
        err = yo - tg_
        dx_, dwn_ = vjp(err * (1.0 / D_MODEL))
        part = jnp.sum(jnp.sum(err * err, axis=1, keepdims=True), axis=0, keepdims=True) * (0.5 / D_MODEL)
        return dx_, part + jnp.zeros((1, LANES), F32), dwn_

    dx2, loss_acc, d_normf = _rowwise(final, [x2, tg2d], [normf], [D_MODEL], [(1, LANES), (1, D_MODEL)],
                                      name="loss_head", tb=WIDE_TILE)
    dgg, duu = _mm_swiglu_bwd(dx2, w_down, gg, uu, name="mm_down_dx")
    g_w_down = _mm(ff, dx2, ta=True, name="mm_down_dw")
    dh2 = _mm(dgg, w_gate, tb=True, name="mm_gate_dx")
    dh2 = _mm(duu, w_up, tb=True, add=dh2, name="mm_up_dx")
    g_w_gate = _mm(h2, dgg, ta=True, name="mm_gate_dw")
    g_w_up = _mm(h2, duu, ta=True, name="mm_up_dw")

    def rms_bwd(x_, dh_, dres_, wn_):
        _, vjp = jax.vjp(_rms, x_, wn_)
        dx_, dwn_ = vjp(dh_)
        return dx_ + dres_, dwn_

    dx1, d_norm2 = _rowwise(rms_bwd, [x1, dh2, dx2], [norm2], [D_MODEL], [(1, D_MODEL)], name="rms2_bwd", tb=WIDE_TILE)
    do = _mm(dx1, w_out, tb=True, name="mm_out_dx")
    g_w_out = _mm(o, dx1, ta=True, name="mm_out_dw")

    def post_bwd(y_, r_, v_, kdf_, kdb_, g_, do_, *consts):
        _, vjp = jax.vjp(lambda *a: _postscan_math(*a, consts[4]), y_, r_, v_, kdf_, kdb_, g_, *consts[:4])
        return vjp(do_)

    (dy, dr_c, dv_c, dkdf_c, dkdb_c, dg, d_gn_w, d_gn_b, d_rkf, d_rkb) = _rowwise(
        post_bwd, post_rows + [(do, 0, 512)], post_consts, [D_RWKV] * 6, [(1, D_RWKV)] * 4, name="postscan_bwd")
    dy_l = _to_value_rows(dy, bsz, seq)
    late_grads = {"w_out": g_w_out[None], "w_gate": g_w_gate[None], "w_up": g_w_up[None], "w_down": g_w_down[None]}
    g_f, g_b, dv_f, dv_b, *late_parts = _scan_bwd(
        xall, v_l, dy_l, hist, fin, sa, name="scan_bwd",
        exchange=[_to_slots(late_grads[n], SHARD_AXIS[n]).astype(BF16) for n in LATE] if late else ())
    dsc = _from_key_rows(g_f, g_b, bsz, seq, name="from_key_rows")
    dv_s = _from_value_rows(dv_f + dv_b, bsz, seq)

    def pre_bwd(r_, k_, xw_, xa_, xg_, dkk_, dr_s, dwf_, dwb_, dbf_, dbb_, dkdf_s, dkdb_s,
                dr_c_, dv_c_, dv_s_, dkdf_c_, dkdb_c_, dg_, *consts):
        _, vjp = jax.vjp(lambda *a: _prescan_math(*a, consts[-1]), r_, k_, xw_, xa_, xg_, *consts[:-1])
        grads = vjp((dkk_, dr_s + dr_c_, dwf_, dwb_, dbf_, dbb_, dkdf_s + dkdf_c_, dkdb_s + dkdb_c_, dg_))
        dr_, dk_, dxw_, dxa_, dxg_ = grads[:5]
        return (dr_, dk_, dv_c_ + dv_s_, dxw_, dxa_, dxg_) + tuple(grads[5:])

    pre_b_rows = (pre_rows + [(dsc, j, 512) for j in range(N_GROUP)]
                  + [dr_c, dv_c, dv_s, dkdf_c, dkdb_c, dg])
    pre_b = _rowwise(pre_bwd, pre_b_rows, pre_consts, [[512, 512, 512, 128, 128, 256]],
                     [(1, D_RWKV)] * 7 + [(128, D_RWKV)] * 4 + [(256, D_RWKV)], name="prescan_bwd")
    d_pss = pre_b[0]
    d_kk_, d_w0f, d_w0b, d_a0f, d_a0b, d_kaf, d_kab, d_wupf, d_wupb, d_aupf, d_aupb, d_gup = pre_b[1:]
    dp, d_mu, d_conv = _shift_conv_bwd(p, d_pss, do, mu, conv_w, seq, name="shift_conv_bwd")
    g_w_in = _mm(h1, dp, ta=True, name="mm_in_dw")
    grads = {
        "w_in": _unpad_cols(g_w_in, P_SEGS)[None], "mu_shift": _unpad_cols(d_mu, S_SEGS),
        "w_up_f": d_wupf[None, :D_LORA], "w0_f": d_w0f, "w_up_b": d_wupb[None, :D_LORA], "w0_b": d_w0b,
        "a_up_f": d_aupf[None, :D_LORA], "a0_f": d_a0f, "a_up_b": d_aupb[None, :D_LORA], "a0_b": d_a0b,
        "g_up": d_gup[None, :D_GATE], "k_k": d_kk_, "k_a_f": d_kaf, "k_a_b": d_kab,
        "r_k_f": d_rkf, "r_k_b": d_rkb, "gn_w": d_gn_w, "gn_b": d_gn_b, "conv_w": d_conv[None, :3],
        "w_out": g_w_out[None], "norm2_w": d_norm2, "w_gate": g_w_gate[None], "w_up": g_w_up[None],
        "w_down": g_w_down[None], "norm_f_w": d_normf,
    }
    early = ("w_in",) + LORA
    parts = dict(zip(LATE, late_parts))
    if late:
        vec_rows = jnp.concatenate([grads[n] for n in VEC] + [jnp.zeros((16 - len(VEC), D_RWKV), F32)], axis=0)
        slots = [_to_slots(grads[n], SHARD_AXIS[n]).astype(BF16 if n in BIG else F32) for n in early]
        dh1, *recv = _mm(dp, w_in, tb=True, exchange=(slots, [vec_rows]), name="mm_in_dx")
        parts.update(zip(early + ("vec",), recv))
    else:
        dh1 = _mm(dp, w_in, tb=True, name="mm_in_dx")
    dx, grads["norm1_w"] = _rowwise(rms_bwd, [x2d, dh1, dx1], [norm1], [D_MODEL], [(1, D_MODEL)], name="rms1_bwd",
                                    tb=WIDE_TILE)
    return loss_acc, dx.reshape(bsz, seq, D_MODEL), grads, parts


def _hbm_specs(n):
    return [pl.BlockSpec(memory_space=pl.ANY)] * n


def _all_gather(arrs, *, name):
    n = len(arrs)

    def body(*refs):
        x_refs, out_refs = refs[:n], refs[n:2 * n]
        send_sems, recv_sems, local_sems = refs[2 * n:]
        x, y, c = lax.axis_index("x"), lax.axis_index("y"), lax.axis_index("c")
        me, sibling = (x, y, c), (x, y, 1 - c)
        chips = [(1 - x, y), (x, 1 - y), (1 - x, 1 - y)]

        def slot(a, px, py, pc):
            return out_refs[a].at[4 * px + 2 * py + pc]

        def copy(a, k, block, to, src=None):
            return pltpu.make_async_remote_copy(
                src_ref=slot(a, *block) if src is None else src, dst_ref=slot(a, *block),
                send_sem=send_sems.at[k, a], recv_sem=recv_sems.at[k, a],
                device_id=to, device_id_type=pl.DeviceIdType.MESH)

        mine = [pltpu.make_async_copy(x_refs[a], slot(a, *me), local_sems.at[a]) for a in range(n)]
        for cp in mine:
            cp.start()
        first = []
        for a in range(n):
            first.append(copy(a, 0, me, sibling, src=x_refs[a]))
            first += [copy(a, 1 + j, me, (*chip, c), src=x_refs[a]) for j, chip in enumerate(chips)]
        for cp in first:
            cp.start()
        passed = []
        for j, chip in enumerate(chips):
            for a in range(n):
                copy(a, 1 + j, (*chip, c), me).wait_recv()
                cp = copy(a, 4 + j, (*chip, c), sibling)
                cp.start()
                passed.append(cp)
        for a in range(n):
            copy(a, 0, sibling, me).wait_recv()
            for j, chip in enumerate(chips):
                copy(a, 4 + j, (*chip, 1 - c), me).wait_recv()
        for cp in first + passed:
            cp.wait_send()
        for cp in mine:
            cp.wait()

    return pl.pallas_call(
        body, out_shape=[jax.ShapeDtypeStruct((N_DEV,) + a.shape, a.dtype) for a in arrs],
        in_specs=_hbm_specs(n), out_specs=_hbm_specs(n),
        scratch_shapes=[pltpu.SemaphoreType.DMA((7, n)), pltpu.SemaphoreType.DMA((7, n)),
                        pltpu.SemaphoreType.DMA((n,))],
        name=name)(*arrs)


def _exchange(sliced, whole, *, name):
    arrs = list(sliced) + list(whole)
    n, n_sliced = len(arrs), len(sliced)

    def body(*refs):
        copies = _exchange_copies(refs[:n], refs[n:2 * n], n_sliced, *refs[2 * n:])
        for cp in copies:
            cp.start()
        for cp in copies:
            cp.wait()

    return pl.pallas_call(
        body, out_shape=_exchange_out_shapes(arrs, n_sliced), in_specs=_hbm_specs(n), out_specs=_hbm_specs(n),
        scratch_shapes=_exchange_sems(n), name=name)(*arrs)


def _exchange_out_shapes(arrs, n_sliced):
    return [jax.ShapeDtypeStruct(a.shape if i < n_sliced else (N_DEV,) + a.shape, a.dtype)
            for i, a in enumerate(arrs)]


def _exchange_sems(n):
    return [pltpu.SemaphoreType.DMA((7, n)), pltpu.SemaphoreType.DMA((7, n)), pltpu.SemaphoreType.DMA((n,))]


def _exchange_copies(in_refs, out_refs, n_sliced, send_sems, recv_sems, local_sems):
    n = len(in_refs)
    x, y, c = lax.axis_index("x"), lax.axis_index("y"), lax.axis_index("c")
    me = 4 * x + 2 * y + c

    def src(a, dev):
        return in_refs[a].at[dev] if a < n_sliced else in_refs[a]

    copies = [pltpu.make_async_copy(src(a, me), out_refs[a].at[me], local_sems.at[a]) for a in range(n)]
    for k in range(1, N_DEV):
        px = 1 - x if k & 4 else x
        py = 1 - y if k & 2 else y
        pc = 1 - c if k & 1 else c
        for a in range(n):
            copies.append(pltpu.make_async_remote_copy(
                src_ref=src(a, 4 * px + 2 * py + pc), dst_ref=out_refs[a].at[me],
                send_sem=send_sems.at[k - 1, a], recv_sem=recv_sems.at[k - 1, a],
                device_id=(px, py, pc), device_id_type=pl.DeviceIdType.MESH))
    return copies


def _adam_math(g, w, m, v):
    nm = ADAM_B1 * m + (1.0 - ADAM_B1) * g
    nv = ADAM_B2 * v + (1.0 - ADAM_B2) * (g * g)
    m_hat = nm / (1.0 - ADAM_B1 ** ADAM_STEP)
    v_hat = nv / (1.0 - ADAM_B2 ** ADAM_STEP)
    return -ADAM_LR * (m_hat / (jnp.sqrt(v_hat) + ADAM_EPS) + ADAM_WD * w), nm, nv


def _slot_sum(ref):
    g = ref[0].astype(F32)
    for s in range(1, N_DEV):
        g = g + ref[s].astype(F32)
    return g


def _adamw_big(parts, w, m, v, *, name):
    _, rws, cols = w.shape
    tr = _tile(rws, (256, 176, 128))

    def kern(p_ref, w_ref, m_ref, v_ref, g_ref, d_ref, nm_ref, nv_ref):
        g = _slot_sum(p_ref)
        g_ref[...] = g
        d_ref[...], nm_ref[...], nv_ref[...] = _adam_math(g, w_ref[...], m_ref[...], v_ref[...])

    spec = pl.BlockSpec((1, tr, cols), lambda i: (0, i, 0))
    return pl.pallas_call(
        kern, out_shape=[jax.ShapeDtypeStruct(w.shape, F32)] * 4, grid=(rws // tr,),
        in_specs=[pl.BlockSpec((N_DEV, 1, tr, cols), lambda i: (0, 0, i, 0)), spec, spec, spec],
        out_specs=[spec] * 4, compiler_params=_params(("parallel",)), name=name)(parts, w, m, v)


def _adamw_small(lora_parts, vec_parts, wide_parts, wmv, *, name):
    names = LORA + VEC + WIDE
    n_l, n = len(LORA), len(names)
    flat = [a for trip in wmv for a in trip]

    def kern(*refs):
        l_refs, vec_ref, wide_ref = refs[:n_l], refs[n_l], refs[n_l + 1]
        in_refs = refs[n_l + 2:n_l + 2 + 3 * n]
        out_refs = refs[n_l + 2 + 3 * n:]
        vec_sum, wide_sum = _slot_sum(vec_ref), _slot_sum(wide_ref)
        for i, nm in enumerate(names):
            w_ref, m_ref, v_ref = in_refs[3 * i:3 * i + 3]
            if i < n_l:
                g = _slot_sum(l_refs[i])
            elif nm in VEC:
                g = vec_sum[i - n_l:i - n_l + 1, :]
            else:
                g = wide_sum[WIDE.index(nm):WIDE.index(nm) + 1, :w_ref.shape[-1]]
            o = out_refs[4 * i:4 * i + 4]
            o[0][...] = g
            o[1][...], o[2][...], o[3][...] = _adam_math(g, w_ref[...], m_ref[...], v_ref[...])

    out_shape = [jax.ShapeDtypeStruct(trip[0].shape, F32) for trip in wmv for _ in range(4)]
    outs = pl.pallas_call(kern, out_shape=out_shape, name=name,
                          compiler_params=pltpu.CompilerParams(vmem_limit_bytes=VMEM_LIMIT))(
        *lora_parts, vec_parts, wide_parts, *flat)
    return [tuple(outs[4 * i:4 * i + 4]) for i in range(n)]


def _to_slots(g, axis):
    _, rws, cols = g.shape
    if axis == 1:
        return g.reshape(N_DEV, 1, rws // N_DEV, cols)
    return g.reshape(1, rws, N_DEV, cols // N_DEV).transpose(2, 0, 1, 3)


def _from_slots(got, axis):
    _, _, rws, cols = got.shape
    if axis == 1:
        return got.reshape(1, N_DEV * rws, cols)
    return got.transpose(1, 2, 0, 3).reshape(1, rws, N_DEV * cols)


def _pad_lanes(a, width):
    return jnp.concatenate([a, jnp.zeros(a.shape[:-1] + (width - a.shape[-1],), a.dtype)], axis=-1)


def kernel(x, norm1_w, w_in, mu_shift, w_up_f, w0_f, w_up_b, w0_b, a_up_f, a0_f, a_up_b, a0_b, g_up, k_k, k_a_f, k_a_b, r_k_f, r_k_b, gn_w, gn_b, conv_w, w_out, norm2_w, w_gate, w_up, w_down, norm_f_w, loss_target, m_norm1_w, m_w_in, m_mu_shift, m_w_up_f, m_w0_f, m_w_up_b, m_w0_b, m_a_up_f, m_a0_f, m_a_up_b, m_a0_b, m_g_up, m_k_k, m_k_a_f, m_k_a_b, m_r_k_f, m_r_k_b, m_gn_w, m_gn_b, m_conv_w, m_w_out, m_norm2_w, m_w_gate, m_w_up, m_w_down, m_norm_f_w, v_norm1_w, v_w_in, v_mu_shift, v_w_up_f, v_w0_f, v_w_up_b, v_w0_b, v_a_up_f, v_a0_f, v_a_up_b, v_a0_b, v_g_up, v_k_k, v_k_a_f, v_k_a_b, v_r_k_f, v_r_k_b, v_gn_w, v_gn_b, v_conv_w, v_w_out, v_norm2_w, v_w_gate, v_w_up, v_w_down, v_norm_f_w):
    local = dict(norm1_w=norm1_w, w_in=w_in, mu_shift=mu_shift, w_up_f=w_up_f, w0_f=w0_f, w_up_b=w_up_b,
                 w0_b=w0_b, a_up_f=a_up_f, a0_f=a0_f, a_up_b=a_up_b, a0_b=a0_b, g_up=g_up, k_k=k_k, k_a_f=k_a_f,
                 k_a_b=k_a_b, r_k_f=r_k_f, r_k_b=r_k_b, gn_w=gn_w, gn_b=gn_b, conv_w=conv_w, w_out=w_out,
                 norm2_w=norm2_w, w_gate=w_gate, w_up=w_up, w_down=w_down, norm_f_w=norm_f_w)
    mom_m = dict(norm1_w=m_norm1_w, w_in=m_w_in, mu_shift=m_mu_shift, w_up_f=m_w_up_f, w0_f=m_w0_f,
                 w_up_b=m_w_up_b, w0_b=m_w0_b, a_up_f=m_a_up_f, a0_f=m_a0_f, a_up_b=m_a_up_b, a0_b=m_a0_b,
                 g_up=m_g_up, k_k=m_k_k, k_a_f=m_k_a_f, k_a_b=m_k_a_b, r_k_f=m_r_k_f, r_k_b=m_r_k_b,
                 gn_w=m_gn_w, gn_b=m_gn_b, conv_w=m_conv_w, w_out=m_w_out, norm2_w=m_norm2_w, w_gate=m_w_gate,
                 w_up=m_w_up, w_down=m_w_down, norm_f_w=m_norm_f_w)
    mom_v = dict(norm1_w=v_norm1_w, w_in=v_w_in, mu_shift=v_mu_shift, w_up_f=v_w_up_f, w0_f=v_w0_f,
                 w_up_b=v_w_up_b, w0_b=v_w0_b, a_up_f=v_a_up_f, a0_f=v_a0_f, a_up_b=v_a_up_b, a0_b=v_a0_b,
                 g_up=v_g_up, k_k=v_k_k, k_a_f=v_k_a_f, k_a_b=v_k_a_b, r_k_f=v_r_k_f, r_k_b=v_r_k_b,
                 gn_w=v_gn_w, gn_b=v_gn_b, conv_w=v_conv_w, w_out=v_w_out, norm2_w=v_norm2_w, w_gate=v_w_gate,
                 w_up=v_w_up, w_down=v_w_down, norm_f_w=v_norm_f_w)

    early = ("w_in",) + LORA
    got = _all_gather([local["w_in"].astype(BF16)] + [local[n] for n in LORA], name="gather")
    full = dict(local)
    full.update({n: _from_slots(a, SHARD_AXIS[n]) for n, a in zip(early, got)})

    loss_part, grad_x, grads, parts = _local_step(x, loss_target, full,
                                                  late={n: local[n].astype(BF16) for n in LATE})

    wide_rows = jnp.concatenate([_pad_lanes(a, WIDE_ROW) for a in [grads[n] for n in WIDE] + [loss_part]]
                                + [jnp.zeros((SUBLANES - len(WIDE) - 1, WIDE_ROW), F32)], axis=0)
    wide_parts, = _exchange([], [wide_rows], name="grad_exchange")
    loss = jnp.sum(wide_parts[:, len(WIDE), 0])
    out = {}
    for n in BIG:
        out[n] = _adamw_big(parts[n], local[n], mom_m[n], mom_v[n], name="adamw_" + n)

    def small_form(n, a):
        if n in LORA:
            return a
        a = a.reshape(1, -1)
        return _pad_lanes(a, WIDE_ROW) if n == "mu_shift" else a

    small = LORA + VEC + WIDE
    res = _adamw_small([parts[n] for n in LORA], parts["vec"], wide_parts,
                       [tuple(small_form(n, d[n]) for d in (local, mom_m, mom_v)) for n in small],
                       name="adamw_small")
    for n, quad in zip(small, res):
        out[n] = tuple(a[..., :local[n].size].reshape(local[n].shape) if n not in LORA else a for a in quad)
    return (loss, grad_x, *[out[n][i] for i in range(4) for n in WEIGHTS])
```

```python
import functools

import jax
import jax.numpy as jnp
from jax import lax
from jax.experimental import pallas as pl
from jax.experimental.pallas import tpu as pltpu

F32 = jnp.float32
BF16 = jnp.bfloat16
HIGHEST = lax.Precision.HIGHEST

N_DEV = 8
D_MODEL = 1024
D_RWKV = 512
D_CONV = 512
HEAD = 64
N_HEAD = D_RWKV // HEAD
D_LORA = 64
D_GATE = 160
D_SHIFTED = 3 * D_RWKV + 2 * D_LORA + D_GATE
XW0, XA0, XG0 = 1536, 1664, 1792
D_SP = 2048
D_INP = D_SP + 3 * D_CONV
LOG_DECAY_SCALE = 0.606531
RMS_EPS = 1e-6
GN_EPS = 64e-5
NORM_EPS = 1e-12
ADAM_LR, ADAM_B1, ADAM_B2, ADAM_EPS, ADAM_WD, ADAM_STEP = 0.001, 0.9, 0.999, 1e-08, 0.01, 10

LANES = 128
SUBLANES = 8
VMEM_LIMIT = 48 * 1024 * 1024
SCAN_CHUNK = 32
SCAN_VMEM_LIMIT = 58 * 1024 * 1024
SCAN_UNROLL = 3
ROW_TILE = 128
WIDE_TILE = 512
RELAYOUT_TILE = 1024

BIG = ("w_in", "w_out", "w_gate", "w_up", "w_down")
LORA = ("w_up_f", "w_up_b", "a_up_f", "a_up_b", "g_up", "conv_w")
SHARD_AXIS = {"w_in": 2, "w_out": 1, "w_gate": 2, "w_up": 2, "w_down": 1, "w_up_f": 2, "w_up_b": 2,
              "a_up_f": 2, "a_up_b": 2, "g_up": 2, "conv_w": 2}
VEC = ("w0_f", "w0_b", "a0_f", "a0_b", "k_k", "k_a_f", "k_a_b", "r_k_f", "r_k_b", "gn_w", "gn_b")
WIDE = ("mu_shift", "norm1_w", "norm2_w", "norm_f_w")
WIDE_ROW = 2048
WEIGHTS = ("norm1_w", "w_in", "mu_shift", "w_up_f", "w0_f", "w_up_b", "w0_b", "a_up_f", "a0_f", "a_up_b",
           "a0_b", "g_up", "k_k", "k_a_f", "k_a_b", "r_k_f", "r_k_b", "gn_w", "gn_b", "conv_w", "w_out",
           "norm2_w", "w_gate", "w_up", "w_down", "norm_f_w")


def _params(sem, limit=VMEM_LIMIT):
    return pltpu.CompilerParams(dimension_semantics=sem, vmem_limit_bytes=limit)


def _tile(n, cands):
    for c in cands:
        if n % c == 0:
            return c
    raise ValueError(f"no tile for {n}")


def _mm(a, b, *, ta=False, tb=False, add=None, exchange=None, name):
    (k_dim, m) = a.shape if ta else a.shape[::-1]
    (k2, n) = b.shape[::-1] if tb else b.shape
    assert k_dim == k2, (a.shape, b.shape, ta, tb)
    tm = _tile(m, (1408, 1024, 512, 256, 128))
    tn = _tile(n, (1408, 1024, 896, 512, 256, 128))
    tk = _tile(k_dim, (1408, 1024, 896, 512, 256, 128))
    nk = k_dim // tk
    grid = (m // tm, n // tn, nk)
    dims = (((0 if ta else 1,), (1 if tb else 0,)), ((), ()))
    sliced, whole = exchange or ((), ())
    riders = list(sliced) + list(whole)
    n_x, n_in = len(riders), 2 + (add is not None)

    def kern(*refs):
        a_ref, b_ref = refs[:2]
        add_ref = refs[2] if add is not None else None
        o_ref, acc_ref = refs[n_in + n_x], refs[n_in + 2 * n_x + 1]
        k = pl.program_id(2)
        step = (pl.program_id(0) * grid[1] + pl.program_id(1)) * nk + k

        def copies():
            return _exchange_copies(refs[n_in:n_in + n_x], refs[n_in + n_x + 1:n_in + 2 * n_x + 1], len(sliced),
                                    *refs[n_in + 2 * n_x + 2:])

        if n_x:
            @pl.when(step == 0)
            def _():
                for cp in copies():
                    cp.start()

        @pl.when(k == 0)
        def _():
            acc_ref[...] = jnp.zeros_like(acc_ref)

        acc_ref[...] += lax.dot_general(a_ref[...].astype(BF16), b_ref[...].astype(BF16), dims,
                                        preferred_element_type=F32)

        @pl.when(k == nk - 1)
        def _():
            if add is None:
                o_ref[...] = acc_ref[...]
            else:
                o_ref[...] = acc_ref[...] + add_ref[...]

        if n_x:
            @pl.when(step == grid[0] * grid[1] * nk - 1)
            def _():
                for cp in copies():
                    cp.wait()

    a_spec = (pl.BlockSpec((tk, tm), lambda i, j, k: (k, i)) if ta
              else pl.BlockSpec((tm, tk), lambda i, j, k: (i, k)))
    b_spec = (pl.BlockSpec((tn, tk), lambda i, j, k: (j, k)) if tb
              else pl.BlockSpec((tk, tn), lambda i, j, k: (k, j)))
    o_spec = pl.BlockSpec((tm, tn), lambda i, j, k: (i, j))
    in_specs = [a_spec, b_spec] + ([o_spec] if add is not None else []) + _hbm_specs(n_x)
    args = (a, b) + ((add,) if add is not None else ()) + tuple(riders)
    out = pl.pallas_call(
        kern, out_shape=[jax.ShapeDtypeStruct((m, n), F32)] + _exchange_out_shapes(riders, len(sliced)), grid=grid,
        in_specs=in_specs, out_specs=[o_spec] + _hbm_specs(n_x),
        scratch_shapes=[pltpu.VMEM((tm, tn), F32)] + (_exchange_sems(n_x) if n_x else []),
        compiler_params=_params(("arbitrary",) * 3 if n_x else ("parallel", "parallel", "arbitrary")),
        name=name)(*args)
    return out if n_x else out[0]


def _swiglu(g, u):
    return jax.nn.silu(g) * u


FFN_TN = 256


def _mm_swiglu(h, w_gate, w_up, *, name):
    m, k_dim = h.shape
    n = w_gate.shape[1]
    tm = _tile(m, (1024, 512, 256, 128))

    def kern(h_ref, wg_ref, wu_ref, g_ref, u_ref, f_ref):
        hv = h_ref[...].astype(BF16)
        g = jnp.dot(hv, wg_ref[...].astype(BF16), preferred_element_type=F32)
        u = jnp.dot(hv, wu_ref[...].astype(BF16), preferred_element_type=F32)
        g_ref[...] = g
        u_ref[...] = u
        f_ref[...] = _swiglu(g, u).astype(f_ref.dtype)

    w_spec = pl.BlockSpec((k_dim, FFN_TN), lambda i, j: (0, j))
    o_spec = pl.BlockSpec((tm, FFN_TN), lambda i, j: (i, j))
    return pl.pallas_call(
        kern, out_shape=[jax.ShapeDtypeStruct((m, n), F32)] * 2 + [jax.ShapeDtypeStruct((m, n), BF16)],
        grid=(m // tm, n // FFN_TN), in_specs=[pl.BlockSpec((tm, k_dim), lambda i, j: (i, 0)), w_spec, w_spec],
        out_specs=[o_spec] * 3, compiler_params=_params(("parallel", "parallel")), name=name)(h, w_gate, w_up)


def _mm_swiglu_bwd(dx, w_down, g, u, *, name):
    m, k_dim = dx.shape
    n = w_down.shape[0]
    tm = _tile(m, (1024, 512, 256, 128))

    def kern(dx_ref, w_ref, g_ref, u_ref, dg_ref, du_ref):
        df = lax.dot_general(dx_ref[...].astype(BF16), w_ref[...].astype(BF16), (((1,), (1,)), ((), ())),
                             preferred_element_type=F32)
        _, vjp = jax.vjp(_swiglu, g_ref[...], u_ref[...])
        dg, du = vjp(df)
        dg_ref[...] = dg.astype(dg_ref.dtype)
        du_ref[...] = du.astype(du_ref.dtype)

    o_spec = pl.BlockSpec((tm, FFN_TN), lambda i, j: (i, j))
    return pl.pallas_call(
        kern, out_shape=[jax.ShapeDtypeStruct((m, n), BF16)] * 2, grid=(m // tm, n // FFN_TN),
        in_specs=[pl.BlockSpec((tm, k_dim), lambda i, j: (i, 0)), pl.BlockSpec((FFN_TN, k_dim), lambda i, j: (j, 0)),
                  o_spec, o_spec],
        out_specs=[o_spec] * 2, compiler_params=_params(("parallel", "parallel")), name=name)(dx, w_down, g, u)


def _rowwise(fn, rows, consts, out_rows, out_accs, *, name, tb=ROW_TILE, out_dtype=F32):
    t = (rows[0][0] if isinstance(rows[0], tuple) else rows[0]).shape[0]
    tb = min(tb, t)
    n_r, n_c, n_o, n_a = len(rows), len(consts), len(out_rows), len(out_accs)
    pieces = [w if isinstance(w, (list, tuple)) else [w] for w in out_rows]

    def kern(*refs):
        r_refs = refs[:n_r]
        c_refs = refs[n_r:n_r + n_c]
        o_refs = refs[n_r + n_c:n_r + n_c + n_o]
        a_refs = refs[n_r + n_c + n_o:]
        vals = fn(*[r[...] for r in r_refs], *[c[...] for c in c_refs])
        vals = list(vals) if isinstance(vals, (tuple, list)) else [vals]
        pos = 0
        for o_ref, ws in zip(o_refs, pieces):
            off = 0
            for w in ws:
                o_ref[:, off:off + w] = vals[pos].astype(o_ref.dtype)
                off += w
                pos += 1
        if n_a:
            @pl.when(pl.program_id(0) == 0)
            def _():
                for a_ref in a_refs:
                    a_ref[...] = jnp.zeros_like(a_ref)
            for a_ref, v in zip(a_refs, vals[pos:]):
                a_ref[...] += v

    in_specs, args = [], []
    for r in rows:
        if isinstance(r, tuple):
            arr, blk, w = r
            in_specs.append(pl.BlockSpec((tb, w), functools.partial(lambda i, blk: (i, blk), blk=blk)))
        else:
            arr = r
            in_specs.append(pl.BlockSpec((tb, arr.shape[1]), lambda i: (i, 0)))
        args.append(arr)
    for c in consts:
        in_specs.append(pl.BlockSpec(c.shape, lambda i: (0, 0)))
        args.append(c)
    out_shape = [jax.ShapeDtypeStruct((t, sum(ws)), out_dtype) for ws in pieces]
    out_specs = [pl.BlockSpec((tb, sum(ws)), lambda i: (i, 0)) for ws in pieces]
    for shp in out_accs:
        out_shape.append(jax.ShapeDtypeStruct(shp, F32))
        out_specs.append(pl.BlockSpec(shp, lambda i: (0, 0)))
    res = pl.pallas_call(
        kern, out_shape=out_shape, grid=(t // tb,), in_specs=in_specs, out_specs=out_specs,
        compiler_params=_params(("arbitrary",) if n_a else ("parallel",)), name=name)(*args)
    return res


def _rms(x, w):
    return x * lax.rsqrt(jnp.mean(x * x, axis=-1, keepdims=True) + RMS_EPS) * w


def _seg_sum(x, bd):
    return jnp.concatenate(
        [jnp.dot(x[:, LANES * j:LANES * (j + 1)], bd, precision=HIGHEST, preferred_element_type=F32)
         for j in range(x.shape[1] // LANES)], axis=1)


@jax.custom_vjp
def _seg(x, bd):
    return _seg_sum(x, bd)


_seg.defvjp(lambda x, bd: (_seg_sum(x, bd), bd), lambda bd, ct: (_seg_sum(ct, bd), jnp.zeros_like(bd)))


def _colsum(x):
    return jnp.sum(x, axis=0, keepdims=True)


def _prescan_math(r, k, xw, xa, xg, k_k, w0f, w0b, a0f, a0b, kaf, kab, wupf, wupb, aupf, aupb, gup, bd):
    kkr = k * k_k
    norm = jnp.sqrt(_seg(kkr * kkr, bd))
    kk = kkr / jnp.maximum(norm, NORM_EPS)
    th = jnp.tanh(xw)

    def direction(w0, wup, a0, aup, ka):
        logit = w0 + jnp.dot(th, wup, preferred_element_type=F32)
        w = jnp.exp(-LOG_DECAY_SCALE * jax.nn.sigmoid(logit))
        a = jax.nn.sigmoid(a0 + jnp.dot(xa, aup, preferred_element_type=F32))
        kd = k * (1.0 + (a - 1.0) * ka)
        return w, kd, kk * a

    wf, kdf, bf = direction(w0f, wupf, a0f, aupf, kaf)
    wb, kdb, bb = direction(w0b, wupb, a0b, aupb, kab)
    g = jnp.dot(jax.nn.sigmoid(xg), gup, preferred_element_type=F32)
    return kk, r, wf, wb, bf, bb, kdf, kdb, g


def _postscan_math(y, r, v, kdf, kdb, g, gn_w, gn_b, rkf, rkb, bd):
    mean = _seg(y, bd) * (1.0 / HEAD)
    yc = y - mean
    var = _seg(yc * yc, bd) * (1.0 / HEAD)
    yg = yc * lax.rsqrt(var + GN_EPS) * gn_w + gn_b
    bonus = (_seg(r * kdf * rkf, bd) + _seg(r * kdb * rkb, bd)) * v
    return (yg + bonus) * g


def _halo_specs(width, col_blk, tb, t):
    nb = t // SUBLANES
    step = tb // SUBLANES
    main = pl.BlockSpec((tb, width), lambda i: (i, col_blk))
    prev = pl.BlockSpec((SUBLANES, width), lambda i: (jnp.maximum(i * step - 1, 0), col_blk))
    nxt = pl.BlockSpec((SUBLANES, width), lambda i: (jnp.minimum((i + 1) * step, nb - 1), col_blk))
    return [main, prev, nxt]


def _neighbours(z, prev8, next8, first, last):
    tb = z.shape[0]
    row = lax.broadcasted_iota(jnp.int32, z.shape, 0)
    prow = jnp.where(first, 0.0, prev8[SUBLANES - 1:SUBLANES, :])
    nrow = jnp.where(last, 0.0, next8[0:1, :])
    down = jnp.where(row == 0, prow, pltpu.roll(z, 1, 0))
    up = jnp.where(row == tb - 1, nrow, pltpu.roll(z, tb - 1, 0))
    return down, up


def _shift_conv_fwd(p, mu, conv_w, seq, *, name, tb=ROW_TILE):
    t = p.shape[0]
    per_seq = seq // tb

    def kern(p_ref, pp_ref, pn_ref, mu_ref, cw_ref, pss_ref, oc_ref):
        i = pl.program_id(0)
        first = (i % per_seq) == 0
        last = (i % per_seq) == per_seq - 1
        ps = p_ref[:, :D_SP]
        down, up = _neighbours(ps, pp_ref[:, :D_SP], pn_ref[:, :D_SP], first, last)
        pss_ref[...] = ps + mu_ref[...] * (0.5 * (down + up) - ps)
        gb = p_ref[:, D_SP:D_SP + D_CONV]
        u = p_ref[:, D_SP + D_CONV:D_SP + 2 * D_CONV] * p_ref[:, D_SP + 2 * D_CONV:]
        u_p = pp_ref[:, D_SP + D_CONV:D_SP + 2 * D_CONV] * pp_ref[:, D_SP + 2 * D_CONV:]
        u_n = pn_ref[:, D_SP + D_CONV:D_SP + 2 * D_CONV] * pn_ref[:, D_SP + 2 * D_CONV:]
        udown, uup = _neighbours(u, u_p, u_n, first, last)
        oc_ref[...] = gb * (cw_ref[0:1, :] * udown + cw_ref[1:2, :] * u + cw_ref[2:3, :] * uup)

    return pl.pallas_call(
        kern,
        out_shape=[jax.ShapeDtypeStruct((t, D_SP), F32), jax.ShapeDtypeStruct((t, D_CONV), F32)],
        grid=(t // tb,),
        in_specs=_halo_specs(D_INP, 0, tb, t) + [pl.BlockSpec((1, D_SP), lambda i: (0, 0)),
                                                 pl.BlockSpec((SUBLANES, D_CONV), lambda i: (0, 0))],
        out_specs=[pl.BlockSpec((tb, D_SP), lambda i: (i, 0)), pl.BlockSpec((tb, D_CONV), lambda i: (i, 0))],
        compiler_params=_params(("parallel",)), name=name)(p, p, p, mu, conv_w)


def _shift_conv_bwd(p, d_pss, d_o, mu, conv_w, seq, *, name, tb=ROW_TILE):
    t = p.shape[0]
    per_seq = seq // tb

    def kern(p_ref, pp_ref, pn_ref, d_ref, dp_ref, dn_ref, do_ref, dop_ref, don_ref, mu_ref, cw_ref,
             out_ref, dmu_ref, dcw_ref):
        i = pl.program_id(0)
        first = (i % per_seq) == 0
        last = (i % per_seq) == per_seq - 1

        @pl.when(i == 0)
        def _():
            dmu_ref[...] = jnp.zeros_like(dmu_ref)
            dcw_ref[...] = jnp.zeros_like(dcw_ref)

        mu_v = mu_ref[...]
        ps = p_ref[:, :D_SP]
        down, up = _neighbours(ps, pp_ref[:, :D_SP], pn_ref[:, :D_SP], first, last)
        d = d_ref[...]
        ddown, dup = _neighbours(d, dp_ref[...], dn_ref[...], first, last)
        out_ref[:, :D_SP] = (d - mu_v * d + 0.5 * (mu_v * ddown + mu_v * dup)).astype(out_ref.dtype)
        dmu_ref[...] += _colsum(d * (0.5 * (down + up) - ps))

        def parts(ref):
            return (ref[:, D_SP:D_SP + D_CONV], ref[:, D_SP + D_CONV:D_SP + 2 * D_CONV],
                    ref[:, D_SP + 2 * D_CONV:])

        gb, gc, hh = parts(p_ref)
        gb_p, gc_p, hh_p = parts(pp_ref)
        gb_n, gc_n, hh_n = parts(pn_ref)
        u = gc * hh
        udown, uup = _neighbours(u, gc_p * hh_p, gc_n * hh_n, first, last)
        cw0, cw1, cw2 = cw_ref[0:1, :], cw_ref[1:2, :], cw_ref[2:3, :]
        do = do_ref[...]
        duc = do * gb
        ducdown, ducup = _neighbours(duc, dop_ref[...] * gb_p, don_ref[...] * gb_n, first, last)
        du = cw0 * ducup + cw1 * duc + cw2 * ducdown
        out_ref[:, D_SP:D_SP + D_CONV] = (do * (cw0 * udown + cw1 * u + cw2 * uup)).astype(out_ref.dtype)
        out_ref[:, D_SP + D_CONV:D_SP + 2 * D_CONV] = (du * hh).astype(out_ref.dtype)
        out_ref[:, D_SP + 2 * D_CONV:] = (du * gc).astype(out_ref.dtype)
        dcw_ref[0:1, :] += _colsum(duc * udown)
        dcw_ref[1:2, :] += _colsum(duc * u)
        dcw_ref[2:3, :] += _colsum(duc * uup)

    return pl.pallas_call(
        kern,
        out_shape=[jax.ShapeDtypeStruct((t, D_INP), BF16), jax.ShapeDtypeStruct((1, D_SP), F32),
                   jax.ShapeDtypeStruct((SUBLANES, D_CONV), F32)],
        grid=(t // tb,),
        in_specs=(_halo_specs(D_INP, 0, tb, t) + _halo_specs(D_SP, 0, tb, t) + _halo_specs(D_CONV, 1, tb, t)
                  + [pl.BlockSpec((1, D_SP), lambda i: (0, 0)),
                     pl.BlockSpec((SUBLANES, D_CONV), lambda i: (0, 0))]),
        out_specs=[pl.BlockSpec((tb, D_INP), lambda i: (i, 0)), pl.BlockSpec((1, D_SP), lambda i: (0, 0)),
                   pl.BlockSpec((SUBLANES, D_CONV), lambda i: (0, 0))],
        compiler_params=_params(("arbitrary",)), name=name)(p, p, p, d_pss, d_pss, d_pss, d_o, d_o, d_o, mu, conv_w)


N_CHAIN = 16
N_GROUP = LANES // N_CHAIN
V_HI = HEAD // SUBLANES
G_KK, G_R, G_W, G_B, G_KD = 0, 1, (2, 3), (4, 5), (6, 7)


K_HI = HEAD // SUBLANES


def _tree_sum(terms):
    terms = list(terms)
    while len(terms) > 1:
        terms = [a + b for a, b in zip(terms[::2], terms[1::2])]
    return terms[0]


def _kscan_specs(nc):
    same = lambda c: c
    mirror = lambda c: nc - 1 - c

    def k_spec(fn):
        return pl.BlockSpec((SCAN_CHUNK, HEAD, LANES), lambda c: (fn(c), 0, 0))

    def v_spec(fn):
        return pl.BlockSpec((SCAN_CHUNK, SUBLANES, LANES), lambda c: (fn(c), 0, 0))

    return same, mirror, k_spec, v_spec


ST_SHAPE = (2, K_HI, V_HI, SUBLANES, LANES)


def _lane_group_index():
    lane = lax.broadcasted_iota(jnp.int32, (SUBLANES, LANES), 1)
    return lax.shift_right_logical(lane, jnp.full_like(lane, 4))


def _spread_groups(x, grp):
    rolled = [x] + [pltpu.roll(x, s * N_CHAIN, 1) for s in range(1, N_GROUP)]
    out = []
    for j in range(N_GROUP):
        t = rolled[(0 - j) % N_GROUP]
        for g in range(1, N_GROUP):
            t = jnp.where(grp == g, rolled[(g - j) % N_GROUP], t)
        out.append(t)
    return out


def _gather_groups(tiles, grp):
    total = None
    for s in range(N_GROUP):
        b = tiles[s % N_GROUP]
        for g in range(1, N_GROUP):
            b = jnp.where(grp == g, tiles[(g + s) % N_GROUP], b)
        b = pltpu.roll(b, s * N_CHAIN, 1) if s else b
        total = b if total is None else total + b
    return total


def _lane_group_sum(x):
    return _tree_sum([x] + [pltpu.roll(x, k * N_CHAIN, 1) for k in range(1, N_GROUP)])


def _key_row(x_t, grp, kh):
    r = SUBLANES * grp + kh
    return jnp.broadcast_to(x_t[r:r + 1, :], (SUBLANES, LANES))


def _acc(total, term):
    return term if total is None else total + term


SA_SHAPE = (2, V_HI, SUBLANES, LANES)


def _scan_fwd(xall, v_c, *, gather=(), name):
    steps = xall.shape[0]
    nc = steps // SCAN_CHUNK
    same, mirror, k_spec, v_spec = _kscan_specs(nc)
    last = SCAN_CHUNK - 1
    n_x = len(gather)

    def kern(*refs):
        xf_ref, xb_ref, vf_ref, vb_ref = refs[:4]
        yf_ref, yb_ref, hist_ref, fin_ref, sa_ref = refs[4 + n_x:9 + n_x]
        st_ref = refs[9 + 2 * n_x]
        c = pl.program_id(0)

        def riders():
            return _exchange_copies(refs[4:4 + n_x], refs[9 + n_x:9 + 2 * n_x], 0, *refs[10 + 2 * n_x:])

        @pl.when(c == 0)
        def _():
            st_ref[...] = jnp.zeros_like(st_ref)
            if n_x:
                for cp in riders():
                    cp.start()

        hist_ref[0] = st_ref[...]
        grp = _lane_group_index()

        def body(i, put):
            j = last - i
            for d, (x_t, v_t, y_ref, at) in enumerate(((xf_ref[i], vf_ref[i], yf_ref, i),
                                                       (xb_ref[j], vb_ref[j], yb_ref, j))):
                v_b = _spread_groups(v_t, grp)
                part = [None] * V_HI
                for kh in range(K_HI):
                    kk_r = _key_row(x_t, G_KK, kh)
                    for vh in range(V_HI):
                        part[vh] = _acc(part[vh], hist_ref[i, d, kh, vh] * kk_r)
                sa = [_lane_group_sum(p) for p in part]
                for vh in range(V_HI):
                    sa_ref[i, d, vh] = sa[vh]
                y_p = [None] * V_HI
                for kh in range(K_HI):
                    r_r, w_r = _key_row(x_t, G_R, kh), _key_row(x_t, G_W[d], kh)
                    b_r, kd_r = _key_row(x_t, G_B[d], kh), _key_row(x_t, G_KD[d], kh)
                    for vh in range(V_HI):
                        new = hist_ref[i, d, kh, vh] * w_r - sa[vh] * b_r + v_b[vh] * kd_r
                        put(d, kh, vh, new)
                        y_p[vh] = _acc(y_p[vh], new * r_r)
                y_ref[at] = _gather_groups(y_p, grp)

        def step(i, carry):
            def put(d, kh, vh, val):
                hist_ref[i + 1, d, kh, vh] = val
            body(i, put)
            return carry

        lax.fori_loop(0, last, step, 0, unroll=SCAN_UNROLL)

        def put_carry(d, kh, vh, val):
            st_ref[d, kh, vh] = val

        body(last, put_carry)

        @pl.when(c == nc - 1)
        def _():
            fin_ref[...] = st_ref[...]
            if n_x:
                for cp in riders():
                    cp.wait()

    return pl.pallas_call(
        kern,
        out_shape=[jax.ShapeDtypeStruct((steps, SUBLANES, LANES), F32)] * 2
        + [jax.ShapeDtypeStruct((steps,) + ST_SHAPE, F32), jax.ShapeDtypeStruct(ST_SHAPE, F32),
           jax.ShapeDtypeStruct((steps,) + SA_SHAPE, F32)]
        + _exchange_out_shapes(gather, 0),
        grid=(nc,), in_specs=[k_spec(same), k_spec(mirror), v_spec(same), v_spec(mirror)] + _hbm_specs(n_x),
        out_specs=[v_spec(same), v_spec(mirror),
                   pl.BlockSpec((SCAN_CHUNK,) + ST_SHAPE, lambda c: (c, 0, 0, 0, 0, 0)),
                   pl.BlockSpec(ST_SHAPE, lambda c: (0, 0, 0, 0, 0)),
                   pl.BlockSpec((SCAN_CHUNK,) + SA_SHAPE, lambda c: (c, 0, 0, 0, 0))] + _hbm_specs(n_x),
        scratch_shapes=[pltpu.VMEM(ST_SHAPE, F32)] + (_exchange_sems(n_x) if n_x else []),
        compiler_params=_params(("arbitrary",), SCAN_VMEM_LIMIT), name=name)(xall, xall, v_c, v_c, *gather)


def _scan_bwd(xall, v_c, dy_c, hist, fin, sa, *, exchange=(), name):
    steps = xall.shape[0]
    nc = steps // SCAN_CHUNK
    same, back, k_spec, v_spec = _kscan_specs(nc)
    last = SCAN_CHUNK - 1
    n_x = len(exchange)

    def kern(*refs):
        xf_ref, xb_ref, vf_ref, vb_ref, dyf_ref, dyb_ref, hist_ref, fin_ref, sa_ref = refs[:9]
        gf_ref, gb_ref, dvf_ref, dvb_ref = refs[9 + n_x:13 + n_x]
        ds_ref, after_ref = refs[13 + 2 * n_x:15 + 2 * n_x]
        c = pl.program_id(0)

        def riders():
            return _exchange_copies(refs[9:9 + n_x], refs[13 + n_x:13 + 2 * n_x], n_x, *refs[15 + 2 * n_x:])

        @pl.when(c == 0)
        def _():
            ds_ref[...] = jnp.zeros_like(ds_ref)
            after_ref[...] = fin_ref[...]
            if n_x:
                for cp in riders():
                    cp.start()

        grp = _lane_group_index()
        row = lax.broadcasted_iota(jnp.int32, (SUBLANES, LANES), 0)
        zero = jnp.zeros((SUBLANES, LANES), F32)

        def body(i, after):
            j = last - i
            for d, (x_t, v_t, dy_t, g_ref, dv_ref, at) in enumerate((
                    (xf_ref[i], vf_ref[i], dyf_ref[i], gf_ref, dvf_ref, i),
                    (xb_ref[j], vb_ref[j], dyb_ref[j], gb_ref, dvb_ref, j))):
                v_s, dy_s = _spread_groups(v_t, grp), _spread_groups(dy_t, grp)
                dsa_p, dv_p = [None] * V_HI, [None] * V_HI
                for kh in range(K_HI):
                    r_r = _key_row(x_t, G_R, kh)
                    b_r, kd_r = _key_row(x_t, G_B[d], kh), _key_row(x_t, G_KD[d], kh)
                    for vh in range(V_HI):
                        g = ds_ref[d, kh, vh] + dy_s[vh] * r_r
                        ds_ref[d, kh, vh] = g
                        dsa_p[vh] = _acc(dsa_p[vh], g * b_r)
                        dv_p[vh] = _acc(dv_p[vh], g * kd_r)
                dsa = [-_lane_group_sum(p) for p in dsa_p]
                sa = [sa_ref[i, d, vh] for vh in range(V_HI)]
                dv_ref[at] = _gather_groups(dv_p, grp)
                blocks = {G_KK: zero, G_R: zero, G_W[d]: zero, G_B[d]: zero, G_KD[d]: zero}
                for kh in range(K_HI):
                    w_r, kk_r = _key_row(x_t, G_W[d], kh), _key_row(x_t, G_KK, kh)
                    dkk = dr = dw = db = dkd = None
                    for vh in range(V_HI):
                        g, before = ds_ref[d, kh, vh], hist_ref[i, d, kh, vh]
                        dr = _acc(dr, after(d, kh, vh) * dy_s[vh])
                        dw = _acc(dw, g * before)
                        dkd = _acc(dkd, g * v_s[vh])
                        db = _acc(db, g * sa[vh])
                        dkk = _acc(dkk, before * dsa[vh])
                        ds_ref[d, kh, vh] = g * w_r + dsa[vh] * kk_r
                    for gi, a in ((G_KK, dkk), (G_R, dr), (G_W[d], dw), (G_B[d], -db), (G_KD[d], dkd)):
                        blocks[gi] = jnp.where(row == kh, _colsum(a), blocks[gi])
                for gi in range(N_GROUP):
                    g_ref[at, SUBLANES * gi:SUBLANES * (gi + 1), :] = blocks.get(gi, zero)

        body(last, lambda d, kh, vh: after_ref[d, kh, vh])

        def step(ii, carry):
            i = last - ii
            body(i, lambda d, kh, vh: hist_ref[i + 1, d, kh, vh])
            return carry

        lax.fori_loop(1, SCAN_CHUNK, step, 0, unroll=SCAN_UNROLL)
        after_ref[...] = hist_ref[0]

        if n_x:
            @pl.when(c == nc - 1)
            def _():
                for cp in riders():
                    cp.wait()

    return pl.pallas_call(
        kern,
        out_shape=[jax.ShapeDtypeStruct((steps, HEAD, LANES), F32)] * 2
        + [jax.ShapeDtypeStruct((steps, SUBLANES, LANES), F32)] * 2 + _exchange_out_shapes(exchange, n_x),
        grid=(nc,),
        in_specs=[k_spec(back), k_spec(same), v_spec(back), v_spec(same), v_spec(back), v_spec(same),
                  pl.BlockSpec((SCAN_CHUNK,) + ST_SHAPE, lambda c: (back(c), 0, 0, 0, 0, 0)),
                  pl.BlockSpec(ST_SHAPE, lambda c: (0, 0, 0, 0, 0)),
                  pl.BlockSpec((SCAN_CHUNK,) + SA_SHAPE, lambda c: (back(c), 0, 0, 0, 0))] + _hbm_specs(n_x),
        out_specs=[k_spec(back), k_spec(same), v_spec(back), v_spec(same)] + _hbm_specs(n_x),
        scratch_shapes=[pltpu.VMEM(ST_SHAPE, F32), pltpu.VMEM(ST_SHAPE, F32)]
        + (_exchange_sems(n_x) if n_x else []),
        compiler_params=_params(("arbitrary",), SCAN_VMEM_LIMIT), name=name)(xall, xall, v_c, v_c, dy_c, dy_c, hist, fin, sa,
                                                            *exchange)


def _bf16_pieces(x):
    hi = x.astype(BF16)
    rest = x - hi.astype(F32)
    mid = rest.astype(BF16)
    return hi, mid, (rest - mid.astype(F32)).astype(BF16)


def _to_key_rows(wide, bsz, seq, *, name):
    assert bsz == 2
    perm = _key_row_maps()
    tt = min(RELAYOUT_TILE, seq)
    per_seq = seq // tt

    def kern(x0_ref, x1_ref, p0_ref, p1_ref, o_ref):
        total = None
        for x_ref, p_ref in ((x0_ref, p0_ref), (x1_ref, p1_ref)):
            for piece in _bf16_pieces(x_ref[...]):
                term = jnp.dot(piece, p_ref[...], preferred_element_type=F32)
                total = term if total is None else total + term
        for r in range(K_HI):
            o_ref[:, r, :] = total[:, LANES * r:LANES * (r + 1)]

    p_spec = pl.BlockSpec((D_RWKV, K_HI * LANES), lambda i, a: (0, 0))
    return pl.pallas_call(
        kern, out_shape=jax.ShapeDtypeStruct((seq, HEAD, LANES), F32), grid=(per_seq, N_GROUP),
        in_specs=[pl.BlockSpec((tt, D_RWKV), lambda i, a: (i, a)),
                  pl.BlockSpec((tt, D_RWKV), lambda i, a: (per_seq + i, a)), p_spec, p_spec],
        out_specs=pl.BlockSpec((tt, K_HI, LANES), lambda i, a: (i, a, 0)),
        compiler_params=_params(("parallel", "parallel")), name=name)(wide, wide, *perm)


def _key_row_maps():
    src = jnp.arange(D_RWKV)
    head, kh, kl = src // HEAD, (src // SUBLANES) % K_HI, src % SUBLANES
    dst = jnp.arange(K_HI * LANES)
    return [((kh[:, None] == dst[None, :] // LANES) & (kl[:, None] == (dst[None, :] // N_CHAIN) % SUBLANES)
             & ((dst[None, :] // N_HEAD) % 2 == b) & (head[:, None] == dst[None, :] % N_HEAD)).astype(BF16)
            for b in range(2)]


def _from_key_rows(g_f, g_b, bsz, seq, *, name):
    assert bsz == 2
    maps = jnp.stack([m.T for m in _key_row_maps()])
    tt = min(RELAYOUT_TILE, seq)
    per_seq = seq // tt

    def kern(gf_ref, gb_ref, q_ref, o_ref):
        a = pl.program_id(2)
        shared = a <= G_R
        from_f = shared | (a % 2 == G_W[0] % 2)

        def rearranged(g_ref):
            g = jnp.concatenate([g_ref[:, r, :] for r in range(K_HI)], axis=1)
            hi, mid = (jnp.dot(piece, q_ref[0], preferred_element_type=F32) for piece in _bf16_pieces(g)[:2])
            return hi + mid

        @pl.when(from_f)
        def _():
            o_ref[...] = rearranged(gf_ref)

        @pl.when(jnp.logical_not(from_f))
        def _():
            o_ref[...] = rearranged(gb_ref)

        @pl.when(shared)
        def _():
            o_ref[...] += rearranged(gb_ref)

    g_spec = pl.BlockSpec((tt, K_HI, LANES), lambda b, i, a: (i, a, 0))
    return pl.pallas_call(
        kern, out_shape=jax.ShapeDtypeStruct((bsz * seq, N_GROUP * D_RWKV), F32), grid=(bsz, per_seq, N_GROUP),
        in_specs=[g_spec, g_spec, pl.BlockSpec((1, K_HI * LANES, D_RWKV), lambda b, i, a: (b, 0, 0))],
        out_specs=pl.BlockSpec((tt, D_RWKV), lambda b, i, a: (b * per_seq + i, a)),
        compiler_params=_params(("parallel", "parallel", "parallel")), name=name)(g_f, g_b, maps)


def _to_value_rows(a, bsz, seq):
    z = a.reshape(bsz, seq, N_HEAD, V_HI, SUBLANES).transpose(1, 4, 3, 0, 2)
    return z.reshape(seq, SUBLANES, LANES)


def _from_value_rows(y, bsz, seq):
    z = y.reshape(seq, SUBLANES, V_HI, bsz, N_HEAD).transpose(3, 0, 4, 2, 1)
    return z.reshape(bsz * seq, D_RWKV)


def _pad_cols(a, segs):
    out, off = [], 0
    for w, wp in segs:
        out.append(a[..., off:off + w])
        if wp > w:
            out.append(jnp.zeros(a.shape[:-1] + (wp - w,), a.dtype))
        off += w
    return jnp.concatenate(out, axis=-1)


def _unpad_cols(a, segs):
    out, off = [], 0
    for w, wp in segs:
        out.append(a[..., off:off + w])
        off += wp
    return jnp.concatenate(out, axis=-1)


P_SEGS = ((3 * D_RWKV, 3 * D_RWKV), (D_LORA, 128), (D_LORA, 128), (D_GATE, 256), (3 * D_CONV, 3 * D_CONV))
S_SEGS = P_SEGS[:4]


def _pad_rows(a, rows):
    return jnp.concatenate([a, jnp.zeros((rows - a.shape[0], a.shape[1]), a.dtype)], axis=0)


LATE = ("w_out", "w_gate", "w_up", "w_down")


def _local_step(x, target, w, late=None):
    bsz, seq, _ = x.shape
    t = bsz * seq
    x2d = x.reshape(t, D_MODEL)
    tg2d = target.reshape(t, D_MODEL)
    row = lambda a: a.reshape(1, -1).astype(F32)

    w_in = _pad_cols(w["w_in"][0], P_SEGS)
    mu = _pad_cols(row(w["mu_shift"]), S_SEGS)
    wupf, wupb, aupf, aupb = (_pad_rows(w[n][0].astype(F32), 128) for n in ("w_up_f", "w_up_b", "a_up_f", "a_up_b"))
    gup = _pad_rows(w["g_up"][0].astype(F32), 256)
    conv_w = _pad_rows(w["conv_w"][0].astype(F32), SUBLANES)
    norm1, norm2, normf = row(w["norm1_w"]), row(w["norm2_w"]), row(w["norm_f_w"])
    vec = {n: row(w[n]) for n in VEC}
    head_of = jnp.arange(LANES) // HEAD
    bd = (head_of[:, None] == head_of[None, :]).astype(F32)
    pre_consts = [vec["k_k"], vec["w0_f"], vec["w0_b"], vec["a0_f"], vec["a0_b"], vec["k_a_f"], vec["k_a_b"],
                  wupf, wupb, aupf, aupb, gup, bd]
    post_consts = [vec["gn_w"], vec["gn_b"], vec["r_k_f"], vec["r_k_b"], bd]

    h1, = _rowwise(_rms, [x2d], [norm1], [D_MODEL], [], name="rms1_fwd", out_dtype=BF16, tb=WIDE_TILE)
    p = _mm(h1, w_in, name="mm_in")
    pss, oconv = _shift_conv_fwd(p, mu, conv_w, seq, name="shift_conv_fwd")
    pre_rows = [(pss, 0, 512), (pss, 1, 512), (pss, XW0 // 128, 128), (pss, XA0 // 128, 128), (pss, XG0 // 256, 256)]
    sc, g = _rowwise(_prescan_math, pre_rows, pre_consts, [[D_RWKV] * N_GROUP, D_RWKV], [], name="prescan_fwd")
    xall = _to_key_rows(sc, bsz, seq, name="to_key_rows")
    v_l = _to_value_rows(pss[:, 2 * D_RWKV:3 * D_RWKV], bsz, seq)
    y_f, y_b, hist, fin, sa, *gathered = _scan_fwd(xall, v_l, gather=[late[n] for n in LATE] if late else (),
                                                   name="scan_fwd")
    w_out, w_gate, w_up, w_down = (
        (_from_slots(a, SHARD_AXIS[n]) if late else w[n])[0] for n, a in zip(LATE, gathered or LATE))
    y = _from_value_rows(y_f + y_b, bsz, seq)
    post_rows = [y, (pss, 0, 512), (pss, 2, 512), (sc, G_KD[0], 512), (sc, G_KD[1], 512), g]

    def post_fwd(y_, r_, v_, kdf_, kdb_, g_, oc_, *consts):
        return _postscan_math(y_, r_, v_, kdf_, kdb_, g_, *consts), oc_

    o, = _rowwise(post_fwd, post_rows + [oconv], post_consts, [[D_RWKV, D_CONV]], [], name="postscan_fwd",
                  out_dtype=BF16)
    x1 = _mm(o, w_out, add=x2d, name="mm_out")
    h2, = _rowwise(_rms, [x1], [norm2], [D_MODEL], [], name="rms2_fwd", out_dtype=BF16, tb=WIDE_TILE)
    gg, uu, ff = _mm_swiglu(h2, w_gate, w_up, name="mm_gate_up")
    x2 = _mm(ff, w_down, add=x1, name="mm_down")

    def final(x_, tg_, wn_):
        yo, vjp = jax.vjp(_rms, x_, wn_)
        err = yo - tg_
        dx_, dwn_ = vjp(err * (1.0 / D_MODEL))
        part = jnp.sum(jnp.sum(err * err, axis=1, keepdims=True), axis=0, keepdims=True) * (0.5 / D_MODEL)
        return dx_, part + jnp.zeros((1, LANES), F32), dwn_

    dx2, loss_acc, d_normf = _rowwise(final, [x2, tg2d], [normf], [D_MODEL], [(1, LANES), (1, D_MODEL)],
                                      name="loss_head", tb=WIDE_TILE)
    dgg, duu = _mm_swiglu_bwd(dx2, w_down, gg, uu, name="mm_down_dx")
    g_w_down = _mm(ff, dx2, ta=True, name="mm_down_dw")
    dh2 = _mm(dgg, w_gate, tb=True, name="mm_gate_dx")
    dh2 = _mm(duu, w_up, tb=True, add=dh2, name="mm_up_dx")
    g_w_gate = _mm(h2, dgg, ta=True, name="mm_gate_dw")
    g_w_up = _mm(h2, duu, ta=True, name="mm_up_dw")

    def rms_bwd(x_, dh_, dres_, wn_):
        _, vjp = jax.vjp(_rms, x_, wn_)
        dx_, dwn_ = vjp(dh_)
        return dx_ + dres_, dwn_

    dx1, d_norm2 = _rowwise(rms_bwd, [x1, dh2, dx2], [norm2], [D_MODEL], [(1, D_MODEL)], name="rms2_bwd", tb=WIDE_TILE)
    do = _mm(dx1, w_out, tb=True, name="mm_out_dx")
    g_w_out = _mm(o, dx1, ta=True, name="mm_out_dw")

    def post_bwd(y_, r_, v_, kdf_, kdb_, g_, do_, *consts):
        _, vjp = jax.vjp(lambda *a: _postscan_math(*a, consts[4]), y_, r_, v_, kdf_, kdb_, g_, *consts[:4])
        return vjp(do_)

    (dy, dr_c, dv_c, dkdf_c, dkdb_c, dg, d_gn_w, d_gn_b, d_rkf, d_rkb) = _rowwise(
        post_bwd, post_rows + [(do, 0, 512)], post_consts, [D_RWKV] * 6, [(1, D_RWKV)] * 4, name="postscan_bwd")
    dy_l = _to_value_rows(dy, bsz, seq)
    late_grads = {"w_out": g_w_out[None], "w_gate": g_w_gate[None], "w_up": g_w_up[None], "w_down": g_w_down[None]}
    g_f, g_b, dv_f, dv_b, *late_parts = _scan_bwd(
        xall, v_l, dy_l, hist, fin, sa, name="scan_bwd",
        exchange=[_to_slots(late_grads[n], SHARD_AXIS[n]).astype(BF16) for n in LATE] if late else ())
    dsc = _from_key_rows(g_f, g_b, bsz, seq, name="from_key_rows")
    dv_s = _from_value_rows(dv_f + dv_b, bsz, seq)

    def pre_bwd(r_, k_, xw_, xa_, xg_, dkk_, dr_s, dwf_, dwb_, dbf_, dbb_, dkdf_s, dkdb_s,
                dr_c_, dv_c_, dv_s_, dkdf_c_, dkdb_c_, dg_, *consts):
        _, vjp = jax.vjp(lambda *a: _prescan_math(*a, consts[-1]), r_, k_, xw_, xa_, xg_, *consts[:-1])
        grads = vjp((dkk_, dr_s + dr_c_, dwf_, dwb_, dbf_, dbb_, dkdf_s + dkdf_c_, dkdb_s + dkdb_c_, dg_))
        dr_, dk_, dxw_, dxa_, dxg_ = grads[:5]
        return (dr_, dk_, dv_c_ + dv_s_, dxw_, dxa_, dxg_) + tuple(grads[5:])

    pre_b_rows = (pre_rows + [(dsc, j, 512) for j in range(N_GROUP)]
                  + [dr_c, dv_c, dv_s, dkdf_c, dkdb_c, dg])
    pre_b = _rowwise(pre_bwd, pre_b_rows, pre_consts, [[512, 512, 512, 128, 128, 256]],
                     [(1, D_RWKV)] * 7 + [(128, D_RWKV)] * 4 + [(256, D_RWKV)], name="prescan_bwd")
    d_pss = pre_b[0]
    d_kk_, d_w0f, d_w0b, d_a0f, d_a0b, d_kaf, d_kab, d_wupf, d_wupb, d_aupf, d_aupb, d_gup = pre_b[1:]
    dp, d_mu, d_conv = _shift_conv_bwd(p, d_pss, do, mu, conv_w, seq, name="shift_conv_bwd")
    g_w_in = _mm(h1, dp, ta=True, name="mm_in_dw")
    grads = {
        "w_in": _unpad_cols(g_w_in, P_SEGS)[None], "mu_shift": _unpad_cols(d_mu, S_SEGS),
        "w_up_f": d_wupf[None, :D_LORA], "w0_f": d_w0f, "w_up_b": d_wupb[None, :D_LORA], "w0_b": d_w0b,
        "a_up_f": d_aupf[None, :D_LORA], "a0_f": d_a0f, "a_up_b": d_aupb[None, :D_LORA], "a0_b": d_a0b,
        "g_up": d_gup[None, :D_GATE], "k_k": d_kk_, "k_a_f": d_kaf, "k_a_b": d_kab,
        "r_k_f": d_rkf, "r_k_b": d_rkb, "gn_w": d_gn_w, "gn_b": d_gn_b, "conv_w": d_conv[None, :3],
        "w_out": g_w_out[None], "norm2_w": d_norm2, "w_gate": g_w_gate[None], "w_up": g_w_up[None],
        "w_down": g_w_down[None], "norm_f_w": d_normf,
    }
    early = ("w_in",) + LORA
    parts = dict(zip(LATE, late_parts))
    if late:
        vec_rows = jnp.concatenate([grads[n] for n in VEC] + [jnp.zeros((16 - len(VEC), D_RWKV), F32)], axis=0)
        slots = [_to_slots(grads[n], SHARD_AXIS[n]).astype(BF16 if n in BIG else F32) for n in early]
        dh1, *recv = _mm(dp, w_in, tb=True, exchange=(slots, [vec_rows]), name="mm_in_dx")
        parts.update(zip(early + ("vec",), recv))
    else:
        dh1 = _mm(dp, w_in, tb=True, name="mm_in_dx")
    dx, grads["norm1_w"] = _rowwise(rms_bwd, [x2d, dh1, dx1], [norm1], [D_MODEL], [(1, D_MODEL)], name="rms1_bwd",
                                    tb=WIDE_TILE)
    return loss_acc, dx.reshape(bsz, seq, D_MODEL), grads, parts


def _hbm_specs(n):
    return [pl.BlockSpec(memory_space=pl.ANY)] * n


def _all_gather(arrs, *, name):
    n = len(arrs)

    def body(*refs):
        x_refs, out_refs = refs[:n], refs[n:2 * n]
        send_sems, recv_sems, local_sems = refs[2 * n:]
        x, y, c = lax.axis_index("x"), lax.axis_index("y"), lax.axis_index("c")
        me, sibling = (x, y, c), (x, y, 1 - c)
        chips = [(1 - x, y), (x, 1 - y), (1 - x, 1 - y)]

        def slot(a, px, py, pc):
            return out_refs[a].at[4 * px + 2 * py + pc]

        def copy(a, k, block, to, src=None):
            return pltpu.make_async_remote_copy(
                src_ref=slot(a, *block) if src is None else src, dst_ref=slot(a, *block),
                send_sem=send_sems.at[k, a], recv_sem=recv_sems.at[k, a],
                device_id=to, device_id_type=pl.DeviceIdType.MESH)

        mine = [pltpu.make_async_copy(x_refs[a], slot(a, *me), local_sems.at[a]) for a in range(n)]
        for cp in mine:
            cp.start()
        first = []
        for a in range(n):
            first.append(copy(a, 0, me, sibling, src=x_refs[a]))
            first += [copy(a, 1 + j, me, (*chip, c), src=x_refs[a]) for j, chip in enumerate(chips)]
        for cp in first:
            cp.start()
        passed = []
        for j, chip in enumerate(chips):
            for a in range(n):
                copy(a, 1 + j, (*chip, c), me).wait_recv()
                cp = copy(a, 4 + j, (*chip, c), sibling)
                cp.start()
                passed.append(cp)
        for a in range(n):
            copy(a, 0, sibling, me).wait_recv()
            for j, chip in enumerate(chips):
                copy(a, 4 + j, (*chip, 1 - c), me).wait_recv()
        for cp in first + passed:
            cp.wait_send()
        for cp in mine:
            cp.wait()

    return pl.pallas_call(
        body, out_shape=[jax.ShapeDtypeStruct((N_DEV,) + a.shape, a.dtype) for a in arrs],
        in_specs=_hbm_specs(n), out_specs=_hbm_specs(n),
        scratch_shapes=[pltpu.SemaphoreType.DMA((7, n)), pltpu.SemaphoreType.DMA((7, n)),
                        pltpu.SemaphoreType.DMA((n,))],
        name=name)(*arrs)


def _exchange(sliced, whole, *, name):
    arrs = list(sliced) + list(whole)
    n, n_sliced = len(arrs), len(sliced)

    def body(*refs):
        copies = _exchange_copies(refs[:n], refs[n:2 * n], n_sliced, *refs[2 * n:])
        for cp in copies:
            cp.start()
        for cp in copies:
            cp.wait()

    return pl.pallas_call(
        body, out_shape=_exchange_out_shapes(arrs, n_sliced), in_specs=_hbm_specs(n), out_specs=_hbm_specs(n),
        scratch_shapes=_exchange_sems(n), name=name)(*arrs)


def _exchange_out_shapes(arrs, n_sliced):
    return [jax.ShapeDtypeStruct(a.shape if i < n_sliced else (N_DEV,) + a.shape, a.dtype)
            for i, a in enumerate(arrs)]


def _exchange_sems(n):
    return [pltpu.SemaphoreType.DMA((7, n)), pltpu.SemaphoreType.DMA((7, n)), pltpu.SemaphoreType.DMA((n,))]


def _exchange_copies(in_refs, out_refs, n_sliced, send_sems, recv_sems, local_sems):
    n = len(in_refs)
    x, y, c = lax.axis_index("x"), lax.axis_index("y"), lax.axis_index("c")
    me = 4 * x + 2 * y + c

    def src(a, dev):
        return in_refs[a].at[dev] if a < n_sliced else in_refs[a]

    copies = [pltpu.make_async_copy(src(a, me), out_refs[a].at[me], local_sems.at[a]) for a in range(n)]
    for k in range(1, N_DEV):
        px = 1 - x if k & 4 else x
        py = 1 - y if k & 2 else y
        pc = 1 - c if k & 1 else c
        for a in range(n):
            copies.append(pltpu.make_async_remote_copy(
                src_ref=src(a, 4 * px + 2 * py + pc), dst_ref=out_refs[a].at[me],
                send_sem=send_sems.at[k - 1, a], recv_sem=recv_sems.at[k - 1, a],
                device_id=(px, py, pc), device_id_type=pl.DeviceIdType.MESH))
    return copies


def _adam_math(g, w, m, v):
    nm = ADAM_B1 * m + (1.0 - ADAM_B1) * g
    nv = ADAM_B2 * v + (1.0 - ADAM_B2) * (g * g)
    m_hat = nm / (1.0 - ADAM_B1 ** ADAM_STEP)
    v_hat = nv / (1.0 - ADAM_B2 ** ADAM_STEP)
    return -ADAM_LR * (m_hat / (jnp.sqrt(v_hat) + ADAM_EPS) + ADAM_WD * w), nm, nv


def _slot_sum(ref):
    g = ref[0].astype(F32)
    for s in range(1, N_DEV):
        g = g + ref[s].astype(F32)
    return g


def _adamw_big(parts, w, m, v, *, name):
    _, rws, cols = w.shape
    tr = _tile(rws, (256, 176, 128))

    def kern(p_ref, w_ref, m_ref, v_ref, g_ref, d_ref, nm_ref, nv_ref):
        g = _slot_sum(p_ref)
        g_ref[...] = g
        d_ref[...], nm_ref[...], nv_ref[...] = _adam_math(g, w_ref[...], m_ref[...], v_ref[...])

    spec = pl.BlockSpec((1, tr, cols), lambda i: (0, i, 0))
    return pl.pallas_call(
        kern, out_shape=[jax.ShapeDtypeStruct(w.shape, F32)] * 4, grid=(rws // tr,),
        in_specs=[pl.BlockSpec((N_DEV, 1, tr, cols), lambda i: (0, 0, i, 0)), spec, spec, spec],
        out_specs=[spec] * 4, compiler_params=_params(("parallel",)), name=name)(parts, w, m, v)


def _adamw_small(lora_parts, vec_parts, wide_parts, wmv, *, name):
    names = LORA + VEC + WIDE
    n_l, n = len(LORA), len(names)
    flat = [a for trip in wmv for a in trip]

    def kern(*refs):
        l_refs, vec_ref, wide_ref = refs[:n_l], refs[n_l], refs[n_l + 1]
        in_refs = refs[n_l + 2:n_l + 2 + 3 * n]
        out_refs = refs[n_l + 2 + 3 * n:]
        vec_sum, wide_sum = _slot_sum(vec_ref), _slot_sum(wide_ref)
        for i, nm in enumerate(names):
            w_ref, m_ref, v_ref = in_refs[3 * i:3 * i + 3]
            if i < n_l:
                g = _slot_sum(l_refs[i])
            elif nm in VEC:
                g = vec_sum[i - n_l:i - n_l + 1, :]
            else:
                g = wide_sum[WIDE.index(nm):WIDE.index(nm) + 1, :w_ref.shape[-1]]
            o = out_refs[4 * i:4 * i + 4]
            o[0][...] = g
            o[1][...], o[2][...], o[3][...] = _adam_math(g, w_ref[...], m_ref[...], v_ref[...])

    out_shape = [jax.ShapeDtypeStruct(trip[0].shape, F32) for trip in wmv for _ in range(4)]
    outs = pl.pallas_call(kern, out_shape=out_shape, name=name,
                          compiler_params=pltpu.CompilerParams(vmem_limit_bytes=VMEM_LIMIT))(
        *lora_parts, vec_parts, wide_parts, *flat)
    return [tuple(outs[4 * i:4 * i + 4]) for i in range(n)]


def _to_slots(g, axis):
    _, rws, cols = g.shape
    if axis == 1:
        return g.reshape(N_DEV, 1, rws // N_DEV, cols)
    return g.reshape(1, rws, N_DEV, cols // N_DEV).transpose(2, 0, 1, 3)


def _from_slots(got, axis):
    _, _, rws, cols = got.shape
    if axis == 1:
        return got.reshape(1, N_DEV * rws, cols)
    return got.transpose(1, 2, 0, 3).reshape(1, rws, N_DEV * cols)


def _pad_lanes(a, width):
    return jnp.concatenate([a, jnp.zeros(a.shape[:-1] + (width - a.shape[-1],), a.dtype)], axis=-1)


def kernel(x, norm1_w, w_in, mu_shift, w_up_f, w0_f, w_up_b, w0_b, a_up_f, a0_f, a_up_b, a0_b, g_up, k_k, k_a_f, k_a_b, r_k_f, r_k_b, gn_w, gn_b, conv_w, w_out, norm2_w, w_gate, w_up, w_down, norm_f_w, loss_target, m_norm1_w, m_w_in, m_mu_shift, m_w_up_f, m_w0_f, m_w_up_b, m_w0_b, m_a_up_f, m_a0_f, m_a_up_b, m_a0_b, m_g_up, m_k_k, m_k_a_f, m_k_a_b, m_r_k_f, m_r_k_b, m_gn_w, m_gn_b, m_conv_w, m_w_out, m_norm2_w, m_w_gate, m_w_up, m_w_down, m_norm_f_w, v_norm1_w, v_w_in, v_mu_shift, v_w_up_f, v_w0_f, v_w_up_b, v_w0_b, v_a_up_f, v_a0_f, v_a_up_b, v_a0_b, v_g_up, v_k_k, v_k_a_f, v_k_a_b, v_r_k_f, v_r_k_b, v_gn_w, v_gn_b, v_conv_w, v_w_out, v_norm2_w, v_w_gate, v_w_up, v_w_down, v_norm_f_w):
    local = dict(norm1_w=norm1_w, w_in=w_in, mu_shift=mu_shift, w_up_f=w_up_f, w0_f=w0_f, w_up_b=w_up_b,
                 w0_b=w0_b, a_up_f=a_up_f, a0_f=a0_f, a_up_b=a_up_b, a0_b=a0_b, g_up=g_up, k_k=k_k, k_a_f=k_a_f,
                 k_a_b=k_a_b, r_k_f=r_k_f, r_k_b=r_k_b, gn_w=gn_w, gn_b=gn_b, conv_w=conv_w, w_out=w_out,
                 norm2_w=norm2_w, w_gate=w_gate, w_up=w_up, w_down=w_down, norm_f_w=norm_f_w)
    mom_m = dict(norm1_w=m_norm1_w, w_in=m_w_in, mu_shift=m_mu_shift, w_up_f=m_w_up_f, w0_f=m_w0_f,
                 w_up_b=m_w_up_b, w0_b=m_w0_b, a_up_f=m_a_up_f, a0_f=m_a0_f, a_up_b=m_a_up_b, a0_b=m_a0_b,
                 g_up=m_g_up, k_k=m_k_k, k_a_f=m_k_a_f, k_a_b=m_k_a_b, r_k_f=m_r_k_f, r_k_b=m_r_k_b,
                 gn_w=m_gn_w, gn_b=m_gn_b, conv_w=m_conv_w, w_out=m_w_out, norm2_w=m_norm2_w, w_gate=m_w_gate,
                 w_up=m_w_up, w_down=m_w_down, norm_f_w=m_norm_f_w)
    mom_v = dict(norm1_w=v_norm1_w, w_in=v_w_in, mu_shift=v_mu_shift, w_up_f=v_w_up_f, w0_f=v_w0_f,
                 w_up_b=v_w_up_b, w0_b=v_w0_b, a_up_f=v_a_up_f, a0_f=v_a0_f, a_up_b=v_a_up_b, a0_b=v_a0_b,
                 g_up=v_g_up, k_k=v_k_k, k_a_f=v_k_a_f, k_a_b=v_k_a_b, r_k_f=v_r_k_f, r_k_b=v_r_k_b,
                 gn_w=v_gn_w, gn_b=v_gn_b, conv_w=v_conv_w, w_out=v_w_out, norm2_w=v_norm2_w, w_gate=v_w_gate,
                 w_up=v_w_up, w_down=v_w_down, norm_f_w=v_norm_f_w)

    early = ("w_in",) + LORA
    got = _all_gather([local["w_in"].astype(BF16)] + [local[n] for n in LORA], name="gather")
    full = dict(local)
    full.update({n: _from_slots(a, SHARD_AXIS[n]) for n, a in zip(early, got)})

    loss_part, grad_x, grads, parts = _local_step(x, loss_target, full,
                                                  late={n: local[n].astype(BF16) for n in LATE})

    wide_rows = jnp.concatenate([_pad_lanes(a, WIDE_ROW) for a in [grads[n] for n in WIDE] + [loss_part]]
                                + [jnp.zeros((SUBLANES - len(WIDE) - 1, WIDE_ROW), F32)], axis=0)
    wide_parts, = _exchange([], [wide_rows], name="grad_exchange")
    loss = jnp.sum(wide_parts[:, len(WIDE), 0])
    out = {}
    for n in BIG:
        out[n] = _adamw_big(parts[n], local[n], mom_m[n], mom_v[n], name="adamw_" + n)

    def small_form(n, a):
        if n in LORA:
            return a
        a = a.reshape(1, -1)
        return _pad_lanes(a, WIDE_ROW) if n == "mu_shift" else a

    small = LORA + VEC + WIDE
    res = _adamw_small([parts[n] for n in LORA], parts["vec"], wide_parts,
                       [tuple(small_form(n, d[n]) for d in (local, mom_m, mom_v)) for n in small],
                       name="adamw_small")
    for n, quad in zip(small, res):
        out[n] = tuple(a[..., :local[n].size].reshape(local[n].shape) if n not in LORA else a for a in quad)
    return (loss, grad_x, *[out[n][i] for i in range(4) for n in WEIGHTS])
```

```python
import functools

import jax
import jax.numpy as jnp
from jax import lax
from jax.experimental import pallas as pl
from jax.experimental.pallas import tpu as pltpu

F32 = jnp.float32
BF16 = jnp.bfloat16
HIGHEST = lax.Precision.HIGHEST

N_DEV = 8
D_MODEL = 1024
D_RWKV = 512
D_CONV = 512
HEAD = 64
N_HEAD = D_RWKV // HEAD
D_LORA = 64
D_GATE = 160
D_SHIFTED = 3 * D_RWKV + 2 * D_LORA + D_GATE
XW0, XA0, XG0 = 1536, 1664, 1792
D_SP = 2048
D_INP = D_SP + 3 * D_CONV
LOG_DECAY_SCALE = 0.606531
RMS_EPS = 1e-6
GN_EPS = 64e-5
NORM_EPS = 1e-12
ADAM_LR, ADAM_B1, ADAM_B2, ADAM_EPS, ADAM_WD, ADAM_STEP = 0.001, 0.9, 0.999, 1e-08, 0.01, 10

LANES = 128
SUBLANES = 8
VMEM_LIMIT = 48 * 1024 * 1024
SCAN_CHUNK = 32
SCAN_VMEM_LIMIT = 58 * 1024 * 1024
SCAN_UNROLL = 3
ROW_TILE = 128
WIDE_TILE = 512
RELAYOUT_TILE = 1024

BIG = ("w_in", "w_out", "w_gate", "w_up", "w_down")
LORA = ("w_up_f", "w_up_b", "a_up_f", "a_up_b", "g_up", "conv_w")
SHARD_AXIS = {"w_in": 2, "w_out": 1, "w_gate": 2, "w_up": 2, "w_down": 1, "w_up_f": 2, "w_up_b": 2,
              "a_up_f": 2, "a_up_b": 2, "g_up": 2, "conv_w": 2}
VEC = ("w0_f", "w0_b", "a0_f", "a0_b", "k_k", "k_a_f", "k_a_b", "r_k_f", "r_k_b", "gn_w", "gn_b")
WIDE = ("mu_shift", "norm1_w", "norm2_w", "norm_f_w")
WIDE_ROW = 2048
WEIGHTS = ("norm1_w", "w_in", "mu_shift", "w_up_f", "w0_f", "w_up_b", "w0_b", "a_up_f", "a0_f", "a_up_b",
           "a0_b", "g_up", "k_k", "k_a_f", "k_a_b", "r_k_f", "r_k_b", "gn_w", "gn_b", "conv_w", "w_out",
           "norm2_w", "w_gate", "w_up", "w_down", "norm_f_w")


def _params(sem, limit=VMEM_LIMIT):
    return pltpu.CompilerParams(dimension_semantics=sem, vmem_limit_bytes=limit)


def _tile(n, cands):
    for c in cands:
        if n % c == 0:
            return c
    raise ValueError(f"no tile for {n}")


def _mm(a, b, *, ta=False, tb=False, add=None, exchange=None, name):
    (k_dim, m) = a.shape if ta else a.shape[::-1]
    (k2, n) = b.shape[::-1] if tb else b.shape
    assert k_dim == k2, (a.shape, b.shape, ta, tb)
    tm = _tile(m, (1408, 1024, 512, 256, 128))
    tn = _tile(n, (1408, 1024, 896, 512, 256, 128))
    tk = _tile(k_dim, (1408, 1024, 896, 512, 256, 128))
    nk = k_dim // tk
    grid = (m // tm, n // tn, nk)
    dims = (((0 if ta else 1,), (1 if tb else 0,)), ((), ()))
    sliced, whole = exchange or ((), ())
    riders = list(sliced) + list(whole)
    n_x, n_in = len(riders), 2 + (add is not None)

    def kern(*refs):
        a_ref, b_ref = refs[:2]
        add_ref = refs[2] if add is not None else None
        o_ref, acc_ref = refs[n_in + n_x], refs[n_in + 2 * n_x + 1]
        k = pl.program_id(2)
        step = (pl.program_id(0) * grid[1] + pl.program_id(1)) * nk + k

        def copies():
            return _exchange_copies(refs[n_in:n_in + n_x], refs[n_in + n_x + 1:n_in + 2 * n_x + 1], len(sliced),
                                    *refs[n_in + 2 * n_x + 2:])

        if n_x:
            @pl.when(step == 0)
            def _():
                for cp in copies():
                    cp.start()

        @pl.when(k == 0)
        def _():
            acc_ref[...] = jnp.zeros_like(acc_ref)

        acc_ref[...] += lax.dot_general(a_ref[...].astype(BF16), b_ref[...].astype(BF16), dims,
                                        preferred_element_type=F32)

        @pl.when(k == nk - 1)
        def _():
            if add is None:
                o_ref[...] = acc_ref[...]
            else:
                o_ref[...] = acc_ref[...] + add_ref[...]

        if n_x:
            @pl.when(step == grid[0] * grid[1] * nk - 1)
            def _():
                for cp in copies():
                    cp.wait()

    a_spec = (pl.BlockSpec((tk, tm), lambda i, j, k: (k, i)) if ta
              else pl.BlockSpec((tm, tk), lambda i, j, k: (i, k)))
    b_spec = (pl.BlockSpec((tn, tk), lambda i, j, k: (j, k)) if tb
              else pl.BlockSpec((tk, tn), lambda i, j, k: (k, j)))
    o_spec = pl.BlockSpec((tm, tn), lambda i, j, k: (i, j))
    in_specs = [a_spec, b_spec] + ([o_spec] if add is not None else []) + _hbm_specs(n_x)
    args = (a, b) + ((add,) if add is not None else ()) + tuple(riders)
    out = pl.pallas_call(
        kern, out_shape=[jax.ShapeDtypeStruct((m, n), F32)] + _exchange_out_shapes(riders, len(sliced)), grid=grid,
        in_specs=in_specs, out_specs=[o_spec] + _hbm_specs(n_x),
        scratch_shapes=[pltpu.VMEM((tm, tn), F32)] + (_exchange_sems(n_x) if n_x else []),
        compiler_params=_params(("arbitrary",) * 3 if n_x else ("parallel", "parallel", "arbitrary")),
        name=name)(*args)
    return out if n_x else out[0]


def _swiglu(g, u):
    return jax.nn.silu(g) * u


FFN_TN = 256


def _mm_swiglu(h, w_gate, w_up, *, name):
    m, k_dim = h.shape
    n = w_gate.shape[1]
    tm = _tile(m, (1024, 512, 256, 128))

    def kern(h_ref, wg_ref, wu_ref, g_ref, u_ref, f_ref):
        hv = h_ref[...].astype(BF16)
        g = jnp.dot(hv, wg_ref[...].astype(BF16), preferred_element_type=F32)
        u = jnp.dot(hv, wu_ref[...].astype(BF16), preferred_element_type=F32)
        g_ref[...] = g
        u_ref[...] = u
        f_ref[...] = _swiglu(g, u).astype(f_ref.dtype)

    w_spec = pl.BlockSpec((k_dim, FFN_TN), lambda i, j: (0, j))
    o_spec = pl.BlockSpec((tm, FFN_TN), lambda i, j: (i, j))
    return pl.pallas_call(
        kern, out_shape=[jax.ShapeDtypeStruct((m, n), F32)] * 2 + [jax.ShapeDtypeStruct((m, n), BF16)],
        grid=(m // tm, n // FFN_TN), in_specs=[pl.BlockSpec((tm, k_dim), lambda i, j: (i, 0)), w_spec, w_spec],
        out_specs=[o_spec] * 3, compiler_params=_params(("parallel", "parallel")), name=name)(h, w_gate, w_up)


def _mm_swiglu_bwd(dx, w_down, g, u, *, name):
    m, k_dim = dx.shape
    n = w_down.shape[0]
    tm = _tile(m, (1024, 512, 256, 128))

    def kern(dx_ref, w_ref, g_ref, u_ref, dg_ref, du_ref):
        df = lax.dot_general(dx_ref[...].astype(BF16), w_ref[...].astype(BF16), (((1,), (1,)), ((), ())),
                             preferred_element_type=F32)
        _, vjp = jax.vjp(_swiglu, g_ref[...], u_ref[...])
        dg, du = vjp(df)
        dg_ref[...] = dg.astype(dg_ref.dtype)
        du_ref[...] = du.astype(du_ref.dtype)

    o_spec = pl.BlockSpec((tm, FFN_TN), lambda i, j: (i, j))
    return pl.pallas_call(
        kern, out_shape=[jax.ShapeDtypeStruct((m, n), BF16)] * 2, grid=(m // tm, n // FFN_TN),
        in_specs=[pl.BlockSpec((tm, k_dim), lambda i, j: (i, 0)), pl.BlockSpec((FFN_TN, k_dim), lambda i, j: (j, 0)),
                  o_spec, o_spec],
        out_specs=[o_spec] * 2, compiler_params=_params(("parallel", "parallel")), name=name)(dx, w_down, g, u)


def _rowwise(fn, rows, consts, out_rows, out_accs, *, name, tb=ROW_TILE, out_dtype=F32):
    t = (rows[0][0] if isinstance(rows[0], tuple) else rows[0]).shape[0]
    tb = min(tb, t)
    n_r, n_c, n_o, n_a = len(rows), len(consts), len(out_rows), len(out_accs)
    pieces = [w if isinstance(w, (list, tuple)) else [w] for w in out_rows]

    def kern(*refs):
        r_refs = refs[:n_r]
        c_refs = refs[n_r:n_r + n_c]
        o_refs = refs[n_r + n_c:n_r + n_c + n_o]
        a_refs = refs[n_r + n_c + n_o:]
        vals = fn(*[r[...] for r in r_refs], *[c[...] for c in c_refs])
        vals = list(vals) if isinstance(vals, (tuple, list)) else [vals]
        pos = 0
        for o_ref, ws in zip(o_refs, pieces):
            off = 0
            for w in ws:
                o_ref[:, off:off + w] = vals[pos].astype(o_ref.dtype)
                off += w
                pos += 1
        if n_a:
            @pl.when(pl.program_id(0) == 0)
            def _():
                for a_ref in a_refs:
                    a_ref[...] = jnp.zeros_like(a_ref)
            for a_ref, v in zip(a_refs, vals[pos:]):
                a_ref[...] += v

    in_specs, args = [], []
    for r in rows:
        if isinstance(r, tuple):
            arr, blk, w = r
            in_specs.append(pl.BlockSpec((tb, w), functools.partial(lambda i, blk: (i, blk), blk=blk)))
        else:
            arr = r
            in_specs.append(pl.BlockSpec((tb, arr.shape[1]), lambda i: (i, 0)))
        args.append(arr)
    for c in consts:
        in_specs.append(pl.BlockSpec(c.shape, lambda i: (0, 0)))
        args.append(c)
    out_shape = [jax.ShapeDtypeStruct((t, sum(ws)), out_dtype) for ws in pieces]
    out_specs = [pl.BlockSpec((tb, sum(ws)), lambda i: (i, 0)) for ws in pieces]
    for shp in out_accs:
        out_shape.append(jax.ShapeDtypeStruct(shp, F32))
        out_specs.append(pl.BlockSpec(shp, lambda i: (0, 0)))
    res = pl.pallas_call(
        kern, out_shape=out_shape, grid=(t // tb,), in_specs=in_specs, out_specs=out_specs,
        compiler_params=_params(("arbitrary",) if n_a else ("parallel",)), name=name)(*args)
    return res


def _rms(x, w):
    return x * lax.rsqrt(jnp.mean(x * x, axis=-1, keepdims=True) + RMS_EPS) * w


def _seg_sum(x, bd):
    return jnp.concatenate(
        [jnp.dot(x[:, LANES * j:LANES * (j + 1)], bd, precision=HIGHEST, preferred_element_type=F32)
         for j in range(x.shape[1] // LANES)], axis=1)


@jax.custom_vjp
def _seg(x, bd):
    return _seg_sum(x, bd)


_seg.defvjp(lambda x, bd: (_seg_sum(x, bd), bd), lambda bd, ct: (_seg_sum(ct, bd), jnp.zeros_like(bd)))


def _colsum(x):
    return jnp.sum(x, axis=0, keepdims=True)


def _prescan_math(r, k, xw, xa, xg, k_k, w0f, w0b, a0f, a0b, kaf, kab, wupf, wupb, aupf, aupb, gup, bd):
    kkr = k * k_k
    norm = jnp.sqrt(_seg(kkr * kkr, bd))
    kk = kkr / jnp.maximum(norm, NORM_EPS)
    th = jnp.tanh(xw)

    def direction(w0, wup, a0, aup, ka):
        logit = w0 + jnp.dot(th, wup, preferred_element_type=F32)
        w = jnp.exp(-LOG_DECAY_SCALE * jax.nn.sigmoid(logit))
        a = jax.nn.sigmoid(a0 + jnp.dot(xa, aup, preferred_element_type=F32))
        kd = k * (1.0 + (a - 1.0) * ka)
        return w, kd, kk * a

    wf, kdf, bf = direction(w0f, wupf, a0f, aupf, kaf)
    wb, kdb, bb = direction(w0b, wupb, a0b, aupb, kab)
    g = jnp.dot(jax.nn.sigmoid(xg), gup, preferred_element_type=F32)
    return kk, r, wf, wb, bf, bb, kdf, kdb, g


def _postscan_math(y, r, v, kdf, kdb, g, gn_w, gn_b, rkf, rkb, bd):
    mean = _seg(y, bd) * (1.0 / HEAD)
    yc = y - mean
    var = _seg(yc * yc, bd) * (1.0 / HEAD)
    yg = yc * lax.rsqrt(var + GN_EPS) * gn_w + gn_b
    bonus = (_seg(r * kdf * rkf, bd) + _seg(r * kdb * rkb, bd)) * v
    return (yg + bonus) * g


def _halo_specs(width, col_blk, tb, t):
    nb = t // SUBLANES
    step = tb // SUBLANES
    main = pl.BlockSpec((tb, width), lambda i: (i, col_blk))
    prev = pl.BlockSpec((SUBLANES, width), lambda i: (jnp.maximum(i * step - 1, 0), col_blk))
    nxt = pl.BlockSpec((SUBLANES, width), lambda i: (jnp.minimum((i + 1) * step, nb - 1), col_blk))
    return [main, prev, nxt]


def _neighbours(z, prev8, next8, first, last):
    tb = z.shape[0]
    row = lax.broadcasted_iota(jnp.int32, z.shape, 0)
    prow = jnp.where(first, 0.0, prev8[SUBLANES - 1:SUBLANES, :])
    nrow = jnp.where(last, 0.0, next8[0:1, :])
    down = jnp.where(row == 0, prow, pltpu.roll(z, 1, 0))
    up = jnp.where(row == tb - 1, nrow, pltpu.roll(z, tb - 1, 0))
    return down, up


def _shift_conv_fwd(p, mu, conv_w, seq, *, name, tb=ROW_TILE):
    t = p.shape[0]
    per_seq = seq // tb

    def kern(p_ref, pp_ref, pn_ref, mu_ref, cw_ref, pss_ref, oc_ref):
        i = pl.program_id(0)
        first = (i % per_seq) == 0
        last = (i % per_seq) == per_seq - 1
        ps = p_ref[:, :D_SP]
        down, up = _neighbours(ps, pp_ref[:, :D_SP], pn_ref[:, :D_SP], first, last)
        pss_ref[...] = ps + mu_ref[...] * (0.5 * (down + up) - ps)
        gb = p_ref[:, D_SP:D_SP + D_CONV]
        u = p_ref[:, D_SP + D_CONV:D_SP + 2 * D_CONV] * p_ref[:, D_SP + 2 * D_CONV:]
        u_p = pp_ref[:, D_SP + D_CONV:D_SP + 2 * D_CONV] * pp_ref[:, D_SP + 2 * D_CONV:]
        u_n = pn_ref[:, D_SP + D_CONV:D_SP + 2 * D_CONV] * pn_ref[:, D_SP + 2 * D_CONV:]
        udown, uup = _neighbours(u, u_p, u_n, first, last)
        oc_ref[...] = gb * (cw_ref[0:1, :] * udown + cw_ref[1:2, :] * u + cw_ref[2:3, :] * uup)

    return pl.pallas_call(
        kern,
        out_shape=[jax.ShapeDtypeStruct((t, D_SP), F32), jax.ShapeDtypeStruct((t, D_CONV), F32)],
        grid=(t // tb,),
        in_specs=_halo_specs(D_INP, 0, tb, t) + [pl.BlockSpec((1, D_SP), lambda i: (0, 0)),
                                                 pl.BlockSpec((SUBLANES, D_CONV), lambda i: (0, 0))],
        out_specs=[pl.BlockSpec((tb, D_SP), lambda i: (i, 0)), pl.BlockSpec((tb, D_CONV), lambda i: (i, 0))],
        compiler_params=_params(("parallel",)), name=name)(p, p, p, mu, conv_w)


def _shift_conv_bwd(p, d_pss, d_o, mu, conv_w, seq, *, name, tb=ROW_TILE):
    t = p.shape[0]
    per_seq = seq // tb

    def kern(p_ref, pp_ref, pn_ref, d_ref, dp_ref, dn_ref, do_ref, dop_ref, don_ref, mu_ref, cw_ref,
             out_ref, dmu_ref, dcw_ref):
        i = pl.program_id(0)
        first = (i % per_seq) == 0
        last = (i % per_seq) == per_seq - 1

        @pl.when(i == 0)
        def _():
            dmu_ref[...] = jnp.zeros_like(dmu_ref)
            dcw_ref[...] = jnp.zeros_like(dcw_ref)

        mu_v = mu_ref[...]
        ps = p_ref[:, :D_SP]
        down, up = _neighbours(ps, pp_ref[:, :D_SP], pn_ref[:, :D_SP], first, last)
        d = d_ref[...]
        ddown, dup = _neighbours(d, dp_ref[...], dn_ref[...], first, last)
        out_ref[:, :D_SP] = (d - mu_v * d + 0.5 * (mu_v * ddown + mu_v * dup)).astype(out_ref.dtype)
        dmu_ref[...] += _colsum(d * (0.5 * (down + up) - ps))

        def parts(ref):
            return (ref[:, D_SP:D_SP + D_CONV], ref[:, D_SP + D_CONV:D_SP + 2 * D_CONV],
                    ref[:, D_SP + 2 * D_CONV:])

        gb, gc, hh = parts(p_ref)
        gb_p, gc_p, hh_p = parts(pp_ref)
        gb_n, gc_n, hh_n = parts(pn_ref)
        u = gc * hh
        udown, uup = _neighbours(u, gc_p * hh_p, gc_n * hh_n, first, last)
        cw0, cw1, cw2 = cw_ref[0:1, :], cw_ref[1:2, :], cw_ref[2:3, :]
        do = do_ref[...]
        duc = do * gb
        ducdown, ducup = _neighbours(duc, dop_ref[...] * gb_p, don_ref[...] * gb_n, first, last)
        du = cw0 * ducup + cw1 * duc + cw2 * ducdown
        out_ref[:, D_SP:D_SP + D_CONV] = (do * (cw0 * udown + cw1 * u + cw2 * uup)).astype(out_ref.dtype)
        out_ref[:, D_SP + D_CONV:D_SP + 2 * D_CONV] = (du * hh).astype(out_ref.dtype)
        out_ref[:, D_SP + 2 * D_CONV:] = (du * gc).astype(out_ref.dtype)
        dcw_ref[0:1, :] += _colsum(duc * udown)
        dcw_ref[1:2, :] += _colsum(duc * u)
        dcw_ref[2:3, :] += _colsum(duc * uup)

    return pl.pallas_call(
        kern,
        out_shape=[jax.ShapeDtypeStruct((t, D_INP), BF16), jax.ShapeDtypeStruct((1, D_SP), F32),
                   jax.ShapeDtypeStruct((SUBLANES, D_CONV), F32)],
        grid=(t // tb,),
        in_specs=(_halo_specs(D_INP, 0, tb, t) + _halo_specs(D_SP, 0, tb, t) + _halo_specs(D_CONV, 1, tb, t)
                  + [pl.BlockSpec((1, D_SP), lambda i: (0, 0)),
                     pl.BlockSpec((SUBLANES, D_CONV), lambda i: (0, 0))]),
        out_specs=[pl.BlockSpec((tb, D_INP), lambda i: (i, 0)), pl.BlockSpec((1, D_SP), lambda i: (0, 0)),
                   pl.BlockSpec((SUBLANES, D_CONV), lambda i: (0, 0))],
        compiler_params=_params(("arbitrary",)), name=name)(p, p, p, d_pss, d_pss, d_pss, d_o, d_o, d_o, mu, conv_w)


N_CHAIN = 16
N_GROUP = LANES // N_CHAIN
V_HI = HEAD // SUBLANES
G_KK, G_R, G_W, G_B, G_KD = 0, 1, (2, 3), (4, 5), (6, 7)


K_HI = HEAD // SUBLANES


def _tree_sum(terms):
    terms = list(terms)
    while len(terms) > 1:
        terms = [a + b for a, b in zip(terms[::2], terms[1::2])]
    return terms[0]


def _kscan_specs(nc):
    same = lambda c: c
    mirror = lambda c: nc - 1 - c

    def k_spec(fn):
        return pl.BlockSpec((SCAN_CHUNK, HEAD, LANES), lambda c: (fn(c), 0, 0))

    def v_spec(fn):
        return pl.BlockSpec((SCAN_CHUNK, SUBLANES, LANES), lambda c: (fn(c), 0, 0))

    return same, mirror, k_spec, v_spec


ST_SHAPE = (2, K_HI, V_HI, SUBLANES, LANES)


def _lane_group_index():
    lane = lax.broadcasted_iota(jnp.int32, (SUBLANES, LANES), 1)
    return lax.shift_right_logical(lane, jnp.full_like(lane, 4))


def _spread_groups(x, grp):
    rolled = [x] + [pltpu.roll(x, s * N_CHAIN, 1) for s in range(1, N_GROUP)]
    out = []
    for j in range(N_GROUP):
        t = rolled[(0 - j) % N_GROUP]
        for g in range(1, N_GROUP):
            t = jnp.where(grp == g, rolled[(g - j) % N_GROUP], t)
        out.append(t)
    return out


def _gather_groups(tiles, grp):
    total = None
    for s in range(N_GROUP):
        b = tiles[s % N_GROUP]
        for g in range(1, N_GROUP):
            b = jnp.where(grp == g, tiles[(g + s) % N_GROUP], b)
        b = pltpu.roll(b, s * N_CHAIN, 1) if s else b
        total = b if total is None else total + b
    return total


def _lane_group_sum(x):
    return _tree_sum([x] + [pltpu.roll(x, k * N_CHAIN, 1) for k in range(1, N_GROUP)])


def _key_row(x_t, grp, kh):
    r = SUBLANES * grp + kh
    return jnp.broadcast_to(x_t[r:r + 1, :], (SUBLANES, LANES))


def _acc(total, term):
    return term if total is None else total + term


SA_SHAPE = (2, V_HI, SUBLANES, LANES)


def _scan_fwd(xall, v_c, *, gather=(), name):
    steps = xall.shape[0]
    nc = steps // SCAN_CHUNK
    same, mirror, k_spec, v_spec = _kscan_specs(nc)
    last = SCAN_CHUNK - 1
    n_x = len(gather)

    def kern(*refs):
        xf_ref, xb_ref, vf_ref, vb_ref = refs[:4]
        yf_ref, yb_ref, hist_ref, fin_ref, sa_ref = refs[4 + n_x:9 + n_x]
        st_ref = refs[9 + 2 * n_x]
        c = pl.program_id(0)

        def riders():
            return _exchange_copies(refs[4:4 + n_x], refs[9 + n_x:9 + 2 * n_x], 0, *refs[10 + 2 * n_x:])

        @pl.when(c == 0)
        def _():
            st_ref[...] = jnp.zeros_like(st_ref)
            if n_x:
                for cp in riders():
                    cp.start()

        hist_ref[0] = st_ref[...]
        grp = _lane_group_index()

        def body(i, put):
            j = last - i
            for d, (x_t, v_t, y_ref, at) in enumerate(((xf_ref[i], vf_ref[i], yf_ref, i),
                                                       (xb_ref[j], vb_ref[j], yb_ref, j))):
                v_b = _spread_groups(v_t, grp)
                part = [None] * V_HI
                for kh in range(K_HI):
                    kk_r = _key_row(x_t, G_KK, kh)
                    for vh in range(V_HI):
                        part[vh] = _acc(part[vh], hist_ref[i, d, kh, vh] * kk_r)
                sa = [_lane_group_sum(p) for p in part]
                for vh in range(V_HI):
                    sa_ref[i, d, vh] = sa[vh]
                y_p = [None] * V_HI
                for kh in range(K_HI):
                    r_r, w_r = _key_row(x_t, G_R, kh), _key_row(x_t, G_W[d], kh)
                    b_r, kd_r = _key_row(x_t, G_B[d], kh), _key_row(x_t, G_KD[d], kh)
                    for vh in range(V_HI):
                        new = hist_ref[i, d, kh, vh] * w_r - sa[vh] * b_r + v_b[vh] * kd_r
                        put(d, kh, vh, new)
                        y_p[vh] = _acc(y_p[vh], new * r_r)
                y_ref[at] = _gather_groups(y_p, grp)

        def step(i, carry):
            def put(d, kh, vh, val):
                hist_ref[i + 1, d, kh, vh] = val
            body(i, put)
            return carry

        lax.fori_loop(0, last, step, 0, unroll=SCAN_UNROLL)

        def put_carry(d, kh, vh, val):
            st_ref[d, kh, vh] = val

        body(last, put_carry)

        @pl.when(c == nc - 1)
        def _():
            fin_ref[...] = st_ref[...]
            if n_x:
                for cp in riders():
                    cp.wait()

    return pl.pallas_call(
        kern,
        out_shape=[jax.ShapeDtypeStruct((steps, SUBLANES, LANES), F32)] * 2
        + [jax.ShapeDtypeStruct((steps,) + ST_SHAPE, F32), jax.ShapeDtypeStruct(ST_SHAPE, F32),
           jax.ShapeDtypeStruct((steps,) + SA_SHAPE, F32)]
        + _exchange_out_shapes(gather, 0),
        grid=(nc,), in_specs=[k_spec(same), k_spec(mirror), v_spec(same), v_spec(mirror)] + _hbm_specs(n_x),
        out_specs=[v_spec(same), v_spec(mirror),
                   pl.BlockSpec((SCAN_CHUNK,) + ST_SHAPE, lambda c: (c, 0, 0, 0, 0, 0)),
                   pl.BlockSpec(ST_SHAPE, lambda c: (0, 0, 0, 0, 0)),
                   pl.BlockSpec((SCAN_CHUNK,) + SA_SHAPE, lambda c: (c, 0, 0, 0, 0))] + _hbm_specs(n_x),
        scratch_shapes=[pltpu.VMEM(ST_SHAPE, F32)] + (_exchange_sems(n_x) if n_x else []),
        compiler_params=_params(("arbitrary",), SCAN_VMEM_LIMIT), name=name)(xall, xall, v_c, v_c, *gather)


def _scan_bwd(xall, v_c, dy_c, hist, fin, sa, *, exchange=(), name):
    steps = xall.shape[0]
    nc = steps // SCAN_CHUNK
    same, back, k_spec, v_spec = _kscan_specs(nc)
    last = SCAN_CHUNK - 1
    n_x = len(exchange)

    def kern(*refs):
        xf_ref, xb_ref, vf_ref, vb_ref, dyf_ref, dyb_ref, hist_ref, fin_ref, sa_ref = refs[:9]
        gf_ref, gb_ref, dvf_ref, dvb_ref = refs[9 + n_x:13 + n_x]
        ds_ref, after_ref = refs[13 + 2 * n_x:15 + 2 * n_x]
        c = pl.program_id(0)

        def riders():
            return _exchange_copies(refs[9:9 + n_x], refs[13 + n_x:13 + 2 * n_x], n_x, *refs[15 + 2 * n_x:])

        @pl.when(c == 0)
        def _():
            ds_ref[...] = jnp.zeros_like(ds_ref)
            after_ref[...] = fin_ref[...]
            if n_x:
                for cp in riders():
                    cp.start()

        grp = _lane_group_index()
        row = lax.broadcasted_iota(jnp.int32, (SUBLANES, LANES), 0)
        zero = jnp.zeros((SUBLANES, LANES), F32)

        def body(i, after):
            j = last - i
            for d, (x_t, v_t, dy_t, g_ref, dv_ref, at) in enumerate((
                    (xf_ref[i], vf_ref[i], dyf_ref[i], gf_ref, dvf_ref, i),
                    (xb_ref[j], vb_ref[j], dyb_ref[j], gb_ref, dvb_ref, j))):
                v_s, dy_s = _spread_groups(v_t, grp), _spread_groups(dy_t, grp)
                dsa_p, dv_p = [None] * V_HI, [None] * V_HI
                for kh in range(K_HI):
                    r_r = _key_row(x_t, G_R, kh)
                    b_r, kd_r = _key_row(x_t, G_B[d], kh), _key_row(x_t, G_KD[d], kh)
                    for vh in range(V_HI):
                        g = ds_ref[d, kh, vh] + dy_s[vh] * r_r
                        ds_ref[d, kh, vh] = g
                        dsa_p[vh] = _acc(dsa_p[vh], g * b_r)
                        dv_p[vh] = _acc(dv_p[vh], g * kd_r)
                dsa = [-_lane_group_sum(p) for p in dsa_p]
                sa = [sa_ref[i, d, vh] for vh in range(V_HI)]
                dv_ref[at] = _gather_groups(dv_p, grp)
                blocks = {G_KK: zero, G_R: zero, G_W[d]: zero, G_B[d]: zero, G_KD[d]: zero}
                for kh in range(K_HI):
                    w_r, kk_r = _key_row(x_t, G_W[d], kh), _key_row(x_t, G_KK, kh)
                    dkk = dr = dw = db = dkd = None
                    for vh in range(V_HI):
                        g, before = ds_ref[d, kh, vh], hist_ref[i, d, kh, vh]
                        dr = _acc(dr, after(d, kh, vh) * dy_s[vh])
                        dw = _acc(dw, g * before)
                        dkd = _acc(dkd, g * v_s[vh])
                        db = _acc(db, g * sa[vh])
                        dkk = _acc(dkk, before * dsa[vh])
                        ds_ref[d, kh, vh] = g * w_r + dsa[vh] * kk_r
                    for gi, a in ((G_KK, dkk), (G_R, dr), (G_W[d], dw), (G_B[d], -db), (G_KD[d], dkd)):
                        blocks[gi] = jnp.where(row == kh, _colsum(a), blocks[gi])
                for gi in range(N_GROUP):
                    g_ref[at, SUBLANES * gi:SUBLANES * (gi + 1), :] = blocks.get(gi, zero)

        body(last, lambda d, kh, vh: after_ref[d, kh, vh])

        def step(ii, carry):
            i = last - ii
            body(i, lambda d, kh, vh: hist_ref[i + 1, d, kh, vh])
            return carry

        lax.fori_loop(1, SCAN_CHUNK, step, 0, unroll=SCAN_UNROLL)
        after_ref[...] = hist_ref[0]

        if n_x:
            @pl.when(c == nc - 1)
            def _():
                for cp in riders():
                    cp.wait()

    return pl.pallas_call(
        kern,
        out_shape=[jax.ShapeDtypeStruct((steps, HEAD, LANES), F32)] * 2
        + [jax.ShapeDtypeStruct((steps, SUBLANES, LANES), F32)] * 2 + _exchange_out_shapes(exchange, n_x),
        grid=(nc,),
        in_specs=[k_spec(back), k_spec(same), v_spec(back), v_spec(same), v_spec(back), v_spec(same),
                  pl.BlockSpec((SCAN_CHUNK,) + ST_SHAPE, lambda c: (back(c), 0, 0, 0, 0, 0)),
                  pl.BlockSpec(ST_SHAPE, lambda c: (0, 0, 0, 0, 0)),
                  pl.BlockSpec((SCAN_CHUNK,) + SA_SHAPE, lambda c: (back(c), 0, 0, 0, 0))] + _hbm_specs(n_x),
        out_specs=[k_spec(back), k_spec(same), v_spec(back), v_spec(same)] + _hbm_specs(n_x),
        scratch_shapes=[pltpu.VMEM(ST_SHAPE, F32), pltpu.VMEM(ST_SHAPE, F32)]
        + (_exchange_sems(n_x) if n_x else []),
        compiler_params=_params(("arbitrary",), SCAN_VMEM_LIMIT), name=name)(xall, xall, v_c, v_c, dy_c, dy_c, hist, fin, sa,
                                                            *exchange)


def _bf16_pieces(x):
    hi = x.astype(BF16)
    return hi, (x - hi.astype(F32)).astype(BF16)


def _to_key_rows(wide, bsz, seq, *, name):
    assert bsz == 2
    perm = _key_row_maps()
    tt = min(RELAYOUT_TILE, seq)
    per_seq = seq // tt

    def kern(x0_ref, x1_ref, p0_ref, p1_ref, o_ref):
        total = None
        for x_ref, p_ref in ((x0_ref, p0_ref), (x1_ref, p1_ref)):
            for piece in _bf16_pieces(x_ref[...]):
                term = jnp.dot(piece, p_ref[...], preferred_element_type=F32)
                total = term if total is None else total + term
        for r in range(K_HI):
            o_ref[:, r, :] = total[:, LANES * r:LANES * (r + 1)]

    p_spec = pl.BlockSpec((D_RWKV, K_HI * LANES), lambda i, a: (0, 0))
    return pl.pallas_call(
        kern, out_shape=jax.ShapeDtypeStruct((seq, HEAD, LANES), F32), grid=(per_seq, N_GROUP),
        in_specs=[pl.BlockSpec((tt, D_RWKV), lambda i, a: (i, a)),
                  pl.BlockSpec((tt, D_RWKV), lambda i, a: (per_seq + i, a)), p_spec, p_spec],
        out_specs=pl.BlockSpec((tt, K_HI, LANES), lambda i, a: (i, a, 0)),
        compiler_params=_params(("parallel", "parallel")), name=name)(wide, wide, *perm)


def _key_row_maps():
    src = jnp.arange(D_RWKV)
    head, kh, kl = src // HEAD, (src // SUBLANES) % K_HI, src % SUBLANES
    dst = jnp.arange(K_HI * LANES)
    return [((kh[:, None] == dst[None, :] // LANES) & (kl[:, None] == (dst[None, :] // N_CHAIN) % SUBLANES)
             & ((dst[None, :] // N_HEAD) % 2 == b) & (head[:, None] == dst[None, :] % N_HEAD)).astype(BF16)
            for b in range(2)]


def _from_key_rows(g_f, g_b, bsz, seq, *, name):
    assert bsz == 2
    maps = jnp.stack([m.T for m in _key_row_maps()])
    tt = min(RELAYOUT_TILE, seq)
    per_seq = seq // tt

    def kern(gf_ref, gb_ref, q_ref, o_ref):
        a = pl.program_id(2)
        shared = a <= G_R
        from_f = shared | (a % 2 == G_W[0] % 2)

        def rearranged(g_ref):
            g = jnp.concatenate([g_ref[:, r, :] for r in range(K_HI)], axis=1)
            hi, mid = (jnp.dot(piece, q_ref[0], preferred_element_type=F32) for piece in _bf16_pieces(g))
            return hi + mid

        @pl.when(from_f)
        def _():
            o_ref[...] = rearranged(gf_ref)

        @pl.when(jnp.logical_not(from_f))
        def _():
            o_ref[...] = rearranged(gb_ref)

        @pl.when(shared)
        def _():
            o_ref[...] += rearranged(gb_ref)

    g_spec = pl.BlockSpec((tt, K_HI, LANES), lambda b, i, a: (i, a, 0))
    return pl.pallas_call(
        kern, out_shape=jax.ShapeDtypeStruct((bsz * seq, N_GROUP * D_RWKV), F32), grid=(bsz, per_seq, N_GROUP),
        in_specs=[g_spec, g_spec, pl.BlockSpec((1, K_HI * LANES, D_RWKV), lambda b, i, a: (b, 0, 0))],
        out_specs=pl.BlockSpec((tt, D_RWKV), lambda b, i, a: (b * per_seq + i, a)),
        compiler_params=_params(("parallel", "parallel", "parallel")), name=name)(g_f, g_b, maps)


def _to_value_rows(a, bsz, seq):
    z = a.reshape(bsz, seq, N_HEAD, V_HI, SUBLANES).transpose(1, 4, 3, 0, 2)
    return z.reshape(seq, SUBLANES, LANES)


def _from_value_rows(y, bsz, seq):
    z = y.reshape(seq, SUBLANES, V_HI, bsz, N_HEAD).transpose(3, 0, 4, 2, 1)
    return z.reshape(bsz * seq, D_RWKV)


def _pad_cols(a, segs):
    out, off = [], 0
    for w, wp in segs:
        out.append(a[..., off:off + w])
        if wp > w:
            out.append(jnp.zeros(a.shape[:-1] + (wp - w,), a.dtype))
        off += w
    return jnp.concatenate(out, axis=-1)


def _unpad_cols(a, segs):
    out, off = [], 0
    for w, wp in segs:
        out.append(a[..., off:off + w])
        off += wp
    return jnp.concatenate(out, axis=-1)


P_SEGS = ((3 * D_RWKV, 3 * D_RWKV), (D_LORA, 128), (D_LORA, 128), (D_GATE, 256), (3 * D_CONV, 3 * D_CONV))
S_SEGS = P_SEGS[:4]


def _pad_rows(a, rows):
    return jnp.concatenate([a, jnp.zeros((rows - a.shape[0], a.shape[1]), a.dtype)], axis=0)


LATE = ("w_out", "w_gate", "w_up", "w_down")


def _local_step(x, target, w, late=None):
    bsz, seq, _ = x.shape
    t = bsz * seq
    x2d = x.reshape(t, D_MODEL)
    tg2d = target.reshape(t, D_MODEL)
    row = lambda a: a.reshape(1, -1).astype(F32)

    w_in = _pad_cols(w["w_in"][0], P_SEGS)
    mu = _pad_cols(row(w["mu_shift"]), S_SEGS)
    wupf, wupb, aupf, aupb = (_pad_rows(w[n][0].astype(F32), 128) for n in ("w_up_f", "w_up_b", "a_up_f", "a_up_b"))
    gup = _pad_rows(w["g_up"][0].astype(F32), 256)
    conv_w = _pad_rows(w["conv_w"][0].astype(F32), SUBLANES)
    norm1, norm2, normf = row(w["norm1_w"]), row(w["norm2_w"]), row(w["norm_f_w"])
    vec = {n: row(w[n]) for n in VEC}
    head_of = jnp.arange(LANES) // HEAD
    bd = (head_of[:, None] == head_of[None, :]).astype(F32)
    pre_consts = [vec["k_k"], vec["w0_f"], vec["w0_b"], vec["a0_f"], vec["a0_b"], vec["k_a_f"], vec["k_a_b"],
                  wupf, wupb, aupf, aupb, gup, bd]
    post_consts = [vec["gn_w"], vec["gn_b"], vec["r_k_f"], vec["r_k_b"], bd]

    h1, = _rowwise(_rms, [x2d], [norm1], [D_MODEL], [], name="rms1_fwd", out_dtype=BF16, tb=WIDE_TILE)
    p = _mm(h1, w_in, name="mm_in")
    pss, oconv = _shift_conv_fwd(p, mu, conv_w, seq, name="shift_conv_fwd")
    pre_rows = [(pss, 0, 512), (pss, 1, 512), (pss, XW0 // 128, 128), (pss, XA0 // 128, 128), (pss, XG0 // 256, 256)]
    sc, g = _rowwise(_prescan_math, pre_rows, pre_consts, [[D_RWKV] * N_GROUP, D_RWKV], [], name="prescan_fwd")
    xall = _to_key_rows(sc, bsz, seq, name="to_key_rows")
    v_l = _to_value_rows(pss[:, 2 * D_RWKV:3 * D_RWKV], bsz, seq)
    y_f, y_b, hist, fin, sa, *gathered = _scan_fwd(xall, v_l, gather=[late[n] for n in LATE] if late else (),
                                                   name="scan_fwd")
    w_out, w_gate, w_up, w_down = (
        (_from_slots(a, SHARD_AXIS[n]) if late else w[n])[0] for n, a in zip(LATE, gathered or LATE))
    y = _from_value_rows(y_f + y_b, bsz, seq)
    post_rows = [y, (pss, 0, 512), (pss, 2, 512), (sc, G_KD[0], 512), (sc, G_KD[1], 512), g]

    def post_fwd(y_, r_, v_, kdf_, kdb_, g_, oc_, *consts):
        return _postscan_math(y_, r_, v_, kdf_, kdb_, g_, *consts), oc_

    o, = _rowwise(post_fwd, post_rows + [oconv], post_consts, [[D_RWKV, D_CONV]], [], name="postscan_fwd",
                  out_dtype=BF16)
    x1 = _mm(o, w_out, add=x2d, name="mm_out")
    h2, = _rowwise(_rms, [x1], [norm2], [D_MODEL], [], name="rms2_fwd", out_dtype=BF16, tb=WIDE_TILE)
    gg, uu, ff = _mm_swiglu(h2, w_gate, w_up, name="mm_gate_up")
    x2 = _mm(ff, w_down, add=x1, name="mm_down")

    def final(x_, tg_, wn_):
        yo, vjp = jax.vjp(_rms, x_, wn_)
        err = yo - tg_
        dx_, dwn_ = vjp(err * (1.0 / D_MODEL))
        part = jnp.sum(jnp.sum(err * err, axis=1, keepdims=True), axis=0, keepdims=True) * (0.5 / D_MODEL)
        return dx_, part + jnp.zeros((1, LANES), F32), dwn_

    dx2, loss_acc, d_normf = _rowwise(final, [x2, tg2d], [normf], [D_MODEL], [(1, LANES), (1, D_MODEL)],
                                      name="loss_head", tb=WIDE_TILE)
    dgg, duu = _mm_swiglu_bwd(dx2, w_down, gg, uu, name="mm_down_dx")
    g_w_down = _mm(ff, dx2, ta=True, name="mm_down_dw")
    dh2 = _mm(dgg, w_gate, tb=True, name="mm_gate_dx")
    dh2 = _mm(duu, w_up, tb=True, add=dh2, name="mm_up_dx")
    g_w_gate = _mm(h2, dgg, ta=True, name="mm_gate_dw")
    g_w_up = _mm(h2, duu, ta=True, name="mm_up_dw")

    def rms_bwd(x_, dh_, dres_, wn_):
        _, vjp = jax.vjp(_rms, x_, wn_)
        dx_, dwn_ = vjp(dh_)
        return dx_ + dres_, dwn_

    dx1, d_norm2 = _rowwise(rms_bwd, [x1, dh2, dx2], [norm2], [D_MODEL], [(1, D_MODEL)], name="rms2_bwd", tb=WIDE_TILE)
    do = _mm(dx1, w_out, tb=True, name="mm_out_dx")
    g_w_out = _mm(o, dx1, ta=True, name="mm_out_dw")

    def post_bwd(y_, r_, v_, kdf_, kdb_, g_, do_, *consts):
        _, vjp = jax.vjp(lambda *a: _postscan_math(*a, consts[4]), y_, r_, v_, kdf_, kdb_, g_, *consts[:4])
        return vjp(do_)

    (dy, dr_c, dv_c, dkdf_c, dkdb_c, dg, d_gn_w, d_gn_b, d_rkf, d_rkb) = _rowwise(
        post_bwd, post_rows + [(do, 0, 512)], post_consts, [D_RWKV] * 6, [(1, D_RWKV)] * 4, name="postscan_bwd")
    dy_l = _to_value_rows(dy, bsz, seq)
    late_grads = {"w_out": g_w_out[None], "w_gate": g_w_gate[None], "w_up": g_w_up[None], "w_down": g_w_down[None]}
    g_f, g_b, dv_f, dv_b, *late_parts = _scan_bwd(
        xall, v_l, dy_l, hist, fin, sa, name="scan_bwd",
        exchange=[_to_slots(late_grads[n], SHARD_AXIS[n]).astype(BF16) for n in LATE] if late else ())
    dsc = _from_key_rows(g_f, g_b, bsz, seq, name="from_key_rows")
    dv_s = _from_value_rows(dv_f + dv_b, bsz, seq)

    def pre_bwd(r_, k_, xw_, xa_, xg_, dkk_, dr_s, dwf_, dwb_, dbf_, dbb_, dkdf_s, dkdb_s,
                dr_c_, dv_c_, dv_s_, dkdf_c_, dkdb_c_, dg_, *consts):
        _, vjp = jax.vjp(lambda *a: _prescan_math(*a, consts[-1]), r_, k_, xw_, xa_, xg_, *consts[:-1])
        grads = vjp((dkk_, dr_s + dr_c_, dwf_, dwb_, dbf_, dbb_, dkdf_s + dkdf_c_, dkdb_s + dkdb_c_, dg_))
        dr_, dk_, dxw_, dxa_, dxg_ = grads[:5]
        return (dr_, dk_, dv_c_ + dv_s_, dxw_, dxa_, dxg_) + tuple(grads[5:])

    pre_b_rows = (pre_rows + [(dsc, j, 512) for j in range(N_GROUP)]
                  + [dr_c, dv_c, dv_s, dkdf_c, dkdb_c, dg])
    pre_b = _rowwise(pre_bwd, pre_b_rows, pre_consts, [[512, 512, 512, 128, 128, 256]],
                     [(1, D_RWKV)] * 7 + [(128, D_RWKV)] * 4 + [(256, D_RWKV)], name="prescan_bwd")
    d_pss = pre_b[0]
    d_kk_, d_w0f, d_w0b, d_a0f, d_a0b, d_kaf, d_kab, d_wupf, d_wupb, d_aupf, d_aupb, d_gup = pre_b[1:]
    dp, d_mu, d_conv = _shift_conv_bwd(p, d_pss, do, mu, conv_w, seq, name="shift_conv_bwd")
    g_w_in = _mm(h1, dp, ta=True, name="mm_in_dw")
    grads = {
        "w_in": _unpad_cols(g_w_in, P_SEGS)[None], "mu_shift": _unpad_cols(d_mu, S_SEGS),
        "w_up_f": d_wupf[None, :D_LORA], "w0_f": d_w0f, "w_up_b": d_wupb[None, :D_LORA], "w0_b": d_w0b,
        "a_up_f": d_aupf[None, :D_LORA], "a0_f": d_a0f, "a_up_b": d_aupb[None, :D_LORA], "a0_b": d_a0b,
        "g_up": d_gup[None, :D_GATE], "k_k": d_kk_, "k_a_f": d_kaf, "k_a_b": d_kab,
        "r_k_f": d_rkf, "r_k_b": d_rkb, "gn_w": d_gn_w, "gn_b": d_gn_b, "conv_w": d_conv[None, :3],
        "w_out": g_w_out[None], "norm2_w": d_norm2, "w_gate": g_w_gate[None], "w_up": g_w_up[None],
        "w_down": g_w_down[None], "norm_f_w": d_normf,
    }
    early = ("w_in",) + LORA
    parts = dict(zip(LATE, late_parts))
    if late:
        vec_rows = jnp.concatenate([grads[n] for n in VEC] + [jnp.zeros((16 - len(VEC), D_RWKV), F32)], axis=0)
        slots = [_to_slots(grads[n], SHARD_AXIS[n]).astype(BF16 if n in BIG else F32) for n in early]
        dh1, *recv = _mm(dp, w_in, tb=True, exchange=(slots, [vec_rows]), name="mm_in_dx")
        parts.update(zip(early + ("vec",), recv))
    else:
        dh1 = _mm(dp, w_in, tb=True, name="mm_in_dx")
    dx, grads["norm1_w"] = _rowwise(rms_bwd, [x2d, dh1, dx1], [norm1], [D_MODEL], [(1, D_MODEL)], name="rms1_bwd",
                                    tb=WIDE_TILE)
    return loss_acc, dx.reshape(bsz, seq, D_MODEL), grads, parts


def _hbm_specs(n):
    return [pl.BlockSpec(memory_space=pl.ANY)] * n


def _all_gather(arrs, *, name):
    n = len(arrs)

    def body(*refs):
        x_refs, out_refs = refs[:n], refs[n:2 * n]
        send_sems, recv_sems, local_sems = refs[2 * n:]
        x, y, c = lax.axis_index("x"), lax.axis_index("y"), lax.axis_index("c")
        me, sibling = (x, y, c), (x, y, 1 - c)
        chips = [(1 - x, y), (x, 1 - y), (1 - x, 1 - y)]

        def slot(a, px, py, pc):
            return out_refs[a].at[4 * px + 2 * py + pc]

        def copy(a, k, block, to, src=None):
            return pltpu.make_async_remote_copy(
                src_ref=slot(a, *block) if src is None else src, dst_ref=slot(a, *block),
                send_sem=send_sems.at[k, a], recv_sem=recv_sems.at[k, a],
                device_id=to, device_id_type=pl.DeviceIdType.MESH)

        mine = [pltpu.make_async_copy(x_refs[a], slot(a, *me), local_sems.at[a]) for a in range(n)]
        for cp in mine:
            cp.start()
        first = []
        for a in range(n):
            first.append(copy(a, 0, me, sibling, src=x_refs[a]))
            first += [copy(a, 1 + j, me, (*chip, c), src=x_refs[a]) for j, chip in enumerate(chips)]
        for cp in first:
            cp.start()
        passed = []
        for j, chip in enumerate(chips):
            for a in range(n):
                copy(a, 1 + j, (*chip, c), me).wait_recv()
                cp = copy(a, 4 + j, (*chip, c), sibling)
                cp.start()
                passed.append(cp)
        for a in range(n):
            copy(a, 0, sibling, me).wait_recv()
            for j, chip in enumerate(chips):
                copy(a, 4 + j, (*chip, 1 - c), me).wait_recv()
        for cp in first + passed:
            cp.wait_send()
        for cp in mine:
            cp.wait()

    return pl.pallas_call(
        body, out_shape=[jax.ShapeDtypeStruct((N_DEV,) + a.shape, a.dtype) for a in arrs],
        in_specs=_hbm_specs(n), out_specs=_hbm_specs(n),
        scratch_shapes=[pltpu.SemaphoreType.DMA((7, n)), pltpu.SemaphoreType.DMA((7, n)),
                        pltpu.SemaphoreType.DMA((n,))],
        name=name)(*arrs)


def _exchange(sliced, whole, *, name):
    arrs = list(sliced) + list(whole)
    n, n_sliced = len(arrs), len(sliced)

    def body(*refs):
        copies = _exchange_copies(refs[:n], refs[n:2 * n], n_sliced, *refs[2 * n:])
        for cp in copies:
            cp.start()
        for cp in copies:
            cp.wait()

    return pl.pallas_call(
        body, out_shape=_exchange_out_shapes(arrs, n_sliced), in_specs=_hbm_specs(n), out_specs=_hbm_specs(n),
        scratch_shapes=_exchange_sems(n), name=name)(*arrs)


def _exchange_out_shapes(arrs, n_sliced):
    return [jax.ShapeDtypeStruct(a.shape if i < n_sliced else (N_DEV,) + a.shape, a.dtype)
            for i, a in enumerate(arrs)]


def _exchange_sems(n):
    return [pltpu.SemaphoreType.DMA((7, n)), pltpu.SemaphoreType.DMA((7, n)), pltpu.SemaphoreType.DMA((n,))]


def _exchange_copies(in_refs, out_refs, n_sliced, send_sems, recv_sems, local_sems):
    n = len(in_refs)
    x, y, c = lax.axis_index("x"), lax.axis_index("y"), lax.axis_index("c")
    me = 4 * x + 2 * y + c

    def src(a, dev):
        return in_refs[a].at[dev] if a < n_sliced else in_refs[a]

    copies = [pltpu.make_async_copy(src(a, me), out_refs[a].at[me], local_sems.at[a]) for a in range(n)]
    for k in range(1, N_DEV):
        px = 1 - x if k & 4 else x
        py = 1 - y if k & 2 else y
        pc = 1 - c if k & 1 else c
        for a in range(n):
            copies.append(pltpu.make_async_remote_copy(
                src_ref=src(a, 4 * px + 2 * py + pc), dst_ref=out_refs[a].at[me],
                send_sem=send_sems.at[k - 1, a], recv_sem=recv_sems.at[k - 1, a],
                device_id=(px, py, pc), device_id_type=pl.DeviceIdType.MESH))
    return copies


def _adam_math(g, w, m, v):
    nm = ADAM_B1 * m + (1.0 - ADAM_B1) * g
    nv = ADAM_B2 * v + (1.0 - ADAM_B2) * (g * g)
    m_hat = nm / (1.0 - ADAM_B1 ** ADAM_STEP)
    v_hat = nv / (1.0 - ADAM_B2 ** ADAM_STEP)
    return -ADAM_LR * (m_hat / (jnp.sqrt(v_hat) + ADAM_EPS) + ADAM_WD * w), nm, nv


def _slot_sum(ref):
    g = ref[0].astype(F32)
    for s in range(1, N_DEV):
        g = g + ref[s].astype(F32)
    return g


def _adamw_big(parts, w, m, v, *, name):
    _, rws, cols = w.shape
    tr = _tile(rws, (256, 176, 128))

    def kern(p_ref, w_ref, m_ref, v_ref, g_ref, d_ref, nm_ref, nv_ref):
        g = _slot_sum(p_ref)
        g_ref[...] = g
        d_ref[...], nm_ref[...], nv_ref[...] = _adam_math(g, w_ref[...], m_ref[...], v_ref[...])

    spec = pl.BlockSpec((1, tr, cols), lambda i: (0, i, 0))
    return pl.pallas_call(
        kern, out_shape=[jax.ShapeDtypeStruct(w.shape, F32)] * 4, grid=(rws // tr,),
        in_specs=[pl.BlockSpec((N_DEV, 1, tr, cols), lambda i: (0, 0, i, 0)), spec, spec, spec],
        out_specs=[spec] * 4, compiler_params=_params(("parallel",)), name=name)(parts, w, m, v)


def _adamw_small(lora_parts, vec_parts, wide_parts, wmv, *, name):
    names = LORA + VEC + WIDE
    n_l, n = len(LORA), len(names)
    flat = [a for trip in wmv for a in trip]

    def kern(*refs):
        l_refs, vec_ref, wide_ref = refs[:n_l], refs[n_l], refs[n_l + 1]
        in_refs = refs[n_l + 2:n_l + 2 + 3 * n]
        out_refs = refs[n_l + 2 + 3 * n:]
        vec_sum, wide_sum = _slot_sum(vec_ref), _slot_sum(wide_ref)
        for i, nm in enumerate(names):
            w_ref, m_ref, v_ref = in_refs[3 * i:3 * i + 3]
            if i < n_l:
                g = _slot_sum(l_refs[i])
            elif nm in VEC:
                g = vec_sum[i - n_l:i - n_l + 1, :]
            else:
                g = wide_sum[WIDE.index(nm):WIDE.index(nm) + 1, :w_ref.shape[-1]]
            o = out_refs[4 * i:4 * i + 4]
            o[0][...] = g
            o[1][...], o[2][...], o[3][...] = _adam_math(g, w_ref[...], m_ref[...], v_ref[...])

    out_shape = [jax.ShapeDtypeStruct(trip[0].shape, F32) for trip in wmv for _ in range(4)]
    outs = pl.pallas_call(kern, out_shape=out_shape, name=name,
                          compiler_params=pltpu.CompilerParams(vmem_limit_bytes=VMEM_LIMIT))(
        *lora_parts, vec_parts, wide_parts, *flat)
    return [tuple(outs[4 * i:4 * i + 4]) for i in range(n)]


def _to_slots(g, axis):
    _, rws, cols = g.shape
    if axis == 1:
        return g.reshape(N_DEV, 1, rws // N_DEV, cols)
    return g.reshape(1, rws, N_DEV, cols // N_DEV).transpose(2, 0, 1, 3)


def _from_slots(got, axis):
    _, _, rws, cols = got.shape
    if axis == 1:
        return got.reshape(1, N_DEV * rws, cols)
    return got.transpose(1, 2, 0, 3).reshape(1, rws, N_DEV * cols)


def _pad_lanes(a, width):
    return jnp.concatenate([a, jnp.zeros(a.shape[:-1] + (width - a.shape[-1],), a.dtype)], axis=-1)


def kernel(x, norm1_w, w_in, mu_shift, w_up_f, w0_f, w_up_b, w0_b, a_up_f, a0_f, a_up_b, a0_b, g_up, k_k, k_a_f, k_a_b, r_k_f, r_k_b, gn_w, gn_b, conv_w, w_out, norm2_w, w_gate, w_up, w_down, norm_f_w, loss_target, m_norm1_w, m_w_in, m_mu_shift, m_w_up_f, m_w0_f, m_w_up_b, m_w0_b, m_a_up_f, m_a0_f, m_a_up_b, m_a0_b, m_g_up, m_k_k, m_k_a_f, m_k_a_b, m_r_k_f, m_r_k_b, m_gn_w, m_gn_b, m_conv_w, m_w_out, m_norm2_w, m_w_gate, m_w_up, m_w_down, m_norm_f_w, v_norm1_w, v_w_in, v_mu_shift, v_w_up_f, v_w0_f, v_w_up_b, v_w0_b, v_a_up_f, v_a0_f, v_a_up_b, v_a0_b, v_g_up, v_k_k, v_k_a_f, v_k_a_b, v_r_k_f, v_r_k_b, v_gn_w, v_gn_b, v_conv_w, v_w_out, v_norm2_w, v_w_gate, v_w_up, v_w_down, v_norm_f_w):
    local = dict(norm1_w=norm1_w, w_in=w_in, mu_shift=mu_shift, w_up_f=w_up_f, w0_f=w0_f, w_up_b=w_up_b,
                 w0_b=w0_b, a_up_f=a_up_f, a0_f=a0_f, a_up_b=a_up_b, a0_b=a0_b, g_up=g_up, k_k=k_k, k_a_f=k_a_f,
                 k_a_b=k_a_b, r_k_f=r_k_f, r_k_b=r_k_b, gn_w=gn_w, gn_b=gn_b, conv_w=conv_w, w_out=w_out,
                 norm2_w=norm2_w, w_gate=w_gate, w_up=w_up, w_down=w_down, norm_f_w=norm_f_w)
    mom_m = dict(norm1_w=m_norm1_w, w_in=m_w_in, mu_shift=m_mu_shift, w_up_f=m_w_up_f, w0_f=m_w0_f,
                 w_up_b=m_w_up_b, w0_b=m_w0_b, a_up_f=m_a_up_f, a0_f=m_a0_f, a_up_b=m_a_up_b, a0_b=m_a0_b,
                 g_up=m_g_up, k_k=m_k_k, k_a_f=m_k_a_f, k_a_b=m_k_a_b, r_k_f=m_r_k_f, r_k_b=m_r_k_b,
                 gn_w=m_gn_w, gn_b=m_gn_b, conv_w=m_conv_w, w_out=m_w_out, norm2_w=m_norm2_w, w_gate=m_w_gate,
                 w_up=m_w_up, w_down=m_w_down, norm_f_w=m_norm_f_w)
    mom_v = dict(norm1_w=v_norm1_w, w_in=v_w_in, mu_shift=v_mu_shift, w_up_f=v_w_up_f, w0_f=v_w0_f,
                 w_up_b=v_w_up_b, w0_b=v_w0_b, a_up_f=v_a_up_f, a0_f=v_a0_f, a_up_b=v_a_up_b, a0_b=v_a0_b,
                 g_up=v_g_up, k_k=v_k_k, k_a_f=v_k_a_f, k_a_b=v_k_a_b, r_k_f=v_r_k_f, r_k_b=v_r_k_b,
                 gn_w=v_gn_w, gn_b=v_gn_b, conv_w=v_conv_w, w_out=v_w_out, norm2_w=v_norm2_w, w_gate=v_w_gate,
                 w_up=v_w_up, w_down=v_w_down, norm_f_w=v_norm_f_w)

    early = ("w_in",) + LORA
    got = _all_gather([local["w_in"].astype(BF16)] + [local[n] for n in LORA], name="gather")
    full = dict(local)
    full.update({n: _from_slots(a, SHARD_AXIS[n]) for n, a in zip(early, got)})

    loss_part, grad_x, grads, parts = _local_step(x, loss_target, full,
                                                  late={n: local[n].astype(BF16) for n in LATE})

    wide_rows = jnp.concatenate([_pad_lanes(a, WIDE_ROW) for a in [grads[n] for n in WIDE] + [loss_part]]
                                + [jnp.zeros((SUBLANES - len(WIDE) - 1, WIDE_ROW), F32)], axis=0)
    wide_parts, = _exchange([], [wide_rows], name="grad_exchange")
    loss = jnp.sum(wide_parts[:, len(WIDE), 0])
    out = {}
    for n in BIG:
        out[n] = _adamw_big(parts[n], local[n], mom_m[n], mom_v[n], name="adamw_" + n)

    def small_form(n, a):
        if n in LORA:
            return a
        a = a.reshape(1, -1)
        return _pad_lanes(a, WIDE_ROW) if n == "mu_shift" else a

    small = LORA + VEC + WIDE
    res = _adamw_small([parts[n] for n in LORA], parts["vec"], wide_parts,
                       [tuple(small_form(n, d[n]) for d in (local, mom_m, mom_v)) for n in small],
                       name="adamw_small")
    for n, quad in zip(small, res):
        out[n] = tuple(a[..., :local[n].size].reshape(local[n].shape) if n not in LORA else a for a in quad)
    return (loss, grad_x, *[out[n][i] for i in range(4) for n in WEIGHTS])
```

```python
import functools

import jax
import jax.numpy as jnp
from jax import lax
from jax.experimental import pallas as pl
from jax.experimental.pallas import tpu as pltpu

F32 = jnp.float32
BF16 = jnp.bfloat16
HIGHEST = lax.Precision.HIGHEST

N_DEV = 8
D_MODEL = 1024
D_RWKV = 512
D_CONV = 512
HEAD = 64
N_HEAD = D_RWKV // HEAD
D_LORA = 64
D_GATE = 160
D_SHIFTED = 3 * D_RWKV + 2 * D_LORA + D_GATE
XW0, XA0, XG0 = 1536, 1664, 1792
D_SP = 2048
D_INP = D_SP + 3 * D_CONV
LOG_DECAY_SCALE = 0.606531
RMS_EPS = 1e-6
GN_EPS = 64e-5
NORM_EPS = 1e-12
ADAM_LR, ADAM_B1, ADAM_B2, ADAM_EPS, ADAM_WD, ADAM_STEP = 0.001, 0.9, 0.999, 1e-08, 0.01, 10

LANES = 128
SUBLANES = 8
VMEM_LIMIT = 48 * 1024 * 1024
SCAN_CHUNK = 32
SCAN_VMEM_LIMIT = 58 * 1024 * 1024
SCAN_UNROLL = 3
ROW_TILE = 128
WIDE_TILE = 512
RELAYOUT_TILE = 1024

BIG = ("w_in", "w_out", "w_gate", "w_up", "w_down")
LORA = ("w_up_f", "w_up_b", "a_up_f", "a_up_b", "g_up", "conv_w")
SHARD_AXIS = {"w_in": 2, "w_out": 1, "w_gate": 2, "w_up": 2, "w_down": 1, "w_up_f": 2, "w_up_b": 2,
              "a_up_f": 2, "a_up_b": 2, "g_up": 2, "conv_w": 2}
VEC = ("w0_f", "w0_b", "a0_f", "a0_b", "k_k", "k_a_f", "k_a_b", "r_k_f", "r_k_b", "gn_w", "gn_b")
WIDE = ("mu_shift", "norm1_w", "norm2_w", "norm_f_w")
WIDE_ROW = 2048
WEIGHTS = ("norm1_w", "w_in", "mu_shift", "w_up_f", "w0_f", "w_up_b", "w0_b", "a_up_f", "a0_f", "a_up_b",
           "a0_b", "g_up", "k_k", "k_a_f", "k_a_b", "r_k_f", "r_k_b", "gn_w", "gn_b", "conv_w", "w_out",
           "norm2_w", "w_gate", "w_up", "w_down", "norm_f_w")


def _params(sem, limit=VMEM_LIMIT):
    return pltpu.CompilerParams(dimension_semantics=sem, vmem_limit_bytes=limit)


def _tile(n, cands):
    for c in cands:
        if n % c == 0:
            return c
    raise ValueError(f"no tile for {n}")


def _mm(a, b, *, ta=False, tb=False, add=None, exchange=None, name):
    (k_dim, m) = a.shape if ta else a.shape[::-1]
    (k2, n) = b.shape[::-1] if tb else b.shape
    assert k_dim == k2, (a.shape, b.shape, ta, tb)
    tm = _tile(m, (1408, 1024, 512, 256, 128))
    tn = _tile(n, (1408, 1024, 896, 512, 256, 128))
    tk = _tile(k_dim, (1408, 1024, 896, 512, 256, 128))
    nk = k_dim // tk
    grid = (m // tm, n // tn, nk)
    dims = (((0 if ta else 1,), (1 if tb else 0,)), ((), ()))
    sliced, whole = exchange or ((), ())
    riders = list(sliced) + list(whole)
    n_x, n_in = len(riders), 2 + (add is not None)

    def kern(*refs):
        a_ref, b_ref = refs[:2]
        add_ref = refs[2] if add is not None else None
        o_ref, acc_ref = refs[n_in + n_x], refs[n_in + 2 * n_x + 1]
        k = pl.program_id(2)
        step = (pl.program_id(0) * grid[1] + pl.program_id(1)) * nk + k

        def copies():
            return _exchange_copies(refs[n_in:n_in + n_x], refs[n_in + n_x + 1:n_in + 2 * n_x + 1], len(sliced),
                                    *refs[n_in + 2 * n_x + 2:])

        if n_x:
            @pl.when(step == 0)
            def _():
                for cp in copies():
                    cp.start()

        @pl.when(k == 0)
        def _():
            acc_ref[...] = jnp.zeros_like(acc_ref)

        acc_ref[...] += lax.dot_general(a_ref[...].astype(BF16), b_ref[...].astype(BF16), dims,
                                        preferred_element_type=F32)

        @pl.when(k == nk - 1)
        def _():
            if add is None:
                o_ref[...] = acc_ref[...]
            else:
                o_ref[...] = acc_ref[...] + add_ref[...]

        if n_x:
            @pl.when(step == grid[0] * grid[1] * nk - 1)
            def _():
                for cp in copies():
                    cp.wait()

    a_spec = (pl.BlockSpec((tk, tm), lambda i, j, k: (k, i)) if ta
              else pl.BlockSpec((tm, tk), lambda i, j, k: (i, k)))
    b_spec = (pl.BlockSpec((tn, tk), lambda i, j, k: (j, k)) if tb
              else pl.BlockSpec((tk, tn), lambda i, j, k: (k, j)))
    o_spec = pl.BlockSpec((tm, tn), lambda i, j, k: (i, j))
    in_specs = [a_spec, b_spec] + ([o_spec] if add is not None else []) + _hbm_specs(n_x)
    args = (a, b) + ((add,) if add is not None else ()) + tuple(riders)
    out = pl.pallas_call(
        kern, out_shape=[jax.ShapeDtypeStruct((m, n), F32)] + _exchange_out_shapes(riders, len(sliced)), grid=grid,
        in_specs=in_specs, out_specs=[o_spec] + _hbm_specs(n_x),
        scratch_shapes=[pltpu.VMEM((tm, tn), F32)] + (_exchange_sems(n_x) if n_x else []),
        compiler_params=_params(("arbitrary",) * 3 if n_x else ("parallel", "parallel", "arbitrary")),
        name=name)(*args)
    return out if n_x else out[0]


def _swiglu(g, u):
    return jax.nn.silu(g) * u


FFN_TN = 256


def _mm_swiglu(h, w_gate, w_up, *, name):
    m, k_dim = h.shape
    n = w_gate.shape[1]
    tm = _tile(m, (1024, 512, 256, 128))

    def kern(h_ref, wg_ref, wu_ref, g_ref, u_ref, f_ref):
        hv = h_ref[...].astype(BF16)
        g = jnp.dot(hv, wg_ref[...].astype(BF16), preferred_element_type=F32)
        u = jnp.dot(hv, wu_ref[...].astype(BF16), preferred_element_type=F32)
        g_ref[...] = g
        u_ref[...] = u
        f_ref[...] = _swiglu(g, u).astype(f_ref.dtype)

    w_spec = pl.BlockSpec((k_dim, FFN_TN), lambda i, j: (0, j))
    o_spec = pl.BlockSpec((tm, FFN_TN), lambda i, j: (i, j))
    return pl.pallas_call(
        kern, out_shape=[jax.ShapeDtypeStruct((m, n), F32)] * 2 + [jax.ShapeDtypeStruct((m, n), BF16)],
        grid=(m // tm, n // FFN_TN), in_specs=[pl.BlockSpec((tm, k_dim), lambda i, j: (i, 0)), w_spec, w_spec],
        out_specs=[o_spec] * 3, compiler_params=_params(("parallel", "parallel")), name=name)(h, w_gate, w_up)


def _mm_swiglu_bwd(dx, w_down, g, u, *, name):
    m, k_dim = dx.shape
    n = w_down.shape[0]
    tm = _tile(m, (1024, 512, 256, 128))

    def kern(dx_ref, w_ref, g_ref, u_ref, dg_ref, du_ref):
        df = lax.dot_general(dx_ref[...].astype(BF16), w_ref[...].astype(BF16), (((1,), (1,)), ((), ())),
                             preferred_element_type=F32)
        _, vjp = jax.vjp(_swiglu, g_ref[...], u_ref[...])
        dg, du = vjp(df)
        dg_ref[...] = dg.astype(dg_ref.dtype)
        du_ref[...] = du.astype(du_ref.dtype)

    o_spec = pl.BlockSpec((tm, FFN_TN), lambda i, j: (i, j))
    return pl.pallas_call(
        kern, out_shape=[jax.ShapeDtypeStruct((m, n), BF16)] * 2, grid=(m // tm, n // FFN_TN),
        in_specs=[pl.BlockSpec((tm, k_dim), lambda i, j: (i, 0)), pl.BlockSpec((FFN_TN, k_dim), lambda i, j: (j, 0)),
                  o_spec, o_spec],
        out_specs=[o_spec] * 2, compiler_params=_params(("parallel", "parallel")), name=name)(dx, w_down, g, u)


def _rowwise(fn, rows, consts, out_rows, out_accs, *, name, tb=ROW_TILE, out_dtype=F32):
    t = (rows[0][0] if isinstance(rows[0], tuple) else rows[0]).shape[0]
    tb = min(tb, t)
    n_r, n_c, n_o, n_a = len(rows), len(consts), len(out_rows), len(out_accs)
    pieces = [w if isinstance(w, (list, tuple)) else [w] for w in out_rows]

    def kern(*refs):
        r_refs = refs[:n_r]
        c_refs = refs[n_r:n_r + n_c]
        o_refs = refs[n_r + n_c:n_r + n_c + n_o]
        a_refs = refs[n_r + n_c + n_o:]
        vals = fn(*[r[...] for r in r_refs], *[c[...] for c in c_refs])
        vals = list(vals) if isinstance(vals, (tuple, list)) else [vals]
        pos = 0
        for o_ref, ws in zip(o_refs, pieces):
            off = 0
            for w in ws:
                o_ref[:, off:off + w] = vals[pos].astype(o_ref.dtype)
                off += w
                pos += 1
        if n_a:
            @pl.when(pl.program_id(0) == 0)
            def _():
                for a_ref in a_refs:
                    a_ref[...] = jnp.zeros_like(a_ref)
            for a_ref, v in zip(a_refs, vals[pos:]):
                a_ref[...] += v

    in_specs, args = [], []
    for r in rows:
        if isinstance(r, tuple):
            arr, blk, w = r
            in_specs.append(pl.BlockSpec((tb, w), functools.partial(lambda i, blk: (i, blk), blk=blk)))
        else:
            arr = r
            in_specs.append(pl.BlockSpec((tb, arr.shape[1]), lambda i: (i, 0)))
        args.append(arr)
    for c in consts:
        in_specs.append(pl.BlockSpec(c.shape, lambda i: (0, 0)))
        args.append(c)
    out_shape = [jax.ShapeDtypeStruct((t, sum(ws)), out_dtype) for ws in pieces]
    out_specs = [pl.BlockSpec((tb, sum(ws)), lambda i: (i, 0)) for ws in pieces]
    for shp in out_accs:
        out_shape.append(jax.ShapeDtypeStruct(shp, F32))
        out_specs.append(pl.BlockSpec(shp, lambda i: (0, 0)))
    res = pl.pallas_call(
        kern, out_shape=out_shape, grid=(t // tb,), in_specs=in_specs, out_specs=out_specs,
        compiler_params=_params(("arbitrary",) if n_a else ("parallel",)), name=name)(*args)
    return res


def _rms(x, w):
    return x * lax.rsqrt(jnp.mean(x * x, axis=-1, keepdims=True) + RMS_EPS) * w


def _seg_sum(x, bd):
    return jnp.concatenate(
        [jnp.dot(x[:, LANES * j:LANES * (j + 1)], bd, precision=HIGHEST, preferred_element_type=F32)
         for j in range(x.shape[1] // LANES)], axis=1)


@jax.custom_vjp
def _seg(x, bd):
    return _seg_sum(x, bd)


_seg.defvjp(lambda x, bd: (_seg_sum(x, bd), bd), lambda bd, ct: (_seg_sum(ct, bd), jnp.zeros_like(bd)))


def _colsum(x):
    return jnp.sum(x, axis=0, keepdims=True)


def _prescan_math(r, k, xw, xa, xg, k_k, w0f, w0b, a0f, a0b, kaf, kab, wupf, wupb, aupf, aupb, gup, bd):
    kkr = k * k_k
    norm = jnp.sqrt(_seg(kkr * kkr, bd))
    kk = kkr / jnp.maximum(norm, NORM_EPS)
    th = jnp.tanh(xw)

    def direction(w0, wup, a0, aup, ka):
        logit = w0 + jnp.dot(th, wup, preferred_element_type=F32)
        w = jnp.exp(-LOG_DECAY_SCALE * jax.nn.sigmoid(logit))
        a = jax.nn.sigmoid(a0 + jnp.dot(xa, aup, preferred_element_type=F32))
        kd = k * (1.0 + (a - 1.0) * ka)
        return w, kd, kk * a

    wf, kdf, bf = direction(w0f, wupf, a0f, aupf, kaf)
    wb, kdb, bb = direction(w0b, wupb, a0b, aupb, kab)
    g = jnp.dot(jax.nn.sigmoid(xg), gup, preferred_element_type=F32)
    return kk, r, wf, wb, bf, bb, kdf, kdb, g


def _postscan_math(y, r, v, kdf, kdb, g, gn_w, gn_b, rkf, rkb, bd):
    mean = _seg(y, bd) * (1.0 / HEAD)
    yc = y - mean
    var = _seg(yc * yc, bd) * (1.0 / HEAD)
    yg = yc * lax.rsqrt(var + GN_EPS) * gn_w + gn_b
    bonus = (_seg(r * kdf * rkf, bd) + _seg(r * kdb * rkb, bd)) * v
    return (yg + bonus) * g


def _halo_specs(width, col_blk, tb, t):
    nb = t // SUBLANES
    step = tb // SUBLANES
    main = pl.BlockSpec((tb, width), lambda i: (i, col_blk))
    prev = pl.BlockSpec((SUBLANES, width), lambda i: (jnp.maximum(i * step - 1, 0), col_blk))
    nxt = pl.BlockSpec((SUBLANES, width), lambda i: (jnp.minimum((i + 1) * step, nb - 1), col_blk))
    return [main, prev, nxt]


def _neighbours(z, prev8, next8, first, last):
    tb = z.shape[0]
    row = lax.broadcasted_iota(jnp.int32, z.shape, 0)
    prow = jnp.where(first, 0.0, prev8[SUBLANES - 1:SUBLANES, :])
    nrow = jnp.where(last, 0.0, next8[0:1, :])
    down = jnp.where(row == 0, prow, pltpu.roll(z, 1, 0))
    up = jnp.where(row == tb - 1, nrow, pltpu.roll(z, tb - 1, 0))
    return down, up


def _shift_conv_fwd(p, mu, conv_w, seq, *, name, tb=ROW_TILE):
    t = p.shape[0]
    per_seq = seq // tb

    def kern(p_ref, pp_ref, pn_ref, mu_ref, cw_ref, pss_ref, oc_ref):
        i = pl.program_id(0)
        first = (i % per_seq) == 0
        last = (i % per_seq) == per_seq - 1
        ps = p_ref[:, :D_SP]
        down, up = _neighbours(ps, pp_ref[:, :D_SP], pn_ref[:, :D_SP], first, last)
        pss_ref[...] = ps + mu_ref[...] * (0.5 * (down + up) - ps)
        gb = p_ref[:, D_SP:D_SP + D_CONV]
        u = p_ref[:, D_SP + D_CONV:D_SP + 2 * D_CONV] * p_ref[:, D_SP + 2 * D_CONV:]
        u_p = pp_ref[:, D_SP + D_CONV:D_SP + 2 * D_CONV] * pp_ref[:, D_SP + 2 * D_CONV:]
        u_n = pn_ref[:, D_SP + D_CONV:D_SP + 2 * D_CONV] * pn_ref[:, D_SP + 2 * D_CONV:]
        udown, uup = _neighbours(u, u_p, u_n, first, last)
        oc_ref[...] = gb * (cw_ref[0:1, :] * udown + cw_ref[1:2, :] * u + cw_ref[2:3, :] * uup)

    return pl.pallas_call(
        kern,
        out_shape=[jax.ShapeDtypeStruct((t, D_SP), F32), jax.ShapeDtypeStruct((t, D_CONV), F32)],
        grid=(t // tb,),
        in_specs=_halo_specs(D_INP, 0, tb, t) + [pl.BlockSpec((1, D_SP), lambda i: (0, 0)),
                                                 pl.BlockSpec((SUBLANES, D_CONV), lambda i: (0, 0))],
        out_specs=[pl.BlockSpec((tb, D_SP), lambda i: (i, 0)), pl.BlockSpec((tb, D_CONV), lambda i: (i, 0))],
        compiler_params=_params(("parallel",)), name=name)(p, p, p, mu, conv_w)


def _shift_conv_bwd(p, d_pss, d_o, mu, conv_w, seq, *, name, tb=ROW_TILE):
    t = p.shape[0]
    per_seq = seq // tb

    def kern(p_ref, pp_ref, pn_ref, d_ref, dp_ref, dn_ref, do_ref, dop_ref, don_ref, mu_ref, cw_ref,
             out_ref, dmu_ref, dcw_ref):
        i = pl.program_id(0)
        first = (i % per_seq) == 0
        last = (i % per_seq) == per_seq - 1

        @pl.when(i == 0)
        def _():
            dmu_ref[...] = jnp.zeros_like(dmu_ref)
            dcw_ref[...] = jnp.zeros_like(dcw_ref)

        mu_v = mu_ref[...]
        ps = p_ref[:, :D_SP]
        down, up = _neighbours(ps, pp_ref[:, :D_SP], pn_ref[:, :D_SP], first, last)
        d = d_ref[...]
        ddown, dup = _neighbours(d, dp_ref[...], dn_ref[...], first, last)
        out_ref[:, :D_SP] = (d - mu_v * d + 0.5 * (mu_v * ddown + mu_v * dup)).astype(out_ref.dtype)
        dmu_ref[...] += _colsum(d * (0.5 * (down + up) - ps))

        def parts(ref):
            return (ref[:, D_SP:D_SP + D_CONV], ref[:, D_SP + D_CONV:D_SP + 2 * D_CONV],
                    ref[:, D_SP + 2 * D_CONV:])

        gb, gc, hh = parts(p_ref)
        gb_p, gc_p, hh_p = parts(pp_ref)
        gb_n, gc_n, hh_n = parts(pn_ref)
        u = gc * hh
        udown, uup = _neighbours(u, gc_p * hh_p, gc_n * hh_n, first, last)
        cw0, cw1, cw2 = cw_ref[0:1, :], cw_ref[1:2, :], cw_ref[2:3, :]
        do = do_ref[...]
        duc = do * gb
        ducdown, ducup = _neighbours(duc, dop_ref[...] * gb_p, don_ref[...] * gb_n, first, last)
        du = cw0 * ducup + cw1 * duc + cw2 * ducdown
        out_ref[:, D_SP:D_SP + D_CONV] = (do * (cw0 * udown + cw1 * u + cw2 * uup)).astype(out_ref.dtype)
        out_ref[:, D_SP + D_CONV:D_SP + 2 * D_CONV] = (du * hh).astype(out_ref.dtype)
        out_ref[:, D_SP + 2 * D_CONV:] = (du * gc).astype(out_ref.dtype)
        dcw_ref[0:1, :] += _colsum(duc * udown)
        dcw_ref[1:2, :] += _colsum(duc * u)
        dcw_ref[2:3, :] += _colsum(duc * uup)

    return pl.pallas_call(
        kern,
        out_shape=[jax.ShapeDtypeStruct((t, D_INP), BF16), jax.ShapeDtypeStruct((1, D_SP), F32),
                   jax.ShapeDtypeStruct((SUBLANES, D_CONV), F32)],
        grid=(t // tb,),
        in_specs=(_halo_specs(D_INP, 0, tb, t) + _halo_specs(D_SP, 0, tb, t) + _halo_specs(D_CONV, 1, tb, t)
                  + [pl.BlockSpec((1, D_SP), lambda i: (0, 0)),
                     pl.BlockSpec((SUBLANES, D_CONV), lambda i: (0, 0))]),
        out_specs=[pl.BlockSpec((tb, D_INP), lambda i: (i, 0)), pl.BlockSpec((1, D_SP), lambda i: (0, 0)),
                   pl.BlockSpec((SUBLANES, D_CONV), lambda i: (0, 0))],
        compiler_params=_params(("arbitrary",)), name=name)(p, p, p, d_pss, d_pss, d_pss, d_o, d_o, d_o, mu, conv_w)


N_CHAIN = 16
N_GROUP = LANES // N_CHAIN
V_HI = HEAD // SUBLANES
G_KK, G_R, G_W, G_B, G_KD = 0, 1, (2, 3), (4, 5), (6, 7)


K_HI = HEAD // SUBLANES


def _tree_sum(terms):
    terms = list(terms)
    while len(terms) > 1:
        terms = [a + b for a, b in zip(terms[::2], terms[1::2])]
    return terms[0]


def _kscan_specs(nc):
    same = lambda c: c
    mirror = lambda c: nc - 1 - c

    def k_spec(fn):
        return pl.BlockSpec((SCAN_CHUNK, HEAD, LANES), lambda c: (fn(c), 0, 0))

    def v_spec(fn):
        return pl.BlockSpec((SCAN_CHUNK, SUBLANES, LANES), lambda c: (fn(c), 0, 0))

    return same, mirror, k_spec, v_spec


ST_SHAPE = (2, K_HI, V_HI, SUBLANES, LANES)


def _lane_group_index():
    lane = lax.broadcasted_iota(jnp.int32, (SUBLANES, LANES), 1)
    return lax.shift_right_logical(lane, jnp.full_like(lane, 4))


def _spread_groups(x, grp):
    rolled = [x] + [pltpu.roll(x, s * N_CHAIN, 1) for s in range(1, N_GROUP)]
    out = []
    for j in range(N_GROUP):
        t = rolled[(0 - j) % N_GROUP]
        for g in range(1, N_GROUP):
            t = jnp.where(grp == g, rolled[(g - j) % N_GROUP], t)
        out.append(t)
    return out


def _gather_groups(tiles, grp):
    total = None
    for s in range(N_GROUP):
        b = tiles[s % N_GROUP]
        for g in range(1, N_GROUP):
            b = jnp.where(grp == g, tiles[(g + s) % N_GROUP], b)
        b = pltpu.roll(b, s * N_CHAIN, 1) if s else b
        total = b if total is None else total + b
    return total


def _lane_group_sum(x):
    return _tree_sum([x] + [pltpu.roll(x, k * N_CHAIN, 1) for k in range(1, N_GROUP)])


def _key_row(x_t, grp, kh):
    r = SUBLANES * grp + kh
    return jnp.broadcast_to(x_t[r:r + 1, :], (SUBLANES, LANES))


def _acc(total, term):
    return term if total is None else total + term


SA_SHAPE = (2, V_HI, SUBLANES, LANES)


def _scan_fwd(xall, v_c, *, gather=(), name):
    steps = xall.shape[0]
    nc = steps // SCAN_CHUNK
    same, mirror, k_spec, v_spec = _kscan_specs(nc)
    last = SCAN_CHUNK - 1
    n_x = len(gather)

    def kern(*refs):
        xf_ref, xb_ref, vf_ref, vb_ref = refs[:4]
        yf_ref, yb_ref, hist_ref, fin_ref, sa_ref = refs[4 + n_x:9 + n_x]
        st_ref = refs[9 + 2 * n_x]
        c = pl.program_id(0)

        def riders():
            return _exchange_copies(refs[4:4 + n_x], refs[9 + n_x:9 + 2 * n_x], 0, *refs[10 + 2 * n_x:])

        @pl.when(c == 0)
        def _():
            st_ref[...] = jnp.zeros_like(st_ref)
            if n_x:
                for cp in riders():
                    cp.start()

        hist_ref[0] = st_ref[...]
        grp = _lane_group_index()

        def body(i, put):
            j = last - i
            for d, (x_t, v_t, y_ref, at) in enumerate(((xf_ref[i], vf_ref[i], yf_ref, i),
                                                       (xb_ref[j], vb_ref[j], yb_ref, j))):
                v_b = _spread_groups(v_t, grp)
                part = [None] * V_HI
                for kh in range(K_HI):
                    kk_r = _key_row(x_t, G_KK, kh)
                    for vh in range(V_HI):
                        part[vh] = _acc(part[vh], hist_ref[i, d, kh, vh] * kk_r)
                sa = [_lane_group_sum(p) for p in part]
                for vh in range(V_HI):
                    sa_ref[i, d, vh] = sa[vh]
                y_p = [None] * V_HI
                for kh in range(K_HI):
                    r_r, w_r = _key_row(x_t, G_R, kh), _key_row(x_t, G_W[d], kh)
                    b_r, kd_r = _key_row(x_t, G_B[d], kh), _key_row(x_t, G_KD[d], kh)
                    for vh in range(V_HI):
                        new = hist_ref[i, d, kh, vh] * w_r - sa[vh] * b_r + v_b[vh] * kd_r
                        put(d, kh, vh, new)
                        y_p[vh] = _acc(y_p[vh], new * r_r)
                y_ref[at] = _gather_groups(y_p, grp)

        def step(i, carry):
            def put(d, kh, vh, val):
                hist_ref[i + 1, d, kh, vh] = val
            body(i, put)
            return carry

        lax.fori_loop(0, last, step, 0, unroll=SCAN_UNROLL)

        def put_carry(d, kh, vh, val):
            st_ref[d, kh, vh] = val

        body(last, put_carry)

        @pl.when(c == nc - 1)
        def _():
            fin_ref[...] = st_ref[...]
            if n_x:
                for cp in riders():
                    cp.wait()

    return pl.pallas_call(
        kern,
        out_shape=[jax.ShapeDtypeStruct((steps, SUBLANES, LANES), F32)] * 2
        + [jax.ShapeDtypeStruct((steps,) + ST_SHAPE, F32), jax.ShapeDtypeStruct(ST_SHAPE, F32),
           jax.ShapeDtypeStruct((steps,) + SA_SHAPE, F32)]
        + _exchange_out_shapes(gather, 0),
        grid=(nc,), in_specs=[k_spec(same), k_spec(mirror), v_spec(same), v_spec(mirror)] + _hbm_specs(n_x),
        out_specs=[v_spec(same), v_spec(mirror),
                   pl.BlockSpec((SCAN_CHUNK,) + ST_SHAPE, lambda c: (c, 0, 0, 0, 0, 0)),
                   pl.BlockSpec(ST_SHAPE, lambda c: (0, 0, 0, 0, 0)),
                   pl.BlockSpec((SCAN_CHUNK,) + SA_SHAPE, lambda c: (c, 0, 0, 0, 0))] + _hbm_specs(n_x),
        scratch_shapes=[pltpu.VMEM(ST_SHAPE, F32)] + (_exchange_sems(n_x) if n_x else []),
        compiler_params=_params(("arbitrary",), SCAN_VMEM_LIMIT), name=name)(xall, xall, v_c, v_c, *gather)


def _scan_bwd(xall, v_c, dy_c, hist, fin, sa, *, exchange=(), name):
    steps = xall.shape[0]
    nc = steps // SCAN_CHUNK
    same, back, k_spec, v_spec = _kscan_specs(nc)
    last = SCAN_CHUNK - 1
    n_x = len(exchange)

    def kern(*refs):
        xf_ref, xb_ref, vf_ref, vb_ref, dyf_ref, dyb_ref, hist_ref, fin_ref, sa_ref = refs[:9]
        gf_ref, gb_ref, dvf_ref, dvb_ref = refs[9 + n_x:13 + n_x]
        ds_ref, after_ref = refs[13 + 2 * n_x:15 + 2 * n_x]
        c = pl.program_id(0)

        def riders():
            return _exchange_copies(refs[9:9 + n_x], refs[13 + n_x:13 + 2 * n_x], n_x, *refs[15 + 2 * n_x:])

        @pl.when(c == 0)
        def _():
            ds_ref[...] = jnp.zeros_like(ds_ref)
            after_ref[...] = fin_ref[...]
            if n_x:
                for cp in riders():
                    cp.start()

        grp = _lane_group_index()
        row = lax.broadcasted_iota(jnp.int32, (SUBLANES, LANES), 0)
        zero = jnp.zeros((SUBLANES, LANES), F32)

        def body(i, after):
            j = last - i
            for d, (x_t, v_t, dy_t, g_ref, dv_ref, at) in enumerate((
                    (xf_ref[i], vf_ref[i], dyf_ref[i], gf_ref, dvf_ref, i),
                    (xb_ref[j], vb_ref[j], dyb_ref[j], gb_ref, dvb_ref, j))):
                v_s, dy_s = _spread_groups(v_t, grp), _spread_groups(dy_t, grp)
                dsa_p, dv_p = [None] * V_HI, [None] * V_HI
                for kh in range(K_HI):
                    r_r = _key_row(x_t, G_R, kh)
                    b_r, kd_r = _key_row(x_t, G_B[d], kh), _key_row(x_t, G_KD[d], kh)
                    for vh in range(V_HI):
                        g = ds_ref[d, kh, vh] + dy_s[vh] * r_r
                        ds_ref[d, kh, vh] = g
                        dsa_p[vh] = _acc(dsa_p[vh], g * b_r)
                        dv_p[vh] = _acc(dv_p[vh], g * kd_r)
                dsa = [-_lane_group_sum(p) for p in dsa_p]
                sa = [sa_ref[i, d, vh] for vh in range(V_HI)]
                dv_ref[at] = _gather_groups(dv_p, grp)
                blocks = {G_KK: zero, G_R: zero, G_W[d]: zero, G_B[d]: zero, G_KD[d]: zero}
                for kh in range(K_HI):
                    w_r, kk_r = _key_row(x_t, G_W[d], kh), _key_row(x_t, G_KK, kh)
                    dkk = dr = dw = db = dkd = None
                    for vh in range(V_HI):
                        g, before = ds_ref[d, kh, vh], hist_ref[i, d, kh, vh]
                        dr = _acc(dr, after(d, kh, vh) * dy_s[vh])
                        dw = _acc(dw, g * before)
                        dkd = _acc(dkd, g * v_s[vh])
                        db = _acc(db, g * sa[vh])
                        dkk = _acc(dkk, before * dsa[vh])
                        ds_ref[d, kh, vh] = g * w_r + dsa[vh] * kk_r
                    for gi, a in ((G_KK, dkk), (G_R, dr), (G_W[d], dw), (G_B[d], -db), (G_KD[d], dkd)):
                        blocks[gi] = jnp.where(row == kh, _colsum(a), blocks[gi])
                for gi in range(N_GROUP):
                    g_ref[at, SUBLANES * gi:SUBLANES * (gi + 1), :] = blocks.get(gi, zero)

        body(last, lambda d, kh, vh: after_ref[d, kh, vh])

        def step(ii, carry):
            i = last - ii
            body(i, lambda d, kh, vh: hist_ref[i + 1, d, kh, vh])
            return carry

        lax.fori_loop(1, SCAN_CHUNK, step, 0, unroll=SCAN_UNROLL)
        after_ref[...] = hist_ref[0]

        if n_x:
            @pl.when(c == nc - 1)
            def _():
                for cp in riders():
                    cp.wait()

    return pl.pallas_call(
        kern,
        out_shape=[jax.ShapeDtypeStruct((steps, HEAD, LANES), F32)] * 2
        + [jax.ShapeDtypeStruct((steps, SUBLANES, LANES), F32)] * 2 + _exchange_out_shapes(exchange, n_x),
        grid=(nc,),
        in_specs=[k_spec(back), k_spec(same), v_spec(back), v_spec(same), v_spec(back), v_spec(same),
                  pl.BlockSpec((SCAN_CHUNK,) + ST_SHAPE, lambda c: (back(c), 0, 0, 0, 0, 0)),
                  pl.BlockSpec(ST_SHAPE, lambda c: (0, 0, 0, 0, 0)),
                  pl.BlockSpec((SCAN_CHUNK,) + SA_SHAPE, lambda c: (back(c), 0, 0, 0, 0))] + _hbm_specs(n_x),
        out_specs=[k_spec(back), k_spec(same), v_spec(back), v_spec(same)] + _hbm_specs(n_x),
        scratch_shapes=[pltpu.VMEM(ST_SHAPE, F32), pltpu.VMEM(ST_SHAPE, F32)]
        + (_exchange_sems(n_x) if n_x else []),
        compiler_params=_params(("arbitrary",), SCAN_VMEM_LIMIT), name=name)(xall, xall, v_c, v_c, dy_c, dy_c, hist, fin, sa,
                                                            *exchange)


def _bf16_pieces(x):
    hi = x.astype(BF16)
    return hi, (x - hi.astype(F32)).astype(BF16)


def _to_key_rows(wide, bsz, seq, *, name):
    assert bsz == 2
    perm = _key_row_maps()
    tt = min(RELAYOUT_TILE, seq)
    per_seq = seq // tt

    def kern(x0_ref, x1_ref, p0_ref, p1_ref, o_ref):
        total = None
        for x_ref, p_ref in ((x0_ref, p0_ref), (x1_ref, p1_ref)):
            for piece in _bf16_pieces(x_ref[...]):
                term = jnp.dot(piece, p_ref[...], preferred_element_type=F32)
                total = term if total is None else total + term
        for r in range(K_HI):
            o_ref[:, r, :] = total[:, LANES * r:LANES * (r + 1)]

    p_spec = pl.BlockSpec((D_RWKV, K_HI * LANES), lambda i, a: (0, 0))
    return pl.pallas_call(
        kern, out_shape=jax.ShapeDtypeStruct((seq, HEAD, LANES), F32), grid=(per_seq, N_GROUP),
        in_specs=[pl.BlockSpec((tt, D_RWKV), lambda i, a: (i, a)),
                  pl.BlockSpec((tt, D_RWKV), lambda i, a: (per_seq + i, a)), p_spec, p_spec],
        out_specs=pl.BlockSpec((tt, K_HI, LANES), lambda i, a: (i, a, 0)),
        compiler_params=_params(("parallel", "parallel")), name=name)(wide, wide, *perm)


def _key_row_maps():
    src = jnp.arange(D_RWKV)
    head, kh, kl = src // HEAD, (src // SUBLANES) % K_HI, src % SUBLANES
    dst = jnp.arange(K_HI * LANES)
    return [((kh[:, None] == dst[None, :] // LANES) & (kl[:, None] == (dst[None, :] // N_CHAIN) % SUBLANES)
             & ((dst[None, :] // N_HEAD) % 2 == b) & (head[:, None] == dst[None, :] % N_HEAD)).astype(BF16)
            for b in range(2)]


def _from_key_rows(g_f, g_b, bsz, seq, *, name):
    assert bsz == 2
    maps = jnp.concatenate([m.T for m in _key_row_maps()], axis=1)
    tt = min(RELAYOUT_TILE // 2, seq)
    per_seq = seq // tt

    def kern(gf_ref, gb_ref, q_ref, o_ref):
        a = pl.program_id(1)
        shared = a <= G_R
        from_f = shared | (a % 2 == G_W[0] % 2)

        def rearranged(g_ref):
            g = jnp.concatenate([g_ref[:, r, :] for r in range(K_HI)], axis=1)
            hi, mid = (jnp.dot(piece, q_ref[...], preferred_element_type=F32) for piece in _bf16_pieces(g))
            both = hi + mid
            return both[:, :D_RWKV], both[:, D_RWKV:]

        @pl.when(from_f)
        def _():
            o_ref[0], o_ref[1] = rearranged(gf_ref)

        @pl.when(jnp.logical_not(from_f))
        def _():
            o_ref[0], o_ref[1] = rearranged(gb_ref)

        @pl.when(shared)
        def _():
            more = rearranged(gb_ref)
            o_ref[0] += more[0]
            o_ref[1] += more[1]

    g_spec = pl.BlockSpec((tt, K_HI, LANES), lambda i, a: (i, a, 0))
    out = pl.pallas_call(
        kern, out_shape=jax.ShapeDtypeStruct((bsz, seq, N_GROUP * D_RWKV), F32), grid=(per_seq, N_GROUP),
        in_specs=[g_spec, g_spec, pl.BlockSpec((K_HI * LANES, bsz * D_RWKV), lambda i, a: (0, 0))],
        out_specs=pl.BlockSpec((bsz, tt, D_RWKV), lambda i, a: (0, i, a)),
        compiler_params=_params(("parallel", "parallel")), name=name)(g_f, g_b, maps)
    return out.reshape(bsz * seq, N_GROUP * D_RWKV)


def _to_value_rows(a, bsz, seq):
    z = a.reshape(bsz, seq, N_HEAD, V_HI, SUBLANES).transpose(1, 4, 3, 0, 2)
    return z.reshape(seq, SUBLANES, LANES)


def _from_value_rows(y, bsz, seq):
    z = y.reshape(seq, SUBLANES, V_HI, bsz, N_HEAD).transpose(3, 0, 4, 2, 1)
    return z.reshape(bsz * seq, D_RWKV)


def _pad_cols(a, segs):
    out, off = [], 0
    for w, wp in segs:
        out.append(a[..., off:off + w])
        if wp > w:
            out.append(jnp.zeros(a.shape[:-1] + (wp - w,), a.dtype))
        off += w
    return jnp.concatenate(out, axis=-1)


def _unpad_cols(a, segs):
    out, off = [], 0
    for w, wp in segs:
        out.append(a[..., off:off + w])
        off += wp
    return jnp.concatenate(out, axis=-1)


P_SEGS = ((3 * D_RWKV, 3 * D_RWKV), (D_LORA, 128), (D_LORA, 128), (D_GATE, 256), (3 * D_CONV, 3 * D_CONV))
S_SEGS = P_SEGS[:4]


def _pad_rows(a, rows):
    return jnp.concatenate([a, jnp.zeros((rows - a.shape[0], a.shape[1]), a.dtype)], axis=0)


LATE = ("w_out", "w_gate", "w_up", "w_down")


def _local_step(x, target, w, late=None):
    bsz, seq, _ = x.shape
    t = bsz * seq
    x2d = x.reshape(t, D_MODEL)
    tg2d = target.reshape(t, D_MODEL)
    row = lambda a: a.reshape(1, -1).astype(F32)

    w_in = _pad_cols(w["w_in"][0], P_SEGS)
    mu = _pad_cols(row(w["mu_shift"]), S_SEGS)
    wupf, wupb, aupf, aupb = (_pad_rows(w[n][0].astype(F32), 128) for n in ("w_up_f", "w_up_b", "a_up_f", "a_up_b"))
    gup = _pad_rows(w["g_up"][0].astype(F32), 256)
    conv_w = _pad_rows(w["conv_w"][0].astype(F32), SUBLANES)
    norm1, norm2, normf = row(w["norm1_w"]), row(w["norm2_w"]), row(w["norm_f_w"])
    vec = {n: row(w[n]) for n in VEC}
    head_of = jnp.arange(LANES) // HEAD
    bd = (head_of[:, None] == head_of[None, :]).astype(F32)
    pre_consts = [vec["k_k"], vec["w0_f"], vec["w0_b"], vec["a0_f"], vec["a0_b"], vec["k_a_f"], vec["k_a_b"],
                  wupf, wupb, aupf, aupb, gup, bd]
    post_consts = [vec["gn_w"], vec["gn_b"], vec["r_k_f"], vec["r_k_b"], bd]

    h1, = _rowwise(_rms, [x2d], [norm1], [D_MODEL], [], name="rms1_fwd", out_dtype=BF16, tb=WIDE_TILE)
    p = _mm(h1, w_in, name="mm_in")
    pss, oconv = _shift_conv_fwd(p, mu, conv_w, seq, name="shift_conv_fwd")
    pre_rows = [(pss, 0, 512), (pss, 1, 512), (pss, XW0 // 128, 128), (pss, XA0 // 128, 128), (pss, XG0 // 256, 256)]
    sc, g = _rowwise(_prescan_math, pre_rows, pre_consts, [[D_RWKV] * N_GROUP, D_RWKV], [], name="prescan_fwd")
    xall = _to_key_rows(sc, bsz, seq, name="to_key_rows")
    v_l = _to_value_rows(pss[:, 2 * D_RWKV:3 * D_RWKV], bsz, seq)
    y_f, y_b, hist, fin, sa, *gathered = _scan_fwd(xall, v_l, gather=[late[n] for n in LATE] if late else (),
                                                   name="scan_fwd")
    w_out, w_gate, w_up, w_down = (
        (_from_slots(a, SHARD_AXIS[n]) if late else w[n])[0] for n, a in zip(LATE, gathered or LATE))
    y = _from_value_rows(y_f + y_b, bsz, seq)
    post_rows = [y, (pss, 0, 512), (pss, 2, 512), (sc, G_KD[0], 512), (sc, G_KD[1], 512), g]

    def post_fwd(y_, r_, v_, kdf_, kdb_, g_, oc_, *consts):
        return _postscan_math(y_, r_, v_, kdf_, kdb_, g_, *consts), oc_

    o, = _rowwise(post_fwd, post_rows + [oconv], post_consts, [[D_RWKV, D_CONV]], [], name="postscan_fwd",
                  out_dtype=BF16)
    x1 = _mm(o, w_out, add=x2d, name="mm_out")
    h2, = _rowwise(_rms, [x1], [norm2], [D_MODEL], [], name="rms2_fwd", out_dtype=BF16, tb=WIDE_TILE)
    gg, uu, ff = _mm_swiglu(h2, w_gate, w_up, name="mm_gate_up")
    x2 = _mm(ff, w_down, add=x1, name="mm_down")

    def final(x_, tg_, wn_):
        yo, vjp = jax.vjp(_rms, x_, wn_)
        err = yo - tg_
        dx_, dwn_ = vjp(err * (1.0 / D_MODEL))
        part = jnp.sum(jnp.sum(err * err, axis=1, keepdims=True), axis=0, keepdims=True) * (0.5 / D_MODEL)
        return dx_, part + jnp.zeros((1, LANES), F32), dwn_

    dx2, loss_acc, d_normf = _rowwise(final, [x2, tg2d], [normf], [D_MODEL], [(1, LANES), (1, D_MODEL)],
                                      name="loss_head", tb=WIDE_TILE)
    dgg, duu = _mm_swiglu_bwd(dx2, w_down, gg, uu, name="mm_down_dx")
    g_w_down = _mm(ff, dx2, ta=True, name="mm_down_dw")
    dh2 = _mm(dgg, w_gate, tb=True, name="mm_gate_dx")
    dh2 = _mm(duu, w_up, tb=True, add=dh2, name="mm_up_dx")
    g_w_gate = _mm(h2, dgg, ta=True, name="mm_gate_dw")
    g_w_up = _mm(h2, duu, ta=True, name="mm_up_dw")

    def rms_bwd(x_, dh_, dres_, wn_):
        _, vjp = jax.vjp(_rms, x_, wn_)
        dx_, dwn_ = vjp(dh_)
        return dx_ + dres_, dwn_

    dx1, d_norm2 = _rowwise(rms_bwd, [x1, dh2, dx2], [norm2], [D_MODEL], [(1, D_MODEL)], name="rms2_bwd", tb=WIDE_TILE)
    do = _mm(dx1, w_out, tb=True, name="mm_out_dx")
    g_w_out = _mm(o, dx1, ta=True, name="mm_out_dw")

    def post_bwd(y_, r_, v_, kdf_, kdb_, g_, do_, *consts):
        _, vjp = jax.vjp(lambda *a: _postscan_math(*a, consts[4]), y_, r_, v_, kdf_, kdb_, g_, *consts[:4])
        return vjp(do_)

    (dy, dr_c, dv_c, dkdf_c, dkdb_c, dg, d_gn_w, d_gn_b, d_rkf, d_rkb) = _rowwise(
        post_bwd, post_rows + [(do, 0, 512)], post_consts, [D_RWKV] * 6, [(1, D_RWKV)] * 4, name="postscan_bwd")
    dy_l = _to_value_rows(dy, bsz, seq)
    late_grads = {"w_out": g_w_out[None], "w_gate": g_w_gate[None], "w_up": g_w_up[None], "w_down": g_w_down[None]}
    g_f, g_b, dv_f, dv_b, *late_parts = _scan_bwd(
        xall, v_l, dy_l, hist, fin, sa, name="scan_bwd",
        exchange=[_to_slots(late_grads[n], SHARD_AXIS[n]).astype(BF16) for n in LATE] if late else ())
    dsc = _from_key_rows(g_f, g_b, bsz, seq, name="from_key_rows")
    dv_s = _from_value_rows(dv_f + dv_b, bsz, seq)

    def pre_bwd(r_, k_, xw_, xa_, xg_, dkk_, dr_s, dwf_, dwb_, dbf_, dbb_, dkdf_s, dkdb_s,
                dr_c_, dv_c_, dv_s_, dkdf_c_, dkdb_c_, dg_, *consts):
        _, vjp = jax.vjp(lambda *a: _prescan_math(*a, consts[-1]), r_, k_, xw_, xa_, xg_, *consts[:-1])
        grads = vjp((dkk_, dr_s + dr_c_, dwf_, dwb_, dbf_, dbb_, dkdf_s + dkdf_c_, dkdb_s + dkdb_c_, dg_))
        dr_, dk_, dxw_, dxa_, dxg_ = grads[:5]
        return (dr_, dk_, dv_c_ + dv_s_, dxw_, dxa_, dxg_) + tuple(grads[5:])

    pre_b_rows = (pre_rows + [(dsc, j, 512) for j in range(N_GROUP)]
                  + [dr_c, dv_c, dv_s, dkdf_c, dkdb_c, dg])
    pre_b = _rowwise(pre_bwd, pre_b_rows, pre_consts, [[512, 512, 512, 128, 128, 256]],
                     [(1, D_RWKV)] * 7 + [(128, D_RWKV)] * 4 + [(256, D_RWKV)], name="prescan_bwd")
    d_pss = pre_b[0]
    d_kk_, d_w0f, d_w0b, d_a0f, d_a0b, d_kaf, d_kab, d_wupf, d_wupb, d_aupf, d_aupb, d_gup = pre_b[1:]
    dp, d_mu, d_conv = _shift_conv_bwd(p, d_pss, do, mu, conv_w, seq, name="shift_conv_bwd")
    g_w_in = _mm(h1, dp, ta=True, name="mm_in_dw")
    grads = {
        "w_in": _unpad_cols(g_w_in, P_SEGS)[None], "mu_shift": _unpad_cols(d_mu, S_SEGS),
        "w_up_f": d_wupf[None, :D_LORA], "w0_f": d_w0f, "w_up_b": d_wupb[None, :D_LORA], "w0_b": d_w0b,
        "a_up_f": d_aupf[None, :D_LORA], "a0_f": d_a0f, "a_up_b": d_aupb[None, :D_LORA], "a0_b": d_a0b,
        "g_up": d_gup[None, :D_GATE], "k_k": d_kk_, "k_a_f": d_kaf, "k_a_b": d_kab,
        "r_k_f": d_rkf, "r_k_b": d_rkb, "gn_w": d_gn_w, "gn_b": d_gn_b, "conv_w": d_conv[None, :3],
        "w_out": g_w_out[None], "norm2_w": d_norm2, "w_gate": g_w_gate[None], "w_up": g_w_up[None],
        "w_down": g_w_down[None], "norm_f_w": d_normf,
    }
    early = ("w_in",) + LORA
    parts = dict(zip(LATE, late_parts))
    if late:
        vec_rows = jnp.concatenate([grads[n] for n in VEC] + [jnp.zeros((16 - len(VEC), D_RWKV), F32)], axis=0)
        slots = [_to_slots(grads[n], SHARD_AXIS[n]).astype(BF16 if n in BIG else F32) for n in early]
        dh1, *recv = _mm(dp, w_in, tb=True, exchange=(slots, [vec_rows]), name="mm_in_dx")
        parts.update(zip(early + ("vec",), recv))
    else:
        dh1 = _mm(dp, w_in, tb=True, name="mm_in_dx")
    dx, grads["norm1_w"] = _rowwise(rms_bwd, [x2d, dh1, dx1], [norm1], [D_MODEL], [(1, D_MODEL)], name="rms1_bwd",
                                    tb=WIDE_TILE)
    return loss_acc, dx.reshape(bsz, seq, D_MODEL), grads, parts


def _hbm_specs(n):
    return [pl.BlockSpec(memory_space=pl.ANY)] * n


def _all_gather(arrs, *, name):
    n = len(arrs)

    def body(*refs):
        x_refs, out_refs = refs[:n], refs[n:2 * n]
        send_sems, recv_sems, local_sems = refs[2 * n:]
        x, y, c = lax.axis_index("x"), lax.axis_index("y"), lax.axis_index("c")
        me, sibling = (x, y, c), (x, y, 1 - c)
        chips = [(1 - x, y), (x, 1 - y), (1 - x, 1 - y)]

        def slot(a, px, py, pc):
            return out_refs[a].at[4 * px + 2 * py + pc]

        def copy(a, k, block, to, src=None):
            return pltpu.make_async_remote_copy(
                src_ref=slot(a, *block) if src is None else src, dst_ref=slot(a, *block),
                send_sem=send_sems.at[k, a], recv_sem=recv_sems.at[k, a],
                device_id=to, device_id_type=pl.DeviceIdType.MESH)

        mine = [pltpu.make_async_copy(x_refs[a], slot(a, *me), local_sems.at[a]) for a in range(n)]
        for cp in mine:
            cp.start()
        first = []
        for a in range(n):
            first.append(copy(a, 0, me, sibling, src=x_refs[a]))
            first += [copy(a, 1 + j, me, (*chip, c), src=x_refs[a]) for j, chip in enumerate(chips)]
        for cp in first:
            cp.start()
        passed = []
        for j, chip in enumerate(chips):
            for a in range(n):
                copy(a, 1 + j, (*chip, c), me).wait_recv()
                cp = copy(a, 4 + j, (*chip, c), sibling)
                cp.start()
                passed.append(cp)
        for a in range(n):
            copy(a, 0, sibling, me).wait_recv()
            for j, chip in enumerate(chips):
                copy(a, 4 + j, (*chip, 1 - c), me).wait_recv()
        for cp in first + passed:
            cp.wait_send()
        for cp in mine:
            cp.wait()

    return pl.pallas_call(
        body, out_shape=[jax.ShapeDtypeStruct((N_DEV,) + a.shape, a.dtype) for a in arrs],
        in_specs=_hbm_specs(n), out_specs=_hbm_specs(n),
        scratch_shapes=[pltpu.SemaphoreType.DMA((7, n)), pltpu.SemaphoreType.DMA((7, n)),
                        pltpu.SemaphoreType.DMA((n,))],
        name=name)(*arrs)


def _exchange(sliced, whole, *, name):
    arrs = list(sliced) + list(whole)
    n, n_sliced = len(arrs), len(sliced)

    def body(*refs):
        copies = _exchange_copies(refs[:n], refs[n:2 * n], n_sliced, *refs[2 * n:])
        for cp in copies:
            cp.start()
        for cp in copies:
            cp.wait()

    return pl.pallas_call(
        body, out_shape=_exchange_out_shapes(arrs, n_sliced), in_specs=_hbm_specs(n), out_specs=_hbm_specs(n),
        scratch_shapes=_exchange_sems(n), name=name)(*arrs)


def _exchange_out_shapes(arrs, n_sliced):
    return [jax.ShapeDtypeStruct(a.shape if i < n_sliced else (N_DEV,) + a.shape, a.dtype)
            for i, a in enumerate(arrs)]


def _exchange_sems(n):
    return [pltpu.SemaphoreType.DMA((7, n)), pltpu.SemaphoreType.DMA((7, n)), pltpu.SemaphoreType.DMA((n,))]


def _exchange_copies(in_refs, out_refs, n_sliced, send_sems, recv_sems, local_sems):
    n = len(in_refs)
    x, y, c = lax.axis_index("x"), lax.axis_index("y"), lax.axis_index("c")
    me = 4 * x + 2 * y + c

    def src(a, dev):
        return in_refs[a].at[dev] if a < n_sliced else in_refs[a]

    copies = [pltpu.make_async_copy(src(a, me), out_refs[a].at[me], local_sems.at[a]) for a in range(n)]
    for k in range(1, N_DEV):
        px = 1 - x if k & 4 else x
        py = 1 - y if k & 2 else y
        pc = 1 - c if k & 1 else c
        for a in range(n):
            copies.append(pltpu.make_async_remote_copy(
                src_ref=src(a, 4 * px + 2 * py + pc), dst_ref=out_refs[a].at[me],
                send_sem=send_sems.at[k - 1, a], recv_sem=recv_sems.at[k - 1, a],
                device_id=(px, py, pc), device_id_type=pl.DeviceIdType.MESH))
    return copies


def _adam_math(g, w, m, v):
    nm = ADAM_B1 * m + (1.0 - ADAM_B1) * g
    nv = ADAM_B2 * v + (1.0 - ADAM_B2) * (g * g)
    m_hat = nm / (1.0 - ADAM_B1 ** ADAM_STEP)
    v_hat = nv / (1.0 - ADAM_B2 ** ADAM_STEP)
    return -ADAM_LR * (m_hat / (jnp.sqrt(v_hat) + ADAM_EPS) + ADAM_WD * w), nm, nv


def _slot_sum(ref):
    g = ref[0].astype(F32)
    for s in range(1, N_DEV):
        g = g + ref[s].astype(F32)
    return g


def _adamw_big(parts, w, m, v, *, name):
    _, rws, cols = w.shape
    tr = _tile(rws, (256, 176, 128))

    def kern(p_ref, w_ref, m_ref, v_ref, g_ref, d_ref, nm_ref, nv_ref):
        g = _slot_sum(p_ref)
        g_ref[...] = g
        d_ref[...], nm_ref[...], nv_ref[...] = _adam_math(g, w_ref[...], m_ref[...], v_ref[...])

    spec = pl.BlockSpec((1, tr, cols), lambda i: (0, i, 0))
    return pl.pallas_call(
        kern, out_shape=[jax.ShapeDtypeStruct(w.shape, F32)] * 4, grid=(rws // tr,),
        in_specs=[pl.BlockSpec((N_DEV, 1, tr, cols), lambda i: (0, 0, i, 0)), spec, spec, spec],
        out_specs=[spec] * 4, compiler_params=_params(("parallel",)), name=name)(parts, w, m, v)


def _adamw_small(lora_parts, vec_parts, wide_parts, wmv, *, name):
    names = LORA + VEC + WIDE
    n_l, n = len(LORA), len(names)
    flat = [a for trip in wmv for a in trip]

    def kern(*refs):
        l_refs, vec_ref, wide_ref = refs[:n_l], refs[n_l], refs[n_l + 1]
        in_refs = refs[n_l + 2:n_l + 2 + 3 * n]
        out_refs = refs[n_l + 2 + 3 * n:]
        vec_sum, wide_sum = _slot_sum(vec_ref), _slot_sum(wide_ref)
        for i, nm in enumerate(names):
            w_ref, m_ref, v_ref = in_refs[3 * i:3 * i + 3]
            if i < n_l:
                g = _slot_sum(l_refs[i])
            elif nm in VEC:
                g = vec_sum[i - n_l:i - n_l + 1, :]
            else:
                g = wide_sum[WIDE.index(nm):WIDE.index(nm) + 1, :w_ref.shape[-1]]
            o = out_refs[4 * i:4 * i + 4]
            o[0][...] = g
            o[1][...], o[2][...], o[3][...] = _adam_math(g, w_ref[...], m_ref[...], v_ref[...])

    out_shape = [jax.ShapeDtypeStruct(trip[0].shape, F32) for trip in wmv for _ in range(4)]
    outs = pl.pallas_call(kern, out_shape=out_shape, name=name,
                          compiler_params=pltpu.CompilerParams(vmem_limit_bytes=VMEM_LIMIT))(
        *lora_parts, vec_parts, wide_parts, *flat)
    return [tuple(outs[4 * i:4 * i + 4]) for i in range(n)]


def _to_slots(g, axis):
    _, rws, cols = g.shape
    if axis == 1:
        return g.reshape(N_DEV, 1, rws // N_DEV, cols)
    return g.reshape(1, rws, N_DEV, cols // N_DEV).transpose(2, 0, 1, 3)


def _from_slots(got, axis):
    _, _, rws, cols = got.shape
    if axis == 1:
        return got.reshape(1, N_DEV * rws, cols)
    return got.transpose(1, 2, 0, 3).reshape(1, rws, N_DEV * cols)


def _pad_lanes(a, width):
    return jnp.concatenate([a, jnp.zeros(a.shape[:-1] + (width - a.shape[-1],), a.dtype)], axis=-1)


def kernel(x, norm1_w, w_in, mu_shift, w_up_f, w0_f, w_up_b, w0_b, a_up_f, a0_f, a_up_b, a0_b, g_up, k_k, k_a_f, k_a_b, r_k_f, r_k_b, gn_w, gn_b, conv_w, w_out, norm2_w, w_gate, w_up, w_down, norm_f_w, loss_target, m_norm1_w, m_w_in, m_mu_shift, m_w_up_f, m_w0_f, m_w_up_b, m_w0_b, m_a_up_f, m_a0_f, m_a_up_b, m_a0_b, m_g_up, m_k_k, m_k_a_f, m_k_a_b, m_r_k_f, m_r_k_b, m_gn_w, m_gn_b, m_conv_w, m_w_out, m_norm2_w, m_w_gate, m_w_up, m_w_down, m_norm_f_w, v_norm1_w, v_w_in, v_mu_shift, v_w_up_f, v_w0_f, v_w_up_b, v_w0_b, v_a_up_f, v_a0_f, v_a_up_b, v_a0_b, v_g_up, v_k_k, v_k_a_f, v_k_a_b, v_r_k_f, v_r_k_b, v_gn_w, v_gn_b, v_conv_w, v_w_out, v_norm2_w, v_w_gate, v_w_up, v_w_down, v_norm_f_w):
    local = dict(norm1_w=norm1_w, w_in=w_in, mu_shift=mu_shift, w_up_f=w_up_f, w0_f=w0_f, w_up_b=w_up_b,
                 w0_b=w0_b, a_up_f=a_up_f, a0_f=a0_f, a_up_b=a_up_b, a0_b=a0_b, g_up=g_up, k_k=k_k, k_a_f=k_a_f,
                 k_a_b=k_a_b, r_k_f=r_k_f, r_k_b=r_k_b, gn_w=gn_w, gn_b=gn_b, conv_w=conv_w, w_out=w_out,
                 norm2_w=norm2_w, w_gate=w_gate, w_up=w_up, w_down=w_down, norm_f_w=norm_f_w)
    mom_m = dict(norm1_w=m_norm1_w, w_in=m_w_in, mu_shift=m_mu_shift, w_up_f=m_w_up_f, w0_f=m_w0_f,
                 w_up_b=m_w_up_b, w0_b=m_w0_b, a_up_f=m_a_up_f, a0_f=m_a0_f, a_up_b=m_a_up_b, a0_b=m_a0_b,
                 g_up=m_g_up, k_k=m_k_k, k_a_f=m_k_a_f, k_a_b=m_k_a_b, r_k_f=m_r_k_f, r_k_b=m_r_k_b,
                 gn_w=m_gn_w, gn_b=m_gn_b, conv_w=m_conv_w, w_out=m_w_out, norm2_w=m_norm2_w, w_gate=m_w_gate,
                 w_up=m_w_up, w_down=m_w_down, norm_f_w=m_norm_f_w)
    mom_v = dict(norm1_w=v_norm1_w, w_in=v_w_in, mu_shift=v_mu_shift, w_up_f=v_w_up_f, w0_f=v_w0_f,
                 w_up_b=v_w_up_b, w0_b=v_w0_b, a_up_f=v_a_up_f, a0_f=v_a0_f, a_up_b=v_a_up_b, a0_b=v_a0_b,
                 g_up=v_g_up, k_k=v_k_k, k_a_f=v_k_a_f, k_a_b=v_k_a_b, r_k_f=v_r_k_f, r_k_b=v_r_k_b,
                 gn_w=v_gn_w, gn_b=v_gn_b, conv_w=v_conv_w, w_out=v_w_out, norm2_w=v_norm2_w, w_gate=v_w_gate,
                 w_up=v_w_up, w_down=v_w_down, norm_f_w=v_norm_f_w)

    early = ("w_in",) + LORA
    got = _all_gather([local["w_in"].astype(BF16)] + [local[n] for n in LORA], name="gather")
    full = dict(local)
    full.update({n: _from_slots(a, SHARD_AXIS[n]) for n, a in zip(early, got)})

    loss_part, grad_x, grads, parts = _local_step(x, loss_target, full,
                                                  late={n: local[n].astype(BF16) for n in LATE})

    wide_rows = jnp.concatenate([_pad_lanes(a, WIDE_ROW) for a in [grads[n] for n in WIDE] + [loss_part]]
                                + [jnp.zeros((SUBLANES - len(WIDE) - 1, WIDE_ROW), F32)], axis=0)
    wide_parts, = _exchange([], [wide_rows], name="grad_exchange")
    loss = jnp.sum(wide_parts[:, len(WIDE), 0])
    out = {}
    for n in BIG:
        out[n] = _adamw_big(parts[n], local[n], mom_m[n], mom_v[n], name="adamw_" + n)

    def small_form(n, a):
        if n in LORA:
            return a
        a = a.reshape(1, -1)
        return _pad_lanes(a, WIDE_ROW) if n == "mu_shift" else a

    small = LORA + VEC + WIDE
    res = _adamw_small([parts[n] for n in LORA], parts["vec"], wide_parts,
                       [tuple(small_form(n, d[n]) for d in (local, mom_m, mom_v)) for n in small],
                       name="adamw_small")
    for n, quad in zip(small, res):
        out[n] = tuple(a[..., :local[n].size].reshape(local[n].shape) if n not in LORA else a for a in quad)
    return (loss, grad_x, *[out[n][i] for i in range(4) for n in WEIGHTS])
```

```python
import functools

import jax
import jax.numpy as jnp
from jax import lax
from jax.experimental import pallas as pl
from jax.experimental.pallas import tpu as pltpu

F32 = jnp.float32
BF16 = jnp.bfloat16
HIGHEST = lax.Precision.HIGHEST

N_DEV = 8
D_MODEL = 1024
D_RWKV = 512
D_CONV = 512
HEAD = 64
N_HEAD = D_RWKV // HEAD
D_LORA = 64
D_GATE = 160
D_SHIFTED = 3 * D_RWKV + 2 * D_LORA + D_GATE
XW0, XA0, XG0 = 1536, 1664, 1792
D_SP = 2048
D_INP = D_SP + 3 * D_CONV
LOG_DECAY_SCALE = 0.606531
RMS_EPS = 1e-6
GN_EPS = 64e-5
NORM_EPS = 1e-12
ADAM_LR, ADAM_B1, ADAM_B2, ADAM_EPS, ADAM_WD, ADAM_STEP = 0.001, 0.9, 0.999, 1e-08, 0.01, 10

LANES = 128
SUBLANES = 8
VMEM_LIMIT = 48 * 1024 * 1024
SCAN_CHUNK = 32
SCAN_VMEM_LIMIT = 58 * 1024 * 1024
SCAN_UNROLL = 3
ROW_TILE = 128
WIDE_TILE = 512
RELAYOUT_TILE = 512

BIG = ("w_in", "w_out", "w_gate", "w_up", "w_down")
LORA = ("w_up_f", "w_up_b", "a_up_f", "a_up_b", "g_up", "conv_w")
SHARD_AXIS = {"w_in": 2, "w_out": 1, "w_gate": 2, "w_up": 2, "w_down": 1, "w_up_f": 2, "w_up_b": 2,
              "a_up_f": 2, "a_up_b": 2, "g_up": 2, "conv_w": 2}
VEC = ("w0_f", "w0_b", "a0_f", "a0_b", "k_k", "k_a_f", "k_a_b", "r_k_f", "r_k_b", "gn_w", "gn_b")
WIDE = ("mu_shift", "norm1_w", "norm2_w", "norm_f_w")
WIDE_ROW = 2048
WEIGHTS = ("norm1_w", "w_in", "mu_shift", "w_up_f", "w0_f", "w_up_b", "w0_b", "a_up_f", "a0_f", "a_up_b",
           "a0_b", "g_up", "k_k", "k_a_f", "k_a_b", "r_k_f", "r_k_b", "gn_w", "gn_b", "conv_w", "w_out",
           "norm2_w", "w_gate", "w_up", "w_down", "norm_f_w")


def _params(sem, limit=VMEM_LIMIT):
    return pltpu.CompilerParams(dimension_semantics=sem, vmem_limit_bytes=limit)


def _tile(n, cands):
    for c in cands:
        if n % c == 0:
            return c
    raise ValueError(f"no tile for {n}")


def _mm(a, b, *, ta=False, tb=False, add=None, exchange=None, name):
    (k_dim, m) = a.shape if ta else a.shape[::-1]
    (k2, n) = b.shape[::-1] if tb else b.shape
    assert k_dim == k2, (a.shape, b.shape, ta, tb)
    tm = _tile(m, (1408, 1024, 512, 256, 128))
    tn = _tile(n, (1408, 1024, 896, 512, 256, 128))
    tk = _tile(k_dim, (1408, 1024, 896, 512, 256, 128))
    nk = k_dim // tk
    grid = (m // tm, n // tn, nk)
    dims = (((0 if ta else 1,), (1 if tb else 0,)), ((), ()))
    sliced, whole = exchange or ((), ())
    riders = list(sliced) + list(whole)
    n_x, n_in = len(riders), 2 + (add is not None)

    def kern(*refs):
        a_ref, b_ref = refs[:2]
        add_ref = refs[2] if add is not None else None
        o_ref, acc_ref = refs[n_in + n_x], refs[n_in + 2 * n_x + 1]
        k = pl.program_id(2)
        step = (pl.program_id(0) * grid[1] + pl.program_id(1)) * nk + k

        def copies():
            return _exchange_copies(refs[n_in:n_in + n_x], refs[n_in + n_x + 1:n_in + 2 * n_x + 1], len(sliced),
                                    *refs[n_in + 2 * n_x + 2:])

        if n_x:
            @pl.when(step == 0)
            def _():
                for cp in copies():
                    cp.start()

        @pl.when(k == 0)
        def _():
            acc_ref[...] = jnp.zeros_like(acc_ref)

        acc_ref[...] += lax.dot_general(a_ref[...].astype(BF16), b_ref[...].astype(BF16), dims,
                                        preferred_element_type=F32)

        @pl.when(k == nk - 1)
        def _():
            if add is None:
                o_ref[...] = acc_ref[...]
            else:
                o_ref[...] = acc_ref[...] + add_ref[...]

        if n_x:
            @pl.when(step == grid[0] * grid[1] * nk - 1)
            def _():
                for cp in copies():
                    cp.wait()

    a_spec = (pl.BlockSpec((tk, tm), lambda i, j, k: (k, i)) if ta
              else pl.BlockSpec((tm, tk), lambda i, j, k: (i, k)))
    b_spec = (pl.BlockSpec((tn, tk), lambda i, j, k: (j, k)) if tb
              else pl.BlockSpec((tk, tn), lambda i, j, k: (k, j)))
    o_spec = pl.BlockSpec((tm, tn), lambda i, j, k: (i, j))
    in_specs = [a_spec, b_spec] + ([o_spec] if add is not None else []) + _hbm_specs(n_x)
    args = (a, b) + ((add,) if add is not None else ()) + tuple(riders)
    out = pl.pallas_call(
        kern, out_shape=[jax.ShapeDtypeStruct((m, n), F32)] + _exchange_out_shapes(riders, len(sliced)), grid=grid,
        in_specs=in_specs, out_specs=[o_spec] + _hbm_specs(n_x),
        scratch_shapes=[pltpu.VMEM((tm, tn), F32)] + (_exchange_sems(n_x) if n_x else []),
        compiler_params=_params(("arbitrary",) * 3 if n_x else ("parallel", "parallel", "arbitrary")),
        name=name)(*args)
    return out if n_x else out[0]


def _swiglu(g, u):
    return jax.nn.silu(g) * u


FFN_TN = 256


def _mm_swiglu(h, w_gate, w_up, *, name):
    m, k_dim = h.shape
    n = w_gate.shape[1]
    tm = _tile(m, (1024, 512, 256, 128))

    def kern(h_ref, wg_ref, wu_ref, g_ref, u_ref, f_ref):
        hv = h_ref[...].astype(BF16)
        g = jnp.dot(hv, wg_ref[...].astype(BF16), preferred_element_type=F32)
        u = jnp.dot(hv, wu_ref[...].astype(BF16), preferred_element_type=F32)
        g_ref[...] = g
        u_ref[...] = u
        f_ref[...] = _swiglu(g, u).astype(f_ref.dtype)

    w_spec = pl.BlockSpec((k_dim, FFN_TN), lambda i, j: (0, j))
    o_spec = pl.BlockSpec((tm, FFN_TN), lambda i, j: (i, j))
    return pl.pallas_call(
        kern, out_shape=[jax.ShapeDtypeStruct((m, n), F32)] * 2 + [jax.ShapeDtypeStruct((m, n), BF16)],
        grid=(m // tm, n // FFN_TN), in_specs=[pl.BlockSpec((tm, k_dim), lambda i, j: (i, 0)), w_spec, w_spec],
        out_specs=[o_spec] * 3, compiler_params=_params(("parallel", "parallel")), name=name)(h, w_gate, w_up)


def _mm_swiglu_bwd(dx, w_down, g, u, *, name):
    m, k_dim = dx.shape
    n = w_down.shape[0]
    tm = _tile(m, (1024, 512, 256, 128))

    def kern(dx_ref, w_ref, g_ref, u_ref, dg_ref, du_ref):
        df = lax.dot_general(dx_ref[...].astype(BF16), w_ref[...].astype(BF16), (((1,), (1,)), ((), ())),
                             preferred_element_type=F32)
        _, vjp = jax.vjp(_swiglu, g_ref[...], u_ref[...])
        dg, du = vjp(df)
        dg_ref[...] = dg.astype(dg_ref.dtype)
        du_ref[...] = du.astype(du_ref.dtype)

    o_spec = pl.BlockSpec((tm, FFN_TN), lambda i, j: (i, j))
    return pl.pallas_call(
        kern, out_shape=[jax.ShapeDtypeStruct((m, n), BF16)] * 2, grid=(m // tm, n // FFN_TN),
        in_specs=[pl.BlockSpec((tm, k_dim), lambda i, j: (i, 0)), pl.BlockSpec((FFN_TN, k_dim), lambda i, j: (j, 0)),
                  o_spec, o_spec],
        out_specs=[o_spec] * 2, compiler_params=_params(("parallel", "parallel")), name=name)(dx, w_down, g, u)


def _rowwise(fn, rows, consts, out_rows, out_accs, *, name, tb=ROW_TILE, out_dtype=F32):
    t = (rows[0][0] if isinstance(rows[0], tuple) else rows[0]).shape[0]
    tb = min(tb, t)
    n_r, n_c, n_o, n_a = len(rows), len(consts), len(out_rows), len(out_accs)
    pieces = [w if isinstance(w, (list, tuple)) else [w] for w in out_rows]

    def kern(*refs):
        r_refs = refs[:n_r]
        c_refs = refs[n_r:n_r + n_c]
        o_refs = refs[n_r + n_c:n_r + n_c + n_o]
        a_refs = refs[n_r + n_c + n_o:]
        vals = fn(*[r[...] for r in r_refs], *[c[...] for c in c_refs])
        vals = list(vals) if isinstance(vals, (tuple, list)) else [vals]
        pos = 0
        for o_ref, ws in zip(o_refs, pieces):
            off = 0
            for w in ws:
                o_ref[:, off:off + w] = vals[pos].astype(o_ref.dtype)
                off += w
                pos += 1
        if n_a:
            @pl.when(pl.program_id(0) == 0)
            def _():
                for a_ref in a_refs:
                    a_ref[...] = jnp.zeros_like(a_ref)
            for a_ref, v in zip(a_refs, vals[pos:]):
                a_ref[...] += v

    in_specs, args = [], []
    for r in rows:
        if isinstance(r, tuple):
            arr, blk, w = r
            in_specs.append(pl.BlockSpec((tb, w), functools.partial(lambda i, blk: (i, blk), blk=blk)))
        else:
            arr = r
            in_specs.append(pl.BlockSpec((tb, arr.shape[1]), lambda i: (i, 0)))
        args.append(arr)
    for c in consts:
        in_specs.append(pl.BlockSpec(c.shape, lambda i: (0, 0)))
        args.append(c)
    out_shape = [jax.ShapeDtypeStruct((t, sum(ws)), out_dtype) for ws in pieces]
    out_specs = [pl.BlockSpec((tb, sum(ws)), lambda i: (i, 0)) for ws in pieces]
    for shp in out_accs:
        out_shape.append(jax.ShapeDtypeStruct(shp, F32))
        out_specs.append(pl.BlockSpec(shp, lambda i: (0, 0)))
    res = pl.pallas_call(
        kern, out_shape=out_shape, grid=(t // tb,), in_specs=in_specs, out_specs=out_specs,
        compiler_params=_params(("arbitrary",) if n_a else ("parallel",)), name=name)(*args)
    return res


def _rms(x, w):
    return x * lax.rsqrt(jnp.mean(x * x, axis=-1, keepdims=True) + RMS_EPS) * w


def _seg_sum(x, bd):
    return jnp.concatenate(
        [jnp.dot(x[:, LANES * j:LANES * (j + 1)], bd, precision=HIGHEST, preferred_element_type=F32)
         for j in range(x.shape[1] // LANES)], axis=1)


@jax.custom_vjp
def _seg(x, bd):
    return _seg_sum(x, bd)


_seg.defvjp(lambda x, bd: (_seg_sum(x, bd), bd), lambda bd, ct: (_seg_sum(ct, bd), jnp.zeros_like(bd)))


def _colsum(x):
    return jnp.sum(x, axis=0, keepdims=True)


def _prescan_math(r, k, xw, xa, xg, k_k, w0f, w0b, a0f, a0b, kaf, kab, wupf, wupb, aupf, aupb, gup, bd):
    kkr = k * k_k
    norm = jnp.sqrt(_seg(kkr * kkr, bd))
    kk = kkr / jnp.maximum(norm, NORM_EPS)
    th = jnp.tanh(xw)

    def direction(w0, wup, a0, aup, ka):
        logit = w0 + jnp.dot(th, wup, preferred_element_type=F32)
        w = jnp.exp(-LOG_DECAY_SCALE * jax.nn.sigmoid(logit))
        a = jax.nn.sigmoid(a0 + jnp.dot(xa, aup, preferred_element_type=F32))
        kd = k * (1.0 + (a - 1.0) * ka)
        return w, kd, kk * a

    wf, kdf, bf = direction(w0f, wupf, a0f, aupf, kaf)
    wb, kdb, bb = direction(w0b, wupb, a0b, aupb, kab)
    g = jnp.dot(jax.nn.sigmoid(xg), gup, preferred_element_type=F32)
    return kk, r, wf, wb, bf, bb, kdf, kdb, g


def _postscan_math(y, r, v, kdf, kdb, g, gn_w, gn_b, rkf, rkb, bd):
    mean = _seg(y, bd) * (1.0 / HEAD)
    yc = y - mean
    var = _seg(yc * yc, bd) * (1.0 / HEAD)
    yg = yc * lax.rsqrt(var + GN_EPS) * gn_w + gn_b
    bonus = (_seg(r * kdf * rkf, bd) + _seg(r * kdb * rkb, bd)) * v
    return (yg + bonus) * g


def _halo_specs(width, col_blk, tb, t):
    nb = t // SUBLANES
    step = tb // SUBLANES
    main = pl.BlockSpec((tb, width), lambda i: (i, col_blk))
    prev = pl.BlockSpec((SUBLANES, width), lambda i: (jnp.maximum(i * step - 1, 0), col_blk))
    nxt = pl.BlockSpec((SUBLANES, width), lambda i: (jnp.minimum((i + 1) * step, nb - 1), col_blk))
    return [main, prev, nxt]


def _neighbours(z, prev8, next8, first, last):
    tb = z.shape[0]
    row = lax.broadcasted_iota(jnp.int32, z.shape, 0)
    prow = jnp.where(first, 0.0, prev8[SUBLANES - 1:SUBLANES, :])
    nrow = jnp.where(last, 0.0, next8[0:1, :])
    down = jnp.where(row == 0, prow, pltpu.roll(z, 1, 0))
    up = jnp.where(row == tb - 1, nrow, pltpu.roll(z, tb - 1, 0))
    return down, up


def _shift_conv_fwd(p, mu, conv_w, seq, *, name, tb=ROW_TILE):
    t = p.shape[0]
    per_seq = seq // tb

    def kern(p_ref, pp_ref, pn_ref, mu_ref, cw_ref, pss_ref, oc_ref):
        i = pl.program_id(0)
        first = (i % per_seq) == 0
        last = (i % per_seq) == per_seq - 1
        ps = p_ref[:, :D_SP]
        down, up = _neighbours(ps, pp_ref[:, :D_SP], pn_ref[:, :D_SP], first, last)
        pss_ref[...] = ps + mu_ref[...] * (0.5 * (down + up) - ps)
        gb = p_ref[:, D_SP:D_SP + D_CONV]
        u = p_ref[:, D_SP + D_CONV:D_SP + 2 * D_CONV] * p_ref[:, D_SP + 2 * D_CONV:]
        u_p = pp_ref[:, D_SP + D_CONV:D_SP + 2 * D_CONV] * pp_ref[:, D_SP + 2 * D_CONV:]
        u_n = pn_ref[:, D_SP + D_CONV:D_SP + 2 * D_CONV] * pn_ref[:, D_SP + 2 * D_CONV:]
        udown, uup = _neighbours(u, u_p, u_n, first, last)
        oc_ref[...] = gb * (cw_ref[0:1, :] * udown + cw_ref[1:2, :] * u + cw_ref[2:3, :] * uup)

    return pl.pallas_call(
        kern,
        out_shape=[jax.ShapeDtypeStruct((t, D_SP), F32), jax.ShapeDtypeStruct((t, D_CONV), F32)],
        grid=(t // tb,),
        in_specs=_halo_specs(D_INP, 0, tb, t) + [pl.BlockSpec((1, D_SP), lambda i: (0, 0)),
                                                 pl.BlockSpec((SUBLANES, D_CONV), lambda i: (0, 0))],
        out_specs=[pl.BlockSpec((tb, D_SP), lambda i: (i, 0)), pl.BlockSpec((tb, D_CONV), lambda i: (i, 0))],
        compiler_params=_params(("parallel",)), name=name)(p, p, p, mu, conv_w)


def _shift_conv_bwd(p, d_pss, d_o, mu, conv_w, seq, *, name, tb=ROW_TILE):
    t = p.shape[0]
    per_seq = seq // tb

    def kern(p_ref, pp_ref, pn_ref, d_ref, dp_ref, dn_ref, do_ref, dop_ref, don_ref, mu_ref, cw_ref,
             out_ref, dmu_ref, dcw_ref):
        i = pl.program_id(0)
        first = (i % per_seq) == 0
        last = (i % per_seq) == per_seq - 1

        @pl.when(i == 0)
        def _():
            dmu_ref[...] = jnp.zeros_like(dmu_ref)
            dcw_ref[...] = jnp.zeros_like(dcw_ref)

        mu_v = mu_ref[...]
        ps = p_ref[:, :D_SP]
        down, up = _neighbours(ps, pp_ref[:, :D_SP], pn_ref[:, :D_SP], first, last)
        d = d_ref[...]
        ddown, dup = _neighbours(d, dp_ref[...], dn_ref[...], first, last)
        out_ref[:, :D_SP] = (d - mu_v * d + 0.5 * (mu_v * ddown + mu_v * dup)).astype(out_ref.dtype)
        dmu_ref[...] += _colsum(d * (0.5 * (down + up) - ps))

        def parts(ref):
            return (ref[:, D_SP:D_SP + D_CONV], ref[:, D_SP + D_CONV:D_SP + 2 * D_CONV],
                    ref[:, D_SP + 2 * D_CONV:])

        gb, gc, hh = parts(p_ref)
        gb_p, gc_p, hh_p = parts(pp_ref)
        gb_n, gc_n, hh_n = parts(pn_ref)
        u = gc * hh
        udown, uup = _neighbours(u, gc_p * hh_p, gc_n * hh_n, first, last)
        cw0, cw1, cw2 = cw_ref[0:1, :], cw_ref[1:2, :], cw_ref[2:3, :]
        do = do_ref[...]
        duc = do * gb
        ducdown, ducup = _neighbours(duc, dop_ref[...] * gb_p, don_ref[...] * gb_n, first, last)
        du = cw0 * ducup + cw1 * duc + cw2 * ducdown
        out_ref[:, D_SP:D_SP + D_CONV] = (do * (cw0 * udown + cw1 * u + cw2 * uup)).astype(out_ref.dtype)
        out_ref[:, D_SP + D_CONV:D_SP + 2 * D_CONV] = (du * hh).astype(out_ref.dtype)
        out_ref[:, D_SP + 2 * D_CONV:] = (du * gc).astype(out_ref.dtype)
        dcw_ref[0:1, :] += _colsum(duc * udown)
        dcw_ref[1:2, :] += _colsum(duc * u)
        dcw_ref[2:3, :] += _colsum(duc * uup)

    return pl.pallas_call(
        kern,
        out_shape=[jax.ShapeDtypeStruct((t, D_INP), BF16), jax.ShapeDtypeStruct((1, D_SP), F32),
                   jax.ShapeDtypeStruct((SUBLANES, D_CONV), F32)],
        grid=(t // tb,),
        in_specs=(_halo_specs(D_INP, 0, tb, t) + _halo_specs(D_SP, 0, tb, t) + _halo_specs(D_CONV, 1, tb, t)
                  + [pl.BlockSpec((1, D_SP), lambda i: (0, 0)),
                     pl.BlockSpec((SUBLANES, D_CONV), lambda i: (0, 0))]),
        out_specs=[pl.BlockSpec((tb, D_INP), lambda i: (i, 0)), pl.BlockSpec((1, D_SP), lambda i: (0, 0)),
                   pl.BlockSpec((SUBLANES, D_CONV), lambda i: (0, 0))],
        compiler_params=_params(("arbitrary",)), name=name)(p, p, p, d_pss, d_pss, d_pss, d_o, d_o, d_o, mu, conv_w)


N_CHAIN = 16
N_GROUP = LANES // N_CHAIN
V_HI = HEAD // SUBLANES
G_KK, G_R, G_W, G_B, G_KD = 0, 1, (2, 3), (4, 5), (6, 7)


K_HI = HEAD // SUBLANES


def _tree_sum(terms):
    terms = list(terms)
    while len(terms) > 1:
        terms = [a + b for a, b in zip(terms[::2], terms[1::2])]
    return terms[0]


def _kscan_specs(nc):
    same = lambda c: c
    mirror = lambda c: nc - 1 - c

    def k_spec(fn):
        return pl.BlockSpec((SCAN_CHUNK, HEAD, LANES), lambda c: (fn(c), 0, 0))

    def v_spec(fn):
        return pl.BlockSpec((SCAN_CHUNK, SUBLANES, LANES), lambda c: (fn(c), 0, 0))

    return same, mirror, k_spec, v_spec


ST_SHAPE = (2, K_HI, V_HI, SUBLANES, LANES)


def _lane_group_index():
    lane = lax.broadcasted_iota(jnp.int32, (SUBLANES, LANES), 1)
    return lax.shift_right_logical(lane, jnp.full_like(lane, 4))


def _spread_groups(x, grp):
    rolled = [x] + [pltpu.roll(x, s * N_CHAIN, 1) for s in range(1, N_GROUP)]
    out = []
    for j in range(N_GROUP):
        t = rolled[(0 - j) % N_GROUP]
        for g in range(1, N_GROUP):
            t = jnp.where(grp == g, rolled[(g - j) % N_GROUP], t)
        out.append(t)
    return out


def _gather_groups(tiles, grp):
    total = None
    for s in range(N_GROUP):
        b = tiles[s % N_GROUP]
        for g in range(1, N_GROUP):
            b = jnp.where(grp == g, tiles[(g + s) % N_GROUP], b)
        b = pltpu.roll(b, s * N_CHAIN, 1) if s else b
        total = b if total is None else total + b
    return total


def _lane_group_sum(x):
    return _tree_sum([x] + [pltpu.roll(x, k * N_CHAIN, 1) for k in range(1, N_GROUP)])


def _key_row(x_t, grp, kh):
    r = SUBLANES * grp + kh
    return jnp.broadcast_to(x_t[r:r + 1, :], (SUBLANES, LANES))


def _acc(total, term):
    return term if total is None else total + term


SA_SHAPE = (2, V_HI, SUBLANES, LANES)


def _scan_fwd(xall, v_c, *, gather=(), name):
    steps = xall.shape[0]
    nc = steps // SCAN_CHUNK
    same, mirror, k_spec, v_spec = _kscan_specs(nc)
    last = SCAN_CHUNK - 1
    n_x = len(gather)

    def kern(*refs):
        xf_ref, xb_ref, vf_ref, vb_ref = refs[:4]
        yf_ref, yb_ref, hist_ref, fin_ref, sa_ref = refs[4 + n_x:9 + n_x]
        st_ref = refs[9 + 2 * n_x]
        c = pl.program_id(0)

        def riders():
            return _exchange_copies(refs[4:4 + n_x], refs[9 + n_x:9 + 2 * n_x], 0, *refs[10 + 2 * n_x:])

        @pl.when(c == 0)
        def _():
            st_ref[...] = jnp.zeros_like(st_ref)
            if n_x:
                for cp in riders():
                    cp.start()

        hist_ref[0] = st_ref[...]
        grp = _lane_group_index()

        def body(i, put):
            j = last - i
            for d, (x_t, v_t, y_ref, at) in enumerate(((xf_ref[i], vf_ref[i], yf_ref, i),
                                                       (xb_ref[j], vb_ref[j], yb_ref, j))):
                v_b = _spread_groups(v_t, grp)
                part = [None] * V_HI
                for kh in range(K_HI):
                    kk_r = _key_row(x_t, G_KK, kh)
                    for vh in range(V_HI):
                        part[vh] = _acc(part[vh], hist_ref[i, d, kh, vh] * kk_r)
                sa = [_lane_group_sum(p) for p in part]
                for vh in range(V_HI):
                    sa_ref[i, d, vh] = sa[vh]
                y_p = [None] * V_HI
                for kh in range(K_HI):
                    r_r, w_r = _key_row(x_t, G_R, kh), _key_row(x_t, G_W[d], kh)
                    b_r, kd_r = _key_row(x_t, G_B[d], kh), _key_row(x_t, G_KD[d], kh)
                    for vh in range(V_HI):
                        new = hist_ref[i, d, kh, vh] * w_r - sa[vh] * b_r + v_b[vh] * kd_r
                        put(d, kh, vh, new)
                        y_p[vh] = _acc(y_p[vh], new * r_r)
                y_ref[at] = _gather_groups(y_p, grp)

        def step(i, carry):
            def put(d, kh, vh, val):
                hist_ref[i + 1, d, kh, vh] = val
            body(i, put)
            return carry

        lax.fori_loop(0, last, step, 0, unroll=SCAN_UNROLL)

        def put_carry(d, kh, vh, val):
            st_ref[d, kh, vh] = val

        body(last, put_carry)

        @pl.when(c == nc - 1)
        def _():
            fin_ref[...] = st_ref[...]
            if n_x:
                for cp in riders():
                    cp.wait()

    return pl.pallas_call(
        kern,
        out_shape=[jax.ShapeDtypeStruct((steps, SUBLANES, LANES), F32)] * 2
        + [jax.ShapeDtypeStruct((steps,) + ST_SHAPE, F32), jax.ShapeDtypeStruct(ST_SHAPE, F32),
           jax.ShapeDtypeStruct((steps,) + SA_SHAPE, F32)]
        + _exchange_out_shapes(gather, 0),
        grid=(nc,), in_specs=[k_spec(same), k_spec(mirror), v_spec(same), v_spec(mirror)] + _hbm_specs(n_x),
        out_specs=[v_spec(same), v_spec(mirror),
                   pl.BlockSpec((SCAN_CHUNK,) + ST_SHAPE, lambda c: (c, 0, 0, 0, 0, 0)),
                   pl.BlockSpec(ST_SHAPE, lambda c: (0, 0, 0, 0, 0)),
                   pl.BlockSpec((SCAN_CHUNK,) + SA_SHAPE, lambda c: (c, 0, 0, 0, 0))] + _hbm_specs(n_x),
        scratch_shapes=[pltpu.VMEM(ST_SHAPE, F32)] + (_exchange_sems(n_x) if n_x else []),
        compiler_params=_params(("arbitrary",), SCAN_VMEM_LIMIT), name=name)(xall, xall, v_c, v_c, *gather)


def _scan_bwd(xall, v_c, dy_c, hist, fin, sa, *, exchange=(), name):
    steps = xall.shape[0]
    nc = steps // SCAN_CHUNK
    same, back, k_spec, v_spec = _kscan_specs(nc)
    last = SCAN_CHUNK - 1
    n_x = len(exchange)

    def kern(*refs):
        xf_ref, xb_ref, vf_ref, vb_ref, dyf_ref, dyb_ref, hist_ref, fin_ref, sa_ref = refs[:9]
        gf_ref, gb_ref, dvf_ref, dvb_ref = refs[9 + n_x:13 + n_x]
        ds_ref, after_ref = refs[13 + 2 * n_x:15 + 2 * n_x]
        c = pl.program_id(0)

        def riders():
            return _exchange_copies(refs[9:9 + n_x], refs[13 + n_x:13 + 2 * n_x], n_x, *refs[15 + 2 * n_x:])

        @pl.when(c == 0)
        def _():
            ds_ref[...] = jnp.zeros_like(ds_ref)
            after_ref[...] = fin_ref[...]
            if n_x:
                for cp in riders():
                    cp.start()

        grp = _lane_group_index()
        row = lax.broadcasted_iota(jnp.int32, (SUBLANES, LANES), 0)
        zero = jnp.zeros((SUBLANES, LANES), F32)

        def body(i, after):
            j = last - i
            for d, (x_t, v_t, dy_t, g_ref, dv_ref, at) in enumerate((
                    (xf_ref[i], vf_ref[i], dyf_ref[i], gf_ref, dvf_ref, i),
                    (xb_ref[j], vb_ref[j], dyb_ref[j], gb_ref, dvb_ref, j))):
                v_s, dy_s = _spread_groups(v_t, grp), _spread_groups(dy_t, grp)
                dsa_p, dv_p = [None] * V_HI, [None] * V_HI
                for kh in range(K_HI):
                    r_r = _key_row(x_t, G_R, kh)
                    b_r, kd_r = _key_row(x_t, G_B[d], kh), _key_row(x_t, G_KD[d], kh)
                    for vh in range(V_HI):
                        g = ds_ref[d, kh, vh] + dy_s[vh] * r_r
                        ds_ref[d, kh, vh] = g
                        dsa_p[vh] = _acc(dsa_p[vh], g * b_r)
                        dv_p[vh] = _acc(dv_p[vh], g * kd_r)
                dsa = [-_lane_group_sum(p) for p in dsa_p]
                sa = [sa_ref[i, d, vh] for vh in range(V_HI)]
                dv_ref[at] = _gather_groups(dv_p, grp)
                blocks = {G_KK: zero, G_R: zero, G_W[d]: zero, G_B[d]: zero, G_KD[d]: zero}
                for kh in range(K_HI):
                    w_r, kk_r = _key_row(x_t, G_W[d], kh), _key_row(x_t, G_KK, kh)
                    dkk = dr = dw = db = dkd = None
                    for vh in range(V_HI):
                        g, before = ds_ref[d, kh, vh], hist_ref[i, d, kh, vh]
                        dr = _acc(dr, after(d, kh, vh) * dy_s[vh])
                        dw = _acc(dw, g * before)
                        dkd = _acc(dkd, g * v_s[vh])
                        db = _acc(db, g * sa[vh])
                        dkk = _acc(dkk, before * dsa[vh])
                        ds_ref[d, kh, vh] = g * w_r + dsa[vh] * kk_r
                    for gi, a in ((G_KK, dkk), (G_R, dr), (G_W[d], dw), (G_B[d], -db), (G_KD[d], dkd)):
                        blocks[gi] = jnp.where(row == kh, _colsum(a), blocks[gi])
                for gi in range(N_GROUP):
                    g_ref[at, SUBLANES * gi:SUBLANES * (gi + 1), :] = blocks.get(gi, zero)

        body(last, lambda d, kh, vh: after_ref[d, kh, vh])

        def step(ii, carry):
            i = last - ii
            body(i, lambda d, kh, vh: hist_ref[i + 1, d, kh, vh])
            return carry

        lax.fori_loop(1, SCAN_CHUNK, step, 0, unroll=SCAN_UNROLL)
        after_ref[...] = hist_ref[0]

        if n_x:
            @pl.when(c == nc - 1)
            def _():
                for cp in riders():
                    cp.wait()

    return pl.pallas_call(
        kern,
        out_shape=[jax.ShapeDtypeStruct((steps, HEAD, LANES), F32)] * 2
        + [jax.ShapeDtypeStruct((steps, SUBLANES, LANES), F32)] * 2 + _exchange_out_shapes(exchange, n_x),
        grid=(nc,),
        in_specs=[k_spec(back), k_spec(same), v_spec(back), v_spec(same), v_spec(back), v_spec(same),
                  pl.BlockSpec((SCAN_CHUNK,) + ST_SHAPE, lambda c: (back(c), 0, 0, 0, 0, 0)),
                  pl.BlockSpec(ST_SHAPE, lambda c: (0, 0, 0, 0, 0)),
                  pl.BlockSpec((SCAN_CHUNK,) + SA_SHAPE, lambda c: (back(c), 0, 0, 0, 0))] + _hbm_specs(n_x),
        out_specs=[k_spec(back), k_spec(same), v_spec(back), v_spec(same)] + _hbm_specs(n_x),
        scratch_shapes=[pltpu.VMEM(ST_SHAPE, F32), pltpu.VMEM(ST_SHAPE, F32)]
        + (_exchange_sems(n_x) if n_x else []),
        compiler_params=_params(("arbitrary",), SCAN_VMEM_LIMIT), name=name)(xall, xall, v_c, v_c, dy_c, dy_c, hist, fin, sa,
                                                            *exchange)


def _bf16_pieces(x):
    hi = x.astype(BF16)
    return hi, (x - hi.astype(F32)).astype(BF16)


def _to_key_rows(wide, bsz, seq, *, name):
    assert bsz == 2
    perm = _key_row_maps()
    tt = min(RELAYOUT_TILE, seq)
    per_seq = seq // tt

    def kern(x0_ref, x1_ref, p0_ref, p1_ref, o_ref):
        total = None
        for x_ref, p_ref in ((x0_ref, p0_ref), (x1_ref, p1_ref)):
            for piece in _bf16_pieces(x_ref[...]):
                term = jnp.dot(piece, p_ref[...], preferred_element_type=F32)
                total = term if total is None else total + term
        for r in range(K_HI):
            o_ref[:, r, :] = total[:, LANES * r:LANES * (r + 1)]

    p_spec = pl.BlockSpec((D_RWKV, K_HI * LANES), lambda i, a: (0, 0))
    return pl.pallas_call(
        kern, out_shape=jax.ShapeDtypeStruct((seq, HEAD, LANES), F32), grid=(per_seq, N_GROUP),
        in_specs=[pl.BlockSpec((tt, D_RWKV), lambda i, a: (i, a)),
                  pl.BlockSpec((tt, D_RWKV), lambda i, a: (per_seq + i, a)), p_spec, p_spec],
        out_specs=pl.BlockSpec((tt, K_HI, LANES), lambda i, a: (i, a, 0)),
        compiler_params=_params(("parallel", "parallel")), name=name)(wide, wide, *perm)


def _key_row_maps():
    src = jnp.arange(D_RWKV)
    head, kh, kl = src // HEAD, (src // SUBLANES) % K_HI, src % SUBLANES
    dst = jnp.arange(K_HI * LANES)
    return [((kh[:, None] == dst[None, :] // LANES) & (kl[:, None] == (dst[None, :] // N_CHAIN) % SUBLANES)
             & ((dst[None, :] // N_HEAD) % 2 == b) & (head[:, None] == dst[None, :] % N_HEAD)).astype(BF16)
            for b in range(2)]


def _from_key_rows(g_f, g_b, bsz, seq, *, name):
    assert bsz == 2
    maps = jnp.concatenate([m.T for m in _key_row_maps()], axis=1)
    tt = min(RELAYOUT_TILE, seq)
    per_seq = seq // tt

    def kern(gf_ref, gb_ref, q_ref, o_ref):
        a = pl.program_id(1)
        shared = a <= G_R
        from_f = shared | (a % 2 == G_W[0] % 2)

        def rearranged(g_ref):
            g = jnp.concatenate([g_ref[:, r, :] for r in range(K_HI)], axis=1)
            hi, mid = (jnp.dot(piece, q_ref[...], preferred_element_type=F32) for piece in _bf16_pieces(g))
            both = hi + mid
            return both[:, :D_RWKV], both[:, D_RWKV:]

        @pl.when(from_f)
        def _():
            o_ref[0], o_ref[1] = rearranged(gf_ref)

        @pl.when(jnp.logical_not(from_f))
        def _():
            o_ref[0], o_ref[1] = rearranged(gb_ref)

        @pl.when(shared)
        def _():
            more = rearranged(gb_ref)
            o_ref[0] += more[0]
            o_ref[1] += more[1]

    g_spec = pl.BlockSpec((tt, K_HI, LANES), lambda i, a: (i, a, 0))
    out = pl.pallas_call(
        kern, out_shape=jax.ShapeDtypeStruct((bsz, seq, N_GROUP * D_RWKV), F32), grid=(per_seq, N_GROUP),
        in_specs=[g_spec, g_spec, pl.BlockSpec((K_HI * LANES, bsz * D_RWKV), lambda i, a: (0, 0))],
        out_specs=pl.BlockSpec((bsz, tt, D_RWKV), lambda i, a: (0, i, a)),
        compiler_params=_params(("parallel", "parallel")), name=name)(g_f, g_b, maps)
    return out.reshape(bsz * seq, N_GROUP * D_RWKV)


def _to_value_rows(a, bsz, seq):
    z = a.reshape(bsz, seq, N_HEAD, V_HI, SUBLANES).transpose(1, 4, 3, 0, 2)
    return z.reshape(seq, SUBLANES, LANES)


def _from_value_rows(y, bsz, seq):
    z = y.reshape(seq, SUBLANES, V_HI, bsz, N_HEAD).transpose(3, 0, 4, 2, 1)
    return z.reshape(bsz * seq, D_RWKV)


def _pad_cols(a, segs):
    out, off = [], 0
    for w, wp in segs:
        out.append(a[..., off:off + w])
        if wp > w:
            out.append(jnp.zeros(a.shape[:-1] + (wp - w,), a.dtype))
        off += w
    return jnp.concatenate(out, axis=-1)


def _unpad_cols(a, segs):
    out, off = [], 0
    for w, wp in segs:
        out.append(a[..., off:off + w])
        off += wp
    return jnp.concatenate(out, axis=-1)


P_SEGS = ((3 * D_RWKV, 3 * D_RWKV), (D_LORA, 128), (D_LORA, 128), (D_GATE, 256), (3 * D_CONV, 3 * D_CONV))
S_SEGS = P_SEGS[:4]


def _pad_rows(a, rows):
    return jnp.concatenate([a, jnp.zeros((rows - a.shape[0], a.shape[1]), a.dtype)], axis=0)


LATE = ("w_out", "w_gate", "w_up", "w_down")


def _local_step(x, target, w, late=None):
    bsz, seq, _ = x.shape
    t = bsz * seq
    x2d = x.reshape(t, D_MODEL)
    tg2d = target.reshape(t, D_MODEL)
    row = lambda a: a.reshape(1, -1).astype(F32)

    w_in = _pad_cols(w["w_in"][0], P_SEGS)
    mu = _pad_cols(row(w["mu_shift"]), S_SEGS)
    wupf, wupb, aupf, aupb = (_pad_rows(w[n][0].astype(F32), 128) for n in ("w_up_f", "w_up_b", "a_up_f", "a_up_b"))
    gup = _pad_rows(w["g_up"][0].astype(F32), 256)
    conv_w = _pad_rows(w["conv_w"][0].astype(F32), SUBLANES)
    norm1, norm2, normf = row(w["norm1_w"]), row(w["norm2_w"]), row(w["norm_f_w"])
    vec = {n: row(w[n]) for n in VEC}
    head_of = jnp.arange(LANES) // HEAD
    bd = (head_of[:, None] == head_of[None, :]).astype(F32)
    pre_consts = [vec["k_k"], vec["w0_f"], vec["w0_b"], vec["a0_f"], vec["a0_b"], vec["k_a_f"], vec["k_a_b"],
                  wupf, wupb, aupf, aupb, gup, bd]
    post_consts = [vec["gn_w"], vec["gn_b"], vec["r_k_f"], vec["r_k_b"], bd]

    h1, = _rowwise(_rms, [x2d], [norm1], [D_MODEL], [], name="rms1_fwd", out_dtype=BF16, tb=WIDE_TILE)
    p = _mm(h1, w_in, name="mm_in")
    pss, oconv = _shift_conv_fwd(p, mu, conv_w, seq, name="shift_conv_fwd")
    pre_rows = [(pss, 0, 512), (pss, 1, 512), (pss, XW0 // 128, 128), (pss, XA0 // 128, 128), (pss, XG0 // 256, 256)]
    sc, g = _rowwise(_prescan_math, pre_rows, pre_consts, [[D_RWKV] * N_GROUP, D_RWKV], [], name="prescan_fwd",
                     tb=2 * ROW_TILE)
    xall = _to_key_rows(sc, bsz, seq, name="to_key_rows")
    v_l = _to_value_rows(pss[:, 2 * D_RWKV:3 * D_RWKV], bsz, seq)
    y_f, y_b, hist, fin, sa, *gathered = _scan_fwd(xall, v_l, gather=[late[n] for n in LATE] if late else (),
                                                   name="scan_fwd")
    w_out, w_gate, w_up, w_down = (
        (_from_slots(a, SHARD_AXIS[n]) if late else w[n])[0] for n, a in zip(LATE, gathered or LATE))
    y = _from_value_rows(y_f + y_b, bsz, seq)
    post_rows = [y, (pss, 0, 512), (pss, 2, 512), (sc, G_KD[0], 512), (sc, G_KD[1], 512), g]

    def post_fwd(y_, r_, v_, kdf_, kdb_, g_, oc_, *consts):
        return _postscan_math(y_, r_, v_, kdf_, kdb_, g_, *consts), oc_

    o, = _rowwise(post_fwd, post_rows + [oconv], post_consts, [[D_RWKV, D_CONV]], [], name="postscan_fwd",
                  out_dtype=BF16, tb=2 * ROW_TILE)
    x1 = _mm(o, w_out, add=x2d, name="mm_out")
    h2, = _rowwise(_rms, [x1], [norm2], [D_MODEL], [], name="rms2_fwd", out_dtype=BF16, tb=WIDE_TILE)
    gg, uu, ff = _mm_swiglu(h2, w_gate, w_up, name="mm_gate_up")
    x2 = _mm(ff, w_down, add=x1, name="mm_down")

    def final(x_, tg_, wn_):
        yo, vjp = jax.vjp(_rms, x_, wn_)
        err = yo - tg_
        dx_, dwn_ = vjp(err * (1.0 / D_MODEL))
        part = jnp.sum(jnp.sum(err * err, axis=1, keepdims=True), axis=0, keepdims=True) * (0.5 / D_MODEL)
        return dx_, part + jnp.zeros((1, LANES), F32), dwn_

    dx2, loss_acc, d_normf = _rowwise(final, [x2, tg2d], [normf], [D_MODEL], [(1, LANES), (1, D_MODEL)],
                                      name="loss_head", tb=WIDE_TILE)
    dgg, duu = _mm_swiglu_bwd(dx2, w_down, gg, uu, name="mm_down_dx")
    g_w_down = _mm(ff, dx2, ta=True, name="mm_down_dw")
    dh2 = _mm(dgg, w_gate, tb=True, name="mm_gate_dx")
    dh2 = _mm(duu, w_up, tb=True, add=dh2, name="mm_up_dx")
    g_w_gate = _mm(h2, dgg, ta=True, name="mm_gate_dw")
    g_w_up = _mm(h2, duu, ta=True, name="mm_up_dw")

    def rms_bwd(x_, dh_, dres_, wn_):
        _, vjp = jax.vjp(_rms, x_, wn_)
        dx_, dwn_ = vjp(dh_)
        return dx_ + dres_, dwn_

    dx1, d_norm2 = _rowwise(rms_bwd, [x1, dh2, dx2], [norm2], [D_MODEL], [(1, D_MODEL)], name="rms2_bwd", tb=WIDE_TILE)
    do = _mm(dx1, w_out, tb=True, name="mm_out_dx")
    g_w_out = _mm(o, dx1, ta=True, name="mm_out_dw")

    def post_bwd(y_, r_, v_, kdf_, kdb_, g_, do_, *consts):
        _, vjp = jax.vjp(lambda *a: _postscan_math(*a, consts[4]), y_, r_, v_, kdf_, kdb_, g_, *consts[:4])
        return vjp(do_)

    (dy, dr_c, dv_c, dkdf_c, dkdb_c, dg, d_gn_w, d_gn_b, d_rkf, d_rkb) = _rowwise(
        post_bwd, post_rows + [(do, 0, 512)], post_consts, [D_RWKV] * 6, [(1, D_RWKV)] * 4, name="postscan_bwd")
    dy_l = _to_value_rows(dy, bsz, seq)
    late_grads = {"w_out": g_w_out[None], "w_gate": g_w_gate[None], "w_up": g_w_up[None], "w_down": g_w_down[None]}
    g_f, g_b, dv_f, dv_b, *late_parts = _scan_bwd(
        xall, v_l, dy_l, hist, fin, sa, name="scan_bwd",
        exchange=[_to_slots(late_grads[n], SHARD_AXIS[n]).astype(BF16) for n in LATE] if late else ())
    dsc = _from_key_rows(g_f, g_b, bsz, seq, name="from_key_rows")
    dv_s = _from_value_rows(dv_f + dv_b, bsz, seq)

    def pre_bwd(r_, k_, xw_, xa_, xg_, dkk_, dr_s, dwf_, dwb_, dbf_, dbb_, dkdf_s, dkdb_s,
                dr_c_, dv_c_, dv_s_, dkdf_c_, dkdb_c_, dg_, *consts):
        _, vjp = jax.vjp(lambda *a: _prescan_math(*a, consts[-1]), r_, k_, xw_, xa_, xg_, *consts[:-1])
        grads = vjp((dkk_, dr_s + dr_c_, dwf_, dwb_, dbf_, dbb_, dkdf_s + dkdf_c_, dkdb_s + dkdb_c_, dg_))
        dr_, dk_, dxw_, dxa_, dxg_ = grads[:5]
        return (dr_, dk_, dv_c_ + dv_s_, dxw_, dxa_, dxg_) + tuple(grads[5:])

    pre_b_rows = (pre_rows + [(dsc, j, 512) for j in range(N_GROUP)]
                  + [dr_c, dv_c, dv_s, dkdf_c, dkdb_c, dg])
    pre_b = _rowwise(pre_bwd, pre_b_rows, pre_consts, [[512, 512, 512, 128, 128, 256]],
                     [(1, D_RWKV)] * 7 + [(128, D_RWKV)] * 4 + [(256, D_RWKV)], name="prescan_bwd")
    d_pss = pre_b[0]
    d_kk_, d_w0f, d_w0b, d_a0f, d_a0b, d_kaf, d_kab, d_wupf, d_wupb, d_aupf, d_aupb, d_gup = pre_b[1:]
    dp, d_mu, d_conv = _shift_conv_bwd(p, d_pss, do, mu, conv_w, seq, name="shift_conv_bwd")
    g_w_in = _mm(h1, dp, ta=True, name="mm_in_dw")
    grads = {
        "w_in": _unpad_cols(g_w_in, P_SEGS)[None], "mu_shift": _unpad_cols(d_mu, S_SEGS),
        "w_up_f": d_wupf[None, :D_LORA], "w0_f": d_w0f, "w_up_b": d_wupb[None, :D_LORA], "w0_b": d_w0b,
        "a_up_f": d_aupf[None, :D_LORA], "a0_f": d_a0f, "a_up_b": d_aupb[None, :D_LORA], "a0_b": d_a0b,
        "g_up": d_gup[None, :D_GATE], "k_k": d_kk_, "k_a_f": d_kaf, "k_a_b": d_kab,
        "r_k_f": d_rkf, "r_k_b": d_rkb, "gn_w": d_gn_w, "gn_b": d_gn_b, "conv_w": d_conv[None, :3],
        "w_out": g_w_out[None], "norm2_w": d_norm2, "w_gate": g_w_gate[None], "w_up": g_w_up[None],
        "w_down": g_w_down[None], "norm_f_w": d_normf,
    }
    early = ("w_in",) + LORA
    parts = dict(zip(LATE, late_parts))
    if late:
        vec_rows = jnp.concatenate([grads[n] for n in VEC] + [jnp.zeros((16 - len(VEC), D_RWKV), F32)], axis=0)
        slots = [_to_slots(grads[n], SHARD_AXIS[n]).astype(BF16 if n in BIG else F32) for n in early]
        dh1, *recv = _mm(dp, w_in, tb=True, exchange=(slots, [vec_rows]), name="mm_in_dx")
        parts.update(zip(early + ("vec",), recv))
    else:
        dh1 = _mm(dp, w_in, tb=True, name="mm_in_dx")
    dx, grads["norm1_w"] = _rowwise(rms_bwd, [x2d, dh1, dx1], [norm1], [D_MODEL], [(1, D_MODEL)], name="rms1_bwd",
                                    tb=WIDE_TILE)
    return loss_acc, dx.reshape(bsz, seq, D_MODEL), grads, parts


def _hbm_specs(n):
    return [pl.BlockSpec(memory_space=pl.ANY)] * n


def _all_gather(arrs, *, name):
    n = len(arrs)

    def body(*refs):
        x_refs, out_refs = refs[:n], refs[n:2 * n]
        send_sems, recv_sems, local_sems = refs[2 * n:]
        x, y, c = lax.axis_index("x"), lax.axis_index("y"), lax.axis_index("c")
        me, sibling = (x, y, c), (x, y, 1 - c)
        chips = [(1 - x, y), (x, 1 - y), (1 - x, 1 - y)]

        def slot(a, px, py, pc):
            return out_refs[a].at[4 * px + 2 * py + pc]

        def copy(a, k, block, to, src=None):
            return pltpu.make_async_remote_copy(
                src_ref=slot(a, *block) if src is None else src, dst_ref=slot(a, *block),
                send_sem=send_sems.at[k, a], recv_sem=recv_sems.at[k, a],
                device_id=to, device_id_type=pl.DeviceIdType.MESH)

        mine = [pltpu.make_async_copy(x_refs[a], slot(a, *me), local_sems.at[a]) for a in range(n)]
        for cp in mine:
            cp.start()
        first = []
        for a in range(n):
            first.append(copy(a, 0, me, sibling, src=x_refs[a]))
            first += [copy(a, 1 + j, me, (*chip, c), src=x_refs[a]) for j, chip in enumerate(chips)]
        for cp in first:
            cp.start()
        passed = []
        for j, chip in enumerate(chips):
            for a in range(n):
                copy(a, 1 + j, (*chip, c), me).wait_recv()
                cp = copy(a, 4 + j, (*chip, c), sibling)
                cp.start()
                passed.append(cp)
        for a in range(n):
            copy(a, 0, sibling, me).wait_recv()
            for j, chip in enumerate(chips):
                copy(a, 4 + j, (*chip, 1 - c), me).wait_recv()
        for cp in first + passed:
            cp.wait_send()
        for cp in mine:
            cp.wait()

    return pl.pallas_call(
        body, out_shape=[jax.ShapeDtypeStruct((N_DEV,) + a.shape, a.dtype) for a in arrs],
        in_specs=_hbm_specs(n), out_specs=_hbm_specs(n),
        scratch_shapes=[pltpu.SemaphoreType.DMA((7, n)), pltpu.SemaphoreType.DMA((7, n)),
                        pltpu.SemaphoreType.DMA((n,))],
        name=name)(*arrs)


def _exchange(sliced, whole, *, name):
    arrs = list(sliced) + list(whole)
    n, n_sliced = len(arrs), len(sliced)

    def body(*refs):
        copies = _exchange_copies(refs[:n], refs[n:2 * n], n_sliced, *refs[2 * n:])
        for cp in copies:
            cp.start()
        for cp in copies:
            cp.wait()

    return pl.pallas_call(
        body, out_shape=_exchange_out_shapes(arrs, n_sliced), in_specs=_hbm_specs(n), out_specs=_hbm_specs(n),
        scratch_shapes=_exchange_sems(n), name=name)(*arrs)


def _exchange_out_shapes(arrs, n_sliced):
    return [jax.ShapeDtypeStruct(a.shape if i < n_sliced else (N_DEV,) + a.shape, a.dtype)
            for i, a in enumerate(arrs)]


def _exchange_sems(n):
    return [pltpu.SemaphoreType.DMA((7, n)), pltpu.SemaphoreType.DMA((7, n)), pltpu.SemaphoreType.DMA((n,))]


def _exchange_copies(in_refs, out_refs, n_sliced, send_sems, recv_sems, local_sems):
    n = len(in_refs)
    x, y, c = lax.axis_index("x"), lax.axis_index("y"), lax.axis_index("c")
    me = 4 * x + 2 * y + c

    def src(a, dev):
        return in_refs[a].at[dev] if a < n_sliced else in_refs[a]

    copies = [pltpu.make_async_copy(src(a, me), out_refs[a].at[me], local_sems.at[a]) for a in range(n)]
    for k in range(1, N_DEV):
        px = 1 - x if k & 4 else x
        py = 1 - y if k & 2 else y
        pc = 1 - c if k & 1 else c
        for a in range(n):
            copies.append(pltpu.make_async_remote_copy(
                src_ref=src(a, 4 * px + 2 * py + pc), dst_ref=out_refs[a].at[me],
                send_sem=send_sems.at[k - 1, a], recv_sem=recv_sems.at[k - 1, a],
                device_id=(px, py, pc), device_id_type=pl.DeviceIdType.MESH))
    return copies


def _adam_math(g, w, m, v):
    nm = ADAM_B1 * m + (1.0 - ADAM_B1) * g
    nv = ADAM_B2 * v + (1.0 - ADAM_B2) * (g * g)
    m_hat = nm / (1.0 - ADAM_B1 ** ADAM_STEP)
    v_hat = nv / (1.0 - ADAM_B2 ** ADAM_STEP)
    return -ADAM_LR * (m_hat / (jnp.sqrt(v_hat) + ADAM_EPS) + ADAM_WD * w), nm, nv


def _slot_sum(ref):
    g = ref[0].astype(F32)
    for s in range(1, N_DEV):
        g = g + ref[s].astype(F32)
    return g


def _adamw_big(parts, w, m, v, *, name):
    _, rws, cols = w.shape
    tr = _tile(rws, (256, 176, 128))

    def kern(p_ref, w_ref, m_ref, v_ref, g_ref, d_ref, nm_ref, nv_ref):
        g = _slot_sum(p_ref)
        g_ref[...] = g
        d_ref[...], nm_ref[...], nv_ref[...] = _adam_math(g, w_ref[...], m_ref[...], v_ref[...])

    spec = pl.BlockSpec((1, tr, cols), lambda i: (0, i, 0))
    return pl.pallas_call(
        kern, out_shape=[jax.ShapeDtypeStruct(w.shape, F32)] * 4, grid=(rws // tr,),
        in_specs=[pl.BlockSpec((N_DEV, 1, tr, cols), lambda i: (0, 0, i, 0)), spec, spec, spec],
        out_specs=[spec] * 4, compiler_params=_params(("parallel",)), name=name)(parts, w, m, v)


def _adamw_small(lora_parts, vec_parts, wide_parts, wmv, *, name):
    names = LORA + VEC + WIDE
    n_l, n = len(LORA), len(names)
    flat = [a for trip in wmv for a in trip]

    def kern(*refs):
        l_refs, vec_ref, wide_ref = refs[:n_l], refs[n_l], refs[n_l + 1]
        in_refs = refs[n_l + 2:n_l + 2 + 3 * n]
        out_refs = refs[n_l + 2 + 3 * n:]
        vec_sum, wide_sum = _slot_sum(vec_ref), _slot_sum(wide_ref)
        for i, nm in enumerate(names):
            w_ref, m_ref, v_ref = in_refs[3 * i:3 * i + 3]
            if i < n_l:
                g = _slot_sum(l_refs[i])
            elif nm in VEC:
                g = vec_sum[i - n_l:i - n_l + 1, :]
            else:
                g = wide_sum[WIDE.index(nm):WIDE.index(nm) + 1, :w_ref.shape[-1]]
            o = out_refs[4 * i:4 * i + 4]
            o[0][...] = g
            o[1][...], o[2][...], o[3][...] = _adam_math(g, w_ref[...], m_ref[...], v_ref[...])

    out_shape = [jax.ShapeDtypeStruct(trip[0].shape, F32) for trip in wmv for _ in range(4)]
    outs = pl.pallas_call(kern, out_shape=out_shape, name=name,
                          compiler_params=pltpu.CompilerParams(vmem_limit_bytes=VMEM_LIMIT))(
        *lora_parts, vec_parts, wide_parts, *flat)
    return [tuple(outs[4 * i:4 * i + 4]) for i in range(n)]


def _to_slots(g, axis):
    _, rws, cols = g.shape
    if axis == 1:
        return g.reshape(N_DEV, 1, rws // N_DEV, cols)
    return g.reshape(1, rws, N_DEV, cols // N_DEV).transpose(2, 0, 1, 3)


def _from_slots(got, axis):
    _, _, rws, cols = got.shape
    if axis == 1:
        return got.reshape(1, N_DEV * rws, cols)
    return got.transpose(1, 2, 0, 3).reshape(1, rws, N_DEV * cols)


def _pad_lanes(a, width):
    return jnp.concatenate([a, jnp.zeros(a.shape[:-1] + (width - a.shape[-1],), a.dtype)], axis=-1)


def kernel(x, norm1_w, w_in, mu_shift, w_up_f, w0_f, w_up_b, w0_b, a_up_f, a0_f, a_up_b, a0_b, g_up, k_k, k_a_f, k_a_b, r_k_f, r_k_b, gn_w, gn_b, conv_w, w_out, norm2_w, w_gate, w_up, w_down, norm_f_w, loss_target, m_norm1_w, m_w_in, m_mu_shift, m_w_up_f, m_w0_f, m_w_up_b, m_w0_b, m_a_up_f, m_a0_f, m_a_up_b, m_a0_b, m_g_up, m_k_k, m_k_a_f, m_k_a_b, m_r_k_f, m_r_k_b, m_gn_w, m_gn_b, m_conv_w, m_w_out, m_norm2_w, m_w_gate, m_w_up, m_w_down, m_norm_f_w, v_norm1_w, v_w_in, v_mu_shift, v_w_up_f, v_w0_f, v_w_up_b, v_w0_b, v_a_up_f, v_a0_f, v_a_up_b, v_a0_b, v_g_up, v_k_k, v_k_a_f, v_k_a_b, v_r_k_f, v_r_k_b, v_gn_w, v_gn_b, v_conv_w, v_w_out, v_norm2_w, v_w_gate, v_w_up, v_w_down, v_norm_f_w):
    local = dict(norm1_w=norm1_w, w_in=w_in, mu_shift=mu_shift, w_up_f=w_up_f, w0_f=w0_f, w_up_b=w_up_b,
                 w0_b=w0_b, a_up_f=a_up_f, a0_f=a0_f, a_up_b=a_up_b, a0_b=a0_b, g_up=g_up, k_k=k_k, k_a_f=k_a_f,
                 k_a_b=k_a_b, r_k_f=r_k_f, r_k_b=r_k_b, gn_w=gn_w, gn_b=gn_b, conv_w=conv_w, w_out=w_out,
                 norm2_w=norm2_w, w_gate=w_gate, w_up=w_up, w_down=w_down, norm_f_w=norm_f_w)
    mom_m = dict(norm1_w=m_norm1_w, w_in=m_w_in, mu_shift=m_mu_shift, w_up_f=m_w_up_f, w0_f=m_w0_f,
                 w_up_b=m_w_up_b, w0_b=m_w0_b, a_up_f=m_a_up_f, a0_f=m_a0_f, a_up_b=m_a_up_b, a0_b=m_a0_b,
                 g_up=m_g_up, k_k=m_k_k, k_a_f=m_k_a_f, k_a_b=m_k_a_b, r_k_f=m_r_k_f, r_k_b=m_r_k_b,
                 gn_w=m_gn_w, gn_b=m_gn_b, conv_w=m_conv_w, w_out=m_w_out, norm2_w=m_norm2_w, w_gate=m_w_gate,
                 w_up=m_w_up, w_down=m_w_down, norm_f_w=m_norm_f_w)
    mom_v = dict(norm1_w=v_norm1_w, w_in=v_w_in, mu_shift=v_mu_shift, w_up_f=v_w_up_f, w0_f=v_w0_f,
                 w_up_b=v_w_up_b, w0_b=v_w0_b, a_up_f=v_a_up_f, a0_f=v_a0_f, a_up_b=v_a_up_b, a0_b=v_a0_b,
                 g_up=v_g_up, k_k=v_k_k, k_a_f=v_k_a_f, k_a_b=v_k_a_b, r_k_f=v_r_k_f, r_k_b=v_r_k_b,
                 gn_w=v_gn_w, gn_b=v_gn_b, conv_w=v_conv_w, w_out=v_w_out, norm2_w=v_norm2_w, w_gate=v_w_gate,
                 w_up=v_w_up, w_down=v_w_down, norm_f_w=v_norm_f_w)

    early = ("w_in",) + LORA
    got = _all_gather([local["w_in"].astype(BF16)] + [local[n] for n in LORA], name="gather")
    full = dict(local)
    full.update({n: _from_slots(a, SHARD_AXIS[n]) for n, a in zip(early, got)})

    loss_part, grad_x, grads, parts = _local_step(x, loss_target, full,
                                                  late={n: local[n].astype(BF16) for n in LATE})

    wide_rows = jnp.concatenate([_pad_lanes(a, WIDE_ROW) for a in [grads[n] for n in WIDE] + [loss_part]]
                                + [jnp.zeros((SUBLANES - len(WIDE) - 1, WIDE_ROW), F32)], axis=0)
    wide_parts, = _exchange([], [wide_rows], name="grad_exchange")
    loss = jnp.sum(wide_parts[:, len(WIDE), 0])
    out = {}
    for n in BIG:
        out[n] = _adamw_big(parts[n], local[n], mom_m[n], mom_v[n], name="adamw_" + n)

    def small_form(n, a):
        if n in LORA:
            return a
        a = a.reshape(1, -1)
        return _pad_lanes(a, WIDE_ROW) if n == "mu_shift" else a

    small = LORA + VEC + WIDE
    res = _adamw_small([parts[n] for n in LORA], parts["vec"], wide_parts,
                       [tuple(small_form(n, d[n]) for d in (local, mom_m, mom_v)) for n in small],
                       name="adamw_small")
    for n, quad in zip(small, res):
        out[n] = tuple(a[..., :local[n].size].reshape(local[n].shape) if n not in LORA else a for a in quad)
    return (loss, grad_x, *[out[n][i] for i in range(4) for n in WEIGHTS])
```

```python
import functools

import jax
import jax.numpy as jnp
from jax import lax
from jax.experimental import pallas as pl
from jax.experimental.pallas import tpu as pltpu

F32 = jnp.float32
BF16 = jnp.bfloat16
HIGHEST = lax.Precision.HIGHEST

N_DEV = 8
D_MODEL = 1024
D_RWKV = 512
D_CONV = 512
HEAD = 64
N_HEAD = D_RWKV // HEAD
D_LORA = 64
D_GATE = 160
D_SHIFTED = 3 * D_RWKV + 2 * D_LORA + D_GATE
XW0, XA0, XG0 = 1536, 1664, 1792
D_SP = 2048
D_INP = D_SP + 3 * D_CONV
LOG_DECAY_SCALE = 0.606531
RMS_EPS = 1e-6
GN_EPS = 64e-5
NORM_EPS = 1e-12
ADAM_LR, ADAM_B1, ADAM_B2, ADAM_EPS, ADAM_WD, ADAM_STEP = 0.001, 0.9, 0.999, 1e-08, 0.01, 10

LANES = 128
SUBLANES = 8
VMEM_LIMIT = 48 * 1024 * 1024
SCAN_CHUNK = 32
SCAN_VMEM_LIMIT = 58 * 1024 * 1024
SCAN_UNROLL = 3
ROW_TILE = 128
WIDE_TILE = 512
RELAYOUT_TILE = 512

BIG = ("w_in", "w_out", "w_gate", "w_up", "w_down")
LORA = ("w_up_f", "w_up_b", "a_up_f", "a_up_b", "g_up", "conv_w")
SHARD_AXIS = {"w_in": 2, "w_out": 1, "w_gate": 2, "w_up": 2, "w_down": 1, "w_up_f": 2, "w_up_b": 2,
              "a_up_f": 2, "a_up_b": 2, "g_up": 2, "conv_w": 2}
VEC = ("w0_f", "w0_b", "a0_f", "a0_b", "k_k", "k_a_f", "k_a_b", "r_k_f", "r_k_b", "gn_w", "gn_b")
WIDE = ("mu_shift", "norm1_w", "norm2_w", "norm_f_w")
WIDE_ROW = 2048
WEIGHTS = ("norm1_w", "w_in", "mu_shift", "w_up_f", "w0_f", "w_up_b", "w0_b", "a_up_f", "a0_f", "a_up_b",
           "a0_b", "g_up", "k_k", "k_a_f", "k_a_b", "r_k_f", "r_k_b", "gn_w", "gn_b", "conv_w", "w_out",
           "norm2_w", "w_gate", "w_up", "w_down", "norm_f_w")


def _params(sem, limit=VMEM_LIMIT):
    return pltpu.CompilerParams(dimension_semantics=sem, vmem_limit_bytes=limit)


def _tile(n, cands):
    for c in cands:
        if n % c == 0:
            return c
    raise ValueError(f"no tile for {n}")


def _mm(a, b, *, ta=False, tb=False, add=None, exchange=None, name):
    (k_dim, m) = a.shape if ta else a.shape[::-1]
    (k2, n) = b.shape[::-1] if tb else b.shape
    assert k_dim == k2, (a.shape, b.shape, ta, tb)
    tm = _tile(m, (1408, 1024, 512, 256, 128))
    tn = _tile(n, (1408, 1024, 896, 512, 256, 128))
    tk = _tile(k_dim, (1408, 1024, 896, 512, 256, 128))
    nk = k_dim // tk
    grid = (m // tm, n // tn, nk)
    dims = (((0 if ta else 1,), (1 if tb else 0,)), ((), ()))
    sliced, whole = exchange or ((), ())
    riders = list(sliced) + list(whole)
    n_x, n_in = len(riders), 2 + (add is not None)

    def kern(*refs):
        a_ref, b_ref = refs[:2]
        add_ref = refs[2] if add is not None else None
        o_ref, acc_ref = refs[n_in + n_x], refs[n_in + 2 * n_x + 1]
        k = pl.program_id(2)
        step = (pl.program_id(0) * grid[1] + pl.program_id(1)) * nk + k

        def copies():
            return _exchange_copies(refs[n_in:n_in + n_x], refs[n_in + n_x + 1:n_in + 2 * n_x + 1], len(sliced),
                                    *refs[n_in + 2 * n_x + 2:])

        if n_x:
            @pl.when(step == 0)
            def _():
                for cp in copies():
                    cp.start()

        @pl.when(k == 0)
        def _():
            acc_ref[...] = jnp.zeros_like(acc_ref)

        acc_ref[...] += lax.dot_general(a_ref[...].astype(BF16), b_ref[...].astype(BF16), dims,
                                        preferred_element_type=F32)

        @pl.when(k == nk - 1)
        def _():
            if add is None:
                o_ref[...] = acc_ref[...]
            else:
                o_ref[...] = acc_ref[...] + add_ref[...]

        if n_x:
            @pl.when(step == grid[0] * grid[1] * nk - 1)
            def _():
                for cp in copies():
                    cp.wait()

    a_spec = (pl.BlockSpec((tk, tm), lambda i, j, k: (k, i)) if ta
              else pl.BlockSpec((tm, tk), lambda i, j, k: (i, k)))
    b_spec = (pl.BlockSpec((tn, tk), lambda i, j, k: (j, k)) if tb
              else pl.BlockSpec((tk, tn), lambda i, j, k: (k, j)))
    o_spec = pl.BlockSpec((tm, tn), lambda i, j, k: (i, j))
    in_specs = [a_spec, b_spec] + ([o_spec] if add is not None else []) + _hbm_specs(n_x)
    args = (a, b) + ((add,) if add is not None else ()) + tuple(riders)
    out = pl.pallas_call(
        kern, out_shape=[jax.ShapeDtypeStruct((m, n), F32)] + _exchange_out_shapes(riders, len(sliced)), grid=grid,
        in_specs=in_specs, out_specs=[o_spec] + _hbm_specs(n_x),
        scratch_shapes=[pltpu.VMEM((tm, tn), F32)] + (_exchange_sems(n_x) if n_x else []),
        compiler_params=_params(("arbitrary",) * 3 if n_x else ("parallel", "parallel", "arbitrary")),
        name=name)(*args)
    return out if n_x else out[0]


def _swiglu(g, u):
    return jax.nn.silu(g) * u


FFN_TN = 256


def _mm_swiglu(h, w_gate, w_up, *, name):
    m, k_dim = h.shape
    n = w_gate.shape[1]
    tm = _tile(m, (1024, 512, 256, 128))

    def kern(h_ref, wg_ref, wu_ref, g_ref, u_ref, f_ref):
        hv = h_ref[...].astype(BF16)
        g = jnp.dot(hv, wg_ref[...].astype(BF16), preferred_element_type=F32)
        u = jnp.dot(hv, wu_ref[...].astype(BF16), preferred_element_type=F32)
        g_ref[...] = g
        u_ref[...] = u
        f_ref[...] = _swiglu(g, u).astype(f_ref.dtype)

    w_spec = pl.BlockSpec((k_dim, FFN_TN), lambda i, j: (0, j))
    o_spec = pl.BlockSpec((tm, FFN_TN), lambda i, j: (i, j))
    return pl.pallas_call(
        kern, out_shape=[jax.ShapeDtypeStruct((m, n), F32)] * 2 + [jax.ShapeDtypeStruct((m, n), BF16)],
        grid=(m // tm, n // FFN_TN), in_specs=[pl.BlockSpec((tm, k_dim), lambda i, j: (i, 0)), w_spec, w_spec],
        out_specs=[o_spec] * 3, compiler_params=_params(("parallel", "parallel")), name=name)(h, w_gate, w_up)


def _mm_swiglu_bwd(dx, w_down, g, u, *, name):
    m, k_dim = dx.shape
    n = w_down.shape[0]
    tm = _tile(m, (1024, 512, 256, 128))

    def kern(dx_ref, w_ref, g_ref, u_ref, dg_ref, du_ref):
        df = lax.dot_general(dx_ref[...].astype(BF16), w_ref[...].astype(BF16), (((1,), (1,)), ((), ())),
                             preferred_element_type=F32)
        _, vjp = jax.vjp(_swiglu, g_ref[...], u_ref[...])
        dg, du = vjp(df)
        dg_ref[...] = dg.astype(dg_ref.dtype)
        du_ref[...] = du.astype(du_ref.dtype)

    o_spec = pl.BlockSpec((tm, FFN_TN), lambda i, j: (i, j))
    return pl.pallas_call(
        kern, out_shape=[jax.ShapeDtypeStruct((m, n), BF16)] * 2, grid=(m // tm, n // FFN_TN),
        in_specs=[pl.BlockSpec((tm, k_dim), lambda i, j: (i, 0)), pl.BlockSpec((FFN_TN, k_dim), lambda i, j: (j, 0)),
                  o_spec, o_spec],
        out_specs=[o_spec] * 2, compiler_params=_params(("parallel", "parallel")), name=name)(dx, w_down, g, u)


def _rowwise(fn, rows, consts, out_rows, out_accs, *, name, tb=ROW_TILE, out_dtype=F32):
    t = (rows[0][0] if isinstance(rows[0], tuple) else rows[0]).shape[0]
    tb = min(tb, t)
    n_r, n_c, n_o, n_a = len(rows), len(consts), len(out_rows), len(out_accs)
    pieces = [w if isinstance(w, (list, tuple)) else [w] for w in out_rows]

    def kern(*refs):
        r_refs = refs[:n_r]
        c_refs = refs[n_r:n_r + n_c]
        o_refs = refs[n_r + n_c:n_r + n_c + n_o]
        a_refs = refs[n_r + n_c + n_o:]
        vals = fn(*[r[...] for r in r_refs], *[c[...] for c in c_refs])
        vals = list(vals) if isinstance(vals, (tuple, list)) else [vals]
        pos = 0
        for o_ref, ws in zip(o_refs, pieces):
            off = 0
            for w in ws:
                o_ref[:, off:off + w] = vals[pos].astype(o_ref.dtype)
                off += w
                pos += 1
        if n_a:
            @pl.when(pl.program_id(0) == 0)
            def _():
                for a_ref in a_refs:
                    a_ref[...] = jnp.zeros_like(a_ref)
            for a_ref, v in zip(a_refs, vals[pos:]):
                a_ref[...] += v

    in_specs, args = [], []
    for r in rows:
        if isinstance(r, tuple):
            arr, blk, w = r
            in_specs.append(pl.BlockSpec((tb, w), functools.partial(lambda i, blk: (i, blk), blk=blk)))
        else:
            arr = r
            in_specs.append(pl.BlockSpec((tb, arr.shape[1]), lambda i: (i, 0)))
        args.append(arr)
    for c in consts:
        in_specs.append(pl.BlockSpec(c.shape, lambda i: (0, 0)))
        args.append(c)
    out_shape = [jax.ShapeDtypeStruct((t, sum(ws)), out_dtype) for ws in pieces]
    out_specs = [pl.BlockSpec((tb, sum(ws)), lambda i: (i, 0)) for ws in pieces]
    for shp in out_accs:
        out_shape.append(jax.ShapeDtypeStruct(shp, F32))
        out_specs.append(pl.BlockSpec(shp, lambda i: (0, 0)))
    res = pl.pallas_call(
        kern, out_shape=out_shape, grid=(t // tb,), in_specs=in_specs, out_specs=out_specs,
        compiler_params=_params(("arbitrary",) if n_a else ("parallel",)), name=name)(*args)
    return res


def _rms(x, w):
    return x * lax.rsqrt(jnp.mean(x * x, axis=-1, keepdims=True) + RMS_EPS) * w


def _seg_sum(x, bd):
    return jnp.concatenate(
        [jnp.dot(x[:, LANES * j:LANES * (j + 1)], bd, precision=HIGHEST, preferred_element_type=F32)
         for j in range(x.shape[1] // LANES)], axis=1)


@jax.custom_vjp
def _seg(x, bd):
    return _seg_sum(x, bd)


_seg.defvjp(lambda x, bd: (_seg_sum(x, bd), bd), lambda bd, ct: (_seg_sum(ct, bd), jnp.zeros_like(bd)))


def _colsum(x):
    return jnp.sum(x, axis=0, keepdims=True)


def _prescan_math(r, k, xw, xa, xg, k_k, w0f, w0b, a0f, a0b, kaf, kab, wupf, wupb, aupf, aupb, gup, bd):
    kkr = k * k_k
    norm = jnp.sqrt(_seg(kkr * kkr, bd))
    kk = kkr / jnp.maximum(norm, NORM_EPS)
    th = jnp.tanh(xw)

    def direction(w0, wup, a0, aup, ka):
        logit = w0 + jnp.dot(th, wup, preferred_element_type=F32)
        w = jnp.exp(-LOG_DECAY_SCALE * jax.nn.sigmoid(logit))
        a = jax.nn.sigmoid(a0 + jnp.dot(xa, aup, preferred_element_type=F32))
        kd = k * (1.0 + (a - 1.0) * ka)
        return w, kd, kk * a

    wf, kdf, bf = direction(w0f, wupf, a0f, aupf, kaf)
    wb, kdb, bb = direction(w0b, wupb, a0b, aupb, kab)
    g = jnp.dot(jax.nn.sigmoid(xg), gup, preferred_element_type=F32)
    return kk, r, wf, wb, bf, bb, kdf, kdb, g


def _postscan_math(y, r, v, kdf, kdb, g, gn_w, gn_b, rkf, rkb, bd):
    mean = _seg(y, bd) * (1.0 / HEAD)
    yc = y - mean
    var = _seg(yc * yc, bd) * (1.0 / HEAD)
    yg = yc * lax.rsqrt(var + GN_EPS) * gn_w + gn_b
    bonus = (_seg(r * kdf * rkf, bd) + _seg(r * kdb * rkb, bd)) * v
    return (yg + bonus) * g


def _halo_specs(width, col_blk, tb, t):
    nb = t // SUBLANES
    step = tb // SUBLANES
    main = pl.BlockSpec((tb, width), lambda i: (i, col_blk))
    prev = pl.BlockSpec((SUBLANES, width), lambda i: (jnp.maximum(i * step - 1, 0), col_blk))
    nxt = pl.BlockSpec((SUBLANES, width), lambda i: (jnp.minimum((i + 1) * step, nb - 1), col_blk))
    return [main, prev, nxt]


def _neighbours(z, prev8, next8, first, last):
    tb = z.shape[0]
    row = lax.broadcasted_iota(jnp.int32, z.shape, 0)
    prow = jnp.where(first, 0.0, prev8[SUBLANES - 1:SUBLANES, :])
    nrow = jnp.where(last, 0.0, next8[0:1, :])
    down = jnp.where(row == 0, prow, pltpu.roll(z, 1, 0))
    up = jnp.where(row == tb - 1, nrow, pltpu.roll(z, tb - 1, 0))
    return down, up


def _shift_conv_fwd(p, mu, conv_w, seq, *, name, tb=ROW_TILE):
    t = p.shape[0]
    per_seq = seq // tb

    def kern(p_ref, pp_ref, pn_ref, mu_ref, cw_ref, pss_ref, oc_ref):
        i = pl.program_id(0)
        first = (i % per_seq) == 0
        last = (i % per_seq) == per_seq - 1
        ps = p_ref[:, :D_SP]
        down, up = _neighbours(ps, pp_ref[:, :D_SP], pn_ref[:, :D_SP], first, last)
        pss_ref[...] = ps + mu_ref[...] * (0.5 * (down + up) - ps)
        gb = p_ref[:, D_SP:D_SP + D_CONV]
        u = p_ref[:, D_SP + D_CONV:D_SP + 2 * D_CONV] * p_ref[:, D_SP + 2 * D_CONV:]
        u_p = pp_ref[:, D_SP + D_CONV:D_SP + 2 * D_CONV] * pp_ref[:, D_SP + 2 * D_CONV:]
        u_n = pn_ref[:, D_SP + D_CONV:D_SP + 2 * D_CONV] * pn_ref[:, D_SP + 2 * D_CONV:]
        udown, uup = _neighbours(u, u_p, u_n, first, last)
        oc_ref[...] = gb * (cw_ref[0:1, :] * udown + cw_ref[1:2, :] * u + cw_ref[2:3, :] * uup)

    return pl.pallas_call(
        kern,
        out_shape=[jax.ShapeDtypeStruct((t, D_SP), F32), jax.ShapeDtypeStruct((t, D_CONV), F32)],
        grid=(t // tb,),
        in_specs=_halo_specs(D_INP, 0, tb, t) + [pl.BlockSpec((1, D_SP), lambda i: (0, 0)),
                                                 pl.BlockSpec((SUBLANES, D_CONV), lambda i: (0, 0))],
        out_specs=[pl.BlockSpec((tb, D_SP), lambda i: (i, 0)), pl.BlockSpec((tb, D_CONV), lambda i: (i, 0))],
        compiler_params=_params(("parallel",)), name=name)(p, p, p, mu, conv_w)


def _shift_conv_bwd(p, d_pss, d_o, mu, conv_w, seq, *, name, tb=ROW_TILE):
    t = p.shape[0]
    per_seq = seq // tb

    def kern(p_ref, pp_ref, pn_ref, d_ref, dp_ref, dn_ref, do_ref, dop_ref, don_ref, mu_ref, cw_ref,
             out_ref, dmu_ref, dcw_ref):
        i = pl.program_id(0)
        first = (i % per_seq) == 0
        last = (i % per_seq) == per_seq - 1

        @pl.when(i == 0)
        def _():
            dmu_ref[...] = jnp.zeros_like(dmu_ref)
            dcw_ref[...] = jnp.zeros_like(dcw_ref)

        mu_v = mu_ref[...]
        ps = p_ref[:, :D_SP]
        down, up = _neighbours(ps, pp_ref[:, :D_SP], pn_ref[:, :D_SP], first, last)
        d = d_ref[...]
        ddown, dup = _neighbours(d, dp_ref[...], dn_ref[...], first, last)
        out_ref[:, :D_SP] = (d - mu_v * d + 0.5 * (mu_v * ddown + mu_v * dup)).astype(out_ref.dtype)
        dmu_ref[...] += _colsum(d * (0.5 * (down + up) - ps))

        def parts(ref):
            return (ref[:, D_SP:D_SP + D_CONV], ref[:, D_SP + D_CONV:D_SP + 2 * D_CONV],
                    ref[:, D_SP + 2 * D_CONV:])

        gb, gc, hh = parts(p_ref)
        gb_p, gc_p, hh_p = parts(pp_ref)
        gb_n, gc_n, hh_n = parts(pn_ref)
        u = gc * hh
        udown, uup = _neighbours(u, gc_p * hh_p, gc_n * hh_n, first, last)
        cw0, cw1, cw2 = cw_ref[0:1, :], cw_ref[1:2, :], cw_ref[2:3, :]
        do = do_ref[...]
        duc = do * gb
        ducdown, ducup = _neighbours(duc, dop_ref[...] * gb_p, don_ref[...] * gb_n, first, last)
        du = cw0 * ducup + cw1 * duc + cw2 * ducdown
        out_ref[:, D_SP:D_SP + D_CONV] = (do * (cw0 * udown + cw1 * u + cw2 * uup)).astype(out_ref.dtype)
        out_ref[:, D_SP + D_CONV:D_SP + 2 * D_CONV] = (du * hh).astype(out_ref.dtype)
        out_ref[:, D_SP + 2 * D_CONV:] = (du * gc).astype(out_ref.dtype)
        dcw_ref[0:1, :] += _colsum(duc * udown)
        dcw_ref[1:2, :] += _colsum(duc * u)
        dcw_ref[2:3, :] += _colsum(duc * uup)

    return pl.pallas_call(
        kern,
        out_shape=[jax.ShapeDtypeStruct((t, D_INP), BF16), jax.ShapeDtypeStruct((1, D_SP), F32),
                   jax.ShapeDtypeStruct((SUBLANES, D_CONV), F32)],
        grid=(t // tb,),
        in_specs=(_halo_specs(D_INP, 0, tb, t) + _halo_specs(D_SP, 0, tb, t) + _halo_specs(D_CONV, 1, tb, t)
                  + [pl.BlockSpec((1, D_SP), lambda i: (0, 0)),
                     pl.BlockSpec((SUBLANES, D_CONV), lambda i: (0, 0))]),
        out_specs=[pl.BlockSpec((tb, D_INP), lambda i: (i, 0)), pl.BlockSpec((1, D_SP), lambda i: (0, 0)),
                   pl.BlockSpec((SUBLANES, D_CONV), lambda i: (0, 0))],
        compiler_params=_params(("arbitrary",)), name=name)(p, p, p, d_pss, d_pss, d_pss, d_o, d_o, d_o, mu, conv_w)


N_CHAIN = 16
N_GROUP = LANES // N_CHAIN
V_HI = HEAD // SUBLANES
G_KK, G_R, G_W, G_B, G_KD = 0, 1, (2, 3), (4, 5), (6, 7)


K_HI = HEAD // SUBLANES


def _tree_sum(terms):
    terms = list(terms)
    while len(terms) > 1:
        terms = [a + b for a, b in zip(terms[::2], terms[1::2])]
    return terms[0]


def _kscan_specs(nc):
    same = lambda c: c
    mirror = lambda c: nc - 1 - c

    def k_spec(fn):
        return pl.BlockSpec((SCAN_CHUNK, HEAD, LANES), lambda c: (fn(c), 0, 0))

    def v_spec(fn):
        return pl.BlockSpec((SCAN_CHUNK, SUBLANES, LANES), lambda c: (fn(c), 0, 0))

    return same, mirror, k_spec, v_spec


ST_SHAPE = (2, K_HI, V_HI, SUBLANES, LANES)


def _lane_group_index():
    lane = lax.broadcasted_iota(jnp.int32, (SUBLANES, LANES), 1)
    return lax.shift_right_logical(lane, jnp.full_like(lane, 4))


def _spread_groups(x, grp):
    rolled = [x] + [pltpu.roll(x, s * N_CHAIN, 1) for s in range(1, N_GROUP)]
    out = []
    for j in range(N_GROUP):
        t = rolled[(0 - j) % N_GROUP]
        for g in range(1, N_GROUP):
            t = jnp.where(grp == g, rolled[(g - j) % N_GROUP], t)
        out.append(t)
    return out


def _gather_groups(tiles, grp):
    total = None
    for s in range(N_GROUP):
        b = tiles[s % N_GROUP]
        for g in range(1, N_GROUP):
            b = jnp.where(grp == g, tiles[(g + s) % N_GROUP], b)
        b = pltpu.roll(b, s * N_CHAIN, 1) if s else b
        total = b if total is None else total + b
    return total


def _lane_group_sum(x):
    return _tree_sum([x] + [pltpu.roll(x, k * N_CHAIN, 1) for k in range(1, N_GROUP)])


def _key_row(x_t, grp, kh):
    r = SUBLANES * grp + kh
    return jnp.broadcast_to(x_t[r:r + 1, :], (SUBLANES, LANES))


def _acc(total, term):
    return term if total is None else total + term


SA_SHAPE = (2, V_HI, SUBLANES, LANES)


def _scan_fwd(xall, v_c, *, gather=(), name):
    steps = xall.shape[0]
    nc = steps // SCAN_CHUNK
    same, mirror, k_spec, v_spec = _kscan_specs(nc)
    last = SCAN_CHUNK - 1
    n_x = len(gather)

    def kern(*refs):
        xf_ref, xb_ref, vf_ref, vb_ref = refs[:4]
        yf_ref, yb_ref, hist_ref, fin_ref, sa_ref = refs[4 + n_x:9 + n_x]
        st_ref = refs[9 + 2 * n_x]
        c = pl.program_id(0)

        def riders():
            return _exchange_copies(refs[4:4 + n_x], refs[9 + n_x:9 + 2 * n_x], 0, *refs[10 + 2 * n_x:])

        @pl.when(c == 0)
        def _():
            st_ref[...] = jnp.zeros_like(st_ref)
            if n_x:
                for cp in riders():
                    cp.start()

        hist_ref[0] = st_ref[...]
        grp = _lane_group_index()

        def body(i, put):
            j = last - i
            for d, (x_t, v_t, y_ref, at) in enumerate(((xf_ref[i], vf_ref[i], yf_ref, i),
                                                       (xb_ref[j], vb_ref[j], yb_ref, j))):
                v_b = _spread_groups(v_t, grp)
                part = [None] * V_HI
                for kh in range(K_HI):
                    kk_r = _key_row(x_t, G_KK, kh)
                    for vh in range(V_HI):
                        part[vh] = _acc(part[vh], hist_ref[i, d, kh, vh] * kk_r)
                sa = [_lane_group_sum(p) for p in part]
                for vh in range(V_HI):
                    sa_ref[i, d, vh] = sa[vh]
                y_p = [None] * V_HI
                for kh in range(K_HI):
                    r_r, w_r = _key_row(x_t, G_R, kh), _key_row(x_t, G_W[d], kh)
                    b_r, kd_r = _key_row(x_t, G_B[d], kh), _key_row(x_t, G_KD[d], kh)
                    for vh in range(V_HI):
                        new = hist_ref[i, d, kh, vh] * w_r - sa[vh] * b_r + v_b[vh] * kd_r
                        put(d, kh, vh, new)
                        y_p[vh] = _acc(y_p[vh], new * r_r)
                y_ref[at] = _gather_groups(y_p, grp)

        def step(i, carry):
            def put(d, kh, vh, val):
                hist_ref[i + 1, d, kh, vh] = val
            body(i, put)
            return carry

        lax.fori_loop(0, last, step, 0, unroll=SCAN_UNROLL)

        def put_carry(d, kh, vh, val):
            st_ref[d, kh, vh] = val

        body(last, put_carry)

        @pl.when(c == nc - 1)
        def _():
            fin_ref[...] = st_ref[...]
            if n_x:
                for cp in riders():
                    cp.wait()

    return pl.pallas_call(
        kern,
        out_shape=[jax.ShapeDtypeStruct((steps, SUBLANES, LANES), F32)] * 2
        + [jax.ShapeDtypeStruct((steps,) + ST_SHAPE, F32), jax.ShapeDtypeStruct(ST_SHAPE, F32),
           jax.ShapeDtypeStruct((steps,) + SA_SHAPE, F32)]
        + _exchange_out_shapes(gather, 0),
        grid=(nc,), in_specs=[k_spec(same), k_spec(mirror), v_spec(same), v_spec(mirror)] + _hbm_specs(n_x),
        out_specs=[v_spec(same), v_spec(mirror),
                   pl.BlockSpec((SCAN_CHUNK,) + ST_SHAPE, lambda c: (c, 0, 0, 0, 0, 0)),
                   pl.BlockSpec(ST_SHAPE, lambda c: (0, 0, 0, 0, 0)),
                   pl.BlockSpec((SCAN_CHUNK,) + SA_SHAPE, lambda c: (c, 0, 0, 0, 0))] + _hbm_specs(n_x),
        scratch_shapes=[pltpu.VMEM(ST_SHAPE, F32)] + (_exchange_sems(n_x) if n_x else []),
        compiler_params=_params(("arbitrary",), SCAN_VMEM_LIMIT), name=name)(xall, xall, v_c, v_c, *gather)


def _scan_bwd(xall, v_c, dy_c, hist, fin, sa, *, exchange=(), name):
    steps = xall.shape[0]
    nc = steps // SCAN_CHUNK
    same, back, k_spec, v_spec = _kscan_specs(nc)
    last = SCAN_CHUNK - 1
    n_x = len(exchange)

    def kern(*refs):
        xf_ref, xb_ref, vf_ref, vb_ref, dyf_ref, dyb_ref, hist_ref, fin_ref, sa_ref = refs[:9]
        gf_ref, gb_ref, dvf_ref, dvb_ref = refs[9 + n_x:13 + n_x]
        ds_ref, after_ref = refs[13 + 2 * n_x:15 + 2 * n_x]
        c = pl.program_id(0)

        def riders():
            return _exchange_copies(refs[9:9 + n_x], refs[13 + n_x:13 + 2 * n_x], n_x, *refs[15 + 2 * n_x:])

        @pl.when(c == 0)
        def _():
            ds_ref[...] = jnp.zeros_like(ds_ref)
            after_ref[...] = fin_ref[...]
            if n_x:
                for cp in riders():
                    cp.start()

        grp = _lane_group_index()
        row = lax.broadcasted_iota(jnp.int32, (SUBLANES, LANES), 0)
        zero = jnp.zeros((SUBLANES, LANES), F32)

        def body(i, after):
            j = last - i
            for d, (x_t, v_t, dy_t, g_ref, dv_ref, at) in enumerate((
                    (xf_ref[i], vf_ref[i], dyf_ref[i], gf_ref, dvf_ref, i),
                    (xb_ref[j], vb_ref[j], dyb_ref[j], gb_ref, dvb_ref, j))):
                v_s, dy_s = _spread_groups(v_t, grp), _spread_groups(dy_t, grp)
                dsa_p, dv_p = [None] * V_HI, [None] * V_HI
                for kh in range(K_HI):
                    r_r = _key_row(x_t, G_R, kh)
                    b_r, kd_r = _key_row(x_t, G_B[d], kh), _key_row(x_t, G_KD[d], kh)
                    for vh in range(V_HI):
                        g = ds_ref[d, kh, vh] + dy_s[vh] * r_r
                        ds_ref[d, kh, vh] = g
                        dsa_p[vh] = _acc(dsa_p[vh], g * b_r)
                        dv_p[vh] = _acc(dv_p[vh], g * kd_r)
                dsa = [-_lane_group_sum(p) for p in dsa_p]
                sa = [sa_ref[i, d, vh] for vh in range(V_HI)]
                dv_ref[at] = _gather_groups(dv_p, grp)
                blocks = {G_KK: zero, G_R: zero, G_W[d]: zero, G_B[d]: zero, G_KD[d]: zero}
                for kh in range(K_HI):
                    w_r, kk_r = _key_row(x_t, G_W[d], kh), _key_row(x_t, G_KK, kh)
                    dkk = dr = dw = db = dkd = None
                    for vh in range(V_HI):
                        g, before = ds_ref[d, kh, vh], hist_ref[i, d, kh, vh]
                        dr = _acc(dr, after(d, kh, vh) * dy_s[vh])
                        dw = _acc(dw, g * before)
                        dkd = _acc(dkd, g * v_s[vh])
                        db = _acc(db, g * sa[vh])
                        dkk = _acc(dkk, before * dsa[vh])
                        ds_ref[d, kh, vh] = g * w_r + dsa[vh] * kk_r
                    for gi, a in ((G_KK, dkk), (G_R, dr), (G_W[d], dw), (G_B[d], -db), (G_KD[d], dkd)):
                        blocks[gi] = jnp.where(row == kh, _colsum(a), blocks[gi])
                for gi in range(N_GROUP):
                    g_ref[at, SUBLANES * gi:SUBLANES * (gi + 1), :] = blocks.get(gi, zero)

        body(last, lambda d, kh, vh: after_ref[d, kh, vh])

        def step(ii, carry):
            i = last - ii
            body(i, lambda d, kh, vh: hist_ref[i + 1, d, kh, vh])
            return carry

        lax.fori_loop(1, SCAN_CHUNK, step, 0, unroll=SCAN_UNROLL)
        after_ref[...] = hist_ref[0]

        if n_x:
            @pl.when(c == nc - 1)
            def _():
                for cp in riders():
                    cp.wait()

    return pl.pallas_call(
        kern,
        out_shape=[jax.ShapeDtypeStruct((steps, HEAD, LANES), F32)] * 2
        + [jax.ShapeDtypeStruct((steps, SUBLANES, LANES), F32)] * 2 + _exchange_out_shapes(exchange, n_x),
        grid=(nc,),
        in_specs=[k_spec(back), k_spec(same), v_spec(back), v_spec(same), v_spec(back), v_spec(same),
                  pl.BlockSpec((SCAN_CHUNK,) + ST_SHAPE, lambda c: (back(c), 0, 0, 0, 0, 0)),
                  pl.BlockSpec(ST_SHAPE, lambda c: (0, 0, 0, 0, 0)),
                  pl.BlockSpec((SCAN_CHUNK,) + SA_SHAPE, lambda c: (back(c), 0, 0, 0, 0))] + _hbm_specs(n_x),
        out_specs=[k_spec(back), k_spec(same), v_spec(back), v_spec(same)] + _hbm_specs(n_x),
        scratch_shapes=[pltpu.VMEM(ST_SHAPE, F32), pltpu.VMEM(ST_SHAPE, F32)]
        + (_exchange_sems(n_x) if n_x else []),
        compiler_params=_params(("arbitrary",), SCAN_VMEM_LIMIT), name=name)(xall, xall, v_c, v_c, dy_c, dy_c, hist, fin, sa,
                                                            *exchange)


def _bf16_pieces(x):
    hi = x.astype(BF16)
    return hi, (x - hi.astype(F32)).astype(BF16)


def _to_key_rows(wide, bsz, seq, *, name):
    assert bsz == 2
    perm = _key_row_maps()
    tt = min(RELAYOUT_TILE, seq)
    per_seq = seq // tt

    def kern(x0_ref, x1_ref, p0_ref, p1_ref, o_ref):
        total = None
        for x_ref, p_ref in ((x0_ref, p0_ref), (x1_ref, p1_ref)):
            for piece in _bf16_pieces(x_ref[...]):
                term = jnp.dot(piece, p_ref[...], preferred_element_type=F32)
                total = term if total is None else total + term
        for r in range(K_HI):
            o_ref[:, r, :] = total[:, LANES * r:LANES * (r + 1)]

    p_spec = pl.BlockSpec((D_RWKV, K_HI * LANES), lambda i, a: (0, 0))
    return pl.pallas_call(
        kern, out_shape=jax.ShapeDtypeStruct((seq, HEAD, LANES), F32), grid=(per_seq, N_GROUP),
        in_specs=[pl.BlockSpec((tt, D_RWKV), lambda i, a: (i, a)),
                  pl.BlockSpec((tt, D_RWKV), lambda i, a: (per_seq + i, a)), p_spec, p_spec],
        out_specs=pl.BlockSpec((tt, K_HI, LANES), lambda i, a: (i, a, 0)),
        compiler_params=_params(("parallel", "parallel")), name=name)(wide, wide, *perm)


def _key_row_maps():
    src = jnp.arange(D_RWKV)
    head, kh, kl = src // HEAD, (src // SUBLANES) % K_HI, src % SUBLANES
    dst = jnp.arange(K_HI * LANES)
    return [((kh[:, None] == dst[None, :] // LANES) & (kl[:, None] == (dst[None, :] // N_CHAIN) % SUBLANES)
             & ((dst[None, :] // N_HEAD) % 2 == b) & (head[:, None] == dst[None, :] % N_HEAD)).astype(BF16)
            for b in range(2)]


def _from_key_rows(g_f, g_b, bsz, seq, *, name):
    assert bsz == 2
    maps = jnp.concatenate([m.T for m in _key_row_maps()], axis=1)
    tt = min(RELAYOUT_TILE, seq)
    per_seq = seq // tt

    def kern(gf_ref, gb_ref, q_ref, o_ref):
        a = pl.program_id(1)
        shared = a <= G_R
        from_f = shared | (a % 2 == G_W[0] % 2)

        def rearranged(g_ref):
            g = jnp.concatenate([g_ref[:, r, :] for r in range(K_HI)], axis=1)
            hi, mid = (jnp.dot(piece, q_ref[...], preferred_element_type=F32) for piece in _bf16_pieces(g))
            both = hi + mid
            return both[:, :D_RWKV], both[:, D_RWKV:]

        @pl.when(from_f)
        def _():
            o_ref[0], o_ref[1] = rearranged(gf_ref)

        @pl.when(jnp.logical_not(from_f))
        def _():
            o_ref[0], o_ref[1] = rearranged(gb_ref)

        @pl.when(shared)
        def _():
            more = rearranged(gb_ref)
            o_ref[0] += more[0]
            o_ref[1] += more[1]

    g_spec = pl.BlockSpec((tt, K_HI, LANES), lambda i, a: (i, a, 0))
    out = pl.pallas_call(
        kern, out_shape=jax.ShapeDtypeStruct((bsz, seq, N_GROUP * D_RWKV), F32), grid=(per_seq, N_GROUP),
        in_specs=[g_spec, g_spec, pl.BlockSpec((K_HI * LANES, bsz * D_RWKV), lambda i, a: (0, 0))],
        out_specs=pl.BlockSpec((bsz, tt, D_RWKV), lambda i, a: (0, i, a)),
        compiler_params=_params(("parallel", "parallel")), name=name)(g_f, g_b, maps)
    return out.reshape(bsz * seq, N_GROUP * D_RWKV)


def _to_value_rows(a, bsz, seq):
    z = a.reshape(bsz, seq, N_HEAD, V_HI, SUBLANES).transpose(1, 4, 3, 0, 2)
    return z.reshape(seq, SUBLANES, LANES)


def _from_value_rows(y, bsz, seq):
    z = y.reshape(seq, SUBLANES, V_HI, bsz, N_HEAD).transpose(3, 0, 4, 2, 1)
    return z.reshape(bsz * seq, D_RWKV)


def _pad_cols(a, segs):
    out, off = [], 0
    for w, wp in segs:
        out.append(a[..., off:off + w])
        if wp > w:
            out.append(jnp.zeros(a.shape[:-1] + (wp - w,), a.dtype))
        off += w
    return jnp.concatenate(out, axis=-1)


def _unpad_cols(a, segs):
    out, off = [], 0
    for w, wp in segs:
        out.append(a[..., off:off + w])
        off += wp
    return jnp.concatenate(out, axis=-1)


P_SEGS = ((3 * D_RWKV, 3 * D_RWKV), (D_LORA, 128), (D_LORA, 128), (D_GATE, 256), (3 * D_CONV, 3 * D_CONV))
S_SEGS = P_SEGS[:4]


def _pad_rows(a, rows):
    return jnp.concatenate([a, jnp.zeros((rows - a.shape[0], a.shape[1]), a.dtype)], axis=0)


LATE = ("w_out", "w_gate", "w_up", "w_down")


def _local_step(x, target, w, late=None):
    bsz, seq, _ = x.shape
    t = bsz * seq
    x2d = x.reshape(t, D_MODEL)
    tg2d = target.reshape(t, D_MODEL)
    row = lambda a: a.reshape(1, -1).astype(F32)

    w_in = _pad_cols(w["w_in"][0], P_SEGS)
    mu = _pad_cols(row(w["mu_shift"]), S_SEGS)
    wupf, wupb, aupf, aupb = (_pad_rows(w[n][0].astype(F32), 128) for n in ("w_up_f", "w_up_b", "a_up_f", "a_up_b"))
    gup = _pad_rows(w["g_up"][0].astype(F32), 256)
    conv_w = _pad_rows(w["conv_w"][0].astype(F32), SUBLANES)
    norm1, norm2, normf = row(w["norm1_w"]), row(w["norm2_w"]), row(w["norm_f_w"])
    vec = {n: row(w[n]) for n in VEC}
    head_of = jnp.arange(LANES) // HEAD
    bd = (head_of[:, None] == head_of[None, :]).astype(F32)
    pre_consts = [vec["k_k"], vec["w0_f"], vec["w0_b"], vec["a0_f"], vec["a0_b"], vec["k_a_f"], vec["k_a_b"],
                  wupf, wupb, aupf, aupb, gup, bd]
    post_consts = [vec["gn_w"], vec["gn_b"], vec["r_k_f"], vec["r_k_b"], bd]

    h1, = _rowwise(_rms, [x2d], [norm1], [D_MODEL], [], name="rms1_fwd", out_dtype=BF16, tb=WIDE_TILE)
    p = _mm(h1, w_in, name="mm_in")
    pss, oconv = _shift_conv_fwd(p, mu, conv_w, seq, name="shift_conv_fwd", tb=min(2 * ROW_TILE, seq))
    pre_rows = [(pss, 0, 512), (pss, 1, 512), (pss, XW0 // 128, 128), (pss, XA0 // 128, 128), (pss, XG0 // 256, 256)]
    sc, g = _rowwise(_prescan_math, pre_rows, pre_consts, [[D_RWKV] * N_GROUP, D_RWKV], [], name="prescan_fwd",
                     tb=2 * ROW_TILE)
    xall = _to_key_rows(sc, bsz, seq, name="to_key_rows")
    v_l = _to_value_rows(pss[:, 2 * D_RWKV:3 * D_RWKV], bsz, seq)
    y_f, y_b, hist, fin, sa, *gathered = _scan_fwd(xall, v_l, gather=[late[n] for n in LATE] if late else (),
                                                   name="scan_fwd")
    w_out, w_gate, w_up, w_down = (
        (_from_slots(a, SHARD_AXIS[n]) if late else w[n])[0] for n, a in zip(LATE, gathered or LATE))
    y = _from_value_rows(y_f + y_b, bsz, seq)
    post_rows = [y, (pss, 0, 512), (pss, 2, 512), (sc, G_KD[0], 512), (sc, G_KD[1], 512), g]

    def post_fwd(y_, r_, v_, kdf_, kdb_, g_, oc_, *consts):
        return _postscan_math(y_, r_, v_, kdf_, kdb_, g_, *consts), oc_

    o, = _rowwise(post_fwd, post_rows + [oconv], post_consts, [[D_RWKV, D_CONV]], [], name="postscan_fwd",
                  out_dtype=BF16, tb=2 * ROW_TILE)
    x1 = _mm(o, w_out, add=x2d, name="mm_out")
    h2, = _rowwise(_rms, [x1], [norm2], [D_MODEL], [], name="rms2_fwd", out_dtype=BF16, tb=WIDE_TILE)
    gg, uu, ff = _mm_swiglu(h2, w_gate, w_up, name="mm_gate_up")
    x2 = _mm(ff, w_down, add=x1, name="mm_down")

    def final(x_, tg_, wn_):
        yo, vjp = jax.vjp(_rms, x_, wn_)
        err = yo - tg_
        dx_, dwn_ = vjp(err * (1.0 / D_MODEL))
        part = jnp.sum(jnp.sum(err * err, axis=1, keepdims=True), axis=0, keepdims=True) * (0.5 / D_MODEL)
        return dx_, part + jnp.zeros((1, LANES), F32), dwn_

    dx2, loss_acc, d_normf = _rowwise(final, [x2, tg2d], [normf], [D_MODEL], [(1, LANES), (1, D_MODEL)],
                                      name="loss_head", tb=WIDE_TILE)
    dgg, duu = _mm_swiglu_bwd(dx2, w_down, gg, uu, name="mm_down_dx")
    g_w_down = _mm(ff, dx2, ta=True, name="mm_down_dw")
    dh2 = _mm(dgg, w_gate, tb=True, name="mm_gate_dx")
    dh2 = _mm(duu, w_up, tb=True, add=dh2, name="mm_up_dx")
    g_w_gate = _mm(h2, dgg, ta=True, name="mm_gate_dw")
    g_w_up = _mm(h2, duu, ta=True, name="mm_up_dw")

    def rms_bwd(x_, dh_, dres_, wn_):
        _, vjp = jax.vjp(_rms, x_, wn_)
        dx_, dwn_ = vjp(dh_)
        return dx_ + dres_, dwn_

    dx1, d_norm2 = _rowwise(rms_bwd, [x1, dh2, dx2], [norm2], [D_MODEL], [(1, D_MODEL)], name="rms2_bwd", tb=WIDE_TILE)
    do = _mm(dx1, w_out, tb=True, name="mm_out_dx")
    g_w_out = _mm(o, dx1, ta=True, name="mm_out_dw")

    def post_bwd(y_, r_, v_, kdf_, kdb_, g_, do_, *consts):
        _, vjp = jax.vjp(lambda *a: _postscan_math(*a, consts[4]), y_, r_, v_, kdf_, kdb_, g_, *consts[:4])
        return vjp(do_)

    (dy, dr_c, dv_c, dkdf_c, dkdb_c, dg, d_gn_w, d_gn_b, d_rkf, d_rkb) = _rowwise(
        post_bwd, post_rows + [(do, 0, 512)], post_consts, [D_RWKV] * 6, [(1, D_RWKV)] * 4, name="postscan_bwd",
        tb=2 * ROW_TILE)
    dy_l = _to_value_rows(dy, bsz, seq)
    late_grads = {"w_out": g_w_out[None], "w_gate": g_w_gate[None], "w_up": g_w_up[None], "w_down": g_w_down[None]}
    g_f, g_b, dv_f, dv_b, *late_parts = _scan_bwd(
        xall, v_l, dy_l, hist, fin, sa, name="scan_bwd",
        exchange=[_to_slots(late_grads[n], SHARD_AXIS[n]).astype(BF16) for n in LATE] if late else ())
    dsc = _from_key_rows(g_f, g_b, bsz, seq, name="from_key_rows")
    dv_s = _from_value_rows(dv_f + dv_b, bsz, seq)

    def pre_bwd(r_, k_, xw_, xa_, xg_, dkk_, dr_s, dwf_, dwb_, dbf_, dbb_, dkdf_s, dkdb_s,
                dr_c_, dv_c_, dv_s_, dkdf_c_, dkdb_c_, dg_, *consts):
        _, vjp = jax.vjp(lambda *a: _prescan_math(*a, consts[-1]), r_, k_, xw_, xa_, xg_, *consts[:-1])
        grads = vjp((dkk_, dr_s + dr_c_, dwf_, dwb_, dbf_, dbb_, dkdf_s + dkdf_c_, dkdb_s + dkdb_c_, dg_))
        dr_, dk_, dxw_, dxa_, dxg_ = grads[:5]
        return (dr_, dk_, dv_c_ + dv_s_, dxw_, dxa_, dxg_) + tuple(grads[5:])

    pre_b_rows = (pre_rows + [(dsc, j, 512) for j in range(N_GROUP)]
                  + [dr_c, dv_c, dv_s, dkdf_c, dkdb_c, dg])
    pre_b = _rowwise(pre_bwd, pre_b_rows, pre_consts, [[512, 512, 512, 128, 128, 256]],
                     [(1, D_RWKV)] * 7 + [(128, D_RWKV)] * 4 + [(256, D_RWKV)], name="prescan_bwd")
    d_pss = pre_b[0]
    d_kk_, d_w0f, d_w0b, d_a0f, d_a0b, d_kaf, d_kab, d_wupf, d_wupb, d_aupf, d_aupb, d_gup = pre_b[1:]
    dp, d_mu, d_conv = _shift_conv_bwd(p, d_pss, do, mu, conv_w, seq, name="shift_conv_bwd",
                                       tb=min(2 * ROW_TILE, seq))
    g_w_in = _mm(h1, dp, ta=True, name="mm_in_dw")
    grads = {
        "w_in": _unpad_cols(g_w_in, P_SEGS)[None], "mu_shift": _unpad_cols(d_mu, S_SEGS),
        "w_up_f": d_wupf[None, :D_LORA], "w0_f": d_w0f, "w_up_b": d_wupb[None, :D_LORA], "w0_b": d_w0b,
        "a_up_f": d_aupf[None, :D_LORA], "a0_f": d_a0f, "a_up_b": d_aupb[None, :D_LORA], "a0_b": d_a0b,
        "g_up": d_gup[None, :D_GATE], "k_k": d_kk_, "k_a_f": d_kaf, "k_a_b": d_kab,
        "r_k_f": d_rkf, "r_k_b": d_rkb, "gn_w": d_gn_w, "gn_b": d_gn_b, "conv_w": d_conv[None, :3],
        "w_out": g_w_out[None], "norm2_w": d_norm2, "w_gate": g_w_gate[None], "w_up": g_w_up[None],
        "w_down": g_w_down[None], "norm_f_w": d_normf,
    }
    early = ("w_in",) + LORA
    parts = dict(zip(LATE, late_parts))
    if late:
        vec_rows = jnp.concatenate([grads[n] for n in VEC] + [jnp.zeros((16 - len(VEC), D_RWKV), F32)], axis=0)
        slots = [_to_slots(grads[n], SHARD_AXIS[n]).astype(BF16 if n in BIG else F32) for n in early]
        dh1, *recv = _mm(dp, w_in, tb=True, exchange=(slots, [vec_rows]), name="mm_in_dx")
        parts.update(zip(early + ("vec",), recv))
    else:
        dh1 = _mm(dp, w_in, tb=True, name="mm_in_dx")
    dx, grads["norm1_w"] = _rowwise(rms_bwd, [x2d, dh1, dx1], [norm1], [D_MODEL], [(1, D_MODEL)], name="rms1_bwd",
                                    tb=WIDE_TILE)
    return loss_acc, dx.reshape(bsz, seq, D_MODEL), grads, parts


def _hbm_specs(n):
    return [pl.BlockSpec(memory_space=pl.ANY)] * n


def _all_gather(arrs, *, name):
    n = len(arrs)

    def body(*refs):
        x_refs, out_refs = refs[:n], refs[n:2 * n]
        send_sems, recv_sems, local_sems = refs[2 * n:]
        x, y, c = lax.axis_index("x"), lax.axis_index("y"), lax.axis_index("c")
        me, sibling = (x, y, c), (x, y, 1 - c)
        chips = [(1 - x, y), (x, 1 - y), (1 - x, 1 - y)]

        def slot(a, px, py, pc):
            return out_refs[a].at[4 * px + 2 * py + pc]

        def copy(a, k, block, to, src=None):
            return pltpu.make_async_remote_copy(
                src_ref=slot(a, *block) if src is None else src, dst_ref=slot(a, *block),
                send_sem=send_sems.at[k, a], recv_sem=recv_sems.at[k, a],
                device_id=to, device_id_type=pl.DeviceIdType.MESH)

        mine = [pltpu.make_async_copy(x_refs[a], slot(a, *me), local_sems.at[a]) for a in range(n)]
        for cp in mine:
            cp.start()
        first = []
        for a in range(n):
            first.append(copy(a, 0, me, sibling, src=x_refs[a]))
            first += [copy(a, 1 + j, me, (*chip, c), src=x_refs[a]) for j, chip in enumerate(chips)]
        for cp in first:
            cp.start()
        passed = []
        for j, chip in enumerate(chips):
            for a in range(n):
                copy(a, 1 + j, (*chip, c), me).wait_recv()
                cp = copy(a, 4 + j, (*chip, c), sibling)
                cp.start()
                passed.append(cp)
        for a in range(n):
            copy(a, 0, sibling, me).wait_recv()
            for j, chip in enumerate(chips):
                copy(a, 4 + j, (*chip, 1 - c), me).wait_recv()
        for cp in first + passed:
            cp.wait_send()
        for cp in mine:
            cp.wait()

    return pl.pallas_call(
        body, out_shape=[jax.ShapeDtypeStruct((N_DEV,) + a.shape, a.dtype) for a in arrs],
        in_specs=_hbm_specs(n), out_specs=_hbm_specs(n),
        scratch_shapes=[pltpu.SemaphoreType.DMA((7, n)), pltpu.SemaphoreType.DMA((7, n)),
                        pltpu.SemaphoreType.DMA((n,))],
        name=name)(*arrs)


def _exchange(sliced, whole, *, name):
    arrs = list(sliced) + list(whole)
    n, n_sliced = len(arrs), len(sliced)

    def body(*refs):
        copies = _exchange_copies(refs[:n], refs[n:2 * n], n_sliced, *refs[2 * n:])
        for cp in copies:
            cp.start()
        for cp in copies:
            cp.wait()

    return pl.pallas_call(
        body, out_shape=_exchange_out_shapes(arrs, n_sliced), in_specs=_hbm_specs(n), out_specs=_hbm_specs(n),
        scratch_shapes=_exchange_sems(n), name=name)(*arrs)


def _exchange_out_shapes(arrs, n_sliced):
    return [jax.ShapeDtypeStruct(a.shape if i < n_sliced else (N_DEV,) + a.shape, a.dtype)
            for i, a in enumerate(arrs)]


def _exchange_sems(n):
    return [pltpu.SemaphoreType.DMA((7, n)), pltpu.SemaphoreType.DMA((7, n)), pltpu.SemaphoreType.DMA((n,))]


def _exchange_copies(in_refs, out_refs, n_sliced, send_sems, recv_sems, local_sems):
    n = len(in_refs)
    x, y, c = lax.axis_index("x"), lax.axis_index("y"), lax.axis_index("c")
    me = 4 * x + 2 * y + c

    def src(a, dev):
        return in_refs[a].at[dev] if a < n_sliced else in_refs[a]

    copies = [pltpu.make_async_copy(src(a, me), out_refs[a].at[me], local_sems.at[a]) for a in range(n)]
    for k in range(1, N_DEV):
        px = 1 - x if k & 4 else x
        py = 1 - y if k & 2 else y
        pc = 1 - c if k & 1 else c
        for a in range(n):
            copies.append(pltpu.make_async_remote_copy(
                src_ref=src(a, 4 * px + 2 * py + pc), dst_ref=out_refs[a].at[me],
                send_sem=send_sems.at[k - 1, a], recv_sem=recv_sems.at[k - 1, a],
                device_id=(px, py, pc), device_id_type=pl.DeviceIdType.MESH))
    return copies


def _adam_math(g, w, m, v):
    nm = ADAM_B1 * m + (1.0 - ADAM_B1) * g
    nv = ADAM_B2 * v + (1.0 - ADAM_B2) * (g * g)
    m_hat = nm / (1.0 - ADAM_B1 ** ADAM_STEP)
    v_hat = nv / (1.0 - ADAM_B2 ** ADAM_STEP)
    return -ADAM_LR * (m_hat / (jnp.sqrt(v_hat) + ADAM_EPS) + ADAM_WD * w), nm, nv


def _slot_sum(ref):
    g = ref[0].astype(F32)
    for s in range(1, N_DEV):
        g = g + ref[s].astype(F32)
    return g


def _adamw_big(parts, w, m, v, *, name):
    _, rws, cols = w.shape
    tr = _tile(rws, (256, 176, 128))

    def kern(p_ref, w_ref, m_ref, v_ref, g_ref, d_ref, nm_ref, nv_ref):
        g = _slot_sum(p_ref)
        g_ref[...] = g
        d_ref[...], nm_ref[...], nv_ref[...] = _adam_math(g, w_ref[...], m_ref[...], v_ref[...])

    spec = pl.BlockSpec((1, tr, cols), lambda i: (0, i, 0))
    return pl.pallas_call(
        kern, out_shape=[jax.ShapeDtypeStruct(w.shape, F32)] * 4, grid=(rws // tr,),
        in_specs=[pl.BlockSpec((N_DEV, 1, tr, cols), lambda i: (0, 0, i, 0)), spec, spec, spec],
        out_specs=[spec] * 4, compiler_params=_params(("parallel",)), name=name)(parts, w, m, v)


def _adamw_small(lora_parts, vec_parts, wide_parts, wmv, *, name):
    names = LORA + VEC + WIDE
    n_l, n = len(LORA), len(names)
    flat = [a for trip in wmv for a in trip]

    def kern(*refs):
        l_refs, vec_ref, wide_ref = refs[:n_l], refs[n_l], refs[n_l + 1]
        in_refs = refs[n_l + 2:n_l + 2 + 3 * n]
        out_refs = refs[n_l + 2 + 3 * n:]
        vec_sum, wide_sum = _slot_sum(vec_ref), _slot_sum(wide_ref)
        for i, nm in enumerate(names):
            w_ref, m_ref, v_ref = in_refs[3 * i:3 * i + 3]
            if i < n_l:
                g = _slot_sum(l_refs[i])
            elif nm in VEC:
                g = vec_sum[i - n_l:i - n_l + 1, :]
            else:
                g = wide_sum[WIDE.index(nm):WIDE.index(nm) + 1, :w_ref.shape[-1]]
            o = out_refs[4 * i:4 * i + 4]
            o[0][...] = g
            o[1][...], o[2][...], o[3][...] = _adam_math(g, w_ref[...], m_ref[...], v_ref[...])

    out_shape = [jax.ShapeDtypeStruct(trip[0].shape, F32) for trip in wmv for _ in range(4)]
    outs = pl.pallas_call(kern, out_shape=out_shape, name=name,
                          compiler_params=pltpu.CompilerParams(vmem_limit_bytes=VMEM_LIMIT))(
        *lora_parts, vec_parts, wide_parts, *flat)
    return [tuple(outs[4 * i:4 * i + 4]) for i in range(n)]


def _to_slots(g, axis):
    _, rws, cols = g.shape
    if axis == 1:
        return g.reshape(N_DEV, 1, rws // N_DEV, cols)
    return g.reshape(1, rws, N_DEV, cols // N_DEV).transpose(2, 0, 1, 3)


def _from_slots(got, axis):
    _, _, rws, cols = got.shape
    if axis == 1:
        return got.reshape(1, N_DEV * rws, cols)
    return got.transpose(1, 2, 0, 3).reshape(1, rws, N_DEV * cols)


def _pad_lanes(a, width):
    return jnp.concatenate([a, jnp.zeros(a.shape[:-1] + (width - a.shape[-1],), a.dtype)], axis=-1)


def kernel(x, norm1_w, w_in, mu_shift, w_up_f, w0_f, w_up_b, w0_b, a_up_f, a0_f, a_up_b, a0_b, g_up, k_k, k_a_f, k_a_b, r_k_f, r_k_b, gn_w, gn_b, conv_w, w_out, norm2_w, w_gate, w_up, w_down, norm_f_w, loss_target, m_norm1_w, m_w_in, m_mu_shift, m_w_up_f, m_w0_f, m_w_up_b, m_w0_b, m_a_up_f, m_a0_f, m_a_up_b, m_a0_b, m_g_up, m_k_k, m_k_a_f, m_k_a_b, m_r_k_f, m_r_k_b, m_gn_w, m_gn_b, m_conv_w, m_w_out, m_norm2_w, m_w_gate, m_w_up, m_w_down, m_norm_f_w, v_norm1_w, v_w_in, v_mu_shift, v_w_up_f, v_w0_f, v_w_up_b, v_w0_b, v_a_up_f, v_a0_f, v_a_up_b, v_a0_b, v_g_up, v_k_k, v_k_a_f, v_k_a_b, v_r_k_f, v_r_k_b, v_gn_w, v_gn_b, v_conv_w, v_w_out, v_norm2_w, v_w_gate, v_w_up, v_w_down, v_norm_f_w):
    local = dict(norm1_w=norm1_w, w_in=w_in, mu_shift=mu_shift, w_up_f=w_up_f, w0_f=w0_f, w_up_b=w_up_b,
                 w0_b=w0_b, a_up_f=a_up_f, a0_f=a0_f, a_up_b=a_up_b, a0_b=a0_b, g_up=g_up, k_k=k_k, k_a_f=k_a_f,
                 k_a_b=k_a_b, r_k_f=r_k_f, r_k_b=r_k_b, gn_w=gn_w, gn_b=gn_b, conv_w=conv_w, w_out=w_out,
                 norm2_w=norm2_w, w_gate=w_gate, w_up=w_up, w_down=w_down, norm_f_w=norm_f_w)
    mom_m = dict(norm1_w=m_norm1_w, w_in=m_w_in, mu_shift=m_mu_shift, w_up_f=m_w_up_f, w0_f=m_w0_f,
                 w_up_b=m_w_up_b, w0_b=m_w0_b, a_up_f=m_a_up_f, a0_f=m_a0_f, a_up_b=m_a_up_b, a0_b=m_a0_b,
                 g_up=m_g_up, k_k=m_k_k, k_a_f=m_k_a_f, k_a_b=m_k_a_b, r_k_f=m_r_k_f, r_k_b=m_r_k_b,
                 gn_w=m_gn_w, gn_b=m_gn_b, conv_w=m_conv_w, w_out=m_w_out, norm2_w=m_norm2_w, w_gate=m_w_gate,
                 w_up=m_w_up, w_down=m_w_down, norm_f_w=m_norm_f_w)
    mom_v = dict(norm1_w=v_norm1_w, w_in=v_w_in, mu_shift=v_mu_shift, w_up_f=v_w_up_f, w0_f=v_w0_f,
                 w_up_b=v_w_up_b, w0_b=v_w0_b, a_up_f=v_a_up_f, a0_f=v_a0_f, a_up_b=v_a_up_b, a0_b=v_a0_b,
                 g_up=v_g_up, k_k=v_k_k, k_a_f=v_k_a_f, k_a_b=v_k_a_b, r_k_f=v_r_k_f, r_k_b=v_r_k_b,
                 gn_w=v_gn_w, gn_b=v_gn_b, conv_w=v_conv_w, w_out=v_w_out, norm2_w=v_norm2_w, w_gate=v_w_gate,
                 w_up=v_w_up, w_down=v_w_down, norm_f_w=v_norm_f_w)

    early = ("w_in",) + LORA
    got = _all_gather([local["w_in"].astype(BF16)] + [local[n] for n in LORA], name="gather")
    full = dict(local)
    full.update({n: _from_slots(a, SHARD_AXIS[n]) for n, a in zip(early, got)})

    loss_part, grad_x, grads, parts = _local_step(x, loss_target, full,
                                                  late={n: local[n].astype(BF16) for n in LATE})

    wide_rows = jnp.concatenate([_pad_lanes(a, WIDE_ROW) for a in [grads[n] for n in WIDE] + [loss_part]]
                                + [jnp.zeros((SUBLANES - len(WIDE) - 1, WIDE_ROW), F32)], axis=0)
    wide_parts, = _exchange([], [wide_rows], name="grad_exchange")
    loss = jnp.sum(wide_parts[:, len(WIDE), 0])
    out = {}
    for n in BIG:
        out[n] = _adamw_big(parts[n], local[n], mom_m[n], mom_v[n], name="adamw_" + n)

    def small_form(n, a):
        if n in LORA:
            return a
        a = a.reshape(1, -1)
        return _pad_lanes(a, WIDE_ROW) if n == "mu_shift" else a

    small = LORA + VEC + WIDE
    res = _adamw_small([parts[n] for n in LORA], parts["vec"], wide_parts,
                       [tuple(small_form(n, d[n]) for d in (local, mom_m, mom_v)) for n in small],
                       name="adamw_small")
    for n, quad in zip(small, res):
        out[n] = tuple(a[..., :local[n].size].reshape(local[n].shape) if n not in LORA else a for a in quad)
    return (loss, grad_x, *[out[n][i] for i in range(4) for n in WEIGHTS])
```

```python
import functools

import jax
import jax.numpy as jnp
from jax import lax
from jax.experimental import pallas as pl
from jax.experimental.pallas import tpu as pltpu

F32 = jnp.float32
BF16 = jnp.bfloat16
HIGHEST = lax.Precision.HIGHEST

N_DEV = 8
D_MODEL = 1024
D_RWKV = 512
D_CONV = 512
HEAD = 64
N_HEAD = D_RWKV // HEAD
D_LORA = 64
D_GATE = 160
D_SHIFTED = 3 * D_RWKV + 2 * D_LORA + D_GATE
XW0, XA0, XG0 = 1536, 1664, 1792
D_SP = 2048
D_INP = D_SP + 3 * D_CONV
LOG_DECAY_SCALE = 0.606531
RMS_EPS = 1e-6
GN_EPS = 64e-5
NORM_EPS = 1e-12
ADAM_LR, ADAM_B1, ADAM_B2, ADAM_EPS, ADAM_WD, ADAM_STEP = 0.001, 0.9, 0.999, 1e-08, 0.01, 10

LANES = 128
SUBLANES = 8
VMEM_LIMIT = 48 * 1024 * 1024
SCAN_CHUNK = 32
SCAN_VMEM_LIMIT = 58 * 1024 * 1024
SCAN_UNROLL = 3
ROW_TILE = 128
WIDE_TILE = 512
RELAYOUT_TILE = 512

BIG = ("w_in", "w_out", "w_gate", "w_up", "w_down")
LORA = ("w_up_f", "w_up_b", "a_up_f", "a_up_b", "g_up", "conv_w")
SHARD_AXIS = {"w_in": 2, "w_out": 1, "w_gate": 2, "w_up": 2, "w_down": 1, "w_up_f": 2, "w_up_b": 2,
              "a_up_f": 2, "a_up_b": 2, "g_up": 2, "conv_w": 2}
VEC = ("w0_f", "w0_b", "a0_f", "a0_b", "k_k", "k_a_f", "k_a_b", "r_k_f", "r_k_b", "gn_w", "gn_b")
WIDE = ("mu_shift", "norm1_w", "norm2_w", "norm_f_w")
WIDE_ROW = 2048
WEIGHTS = ("norm1_w", "w_in", "mu_shift", "w_up_f", "w0_f", "w_up_b", "w0_b", "a_up_f", "a0_f", "a_up_b",
           "a0_b", "g_up", "k_k", "k_a_f", "k_a_b", "r_k_f", "r_k_b", "gn_w", "gn_b", "conv_w", "w_out",
           "norm2_w", "w_gate", "w_up", "w_down", "norm_f_w")


def _params(sem, limit=VMEM_LIMIT):
    return pltpu.CompilerParams(dimension_semantics=sem, vmem_limit_bytes=limit)


def _tile(n, cands):
    for c in cands:
        if n % c == 0:
            return c
    raise ValueError(f"no tile for {n}")


def _mm(a, b, *, ta=False, tb=False, add=None, exchange=None, name):
    (k_dim, m) = a.shape if ta else a.shape[::-1]
    (k2, n) = b.shape[::-1] if tb else b.shape
    assert k_dim == k2, (a.shape, b.shape, ta, tb)
    tm = _tile(m, (1408, 1024, 512, 256, 128))
    tn = _tile(n, (1408, 1024, 896, 512, 256, 128))
    tk = _tile(k_dim, (1408, 1024, 896, 512, 256, 128))
    nk = k_dim // tk
    grid = (m // tm, n // tn, nk)
    dims = (((0 if ta else 1,), (1 if tb else 0,)), ((), ()))
    sliced, whole = exchange or ((), ())
    riders = list(sliced) + list(whole)
    n_x, n_in = len(riders), 2 + (add is not None)

    def kern(*refs):
        a_ref, b_ref = refs[:2]
        add_ref = refs[2] if add is not None else None
        o_ref, acc_ref = refs[n_in + n_x], refs[n_in + 2 * n_x + 1]
        k = pl.program_id(2)
        step = (pl.program_id(0) * grid[1] + pl.program_id(1)) * nk + k

        def copies():
            return _exchange_copies(refs[n_in:n_in + n_x], refs[n_in + n_x + 1:n_in + 2 * n_x + 1], len(sliced),
                                    *refs[n_in + 2 * n_x + 2:])

        if n_x:
            @pl.when(step == 0)
            def _():
                for cp in copies():
                    cp.start()

        @pl.when(k == 0)
        def _():
            acc_ref[...] = jnp.zeros_like(acc_ref)

        acc_ref[...] += lax.dot_general(a_ref[...].astype(BF16), b_ref[...].astype(BF16), dims,
                                        preferred_element_type=F32)

        @pl.when(k == nk - 1)
        def _():
            if add is None:
                o_ref[...] = acc_ref[...]
            else:
                o_ref[...] = acc_ref[...] + add_ref[...]

        if n_x:
            @pl.when(step == grid[0] * grid[1] * nk - 1)
            def _():
                for cp in copies():
                    cp.wait()

    a_spec = (pl.BlockSpec((tk, tm), lambda i, j, k: (k, i)) if ta
              else pl.BlockSpec((tm, tk), lambda i, j, k: (i, k)))
    b_spec = (pl.BlockSpec((tn, tk), lambda i, j, k: (j, k)) if tb
              else pl.BlockSpec((tk, tn), lambda i, j, k: (k, j)))
    o_spec = pl.BlockSpec((tm, tn), lambda i, j, k: (i, j))
    in_specs = [a_spec, b_spec] + ([o_spec] if add is not None else []) + _hbm_specs(n_x)
    args = (a, b) + ((add,) if add is not None else ()) + tuple(riders)
    out = pl.pallas_call(
        kern, out_shape=[jax.ShapeDtypeStruct((m, n), F32)] + _exchange_out_shapes(riders, len(sliced)), grid=grid,
        in_specs=in_specs, out_specs=[o_spec] + _hbm_specs(n_x),
        scratch_shapes=[pltpu.VMEM((tm, tn), F32)] + (_exchange_sems(n_x) if n_x else []),
        compiler_params=_params(("arbitrary",) * 3 if n_x else ("parallel", "parallel", "arbitrary")),
        name=name)(*args)
    return out if n_x else out[0]


def _swiglu(g, u):
    return jax.nn.silu(g) * u


FFN_TN = 256


def _mm_swiglu(h, w_gate, w_up, *, name):
    m, k_dim = h.shape
    n = w_gate.shape[1]
    tm = _tile(m, (1024, 512, 256, 128))

    def kern(h_ref, wg_ref, wu_ref, g_ref, u_ref, f_ref):
        hv = h_ref[...].astype(BF16)
        g = jnp.dot(hv, wg_ref[...].astype(BF16), preferred_element_type=F32)
        u = jnp.dot(hv, wu_ref[...].astype(BF16), preferred_element_type=F32)
        g_ref[...] = g
        u_ref[...] = u
        f_ref[...] = _swiglu(g, u).astype(f_ref.dtype)

    w_spec = pl.BlockSpec((k_dim, FFN_TN), lambda i, j: (0, j))
    o_spec = pl.BlockSpec((tm, FFN_TN), lambda i, j: (i, j))
    return pl.pallas_call(
        kern, out_shape=[jax.ShapeDtypeStruct((m, n), F32)] * 2 + [jax.ShapeDtypeStruct((m, n), BF16)],
        grid=(m // tm, n // FFN_TN), in_specs=[pl.BlockSpec((tm, k_dim), lambda i, j: (i, 0)), w_spec, w_spec],
        out_specs=[o_spec] * 3, compiler_params=_params(("parallel", "parallel")), name=name)(h, w_gate, w_up)


def _mm_swiglu_bwd(dx, w_down, g, u, *, name):
    m, k_dim = dx.shape
    n = w_down.shape[0]
    tm = _tile(m, (1024, 512, 256, 128))

    def kern(dx_ref, w_ref, g_ref, u_ref, dg_ref, du_ref):
        df = lax.dot_general(dx_ref[...].astype(BF16), w_ref[...].astype(BF16), (((1,), (1,)), ((), ())),
                             preferred_element_type=F32)
        _, vjp = jax.vjp(_swiglu, g_ref[...], u_ref[...])
        dg, du = vjp(df)
        dg_ref[...] = dg.astype(dg_ref.dtype)
        du_ref[...] = du.astype(du_ref.dtype)

    o_spec = pl.BlockSpec((tm, FFN_TN), lambda i, j: (i, j))
    return pl.pallas_call(
        kern, out_shape=[jax.ShapeDtypeStruct((m, n), BF16)] * 2, grid=(m // tm, n // FFN_TN),
        in_specs=[pl.BlockSpec((tm, k_dim), lambda i, j: (i, 0)), pl.BlockSpec((FFN_TN, k_dim), lambda i, j: (j, 0)),
                  o_spec, o_spec],
        out_specs=[o_spec] * 2, compiler_params=_params(("parallel", "parallel")), name=name)(dx, w_down, g, u)


def _rowwise(fn, rows, consts, out_rows, out_accs, *, name, tb=ROW_TILE, out_dtype=F32):
    t = (rows[0][0] if isinstance(rows[0], tuple) else rows[0]).shape[0]
    tb = min(tb, t)
    n_r, n_c, n_o, n_a = len(rows), len(consts), len(out_rows), len(out_accs)
    pieces = [w if isinstance(w, (list, tuple)) else [w] for w in out_rows]

    def kern(*refs):
        r_refs = refs[:n_r]
        c_refs = refs[n_r:n_r + n_c]
        o_refs = refs[n_r + n_c:n_r + n_c + n_o]
        a_refs = refs[n_r + n_c + n_o:]
        vals = fn(*[r[...] for r in r_refs], *[c[...] for c in c_refs])
        vals = list(vals) if isinstance(vals, (tuple, list)) else [vals]
        pos = 0
        for o_ref, ws in zip(o_refs, pieces):
            off = 0
            for w in ws:
                o_ref[:, off:off + w] = vals[pos].astype(o_ref.dtype)
                off += w
                pos += 1
        if n_a:
            @pl.when(pl.program_id(0) == 0)
            def _():
                for a_ref in a_refs:
                    a_ref[...] = jnp.zeros_like(a_ref)
            for a_ref, v in zip(a_refs, vals[pos:]):
                a_ref[...] += v

    in_specs, args = [], []
    for r in rows:
        if isinstance(r, tuple):
            arr, blk, w = r
            in_specs.append(pl.BlockSpec((tb, w), functools.partial(lambda i, blk: (i, blk), blk=blk)))
        else:
            arr = r
            in_specs.append(pl.BlockSpec((tb, arr.shape[1]), lambda i: (i, 0)))
        args.append(arr)
    for c in consts:
        in_specs.append(pl.BlockSpec(c.shape, lambda i: (0, 0)))
        args.append(c)
    out_shape = [jax.ShapeDtypeStruct((t, sum(ws)), out_dtype) for ws in pieces]
    out_specs = [pl.BlockSpec((tb, sum(ws)), lambda i: (i, 0)) for ws in pieces]
    for shp in out_accs:
        out_shape.append(jax.ShapeDtypeStruct(shp, F32))
        out_specs.append(pl.BlockSpec(shp, lambda i: (0, 0)))
    res = pl.pallas_call(
        kern, out_shape=out_shape, grid=(t // tb,), in_specs=in_specs, out_specs=out_specs,
        compiler_params=_params(("arbitrary",) if n_a else ("parallel",)), name=name)(*args)
    return res


def _rms(x, w):
    return x * lax.rsqrt(jnp.mean(x * x, axis=-1, keepdims=True) + RMS_EPS) * w


def _seg_sum(x, bd):
    return jnp.concatenate(
        [jnp.dot(x[:, LANES * j:LANES * (j + 1)], bd, precision=HIGHEST, preferred_element_type=F32)
         for j in range(x.shape[1] // LANES)], axis=1)


@jax.custom_vjp
def _seg(x, bd):
    return _seg_sum(x, bd)


_seg.defvjp(lambda x, bd: (_seg_sum(x, bd), bd), lambda bd, ct: (_seg_sum(ct, bd), jnp.zeros_like(bd)))


def _colsum(x):
    return jnp.sum(x, axis=0, keepdims=True)


def _prescan_math(r, k, xw, xa, xg, k_k, w0f, w0b, a0f, a0b, kaf, kab, wupf, wupb, aupf, aupb, gup, bd):
    kkr = k * k_k
    norm = jnp.sqrt(_seg(kkr * kkr, bd))
    kk = kkr / jnp.maximum(norm, NORM_EPS)
    th = jnp.tanh(xw)

    def direction(w0, wup, a0, aup, ka):
        logit = w0 + jnp.dot(th, wup, preferred_element_type=F32)
        w = jnp.exp(-LOG_DECAY_SCALE * jax.nn.sigmoid(logit))
        a = jax.nn.sigmoid(a0 + jnp.dot(xa, aup, preferred_element_type=F32))
        kd = k * (1.0 + (a - 1.0) * ka)
        return w, kd, kk * a

    wf, kdf, bf = direction(w0f, wupf, a0f, aupf, kaf)
    wb, kdb, bb = direction(w0b, wupb, a0b, aupb, kab)
    g = jnp.dot(jax.nn.sigmoid(xg), gup, preferred_element_type=F32)
    return kk, r, wf, wb, bf, bb, kdf, kdb, g


def _postscan_math(y, r, v, kdf, kdb, g, gn_w, gn_b, rkf, rkb, bd):
    mean = _seg(y, bd) * (1.0 / HEAD)
    yc = y - mean
    var = _seg(yc * yc, bd) * (1.0 / HEAD)
    yg = yc * lax.rsqrt(var + GN_EPS) * gn_w + gn_b
    bonus = (_seg(r * kdf * rkf, bd) + _seg(r * kdb * rkb, bd)) * v
    return (yg + bonus) * g


def _halo_specs(width, col_blk, tb, t):
    nb = t // SUBLANES
    step = tb // SUBLANES
    main = pl.BlockSpec((tb, width), lambda i: (i, col_blk))
    prev = pl.BlockSpec((SUBLANES, width), lambda i: (jnp.maximum(i * step - 1, 0), col_blk))
    nxt = pl.BlockSpec((SUBLANES, width), lambda i: (jnp.minimum((i + 1) * step, nb - 1), col_blk))
    return [main, prev, nxt]


def _neighbours(z, prev8, next8, first, last):
    tb = z.shape[0]
    row = lax.broadcasted_iota(jnp.int32, z.shape, 0)
    prow = jnp.where(first, 0.0, prev8[SUBLANES - 1:SUBLANES, :])
    nrow = jnp.where(last, 0.0, next8[0:1, :])
    down = jnp.where(row == 0, prow, pltpu.roll(z, 1, 0))
    up = jnp.where(row == tb - 1, nrow, pltpu.roll(z, tb - 1, 0))
    return down, up


def _shift_conv_fwd(p, mu, conv_w, seq, *, name, tb=ROW_TILE):
    t = p.shape[0]
    per_seq = seq // tb

    def kern(p_ref, pp_ref, pn_ref, mu_ref, cw_ref, pss_ref, oc_ref):
        i = pl.program_id(0)
        first = (i % per_seq) == 0
        last = (i % per_seq) == per_seq - 1
        ps = p_ref[:, :D_SP]
        down, up = _neighbours(ps, pp_ref[:, :D_SP], pn_ref[:, :D_SP], first, last)
        pss_ref[...] = ps + mu_ref[...] * (0.5 * (down + up) - ps)
        gb = p_ref[:, D_SP:D_SP + D_CONV]
        u = p_ref[:, D_SP + D_CONV:D_SP + 2 * D_CONV] * p_ref[:, D_SP + 2 * D_CONV:]
        u_p = pp_ref[:, D_SP + D_CONV:D_SP + 2 * D_CONV] * pp_ref[:, D_SP + 2 * D_CONV:]
        u_n = pn_ref[:, D_SP + D_CONV:D_SP + 2 * D_CONV] * pn_ref[:, D_SP + 2 * D_CONV:]
        udown, uup = _neighbours(u, u_p, u_n, first, last)
        oc_ref[...] = gb * (cw_ref[0:1, :] * udown + cw_ref[1:2, :] * u + cw_ref[2:3, :] * uup)

    return pl.pallas_call(
        kern,
        out_shape=[jax.ShapeDtypeStruct((t, D_SP), F32), jax.ShapeDtypeStruct((t, D_CONV), F32)],
        grid=(t // tb,),
        in_specs=_halo_specs(D_INP, 0, tb, t) + [pl.BlockSpec((1, D_SP), lambda i: (0, 0)),
                                                 pl.BlockSpec((SUBLANES, D_CONV), lambda i: (0, 0))],
        out_specs=[pl.BlockSpec((tb, D_SP), lambda i: (i, 0)), pl.BlockSpec((tb, D_CONV), lambda i: (i, 0))],
        compiler_params=_params(("parallel",)), name=name)(p, p, p, mu, conv_w)


def _shift_conv_bwd(p, d_pss, d_o, mu, conv_w, seq, *, name, tb=ROW_TILE):
    t = p.shape[0]
    per_seq = seq // tb

    def kern(p_ref, pp_ref, pn_ref, d_ref, dp_ref, dn_ref, do_ref, dop_ref, don_ref, mu_ref, cw_ref,
             out_ref, dmu_ref, dcw_ref):
        i = pl.program_id(0)
        first = (i % per_seq) == 0
        last = (i % per_seq) == per_seq - 1

        @pl.when(i == 0)
        def _():
            dmu_ref[...] = jnp.zeros_like(dmu_ref)
            dcw_ref[...] = jnp.zeros_like(dcw_ref)

        mu_v = mu_ref[...]
        ps = p_ref[:, :D_SP]
        down, up = _neighbours(ps, pp_ref[:, :D_SP], pn_ref[:, :D_SP], first, last)
        d = d_ref[...]
        ddown, dup = _neighbours(d, dp_ref[...], dn_ref[...], first, last)
        out_ref[:, :D_SP] = (d - mu_v * d + 0.5 * (mu_v * ddown + mu_v * dup)).astype(out_ref.dtype)
        dmu_ref[...] += _colsum(d * (0.5 * (down + up) - ps))

        def parts(ref):
            return (ref[:, D_SP:D_SP + D_CONV], ref[:, D_SP + D_CONV:D_SP + 2 * D_CONV],
                    ref[:, D_SP + 2 * D_CONV:])

        gb, gc, hh = parts(p_ref)
        gb_p, gc_p, hh_p = parts(pp_ref)
        gb_n, gc_n, hh_n = parts(pn_ref)
        u = gc * hh
        udown, uup = _neighbours(u, gc_p * hh_p, gc_n * hh_n, first, last)
        cw0, cw1, cw2 = cw_ref[0:1, :], cw_ref[1:2, :], cw_ref[2:3, :]
        do = do_ref[...]
        duc = do * gb
        ducdown, ducup = _neighbours(duc, dop_ref[...] * gb_p, don_ref[...] * gb_n, first, last)
        du = cw0 * ducup + cw1 * duc + cw2 * ducdown
        out_ref[:, D_SP:D_SP + D_CONV] = (do * (cw0 * udown + cw1 * u + cw2 * uup)).astype(out_ref.dtype)
        out_ref[:, D_SP + D_CONV:D_SP + 2 * D_CONV] = (du * hh).astype(out_ref.dtype)
        out_ref[:, D_SP + 2 * D_CONV:] = (du * gc).astype(out_ref.dtype)
        dcw_ref[0:1, :] += _colsum(duc * udown)
        dcw_ref[1:2, :] += _colsum(duc * u)
        dcw_ref[2:3, :] += _colsum(duc * uup)

    return pl.pallas_call(
        kern,
        out_shape=[jax.ShapeDtypeStruct((t, D_INP), BF16), jax.ShapeDtypeStruct((1, D_SP), F32),
                   jax.ShapeDtypeStruct((SUBLANES, D_CONV), F32)],
        grid=(t // tb,),
        in_specs=(_halo_specs(D_INP, 0, tb, t) + _halo_specs(D_SP, 0, tb, t) + _halo_specs(D_CONV, 1, tb, t)
                  + [pl.BlockSpec((1, D_SP), lambda i: (0, 0)),
                     pl.BlockSpec((SUBLANES, D_CONV), lambda i: (0, 0))]),
        out_specs=[pl.BlockSpec((tb, D_INP), lambda i: (i, 0)), pl.BlockSpec((1, D_SP), lambda i: (0, 0)),
                   pl.BlockSpec((SUBLANES, D_CONV), lambda i: (0, 0))],
        compiler_params=_params(("arbitrary",)), name=name)(p, p, p, d_pss, d_pss, d_pss, d_o, d_o, d_o, mu, conv_w)


N_CHAIN = 16
N_GROUP = LANES // N_CHAIN
V_HI = HEAD // SUBLANES
G_KK, G_R, G_W, G_B, G_KD = 0, 1, (2, 3), (4, 5), (6, 7)


K_HI = HEAD // SUBLANES


def _tree_sum(terms):
    terms = list(terms)
    while len(terms) > 1:
        terms = [a + b for a, b in zip(terms[::2], terms[1::2])]
    return terms[0]


def _kscan_specs(nc):
    same = lambda c: c
    mirror = lambda c: nc - 1 - c

    def k_spec(fn):
        return pl.BlockSpec((SCAN_CHUNK, HEAD, LANES), lambda c: (fn(c), 0, 0))

    def v_spec(fn):
        return pl.BlockSpec((SCAN_CHUNK, SUBLANES, LANES), lambda c: (fn(c), 0, 0))

    return same, mirror, k_spec, v_spec


ST_SHAPE = (2, K_HI, V_HI, SUBLANES, LANES)


def _lane_group_index():
    lane = lax.broadcasted_iota(jnp.int32, (SUBLANES, LANES), 1)
    return lax.shift_right_logical(lane, jnp.full_like(lane, 4))


def _spread_groups(x, grp):
    rolled = [x] + [pltpu.roll(x, s * N_CHAIN, 1) for s in range(1, N_GROUP)]
    out = []
    for j in range(N_GROUP):
        t = rolled[(0 - j) % N_GROUP]
        for g in range(1, N_GROUP):
            t = jnp.where(grp == g, rolled[(g - j) % N_GROUP], t)
        out.append(t)
    return out


def _gather_groups(tiles, grp):
    total = None
    for s in range(N_GROUP):
        b = tiles[s % N_GROUP]
        for g in range(1, N_GROUP):
            b = jnp.where(grp == g, tiles[(g + s) % N_GROUP], b)
        b = pltpu.roll(b, s * N_CHAIN, 1) if s else b
        total = b if total is None else total + b
    return total


def _lane_group_sum(x):
    return _tree_sum([x] + [pltpu.roll(x, k * N_CHAIN, 1) for k in range(1, N_GROUP)])


def _key_row(x_t, grp, kh):
    r = SUBLANES * grp + kh
    return jnp.broadcast_to(x_t[r:r + 1, :], (SUBLANES, LANES))


def _acc(total, term):
    return term if total is None else total + term


SA_SHAPE = (2, V_HI, SUBLANES, LANES)


def _scan_fwd(xall, v_c, *, gather=(), name):
    steps = xall.shape[0]
    nc = steps // SCAN_CHUNK
    same, mirror, k_spec, v_spec = _kscan_specs(nc)
    last = SCAN_CHUNK - 1
    n_x = len(gather)

    def kern(*refs):
        xf_ref, xb_ref, vf_ref, vb_ref = refs[:4]
        yf_ref, yb_ref, hist_ref, fin_ref, sa_ref = refs[4 + n_x:9 + n_x]
        st_ref = refs[9 + 2 * n_x]
        c = pl.program_id(0)

        def riders():
            return _exchange_copies(refs[4:4 + n_x], refs[9 + n_x:9 + 2 * n_x], 0, *refs[10 + 2 * n_x:])

        @pl.when(c == 0)
        def _():
            st_ref[...] = jnp.zeros_like(st_ref)
            if n_x:
                for cp in riders():
                    cp.start()

        hist_ref[0] = st_ref[...]
        grp = _lane_group_index()

        def body(i, put):
            j = last - i
            for d, (x_t, v_t, y_ref, at) in enumerate(((xf_ref[i], vf_ref[i], yf_ref, i),
                                                       (xb_ref[j], vb_ref[j], yb_ref, j))):
                v_b = _spread_groups(v_t, grp)
                part = [None] * V_HI
                for kh in range(K_HI):
                    kk_r = _key_row(x_t, G_KK, kh)
                    for vh in range(V_HI):
                        part[vh] = _acc(part[vh], hist_ref[i, d, kh, vh] * kk_r)
                sa = [_lane_group_sum(p) for p in part]
                for vh in range(V_HI):
                    sa_ref[i, d, vh] = sa[vh]
                y_p = [None] * V_HI
                for kh in range(K_HI):
                    r_r, w_r = _key_row(x_t, G_R, kh), _key_row(x_t, G_W[d], kh)
                    b_r, kd_r = _key_row(x_t, G_B[d], kh), _key_row(x_t, G_KD[d], kh)
                    for vh in range(V_HI):
                        new = hist_ref[i, d, kh, vh] * w_r - sa[vh] * b_r + v_b[vh] * kd_r
                        put(d, kh, vh, new)
                        y_p[vh] = _acc(y_p[vh], new * r_r)
                y_ref[at] = _gather_groups(y_p, grp)

        def step(i, carry):
            def put(d, kh, vh, val):
                hist_ref[i + 1, d, kh, vh] = val
            body(i, put)
            return carry

        lax.fori_loop(0, last, step, 0, unroll=SCAN_UNROLL)

        def put_carry(d, kh, vh, val):
            st_ref[d, kh, vh] = val

        body(last, put_carry)

        @pl.when(c == nc - 1)
        def _():
            fin_ref[...] = st_ref[...]
            if n_x:
                for cp in riders():
                    cp.wait()

    return pl.pallas_call(
        kern,
        out_shape=[jax.ShapeDtypeStruct((steps, SUBLANES, LANES), F32)] * 2
        + [jax.ShapeDtypeStruct((steps,) + ST_SHAPE, F32), jax.ShapeDtypeStruct(ST_SHAPE, F32),
           jax.ShapeDtypeStruct((steps,) + SA_SHAPE, F32)]
        + _exchange_out_shapes(gather, 0),
        grid=(nc,), in_specs=[k_spec(same), k_spec(mirror), v_spec(same), v_spec(mirror)] + _hbm_specs(n_x),
        out_specs=[v_spec(same), v_spec(mirror),
                   pl.BlockSpec((SCAN_CHUNK,) + ST_SHAPE, lambda c: (c, 0, 0, 0, 0, 0)),
                   pl.BlockSpec(ST_SHAPE, lambda c: (0, 0, 0, 0, 0)),
                   pl.BlockSpec((SCAN_CHUNK,) + SA_SHAPE, lambda c: (c, 0, 0, 0, 0))] + _hbm_specs(n_x),
        scratch_shapes=[pltpu.VMEM(ST_SHAPE, F32)] + (_exchange_sems(n_x) if n_x else []),
        compiler_params=_params(("arbitrary",), SCAN_VMEM_LIMIT), name=name)(xall, xall, v_c, v_c, *gather)


def _scan_bwd(xall, v_c, dy_c, hist, fin, sa, *, exchange=(), name):
    steps = xall.shape[0]
    nc = steps // SCAN_CHUNK
    same, back, k_spec, v_spec = _kscan_specs(nc)
    last = SCAN_CHUNK - 1
    n_x = len(exchange)

    def kern(*refs):
        xf_ref, xb_ref, vf_ref, vb_ref, dyf_ref, dyb_ref, hist_ref, fin_ref, sa_ref = refs[:9]
        gf_ref, gb_ref, dvf_ref, dvb_ref = refs[9 + n_x:13 + n_x]
        ds_ref, after_ref = refs[13 + 2 * n_x:15 + 2 * n_x]
        c = pl.program_id(0)

        def riders():
            return _exchange_copies(refs[9:9 + n_x], refs[13 + n_x:13 + 2 * n_x], n_x, *refs[15 + 2 * n_x:])

        @pl.when(c == 0)
        def _():
            ds_ref[...] = jnp.zeros_like(ds_ref)
            after_ref[...] = fin_ref[...]
            if n_x:
                for cp in riders():
                    cp.start()

        grp = _lane_group_index()
        row = lax.broadcasted_iota(jnp.int32, (SUBLANES, LANES), 0)
        zero = jnp.zeros((SUBLANES, LANES), F32)

        def body(i, after):
            j = last - i
            for d, (x_t, v_t, dy_t, g_ref, dv_ref, at) in enumerate((
                    (xf_ref[i], vf_ref[i], dyf_ref[i], gf_ref, dvf_ref, i),
                    (xb_ref[j], vb_ref[j], dyb_ref[j], gb_ref, dvb_ref, j))):
                v_s, dy_s = _spread_groups(v_t, grp), _spread_groups(dy_t, grp)
                dsa_p, dv_p = [None] * V_HI, [None] * V_HI
                for kh in range(K_HI):
                    r_r = _key_row(x_t, G_R, kh)
                    b_r, kd_r = _key_row(x_t, G_B[d], kh), _key_row(x_t, G_KD[d], kh)
                    for vh in range(V_HI):
                        g = ds_ref[d, kh, vh] + dy_s[vh] * r_r
                        ds_ref[d, kh, vh] = g
                        dsa_p[vh] = _acc(dsa_p[vh], g * b_r)
                        dv_p[vh] = _acc(dv_p[vh], g * kd_r)
                dsa = [-_lane_group_sum(p) for p in dsa_p]
                sa = [sa_ref[i, d, vh] for vh in range(V_HI)]
                dv_ref[at] = _gather_groups(dv_p, grp)
                blocks = {G_KK: zero, G_R: zero, G_W[d]: zero, G_B[d]: zero, G_KD[d]: zero}
                for kh in range(K_HI):
                    w_r, kk_r = _key_row(x_t, G_W[d], kh), _key_row(x_t, G_KK, kh)
                    dkk = dr = dw = db = dkd = None
                    for vh in range(V_HI):
                        g, before = ds_ref[d, kh, vh], hist_ref[i, d, kh, vh]
                        dr = _acc(dr, after(d, kh, vh) * dy_s[vh])
                        dw = _acc(dw, g * before)
                        dkd = _acc(dkd, g * v_s[vh])
                        db = _acc(db, g * sa[vh])
                        dkk = _acc(dkk, before * dsa[vh])
                        ds_ref[d, kh, vh] = g * w_r + dsa[vh] * kk_r
                    for gi, a in ((G_KK, dkk), (G_R, dr), (G_W[d], dw), (G_B[d], -db), (G_KD[d], dkd)):
                        blocks[gi] = jnp.where(row == kh, _colsum(a), blocks[gi])
                for gi in range(N_GROUP):
                    g_ref[at, SUBLANES * gi:SUBLANES * (gi + 1), :] = blocks.get(gi, zero)

        body(last, lambda d, kh, vh: after_ref[d, kh, vh])

        def step(ii, carry):
            i = last - ii
            body(i, lambda d, kh, vh: hist_ref[i + 1, d, kh, vh])
            return carry

        lax.fori_loop(1, SCAN_CHUNK, step, 0, unroll=SCAN_UNROLL)
        after_ref[...] = hist_ref[0]

        if n_x:
            @pl.when(c == nc - 1)
            def _():
                for cp in riders():
                    cp.wait()

    return pl.pallas_call(
        kern,
        out_shape=[jax.ShapeDtypeStruct((steps, HEAD, LANES), F32)] * 2
        + [jax.ShapeDtypeStruct((steps, SUBLANES, LANES), F32)] * 2 + _exchange_out_shapes(exchange, n_x),
        grid=(nc,),
        in_specs=[k_spec(back), k_spec(same), v_spec(back), v_spec(same), v_spec(back), v_spec(same),
                  pl.BlockSpec((SCAN_CHUNK,) + ST_SHAPE, lambda c: (back(c), 0, 0, 0, 0, 0)),
                  pl.BlockSpec(ST_SHAPE, lambda c: (0, 0, 0, 0, 0)),
                  pl.BlockSpec((SCAN_CHUNK,) + SA_SHAPE, lambda c: (back(c), 0, 0, 0, 0))] + _hbm_specs(n_x),
        out_specs=[k_spec(back), k_spec(same), v_spec(back), v_spec(same)] + _hbm_specs(n_x),
        scratch_shapes=[pltpu.VMEM(ST_SHAPE, F32), pltpu.VMEM(ST_SHAPE, F32)]
        + (_exchange_sems(n_x) if n_x else []),
        compiler_params=_params(("arbitrary",), SCAN_VMEM_LIMIT), name=name)(xall, xall, v_c, v_c, dy_c, dy_c, hist, fin, sa,
                                                            *exchange)


def _bf16_pieces(x):
    hi = x.astype(BF16)
    return hi, (x - hi.astype(F32)).astype(BF16)


def _to_key_rows(wide, bsz, seq, *, name):
    assert bsz == 2
    perm = _key_row_maps()
    tt = min(RELAYOUT_TILE, seq)
    per_seq = seq // tt

    def kern(x0_ref, x1_ref, p0_ref, p1_ref, o_ref):
        total = None
        for x_ref, p_ref in ((x0_ref, p0_ref), (x1_ref, p1_ref)):
            for piece in _bf16_pieces(x_ref[...]):
                term = jnp.dot(piece, p_ref[...], preferred_element_type=F32)
                total = term if total is None else total + term
        for r in range(K_HI):
            o_ref[:, r, :] = total[:, LANES * r:LANES * (r + 1)]

    p_spec = pl.BlockSpec((D_RWKV, K_HI * LANES), lambda i, a: (0, 0))
    return pl.pallas_call(
        kern, out_shape=jax.ShapeDtypeStruct((seq, HEAD, LANES), F32), grid=(per_seq, N_GROUP),
        in_specs=[pl.BlockSpec((tt, D_RWKV), lambda i, a: (i, a)),
                  pl.BlockSpec((tt, D_RWKV), lambda i, a: (per_seq + i, a)), p_spec, p_spec],
        out_specs=pl.BlockSpec((tt, K_HI, LANES), lambda i, a: (i, a, 0)),
        compiler_params=_params(("parallel", "parallel")), name=name)(wide, wide, *perm)


def _key_row_maps():
    src = jnp.arange(D_RWKV)
    head, kh, kl = src // HEAD, (src // SUBLANES) % K_HI, src % SUBLANES
    dst = jnp.arange(K_HI * LANES)
    return [((kh[:, None] == dst[None, :] // LANES) & (kl[:, None] == (dst[None, :] // N_CHAIN) % SUBLANES)
             & ((dst[None, :] // N_HEAD) % 2 == b) & (head[:, None] == dst[None, :] % N_HEAD)).astype(BF16)
            for b in range(2)]


def _from_key_rows(g_f, g_b, bsz, seq, *, name):
    assert bsz == 2
    maps = jnp.concatenate([m.T for m in _key_row_maps()], axis=1)
    tt = min(RELAYOUT_TILE, seq)
    per_seq = seq // tt

    def kern(gf_ref, gb_ref, q_ref, o_ref):
        a = pl.program_id(1)
        shared = a <= G_R
        from_f = shared | (a % 2 == G_W[0] % 2)

        def rearranged(g_ref):
            g = jnp.concatenate([g_ref[:, r, :] for r in range(K_HI)], axis=1)
            hi, mid = (jnp.dot(piece, q_ref[...], preferred_element_type=F32) for piece in _bf16_pieces(g))
            both = hi + mid
            return both[:, :D_RWKV], both[:, D_RWKV:]

        @pl.when(from_f)
        def _():
            o_ref[0], o_ref[1] = rearranged(gf_ref)

        @pl.when(jnp.logical_not(from_f))
        def _():
            o_ref[0], o_ref[1] = rearranged(gb_ref)

        @pl.when(shared)
        def _():
            more = rearranged(gb_ref)
            o_ref[0] += more[0]
            o_ref[1] += more[1]

    g_spec = pl.BlockSpec((tt, K_HI, LANES), lambda i, a: (i, a, 0))
    out = pl.pallas_call(
        kern, out_shape=jax.ShapeDtypeStruct((bsz, seq, N_GROUP * D_RWKV), F32), grid=(per_seq, N_GROUP),
        in_specs=[g_spec, g_spec, pl.BlockSpec((K_HI * LANES, bsz * D_RWKV), lambda i, a: (0, 0))],
        out_specs=pl.BlockSpec((bsz, tt, D_RWKV), lambda i, a: (0, i, a)),
        compiler_params=_params(("parallel", "parallel")), name=name)(g_f, g_b, maps)
    return out.reshape(bsz * seq, N_GROUP * D_RWKV)


def _to_value_rows(a, bsz, seq):
    z = a.reshape(bsz, seq, N_HEAD, V_HI, SUBLANES).transpose(1, 4, 3, 0, 2)
    return z.reshape(seq, SUBLANES, LANES)


def _from_value_rows(y, bsz, seq):
    z = y.reshape(seq, SUBLANES, V_HI, bsz, N_HEAD).transpose(3, 0, 4, 2, 1)
    return z.reshape(bsz * seq, D_RWKV)


def _pad_cols(a, segs):
    out, off = [], 0
    for w, wp in segs:
        out.append(a[..., off:off + w])
        if wp > w:
            out.append(jnp.zeros(a.shape[:-1] + (wp - w,), a.dtype))
        off += w
    return jnp.concatenate(out, axis=-1)


def _unpad_cols(a, segs):
    out, off = [], 0
    for w, wp in segs:
        out.append(a[..., off:off + w])
        off += wp
    return jnp.concatenate(out, axis=-1)


P_SEGS = ((3 * D_RWKV, 3 * D_RWKV), (D_LORA, 128), (D_LORA, 128), (D_GATE, 256), (3 * D_CONV, 3 * D_CONV))
S_SEGS = P_SEGS[:4]


def _pad_rows(a, rows):
    return jnp.concatenate([a, jnp.zeros((rows - a.shape[0], a.shape[1]), a.dtype)], axis=0)


LATE = ("w_out", "w_gate", "w_up", "w_down")


def _local_step(x, target, w, late=None):
    bsz, seq, _ = x.shape
    t = bsz * seq
    x2d = x.reshape(t, D_MODEL)
    tg2d = target.reshape(t, D_MODEL)
    row = lambda a: a.reshape(1, -1).astype(F32)

    w_in = _pad_cols(w["w_in"][0], P_SEGS)
    mu = _pad_cols(row(w["mu_shift"]), S_SEGS)
    wupf, wupb, aupf, aupb = (_pad_rows(w[n][0].astype(F32), 128) for n in ("w_up_f", "w_up_b", "a_up_f", "a_up_b"))
    gup = _pad_rows(w["g_up"][0].astype(F32), 256)
    conv_w = _pad_rows(w["conv_w"][0].astype(F32), SUBLANES)
    norm1, norm2, normf = row(w["norm1_w"]), row(w["norm2_w"]), row(w["norm_f_w"])
    vec = {n: row(w[n]) for n in VEC}
    head_of = jnp.arange(LANES) // HEAD
    bd = (head_of[:, None] == head_of[None, :]).astype(F32)
    pre_consts = [vec["k_k"], vec["w0_f"], vec["w0_b"], vec["a0_f"], vec["a0_b"], vec["k_a_f"], vec["k_a_b"],
                  wupf, wupb, aupf, aupb, gup, bd]
    post_consts = [vec["gn_w"], vec["gn_b"], vec["r_k_f"], vec["r_k_b"], bd]

    h1, = _rowwise(_rms, [x2d], [norm1], [D_MODEL], [], name="rms1_fwd", out_dtype=BF16, tb=WIDE_TILE)
    p = _mm(h1, w_in, name="mm_in")
    pss, oconv = _shift_conv_fwd(p, mu, conv_w, seq, name="shift_conv_fwd", tb=min(2 * ROW_TILE, seq))
    pre_rows = [(pss, 0, 512), (pss, 1, 512), (pss, XW0 // 128, 128), (pss, XA0 // 128, 128), (pss, XG0 // 256, 256)]
    sc, g = _rowwise(_prescan_math, pre_rows, pre_consts, [[D_RWKV] * N_GROUP, D_RWKV], [], name="prescan_fwd",
                     tb=2 * ROW_TILE)
    xall = _to_key_rows(sc, bsz, seq, name="to_key_rows")
    v_l = _to_value_rows(pss[:, 2 * D_RWKV:3 * D_RWKV], bsz, seq)
    y_f, y_b, hist, fin, sa, *gathered = _scan_fwd(xall, v_l, gather=[late[n] for n in LATE] if late else (),
                                                   name="scan_fwd")
    w_out, w_gate, w_up, w_down = (
        (_from_slots(a, SHARD_AXIS[n]) if late else w[n])[0] for n, a in zip(LATE, gathered or LATE))
    y = _from_value_rows(y_f + y_b, bsz, seq)
    post_rows = [y, (pss, 0, 512), (pss, 2, 512), (sc, G_KD[0], 512), (sc, G_KD[1], 512), g]

    def post_fwd(y_, r_, v_, kdf_, kdb_, g_, oc_, *consts):
        return _postscan_math(y_, r_, v_, kdf_, kdb_, g_, *consts), oc_

    o, = _rowwise(post_fwd, post_rows + [oconv], post_consts, [[D_RWKV, D_CONV]], [], name="postscan_fwd",
                  out_dtype=BF16, tb=2 * ROW_TILE)
    x1 = _mm(o, w_out, add=x2d, name="mm_out")
    h2, = _rowwise(_rms, [x1], [norm2], [D_MODEL], [], name="rms2_fwd", out_dtype=BF16, tb=WIDE_TILE)
    gg, uu, ff = _mm_swiglu(h2, w_gate, w_up, name="mm_gate_up")
    x2 = _mm(ff, w_down, add=x1, name="mm_down")

    def final(x_, tg_, wn_):
        yo, vjp = jax.vjp(_rms, x_, wn_)
        err = yo - tg_
        dx_, dwn_ = vjp(err * (1.0 / D_MODEL))
        part = jnp.sum(jnp.sum(err * err, axis=1, keepdims=True), axis=0, keepdims=True) * (0.5 / D_MODEL)
        return dx_, part + jnp.zeros((1, LANES), F32), dwn_

    dx2, loss_acc, d_normf = _rowwise(final, [x2, tg2d], [normf], [D_MODEL], [(1, LANES), (1, D_MODEL)],
                                      name="loss_head", tb=WIDE_TILE)
    dgg, duu = _mm_swiglu_bwd(dx2, w_down, gg, uu, name="mm_down_dx")
    g_w_down = _mm(ff, dx2, ta=True, name="mm_down_dw")
    dh2 = _mm(dgg, w_gate, tb=True, name="mm_gate_dx")
    dh2 = _mm(duu, w_up, tb=True, add=dh2, name="mm_up_dx")
    g_w_gate = _mm(h2, dgg, ta=True, name="mm_gate_dw")
    g_w_up = _mm(h2, duu, ta=True, name="mm_up_dw")

    def rms_bwd(x_, dh_, dres_, wn_):
        _, vjp = jax.vjp(_rms, x_, wn_)
        dx_, dwn_ = vjp(dh_)
        return dx_ + dres_, dwn_

    dx1, d_norm2 = _rowwise(rms_bwd, [x1, dh2, dx2], [norm2], [D_MODEL], [(1, D_MODEL)], name="rms2_bwd", tb=WIDE_TILE)
    do = _mm(dx1, w_out, tb=True, name="mm_out_dx")
    g_w_out = _mm(o, dx1, ta=True, name="mm_out_dw")

    def post_bwd(y_, r_, v_, kdf_, kdb_, g_, do_, *consts):
        _, vjp = jax.vjp(lambda *a: _postscan_math(*a, consts[4]), y_, r_, v_, kdf_, kdb_, g_, *consts[:4])
        return vjp(do_)

    (dy, dr_c, dv_c, dkdf_c, dkdb_c, dg, d_gn_w, d_gn_b, d_rkf, d_rkb) = _rowwise(
        post_bwd, post_rows + [(do, 0, 512)], post_consts, [D_RWKV] * 6, [(1, D_RWKV)] * 4, name="postscan_bwd",
        tb=2 * ROW_TILE)
    dy_l = _to_value_rows(dy, bsz, seq)
    late_grads = {"w_out": g_w_out[None], "w_gate": g_w_gate[None], "w_up": g_w_up[None], "w_down": g_w_down[None]}
    g_f, g_b, dv_f, dv_b, *late_parts = _scan_bwd(
        xall, v_l, dy_l, hist, fin, sa, name="scan_bwd",
        exchange=[_to_slots(late_grads[n], SHARD_AXIS[n]).astype(BF16) for n in LATE] if late else ())
    dsc = _from_key_rows(g_f, g_b, bsz, seq, name="from_key_rows")
    dv_s = _from_value_rows(dv_f + dv_b, bsz, seq)

    def pre_bwd(r_, k_, xw_, xa_, xg_, dkk_, dr_s, dwf_, dwb_, dbf_, dbb_, dkdf_s, dkdb_s,
                dr_c_, dv_c_, dv_s_, dkdf_c_, dkdb_c_, dg_, *consts):
        _, vjp = jax.vjp(lambda *a: _prescan_math(*a, consts[-1]), r_, k_, xw_, xa_, xg_, *consts[:-1])
        grads = vjp((dkk_, dr_s + dr_c_, dwf_, dwb_, dbf_, dbb_, dkdf_s + dkdf_c_, dkdb_s + dkdb_c_, dg_))
        dr_, dk_, dxw_, dxa_, dxg_ = grads[:5]
        return (dr_, dk_, dv_c_ + dv_s_, dxw_, dxa_, dxg_) + tuple(grads[5:])

    pre_b_rows = (pre_rows + [(dsc, j, 512) for j in range(N_GROUP)]
                  + [dr_c, dv_c, dv_s, dkdf_c, dkdb_c, dg])
    pre_b = _rowwise(pre_bwd, pre_b_rows, pre_consts, [[512, 512, 512, 128, 128, 256]],
                     [(1, D_RWKV)] * 7 + [(128, D_RWKV)] * 4 + [(256, D_RWKV)], name="prescan_bwd",
                     tb=2 * ROW_TILE)
    d_pss = pre_b[0]
    d_kk_, d_w0f, d_w0b, d_a0f, d_a0b, d_kaf, d_kab, d_wupf, d_wupb, d_aupf, d_aupb, d_gup = pre_b[1:]
    dp, d_mu, d_conv = _shift_conv_bwd(p, d_pss, do, mu, conv_w, seq, name="shift_conv_bwd",
                                       tb=min(2 * ROW_TILE, seq))
    g_w_in = _mm(h1, dp, ta=True, name="mm_in_dw")
    grads = {
        "w_in": _unpad_cols(g_w_in, P_SEGS)[None], "mu_shift": _unpad_cols(d_mu, S_SEGS),
        "w_up_f": d_wupf[None, :D_LORA], "w0_f": d_w0f, "w_up_b": d_wupb[None, :D_LORA], "w0_b": d_w0b,
        "a_up_f": d_aupf[None, :D_LORA], "a0_f": d_a0f, "a_up_b": d_aupb[None, :D_LORA], "a0_b": d_a0b,
        "g_up": d_gup[None, :D_GATE], "k_k": d_kk_, "k_a_f": d_kaf, "k_a_b": d_kab,
        "r_k_f": d_rkf, "r_k_b": d_rkb, "gn_w": d_gn_w, "gn_b": d_gn_b, "conv_w": d_conv[None, :3],
        "w_out": g_w_out[None], "norm2_w": d_norm2, "w_gate": g_w_gate[None], "w_up": g_w_up[None],
        "w_down": g_w_down[None], "norm_f_w": d_normf,
    }
    early = ("w_in",) + LORA
    parts = dict(zip(LATE, late_parts))
    if late:
        vec_rows = jnp.concatenate([grads[n] for n in VEC] + [jnp.zeros((16 - len(VEC), D_RWKV), F32)], axis=0)
        slots = [_to_slots(grads[n], SHARD_AXIS[n]).astype(BF16 if n in BIG else F32) for n in early]
        dh1, *recv = _mm(dp, w_in, tb=True, exchange=(slots, [vec_rows]), name="mm_in_dx")
        parts.update(zip(early + ("vec",), recv))
    else:
        dh1 = _mm(dp, w_in, tb=True, name="mm_in_dx")
    dx, grads["norm1_w"] = _rowwise(rms_bwd, [x2d, dh1, dx1], [norm1], [D_MODEL], [(1, D_MODEL)], name="rms1_bwd",
                                    tb=WIDE_TILE)
    return loss_acc, dx.reshape(bsz, seq, D_MODEL), grads, parts


def _hbm_specs(n):
    return [pl.BlockSpec(memory_space=pl.ANY)] * n


def _all_gather(arrs, *, name):
    n = len(arrs)

    def body(*refs):
        x_refs, out_refs = refs[:n], refs[n:2 * n]
        send_sems, recv_sems, local_sems = refs[2 * n:]
        x, y, c = lax.axis_index("x"), lax.axis_index("y"), lax.axis_index("c")
        me, sibling = (x, y, c), (x, y, 1 - c)
        chips = [(1 - x, y), (x, 1 - y), (1 - x, 1 - y)]

        def slot(a, px, py, pc):
            return out_refs[a].at[4 * px + 2 * py + pc]

        def copy(a, k, block, to, src=None):
            return pltpu.make_async_remote_copy(
                src_ref=slot(a, *block) if src is None else src, dst_ref=slot(a, *block),
                send_sem=send_sems.at[k, a], recv_sem=recv_sems.at[k, a],
                device_id=to, device_id_type=pl.DeviceIdType.MESH)

        mine = [pltpu.make_async_copy(x_refs[a], slot(a, *me), local_sems.at[a]) for a in range(n)]
        for cp in mine:
            cp.start()
        first = []
        for a in range(n):
            first.append(copy(a, 0, me, sibling, src=x_refs[a]))
            first += [copy(a, 1 + j, me, (*chip, c), src=x_refs[a]) for j, chip in enumerate(chips)]
        for cp in first:
            cp.start()
        passed = []
        for j, chip in enumerate(chips):
            for a in range(n):
                copy(a, 1 + j, (*chip, c), me).wait_recv()
                cp = copy(a, 4 + j, (*chip, c), sibling)
                cp.start()
                passed.append(cp)
        for a in range(n):
            copy(a, 0, sibling, me).wait_recv()
            for j, chip in enumerate(chips):
                copy(a, 4 + j, (*chip, 1 - c), me).wait_recv()
        for cp in first + passed:
            cp.wait_send()
        for cp in mine:
            cp.wait()

    return pl.pallas_call(
        body, out_shape=[jax.ShapeDtypeStruct((N_DEV,) + a.shape, a.dtype) for a in arrs],
        in_specs=_hbm_specs(n), out_specs=_hbm_specs(n),
        scratch_shapes=[pltpu.SemaphoreType.DMA((7, n)), pltpu.SemaphoreType.DMA((7, n)),
                        pltpu.SemaphoreType.DMA((n,))],
        name=name)(*arrs)


def _exchange(sliced, whole, *, name):
    arrs = list(sliced) + list(whole)
    n, n_sliced = len(arrs), len(sliced)

    def body(*refs):
        copies = _exchange_copies(refs[:n], refs[n:2 * n], n_sliced, *refs[2 * n:])
        for cp in copies:
            cp.start()
        for cp in copies:
            cp.wait()

    return pl.pallas_call(
        body, out_shape=_exchange_out_shapes(arrs, n_sliced), in_specs=_hbm_specs(n), out_specs=_hbm_specs(n),
        scratch_shapes=_exchange_sems(n), name=name)(*arrs)


def _exchange_out_shapes(arrs, n_sliced):
    return [jax.ShapeDtypeStruct(a.shape if i < n_sliced else (N_DEV,) + a.shape, a.dtype)
            for i, a in enumerate(arrs)]


def _exchange_sems(n):
    return [pltpu.SemaphoreType.DMA((7, n)), pltpu.SemaphoreType.DMA((7, n)), pltpu.SemaphoreType.DMA((n,))]


def _exchange_copies(in_refs, out_refs, n_sliced, send_sems, recv_sems, local_sems):
    n = len(in_refs)
    x, y, c = lax.axis_index("x"), lax.axis_index("y"), lax.axis_index("c")
    me = 4 * x + 2 * y + c

    def src(a, dev):
        return in_refs[a].at[dev] if a < n_sliced else in_refs[a]

    copies = [pltpu.make_async_copy(src(a, me), out_refs[a].at[me], local_sems.at[a]) for a in range(n)]
    for k in range(1, N_DEV):
        px = 1 - x if k & 4 else x
        py = 1 - y if k & 2 else y
        pc = 1 - c if k & 1 else c
        for a in range(n):
            copies.append(pltpu.make_async_remote_copy(
                src_ref=src(a, 4 * px + 2 * py + pc), dst_ref=out_refs[a].at[me],
                send_sem=send_sems.at[k - 1, a], recv_sem=recv_sems.at[k - 1, a],
                device_id=(px, py, pc), device_id_type=pl.DeviceIdType.MESH))
    return copies


def _adam_math(g, w, m, v):
    nm = ADAM_B1 * m + (1.0 - ADAM_B1) * g
    nv = ADAM_B2 * v + (1.0 - ADAM_B2) * (g * g)
    m_hat = nm / (1.0 - ADAM_B1 ** ADAM_STEP)
    v_hat = nv / (1.0 - ADAM_B2 ** ADAM_STEP)
    return -ADAM_LR * (m_hat / (jnp.sqrt(v_hat) + ADAM_EPS) + ADAM_WD * w), nm, nv


def _slot_sum(ref):
    g = ref[0].astype(F32)
    for s in range(1, N_DEV):
        g = g + ref[s].astype(F32)
    return g


def _adamw_big(parts, w, m, v, *, name):
    _, rws, cols = w.shape
    tr = _tile(rws, (256, 176, 128))

    def kern(p_ref, w_ref, m_ref, v_ref, g_ref, d_ref, nm_ref, nv_ref):
        g = _slot_sum(p_ref)
        g_ref[...] = g
        d_ref[...], nm_ref[...], nv_ref[...] = _adam_math(g, w_ref[...], m_ref[...], v_ref[...])

    spec = pl.BlockSpec((1, tr, cols), lambda i: (0, i, 0))
    return pl.pallas_call(
        kern, out_shape=[jax.ShapeDtypeStruct(w.shape, F32)] * 4, grid=(rws // tr,),
        in_specs=[pl.BlockSpec((N_DEV, 1, tr, cols), lambda i: (0, 0, i, 0)), spec, spec, spec],
        out_specs=[spec] * 4, compiler_params=_params(("parallel",)), name=name)(parts, w, m, v)


def _adamw_small(lora_parts, vec_parts, wide_parts, wmv, *, name):
    names = LORA + VEC + WIDE
    n_l, n = len(LORA), len(names)
    flat = [a for trip in wmv for a in trip]

    def kern(*refs):
        l_refs, vec_ref, wide_ref = refs[:n_l], refs[n_l], refs[n_l + 1]
        in_refs = refs[n_l + 2:n_l + 2 + 3 * n]
        out_refs = refs[n_l + 2 + 3 * n:]
        vec_sum, wide_sum = _slot_sum(vec_ref), _slot_sum(wide_ref)
        for i, nm in enumerate(names):
            w_ref, m_ref, v_ref = in_refs[3 * i:3 * i + 3]
            if i < n_l:
                g = _slot_sum(l_refs[i])
            elif nm in VEC:
                g = vec_sum[i - n_l:i - n_l + 1, :]
            else:
                g = wide_sum[WIDE.index(nm):WIDE.index(nm) + 1, :w_ref.shape[-1]]
            o = out_refs[4 * i:4 * i + 4]
            o[0][...] = g
            o[1][...], o[2][...], o[3][...] = _adam_math(g, w_ref[...], m_ref[...], v_ref[...])

    out_shape = [jax.ShapeDtypeStruct(trip[0].shape, F32) for trip in wmv for _ in range(4)]
    outs = pl.pallas_call(kern, out_shape=out_shape, name=name,
                          compiler_params=pltpu.CompilerParams(vmem_limit_bytes=VMEM_LIMIT))(
        *lora_parts, vec_parts, wide_parts, *flat)
    return [tuple(outs[4 * i:4 * i + 4]) for i in range(n)]


def _to_slots(g, axis):
    _, rws, cols = g.shape
    if axis == 1:
        return g.reshape(N_DEV, 1, rws // N_DEV, cols)
    return g.reshape(1, rws, N_DEV, cols // N_DEV).transpose(2, 0, 1, 3)


def _from_slots(got, axis):
    _, _, rws, cols = got.shape
    if axis == 1:
        return got.reshape(1, N_DEV * rws, cols)
    return got.transpose(1, 2, 0, 3).reshape(1, rws, N_DEV * cols)


def _pad_lanes(a, width):
    return jnp.concatenate([a, jnp.zeros(a.shape[:-1] + (width - a.shape[-1],), a.dtype)], axis=-1)


def kernel(x, norm1_w, w_in, mu_shift, w_up_f, w0_f, w_up_b, w0_b, a_up_f, a0_f, a_up_b, a0_b, g_up, k_k, k_a_f, k_a_b, r_k_f, r_k_b, gn_w, gn_b, conv_w, w_out, norm2_w, w_gate, w_up, w_down, norm_f_w, loss_target, m_norm1_w, m_w_in, m_mu_shift, m_w_up_f, m_w0_f, m_w_up_b, m_w0_b, m_a_up_f, m_a0_f, m_a_up_b, m_a0_b, m_g_up, m_k_k, m_k_a_f, m_k_a_b, m_r_k_f, m_r_k_b, m_gn_w, m_gn_b, m_conv_w, m_w_out, m_norm2_w, m_w_gate, m_w_up, m_w_down, m_norm_f_w, v_norm1_w, v_w_in, v_mu_shift, v_w_up_f, v_w0_f, v_w_up_b, v_w0_b, v_a_up_f, v_a0_f, v_a_up_b, v_a0_b, v_g_up, v_k_k, v_k_a_f, v_k_a_b, v_r_k_f, v_r_k_b, v_gn_w, v_gn_b, v_conv_w, v_w_out, v_norm2_w, v_w_gate, v_w_up, v_w_down, v_norm_f_w):
    local = dict(norm1_w=norm1_w, w_in=w_in, mu_shift=mu_shift, w_up_f=w_up_f, w0_f=w0_f, w_up_b=w_up_b,
                 w0_b=w0_b, a_up_f=a_up_f, a0_f=a0_f, a_up_b=a_up_b, a0_b=a0_b, g_up=g_up, k_k=k_k, k_a_f=k_a_f,
                 k_a_b=k_a_b, r_k_f=r_k_f, r_k_b=r_k_b, gn_w=gn_w, gn_b=gn_b, conv_w=conv_w, w_out=w_out,
                 norm2_w=norm2_w, w_gate=w_gate, w_up=w_up, w_down=w_down, norm_f_w=norm_f_w)
    mom_m = dict(norm1_w=m_norm1_w, w_in=m_w_in, mu_shift=m_mu_shift, w_up_f=m_w_up_f, w0_f=m_w0_f,
                 w_up_b=m_w_up_b, w0_b=m_w0_b, a_up_f=m_a_up_f, a0_f=m_a0_f, a_up_b=m_a_up_b, a0_b=m_a0_b,
                 g_up=m_g_up, k_k=m_k_k, k_a_f=m_k_a_f, k_a_b=m_k_a_b, r_k_f=m_r_k_f, r_k_b=m_r_k_b,
                 gn_w=m_gn_w, gn_b=m_gn_b, conv_w=m_conv_w, w_out=m_w_out, norm2_w=m_norm2_w, w_gate=m_w_gate,
                 w_up=m_w_up, w_down=m_w_down, norm_f_w=m_norm_f_w)
    mom_v = dict(norm1_w=v_norm1_w, w_in=v_w_in, mu_shift=v_mu_shift, w_up_f=v_w_up_f, w0_f=v_w0_f,
                 w_up_b=v_w_up_b, w0_b=v_w0_b, a_up_f=v_a_up_f, a0_f=v_a0_f, a_up_b=v_a_up_b, a0_b=v_a0_b,
                 g_up=v_g_up, k_k=v_k_k, k_a_f=v_k_a_f, k_a_b=v_k_a_b, r_k_f=v_r_k_f, r_k_b=v_r_k_b,
                 gn_w=v_gn_w, gn_b=v_gn_b, conv_w=v_conv_w, w_out=v_w_out, norm2_w=v_norm2_w, w_gate=v_w_gate,
                 w_up=v_w_up, w_down=v_w_down, norm_f_w=v_norm_f_w)

    early = ("w_in",) + LORA
    got = _all_gather([local["w_in"].astype(BF16)] + [local[n] for n in LORA], name="gather")
    full = dict(local)
    full.update({n: _from_slots(a, SHARD_AXIS[n]) for n, a in zip(early, got)})

    loss_part, grad_x, grads, parts = _local_step(x, loss_target, full,
                                                  late={n: local[n].astype(BF16) for n in LATE})

    wide_rows = jnp.concatenate([_pad_lanes(a, WIDE_ROW) for a in [grads[n] for n in WIDE] + [loss_part]]
                                + [jnp.zeros((SUBLANES - len(WIDE) - 1, WIDE_ROW), F32)], axis=0)
    wide_parts, = _exchange([], [wide_rows], name="grad_exchange")
    loss = jnp.sum(wide_parts[:, len(WIDE), 0])
    out = {}
    for n in BIG:
        out[n] = _adamw_big(parts[n], local[n], mom_m[n], mom_v[n], name="adamw_" + n)

    def small_form(n, a):
        if n in LORA:
            return a
        a = a.reshape(1, -1)
        return _pad_lanes(a, WIDE_ROW) if n == "mu_shift" else a

    small = LORA + VEC + WIDE
    res = _adamw_small([parts[n] for n in LORA], parts["vec"], wide_parts,
                       [tuple(small_form(n, d[n]) for d in (local, mom_m, mom_v)) for n in small],
                       name="adamw_small")
    for n, quad in zip(small, res):
        out[n] = tuple(a[..., :local[n].size].reshape(local[n].shape) if n not in LORA else a for a in quad)
    return (loss, grad_x, *[out[n][i] for i in range(4) for n in WEIGHTS])
```

```python
import functools

import jax
import jax.numpy as jnp
from jax import lax
from jax.experimental import pallas as pl
from jax.experimental.pallas import tpu as pltpu

F32 = jnp.float32
BF16 = jnp.bfloat16
HIGHEST = lax.Precision.HIGHEST

N_DEV = 8
D_MODEL = 1024
D_RWKV = 512
D_CONV = 512
HEAD = 64
N_HEAD = D_RWKV // HEAD
D_LORA = 64
D_GATE = 160
D_SHIFTED = 3 * D_RWKV + 2 * D_LORA + D_GATE
XW0, XA0, XG0 = 1536, 1664, 1792
D_SP = 2048
D_INP = D_SP + 3 * D_CONV
LOG_DECAY_SCALE = 0.606531
RMS_EPS = 1e-6
GN_EPS = 64e-5
NORM_EPS = 1e-12
ADAM_LR, ADAM_B1, ADAM_B2, ADAM_EPS, ADAM_WD, ADAM_STEP = 0.001, 0.9, 0.999, 1e-08, 0.01, 10

LANES = 128
SUBLANES = 8
VMEM_LIMIT = 48 * 1024 * 1024
SCAN_CHUNK = 32
SCAN_VMEM_LIMIT = 58 * 1024 * 1024
SCAN_UNROLL = 3
ROW_TILE = 128
WIDE_TILE = 512
RELAYOUT_TILE = 512

BIG = ("w_in", "w_out", "w_gate", "w_up", "w_down")
LORA = ("w_up_f", "w_up_b", "a_up_f", "a_up_b", "g_up", "conv_w")
SHARD_AXIS = {"w_in": 2, "w_out": 1, "w_gate": 2, "w_up": 2, "w_down": 1, "w_up_f": 2, "w_up_b": 2,
              "a_up_f": 2, "a_up_b": 2, "g_up": 2, "conv_w": 2}
VEC = ("w0_f", "w0_b", "a0_f", "a0_b", "k_k", "k_a_f", "k_a_b", "r_k_f", "r_k_b", "gn_w", "gn_b")
WIDE = ("mu_shift", "norm1_w", "norm2_w", "norm_f_w")
WIDE_ROW = 2048
WEIGHTS = ("norm1_w", "w_in", "mu_shift", "w_up_f", "w0_f", "w_up_b", "w0_b", "a_up_f", "a0_f", "a_up_b",
           "a0_b", "g_up", "k_k", "k_a_f", "k_a_b", "r_k_f", "r_k_b", "gn_w", "gn_b", "conv_w", "w_out",
           "norm2_w", "w_gate", "w_up", "w_down", "norm_f_w")


def _params(sem, limit=VMEM_LIMIT):
    return pltpu.CompilerParams(dimension_semantics=sem, vmem_limit_bytes=limit)


def _tile(n, cands):
    for c in cands:
        if n % c == 0:
            return c
    raise ValueError(f"no tile for {n}")


def _mm(a, b, *, ta=False, tb=False, add=None, exchange=None, name):
    (k_dim, m) = a.shape if ta else a.shape[::-1]
    (k2, n) = b.shape[::-1] if tb else b.shape
    assert k_dim == k2, (a.shape, b.shape, ta, tb)
    tm = _tile(m, (1408, 1024, 512, 256, 128))
    tn = _tile(n, (1408, 1024, 896, 512, 256, 128))
    tk = _tile(k_dim, (1408, 1024, 896, 512, 256, 128))
    nk = k_dim // tk
    grid = (m // tm, n // tn, nk)
    dims = (((0 if ta else 1,), (1 if tb else 0,)), ((), ()))
    sliced, whole = exchange or ((), ())
    riders = list(sliced) + list(whole)
    n_x, n_in = len(riders), 2 + (add is not None)

    def kern(*refs):
        a_ref, b_ref = refs[:2]
        add_ref = refs[2] if add is not None else None
        o_ref, acc_ref = refs[n_in + n_x], refs[n_in + 2 * n_x + 1]
        k = pl.program_id(2)
        step = (pl.program_id(0) * grid[1] + pl.program_id(1)) * nk + k

        def copies():
            return _exchange_copies(refs[n_in:n_in + n_x], refs[n_in + n_x + 1:n_in + 2 * n_x + 1], len(sliced),
                                    *refs[n_in + 2 * n_x + 2:])

        if n_x:
            @pl.when(step == 0)
            def _():
                for cp in copies():
                    cp.start()

        @pl.when(k == 0)
        def _():
            acc_ref[...] = jnp.zeros_like(acc_ref)

        acc_ref[...] += lax.dot_general(a_ref[...].astype(BF16), b_ref[...].astype(BF16), dims,
                                        preferred_element_type=F32)

        @pl.when(k == nk - 1)
        def _():
            if add is None:
                o_ref[...] = acc_ref[...]
            else:
                o_ref[...] = acc_ref[...] + add_ref[...]

        if n_x:
            @pl.when(step == grid[0] * grid[1] * nk - 1)
            def _():
                for cp in copies():
                    cp.wait()

    a_spec = (pl.BlockSpec((tk, tm), lambda i, j, k: (k, i)) if ta
              else pl.BlockSpec((tm, tk), lambda i, j, k: (i, k)))
    b_spec = (pl.BlockSpec((tn, tk), lambda i, j, k: (j, k)) if tb
              else pl.BlockSpec((tk, tn), lambda i, j, k: (k, j)))
    o_spec = pl.BlockSpec((tm, tn), lambda i, j, k: (i, j))
    in_specs = [a_spec, b_spec] + ([o_spec] if add is not None else []) + _hbm_specs(n_x)
    args = (a, b) + ((add,) if add is not None else ()) + tuple(riders)
    out = pl.pallas_call(
        kern, out_shape=[jax.ShapeDtypeStruct((m, n), F32)] + _exchange_out_shapes(riders, len(sliced)), grid=grid,
        in_specs=in_specs, out_specs=[o_spec] + _hbm_specs(n_x),
        scratch_shapes=[pltpu.VMEM((tm, tn), F32)] + (_exchange_sems(n_x) if n_x else []),
        compiler_params=_params(("arbitrary",) * 3 if n_x else ("parallel", "parallel", "arbitrary")),
        name=name)(*args)
    return out if n_x else out[0]


def _swiglu(g, u):
    return jax.nn.silu(g) * u


FFN_TN = 256


def _mm_swiglu(h, w_gate, w_up, *, name):
    m, k_dim = h.shape
    n = w_gate.shape[1]
    tm = _tile(m, (2048, 1024, 512, 256, 128))

    def kern(h_ref, wg_ref, wu_ref, g_ref, u_ref, f_ref):
        hv = h_ref[...].astype(BF16)
        g = jnp.dot(hv, wg_ref[...].astype(BF16), preferred_element_type=F32)
        u = jnp.dot(hv, wu_ref[...].astype(BF16), preferred_element_type=F32)
        g_ref[...] = g
        u_ref[...] = u
        f_ref[...] = _swiglu(g, u).astype(f_ref.dtype)

    w_spec = pl.BlockSpec((k_dim, FFN_TN), lambda i, j: (0, j))
    o_spec = pl.BlockSpec((tm, FFN_TN), lambda i, j: (i, j))
    return pl.pallas_call(
        kern, out_shape=[jax.ShapeDtypeStruct((m, n), F32)] * 2 + [jax.ShapeDtypeStruct((m, n), BF16)],
        grid=(m // tm, n // FFN_TN), in_specs=[pl.BlockSpec((tm, k_dim), lambda i, j: (i, 0)), w_spec, w_spec],
        out_specs=[o_spec] * 3, compiler_params=_params(("parallel", "parallel")), name=name)(h, w_gate, w_up)


def _mm_swiglu_bwd(dx, w_down, g, u, *, name):
    m, k_dim = dx.shape
    n = w_down.shape[0]
    tm = _tile(m, (2048, 1024, 512, 256, 128))

    def kern(dx_ref, w_ref, g_ref, u_ref, dg_ref, du_ref):
        df = lax.dot_general(dx_ref[...].astype(BF16), w_ref[...].astype(BF16), (((1,), (1,)), ((), ())),
                             preferred_element_type=F32)
        _, vjp = jax.vjp(_swiglu, g_ref[...], u_ref[...])
        dg, du = vjp(df)
        dg_ref[...] = dg.astype(dg_ref.dtype)
        du_ref[...] = du.astype(du_ref.dtype)

    o_spec = pl.BlockSpec((tm, FFN_TN), lambda i, j: (i, j))
    return pl.pallas_call(
        kern, out_shape=[jax.ShapeDtypeStruct((m, n), BF16)] * 2, grid=(m // tm, n // FFN_TN),
        in_specs=[pl.BlockSpec((tm, k_dim), lambda i, j: (i, 0)), pl.BlockSpec((FFN_TN, k_dim), lambda i, j: (j, 0)),
                  o_spec, o_spec],
        out_specs=[o_spec] * 2, compiler_params=_params(("parallel", "parallel")), name=name)(dx, w_down, g, u)


def _rowwise(fn, rows, consts, out_rows, out_accs, *, name, tb=ROW_TILE, out_dtype=F32):
    t = (rows[0][0] if isinstance(rows[0], tuple) else rows[0]).shape[0]
    tb = min(tb, t)
    n_r, n_c, n_o, n_a = len(rows), len(consts), len(out_rows), len(out_accs)
    pieces = [w if isinstance(w, (list, tuple)) else [w] for w in out_rows]

    def kern(*refs):
        r_refs = refs[:n_r]
        c_refs = refs[n_r:n_r + n_c]
        o_refs = refs[n_r + n_c:n_r + n_c + n_o]
        a_refs = refs[n_r + n_c + n_o:]
        vals = fn(*[r[...] for r in r_refs], *[c[...] for c in c_refs])
        vals = list(vals) if isinstance(vals, (tuple, list)) else [vals]
        pos = 0
        for o_ref, ws in zip(o_refs, pieces):
            off = 0
            for w in ws:
                o_ref[:, off:off + w] = vals[pos].astype(o_ref.dtype)
                off += w
                pos += 1
        if n_a:
            @pl.when(pl.program_id(0) == 0)
            def _():
                for a_ref in a_refs:
                    a_ref[...] = jnp.zeros_like(a_ref)
            for a_ref, v in zip(a_refs, vals[pos:]):
                a_ref[...] += v

    in_specs, args = [], []
    for r in rows:
        if isinstance(r, tuple):
            arr, blk, w = r
            in_specs.append(pl.BlockSpec((tb, w), functools.partial(lambda i, blk: (i, blk), blk=blk)))
        else:
            arr = r
            in_specs.append(pl.BlockSpec((tb, arr.shape[1]), lambda i: (i, 0)))
        args.append(arr)
    for c in consts:
        in_specs.append(pl.BlockSpec(c.shape, lambda i: (0, 0)))
        args.append(c)
    out_shape = [jax.ShapeDtypeStruct((t, sum(ws)), out_dtype) for ws in pieces]
    out_specs = [pl.BlockSpec((tb, sum(ws)), lambda i: (i, 0)) for ws in pieces]
    for shp in out_accs:
        out_shape.append(jax.ShapeDtypeStruct(shp, F32))
        out_specs.append(pl.BlockSpec(shp, lambda i: (0, 0)))
    res = pl.pallas_call(
        kern, out_shape=out_shape, grid=(t // tb,), in_specs=in_specs, out_specs=out_specs,
        compiler_params=_params(("arbitrary",) if n_a else ("parallel",)), name=name)(*args)
    return res


def _rms(x, w):
    return x * lax.rsqrt(jnp.mean(x * x, axis=-1, keepdims=True) + RMS_EPS) * w


def _seg_sum(x, bd):
    return jnp.concatenate(
        [jnp.dot(x[:, LANES * j:LANES * (j + 1)], bd, precision=HIGHEST, preferred_element_type=F32)
         for j in range(x.shape[1] // LANES)], axis=1)


@jax.custom_vjp
def _seg(x, bd):
    return _seg_sum(x, bd)


_seg.defvjp(lambda x, bd: (_seg_sum(x, bd), bd), lambda bd, ct: (_seg_sum(ct, bd), jnp.zeros_like(bd)))


def _colsum(x):
    return jnp.sum(x, axis=0, keepdims=True)


def _prescan_math(r, k, xw, xa, xg, k_k, w0f, w0b, a0f, a0b, kaf, kab, wupf, wupb, aupf, aupb, gup, bd):
    kkr = k * k_k
    norm = jnp.sqrt(_seg(kkr * kkr, bd))
    kk = kkr / jnp.maximum(norm, NORM_EPS)
    th = jnp.tanh(xw)

    def direction(w0, wup, a0, aup, ka):
        logit = w0 + jnp.dot(th, wup, preferred_element_type=F32)
        w = jnp.exp(-LOG_DECAY_SCALE * jax.nn.sigmoid(logit))
        a = jax.nn.sigmoid(a0 + jnp.dot(xa, aup, preferred_element_type=F32))
        kd = k * (1.0 + (a - 1.0) * ka)
        return w, kd, kk * a

    wf, kdf, bf = direction(w0f, wupf, a0f, aupf, kaf)
    wb, kdb, bb = direction(w0b, wupb, a0b, aupb, kab)
    g = jnp.dot(jax.nn.sigmoid(xg), gup, preferred_element_type=F32)
    return kk, r, wf, wb, bf, bb, kdf, kdb, g


def _postscan_math(y, r, v, kdf, kdb, g, gn_w, gn_b, rkf, rkb, bd):
    mean = _seg(y, bd) * (1.0 / HEAD)
    yc = y - mean
    var = _seg(yc * yc, bd) * (1.0 / HEAD)
    yg = yc * lax.rsqrt(var + GN_EPS) * gn_w + gn_b
    bonus = (_seg(r * kdf * rkf, bd) + _seg(r * kdb * rkb, bd)) * v
    return (yg + bonus) * g


def _halo_specs(width, col_blk, tb, t):
    nb = t // SUBLANES
    step = tb // SUBLANES
    main = pl.BlockSpec((tb, width), lambda i: (i, col_blk))
    prev = pl.BlockSpec((SUBLANES, width), lambda i: (jnp.maximum(i * step - 1, 0), col_blk))
    nxt = pl.BlockSpec((SUBLANES, width), lambda i: (jnp.minimum((i + 1) * step, nb - 1), col_blk))
    return [main, prev, nxt]


def _neighbours(z, prev8, next8, first, last):
    tb = z.shape[0]
    row = lax.broadcasted_iota(jnp.int32, z.shape, 0)
    prow = jnp.where(first, 0.0, prev8[SUBLANES - 1:SUBLANES, :])
    nrow = jnp.where(last, 0.0, next8[0:1, :])
    down = jnp.where(row == 0, prow, pltpu.roll(z, 1, 0))
    up = jnp.where(row == tb - 1, nrow, pltpu.roll(z, tb - 1, 0))
    return down, up


def _shift_conv_fwd(p, mu, conv_w, seq, *, name, tb=ROW_TILE):
    t = p.shape[0]
    per_seq = seq // tb

    def kern(p_ref, pp_ref, pn_ref, mu_ref, cw_ref, pss_ref, oc_ref):
        i = pl.program_id(0)
        first = (i % per_seq) == 0
        last = (i % per_seq) == per_seq - 1
        ps = p_ref[:, :D_SP]
        down, up = _neighbours(ps, pp_ref[:, :D_SP], pn_ref[:, :D_SP], first, last)
        pss_ref[...] = ps + mu_ref[...] * (0.5 * (down + up) - ps)
        gb = p_ref[:, D_SP:D_SP + D_CONV]
        u = p_ref[:, D_SP + D_CONV:D_SP + 2 * D_CONV] * p_ref[:, D_SP + 2 * D_CONV:]
        u_p = pp_ref[:, D_SP + D_CONV:D_SP + 2 * D_CONV] * pp_ref[:, D_SP + 2 * D_CONV:]
        u_n = pn_ref[:, D_SP + D_CONV:D_SP + 2 * D_CONV] * pn_ref[:, D_SP + 2 * D_CONV:]
        udown, uup = _neighbours(u, u_p, u_n, first, last)
        oc_ref[...] = gb * (cw_ref[0:1, :] * udown + cw_ref[1:2, :] * u + cw_ref[2:3, :] * uup)

    return pl.pallas_call(
        kern,
        out_shape=[jax.ShapeDtypeStruct((t, D_SP), F32), jax.ShapeDtypeStruct((t, D_CONV), F32)],
        grid=(t // tb,),
        in_specs=_halo_specs(D_INP, 0, tb, t) + [pl.BlockSpec((1, D_SP), lambda i: (0, 0)),
                                                 pl.BlockSpec((SUBLANES, D_CONV), lambda i: (0, 0))],
        out_specs=[pl.BlockSpec((tb, D_SP), lambda i: (i, 0)), pl.BlockSpec((tb, D_CONV), lambda i: (i, 0))],
        compiler_params=_params(("parallel",)), name=name)(p, p, p, mu, conv_w)


def _shift_conv_bwd(p, d_pss, d_o, mu, conv_w, seq, *, name, tb=ROW_TILE):
    t = p.shape[0]
    per_seq = seq // tb

    def kern(p_ref, pp_ref, pn_ref, d_ref, dp_ref, dn_ref, do_ref, dop_ref, don_ref, mu_ref, cw_ref,
             out_ref, dmu_ref, dcw_ref):
        i = pl.program_id(0)
        first = (i % per_seq) == 0
        last = (i % per_seq) == per_seq - 1

        @pl.when(i == 0)
        def _():
            dmu_ref[...] = jnp.zeros_like(dmu_ref)
            dcw_ref[...] = jnp.zeros_like(dcw_ref)

        mu_v = mu_ref[...]
        ps = p_ref[:, :D_SP]
        down, up = _neighbours(ps, pp_ref[:, :D_SP], pn_ref[:, :D_SP], first, last)
        d = d_ref[...]
        ddown, dup = _neighbours(d, dp_ref[...], dn_ref[...], first, last)
        out_ref[:, :D_SP] = (d - mu_v * d + 0.5 * (mu_v * ddown + mu_v * dup)).astype(out_ref.dtype)
        dmu_ref[...] += _colsum(d * (0.5 * (down + up) - ps))

        def parts(ref):
            return (ref[:, D_SP:D_SP + D_CONV], ref[:, D_SP + D_CONV:D_SP + 2 * D_CONV],
                    ref[:, D_SP + 2 * D_CONV:])

        gb, gc, hh = parts(p_ref)
        gb_p, gc_p, hh_p = parts(pp_ref)
        gb_n, gc_n, hh_n = parts(pn_ref)
        u = gc * hh
        udown, uup = _neighbours(u, gc_p * hh_p, gc_n * hh_n, first, last)
        cw0, cw1, cw2 = cw_ref[0:1, :], cw_ref[1:2, :], cw_ref[2:3, :]
        do = do_ref[...]
        duc = do * gb
        ducdown, ducup = _neighbours(duc, dop_ref[...] * gb_p, don_ref[...] * gb_n, first, last)
        du = cw0 * ducup + cw1 * duc + cw2 * ducdown
        out_ref[:, D_SP:D_SP + D_CONV] = (do * (cw0 * udown + cw1 * u + cw2 * uup)).astype(out_ref.dtype)
        out_ref[:, D_SP + D_CONV:D_SP + 2 * D_CONV] = (du * hh).astype(out_ref.dtype)
        out_ref[:, D_SP + 2 * D_CONV:] = (du * gc).astype(out_ref.dtype)
        dcw_ref[0:1, :] += _colsum(duc * udown)
        dcw_ref[1:2, :] += _colsum(duc * u)
        dcw_ref[2:3, :] += _colsum(duc * uup)

    return pl.pallas_call(
        kern,
        out_shape=[jax.ShapeDtypeStruct((t, D_INP), BF16), jax.ShapeDtypeStruct((1, D_SP), F32),
                   jax.ShapeDtypeStruct((SUBLANES, D_CONV), F32)],
        grid=(t // tb,),
        in_specs=(_halo_specs(D_INP, 0, tb, t) + _halo_specs(D_SP, 0, tb, t) + _halo_specs(D_CONV, 1, tb, t)
                  + [pl.BlockSpec((1, D_SP), lambda i: (0, 0)),
                     pl.BlockSpec((SUBLANES, D_CONV), lambda i: (0, 0))]),
        out_specs=[pl.BlockSpec((tb, D_INP), lambda i: (i, 0)), pl.BlockSpec((1, D_SP), lambda i: (0, 0)),
                   pl.BlockSpec((SUBLANES, D_CONV), lambda i: (0, 0))],
        compiler_params=_params(("arbitrary",)), name=name)(p, p, p, d_pss, d_pss, d_pss, d_o, d_o, d_o, mu, conv_w)


N_CHAIN = 16
N_GROUP = LANES // N_CHAIN
V_HI = HEAD // SUBLANES
G_KK, G_R, G_W, G_B, G_KD = 0, 1, (2, 3), (4, 5), (6, 7)


K_HI = HEAD // SUBLANES


def _tree_sum(terms):
    terms = list(terms)
    while len(terms) > 1:
        terms = [a + b for a, b in zip(terms[::2], terms[1::2])]
    return terms[0]


def _kscan_specs(nc):
    same = lambda c: c
    mirror = lambda c: nc - 1 - c

    def k_spec(fn):
        return pl.BlockSpec((SCAN_CHUNK, HEAD, LANES), lambda c: (fn(c), 0, 0))

    def v_spec(fn):
        return pl.BlockSpec((SCAN_CHUNK, SUBLANES, LANES), lambda c: (fn(c), 0, 0))

    return same, mirror, k_spec, v_spec


ST_SHAPE = (2, K_HI, V_HI, SUBLANES, LANES)


def _lane_group_index():
    lane = lax.broadcasted_iota(jnp.int32, (SUBLANES, LANES), 1)
    return lax.shift_right_logical(lane, jnp.full_like(lane, 4))


def _spread_groups(x, grp):
    rolled = [x] + [pltpu.roll(x, s * N_CHAIN, 1) for s in range(1, N_GROUP)]
    out = []
    for j in range(N_GROUP):
        t = rolled[(0 - j) % N_GROUP]
        for g in range(1, N_GROUP):
            t = jnp.where(grp == g, rolled[(g - j) % N_GROUP], t)
        out.append(t)
    return out


def _gather_groups(tiles, grp):
    total = None
    for s in range(N_GROUP):
        b = tiles[s % N_GROUP]
        for g in range(1, N_GROUP):
            b = jnp.where(grp == g, tiles[(g + s) % N_GROUP], b)
        b = pltpu.roll(b, s * N_CHAIN, 1) if s else b
        total = b if total is None else total + b
    return total


def _lane_group_sum(x):
    return _tree_sum([x] + [pltpu.roll(x, k * N_CHAIN, 1) for k in range(1, N_GROUP)])


def _key_row(x_t, grp, kh):
    r = SUBLANES * grp + kh
    return jnp.broadcast_to(x_t[r:r + 1, :], (SUBLANES, LANES))


def _acc(total, term):
    return term if total is None else total + term


SA_SHAPE = (2, V_HI, SUBLANES, LANES)


def _scan_fwd(xall, v_c, *, gather=(), name):
    steps = xall.shape[0]
    nc = steps // SCAN_CHUNK
    same, mirror, k_spec, v_spec = _kscan_specs(nc)
    last = SCAN_CHUNK - 1
    n_x = len(gather)

    def kern(*refs):
        xf_ref, xb_ref, vf_ref, vb_ref = refs[:4]
        yf_ref, yb_ref, hist_ref, fin_ref, sa_ref = refs[4 + n_x:9 + n_x]
        st_ref = refs[9 + 2 * n_x]
        c = pl.program_id(0)

        def riders():
            return _exchange_copies(refs[4:4 + n_x], refs[9 + n_x:9 + 2 * n_x], 0, *refs[10 + 2 * n_x:])

        @pl.when(c == 0)
        def _():
            st_ref[...] = jnp.zeros_like(st_ref)
            if n_x:
                for cp in riders():
                    cp.start()

        hist_ref[0] = st_ref[...]
        grp = _lane_group_index()

        def body(i, put):
            j = last - i
            for d, (x_t, v_t, y_ref, at) in enumerate(((xf_ref[i], vf_ref[i], yf_ref, i),
                                                       (xb_ref[j], vb_ref[j], yb_ref, j))):
                v_b = _spread_groups(v_t, grp)
                part = [None] * V_HI
                for kh in range(K_HI):
                    kk_r = _key_row(x_t, G_KK, kh)
                    for vh in range(V_HI):
                        part[vh] = _acc(part[vh], hist_ref[i, d, kh, vh] * kk_r)
                sa = [_lane_group_sum(p) for p in part]
                for vh in range(V_HI):
                    sa_ref[i, d, vh] = sa[vh]
                y_p = [None] * V_HI
                for kh in range(K_HI):
                    r_r, w_r = _key_row(x_t, G_R, kh), _key_row(x_t, G_W[d], kh)
                    b_r, kd_r = _key_row(x_t, G_B[d], kh), _key_row(x_t, G_KD[d], kh)
                    for vh in range(V_HI):
                        new = hist_ref[i, d, kh, vh] * w_r - sa[vh] * b_r + v_b[vh] * kd_r
                        put(d, kh, vh, new)
                        y_p[vh] = _acc(y_p[vh], new * r_r)
                y_ref[at] = _gather_groups(y_p, grp)

        def step(i, carry):
            def put(d, kh, vh, val):
                hist_ref[i + 1, d, kh, vh] = val
            body(i, put)
            return carry

        lax.fori_loop(0, last, step, 0, unroll=SCAN_UNROLL)

        def put_carry(d, kh, vh, val):
            st_ref[d, kh, vh] = val

        body(last, put_carry)

        @pl.when(c == nc - 1)
        def _():
            fin_ref[...] = st_ref[...]
            if n_x:
                for cp in riders():
                    cp.wait()

    return pl.pallas_call(
        kern,
        out_shape=[jax.ShapeDtypeStruct((steps, SUBLANES, LANES), F32)] * 2
        + [jax.ShapeDtypeStruct((steps,) + ST_SHAPE, F32), jax.ShapeDtypeStruct(ST_SHAPE, F32),
           jax.ShapeDtypeStruct((steps,) + SA_SHAPE, F32)]
        + _exchange_out_shapes(gather, 0),
        grid=(nc,), in_specs=[k_spec(same), k_spec(mirror), v_spec(same), v_spec(mirror)] + _hbm_specs(n_x),
        out_specs=[v_spec(same), v_spec(mirror),
                   pl.BlockSpec((SCAN_CHUNK,) + ST_SHAPE, lambda c: (c, 0, 0, 0, 0, 0)),
                   pl.BlockSpec(ST_SHAPE, lambda c: (0, 0, 0, 0, 0)),
                   pl.BlockSpec((SCAN_CHUNK,) + SA_SHAPE, lambda c: (c, 0, 0, 0, 0))] + _hbm_specs(n_x),
        scratch_shapes=[pltpu.VMEM(ST_SHAPE, F32)] + (_exchange_sems(n_x) if n_x else []),
        compiler_params=_params(("arbitrary",), SCAN_VMEM_LIMIT), name=name)(xall, xall, v_c, v_c, *gather)


def _scan_bwd(xall, v_c, dy_c, hist, fin, sa, *, exchange=(), name):
    steps = xall.shape[0]
    nc = steps // SCAN_CHUNK
    same, back, k_spec, v_spec = _kscan_specs(nc)
    last = SCAN_CHUNK - 1
    n_x = len(exchange)

    def kern(*refs):
        xf_ref, xb_ref, vf_ref, vb_ref, dyf_ref, dyb_ref, hist_ref, fin_ref, sa_ref = refs[:9]
        gf_ref, gb_ref, dvf_ref, dvb_ref = refs[9 + n_x:13 + n_x]
        ds_ref, after_ref = refs[13 + 2 * n_x:15 + 2 * n_x]
        c = pl.program_id(0)

        def riders():
            return _exchange_copies(refs[9:9 + n_x], refs[13 + n_x:13 + 2 * n_x], n_x, *refs[15 + 2 * n_x:])

        @pl.when(c == 0)
        def _():
            ds_ref[...] = jnp.zeros_like(ds_ref)
            after_ref[...] = fin_ref[...]
            if n_x:
                for cp in riders():
                    cp.start()

        grp = _lane_group_index()
        row = lax.broadcasted_iota(jnp.int32, (SUBLANES, LANES), 0)
        zero = jnp.zeros((SUBLANES, LANES), F32)

        def body(i, after):
            j = last - i
            for d, (x_t, v_t, dy_t, g_ref, dv_ref, at) in enumerate((
                    (xf_ref[i], vf_ref[i], dyf_ref[i], gf_ref, dvf_ref, i),
                    (xb_ref[j], vb_ref[j], dyb_ref[j], gb_ref, dvb_ref, j))):
                v_s, dy_s = _spread_groups(v_t, grp), _spread_groups(dy_t, grp)
                dsa_p, dv_p = [None] * V_HI, [None] * V_HI
                for kh in range(K_HI):
                    r_r = _key_row(x_t, G_R, kh)
                    b_r, kd_r = _key_row(x_t, G_B[d], kh), _key_row(x_t, G_KD[d], kh)
                    for vh in range(V_HI):
                        g = ds_ref[d, kh, vh] + dy_s[vh] * r_r
                        ds_ref[d, kh, vh] = g
                        dsa_p[vh] = _acc(dsa_p[vh], g * b_r)
                        dv_p[vh] = _acc(dv_p[vh], g * kd_r)
                dsa = [-_lane_group_sum(p) for p in dsa_p]
                sa = [sa_ref[i, d, vh] for vh in range(V_HI)]
                dv_ref[at] = _gather_groups(dv_p, grp)
                blocks = {G_KK: zero, G_R: zero, G_W[d]: zero, G_B[d]: zero, G_KD[d]: zero}
                for kh in range(K_HI):
                    w_r, kk_r = _key_row(x_t, G_W[d], kh), _key_row(x_t, G_KK, kh)
                    dkk = dr = dw = db = dkd = None
                    for vh in range(V_HI):
                        g, before = ds_ref[d, kh, vh], hist_ref[i, d, kh, vh]
                        dr = _acc(dr, after(d, kh, vh) * dy_s[vh])
                        dw = _acc(dw, g * before)
                        dkd = _acc(dkd, g * v_s[vh])
                        db = _acc(db, g * sa[vh])
                        dkk = _acc(dkk, before * dsa[vh])
                        ds_ref[d, kh, vh] = g * w_r + dsa[vh] * kk_r
                    for gi, a in ((G_KK, dkk), (G_R, dr), (G_W[d], dw), (G_B[d], -db), (G_KD[d], dkd)):
                        blocks[gi] = jnp.where(row == kh, _colsum(a), blocks[gi])
                for gi in range(N_GROUP):
                    g_ref[at, SUBLANES * gi:SUBLANES * (gi + 1), :] = blocks.get(gi, zero)

        body(last, lambda d, kh, vh: after_ref[d, kh, vh])

        def step(ii, carry):
            i = last - ii
            body(i, lambda d, kh, vh: hist_ref[i + 1, d, kh, vh])
            return carry

        lax.fori_loop(1, SCAN_CHUNK, step, 0, unroll=SCAN_UNROLL)
        after_ref[...] = hist_ref[0]

        if n_x:
            @pl.when(c == nc - 1)
            def _():
                for cp in riders():
                    cp.wait()

    return pl.pallas_call(
        kern,
        out_shape=[jax.ShapeDtypeStruct((steps, HEAD, LANES), F32)] * 2
        + [jax.ShapeDtypeStruct((steps, SUBLANES, LANES), F32)] * 2 + _exchange_out_shapes(exchange, n_x),
        grid=(nc,),
        in_specs=[k_spec(back), k_spec(same), v_spec(back), v_spec(same), v_spec(back), v_spec(same),
                  pl.BlockSpec((SCAN_CHUNK,) + ST_SHAPE, lambda c: (back(c), 0, 0, 0, 0, 0)),
                  pl.BlockSpec(ST_SHAPE, lambda c: (0, 0, 0, 0, 0)),
                  pl.BlockSpec((SCAN_CHUNK,) + SA_SHAPE, lambda c: (back(c), 0, 0, 0, 0))] + _hbm_specs(n_x),
        out_specs=[k_spec(back), k_spec(same), v_spec(back), v_spec(same)] + _hbm_specs(n_x),
        scratch_shapes=[pltpu.VMEM(ST_SHAPE, F32), pltpu.VMEM(ST_SHAPE, F32)]
        + (_exchange_sems(n_x) if n_x else []),
        compiler_params=_params(("arbitrary",), SCAN_VMEM_LIMIT), name=name)(xall, xall, v_c, v_c, dy_c, dy_c, hist, fin, sa,
                                                            *exchange)


def _bf16_pieces(x):
    hi = x.astype(BF16)
    return hi, (x - hi.astype(F32)).astype(BF16)


def _to_key_rows(wide, bsz, seq, *, name):
    assert bsz == 2
    perm = _key_row_maps()
    tt = min(RELAYOUT_TILE, seq)
    per_seq = seq // tt

    def kern(x0_ref, x1_ref, p0_ref, p1_ref, o_ref):
        total = None
        for x_ref, p_ref in ((x0_ref, p0_ref), (x1_ref, p1_ref)):
            for piece in _bf16_pieces(x_ref[...]):
                term = jnp.dot(piece, p_ref[...], preferred_element_type=F32)
                total = term if total is None else total + term
        for r in range(K_HI):
            o_ref[:, r, :] = total[:, LANES * r:LANES * (r + 1)]

    p_spec = pl.BlockSpec((D_RWKV, K_HI * LANES), lambda i, a: (0, 0))
    return pl.pallas_call(
        kern, out_shape=jax.ShapeDtypeStruct((seq, HEAD, LANES), F32), grid=(per_seq, N_GROUP),
        in_specs=[pl.BlockSpec((tt, D_RWKV), lambda i, a: (i, a)),
                  pl.BlockSpec((tt, D_RWKV), lambda i, a: (per_seq + i, a)), p_spec, p_spec],
        out_specs=pl.BlockSpec((tt, K_HI, LANES), lambda i, a: (i, a, 0)),
        compiler_params=_params(("parallel", "parallel")), name=name)(wide, wide, *perm)


def _key_row_maps():
    src = jnp.arange(D_RWKV)
    head, kh, kl = src // HEAD, (src // SUBLANES) % K_HI, src % SUBLANES
    dst = jnp.arange(K_HI * LANES)
    return [((kh[:, None] == dst[None, :] // LANES) & (kl[:, None] == (dst[None, :] // N_CHAIN) % SUBLANES)
             & ((dst[None, :] // N_HEAD) % 2 == b) & (head[:, None] == dst[None, :] % N_HEAD)).astype(BF16)
            for b in range(2)]


def _from_key_rows(g_f, g_b, bsz, seq, *, name):
    assert bsz == 2
    maps = jnp.concatenate([m.T for m in _key_row_maps()], axis=1)
    tt = min(RELAYOUT_TILE, seq)
    per_seq = seq // tt

    def kern(gf_ref, gb_ref, q_ref, o_ref):
        a = pl.program_id(1)
        shared = a <= G_R
        from_f = shared | (a % 2 == G_W[0] % 2)

        def rearranged(g_ref):
            g = jnp.concatenate([g_ref[:, r, :] for r in range(K_HI)], axis=1)
            hi, mid = (jnp.dot(piece, q_ref[...], preferred_element_type=F32) for piece in _bf16_pieces(g))
            both = hi + mid
            return both[:, :D_RWKV], both[:, D_RWKV:]

        @pl.when(from_f)
        def _():
            o_ref[0], o_ref[1] = rearranged(gf_ref)

        @pl.when(jnp.logical_not(from_f))
        def _():
            o_ref[0], o_ref[1] = rearranged(gb_ref)

        @pl.when(shared)
        def _():
            more = rearranged(gb_ref)
            o_ref[0] += more[0]
            o_ref[1] += more[1]

    g_spec = pl.BlockSpec((tt, K_HI, LANES), lambda i, a: (i, a, 0))
    out = pl.pallas_call(
        kern, out_shape=jax.ShapeDtypeStruct((bsz, seq, N_GROUP * D_RWKV), F32), grid=(per_seq, N_GROUP),
        in_specs=[g_spec, g_spec, pl.BlockSpec((K_HI * LANES, bsz * D_RWKV), lambda i, a: (0, 0))],
        out_specs=pl.BlockSpec((bsz, tt, D_RWKV), lambda i, a: (0, i, a)),
        compiler_params=_params(("parallel", "parallel")), name=name)(g_f, g_b, maps)
    return out.reshape(bsz * seq, N_GROUP * D_RWKV)


def _to_value_rows(a, bsz, seq):
    z = a.reshape(bsz, seq, N_HEAD, V_HI, SUBLANES).transpose(1, 4, 3, 0, 2)
    return z.reshape(seq, SUBLANES, LANES)


def _from_value_rows(y, bsz, seq):
    z = y.reshape(seq, SUBLANES, V_HI, bsz, N_HEAD).transpose(3, 0, 4, 2, 1)
    return z.reshape(bsz * seq, D_RWKV)


def _pad_cols(a, segs):
    out, off = [], 0
    for w, wp in segs:
        out.append(a[..., off:off + w])
        if wp > w:
            out.append(jnp.zeros(a.shape[:-1] + (wp - w,), a.dtype))
        off += w
    return jnp.concatenate(out, axis=-1)


def _unpad_cols(a, segs):
    out, off = [], 0
    for w, wp in segs:
        out.append(a[..., off:off + w])
        off += wp
    return jnp.concatenate(out, axis=-1)


P_SEGS = ((3 * D_RWKV, 3 * D_RWKV), (D_LORA, 128), (D_LORA, 128), (D_GATE, 256), (3 * D_CONV, 3 * D_CONV))
S_SEGS = P_SEGS[:4]


def _pad_rows(a, rows):
    return jnp.concatenate([a, jnp.zeros((rows - a.shape[0], a.shape[1]), a.dtype)], axis=0)


LATE = ("w_out", "w_gate", "w_up", "w_down")


def _local_step(x, target, w, late=None):
    bsz, seq, _ = x.shape
    t = bsz * seq
    x2d = x.reshape(t, D_MODEL)
    tg2d = target.reshape(t, D_MODEL)
    row = lambda a: a.reshape(1, -1).astype(F32)

    w_in = _pad_cols(w["w_in"][0], P_SEGS)
    mu = _pad_cols(row(w["mu_shift"]), S_SEGS)
    wupf, wupb, aupf, aupb = (_pad_rows(w[n][0].astype(F32), 128) for n in ("w_up_f", "w_up_b", "a_up_f", "a_up_b"))
    gup = _pad_rows(w["g_up"][0].astype(F32), 256)
    conv_w = _pad_rows(w["conv_w"][0].astype(F32), SUBLANES)
    norm1, norm2, normf = row(w["norm1_w"]), row(w["norm2_w"]), row(w["norm_f_w"])
    vec = {n: row(w[n]) for n in VEC}
    head_of = jnp.arange(LANES) // HEAD
    bd = (head_of[:, None] == head_of[None, :]).astype(F32)
    pre_consts = [vec["k_k"], vec["w0_f"], vec["w0_b"], vec["a0_f"], vec["a0_b"], vec["k_a_f"], vec["k_a_b"],
                  wupf, wupb, aupf, aupb, gup, bd]
    post_consts = [vec["gn_w"], vec["gn_b"], vec["r_k_f"], vec["r_k_b"], bd]

    h1, = _rowwise(_rms, [x2d], [norm1], [D_MODEL], [], name="rms1_fwd", out_dtype=BF16, tb=WIDE_TILE)
    p = _mm(h1, w_in, name="mm_in")
    pss, oconv = _shift_conv_fwd(p, mu, conv_w, seq, name="shift_conv_fwd", tb=min(2 * ROW_TILE, seq))
    pre_rows = [(pss, 0, 512), (pss, 1, 512), (pss, XW0 // 128, 128), (pss, XA0 // 128, 128), (pss, XG0 // 256, 256)]
    sc, g = _rowwise(_prescan_math, pre_rows, pre_consts, [[D_RWKV] * N_GROUP, D_RWKV], [], name="prescan_fwd",
                     tb=2 * ROW_TILE)
    xall = _to_key_rows(sc, bsz, seq, name="to_key_rows")
    v_l = _to_value_rows(pss[:, 2 * D_RWKV:3 * D_RWKV], bsz, seq)
    y_f, y_b, hist, fin, sa, *gathered = _scan_fwd(xall, v_l, gather=[late[n] for n in LATE] if late else (),
                                                   name="scan_fwd")
    w_out, w_gate, w_up, w_down = (
        (_from_slots(a, SHARD_AXIS[n]) if late else w[n])[0] for n, a in zip(LATE, gathered or LATE))
    y = _from_value_rows(y_f + y_b, bsz, seq)
    post_rows = [y, (pss, 0, 512), (pss, 2, 512), (sc, G_KD[0], 512), (sc, G_KD[1], 512), g]

    def post_fwd(y_, r_, v_, kdf_, kdb_, g_, oc_, *consts):
        return _postscan_math(y_, r_, v_, kdf_, kdb_, g_, *consts), oc_

    o, = _rowwise(post_fwd, post_rows + [oconv], post_consts, [[D_RWKV, D_CONV]], [], name="postscan_fwd",
                  out_dtype=BF16, tb=2 * ROW_TILE)
    x1 = _mm(o, w_out, add=x2d, name="mm_out")
    h2, = _rowwise(_rms, [x1], [norm2], [D_MODEL], [], name="rms2_fwd", out_dtype=BF16, tb=WIDE_TILE)
    gg, uu, ff = _mm_swiglu(h2, w_gate, w_up, name="mm_gate_up")
    x2 = _mm(ff, w_down, add=x1, name="mm_down")

    def final(x_, tg_, wn_):
        yo, vjp = jax.vjp(_rms, x_, wn_)
        err = yo - tg_
        dx_, dwn_ = vjp(err * (1.0 / D_MODEL))
        part = jnp.sum(jnp.sum(err * err, axis=1, keepdims=True), axis=0, keepdims=True) * (0.5 / D_MODEL)
        return dx_, part + jnp.zeros((1, LANES), F32), dwn_

    dx2, loss_acc, d_normf = _rowwise(final, [x2, tg2d], [normf], [D_MODEL], [(1, LANES), (1, D_MODEL)],
                                      name="loss_head", tb=WIDE_TILE)
    dgg, duu = _mm_swiglu_bwd(dx2, w_down, gg, uu, name="mm_down_dx")
    g_w_down = _mm(ff, dx2, ta=True, name="mm_down_dw")
    dh2 = _mm(dgg, w_gate, tb=True, name="mm_gate_dx")
    dh2 = _mm(duu, w_up, tb=True, add=dh2, name="mm_up_dx")
    g_w_gate = _mm(h2, dgg, ta=True, name="mm_gate_dw")
    g_w_up = _mm(h2, duu, ta=True, name="mm_up_dw")

    def rms_bwd(x_, dh_, dres_, wn_):
        _, vjp = jax.vjp(_rms, x_, wn_)
        dx_, dwn_ = vjp(dh_)
        return dx_ + dres_, dwn_

    dx1, d_norm2 = _rowwise(rms_bwd, [x1, dh2, dx2], [norm2], [D_MODEL], [(1, D_MODEL)], name="rms2_bwd", tb=WIDE_TILE)
    do = _mm(dx1, w_out, tb=True, name="mm_out_dx")
    g_w_out = _mm(o, dx1, ta=True, name="mm_out_dw")

    def post_bwd(y_, r_, v_, kdf_, kdb_, g_, do_, *consts):
        _, vjp = jax.vjp(lambda *a: _postscan_math(*a, consts[4]), y_, r_, v_, kdf_, kdb_, g_, *consts[:4])
        return vjp(do_)

    (dy, dr_c, dv_c, dkdf_c, dkdb_c, dg, d_gn_w, d_gn_b, d_rkf, d_rkb) = _rowwise(
        post_bwd, post_rows + [(do, 0, 512)], post_consts, [D_RWKV] * 6, [(1, D_RWKV)] * 4, name="postscan_bwd",
        tb=2 * ROW_TILE)
    dy_l = _to_value_rows(dy, bsz, seq)
    late_grads = {"w_out": g_w_out[None], "w_gate": g_w_gate[None], "w_up": g_w_up[None], "w_down": g_w_down[None]}
    g_f, g_b, dv_f, dv_b, *late_parts = _scan_bwd(
        xall, v_l, dy_l, hist, fin, sa, name="scan_bwd",
        exchange=[_to_slots(late_grads[n], SHARD_AXIS[n]).astype(BF16) for n in LATE] if late else ())
    dsc = _from_key_rows(g_f, g_b, bsz, seq, name="from_key_rows")
    dv_s = _from_value_rows(dv_f + dv_b, bsz, seq)

    def pre_bwd(r_, k_, xw_, xa_, xg_, dkk_, dr_s, dwf_, dwb_, dbf_, dbb_, dkdf_s, dkdb_s,
                dr_c_, dv_c_, dv_s_, dkdf_c_, dkdb_c_, dg_, *consts):
        _, vjp = jax.vjp(lambda *a: _prescan_math(*a, consts[-1]), r_, k_, xw_, xa_, xg_, *consts[:-1])
        grads = vjp((dkk_, dr_s + dr_c_, dwf_, dwb_, dbf_, dbb_, dkdf_s + dkdf_c_, dkdb_s + dkdb_c_, dg_))
        dr_, dk_, dxw_, dxa_, dxg_ = grads[:5]
        return (dr_, dk_, dv_c_ + dv_s_, dxw_, dxa_, dxg_) + tuple(grads[5:])

    pre_b_rows = (pre_rows + [(dsc, j, 512) for j in range(N_GROUP)]
                  + [dr_c, dv_c, dv_s, dkdf_c, dkdb_c, dg])
    pre_b = _rowwise(pre_bwd, pre_b_rows, pre_consts, [[512, 512, 512, 128, 128, 256]],
                     [(1, D_RWKV)] * 7 + [(128, D_RWKV)] * 4 + [(256, D_RWKV)], name="prescan_bwd",
                     tb=2 * ROW_TILE)
    d_pss = pre_b[0]
    d_kk_, d_w0f, d_w0b, d_a0f, d_a0b, d_kaf, d_kab, d_wupf, d_wupb, d_aupf, d_aupb, d_gup = pre_b[1:]
    dp, d_mu, d_conv = _shift_conv_bwd(p, d_pss, do, mu, conv_w, seq, name="shift_conv_bwd",
                                       tb=min(2 * ROW_TILE, seq))
    g_w_in = _mm(h1, dp, ta=True, name="mm_in_dw")
    grads = {
        "w_in": _unpad_cols(g_w_in, P_SEGS)[None], "mu_shift": _unpad_cols(d_mu, S_SEGS),
        "w_up_f": d_wupf[None, :D_LORA], "w0_f": d_w0f, "w_up_b": d_wupb[None, :D_LORA], "w0_b": d_w0b,
        "a_up_f": d_aupf[None, :D_LORA], "a0_f": d_a0f, "a_up_b": d_aupb[None, :D_LORA], "a0_b": d_a0b,
        "g_up": d_gup[None, :D_GATE], "k_k": d_kk_, "k_a_f": d_kaf, "k_a_b": d_kab,
        "r_k_f": d_rkf, "r_k_b": d_rkb, "gn_w": d_gn_w, "gn_b": d_gn_b, "conv_w": d_conv[None, :3],
        "w_out": g_w_out[None], "norm2_w": d_norm2, "w_gate": g_w_gate[None], "w_up": g_w_up[None],
        "w_down": g_w_down[None], "norm_f_w": d_normf,
    }
    early = ("w_in",) + LORA
    parts = dict(zip(LATE, late_parts))
    if late:
        vec_rows = jnp.concatenate([grads[n] for n in VEC] + [jnp.zeros((16 - len(VEC), D_RWKV), F32)], axis=0)
        slots = [_to_slots(grads[n], SHARD_AXIS[n]).astype(BF16 if n in BIG else F32) for n in early]
        dh1, *recv = _mm(dp, w_in, tb=True, exchange=(slots, [vec_rows]), name="mm_in_dx")
        parts.update(zip(early + ("vec",), recv))
    else:
        dh1 = _mm(dp, w_in, tb=True, name="mm_in_dx")
    dx, grads["norm1_w"] = _rowwise(rms_bwd, [x2d, dh1, dx1], [norm1], [D_MODEL], [(1, D_MODEL)], name="rms1_bwd",
                                    tb=WIDE_TILE)
    return loss_acc, dx.reshape(bsz, seq, D_MODEL), grads, parts


def _hbm_specs(n):
    return [pl.BlockSpec(memory_space=pl.ANY)] * n


def _all_gather(arrs, *, name):
    n = len(arrs)

    def body(*refs):
        x_refs, out_refs = refs[:n], refs[n:2 * n]
        send_sems, recv_sems, local_sems = refs[2 * n:]
        x, y, c = lax.axis_index("x"), lax.axis_index("y"), lax.axis_index("c")
        me, sibling = (x, y, c), (x, y, 1 - c)
        chips = [(1 - x, y), (x, 1 - y), (1 - x, 1 - y)]

        def slot(a, px, py, pc):
            return out_refs[a].at[4 * px + 2 * py + pc]

        def copy(a, k, block, to, src=None):
            return pltpu.make_async_remote_copy(
                src_ref=slot(a, *block) if src is None else src, dst_ref=slot(a, *block),
                send_sem=send_sems.at[k, a], recv_sem=recv_sems.at[k, a],
                device_id=to, device_id_type=pl.DeviceIdType.MESH)

        mine = [pltpu.make_async_copy(x_refs[a], slot(a, *me), local_sems.at[a]) for a in range(n)]
        for cp in mine:
            cp.start()
        first = []
        for a in range(n):
            first.append(copy(a, 0, me, sibling, src=x_refs[a]))
            first += [copy(a, 1 + j, me, (*chip, c), src=x_refs[a]) for j, chip in enumerate(chips)]
        for cp in first:
            cp.start()
        passed = []
        for j, chip in enumerate(chips):
            for a in range(n):
                copy(a, 1 + j, (*chip, c), me).wait_recv()
                cp = copy(a, 4 + j, (*chip, c), sibling)
                cp.start()
                passed.append(cp)
        for a in range(n):
            copy(a, 0, sibling, me).wait_recv()
            for j, chip in enumerate(chips):
                copy(a, 4 + j, (*chip, 1 - c), me).wait_recv()
        for cp in first + passed:
            cp.wait_send()
        for cp in mine:
            cp.wait()

    return pl.pallas_call(
        body, out_shape=[jax.ShapeDtypeStruct((N_DEV,) + a.shape, a.dtype) for a in arrs],
        in_specs=_hbm_specs(n), out_specs=_hbm_specs(n),
        scratch_shapes=[pltpu.SemaphoreType.DMA((7, n)), pltpu.SemaphoreType.DMA((7, n)),
                        pltpu.SemaphoreType.DMA((n,))],
        name=name)(*arrs)


def _exchange(sliced, whole, *, name):
    arrs = list(sliced) + list(whole)
    n, n_sliced = len(arrs), len(sliced)

    def body(*refs):
        copies = _exchange_copies(refs[:n], refs[n:2 * n], n_sliced, *refs[2 * n:])
        for cp in copies:
            cp.start()
        for cp in copies:
            cp.wait()

    return pl.pallas_call(
        body, out_shape=_exchange_out_shapes(arrs, n_sliced), in_specs=_hbm_specs(n), out_specs=_hbm_specs(n),
        scratch_shapes=_exchange_sems(n), name=name)(*arrs)


def _exchange_out_shapes(arrs, n_sliced):
    return [jax.ShapeDtypeStruct(a.shape if i < n_sliced else (N_DEV,) + a.shape, a.dtype)
            for i, a in enumerate(arrs)]


def _exchange_sems(n):
    return [pltpu.SemaphoreType.DMA((7, n)), pltpu.SemaphoreType.DMA((7, n)), pltpu.SemaphoreType.DMA((n,))]


def _exchange_copies(in_refs, out_refs, n_sliced, send_sems, recv_sems, local_sems):
    n = len(in_refs)
    x, y, c = lax.axis_index("x"), lax.axis_index("y"), lax.axis_index("c")
    me = 4 * x + 2 * y + c

    def src(a, dev):
        return in_refs[a].at[dev] if a < n_sliced else in_refs[a]

    copies = [pltpu.make_async_copy(src(a, me), out_refs[a].at[me], local_sems.at[a]) for a in range(n)]
    for k in range(1, N_DEV):
        px = 1 - x if k & 4 else x
        py = 1 - y if k & 2 else y
        pc = 1 - c if k & 1 else c
        for a in range(n):
            copies.append(pltpu.make_async_remote_copy(
                src_ref=src(a, 4 * px + 2 * py + pc), dst_ref=out_refs[a].at[me],
                send_sem=send_sems.at[k - 1, a], recv_sem=recv_sems.at[k - 1, a],
                device_id=(px, py, pc), device_id_type=pl.DeviceIdType.MESH))
    return copies


def _adam_math(g, w, m, v):
    nm = ADAM_B1 * m + (1.0 - ADAM_B1) * g
    nv = ADAM_B2 * v + (1.0 - ADAM_B2) * (g * g)
    m_hat = nm / (1.0 - ADAM_B1 ** ADAM_STEP)
    v_hat = nv / (1.0 - ADAM_B2 ** ADAM_STEP)
    return -ADAM_LR * (m_hat / (jnp.sqrt(v_hat) + ADAM_EPS) + ADAM_WD * w), nm, nv


def _slot_sum(ref):
    g = ref[0].astype(F32)
    for s in range(1, N_DEV):
        g = g + ref[s].astype(F32)
    return g


def _adamw_big(parts, w, m, v, *, name):
    _, rws, cols = w.shape
    tr = _tile(rws, (256, 176, 128))

    def kern(p_ref, w_ref, m_ref, v_ref, g_ref, d_ref, nm_ref, nv_ref):
        g = _slot_sum(p_ref)
        g_ref[...] = g
        d_ref[...], nm_ref[...], nv_ref[...] = _adam_math(g, w_ref[...], m_ref[...], v_ref[...])

    spec = pl.BlockSpec((1, tr, cols), lambda i: (0, i, 0))
    return pl.pallas_call(
        kern, out_shape=[jax.ShapeDtypeStruct(w.shape, F32)] * 4, grid=(rws // tr,),
        in_specs=[pl.BlockSpec((N_DEV, 1, tr, cols), lambda i: (0, 0, i, 0)), spec, spec, spec],
        out_specs=[spec] * 4, compiler_params=_params(("parallel",)), name=name)(parts, w, m, v)


def _adamw_small(lora_parts, vec_parts, wide_parts, wmv, *, name):
    names = LORA + VEC + WIDE
    n_l, n = len(LORA), len(names)
    flat = [a for trip in wmv for a in trip]

    def kern(*refs):
        l_refs, vec_ref, wide_ref = refs[:n_l], refs[n_l], refs[n_l + 1]
        in_refs = refs[n_l + 2:n_l + 2 + 3 * n]
        out_refs = refs[n_l + 2 + 3 * n:]
        vec_sum, wide_sum = _slot_sum(vec_ref), _slot_sum(wide_ref)
        for i, nm in enumerate(names):
            w_ref, m_ref, v_ref = in_refs[3 * i:3 * i + 3]
            if i < n_l:
                g = _slot_sum(l_refs[i])
            elif nm in VEC:
                g = vec_sum[i - n_l:i - n_l + 1, :]
            else:
                g = wide_sum[WIDE.index(nm):WIDE.index(nm) + 1, :w_ref.shape[-1]]
            o = out_refs[4 * i:4 * i + 4]
            o[0][...] = g
            o[1][...], o[2][...], o[3][...] = _adam_math(g, w_ref[...], m_ref[...], v_ref[...])

    out_shape = [jax.ShapeDtypeStruct(trip[0].shape, F32) for trip in wmv for _ in range(4)]
    outs = pl.pallas_call(kern, out_shape=out_shape, name=name,
                          compiler_params=pltpu.CompilerParams(vmem_limit_bytes=VMEM_LIMIT))(
        *lora_parts, vec_parts, wide_parts, *flat)
    return [tuple(outs[4 * i:4 * i + 4]) for i in range(n)]


def _to_slots(g, axis):
    _, rws, cols = g.shape
    if axis == 1:
        return g.reshape(N_DEV, 1, rws // N_DEV, cols)
    return g.reshape(1, rws, N_DEV, cols // N_DEV).transpose(2, 0, 1, 3)


def _from_slots(got, axis):
    _, _, rws, cols = got.shape
    if axis == 1:
        return got.reshape(1, N_DEV * rws, cols)
    return got.transpose(1, 2, 0, 3).reshape(1, rws, N_DEV * cols)


def _pad_lanes(a, width):
    return jnp.concatenate([a, jnp.zeros(a.shape[:-1] + (width - a.shape[-1],), a.dtype)], axis=-1)


def kernel(x, norm1_w, w_in, mu_shift, w_up_f, w0_f, w_up_b, w0_b, a_up_f, a0_f, a_up_b, a0_b, g_up, k_k, k_a_f, k_a_b, r_k_f, r_k_b, gn_w, gn_b, conv_w, w_out, norm2_w, w_gate, w_up, w_down, norm_f_w, loss_target, m_norm1_w, m_w_in, m_mu_shift, m_w_up_f, m_w0_f, m_w_up_b, m_w0_b, m_a_up_f, m_a0_f, m_a_up_b, m_a0_b, m_g_up, m_k_k, m_k_a_f, m_k_a_b, m_r_k_f, m_r_k_b, m_gn_w, m_gn_b, m_conv_w, m_w_out, m_norm2_w, m_w_gate, m_w_up, m_w_down, m_norm_f_w, v_norm1_w, v_w_in, v_mu_shift, v_w_up_f, v_w0_f, v_w_up_b, v_w0_b, v_a_up_f, v_a0_f, v_a_up_b, v_a0_b, v_g_up, v_k_k, v_k_a_f, v_k_a_b, v_r_k_f, v_r_k_b, v_gn_w, v_gn_b, v_conv_w, v_w_out, v_norm2_w, v_w_gate, v_w_up, v_w_down, v_norm_f_w):
    local = dict(norm1_w=norm1_w, w_in=w_in, mu_shift=mu_shift, w_up_f=w_up_f, w0_f=w0_f, w_up_b=w_up_b,
                 w0_b=w0_b, a_up_f=a_up_f, a0_f=a0_f, a_up_b=a_up_b, a0_b=a0_b, g_up=g_up, k_k=k_k, k_a_f=k_a_f,
                 k_a_b=k_a_b, r_k_f=r_k_f, r_k_b=r_k_b, gn_w=gn_w, gn_b=gn_b, conv_w=conv_w, w_out=w_out,
                 norm2_w=norm2_w, w_gate=w_gate, w_up=w_up, w_down=w_down, norm_f_w=norm_f_w)
    mom_m = dict(norm1_w=m_norm1_w, w_in=m_w_in, mu_shift=m_mu_shift, w_up_f=m_w_up_f, w0_f=m_w0_f,
                 w_up_b=m_w_up_b, w0_b=m_w0_b, a_up_f=m_a_up_f, a0_f=m_a0_f, a_up_b=m_a_up_b, a0_b=m_a0_b,
                 g_up=m_g_up, k_k=m_k_k, k_a_f=m_k_a_f, k_a_b=m_k_a_b, r_k_f=m_r_k_f, r_k_b=m_r_k_b,
                 gn_w=m_gn_w, gn_b=m_gn_b, conv_w=m_conv_w, w_out=m_w_out, norm2_w=m_norm2_w, w_gate=m_w_gate,
                 w_up=m_w_up, w_down=m_w_down, norm_f_w=m_norm_f_w)
    mom_v = dict(norm1_w=v_norm1_w, w_in=v_w_in, mu_shift=v_mu_shift, w_up_f=v_w_up_f, w0_f=v_w0_f,
                 w_up_b=v_w_up_b, w0_b=v_w0_b, a_up_f=v_a_up_f, a0_f=v_a0_f, a_up_b=v_a_up_b, a0_b=v_a0_b,
                 g_up=v_g_up, k_k=v_k_k, k_a_f=v_k_a_f, k_a_b=v_k_a_b, r_k_f=v_r_k_f, r_k_b=v_r_k_b,
                 gn_w=v_gn_w, gn_b=v_gn_b, conv_w=v_conv_w, w_out=v_w_out, norm2_w=v_norm2_w, w_gate=v_w_gate,
                 w_up=v_w_up, w_down=v_w_down, norm_f_w=v_norm_f_w)

    early = ("w_in",) + LORA
    got = _all_gather([local["w_in"].astype(BF16)] + [local[n] for n in LORA], name="gather")
    full = dict(local)
    full.update({n: _from_slots(a, SHARD_AXIS[n]) for n, a in zip(early, got)})

    loss_part, grad_x, grads, parts = _local_step(x, loss_target, full,
                                                  late={n: local[n].astype(BF16) for n in LATE})

    wide_rows = jnp.concatenate([_pad_lanes(a, WIDE_ROW) for a in [grads[n] for n in WIDE] + [loss_part]]
                                + [jnp.zeros((SUBLANES - len(WIDE) - 1, WIDE_ROW), F32)], axis=0)
    wide_parts, = _exchange([], [wide_rows], name="grad_exchange")
    loss = jnp.sum(wide_parts[:, len(WIDE), 0])
    out = {}
    for n in BIG:
        out[n] = _adamw_big(parts[n], local[n], mom_m[n], mom_v[n], name="adamw_" + n)

    def small_form(n, a):
        if n in LORA:
            return a
        a = a.reshape(1, -1)
        return _pad_lanes(a, WIDE_ROW) if n == "mu_shift" else a

    small = LORA + VEC + WIDE
    res = _adamw_small([parts[n] for n in LORA], parts["vec"], wide_parts,
                       [tuple(small_form(n, d[n]) for d in (local, mom_m, mom_v)) for n in small],
                       name="adamw_small")
    for n, quad in zip(small, res):
        out[n] = tuple(a[..., :local[n].size].reshape(local[n].shape) if n not in LORA else a for a in quad)
    return (loss, grad_x, *[out[n][i] for i in range(4) for n in WEIGHTS])
```

```python
import functools

import jax
import jax.numpy as jnp
from jax import lax
from jax.experimental import pallas as pl
from jax.experimental.pallas import tpu as pltpu

F32 = jnp.float32
BF16 = jnp.bfloat16
HIGHEST = lax.Precision.HIGHEST

N_DEV = 8
D_MODEL = 1024
D_RWKV = 512
D_CONV = 512
HEAD = 64
N_HEAD = D_RWKV // HEAD
D_LORA = 64
D_GATE = 160
D_SHIFTED = 3 * D_RWKV + 2 * D_LORA + D_GATE
XW0, XA0, XG0 = 1536, 1664, 1792
D_SP = 2048
D_INP = D_SP + 3 * D_CONV
LOG_DECAY_SCALE = 0.606531
RMS_EPS = 1e-6
GN_EPS = 64e-5
NORM_EPS = 1e-12
ADAM_LR, ADAM_B1, ADAM_B2, ADAM_EPS, ADAM_WD, ADAM_STEP = 0.001, 0.9, 0.999, 1e-08, 0.01, 10

LANES = 128
SUBLANES = 8
VMEM_LIMIT = 48 * 1024 * 1024
SCAN_CHUNK = 32
SCAN_VMEM_LIMIT = 58 * 1024 * 1024
SCAN_UNROLL = 3
ROW_TILE = 128
WIDE_TILE = 512
RELAYOUT_TILE = 512

BIG = ("w_in", "w_out", "w_gate", "w_up", "w_down")
LORA = ("w_up_f", "w_up_b", "a_up_f", "a_up_b", "g_up", "conv_w")
SHARD_AXIS = {"w_in": 2, "w_out": 1, "w_gate": 2, "w_up": 2, "w_down": 1, "w_up_f": 2, "w_up_b": 2,
              "a_up_f": 2, "a_up_b": 2, "g_up": 2, "conv_w": 2}
VEC = ("w0_f", "w0_b", "a0_f", "a0_b", "k_k", "k_a_f", "k_a_b", "r_k_f", "r_k_b", "gn_w", "gn_b")
WIDE = ("mu_shift", "norm1_w", "norm2_w", "norm_f_w")
WIDE_ROW = 2048
WEIGHTS = ("norm1_w", "w_in", "mu_shift", "w_up_f", "w0_f", "w_up_b", "w0_b", "a_up_f", "a0_f", "a_up_b",
           "a0_b", "g_up", "k_k", "k_a_f", "k_a_b", "r_k_f", "r_k_b", "gn_w", "gn_b", "conv_w", "w_out",
           "norm2_w", "w_gate", "w_up", "w_down", "norm_f_w")


def _params(sem, limit=VMEM_LIMIT):
    return pltpu.CompilerParams(dimension_semantics=sem, vmem_limit_bytes=limit)


def _tile(n, cands):
    for c in cands:
        if n % c == 0:
            return c
    raise ValueError(f"no tile for {n}")


def _mm(a, b, *, ta=False, tb=False, add=None, exchange=None, name):
    (k_dim, m) = a.shape if ta else a.shape[::-1]
    (k2, n) = b.shape[::-1] if tb else b.shape
    assert k_dim == k2, (a.shape, b.shape, ta, tb)
    tm = _tile(m, (1408, 1024, 512, 256, 128))
    tn = _tile(n, (1408, 1024, 896, 512, 256, 128))
    tk = _tile(k_dim, (1408, 1024, 896, 512, 256, 128))
    nk = k_dim // tk
    grid = (m // tm, n // tn, nk)
    dims = (((0 if ta else 1,), (1 if tb else 0,)), ((), ()))
    sliced, whole = exchange or ((), ())
    riders = list(sliced) + list(whole)
    n_x, n_in = len(riders), 2 + (add is not None)

    def kern(*refs):
        a_ref, b_ref = refs[:2]
        add_ref = refs[2] if add is not None else None
        o_ref, acc_ref = refs[n_in + n_x], refs[n_in + 2 * n_x + 1]
        k = pl.program_id(2)
        step = (pl.program_id(0) * grid[1] + pl.program_id(1)) * nk + k

        def copies():
            return _exchange_copies(refs[n_in:n_in + n_x], refs[n_in + n_x + 1:n_in + 2 * n_x + 1], len(sliced),
                                    *refs[n_in + 2 * n_x + 2:])

        if n_x:
            @pl.when(step == 0)
            def _():
                for cp in copies():
                    cp.start()

        @pl.when(k == 0)
        def _():
            acc_ref[...] = jnp.zeros_like(acc_ref)

        acc_ref[...] += lax.dot_general(a_ref[...].astype(BF16), b_ref[...].astype(BF16), dims,
                                        preferred_element_type=F32)

        @pl.when(k == nk - 1)
        def _():
            if add is None:
                o_ref[...] = acc_ref[...]
            else:
                o_ref[...] = acc_ref[...] + add_ref[...]

        if n_x:
            @pl.when(step == grid[0] * grid[1] * nk - 1)
            def _():
                for cp in copies():
                    cp.wait()

    a_spec = (pl.BlockSpec((tk, tm), lambda i, j, k: (k, i)) if ta
              else pl.BlockSpec((tm, tk), lambda i, j, k: (i, k)))
    b_spec = (pl.BlockSpec((tn, tk), lambda i, j, k: (j, k)) if tb
              else pl.BlockSpec((tk, tn), lambda i, j, k: (k, j)))
    o_spec = pl.BlockSpec((tm, tn), lambda i, j, k: (i, j))
    in_specs = [a_spec, b_spec] + ([o_spec] if add is not None else []) + _hbm_specs(n_x)
    args = (a, b) + ((add,) if add is not None else ()) + tuple(riders)
    out = pl.pallas_call(
        kern, out_shape=[jax.ShapeDtypeStruct((m, n), F32)] + _exchange_out_shapes(riders, len(sliced)), grid=grid,
        in_specs=in_specs, out_specs=[o_spec] + _hbm_specs(n_x),
        scratch_shapes=[pltpu.VMEM((tm, tn), F32)] + (_exchange_sems(n_x) if n_x else []),
        compiler_params=_params(("arbitrary",) * 3 if n_x else ("parallel", "parallel", "arbitrary")),
        name=name)(*args)
    return out if n_x else out[0]


def _mm_pair_t(a1, b1, a2, b2, *, name):
    m, k_dim = a1.shape
    n = b1.shape[0]
    tm = _tile(m, (1024, 512, 256, 128))
    tn = _tile(n, (1024, 512, 256, 128))
    tk = _tile(k_dim, (1408, 1024, 896, 512, 256, 128))
    nk = k_dim // tk
    dims = (((1,), (1,)), ((), ()))

    def kern(a1_ref, b1_ref, a2_ref, b2_ref, o_ref, acc_ref):
        k = pl.program_id(2)

        @pl.when(k == 0)
        def _():
            acc_ref[...] = jnp.zeros_like(acc_ref)

        acc_ref[...] += (
            lax.dot_general(a1_ref[...].astype(BF16), b1_ref[...].astype(BF16), dims, preferred_element_type=F32)
            + lax.dot_general(a2_ref[...].astype(BF16), b2_ref[...].astype(BF16), dims, preferred_element_type=F32))

        @pl.when(k == nk - 1)
        def _():
            o_ref[...] = acc_ref[...]

    a_spec = pl.BlockSpec((tm, tk), lambda i, j, k: (i, k))
    b_spec = pl.BlockSpec((tn, tk), lambda i, j, k: (j, k))
    o_spec = pl.BlockSpec((tm, tn), lambda i, j, k: (i, j))
    return pl.pallas_call(
        kern, out_shape=jax.ShapeDtypeStruct((m, n), F32), grid=(m // tm, n // tn, nk),
        in_specs=[a_spec, b_spec, a_spec, b_spec], out_specs=o_spec, scratch_shapes=[pltpu.VMEM((tm, tn), F32)],
        compiler_params=_params(("parallel", "parallel", "arbitrary")), name=name)(a1, b1, a2, b2)


def _swiglu(g, u):
    return jax.nn.silu(g) * u


FFN_TN = 256


def _mm_swiglu(h, w_gate, w_up, *, name):
    m, k_dim = h.shape
    n = w_gate.shape[1]
    tm = _tile(m, (2048, 1024, 512, 256, 128))

    def kern(h_ref, wg_ref, wu_ref, g_ref, u_ref, f_ref):
        hv = h_ref[...].astype(BF16)
        g = jnp.dot(hv, wg_ref[...].astype(BF16), preferred_element_type=F32)
        u = jnp.dot(hv, wu_ref[...].astype(BF16), preferred_element_type=F32)
        g_ref[...] = g
        u_ref[...] = u
        f_ref[...] = _swiglu(g, u).astype(f_ref.dtype)

    w_spec = pl.BlockSpec((k_dim, FFN_TN), lambda i, j: (0, j))
    o_spec = pl.BlockSpec((tm, FFN_TN), lambda i, j: (i, j))
    return pl.pallas_call(
        kern, out_shape=[jax.ShapeDtypeStruct((m, n), F32)] * 2 + [jax.ShapeDtypeStruct((m, n), BF16)],
        grid=(m // tm, n // FFN_TN), in_specs=[pl.BlockSpec((tm, k_dim), lambda i, j: (i, 0)), w_spec, w_spec],
        out_specs=[o_spec] * 3, compiler_params=_params(("parallel", "parallel")), name=name)(h, w_gate, w_up)


def _mm_swiglu_bwd(dx, w_down, g, u, *, name):
    m, k_dim = dx.shape
    n = w_down.shape[0]
    tm = _tile(m, (2048, 1024, 512, 256, 128))

    def kern(dx_ref, w_ref, g_ref, u_ref, dg_ref, du_ref):
        df = lax.dot_general(dx_ref[...].astype(BF16), w_ref[...].astype(BF16), (((1,), (1,)), ((), ())),
                             preferred_element_type=F32)
        _, vjp = jax.vjp(_swiglu, g_ref[...], u_ref[...])
        dg, du = vjp(df)
        dg_ref[...] = dg.astype(dg_ref.dtype)
        du_ref[...] = du.astype(du_ref.dtype)

    o_spec = pl.BlockSpec((tm, FFN_TN), lambda i, j: (i, j))
    return pl.pallas_call(
        kern, out_shape=[jax.ShapeDtypeStruct((m, n), BF16)] * 2, grid=(m // tm, n // FFN_TN),
        in_specs=[pl.BlockSpec((tm, k_dim), lambda i, j: (i, 0)), pl.BlockSpec((FFN_TN, k_dim), lambda i, j: (j, 0)),
                  o_spec, o_spec],
        out_specs=[o_spec] * 2, compiler_params=_params(("parallel", "parallel")), name=name)(dx, w_down, g, u)


def _rowwise(fn, rows, consts, out_rows, out_accs, *, name, tb=ROW_TILE, out_dtype=F32):
    t = (rows[0][0] if isinstance(rows[0], tuple) else rows[0]).shape[0]
    tb = min(tb, t)
    n_r, n_c, n_o, n_a = len(rows), len(consts), len(out_rows), len(out_accs)
    pieces = [w if isinstance(w, (list, tuple)) else [w] for w in out_rows]

    def kern(*refs):
        r_refs = refs[:n_r]
        c_refs = refs[n_r:n_r + n_c]
        o_refs = refs[n_r + n_c:n_r + n_c + n_o]
        a_refs = refs[n_r + n_c + n_o:]
        vals = fn(*[r[...] for r in r_refs], *[c[...] for c in c_refs])
        vals = list(vals) if isinstance(vals, (tuple, list)) else [vals]
        pos = 0
        for o_ref, ws in zip(o_refs, pieces):
            off = 0
            for w in ws:
                o_ref[:, off:off + w] = vals[pos].astype(o_ref.dtype)
                off += w
                pos += 1
        if n_a:
            @pl.when(pl.program_id(0) == 0)
            def _():
                for a_ref in a_refs:
                    a_ref[...] = jnp.zeros_like(a_ref)
            for a_ref, v in zip(a_refs, vals[pos:]):
                a_ref[...] += v

    in_specs, args = [], []
    for r in rows:
        if isinstance(r, tuple):
            arr, blk, w = r
            in_specs.append(pl.BlockSpec((tb, w), functools.partial(lambda i, blk: (i, blk), blk=blk)))
        else:
            arr = r
            in_specs.append(pl.BlockSpec((tb, arr.shape[1]), lambda i: (i, 0)))
        args.append(arr)
    for c in consts:
        in_specs.append(pl.BlockSpec(c.shape, lambda i: (0, 0)))
        args.append(c)
    out_shape = [jax.ShapeDtypeStruct((t, sum(ws)), out_dtype) for ws in pieces]
    out_specs = [pl.BlockSpec((tb, sum(ws)), lambda i: (i, 0)) for ws in pieces]
    for shp in out_accs:
        out_shape.append(jax.ShapeDtypeStruct(shp, F32))
        out_specs.append(pl.BlockSpec(shp, lambda i: (0, 0)))
    res = pl.pallas_call(
        kern, out_shape=out_shape, grid=(t // tb,), in_specs=in_specs, out_specs=out_specs,
        compiler_params=_params(("arbitrary",) if n_a else ("parallel",)), name=name)(*args)
    return res


def _rms(x, w):
    return x * lax.rsqrt(jnp.mean(x * x, axis=-1, keepdims=True) + RMS_EPS) * w


def _seg_sum(x, bd):
    return jnp.concatenate(
        [jnp.dot(x[:, LANES * j:LANES * (j + 1)], bd, precision=HIGHEST, preferred_element_type=F32)
         for j in range(x.shape[1] // LANES)], axis=1)


@jax.custom_vjp
def _seg(x, bd):
    return _seg_sum(x, bd)


_seg.defvjp(lambda x, bd: (_seg_sum(x, bd), bd), lambda bd, ct: (_seg_sum(ct, bd), jnp.zeros_like(bd)))


def _colsum(x):
    return jnp.sum(x, axis=0, keepdims=True)


def _prescan_math(r, k, xw, xa, xg, k_k, w0f, w0b, a0f, a0b, kaf, kab, wupf, wupb, aupf, aupb, gup, bd):
    kkr = k * k_k
    norm = jnp.sqrt(_seg(kkr * kkr, bd))
    kk = kkr / jnp.maximum(norm, NORM_EPS)
    th = jnp.tanh(xw)

    def direction(w0, wup, a0, aup, ka):
        logit = w0 + jnp.dot(th, wup, preferred_element_type=F32)
        w = jnp.exp(-LOG_DECAY_SCALE * jax.nn.sigmoid(logit))
        a = jax.nn.sigmoid(a0 + jnp.dot(xa, aup, preferred_element_type=F32))
        kd = k * (1.0 + (a - 1.0) * ka)
        return w, kd, kk * a

    wf, kdf, bf = direction(w0f, wupf, a0f, aupf, kaf)
    wb, kdb, bb = direction(w0b, wupb, a0b, aupb, kab)
    g = jnp.dot(jax.nn.sigmoid(xg), gup, preferred_element_type=F32)
    return kk, r, wf, wb, bf, bb, kdf, kdb, g


def _postscan_math(y, r, v, kdf, kdb, g, gn_w, gn_b, rkf, rkb, bd):
    mean = _seg(y, bd) * (1.0 / HEAD)
    yc = y - mean
    var = _seg(yc * yc, bd) * (1.0 / HEAD)
    yg = yc * lax.rsqrt(var + GN_EPS) * gn_w + gn_b
    bonus = (_seg(r * kdf * rkf, bd) + _seg(r * kdb * rkb, bd)) * v
    return (yg + bonus) * g


def _halo_specs(width, col_blk, tb, t):
    nb = t // SUBLANES
    step = tb // SUBLANES
    main = pl.BlockSpec((tb, width), lambda i: (i, col_blk))
    prev = pl.BlockSpec((SUBLANES, width), lambda i: (jnp.maximum(i * step - 1, 0), col_blk))
    nxt = pl.BlockSpec((SUBLANES, width), lambda i: (jnp.minimum((i + 1) * step, nb - 1), col_blk))
    return [main, prev, nxt]


def _neighbours(z, prev8, next8, first, last):
    tb = z.shape[0]
    row = lax.broadcasted_iota(jnp.int32, z.shape, 0)
    prow = jnp.where(first, 0.0, prev8[SUBLANES - 1:SUBLANES, :])
    nrow = jnp.where(last, 0.0, next8[0:1, :])
    down = jnp.where(row == 0, prow, pltpu.roll(z, 1, 0))
    up = jnp.where(row == tb - 1, nrow, pltpu.roll(z, tb - 1, 0))
    return down, up


def _shift_conv_fwd(p, mu, conv_w, seq, *, name, tb=ROW_TILE):
    t = p.shape[0]
    per_seq = seq // tb

    def kern(p_ref, pp_ref, pn_ref, mu_ref, cw_ref, pss_ref, oc_ref):
        i = pl.program_id(0)
        first = (i % per_seq) == 0
        last = (i % per_seq) == per_seq - 1
        ps = p_ref[:, :D_SP]
        down, up = _neighbours(ps, pp_ref[:, :D_SP], pn_ref[:, :D_SP], first, last)
        pss_ref[...] = ps + mu_ref[...] * (0.5 * (down + up) - ps)
        gb = p_ref[:, D_SP:D_SP + D_CONV]
        u = p_ref[:, D_SP + D_CONV:D_SP + 2 * D_CONV] * p_ref[:, D_SP + 2 * D_CONV:]
        u_p = pp_ref[:, D_SP + D_CONV:D_SP + 2 * D_CONV] * pp_ref[:, D_SP + 2 * D_CONV:]
        u_n = pn_ref[:, D_SP + D_CONV:D_SP + 2 * D_CONV] * pn_ref[:, D_SP + 2 * D_CONV:]
        udown, uup = _neighbours(u, u_p, u_n, first, last)
        oc_ref[...] = gb * (cw_ref[0:1, :] * udown + cw_ref[1:2, :] * u + cw_ref[2:3, :] * uup)

    return pl.pallas_call(
        kern,
        out_shape=[jax.ShapeDtypeStruct((t, D_SP), F32), jax.ShapeDtypeStruct((t, D_CONV), F32)],
        grid=(t // tb,),
        in_specs=_halo_specs(D_INP, 0, tb, t) + [pl.BlockSpec((1, D_SP), lambda i: (0, 0)),
                                                 pl.BlockSpec((SUBLANES, D_CONV), lambda i: (0, 0))],
        out_specs=[pl.BlockSpec((tb, D_SP), lambda i: (i, 0)), pl.BlockSpec((tb, D_CONV), lambda i: (i, 0))],
        compiler_params=_params(("parallel",)), name=name)(p, p, p, mu, conv_w)


def _shift_conv_bwd(p, d_pss, d_o, mu, conv_w, seq, *, name, tb=ROW_TILE):
    t = p.shape[0]
    per_seq = seq // tb

    def kern(p_ref, pp_ref, pn_ref, d_ref, dp_ref, dn_ref, do_ref, dop_ref, don_ref, mu_ref, cw_ref,
             out_ref, dmu_ref, dcw_ref):
        i = pl.program_id(0)
        first = (i % per_seq) == 0
        last = (i % per_seq) == per_seq - 1

        @pl.when(i == 0)
        def _():
            dmu_ref[...] = jnp.zeros_like(dmu_ref)
            dcw_ref[...] = jnp.zeros_like(dcw_ref)

        mu_v = mu_ref[...]
        ps = p_ref[:, :D_SP]
        down, up = _neighbours(ps, pp_ref[:, :D_SP], pn_ref[:, :D_SP], first, last)
        d = d_ref[...]
        ddown, dup = _neighbours(d, dp_ref[...], dn_ref[...], first, last)
        out_ref[:, :D_SP] = (d - mu_v * d + 0.5 * (mu_v * ddown + mu_v * dup)).astype(out_ref.dtype)
        dmu_ref[...] += _colsum(d * (0.5 * (down + up) - ps))

        def parts(ref):
            return (ref[:, D_SP:D_SP + D_CONV], ref[:, D_SP + D_CONV:D_SP + 2 * D_CONV],
                    ref[:, D_SP + 2 * D_CONV:])

        gb, gc, hh = parts(p_ref)
        gb_p, gc_p, hh_p = parts(pp_ref)
        gb_n, gc_n, hh_n = parts(pn_ref)
        u = gc * hh
        udown, uup = _neighbours(u, gc_p * hh_p, gc_n * hh_n, first, last)
        cw0, cw1, cw2 = cw_ref[0:1, :], cw_ref[1:2, :], cw_ref[2:3, :]
        do = do_ref[...]
        duc = do * gb
        ducdown, ducup = _neighbours(duc, dop_ref[...] * gb_p, don_ref[...] * gb_n, first, last)
        du = cw0 * ducup + cw1 * duc + cw2 * ducdown
        out_ref[:, D_SP:D_SP + D_CONV] = (do * (cw0 * udown + cw1 * u + cw2 * uup)).astype(out_ref.dtype)
        out_ref[:, D_SP + D_CONV:D_SP + 2 * D_CONV] = (du * hh).astype(out_ref.dtype)
        out_ref[:, D_SP + 2 * D_CONV:] = (du * gc).astype(out_ref.dtype)
        dcw_ref[0:1, :] += _colsum(duc * udown)
        dcw_ref[1:2, :] += _colsum(duc * u)
        dcw_ref[2:3, :] += _colsum(duc * uup)

    return pl.pallas_call(
        kern,
        out_shape=[jax.ShapeDtypeStruct((t, D_INP), BF16), jax.ShapeDtypeStruct((1, D_SP), F32),
                   jax.ShapeDtypeStruct((SUBLANES, D_CONV), F32)],
        grid=(t // tb,),
        in_specs=(_halo_specs(D_INP, 0, tb, t) + _halo_specs(D_SP, 0, tb, t) + _halo_specs(D_CONV, 1, tb, t)
                  + [pl.BlockSpec((1, D_SP), lambda i: (0, 0)),
                     pl.BlockSpec((SUBLANES, D_CONV), lambda i: (0, 0))]),
        out_specs=[pl.BlockSpec((tb, D_INP), lambda i: (i, 0)), pl.BlockSpec((1, D_SP), lambda i: (0, 0)),
                   pl.BlockSpec((SUBLANES, D_CONV), lambda i: (0, 0))],
        compiler_params=_params(("arbitrary",)), name=name)(p, p, p, d_pss, d_pss, d_pss, d_o, d_o, d_o, mu, conv_w)


N_CHAIN = 16
N_GROUP = LANES // N_CHAIN
V_HI = HEAD // SUBLANES
G_KK, G_R, G_W, G_B, G_KD = 0, 1, (2, 3), (4, 5), (6, 7)


K_HI = HEAD // SUBLANES


def _tree_sum(terms):
    terms = list(terms)
    while len(terms) > 1:
        terms = [a + b for a, b in zip(terms[::2], terms[1::2])]
    return terms[0]


def _kscan_specs(nc):
    same = lambda c: c
    mirror = lambda c: nc - 1 - c

    def k_spec(fn):
        return pl.BlockSpec((SCAN_CHUNK, HEAD, LANES), lambda c: (fn(c), 0, 0))

    def v_spec(fn):
        return pl.BlockSpec((SCAN_CHUNK, SUBLANES, LANES), lambda c: (fn(c), 0, 0))

    return same, mirror, k_spec, v_spec


ST_SHAPE = (2, K_HI, V_HI, SUBLANES, LANES)


def _lane_group_index():
    lane = lax.broadcasted_iota(jnp.int32, (SUBLANES, LANES), 1)
    return lax.shift_right_logical(lane, jnp.full_like(lane, 4))


def _spread_groups(x, grp):
    rolled = [x] + [pltpu.roll(x, s * N_CHAIN, 1) for s in range(1, N_GROUP)]
    out = []
    for j in range(N_GROUP):
        t = rolled[(0 - j) % N_GROUP]
        for g in range(1, N_GROUP):
            t = jnp.where(grp == g, rolled[(g - j) % N_GROUP], t)
        out.append(t)
    return out


def _gather_groups(tiles, grp):
    total = None
    for s in range(N_GROUP):
        b = tiles[s % N_GROUP]
        for g in range(1, N_GROUP):
            b = jnp.where(grp == g, tiles[(g + s) % N_GROUP], b)
        b = pltpu.roll(b, s * N_CHAIN, 1) if s else b
        total = b if total is None else total + b
    return total


def _lane_group_sum(x):
    return _tree_sum([x] + [pltpu.roll(x, k * N_CHAIN, 1) for k in range(1, N_GROUP)])


def _key_row(x_t, grp, kh):
    r = SUBLANES * grp + kh
    return jnp.broadcast_to(x_t[r:r + 1, :], (SUBLANES, LANES))


def _acc(total, term):
    return term if total is None else total + term


SA_SHAPE = (2, V_HI, SUBLANES, LANES)


def _scan_fwd(xall, v_c, *, gather=(), name):
    steps = xall.shape[0]
    nc = steps // SCAN_CHUNK
    same, mirror, k_spec, v_spec = _kscan_specs(nc)
    last = SCAN_CHUNK - 1
    n_x = len(gather)

    def kern(*refs):
        xf_ref, xb_ref, vf_ref, vb_ref = refs[:4]
        yf_ref, yb_ref, hist_ref, fin_ref, sa_ref = refs[4 + n_x:9 + n_x]
        st_ref = refs[9 + 2 * n_x]
        c = pl.program_id(0)

        def riders():
            return _exchange_copies(refs[4:4 + n_x], refs[9 + n_x:9 + 2 * n_x], 0, *refs[10 + 2 * n_x:])

        @pl.when(c == 0)
        def _():
            st_ref[...] = jnp.zeros_like(st_ref)
            if n_x:
                for cp in riders():
                    cp.start()

        hist_ref[0] = st_ref[...]
        grp = _lane_group_index()

        def body(i, put):
            j = last - i
            for d, (x_t, v_t, y_ref, at) in enumerate(((xf_ref[i], vf_ref[i], yf_ref, i),
                                                       (xb_ref[j], vb_ref[j], yb_ref, j))):
                v_b = _spread_groups(v_t, grp)
                part = [None] * V_HI
                for kh in range(K_HI):
                    kk_r = _key_row(x_t, G_KK, kh)
                    for vh in range(V_HI):
                        part[vh] = _acc(part[vh], hist_ref[i, d, kh, vh] * kk_r)
                sa = [_lane_group_sum(p) for p in part]
                for vh in range(V_HI):
                    sa_ref[i, d, vh] = sa[vh]
                y_p = [None] * V_HI
                for kh in range(K_HI):
                    r_r, w_r = _key_row(x_t, G_R, kh), _key_row(x_t, G_W[d], kh)
                    b_r, kd_r = _key_row(x_t, G_B[d], kh), _key_row(x_t, G_KD[d], kh)
                    for vh in range(V_HI):
                        new = hist_ref[i, d, kh, vh] * w_r - sa[vh] * b_r + v_b[vh] * kd_r
                        put(d, kh, vh, new)
                        y_p[vh] = _acc(y_p[vh], new * r_r)
                y_ref[at] = _gather_groups(y_p, grp)

        def step(i, carry):
            def put(d, kh, vh, val):
                hist_ref[i + 1, d, kh, vh] = val
            body(i, put)
            return carry

        lax.fori_loop(0, last, step, 0, unroll=SCAN_UNROLL)

        def put_carry(d, kh, vh, val):
            st_ref[d, kh, vh] = val

        body(last, put_carry)

        @pl.when(c == nc - 1)
        def _():
            fin_ref[...] = st_ref[...]
            if n_x:
                for cp in riders():
                    cp.wait()

    return pl.pallas_call(
        kern,
        out_shape=[jax.ShapeDtypeStruct((steps, SUBLANES, LANES), F32)] * 2
        + [jax.ShapeDtypeStruct((steps,) + ST_SHAPE, F32), jax.ShapeDtypeStruct(ST_SHAPE, F32),
           jax.ShapeDtypeStruct((steps,) + SA_SHAPE, F32)]
        + _exchange_out_shapes(gather, 0),
        grid=(nc,), in_specs=[k_spec(same), k_spec(mirror), v_spec(same), v_spec(mirror)] + _hbm_specs(n_x),
        out_specs=[v_spec(same), v_spec(mirror),
                   pl.BlockSpec((SCAN_CHUNK,) + ST_SHAPE, lambda c: (c, 0, 0, 0, 0, 0)),
                   pl.BlockSpec(ST_SHAPE, lambda c: (0, 0, 0, 0, 0)),
                   pl.BlockSpec((SCAN_CHUNK,) + SA_SHAPE, lambda c: (c, 0, 0, 0, 0))] + _hbm_specs(n_x),
        scratch_shapes=[pltpu.VMEM(ST_SHAPE, F32)] + (_exchange_sems(n_x) if n_x else []),
        compiler_params=_params(("arbitrary",), SCAN_VMEM_LIMIT), name=name)(xall, xall, v_c, v_c, *gather)


def _scan_bwd(xall, v_c, dy_c, hist, fin, sa, *, exchange=(), name):
    steps = xall.shape[0]
    nc = steps // SCAN_CHUNK
    same, back, k_spec, v_spec = _kscan_specs(nc)
    last = SCAN_CHUNK - 1
    n_x = len(exchange)

    def kern(*refs):
        xf_ref, xb_ref, vf_ref, vb_ref, dyf_ref, dyb_ref, hist_ref, fin_ref, sa_ref = refs[:9]
        gf_ref, gb_ref, dvf_ref, dvb_ref = refs[9 + n_x:13 + n_x]
        ds_ref, after_ref = refs[13 + 2 * n_x:15 + 2 * n_x]
        c = pl.program_id(0)

        def riders():
            return _exchange_copies(refs[9:9 + n_x], refs[13 + n_x:13 + 2 * n_x], n_x, *refs[15 + 2 * n_x:])

        @pl.when(c == 0)
        def _():
            ds_ref[...] = jnp.zeros_like(ds_ref)
            after_ref[...] = fin_ref[...]
            if n_x:
                for cp in riders():
                    cp.start()

        grp = _lane_group_index()
        row = lax.broadcasted_iota(jnp.int32, (SUBLANES, LANES), 0)
        zero = jnp.zeros((SUBLANES, LANES), F32)

        def body(i, after):
            j = last - i
            for d, (x_t, v_t, dy_t, g_ref, dv_ref, at) in enumerate((
                    (xf_ref[i], vf_ref[i], dyf_ref[i], gf_ref, dvf_ref, i),
                    (xb_ref[j], vb_ref[j], dyb_ref[j], gb_ref, dvb_ref, j))):
                v_s, dy_s = _spread_groups(v_t, grp), _spread_groups(dy_t, grp)
                dsa_p, dv_p = [None] * V_HI, [None] * V_HI
                for kh in range(K_HI):
                    r_r = _key_row(x_t, G_R, kh)
                    b_r, kd_r = _key_row(x_t, G_B[d], kh), _key_row(x_t, G_KD[d], kh)
                    for vh in range(V_HI):
                        g = ds_ref[d, kh, vh] + dy_s[vh] * r_r
                        ds_ref[d, kh, vh] = g
                        dsa_p[vh] = _acc(dsa_p[vh], g * b_r)
                        dv_p[vh] = _acc(dv_p[vh], g * kd_r)
                dsa = [-_lane_group_sum(p) for p in dsa_p]
                sa = [sa_ref[i, d, vh] for vh in range(V_HI)]
                dv_ref[at] = _gather_groups(dv_p, grp)
                blocks = {G_KK: zero, G_R: zero, G_W[d]: zero, G_B[d]: zero, G_KD[d]: zero}
                for kh in range(K_HI):
                    w_r, kk_r = _key_row(x_t, G_W[d], kh), _key_row(x_t, G_KK, kh)
                    dkk = dr = dw = db = dkd = None
                    for vh in range(V_HI):
                        g, before = ds_ref[d, kh, vh], hist_ref[i, d, kh, vh]
                        dr = _acc(dr, after(d, kh, vh) * dy_s[vh])
                        dw = _acc(dw, g * before)
                        dkd = _acc(dkd, g * v_s[vh])
                        db = _acc(db, g * sa[vh])
                        dkk = _acc(dkk, before * dsa[vh])
                        ds_ref[d, kh, vh] = g * w_r + dsa[vh] * kk_r
                    for gi, a in ((G_KK, dkk), (G_R, dr), (G_W[d], dw), (G_B[d], -db), (G_KD[d], dkd)):
                        blocks[gi] = jnp.where(row == kh, _colsum(a), blocks[gi])
                for gi in range(N_GROUP):
                    g_ref[at, SUBLANES * gi:SUBLANES * (gi + 1), :] = blocks.get(gi, zero)

        body(last, lambda d, kh, vh: after_ref[d, kh, vh])

        def step(ii, carry):
            i = last - ii
            body(i, lambda d, kh, vh: hist_ref[i + 1, d, kh, vh])
            return carry

        lax.fori_loop(1, SCAN_CHUNK, step, 0, unroll=SCAN_UNROLL)
        after_ref[...] = hist_ref[0]

        if n_x:
            @pl.when(c == nc - 1)
            def _():
                for cp in riders():
                    cp.wait()

    return pl.pallas_call(
        kern,
        out_shape=[jax.ShapeDtypeStruct((steps, HEAD, LANES), F32)] * 2
        + [jax.ShapeDtypeStruct((steps, SUBLANES, LANES), F32)] * 2 + _exchange_out_shapes(exchange, n_x),
        grid=(nc,),
        in_specs=[k_spec(back), k_spec(same), v_spec(back), v_spec(same), v_spec(back), v_spec(same),
                  pl.BlockSpec((SCAN_CHUNK,) + ST_SHAPE, lambda c: (back(c), 0, 0, 0, 0, 0)),
                  pl.BlockSpec(ST_SHAPE, lambda c: (0, 0, 0, 0, 0)),
                  pl.BlockSpec((SCAN_CHUNK,) + SA_SHAPE, lambda c: (back(c), 0, 0, 0, 0))] + _hbm_specs(n_x),
        out_specs=[k_spec(back), k_spec(same), v_spec(back), v_spec(same)] + _hbm_specs(n_x),
        scratch_shapes=[pltpu.VMEM(ST_SHAPE, F32), pltpu.VMEM(ST_SHAPE, F32)]
        + (_exchange_sems(n_x) if n_x else []),
        compiler_params=_params(("arbitrary",), SCAN_VMEM_LIMIT), name=name)(xall, xall, v_c, v_c, dy_c, dy_c, hist, fin, sa,
                                                            *exchange)


def _bf16_pieces(x):
    hi = x.astype(BF16)
    return hi, (x - hi.astype(F32)).astype(BF16)


def _to_key_rows(wide, bsz, seq, *, name):
    assert bsz == 2
    perm = _key_row_maps()
    tt = min(RELAYOUT_TILE, seq)
    per_seq = seq // tt

    def kern(x0_ref, x1_ref, p0_ref, p1_ref, o_ref):
        total = None
        for x_ref, p_ref in ((x0_ref, p0_ref), (x1_ref, p1_ref)):
            for piece in _bf16_pieces(x_ref[...]):
                term = jnp.dot(piece, p_ref[...], preferred_element_type=F32)
                total = term if total is None else total + term
        for r in range(K_HI):
            o_ref[:, r, :] = total[:, LANES * r:LANES * (r + 1)]

    p_spec = pl.BlockSpec((D_RWKV, K_HI * LANES), lambda i, a: (0, 0))
    return pl.pallas_call(
        kern, out_shape=jax.ShapeDtypeStruct((seq, HEAD, LANES), F32), grid=(per_seq, N_GROUP),
        in_specs=[pl.BlockSpec((tt, D_RWKV), lambda i, a: (i, a)),
                  pl.BlockSpec((tt, D_RWKV), lambda i, a: (per_seq + i, a)), p_spec, p_spec],
        out_specs=pl.BlockSpec((tt, K_HI, LANES), lambda i, a: (i, a, 0)),
        compiler_params=_params(("parallel", "parallel")), name=name)(wide, wide, *perm)


def _key_row_maps():
    src = jnp.arange(D_RWKV)
    head, kh, kl = src // HEAD, (src // SUBLANES) % K_HI, src % SUBLANES
    dst = jnp.arange(K_HI * LANES)
    return [((kh[:, None] == dst[None, :] // LANES) & (kl[:, None] == (dst[None, :] // N_CHAIN) % SUBLANES)
             & ((dst[None, :] // N_HEAD) % 2 == b) & (head[:, None] == dst[None, :] % N_HEAD)).astype(BF16)
            for b in range(2)]


def _from_key_rows(g_f, g_b, bsz, seq, *, name):
    assert bsz == 2
    maps = jnp.concatenate([m.T for m in _key_row_maps()], axis=1)
    tt = min(RELAYOUT_TILE, seq)
    per_seq = seq // tt

    def kern(gf_ref, gb_ref, q_ref, o_ref):
        a = pl.program_id(1)
        shared = a <= G_R
        from_f = shared | (a % 2 == G_W[0] % 2)

        def rearranged(g_ref):
            g = jnp.concatenate([g_ref[:, r, :] for r in range(K_HI)], axis=1)
            hi, mid = (jnp.dot(piece, q_ref[...], preferred_element_type=F32) for piece in _bf16_pieces(g))
            both = hi + mid
            return both[:, :D_RWKV], both[:, D_RWKV:]

        @pl.when(from_f)
        def _():
            o_ref[0], o_ref[1] = rearranged(gf_ref)

        @pl.when(jnp.logical_not(from_f))
        def _():
            o_ref[0], o_ref[1] = rearranged(gb_ref)

        @pl.when(shared)
        def _():
            more = rearranged(gb_ref)
            o_ref[0] += more[0]
            o_ref[1] += more[1]

    g_spec = pl.BlockSpec((tt, K_HI, LANES), lambda i, a: (i, a, 0))
    out = pl.pallas_call(
        kern, out_shape=jax.ShapeDtypeStruct((bsz, seq, N_GROUP * D_RWKV), F32), grid=(per_seq, N_GROUP),
        in_specs=[g_spec, g_spec, pl.BlockSpec((K_HI * LANES, bsz * D_RWKV), lambda i, a: (0, 0))],
        out_specs=pl.BlockSpec((bsz, tt, D_RWKV), lambda i, a: (0, i, a)),
        compiler_params=_params(("parallel", "parallel")), name=name)(g_f, g_b, maps)
    return out.reshape(bsz * seq, N_GROUP * D_RWKV)


def _to_value_rows(a, bsz, seq):
    z = a.reshape(bsz, seq, N_HEAD, V_HI, SUBLANES).transpose(1, 4, 3, 0, 2)
    return z.reshape(seq, SUBLANES, LANES)


def _from_value_rows(y, bsz, seq):
    z = y.reshape(seq, SUBLANES, V_HI, bsz, N_HEAD).transpose(3, 0, 4, 2, 1)
    return z.reshape(bsz * seq, D_RWKV)


def _pad_cols(a, segs):
    out, off = [], 0
    for w, wp in segs:
        out.append(a[..., off:off + w])
        if wp > w:
            out.append(jnp.zeros(a.shape[:-1] + (wp - w,), a.dtype))
        off += w
    return jnp.concatenate(out, axis=-1)


def _unpad_cols(a, segs):
    out, off = [], 0
    for w, wp in segs:
        out.append(a[..., off:off + w])
        off += wp
    return jnp.concatenate(out, axis=-1)


P_SEGS = ((3 * D_RWKV, 3 * D_RWKV), (D_LORA, 128), (D_LORA, 128), (D_GATE, 256), (3 * D_CONV, 3 * D_CONV))
S_SEGS = P_SEGS[:4]


def _pad_rows(a, rows):
    return jnp.concatenate([a, jnp.zeros((rows - a.shape[0], a.shape[1]), a.dtype)], axis=0)


LATE = ("w_out", "w_gate", "w_up", "w_down")


def _local_step(x, target, w, late=None):
    bsz, seq, _ = x.shape
    t = bsz * seq
    x2d = x.reshape(t, D_MODEL)
    tg2d = target.reshape(t, D_MODEL)
    row = lambda a: a.reshape(1, -1).astype(F32)

    w_in = _pad_cols(w["w_in"][0], P_SEGS)
    mu = _pad_cols(row(w["mu_shift"]), S_SEGS)
    wupf, wupb, aupf, aupb = (_pad_rows(w[n][0].astype(F32), 128) for n in ("w_up_f", "w_up_b", "a_up_f", "a_up_b"))
    gup = _pad_rows(w["g_up"][0].astype(F32), 256)
    conv_w = _pad_rows(w["conv_w"][0].astype(F32), SUBLANES)
    norm1, norm2, normf = row(w["norm1_w"]), row(w["norm2_w"]), row(w["norm_f_w"])
    vec = {n: row(w[n]) for n in VEC}
    head_of = jnp.arange(LANES) // HEAD
    bd = (head_of[:, None] == head_of[None, :]).astype(F32)
    pre_consts = [vec["k_k"], vec["w0_f"], vec["w0_b"], vec["a0_f"], vec["a0_b"], vec["k_a_f"], vec["k_a_b"],
                  wupf, wupb, aupf, aupb, gup, bd]
    post_consts = [vec["gn_w"], vec["gn_b"], vec["r_k_f"], vec["r_k_b"], bd]

    h1, = _rowwise(_rms, [x2d], [norm1], [D_MODEL], [], name="rms1_fwd", out_dtype=BF16, tb=WIDE_TILE)
    p = _mm(h1, w_in, name="mm_in")
    pss, oconv = _shift_conv_fwd(p, mu, conv_w, seq, name="shift_conv_fwd", tb=min(2 * ROW_TILE, seq))
    pre_rows = [(pss, 0, 512), (pss, 1, 512), (pss, XW0 // 128, 128), (pss, XA0 // 128, 128), (pss, XG0 // 256, 256)]
    sc, g = _rowwise(_prescan_math, pre_rows, pre_consts, [[D_RWKV] * N_GROUP, D_RWKV], [], name="prescan_fwd",
                     tb=2 * ROW_TILE)
    xall = _to_key_rows(sc, bsz, seq, name="to_key_rows")
    v_l = _to_value_rows(pss[:, 2 * D_RWKV:3 * D_RWKV], bsz, seq)
    y_f, y_b, hist, fin, sa, *gathered = _scan_fwd(xall, v_l, gather=[late[n] for n in LATE] if late else (),
                                                   name="scan_fwd")
    w_out, w_gate, w_up, w_down = (
        (_from_slots(a, SHARD_AXIS[n]) if late else w[n])[0] for n, a in zip(LATE, gathered or LATE))
    y = _from_value_rows(y_f + y_b, bsz, seq)
    post_rows = [y, (pss, 0, 512), (pss, 2, 512), (sc, G_KD[0], 512), (sc, G_KD[1], 512), g]

    def post_fwd(y_, r_, v_, kdf_, kdb_, g_, oc_, *consts):
        return _postscan_math(y_, r_, v_, kdf_, kdb_, g_, *consts), oc_

    o, = _rowwise(post_fwd, post_rows + [oconv], post_consts, [[D_RWKV, D_CONV]], [], name="postscan_fwd",
                  out_dtype=BF16, tb=2 * ROW_TILE)
    x1 = _mm(o, w_out, add=x2d, name="mm_out")
    h2, = _rowwise(_rms, [x1], [norm2], [D_MODEL], [], name="rms2_fwd", out_dtype=BF16, tb=WIDE_TILE)
    gg, uu, ff = _mm_swiglu(h2, w_gate, w_up, name="mm_gate_up")
    x2 = _mm(ff, w_down, add=x1, name="mm_down")

    def final(x_, tg_, wn_):
        yo, vjp = jax.vjp(_rms, x_, wn_)
        err = yo - tg_
        dx_, dwn_ = vjp(err * (1.0 / D_MODEL))
        part = jnp.sum(jnp.sum(err * err, axis=1, keepdims=True), axis=0, keepdims=True) * (0.5 / D_MODEL)
        return dx_, part + jnp.zeros((1, LANES), F32), dwn_

    dx2, loss_acc, d_normf = _rowwise(final, [x2, tg2d], [normf], [D_MODEL], [(1, LANES), (1, D_MODEL)],
                                      name="loss_head", tb=WIDE_TILE)
    dgg, duu = _mm_swiglu_bwd(dx2, w_down, gg, uu, name="mm_down_dx")
    g_w_down = _mm(ff, dx2, ta=True, name="mm_down_dw")
    dh2 = _mm_pair_t(dgg, w_gate, duu, w_up, name="mm_gate_up_dx")
    g_w_gate = _mm(h2, dgg, ta=True, name="mm_gate_dw")
    g_w_up = _mm(h2, duu, ta=True, name="mm_up_dw")

    def rms_bwd(x_, dh_, dres_, wn_):
        _, vjp = jax.vjp(_rms, x_, wn_)
        dx_, dwn_ = vjp(dh_)
        return dx_ + dres_, dwn_

    dx1, d_norm2 = _rowwise(rms_bwd, [x1, dh2, dx2], [norm2], [D_MODEL], [(1, D_MODEL)], name="rms2_bwd", tb=WIDE_TILE)
    do = _mm(dx1, w_out, tb=True, name="mm_out_dx")
    g_w_out = _mm(o, dx1, ta=True, name="mm_out_dw")

    def post_bwd(y_, r_, v_, kdf_, kdb_, g_, do_, *consts):
        _, vjp = jax.vjp(lambda *a: _postscan_math(*a, consts[4]), y_, r_, v_, kdf_, kdb_, g_, *consts[:4])
        return vjp(do_)

    (dy, dr_c, dv_c, dkdf_c, dkdb_c, dg, d_gn_w, d_gn_b, d_rkf, d_rkb) = _rowwise(
        post_bwd, post_rows + [(do, 0, 512)], post_consts, [D_RWKV] * 6, [(1, D_RWKV)] * 4, name="postscan_bwd",
        tb=2 * ROW_TILE)
    dy_l = _to_value_rows(dy, bsz, seq)
    late_grads = {"w_out": g_w_out[None], "w_gate": g_w_gate[None], "w_up": g_w_up[None], "w_down": g_w_down[None]}
    g_f, g_b, dv_f, dv_b, *late_parts = _scan_bwd(
        xall, v_l, dy_l, hist, fin, sa, name="scan_bwd",
        exchange=[_to_slots(late_grads[n], SHARD_AXIS[n]).astype(BF16) for n in LATE] if late else ())
    dsc = _from_key_rows(g_f, g_b, bsz, seq, name="from_key_rows")
    dv_s = _from_value_rows(dv_f + dv_b, bsz, seq)

    def pre_bwd(r_, k_, xw_, xa_, xg_, dkk_, dr_s, dwf_, dwb_, dbf_, dbb_, dkdf_s, dkdb_s,
                dr_c_, dv_c_, dv_s_, dkdf_c_, dkdb_c_, dg_, *consts):
        _, vjp = jax.vjp(lambda *a: _prescan_math(*a, consts[-1]), r_, k_, xw_, xa_, xg_, *consts[:-1])
        grads = vjp((dkk_, dr_s + dr_c_, dwf_, dwb_, dbf_, dbb_, dkdf_s + dkdf_c_, dkdb_s + dkdb_c_, dg_))
        dr_, dk_, dxw_, dxa_, dxg_ = grads[:5]
        return (dr_, dk_, dv_c_ + dv_s_, dxw_, dxa_, dxg_) + tuple(grads[5:])

    pre_b_rows = (pre_rows + [(dsc, j, 512) for j in range(N_GROUP)]
                  + [dr_c, dv_c, dv_s, dkdf_c, dkdb_c, dg])
    pre_b = _rowwise(pre_bwd, pre_b_rows, pre_consts, [[512, 512, 512, 128, 128, 256]],
                     [(1, D_RWKV)] * 7 + [(128, D_RWKV)] * 4 + [(256, D_RWKV)], name="prescan_bwd",
                     tb=2 * ROW_TILE)
    d_pss = pre_b[0]
    d_kk_, d_w0f, d_w0b, d_a0f, d_a0b, d_kaf, d_kab, d_wupf, d_wupb, d_aupf, d_aupb, d_gup = pre_b[1:]
    dp, d_mu, d_conv = _shift_conv_bwd(p, d_pss, do, mu, conv_w, seq, name="shift_conv_bwd",
                                       tb=min(2 * ROW_TILE, seq))
    g_w_in = _mm(h1, dp, ta=True, name="mm_in_dw")
    grads = {
        "w_in": _unpad_cols(g_w_in, P_SEGS)[None], "mu_shift": _unpad_cols(d_mu, S_SEGS),
        "w_up_f": d_wupf[None, :D_LORA], "w0_f": d_w0f, "w_up_b": d_wupb[None, :D_LORA], "w0_b": d_w0b,
        "a_up_f": d_aupf[None, :D_LORA], "a0_f": d_a0f, "a_up_b": d_aupb[None, :D_LORA], "a0_b": d_a0b,
        "g_up": d_gup[None, :D_GATE], "k_k": d_kk_, "k_a_f": d_kaf, "k_a_b": d_kab,
        "r_k_f": d_rkf, "r_k_b": d_rkb, "gn_w": d_gn_w, "gn_b": d_gn_b, "conv_w": d_conv[None, :3],
        "w_out": g_w_out[None], "norm2_w": d_norm2, "w_gate": g_w_gate[None], "w_up": g_w_up[None],
        "w_down": g_w_down[None], "norm_f_w": d_normf,
    }
    early = ("w_in",) + LORA
    parts = dict(zip(LATE, late_parts))
    if late:
        vec_rows = jnp.concatenate([grads[n] for n in VEC] + [jnp.zeros((16 - len(VEC), D_RWKV), F32)], axis=0)
        slots = [_to_slots(grads[n], SHARD_AXIS[n]).astype(BF16 if n in BIG else F32) for n in early]
        dh1, *recv = _mm(dp, w_in, tb=True, exchange=(slots, [vec_rows]), name="mm_in_dx")
        parts.update(zip(early + ("vec",), recv))
    else:
        dh1 = _mm(dp, w_in, tb=True, name="mm_in_dx")
    dx, grads["norm1_w"] = _rowwise(rms_bwd, [x2d, dh1, dx1], [norm1], [D_MODEL], [(1, D_MODEL)], name="rms1_bwd",
                                    tb=WIDE_TILE)
    return loss_acc, dx.reshape(bsz, seq, D_MODEL), grads, parts


def _hbm_specs(n):
    return [pl.BlockSpec(memory_space=pl.ANY)] * n


def _all_gather(arrs, *, name):
    n = len(arrs)

    def body(*refs):
        x_refs, out_refs = refs[:n], refs[n:2 * n]
        send_sems, recv_sems, local_sems = refs[2 * n:]
        x, y, c = lax.axis_index("x"), lax.axis_index("y"), lax.axis_index("c")
        me, sibling = (x, y, c), (x, y, 1 - c)
        chips = [(1 - x, y), (x, 1 - y), (1 - x, 1 - y)]

        def slot(a, px, py, pc):
            return out_refs[a].at[4 * px + 2 * py + pc]

        def copy(a, k, block, to, src=None):
            return pltpu.make_async_remote_copy(
                src_ref=slot(a, *block) if src is None else src, dst_ref=slot(a, *block),
                send_sem=send_sems.at[k, a], recv_sem=recv_sems.at[k, a],
                device_id=to, device_id_type=pl.DeviceIdType.MESH)

        mine = [pltpu.make_async_copy(x_refs[a], slot(a, *me), local_sems.at[a]) for a in range(n)]
        for cp in mine:
            cp.start()
        first = []
        for a in range(n):
            first.append(copy(a, 0, me, sibling, src=x_refs[a]))
            first += [copy(a, 1 + j, me, (*chip, c), src=x_refs[a]) for j, chip in enumerate(chips)]
        for cp in first:
            cp.start()
        passed = []
        for j, chip in enumerate(chips):
            for a in range(n):
                copy(a, 1 + j, (*chip, c), me).wait_recv()
                cp = copy(a, 4 + j, (*chip, c), sibling)
                cp.start()
                passed.append(cp)
        for a in range(n):
            copy(a, 0, sibling, me).wait_recv()
            for j, chip in enumerate(chips):
                copy(a, 4 + j, (*chip, 1 - c), me).wait_recv()
        for cp in first + passed:
            cp.wait_send()
        for cp in mine:
            cp.wait()

    return pl.pallas_call(
        body, out_shape=[jax.ShapeDtypeStruct((N_DEV,) + a.shape, a.dtype) for a in arrs],
        in_specs=_hbm_specs(n), out_specs=_hbm_specs(n),
        scratch_shapes=[pltpu.SemaphoreType.DMA((7, n)), pltpu.SemaphoreType.DMA((7, n)),
                        pltpu.SemaphoreType.DMA((n,))],
        name=name)(*arrs)


def _exchange(sliced, whole, *, name):
    arrs = list(sliced) + list(whole)
    n, n_sliced = len(arrs), len(sliced)

    def body(*refs):
        copies = _exchange_copies(refs[:n], refs[n:2 * n], n_sliced, *refs[2 * n:])
        for cp in copies:
            cp.start()
        for cp in copies:
            cp.wait()

    return pl.pallas_call(
        body, out_shape=_exchange_out_shapes(arrs, n_sliced), in_specs=_hbm_specs(n), out_specs=_hbm_specs(n),
        scratch_shapes=_exchange_sems(n), name=name)(*arrs)


def _exchange_out_shapes(arrs, n_sliced):
    return [jax.ShapeDtypeStruct(a.shape if i < n_sliced else (N_DEV,) + a.shape, a.dtype)
            for i, a in enumerate(arrs)]


def _exchange_sems(n):
    return [pltpu.SemaphoreType.DMA((7, n)), pltpu.SemaphoreType.DMA((7, n)), pltpu.SemaphoreType.DMA((n,))]


def _exchange_copies(in_refs, out_refs, n_sliced, send_sems, recv_sems, local_sems):
    n = len(in_refs)
    x, y, c = lax.axis_index("x"), lax.axis_index("y"), lax.axis_index("c")
    me = 4 * x + 2 * y + c

    def src(a, dev):
        return in_refs[a].at[dev] if a < n_sliced else in_refs[a]

    copies = [pltpu.make_async_copy(src(a, me), out_refs[a].at[me], local_sems.at[a]) for a in range(n)]
    for k in range(1, N_DEV):
        px = 1 - x if k & 4 else x
        py = 1 - y if k & 2 else y
        pc = 1 - c if k & 1 else c
        for a in range(n):
            copies.append(pltpu.make_async_remote_copy(
                src_ref=src(a, 4 * px + 2 * py + pc), dst_ref=out_refs[a].at[me],
                send_sem=send_sems.at[k - 1, a], recv_sem=recv_sems.at[k - 1, a],
                device_id=(px, py, pc), device_id_type=pl.DeviceIdType.MESH))
    return copies


def _adam_math(g, w, m, v):
    nm = ADAM_B1 * m + (1.0 - ADAM_B1) * g
    nv = ADAM_B2 * v + (1.0 - ADAM_B2) * (g * g)
    m_hat = nm / (1.0 - ADAM_B1 ** ADAM_STEP)
    v_hat = nv / (1.0 - ADAM_B2 ** ADAM_STEP)
    return -ADAM_LR * (m_hat / (jnp.sqrt(v_hat) + ADAM_EPS) + ADAM_WD * w), nm, nv


def _slot_sum(ref):
    g = ref[0].astype(F32)
    for s in range(1, N_DEV):
        g = g + ref[s].astype(F32)
    return g


def _adamw_big(parts, w, m, v, *, name):
    _, rws, cols = w.shape
    tr = _tile(rws, (256, 176, 128))

    def kern(p_ref, w_ref, m_ref, v_ref, g_ref, d_ref, nm_ref, nv_ref):
        g = _slot_sum(p_ref)
        g_ref[...] = g
        d_ref[...], nm_ref[...], nv_ref[...] = _adam_math(g, w_ref[...], m_ref[...], v_ref[...])

    spec = pl.BlockSpec((1, tr, cols), lambda i: (0, i, 0))
    return pl.pallas_call(
        kern, out_shape=[jax.ShapeDtypeStruct(w.shape, F32)] * 4, grid=(rws // tr,),
        in_specs=[pl.BlockSpec((N_DEV, 1, tr, cols), lambda i: (0, 0, i, 0)), spec, spec, spec],
        out_specs=[spec] * 4, compiler_params=_params(("parallel",)), name=name)(parts, w, m, v)


def _adamw_small(lora_parts, vec_parts, wide_parts, wmv, *, name):
    names = LORA + VEC + WIDE
    n_l, n = len(LORA), len(names)
    flat = [a for trip in wmv for a in trip]

    def kern(*refs):
        l_refs, vec_ref, wide_ref = refs[:n_l], refs[n_l], refs[n_l + 1]
        in_refs = refs[n_l + 2:n_l + 2 + 3 * n]
        out_refs = refs[n_l + 2 + 3 * n:]
        vec_sum, wide_sum = _slot_sum(vec_ref), _slot_sum(wide_ref)
        for i, nm in enumerate(names):
            w_ref, m_ref, v_ref = in_refs[3 * i:3 * i + 3]
            if i < n_l:
                g = _slot_sum(l_refs[i])
            elif nm in VEC:
                g = vec_sum[i - n_l:i - n_l + 1, :]
            else:
                g = wide_sum[WIDE.index(nm):WIDE.index(nm) + 1, :w_ref.shape[-1]]
            o = out_refs[4 * i:4 * i + 4]
            o[0][...] = g
            o[1][...], o[2][...], o[3][...] = _adam_math(g, w_ref[...], m_ref[...], v_ref[...])

    out_shape = [jax.ShapeDtypeStruct(trip[0].shape, F32) for trip in wmv for _ in range(4)]
    outs = pl.pallas_call(kern, out_shape=out_shape, name=name,
                          compiler_params=pltpu.CompilerParams(vmem_limit_bytes=VMEM_LIMIT))(
        *lora_parts, vec_parts, wide_parts, *flat)
    return [tuple(outs[4 * i:4 * i + 4]) for i in range(n)]


def _to_slots(g, axis):
    _, rws, cols = g.shape
    if axis == 1:
        return g.reshape(N_DEV, 1, rws // N_DEV, cols)
    return g.reshape(1, rws, N_DEV, cols // N_DEV).transpose(2, 0, 1, 3)


def _from_slots(got, axis):
    _, _, rws, cols = got.shape
    if axis == 1:
        return got.reshape(1, N_DEV * rws, cols)
    return got.transpose(1, 2, 0, 3).reshape(1, rws, N_DEV * cols)


def _pad_lanes(a, width):
    return jnp.concatenate([a, jnp.zeros(a.shape[:-1] + (width - a.shape[-1],), a.dtype)], axis=-1)


def kernel(x, norm1_w, w_in, mu_shift, w_up_f, w0_f, w_up_b, w0_b, a_up_f, a0_f, a_up_b, a0_b, g_up, k_k, k_a_f, k_a_b, r_k_f, r_k_b, gn_w, gn_b, conv_w, w_out, norm2_w, w_gate, w_up, w_down, norm_f_w, loss_target, m_norm1_w, m_w_in, m_mu_shift, m_w_up_f, m_w0_f, m_w_up_b, m_w0_b, m_a_up_f, m_a0_f, m_a_up_b, m_a0_b, m_g_up, m_k_k, m_k_a_f, m_k_a_b, m_r_k_f, m_r_k_b, m_gn_w, m_gn_b, m_conv_w, m_w_out, m_norm2_w, m_w_gate, m_w_up, m_w_down, m_norm_f_w, v_norm1_w, v_w_in, v_mu_shift, v_w_up_f, v_w0_f, v_w_up_b, v_w0_b, v_a_up_f, v_a0_f, v_a_up_b, v_a0_b, v_g_up, v_k_k, v_k_a_f, v_k_a_b, v_r_k_f, v_r_k_b, v_gn_w, v_gn_b, v_conv_w, v_w_out, v_norm2_w, v_w_gate, v_w_up, v_w_down, v_norm_f_w):
    local = dict(norm1_w=norm1_w, w_in=w_in, mu_shift=mu_shift, w_up_f=w_up_f, w0_f=w0_f, w_up_b=w_up_b,
                 w0_b=w0_b, a_up_f=a_up_f, a0_f=a0_f, a_up_b=a_up_b, a0_b=a0_b, g_up=g_up, k_k=k_k, k_a_f=k_a_f,
                 k_a_b=k_a_b, r_k_f=r_k_f, r_k_b=r_k_b, gn_w=gn_w, gn_b=gn_b, conv_w=conv_w, w_out=w_out,
                 norm2_w=norm2_w, w_gate=w_gate, w_up=w_up, w_down=w_down, norm_f_w=norm_f_w)
    mom_m = dict(norm1_w=m_norm1_w, w_in=m_w_in, mu_shift=m_mu_shift, w_up_f=m_w_up_f, w0_f=m_w0_f,
                 w_up_b=m_w_up_b, w0_b=m_w0_b, a_up_f=m_a_up_f, a0_f=m_a0_f, a_up_b=m_a_up_b, a0_b=m_a0_b,
                 g_up=m_g_up, k_k=m_k_k, k_a_f=m_k_a_f, k_a_b=m_k_a_b, r_k_f=m_r_k_f, r_k_b=m_r_k_b,
                 gn_w=m_gn_w, gn_b=m_gn_b, conv_w=m_conv_w, w_out=m_w_out, norm2_w=m_norm2_w, w_gate=m_w_gate,
                 w_up=m_w_up, w_down=m_w_down, norm_f_w=m_norm_f_w)
    mom_v = dict(norm1_w=v_norm1_w, w_in=v_w_in, mu_shift=v_mu_shift, w_up_f=v_w_up_f, w0_f=v_w0_f,
                 w_up_b=v_w_up_b, w0_b=v_w0_b, a_up_f=v_a_up_f, a0_f=v_a0_f, a_up_b=v_a_up_b, a0_b=v_a0_b,
                 g_up=v_g_up, k_k=v_k_k, k_a_f=v_k_a_f, k_a_b=v_k_a_b, r_k_f=v_r_k_f, r_k_b=v_r_k_b,
                 gn_w=v_gn_w, gn_b=v_gn_b, conv_w=v_conv_w, w_out=v_w_out, norm2_w=v_norm2_w, w_gate=v_w_gate,
                 w_up=v_w_up, w_down=v_w_down, norm_f_w=v_norm_f_w)

    early = ("w_in",) + LORA
    got = _all_gather([local["w_in"].astype(BF16)] + [local[n] for n in LORA], name="gather")
    full = dict(local)
    full.update({n: _from_slots(a, SHARD_AXIS[n]) for n, a in zip(early, got)})

    loss_part, grad_x, grads, parts = _local_step(x, loss_target, full,
                                                  late={n: local[n].astype(BF16) for n in LATE})

    wide_rows = jnp.concatenate([_pad_lanes(a, WIDE_ROW) for a in [grads[n] for n in WIDE] + [loss_part]]
                                + [jnp.zeros((SUBLANES - len(WIDE) - 1, WIDE_ROW), F32)], axis=0)
    wide_parts, = _exchange([], [wide_rows], name="grad_exchange")
    loss = jnp.sum(wide_parts[:, len(WIDE), 0])
    out = {}
    for n in BIG:
        out[n] = _adamw_big(parts[n], local[n], mom_m[n], mom_v[n], name="adamw_" + n)

    def small_form(n, a):
        if n in LORA:
            return a
        a = a.reshape(1, -1)
        return _pad_lanes(a, WIDE_ROW) if n == "mu_shift" else a

    small = LORA + VEC + WIDE
    res = _adamw_small([parts[n] for n in LORA], parts["vec"], wide_parts,
                       [tuple(small_form(n, d[n]) for d in (local, mom_m, mom_v)) for n in small],
                       name="adamw_small")
    for n, quad in zip(small, res):
        out[n] = tuple(a[..., :local[n].size].reshape(local[n].shape) if n not in LORA else a for a in quad)
    return (loss, grad_x, *[out[n][i] for i in range(4) for n in WEIGHTS])
```
